```python
import jax, jax.numpy as jnp
from jax import lax
import numpy as np

D_MODEL = 2048
BATCH = 8
SEQ = 2048
DEPTH = 1

DN_HEADS = 8
DN_HEAD_DIM = 128
DN_WIDTH = DN_HEADS * DN_HEAD_DIM
DN_CONV = 4
DN_CHUNK = 64
CF_WIDTH = 1024
CF_KERNEL = 31
FFN_DIM = 5632
FFN_CONV = 3
EPS = 1e-6

IN_SIZES = [3 * DN_WIDTH,
            DN_WIDTH,
            DN_HEADS,
            DN_HEADS,
            2 * CF_WIDTH,
            D_MODEL,
            D_MODEL]
N_IN = sum(IN_SIZES)
IN_SPLITS = [int(v) for v in np.cumsum(IN_SIZES)[:-1]]

kernel_name = "hybrid_deltanet_conformer_convffn_adaln"


def rmsnorm(x, g):
    xf = x.astype(jnp.float32)
    y = xf * lax.rsqrt(jnp.mean(xf * xf, axis=-1, keepdims=True) + EPS)
    return (y * g.astype(jnp.float32)).astype(x.dtype)


def layernorm(x, g, b):
    xf = x.astype(jnp.float32)
    mu = jnp.mean(xf, axis=-1, keepdims=True)
    xc = xf - mu
    y = xc * lax.rsqrt(jnp.mean(xc * xc, axis=-1, keepdims=True) + EPS)
    return (y * g.astype(jnp.float32) + b.astype(jnp.float32)).astype(x.dtype)


def l2norm(x):
    return x * lax.rsqrt(jnp.sum(x * x, axis=-1, keepdims=True) + EPS)


def causal_dwconv(x, w):
    k = w.shape[0]
    return lax.conv_general_dilated(
        x, w[:, None, :].astype(x.dtype), window_strides=(1,), padding=[(k - 1, 0)],
        dimension_numbers=("NWC", "WIO", "NWC"), feature_group_count=x.shape[-1])


def chunk_gated_delta_rule(q, k, v, g, beta):
    b, s, h, dk = q.shape
    dv = v.shape[-1]
    c = DN_CHUNK
    n = s // c

    def to_chunks(t):
        return jnp.moveaxis(t.reshape((b, n, c, h) + t.shape[3:]), 3, 1)

    q, k, v, g, beta = (to_chunks(t) for t in (q, k, v, g, beta))
    g = jnp.cumsum(g, axis=-1)
    causal = jnp.tril(jnp.ones((c, c), dtype=bool))
    strict = jnp.tril(jnp.ones((c, c), dtype=bool), -1)
    diff = g[..., :, None] - g[..., None, :]
    decay = jnp.where(causal, jnp.exp(jnp.where(causal, diff, 0.0)), 0.0)

    kb = k * beta[..., None]
    m = jnp.einsum("bhnid,bhnjd->bhnij", kb, k) * decay
    m = jnp.where(strict, m, 0.0) + jnp.eye(c, dtype=jnp.float32)
    rhs = jnp.concatenate([v * beta[..., None], kb * jnp.exp(g)[..., None]], axis=-1)
    sol = lax.linalg.triangular_solve(m, rhs, left_side=True, lower=True, unit_diagonal=True)
    u0, w = sol[..., :dv], sol[..., dv:]

    attn = jnp.einsum("bhnid,bhnjd->bhnij", q, k) * decay
    q_dec = q * jnp.exp(g)[..., None]
    g_last = g[..., -1]
    k_dec = k * jnp.exp(g_last[..., None] - g)[..., None]

    def step(state, inp):
        u0_i, w_i, attn_i, qd_i, kd_i, gl_i = inp
        u = u0_i - jnp.einsum("bhck,bhkv->bhcv", w_i, state)
        o = (jnp.einsum("bhck,bhkv->bhcv", qd_i, state)
             + jnp.einsum("bhij,bhjv->bhiv", attn_i, u))
        state = (state * jnp.exp(gl_i)[..., None, None]
                 + jnp.einsum("bhck,bhcv->bhkv", kd_i, u))
        return state, o

    xs = tuple(jnp.moveaxis(t, 2, 0) for t in (u0, w, attn, q_dec, k_dec, g_last))
    s0 = jnp.zeros((b, h, dk, dv), jnp.float32)
    _, o = lax.scan(step, s0, xs)
    o = jnp.moveaxis(o, 0, 2)
    return jnp.moveaxis(o, 1, 3).reshape(b, s, h, dv)


def hybrid_mixer(hn, w_in, dn_conv_w, dn_a_log, dn_dt_bias, dn_norm_g, dn_w_o,
                 cf_conv_w, cf_ln_g, cf_ln_b, cf_w_o, w_out):
    b, s, _ = hn.shape
    proj = hn @ w_in
    qkv, z, beta_logit, a_logit, glu_in, gate_a, gate_b = jnp.split(proj, IN_SPLITS, axis=-1)

    qkv = jax.nn.silu(causal_dwconv(qkv, dn_conv_w)).astype(jnp.float32)
    q, k, v = (t.reshape(b, s, DN_HEADS, DN_HEAD_DIM) for t in jnp.split(qkv, 3, axis=-1))
    q = l2norm(q) * (DN_HEAD_DIM ** -0.5)
    k = l2norm(k)
    beta = jax.nn.sigmoid(beta_logit.astype(jnp.float32))
    g = -jnp.exp(dn_a_log.astype(jnp.float32)) * jax.nn.softplus(
        a_logit.astype(jnp.float32) + dn_dt_bias.astype(jnp.float32))
    o = chunk_gated_delta_rule(q, k, v, g, beta)
    o = o * lax.rsqrt(jnp.mean(o * o, axis=-1, keepdims=True) + EPS) * dn_norm_g.astype(jnp.float32)
    o = o * jax.nn.silu(z.astype(jnp.float32).reshape(b, s, DN_HEADS, DN_HEAD_DIM))
    branch_a = o.reshape(b, s, DN_WIDTH).astype(hn.dtype) @ dn_w_o

    val, gl = jnp.split(glu_in, 2, axis=-1)
    u = val * jax.nn.sigmoid(gl)
    u = causal_dwconv(u, cf_conv_w)
    u = jax.nn.silu(layernorm(u, cf_ln_g, cf_ln_b))
    branch_b = u @ cf_w_o

    merged = jax.nn.sigmoid(gate_a) * branch_a + jax.nn.sigmoid(gate_b) * branch_b
    return merged @ w_out


def conv_glu_ffn(hn, w_up, conv_w, w_down):
    gate, up = jnp.split(hn @ w_up, 2, axis=-1)
    gate = causal_dwconv(gate, conv_w)
    return (jax.nn.silu(gate) * up) @ w_down


def _fwd_setup_inputs(seed: int = 0) -> dict:
    key = jax.random.key(seed)
    ks = jax.random.split(key, 24)
    L, D = DEPTH, D_MODEL
    nrm = lambda k, shape, s: jax.random.normal(k, shape, jnp.float32) * s
    dt = jnp.exp(jax.random.uniform(ks[5], (L, DN_HEADS), jnp.float32,
                                    float(np.log(1e-3)), float(np.log(1e-1))))
    return {
        "x": nrm(ks[0], (BATCH, SEQ, D), 1.0),
        "c": nrm(ks[1], (BATCH, D), 1.0),
        "w_ada": nrm(ks[2], (L, D, 6 * D), 0.5 * D ** -0.5),
        "b_ada": nrm(ks[3], (L, 6 * D), 0.01),
        "norm1_g": 1.0 + nrm(ks[4], (L, D), 0.02),
        "w_in": nrm(ks[6], (L, D, N_IN), D ** -0.5),
        "dn_conv_w": nrm(ks[7], (L, DN_CONV, 3 * DN_WIDTH), DN_CONV ** -0.5),
        "dn_a_log": jnp.log(jax.random.uniform(ks[8], (L, DN_HEADS), jnp.float32, 1.0, 16.0)),
        "dn_dt_bias": jnp.log(jnp.expm1(dt)),
        "dn_norm_g": 1.0 + nrm(ks[9], (L, DN_HEAD_DIM), 0.02),
        "dn_w_o": nrm(ks[10], (L, DN_WIDTH, D), DN_WIDTH ** -0.5),
        "cf_conv_w": nrm(ks[11], (L, CF_KERNEL, CF_WIDTH), CF_KERNEL ** -0.5),
        "cf_ln_g": 1.0 + nrm(ks[12], (L, CF_WIDTH), 0.02),
        "cf_ln_b": nrm(ks[13], (L, CF_WIDTH), 0.02),
        "cf_w_o": nrm(ks[14], (L, CF_WIDTH, D), CF_WIDTH ** -0.5),
        "w_out": nrm(ks[15], (L, D, D), D ** -0.5),
        "norm2_g": 1.0 + nrm(ks[16], (L, D), 0.02),
        "ffn_w_up": nrm(ks[17], (L, D, 2 * FFN_DIM), D ** -0.5),
        "ffn_conv_w": nrm(ks[18], (L, FFN_CONV, FFN_DIM), FFN_CONV ** -0.5),
        "ffn_w_down": nrm(ks[19], (L, FFN_DIM, D), FFN_DIM ** -0.5),
        "final_norm_g": 1.0 + nrm(ks[20], (D,), 0.02),
    }


def _fwd_reference(x, c, w_ada, b_ada, norm1_g, w_in, dn_conv_w, dn_a_log, dn_dt_bias, dn_norm_g,
              dn_w_o, cf_conv_w, cf_ln_g, cf_ln_b, cf_w_o, w_out, norm2_g, ffn_w_up, ffn_conv_w,
              ffn_w_down, final_norm_g):
    c_act = jax.nn.silu(c)
    for l in range(DEPTH):
        mod = c_act @ w_ada[l] + b_ada[l]
        sh1, sc1, gt1, sh2, sc2, gt2 = jnp.split(mod[:, None, :], 6, axis=-1)
        hn = rmsnorm(x, norm1_g[l]) * (1.0 + sc1) + sh1
        x = x + gt1 * hybrid_mixer(hn, w_in[l], dn_conv_w[l], dn_a_log[l], dn_dt_bias[l],
                                   dn_norm_g[l], dn_w_o[l], cf_conv_w[l], cf_ln_g[l],
                                   cf_ln_b[l], cf_w_o[l], w_out[l])
        hn = rmsnorm(x, norm2_g[l]) * (1.0 + sc2) + sh2
        x = x + gt2 * conv_glu_ffn(hn, ffn_w_up[l], ffn_conv_w[l], ffn_w_down[l])
    return rmsnorm(x, final_norm_g)


import jax as _jax
import jax.numpy as _jnp

TWIN_FORMAT = 'train_step'
FWD_PARAMS = ['x', 'c', 'w_ada', 'b_ada', 'norm1_g', 'w_in', 'dn_conv_w', 'dn_a_log', 'dn_dt_bias', 'dn_norm_g', 'dn_w_o', 'cf_conv_w', 'cf_ln_g', 'cf_ln_b', 'cf_w_o', 'w_out', 'norm2_g', 'ffn_w_up', 'ffn_conv_w', 'ffn_w_down', 'final_norm_g']
TWIN_WEIGHTS = ['w_ada', 'b_ada', 'norm1_g', 'w_in', 'dn_conv_w', 'dn_a_log', 'dn_dt_bias', 'dn_norm_g', 'dn_w_o', 'cf_conv_w', 'cf_ln_g', 'cf_ln_b', 'cf_w_o', 'w_out', 'norm2_g', 'ffn_w_up', 'ffn_conv_w', 'ffn_w_down', 'final_norm_g']
TWIN_DIFF_INPUT = 'x'
TWIN_INPUTS = ['x', 'c', 'w_ada', 'b_ada', 'norm1_g', 'w_in', 'dn_conv_w', 'dn_a_log', 'dn_dt_bias', 'dn_norm_g', 'dn_w_o', 'cf_conv_w', 'cf_ln_g', 'cf_ln_b', 'cf_w_o', 'w_out', 'norm2_g', 'ffn_w_up', 'ffn_conv_w', 'ffn_w_down', 'final_norm_g', 'loss_target', 'm_w_ada', 'm_b_ada', 'm_norm1_g', 'm_w_in', 'm_dn_conv_w', 'm_dn_a_log', 'm_dn_dt_bias', 'm_dn_norm_g', 'm_dn_w_o', 'm_cf_conv_w', 'm_cf_ln_g', 'm_cf_ln_b', 'm_cf_w_o', 'm_w_out', 'm_norm2_g', 'm_ffn_w_up', 'm_ffn_conv_w', 'm_ffn_w_down', 'm_final_norm_g', 'v_w_ada', 'v_b_ada', 'v_norm1_g', 'v_w_in', 'v_dn_conv_w', 'v_dn_a_log', 'v_dn_dt_bias', 'v_dn_norm_g', 'v_dn_w_o', 'v_cf_conv_w', 'v_cf_ln_g', 'v_cf_ln_b', 'v_cf_w_o', 'v_w_out', 'v_norm2_g', 'v_ffn_w_up', 'v_ffn_conv_w', 'v_ffn_w_down', 'v_final_norm_g']
TWIN_OUTPUTS = ['loss', 'grad_x', 'grad_w_ada', 'grad_b_ada', 'grad_norm1_g', 'grad_w_in', 'grad_dn_conv_w', 'grad_dn_a_log', 'grad_dn_dt_bias', 'grad_dn_norm_g', 'grad_dn_w_o', 'grad_cf_conv_w', 'grad_cf_ln_g', 'grad_cf_ln_b', 'grad_cf_w_o', 'grad_w_out', 'grad_norm2_g', 'grad_ffn_w_up', 'grad_ffn_conv_w', 'grad_ffn_w_down', 'grad_final_norm_g', 'delta_w_ada', 'delta_b_ada', 'delta_norm1_g', 'delta_w_in', 'delta_dn_conv_w', 'delta_dn_a_log', 'delta_dn_dt_bias', 'delta_dn_norm_g', 'delta_dn_w_o', 'delta_cf_conv_w', 'delta_cf_ln_g', 'delta_cf_ln_b', 'delta_cf_w_o', 'delta_w_out', 'delta_norm2_g', 'delta_ffn_w_up', 'delta_ffn_conv_w', 'delta_ffn_w_down', 'delta_final_norm_g', 'new_m_w_ada', 'new_m_b_ada', 'new_m_norm1_g', 'new_m_w_in', 'new_m_dn_conv_w', 'new_m_dn_a_log', 'new_m_dn_dt_bias', 'new_m_dn_norm_g', 'new_m_dn_w_o', 'new_m_cf_conv_w', 'new_m_cf_ln_g', 'new_m_cf_ln_b', 'new_m_cf_w_o', 'new_m_w_out', 'new_m_norm2_g', 'new_m_ffn_w_up', 'new_m_ffn_conv_w', 'new_m_ffn_w_down', 'new_m_final_norm_g', 'new_v_w_ada', 'new_v_b_ada', 'new_v_norm1_g', 'new_v_w_in', 'new_v_dn_conv_w', 'new_v_dn_a_log', 'new_v_dn_dt_bias', 'new_v_dn_norm_g', 'new_v_dn_w_o', 'new_v_cf_conv_w', 'new_v_cf_ln_g', 'new_v_cf_ln_b', 'new_v_cf_w_o', 'new_v_w_out', 'new_v_norm2_g', 'new_v_ffn_w_up', 'new_v_ffn_conv_w', 'new_v_ffn_w_down', 'new_v_final_norm_g']
TWIN_LEAF_KINDS = {'loss': 'loss', 'grad_x': 'grad_x', 'grad_w_ada': 'grad_w', 'grad_b_ada': 'grad_w', 'grad_norm1_g': 'grad_w', 'grad_w_in': 'grad_w', 'grad_dn_conv_w': 'grad_w', 'grad_dn_a_log': 'grad_w', 'grad_dn_dt_bias': 'grad_w', 'grad_dn_norm_g': 'grad_w', 'grad_dn_w_o': 'grad_w', 'grad_cf_conv_w': 'grad_w', 'grad_cf_ln_g': 'grad_w', 'grad_cf_ln_b': 'grad_w', 'grad_cf_w_o': 'grad_w', 'grad_w_out': 'grad_w', 'grad_norm2_g': 'grad_w', 'grad_ffn_w_up': 'grad_w', 'grad_ffn_conv_w': 'grad_w', 'grad_ffn_w_down': 'grad_w', 'grad_final_norm_g': 'grad_w', 'delta_w_ada': 'delta_w', 'delta_b_ada': 'delta_w', 'delta_norm1_g': 'delta_w', 'delta_w_in': 'delta_w', 'delta_dn_conv_w': 'delta_w', 'delta_dn_a_log': 'delta_w', 'delta_dn_dt_bias': 'delta_w', 'delta_dn_norm_g': 'delta_w', 'delta_dn_w_o': 'delta_w', 'delta_cf_conv_w': 'delta_w', 'delta_cf_ln_g': 'delta_w', 'delta_cf_ln_b': 'delta_w', 'delta_cf_w_o': 'delta_w', 'delta_w_out': 'delta_w', 'delta_norm2_g': 'delta_w', 'delta_ffn_w_up': 'delta_w', 'delta_ffn_conv_w': 'delta_w', 'delta_ffn_w_down': 'delta_w', 'delta_final_norm_g': 'delta_w', 'new_m_w_ada': 'new_m', 'new_m_b_ada': 'new_m', 'new_m_norm1_g': 'new_m', 'new_m_w_in': 'new_m', 'new_m_dn_conv_w': 'new_m', 'new_m_dn_a_log': 'new_m', 'new_m_dn_dt_bias': 'new_m', 'new_m_dn_norm_g': 'new_m', 'new_m_dn_w_o': 'new_m', 'new_m_cf_conv_w': 'new_m', 'new_m_cf_ln_g': 'new_m', 'new_m_cf_ln_b': 'new_m', 'new_m_cf_w_o': 'new_m', 'new_m_w_out': 'new_m', 'new_m_norm2_g': 'new_m', 'new_m_ffn_w_up': 'new_m', 'new_m_ffn_conv_w': 'new_m', 'new_m_ffn_w_down': 'new_m', 'new_m_final_norm_g': 'new_m', 'new_v_w_ada': 'new_v', 'new_v_b_ada': 'new_v', 'new_v_norm1_g': 'new_v', 'new_v_w_in': 'new_v', 'new_v_dn_conv_w': 'new_v', 'new_v_dn_a_log': 'new_v', 'new_v_dn_dt_bias': 'new_v', 'new_v_dn_norm_g': 'new_v', 'new_v_dn_w_o': 'new_v', 'new_v_cf_conv_w': 'new_v', 'new_v_cf_ln_g': 'new_v', 'new_v_cf_ln_b': 'new_v', 'new_v_cf_w_o': 'new_v', 'new_v_w_out': 'new_v', 'new_v_norm2_g': 'new_v', 'new_v_ffn_w_up': 'new_v', 'new_v_ffn_conv_w': 'new_v', 'new_v_ffn_w_down': 'new_v', 'new_v_final_norm_g': 'new_v'}


def _forward(args):
    return _fwd_reference(*[args[k] for k in FWD_PARAMS])


def _output_shape():
    out = _jax.eval_shape(lambda: _forward(_fwd_setup_inputs(0)))
    return out.shape, out.dtype

N_MICROBATCH = 1
ADAM_LR = 0.001
ADAM_B1 = 0.9
ADAM_B2 = 0.999
ADAM_EPS = 1e-08
ADAM_WD = 0.01
ADAM_STEP = 10
PER_EXAMPLE_BATCH_AXIS = {'x': 0, 'c': 0, 'loss_target': 0}
SHARED_INPUTS = []
_WEIGHT_DTYPES = {'w_ada': _jnp.float32, 'b_ada': _jnp.float32, 'norm1_g': _jnp.float32, 'w_in': _jnp.float32, 'dn_conv_w': _jnp.float32, 'dn_a_log': _jnp.float32, 'dn_dt_bias': _jnp.float32, 'dn_norm_g': _jnp.float32, 'dn_w_o': _jnp.float32, 'cf_conv_w': _jnp.float32, 'cf_ln_g': _jnp.float32, 'cf_ln_b': _jnp.float32, 'cf_w_o': _jnp.float32, 'w_out': _jnp.float32, 'norm2_g': _jnp.float32, 'ffn_w_up': _jnp.float32, 'ffn_conv_w': _jnp.float32, 'ffn_w_down': _jnp.float32, 'final_norm_g': _jnp.float32}
MOMENT_SCALE = {'w_ada': 1.580266e-02, 'b_ada': 2.606809e-02, 'norm1_g': 1.345361e-02, 'w_in': 6.170669e-03, 'dn_conv_w': 7.282626e-03, 'dn_a_log': 3.998296e-02, 'dn_dt_bias': 3.820562e-02, 'dn_norm_g': 2.864298e-02, 'dn_w_o': 6.792541e-03, 'cf_conv_w': 9.197162e-03, 'cf_ln_g': 1.087911e-02, 'cf_ln_b': 9.675557e-03, 'cf_w_o': 6.349247e-03, 'w_out': 9.253225e-03, 'norm2_g': 1.893984e-02, 'ffn_w_up': 8.174718e-03, 'ffn_conv_w': 8.427182e-03, 'ffn_w_down': 1.331898e-02, 'final_norm_g': 8.009666e+00}


def _to_microbatches(a, axis):
    t = _jnp.moveaxis(a, axis, 0)
    t = t.reshape((N_MICROBATCH, t.shape[0] // N_MICROBATCH) + t.shape[1:])
    return _jnp.moveaxis(t, 1, axis + 1)


def setup_inputs(seed: int = 0) -> dict:
    inp = _fwd_setup_inputs(seed)
    key = _jax.random.fold_in(_jax.random.key(seed), 7919)
    shape, _ = _output_shape()
    out = dict(inp)
    out["loss_target"] = _jax.random.normal(_jax.random.fold_in(key, 0), shape, _jnp.float32)
    for i, name in enumerate(TWIN_WEIGHTS):
        w = inp[name].astype(_jnp.float32)
        if MOMENT_SCALE is None:
            s = _jnp.sqrt(_jnp.mean(_jnp.square(w)) + 1e-30)
        else:
            s = MOMENT_SCALE[name]
        km, kv = _jax.random.split(_jax.random.fold_in(key, i + 1))
        out[name] = w
        out["m_" + name] = s * _jax.random.normal(km, w.shape, _jnp.float32)
        out["v_" + name] = (s * s) * _jax.random.uniform(kv, w.shape, _jnp.float32, 0.5, 1.5)
    if N_MICROBATCH > 1:
        for name, axis in PER_EXAMPLE_BATCH_AXIS.items():
            out[name] = _to_microbatches(out[name], axis)
    return {'x': out['x'], 'c': out['c'], 'w_ada': out['w_ada'], 'b_ada': out['b_ada'], 'norm1_g': out['norm1_g'], 'w_in': out['w_in'], 'dn_conv_w': out['dn_conv_w'], 'dn_a_log': out['dn_a_log'], 'dn_dt_bias': out['dn_dt_bias'], 'dn_norm_g': out['dn_norm_g'], 'dn_w_o': out['dn_w_o'], 'cf_conv_w': out['cf_conv_w'], 'cf_ln_g': out['cf_ln_g'], 'cf_ln_b': out['cf_ln_b'], 'cf_w_o': out['cf_w_o'], 'w_out': out['w_out'], 'norm2_g': out['norm2_g'], 'ffn_w_up': out['ffn_w_up'], 'ffn_conv_w': out['ffn_conv_w'], 'ffn_w_down': out['ffn_w_down'], 'final_norm_g': out['final_norm_g'], 'loss_target': out['loss_target'], 'm_w_ada': out['m_w_ada'], 'm_b_ada': out['m_b_ada'], 'm_norm1_g': out['m_norm1_g'], 'm_w_in': out['m_w_in'], 'm_dn_conv_w': out['m_dn_conv_w'], 'm_dn_a_log': out['m_dn_a_log'], 'm_dn_dt_bias': out['m_dn_dt_bias'], 'm_dn_norm_g': out['m_dn_norm_g'], 'm_dn_w_o': out['m_dn_w_o'], 'm_cf_conv_w': out['m_cf_conv_w'], 'm_cf_ln_g': out['m_cf_ln_g'], 'm_cf_ln_b': out['m_cf_ln_b'], 'm_cf_w_o': out['m_cf_w_o'], 'm_w_out': out['m_w_out'], 'm_norm2_g': out['m_norm2_g'], 'm_ffn_w_up': out['m_ffn_w_up'], 'm_ffn_conv_w': out['m_ffn_conv_w'], 'm_ffn_w_down': out['m_ffn_w_down'], 'm_final_norm_g': out['m_final_norm_g'], 'v_w_ada': out['v_w_ada'], 'v_b_ada': out['v_b_ada'], 'v_norm1_g': out['v_norm1_g'], 'v_w_in': out['v_w_in'], 'v_dn_conv_w': out['v_dn_conv_w'], 'v_dn_a_log': out['v_dn_a_log'], 'v_dn_dt_bias': out['v_dn_dt_bias'], 'v_dn_norm_g': out['v_dn_norm_g'], 'v_dn_w_o': out['v_dn_w_o'], 'v_cf_conv_w': out['v_cf_conv_w'], 'v_cf_ln_g': out['v_cf_ln_g'], 'v_cf_ln_b': out['v_cf_ln_b'], 'v_cf_w_o': out['v_cf_w_o'], 'v_w_out': out['v_w_out'], 'v_norm2_g': out['v_norm2_g'], 'v_ffn_w_up': out['v_ffn_w_up'], 'v_ffn_conv_w': out['v_ffn_conv_w'], 'v_ffn_w_down': out['v_ffn_w_down'], 'v_final_norm_g': out['v_final_norm_g']}


def _loss(weights, diff, rest, loss_target):
    with _jax.named_scope("forward"):
        args = {**rest, TWIN_DIFF_INPUT: diff, **{k: w.astype(_WEIGHT_DTYPES[k]) for k, w in weights.items()}}
        y = _forward(args)
    with _jax.named_scope("loss_head"):
        err = _jnp.square(y.astype(_jnp.float32) - loss_target)
        return 0.5 * _jnp.sum(_jnp.mean(err, axis=-1)) if err.ndim else 0.5 * err


def _adamw(w, g, m, v):
    m = ADAM_B1 * m + (1.0 - ADAM_B1) * g
    v = ADAM_B2 * v + (1.0 - ADAM_B2) * _jnp.square(g)
    m_hat = m / (1.0 - ADAM_B1 ** ADAM_STEP)
    v_hat = v / (1.0 - ADAM_B2 ** ADAM_STEP)
    delta = -ADAM_LR * (m_hat / (_jnp.sqrt(v_hat) + ADAM_EPS) + ADAM_WD * w)
    return delta, m, v


def reference(x, c, w_ada, b_ada, norm1_g, w_in, dn_conv_w, dn_a_log, dn_dt_bias, dn_norm_g, dn_w_o, cf_conv_w, cf_ln_g, cf_ln_b, cf_w_o, w_out, norm2_g, ffn_w_up, ffn_conv_w, ffn_w_down, final_norm_g, loss_target, m_w_ada, m_b_ada, m_norm1_g, m_w_in, m_dn_conv_w, m_dn_a_log, m_dn_dt_bias, m_dn_norm_g, m_dn_w_o, m_cf_conv_w, m_cf_ln_g, m_cf_ln_b, m_cf_w_o, m_w_out, m_norm2_g, m_ffn_w_up, m_ffn_conv_w, m_ffn_w_down, m_final_norm_g, v_w_ada, v_b_ada, v_norm1_g, v_w_in, v_dn_conv_w, v_dn_a_log, v_dn_dt_bias, v_dn_norm_g, v_dn_w_o, v_cf_conv_w, v_cf_ln_g, v_cf_ln_b, v_cf_w_o, v_w_out, v_norm2_g, v_ffn_w_up, v_ffn_conv_w, v_ffn_w_down, v_final_norm_g):
    given = dict(x=x, c=c, w_ada=w_ada, b_ada=b_ada, norm1_g=norm1_g, w_in=w_in, dn_conv_w=dn_conv_w, dn_a_log=dn_a_log, dn_dt_bias=dn_dt_bias, dn_norm_g=dn_norm_g, dn_w_o=dn_w_o, cf_conv_w=cf_conv_w, cf_ln_g=cf_ln_g, cf_ln_b=cf_ln_b, cf_w_o=cf_w_o, w_out=w_out, norm2_g=norm2_g, ffn_w_up=ffn_w_up, ffn_conv_w=ffn_conv_w, ffn_w_down=ffn_w_down, final_norm_g=final_norm_g, loss_target=loss_target, m_w_ada=m_w_ada, m_b_ada=m_b_ada, m_norm1_g=m_norm1_g, m_w_in=m_w_in, m_dn_conv_w=m_dn_conv_w, m_dn_a_log=m_dn_a_log, m_dn_dt_bias=m_dn_dt_bias, m_dn_norm_g=m_dn_norm_g, m_dn_w_o=m_dn_w_o, m_cf_conv_w=m_cf_conv_w, m_cf_ln_g=m_cf_ln_g, m_cf_ln_b=m_cf_ln_b, m_cf_w_o=m_cf_w_o, m_w_out=m_w_out, m_norm2_g=m_norm2_g, m_ffn_w_up=m_ffn_w_up, m_ffn_conv_w=m_ffn_conv_w, m_ffn_w_down=m_ffn_w_down, m_final_norm_g=m_final_norm_g, v_w_ada=v_w_ada, v_b_ada=v_b_ada, v_norm1_g=v_norm1_g, v_w_in=v_w_in, v_dn_conv_w=v_dn_conv_w, v_dn_a_log=v_dn_a_log, v_dn_dt_bias=v_dn_dt_bias, v_dn_norm_g=v_dn_norm_g, v_dn_w_o=v_dn_w_o, v_cf_conv_w=v_cf_conv_w, v_cf_ln_g=v_cf_ln_g, v_cf_ln_b=v_cf_ln_b, v_cf_w_o=v_cf_w_o, v_w_out=v_w_out, v_norm2_g=v_norm2_g, v_ffn_w_up=v_ffn_w_up, v_ffn_conv_w=v_ffn_conv_w, v_ffn_w_down=v_ffn_w_down, v_final_norm_g=v_final_norm_g)
    weights = {n: given[n] for n in TWIN_WEIGHTS}
    shared = {n: given[n] for n in SHARED_INPUTS}
    per_example = {n: given[n] for n in ['x', 'c']}
    grad_fn = _jax.value_and_grad(_loss, argnums=(0, 1))

    def one_microbatch(ex, loss_target):
        ex = dict(ex)
        diff = ex.pop(TWIN_DIFF_INPUT)
        return grad_fn(weights, diff, {**shared, **ex}, loss_target)

    if N_MICROBATCH == 1:
        loss, (grad_w, grad_x) = one_microbatch(per_example, given["loss_target"])
    else:
        def body(carry, xs):
            loss_sum, grad_sum = carry
            l_k, (gw_k, gx_k) = one_microbatch(xs[0], xs[1])
            with _jax.named_scope("update"):
                return (loss_sum + l_k, _jax.tree.map(_jnp.add, grad_sum, gw_k)), gx_k

        init = (_jnp.zeros((), _jnp.float32), _jax.tree.map(_jnp.zeros_like, weights))
        (loss, grad_w), grad_x = _jax.lax.scan(body, init, (per_example, given["loss_target"]))
    with _jax.named_scope("update"):
        delta_w, new_m, new_v = {}, {}, {}
        for n in TWIN_WEIGHTS:
            delta_w[n], new_m[n], new_v[n] = _adamw(weights[n], grad_w[n], given["m_" + n], given["v_" + n])
    return (loss, grad_x, *[grad_w[n] for n in TWIN_WEIGHTS], *[delta_w[n] for n in TWIN_WEIGHTS],
            *[new_m[n] for n in TWIN_WEIGHTS], *[new_v[n] for n in TWIN_WEIGHTS])
```

```python
import functools

import jax
import jax.numpy as jnp
from jax import lax
from jax.experimental import pallas as pl
from jax.experimental.pallas import tpu as pltpu

F32 = jnp.float32
BF16 = jnp.bfloat16
HI = lax.Precision.HIGHEST
MESH = pl.DeviceIdType.MESH
ANY = pl.BlockSpec(memory_space=pl.ANY)

NDEV = 8
D = 2048
S = 2048
H = 8
DH = 128
DNW = H * DH
CFW = 1024
CFK = 31
DNK = 4
FFN = 5632
FFK = 3
CH = 64
NCH = S // CH
EPS = 1e-6
NIN = 10256
NINP = 10368
O_Z, O_GLU, O_GA, O_GB, O_SM = 3072, 4096, 6144, 8192, 10240
LANE = 128
TS = 256
VMEM_LIMIT = 56 * 1024 * 1024

ADAM_LR, ADAM_B1, ADAM_B2, ADAM_EPS, ADAM_WD, ADAM_STEP = 0.001, 0.9, 0.999, 1e-08, 0.01, 10


def _call(body, name, out_shape, grid=(), in_specs=None, out_specs=None, scratch=(), sem=None):
    kw = {}
    if in_specs is not None:
        kw["in_specs"] = in_specs
    if out_specs is not None:
        kw["out_specs"] = out_specs
    return pl.pallas_call(
        body, out_shape=out_shape, grid=grid, scratch_shapes=scratch, name=name,
        compiler_params=pltpu.CompilerParams(dimension_semantics=sem, vmem_limit_bytes=VMEM_LIMIT), **kw)


def _sds(shape, dtype=F32):
    return jax.ShapeDtypeStruct(shape, dtype)


def _tile(dim, pref):
    if dim <= pref:
        return dim
    best = None
    for t in range(LANE, pref + 1, LANE):
        if dim % t == 0:
            best = t
    assert best is not None, (dim, pref)
    return best


def _sigmoid(x):
    return 1.0 / (1.0 + jnp.exp(-x))


def _silu(x):
    return x * _sigmoid(x)


def _dsilu(x):
    s = _sigmoid(x)
    return s * (1.0 + x * (1.0 - s))


def _softplus(x):
    return jnp.maximum(x, 0.0) + jnp.log(1.0 + jnp.exp(-jnp.abs(x)))


def _dot(a, b, dims, precision=None):
    return lax.dot_general(a, b, (dims, ((), ())), preferred_element_type=F32, precision=precision)


NN = ((1,), (0,))
NT = ((1,), (1,))
TN = ((0,), (0,))


def _my_pos():
    return lax.axis_index("x"), lax.axis_index("y"), lax.axis_index("c")


def _mm(a, b, mode, out_dtype, name, tm=1024, tn=1024, tk=2048):
    if mode == "nn":
        (m, k), (k2, n) = a.shape, b.shape
    elif mode == "nt":
        (m, k), (n, k2) = a.shape, b.shape
    else:
        (k, m), (k2, n) = a.shape, b.shape
    assert k == k2, (a.shape, b.shape, mode)
    tm, tn, tk = _tile(m, tm), _tile(n, tn), _tile(k, tk)
    nk = k // tk
    dims = {"nn": NN, "nt": NT, "tn": TN}[mode]

    def body(a_ref, b_ref, o_ref, acc_ref):
        kk = pl.program_id(2)
        part = _dot(a_ref[...], b_ref[...], dims)

        @pl.when(kk == 0)
        def _():
            acc_ref[...] = part

        @pl.when(kk > 0)
        def _():
            acc_ref[...] += part

        @pl.when(kk == nk - 1)
        def _():
            o_ref[...] = acc_ref[...].astype(o_ref.dtype)

    if mode == "tn":
        a_spec = pl.BlockSpec((tk, tm), lambda i, j, kk: (kk, i))
    else:
        a_spec = pl.BlockSpec((tm, tk), lambda i, j, kk: (i, kk))
    if mode == "nt":
        b_spec = pl.BlockSpec((tn, tk), lambda i, j, kk: (j, kk))
    else:
        b_spec = pl.BlockSpec((tk, tn), lambda i, j, kk: (kk, j))
    return _call(body, name, _sds((m, n), out_dtype), grid=(m // tm, n // tn, nk),
                 in_specs=[a_spec, b_spec], out_specs=pl.BlockSpec((tm, tn), lambda i, j, kk: (i, j)),
                 scratch=[pltpu.VMEM((tm, tn), F32)], sem=("parallel", "parallel", "arbitrary"))(a, b)


def _ada_fwd(c_all, w_sh, b_sh):
    n = w_sh.shape[1]
    tn = 512

    def body(c_ref, w_ref, b_ref, o_ref):
        ca = _silu(c_ref[...]).astype(BF16)
        o_ref[...] = _dot(ca, w_ref[...].astype(BF16), NN) + b_ref[...]

    return _call(body, "ada_fwd", _sds((NDEV, n)), grid=(n // tn,),
                 in_specs=[pl.BlockSpec((NDEV, D), lambda j: (0, 0)), pl.BlockSpec((D, tn), lambda j: (0, j)),
                           pl.BlockSpec((1, tn), lambda j: (0, j))],
                 out_specs=pl.BlockSpec((NDEV, tn), lambda j: (0, j)), sem=("parallel",))(c_all, w_sh, b_sh)


def _adam(w, g, m, v):
    m = ADAM_B1 * m + (1.0 - ADAM_B1) * g
    v = ADAM_B2 * v + (1.0 - ADAM_B2) * (g * g)
    m_hat = m / (1.0 - ADAM_B1 ** ADAM_STEP)
    v_hat = v / (1.0 - ADAM_B2 ** ADAM_STEP)
    delta = -ADAM_LR * (m_hat / (jnp.sqrt(v_hat) + ADAM_EPS) + ADAM_WD * w)
    return delta, m, v


def _ada_bwd_adam(c_all, dmod_sel, w, m, v):
    r, n = w.shape
    tr = 256

    def body(c_ref, d_ref, w_ref, m_ref, v_ref, g_ref, dl_ref, nm_ref, nv_ref):
        ca = _silu(c_ref[...])
        g = _dot(ca, d_ref[...], TN, precision=HI)
        dl, nm, nv = _adam(w_ref[...], g, m_ref[...], v_ref[...])
        g_ref[...] = g
        dl_ref[...] = dl
        nm_ref[...] = nm
        nv_ref[...] = nv

    big = pl.BlockSpec((tr, n), lambda i: (i, 0))
    return _call(body, "ada_bwd_adam", [_sds((r, n))] * 4, grid=(r // tr,),
                 in_specs=[pl.BlockSpec((NDEV, tr), lambda i: (0, i)), pl.BlockSpec((NDEV, n), lambda i: (0, 0)),
                           big, big, big],
                 out_specs=[big] * 4, sem=("parallel",))(c_all, dmod_sel, w, m, v)


def _row_spec(width=D):
    return pl.BlockSpec((TS, width), lambda i: (i, 0))


def _vec_spec(width=D):
    return pl.BlockSpec((1, width), lambda i: (0, 0))


def _acc_spec(width=D):
    return pl.BlockSpec((8, width), lambda i: (0, 0))


def _norm_mod(x, g, sc, sh, name):
    def body(x_ref, g_ref, sc_ref, sh_ref, o_ref):
        xv = x_ref[...]
        r = lax.rsqrt(jnp.mean(xv * xv, axis=-1, keepdims=True) + EPS)
        o_ref[...] = ((xv * r) * g_ref[...] * (1.0 + sc_ref[...]) + sh_ref[...]).astype(BF16)

    return _call(body, name, _sds((S, D), BF16), grid=(S // TS,),
                 in_specs=[_row_spec(), _vec_spec(), _vec_spec(), _vec_spec()], out_specs=_row_spec(),
                 sem=("parallel",))(x, g, sc, sh)


def _resid_norm_mod(x, mix, gt, g, sc, sh, name):
    def body(x_ref, mix_ref, gt_ref, g_ref, sc_ref, sh_ref, x2_ref, o_ref):
        xv = x_ref[...] + gt_ref[...] * mix_ref[...]
        x2_ref[...] = xv
        r = lax.rsqrt(jnp.mean(xv * xv, axis=-1, keepdims=True) + EPS)
        o_ref[...] = ((xv * r) * g_ref[...] * (1.0 + sc_ref[...]) + sh_ref[...]).astype(BF16)

    return _call(body, name, [_sds((S, D)), _sds((S, D), BF16)], grid=(S // TS,),
                 in_specs=[_row_spec(), _row_spec()] + [_vec_spec()] * 4, out_specs=[_row_spec(), _row_spec()],
                 sem=("parallel",))(x, mix, gt, g, sc, sh)


def _acc_rows(acc_ref, rows):
    @pl.when(pl.program_id(0) == 0)
    def _():
        acc_ref[...] = jnp.zeros_like(acc_ref)

    for k, row in enumerate(rows):
        acc_ref[k:k + 1, :] += row


def _loss_head(x2, f, tgt, gt2, gf):
    def body(x2_ref, f_ref, t_ref, gt_ref, gf_ref, dx_ref, df_ref, acc_ref):
        fv = f_ref[...]
        x3 = x2_ref[...] + gt_ref[...] * fv
        r = lax.rsqrt(jnp.mean(x3 * x3, axis=-1, keepdims=True) + EPS)
        xn = x3 * r
        e = xn * gf_ref[...] - t_ref[...]
        loss = 0.5 * jnp.sum(jnp.mean(e * e, axis=-1, keepdims=True), axis=0, keepdims=True)
        dy = e * (1.0 / D)
        dxn = dy * gf_ref[...]
        dx3 = r * (dxn - xn * jnp.mean(dxn * xn, axis=-1, keepdims=True))
        dx_ref[...] = dx3
        df_ref[...] = (dx3 * gt_ref[...]).astype(BF16)
        _acc_rows(acc_ref, [jnp.sum(dy * xn, axis=0, keepdims=True), jnp.sum(dx3 * fv, axis=0, keepdims=True),
                            jnp.broadcast_to(loss, (1, D))])

    return _call(body, "loss_head", [_sds((S, D)), _sds((S, D), BF16), _sds((8, D))], grid=(S // TS,),
                 in_specs=[_row_spec(), _row_spec(), _row_spec(), _vec_spec(), _vec_spec()],
                 out_specs=[_row_spec(), _row_spec(), _acc_spec()], sem=("arbitrary",))(x2, f, tgt, gt2, gf)


def _norm_mod_bwd(dhn, x, dres, g, sc, name, mix=None, gt=None):
    gated = mix is not None

    def body(*refs):
        if gated:
            dhn_ref, x_ref, dres_ref, g_ref, sc_ref, mix_ref, gt_ref, dx_ref, dmix_ref, acc_ref = refs
        else:
            dhn_ref, x_ref, dres_ref, g_ref, sc_ref, dx_ref, acc_ref = refs
        xv = x_ref[...]
        dh = dhn_ref[...]
        r = lax.rsqrt(jnp.mean(xv * xv, axis=-1, keepdims=True) + EPS)
        xn = xv * r
        gv = g_ref[...]
        sc1 = 1.0 + sc_ref[...]
        dxn = dh * gv * sc1
        dx = dres_ref[...] + r * (dxn - xn * jnp.mean(dxn * xn, axis=-1, keepdims=True))
        dx_ref[...] = dx
        rows = [jnp.sum(dh, axis=0, keepdims=True), jnp.sum(dh * xn * gv, axis=0, keepdims=True),
                jnp.sum(dh * xn * sc1, axis=0, keepdims=True)]
        if gated:
            rows.append(jnp.sum(dx * mix_ref[...], axis=0, keepdims=True))
            dmix_ref[...] = (dx * gt_ref[...]).astype(BF16)
        _acc_rows(acc_ref, rows)

    ins = [dhn, x, dres, g, sc]
    in_specs = [_row_spec(), _row_spec(), _row_spec(), _vec_spec(), _vec_spec()]
    outs = [_sds((S, D))]
    out_specs = [_row_spec()]
    if gated:
        ins += [mix, gt]
        in_specs += [_row_spec(), _vec_spec()]
        outs.append(_sds((S, D), BF16))
        out_specs.append(_row_spec())
    outs.append(_sds((8, D)))
    out_specs.append(_acc_spec())
    return _call(body, name, outs, grid=(S // TS,), in_specs=in_specs, out_specs=out_specs,
                 sem=("arbitrary",))(*ins)


RC = 256


def _conv_fwd_rows(pad_ref, w_ref, kw, head, r0):
    acc = None
    for k in range(kw):
        term = w_ref[k:k + 1, :] * pad_ref[pl.ds(head - (kw - 1) + k + r0, RC), :]
        acc = term if acc is None else acc + term
    return acc


def _conv_bwd_rows(pad2_ref, w_ref, kw, r0):
    acc = None
    for k in range(kw):
        term = w_ref[k:k + 1, :] * pad2_ref[pl.ds(kw - 1 - k + r0, RC), :]
        acc = term if acc is None else acc + term
    return acc


def _conv_dw(pad_ref, dout_ref, dw_ref, kw, head):
    for k in range(kw):
        acc = None
        for r0 in range(0, S, RC):
            term = jnp.sum(pad_ref[pl.ds(head - (kw - 1) + k + r0, RC), :] * dout_ref[pl.ds(r0, RC), :],
                           axis=0, keepdims=True)
            acc = term if acc is None else acc + term
        dw_ref[k:k + 1, :] = acc


def _col_spec(width, off_blocks=0):
    return pl.BlockSpec((S, width), lambda j: (0, j + off_blocks))


def _dn_pre_fwd(proj, conv_w):
    head = 8

    def body(x_ref, w_ref, o_ref, pad_ref):
        j = pl.program_id(0)
        pad_ref[pl.ds(0, head), :] = jnp.zeros((head, DH), F32)
        pad_ref[pl.ds(head, S), :] = x_ref[...]
        scale = jnp.where(j < H, DH ** -0.5, 1.0)
        for r0 in range(0, S, RC):
            y = _silu(_conv_fwd_rows(pad_ref, w_ref, DNK, head, r0))
            rinv = lax.rsqrt(jnp.sum(y * y, axis=-1, keepdims=True) + EPS)
            o_ref[pl.ds(r0, RC), :] = jnp.where(j < 2 * H, y * rinv * scale, y)

    return _call(body, "dn_pre_fwd", _sds((S, 3 * DNW)), grid=(3 * H,),
                 in_specs=[_col_spec(DH), pl.BlockSpec((DNK, DH), lambda j: (0, j))], out_specs=_col_spec(DH),
                 scratch=[pltpu.VMEM((S + head, DH), F32)], sem=("parallel",))(proj, conv_w)


def _dn_pre_bwd(dqkv, proj, conv_w):
    head = 8

    def body(d_ref, x_ref, w_ref, dx_ref, dw_ref, pad_ref, pad2_ref):
        j = pl.program_id(0)
        pad_ref[pl.ds(0, head), :] = jnp.zeros((head, DH), F32)
        pad_ref[pl.ds(head, S), :] = x_ref[...]
        pad2_ref[pl.ds(S, head), :] = jnp.zeros((head, DH), F32)
        scale = jnp.where(j < H, DH ** -0.5, 1.0)
        for r0 in range(0, S, RC):
            xc = _conv_fwd_rows(pad_ref, w_ref, DNK, head, r0)
            y = _silu(xc)
            rinv = lax.rsqrt(jnp.sum(y * y, axis=-1, keepdims=True) + EPS)
            yn = y * rinv
            do = d_ref[pl.ds(r0, RC), :]
            dy_n = scale * rinv * (do - yn * jnp.sum(do * yn, axis=-1, keepdims=True))
            dy = jnp.where(j < 2 * H, dy_n, do)
            pad2_ref[pl.ds(r0, RC), :] = dy * _dsilu(xc)
        for r0 in range(0, S, RC):
            dx_ref[pl.ds(r0, RC), :] = _conv_bwd_rows(pad2_ref, w_ref, DNK, r0).astype(BF16)
        _conv_dw(pad_ref, pad2_ref, dw_ref, DNK, head)

    wspec = pl.BlockSpec((DNK, DH), lambda j: (0, j))
    return _call(body, "dn_pre_bwd", [_sds((S, 3 * DNW), BF16), _sds((DNK, 3 * DNW))], grid=(3 * H,),
                 in_specs=[_col_spec(DH), _col_spec(DH), wspec], out_specs=[_col_spec(DH), wspec],
                 scratch=[pltpu.VMEM((S + head, DH), F32), pltpu.VMEM((S + head, DH), F32)],
                 sem=("parallel",))(dqkv, proj, conv_w)


CF_HEAD = 32
CF_VAL0 = O_GLU // LANE
CF_GL0 = (O_GLU + CFW) // LANE


def _cf_conv_fwd(proj, conv_w):
    def body(val_ref, gl_ref, w_ref, o_ref, pad_ref):
        pad_ref[pl.ds(0, CF_HEAD), :] = jnp.zeros((CF_HEAD, LANE), F32)
        pad_ref[pl.ds(CF_HEAD, S), :] = val_ref[...] * _sigmoid(gl_ref[...])
        for r0 in range(0, S, RC):
            o_ref[pl.ds(r0, RC), :] = _conv_fwd_rows(pad_ref, w_ref, CFK, CF_HEAD, r0)

    wspec = pl.BlockSpec((CFK, LANE), lambda j: (0, j))
    return _call(body, "cf_conv_fwd", _sds((S, CFW)), grid=(CFW // LANE,),
                 in_specs=[_col_spec(LANE, CF_VAL0), _col_spec(LANE, CF_GL0), wspec], out_specs=_col_spec(LANE),
                 scratch=[pltpu.VMEM((S + CF_HEAD, LANE), F32)], sem=("parallel",))(proj, proj, conv_w)


def _cf_conv_bwd(du1, proj, conv_w):
    def body(d_ref, val_ref, gl_ref, w_ref, dval_ref, dgl_ref, dw_ref, pad_ref, pad2_ref):
        sg = _sigmoid(gl_ref[...])
        pad_ref[pl.ds(0, CF_HEAD), :] = jnp.zeros((CF_HEAD, LANE), F32)
        pad_ref[pl.ds(CF_HEAD, S), :] = val_ref[...] * sg
        pad2_ref[pl.ds(0, S), :] = d_ref[...]
        pad2_ref[pl.ds(S, CF_HEAD), :] = jnp.zeros((CF_HEAD, LANE), F32)
        for r0 in range(0, S, RC):
            du0 = _conv_bwd_rows(pad2_ref, w_ref, CFK, r0)
            rows = pl.ds(r0, RC)
            sgr = _sigmoid(gl_ref[rows, :])
            dval_ref[rows, :] = (du0 * sgr).astype(BF16)
            dgl_ref[rows, :] = (du0 * val_ref[rows, :] * sgr * (1.0 - sgr)).astype(BF16)
        _conv_dw(pad_ref, pad2_ref, dw_ref, CFK, CF_HEAD)

    wspec = pl.BlockSpec((CFK, LANE), lambda j: (0, j))
    return _call(body, "cf_conv_bwd", [_sds((S, CFW), BF16), _sds((S, CFW), BF16), _sds((CFK, CFW))],
                 grid=(CFW // LANE,),
                 in_specs=[_col_spec(LANE), _col_spec(LANE, CF_VAL0), _col_spec(LANE, CF_GL0), wspec],
                 out_specs=[_col_spec(LANE), _col_spec(LANE), wspec],
                 scratch=[pltpu.VMEM((S + CF_HEAD, LANE), F32), pltpu.VMEM((S + CF_HEAD, LANE), F32)],
                 sem=("parallel",))(du1, proj, proj, conv_w)


def _cf_ln_fwd(u1, g, b):
    def body(u_ref, g_ref, b_ref, o_ref):
        u = u_ref[...]
        mu = jnp.mean(u, axis=-1, keepdims=True)
        xc = u - mu
        y = xc * lax.rsqrt(jnp.mean(xc * xc, axis=-1, keepdims=True) + EPS)
        o_ref[...] = _silu(y * g_ref[...] + b_ref[...]).astype(BF16)

    return _call(body, "cf_ln_fwd", _sds((S, CFW), BF16), grid=(S // TS,),
                 in_specs=[_row_spec(CFW), _vec_spec(CFW), _vec_spec(CFW)], out_specs=_row_spec(CFW),
                 sem=("parallel",))(u1, g, b)


def _cf_ln_bwd(du3, u1, g, b):
    def body(d_ref, u_ref, g_ref, b_ref, du_ref, acc_ref):
        u = u_ref[...]
        mu = jnp.mean(u, axis=-1, keepdims=True)
        xc = u - mu
        rstd = lax.rsqrt(jnp.mean(xc * xc, axis=-1, keepdims=True) + EPS)
        xh = xc * rstd
        du2 = d_ref[...] * _dsilu(xh * g_ref[...] + b_ref[...])
        dxh = du2 * g_ref[...]
        du_ref[...] = rstd * (dxh - jnp.mean(dxh, axis=-1, keepdims=True)
                              - xh * jnp.mean(dxh * xh, axis=-1, keepdims=True))
        _acc_rows(acc_ref, [jnp.sum(du2 * xh, axis=0, keepdims=True), jnp.sum(du2, axis=0, keepdims=True)])

    return _call(body, "cf_ln_bwd", [_sds((S, CFW)), _sds((8, CFW))], grid=(S // TS,),
                 in_specs=[_row_spec(CFW), _row_spec(CFW), _vec_spec(CFW), _vec_spec(CFW)],
                 out_specs=[_row_spec(CFW), _acc_spec(CFW)], sem=("arbitrary",))(du3, u1, g, b)


FB = 256
FNB = FFN // FB
FF_HEAD = 8


def _ffn_mid_fwd(upall, conv_w):
    def body(gate_ref, up_ref, w_ref, o_ref, pad_ref):
        pad_ref[pl.ds(0, FF_HEAD), :] = jnp.zeros((FF_HEAD, FB), F32)
        pad_ref[pl.ds(FF_HEAD, S), :] = gate_ref[...]
        for r0 in range(0, S, RC):
            gc = _conv_fwd_rows(pad_ref, w_ref, FFK, FF_HEAD, r0)
            o_ref[pl.ds(r0, RC), :] = (_silu(gc) * up_ref[pl.ds(r0, RC), :]).astype(BF16)

    wspec = pl.BlockSpec((FFK, FB), lambda j: (0, j))
    return _call(body, "ffn_mid_fwd", _sds((S, FFN), BF16), grid=(FNB,),
                 in_specs=[_col_spec(FB), _col_spec(FB, FNB), wspec], out_specs=_col_spec(FB),
                 scratch=[pltpu.VMEM((S + FF_HEAD, FB), F32)], sem=("parallel",))(upall, upall, conv_w)


def _ffn_mid_bwd(dh, upall, conv_w):
    def body(d_ref, gate_ref, up_ref, w_ref, dgate_ref, dup_ref, dw_ref, pad_ref, pad2_ref):
        pad_ref[pl.ds(0, FF_HEAD), :] = jnp.zeros((FF_HEAD, FB), F32)
        pad_ref[pl.ds(FF_HEAD, S), :] = gate_ref[...]
        pad2_ref[pl.ds(S, FF_HEAD), :] = jnp.zeros((FF_HEAD, FB), F32)
        for r0 in range(0, S, RC):
            rows = pl.ds(r0, RC)
            gc = _conv_fwd_rows(pad_ref, w_ref, FFK, FF_HEAD, r0)
            dhv = d_ref[rows, :]
            dup_ref[rows, :] = (dhv * _silu(gc)).astype(BF16)
            pad2_ref[rows, :] = dhv * up_ref[rows, :] * _dsilu(gc)
        for r0 in range(0, S, RC):
            dgate_ref[pl.ds(r0, RC), :] = _conv_bwd_rows(pad2_ref, w_ref, FFK, r0).astype(BF16)
        _conv_dw(pad_ref, pad2_ref, dw_ref, FFK, FF_HEAD)

    wspec = pl.BlockSpec((FFK, FB), lambda j: (0, j))
    return _call(body, "ffn_mid_bwd", [_sds((S, FFN), BF16), _sds((S, FFN), BF16), _sds((FFK, FFN))],
                 grid=(FNB,), in_specs=[_col_spec(FB), _col_spec(FB), _col_spec(FB, FNB), wspec],
                 out_specs=[_col_spec(FB), _col_spec(FB), wspec],
                 scratch=[pltpu.VMEM((S + FF_HEAD, FB), F32), pltpu.VMEM((S + FF_HEAD, FB), F32)],
                 sem=("parallel",))(dh, upall, upall, conv_w)


GT = 256
SM_BLK = O_SM // LANE


def _chunk_tri(lower):
    r = lax.broadcasted_iota(jnp.int32, (GT, GT), 0)
    c = lax.broadcasted_iota(jnp.int32, (GT, GT), 1)
    same = (r // CH) == (c // CH)
    tri = (c <= r) if lower else (c >= r)
    return jnp.where(same & tri, 1.0, 0.0).astype(F32)


def _gates_fwd(proj, alog_v, dtb_v):
    def body(sm_ref, al_ref, dt_ref, o_ref):
        lane = lax.broadcasted_iota(jnp.int32, (GT, LANE), 1)
        tri = _chunk_tri(True)
        na = -jnp.exp(al_ref[...])
        for r0 in range(0, S, GT):
            sm = sm_ref[pl.ds(r0, GT), :]
            raw = jnp.where((lane >= H) & (lane < 2 * H), na * _softplus(sm + dt_ref[...]), 0.0)
            gc = _dot(tri, raw, NN, precision=HI)
            o_ref[pl.ds(r0, GT), :] = jnp.where(lane < H, _sigmoid(sm), gc)

    return _call(body, "gates_fwd", _sds((S, LANE)), grid=(1,),
                 in_specs=[pl.BlockSpec((S, LANE), lambda i: (0, SM_BLK)), _vec_spec(LANE), _vec_spec(LANE)],
                 out_specs=pl.BlockSpec((S, LANE), lambda i: (0, 0)), sem=("arbitrary",))(proj, alog_v, dtb_v)


def _gates_bwd(dgb, proj, alog_v, dtb_v):
    def body(d_ref, sm_ref, al_ref, dt_ref, o_ref, acc_ref):
        lane = lax.broadcasted_iota(jnp.int32, (GT, LANE), 1)
        is_g = (lane >= H) & (lane < 2 * H)
        tri = _chunk_tri(False)
        na = -jnp.exp(al_ref[...])
        d_al = jnp.zeros((1, LANE), F32)
        d_dt = jnp.zeros((1, LANE), F32)
        for r0 in range(0, S, GT):
            sm = sm_ref[pl.ds(r0, GT), :]
            dv = d_ref[pl.ds(r0, GT), :]
            z = sm + dt_ref[...]
            draw = _dot(tri, jnp.where(is_g, dv, 0.0), NN, precision=HI)
            dlogit = jnp.where(is_g, draw * na * _sigmoid(z), 0.0)
            d_al = d_al + jnp.sum(jnp.where(is_g, draw * na * _softplus(z), 0.0), axis=0, keepdims=True)
            d_dt = d_dt + jnp.sum(dlogit, axis=0, keepdims=True)
            bt = _sigmoid(sm)
            o_ref[pl.ds(r0, GT), :] = jnp.where(lane < H, dv * bt * (1.0 - bt), dlogit).astype(BF16)
        acc_ref[...] = jnp.zeros_like(acc_ref)
        acc_ref[0:1, :] = d_al
        acc_ref[1:2, :] = d_dt

    return _call(body, "gates_bwd", [_sds((S, LANE), BF16), _sds((8, LANE))], grid=(1,),
                 in_specs=[pl.BlockSpec((S, LANE), lambda i: (0, 0)), pl.BlockSpec((S, LANE), lambda i: (0, SM_BLK)),
                           _vec_spec(LANE), _vec_spec(LANE)],
                 out_specs=[pl.BlockSpec((S, LANE), lambda i: (0, 0)), _acc_spec(LANE)],
                 sem=("arbitrary",))(dgb, proj, alog_v, dtb_v)


def _neumann_inv(a, eye):
    x = -a
    t = eye + x
    p = x
    for _ in range(5):
        p = _dot(p, p, NN, precision=HI)
        t = t + _dot(t, p, NN, precision=HI)
    return t


def _head_specs():
    q = pl.BlockSpec((S, DH), lambda h: (0, h))
    k = pl.BlockSpec((S, DH), lambda h: (0, H + h))
    v = pl.BlockSpec((S, DH), lambda h: (0, 2 * H + h))
    gb = pl.BlockSpec((None, S, DH), lambda h: (h, 0, 0))
    gr = pl.BlockSpec((None, NCH, CH), lambda h: (h, 0, 0))
    return q, k, v, gb, gr


ST_SPEC = pl.BlockSpec((None, NCH, DH, DH), lambda h: (h, 0, 0, 0))
TM_SPEC = pl.BlockSpec((None, NCH, CH, CH), lambda h: (h, 0, 0, 0))


def _delta_fwd(qkvn, gb, gr, bb):
    def body(q_ref, k_ref, v_ref, gb_ref, gr_ref, bb_ref, o_ref, st_ref, tm_ref):
        ri = lax.broadcasted_iota(jnp.int32, (CH, CH), 0)
        ci = lax.broadcasted_iota(jnp.int32, (CH, CH), 1)
        strict = ri > ci
        causal = ri >= ci
        eye = jnp.where(ri == ci, 1.0, 0.0).astype(F32)

        def step(n, st):
            rows = pl.ds(pl.multiple_of(n * CH, CH), CH)
            q, k, v, g, beta = q_ref[rows, :], k_ref[rows, :], v_ref[rows, :], gb_ref[rows, :], bb_ref[rows, :]
            diff = g[:, :CH] - gr_ref[pl.ds(n, 1), :]
            el = jnp.exp(jnp.where(causal, diff, 0.0))
            eg = jnp.exp(g)
            gl = g[CH - 1:CH, :]
            kb = k * beta
            kbf = k.astype(BF16)
            a = jnp.where(strict, _dot(kb.astype(BF16), kbf, NT) * el, 0.0)
            t = _neumann_inv(a, eye)
            tm_ref[n] = t
            st_ref[n] = st
            sb = st.astype(BF16)
            r = v * beta - _dot((kb * eg).astype(BF16), sb, NN)
            ub = _dot(t, r, NN, precision=HI).astype(BF16)
            p = jnp.where(causal, _dot(q.astype(BF16), kbf, NT) * el, 0.0)
            o_ref[rows, :] = _dot((q * eg).astype(BF16), sb, NN) + _dot(p.astype(BF16), ub, NN)
            kd = k * jnp.exp(gl - g)
            return st * jnp.exp(gl) + _dot(kd.astype(BF16), ub, TN)

        lax.fori_loop(0, NCH, step, jnp.zeros((DH, DH), F32))

    q, k, v, gbs, grs = _head_specs()
    return _call(body, "delta_fwd", [_sds((S, DNW)), _sds((H, NCH, DH, DH)), _sds((H, NCH, CH, CH))], grid=(H,),
                 in_specs=[q, k, v, gbs, grs, gbs], out_specs=[pl.BlockSpec((S, DH), lambda h: (0, h)), ST_SPEC, TM_SPEC],
                 sem=("parallel",))(qkvn, qkvn, qkvn, gb, gr, bb)


def _delta_bwd(qkvn, gb, gr, bb, st_all, tm_all, do_all):
    def body(q_ref, k_ref, v_ref, gb_ref, gr_ref, bb_ref, st_ref, tm_ref, do_ref,
             dq_ref, dk_ref, dv_ref, dg_ref, db_ref):
        ri = lax.broadcasted_iota(jnp.int32, (CH, CH), 0)
        ci = lax.broadcasted_iota(jnp.int32, (CH, CH), 1)
        lo_s, lo_c, up_s, up_c = ri > ci, ri >= ci, ri < ci, ri <= ci
        last_row = lax.broadcasted_iota(jnp.int32, (CH, 1), 0) == CH - 1

        def rs(mat):
            return jnp.sum(mat, axis=1, keepdims=True)

        def total(mat):
            return jnp.sum(rs(mat), axis=0, keepdims=True)

        def step(i, ds):
            n = NCH - 1 - i
            rows = pl.ds(pl.multiple_of(n * CH, CH), CH)
            q, k, v, g, beta = q_ref[rows, :], k_ref[rows, :], v_ref[rows, :], gb_ref[rows, :], bb_ref[rows, :]
            do = do_ref[rows, :]
            t = tm_ref[n]
            st = st_ref[n]
            diff = g[:, :CH] - gr_ref[pl.ds(n, 1), :]
            el = jnp.exp(jnp.where(lo_c, diff, 0.0))
            eu = jnp.exp(jnp.where(up_c, -diff, 0.0))
            eg = jnp.exp(g)
            gl = g[CH - 1:CH, :]
            egl = jnp.exp(gl)
            ekd = jnp.exp(gl - g)
            kb = k * beta
            kbg = kb * eg
            qg = q * eg
            kd = k * ekd
            qb, kbf, kbb = q.astype(BF16), k.astype(BF16), kb.astype(BF16)
            kbgb, qgb, kdb = kbg.astype(BF16), qg.astype(BF16), kd.astype(BF16)
            sb, dob, dsb = st.astype(BF16), do.astype(BF16), ds.astype(BF16)
            r = v * beta - _dot(kbgb, sb, NN)
            u = _dot(t, r, NN, precision=HI)
            ub = u.astype(BF16)
            kk, qk = _dot(kbb, kbf, NT), _dot(qb, kbf, NT)
            kkt, qkt = _dot(kbf, kbb, NT), _dot(kbf, qb, NT)
            pt = jnp.where(up_c, qkt * eu, 0.0)
            du = _dot(pt.astype(BF16), dob, NN) + _dot(kdb, dsb, NN)
            dr = _dot(t, du, TN, precision=HI)
            drb = dr.astype(BF16)
            dpg = jnp.where(lo_c, _dot(dob, ub, NT), 0.0) * el
            dpgt = jnp.where(up_c, _dot(ub, dob, NT), 0.0) * eu
            dag = -jnp.where(lo_s, _dot(drb, ub, NT), 0.0) * el
            dagt = -jnp.where(up_s, _dot(ub, drb, NT), 0.0) * eu
            dqg = _dot(dob, sb, NT)
            dkbg = -_dot(drb, sb, NT)
            dkd = _dot(ub, dsb, NT)
            ds_new = _dot(qgb, dob, TN) + egl * ds - _dot(kbgb, drb, TN)
            dkb = _dot(dag.astype(BF16), kbf, NN) + dkbg * eg
            dk = (_dot(dagt.astype(BF16), kbb, NN) + _dot(dpgt.astype(BF16), qb, NN) + dkd * ekd + dkb * beta)
            dq = _dot(dpg.astype(BF16), kbf, NN) + dqg * eg
            dkd_kd = rs(dkd * kd)
            dg = (rs(dag * kk + dpg * qk) - rs(dagt * kkt + dpgt * qkt) + rs(dqg * qg) + rs(dkbg * kbg) - dkd_kd)
            dgl = jnp.sum(dkd_kd, axis=0, keepdims=True) + egl[:, 0:1] * total(ds * st)
            dg = dg + jnp.where(last_row, dgl, 0.0)
            dbeta = rs(dkb * k) + rs(dr * v)
            dq_ref[rows, :] = dq
            dk_ref[rows, :] = dk
            dv_ref[rows, :] = dr * beta
            dg_ref[rows, :] = jnp.broadcast_to(dg, (CH, DH))
            db_ref[rows, :] = jnp.broadcast_to(dbeta, (CH, DH))
            return ds_new

        lax.fori_loop(0, NCH, step, jnp.zeros((DH, DH), F32))

    q, k, v, gbs, grs = _head_specs()
    hcol = pl.BlockSpec((S, DH), lambda h: (0, h))
    return _call(body, "delta_bwd",
                 [_sds((S, DNW)), _sds((S, DNW)), _sds((S, DNW)), _sds((H, S, DH)), _sds((H, S, DH))], grid=(H,),
                 in_specs=[q, k, v, gbs, grs, gbs, ST_SPEC, TM_SPEC, hcol], out_specs=[hcol, hcol, hcol, gbs, gbs],
                 sem=("parallel",))(qkvn, qkvn, qkvn, gb, gr, bb, st_all, tm_all, do_all)


Z_BLK = O_Z // DNW


def _dn_post_fwd(o, proj, gn):
    def body(o_ref, z_ref, gn_ref, og_ref):
        for h in range(H):
            cols = slice(h * DH, (h + 1) * DH)
            ov = o_ref[:, cols]
            on = ov * lax.rsqrt(jnp.mean(ov * ov, axis=-1, keepdims=True) + EPS) * gn_ref[...]
            og_ref[:, cols] = (on * _silu(z_ref[:, cols])).astype(BF16)

    return _call(body, "dn_post_fwd", _sds((S, DNW), BF16), grid=(S // TS,),
                 in_specs=[_row_spec(DNW), pl.BlockSpec((TS, DNW), lambda i: (i, Z_BLK)), _vec_spec(DH)],
                 out_specs=_row_spec(DNW), sem=("parallel",))(o, proj, gn)


def _dn_post_bwd(dog, o, proj, gn):
    def body(d_ref, o_ref, z_ref, gn_ref, do_ref, dz_ref, acc_ref):
        dgn = jnp.zeros((1, DH), F32)
        for h in range(H):
            cols = slice(h * DH, (h + 1) * DH)
            ov, zv, dv = o_ref[:, cols], z_ref[:, cols], d_ref[:, cols]
            rinv = lax.rsqrt(jnp.mean(ov * ov, axis=-1, keepdims=True) + EPS)
            xn = ov * rinv
            don = dv * _silu(zv)
            dz_ref[:, cols] = (dv * xn * gn_ref[...] * _dsilu(zv)).astype(BF16)
            dgn = dgn + jnp.sum(don * xn, axis=0, keepdims=True)
            dxn = don * gn_ref[...]
            do_ref[:, cols] = rinv * (dxn - xn * jnp.mean(dxn * xn, axis=-1, keepdims=True))
        _acc_rows(acc_ref, [dgn])

    return _call(body, "dn_post_bwd", [_sds((S, DNW)), _sds((S, DNW), BF16), _sds((8, DH))], grid=(S // TS,),
                 in_specs=[_row_spec(DNW), _row_spec(DNW), pl.BlockSpec((TS, DNW), lambda i: (i, Z_BLK)), _vec_spec(DH)],
                 out_specs=[_row_spec(DNW), _row_spec(DNW), _acc_spec(DH)], sem=("arbitrary",))(dog, o, proj, gn)


GA_BLK = O_GA // D
GB_BLK = O_GB // D


def _merge_fwd(ba, bb, proj):
    def body(a_ref, b_ref, ga_ref, gb_ref, o_ref):
        o_ref[...] = (_sigmoid(ga_ref[...]) * a_ref[...] + _sigmoid(gb_ref[...]) * b_ref[...]).astype(BF16)

    return _call(body, "merge_fwd", _sds((S, D), BF16), grid=(S // TS,),
                 in_specs=[_row_spec(), _row_spec(), pl.BlockSpec((TS, D), lambda i: (i, GA_BLK)),
                           pl.BlockSpec((TS, D), lambda i: (i, GB_BLK))],
                 out_specs=_row_spec(), sem=("parallel",))(ba, bb, proj, proj)


def _merge_bwd(dm, ba, bb, proj):
    def body(d_ref, a_ref, b_ref, ga_ref, gb_ref, dga_ref, dgb_ref, da_ref, db_ref):
        d = d_ref[...]
        sa, sb = _sigmoid(ga_ref[...]), _sigmoid(gb_ref[...])
        dga_ref[...] = (d * a_ref[...] * sa * (1.0 - sa)).astype(BF16)
        dgb_ref[...] = (d * b_ref[...] * sb * (1.0 - sb)).astype(BF16)
        da_ref[...] = (d * sa).astype(BF16)
        db_ref[...] = (d * sb).astype(BF16)

    return _call(body, "merge_bwd", [_sds((S, D), BF16)] * 4, grid=(S // TS,),
                 in_specs=[_row_spec(), _row_spec(), _row_spec(), pl.BlockSpec((TS, D), lambda i: (i, GA_BLK)),
                           pl.BlockSpec((TS, D), lambda i: (i, GB_BLK))],
                 out_specs=[_row_spec()] * 4, sem=("parallel",))(dm, ba, bb, proj, proj)


def _pad_win(w):
    return jnp.concatenate([w[:, :4096], w[:, 4112:], w[:, 4096:4112], jnp.zeros((w.shape[0], NINP - NIN), w.dtype)], axis=1)


def _unpad_win(w):
    return jnp.concatenate([w[:, :4096], w[:, O_SM:O_SM + 16], w[:, 4096:O_SM]], axis=1)


def _lane_vec(v8, offset):
    return jnp.pad(v8, ((0, 0), (offset, LANE - 8 - offset)))


def _local_step(x, tgt, mod, norm1_g, norm2_g, final_g, w_in_p, dn_conv_w, a_log, dt_bias, dn_norm_g, dn_w_o,
                cf_conv_w, cf_ln_g, cf_ln_b, cf_w_o, w_out, ffn_w_up, ffn_conv_w, ffn_w_down):
    sh1, sc1, gt1, sh2, sc2, gt2 = (mod[:, i * D:(i + 1) * D] for i in range(6))
    alog_v, dtb_v = _lane_vec(a_log, H), _lane_vec(dt_bias, H)

    hn1 = _norm_mod(x, norm1_g, sc1, sh1, "norm_mod1")
    proj = _mm(hn1, w_in_p, "nn", F32, "mm_in", tn=1152)
    qkvn = _dn_pre_fwd(proj, dn_conv_w)
    gates = _gates_fwd(proj, alog_v, dtb_v)
    beta_t = gates[:, 0:H].T
    g_t = gates[:, H:2 * H].T
    gb = jnp.broadcast_to(g_t[:, :, None], (H, S, DH))
    bb = jnp.broadcast_to(beta_t[:, :, None], (H, S, DH))
    gr = g_t.reshape(H, NCH, CH)
    o, st_all, tm_all = _delta_fwd(qkvn, gb, gr, bb)
    og = _dn_post_fwd(o, proj, dn_norm_g)
    br_a = _mm(og, dn_w_o, "nn", F32, "mm_dn_o")
    u1 = _cf_conv_fwd(proj, cf_conv_w)
    u3 = _cf_ln_fwd(u1, cf_ln_g, cf_ln_b)
    br_b = _mm(u3, cf_w_o, "nn", F32, "mm_cf_o")
    merged = _merge_fwd(br_a, br_b, proj)
    mix = _mm(merged, w_out, "nn", F32, "mm_out")
    x2, hn2 = _resid_norm_mod(x, mix, gt1, norm2_g, sc2, sh2, "resid_norm_mod2")
    upall = _mm(hn2, ffn_w_up, "nn", F32, "mm_up")
    hmid = _ffn_mid_fwd(upall, ffn_conv_w)
    f = _mm(hmid, ffn_w_down, "nn", F32, "mm_down")

    dx3, df, acc_f = _loss_head(x2, f, tgt, gt2, final_g)
    d_final_g, d_gt2, loss = acc_f[0:1], acc_f[1:2], acc_f[2:3, 0:1]
    dhmid = _mm(df, ffn_w_down, "nt", F32, "mm_down_dx")
    g_w_down = _mm(hmid, df, "tn", BF16, "mm_down_dw")
    d_gate, d_up, g_ffn_conv = _ffn_mid_bwd(dhmid, upall, ffn_conv_w)
    d_upall = jnp.concatenate([d_gate, d_up], axis=1)
    dhn2 = _mm(d_upall, ffn_w_up, "nt", F32, "mm_up_dx")
    g_w_up = _mm(hn2, d_upall, "tn", BF16, "mm_up_dw")
    dx2, dmix, acc2 = _norm_mod_bwd(dhn2, x2, dx3, norm2_g, sc2, "norm_mod2_bwd", mix=mix, gt=gt1)
    d_sh2, d_sc2, d_norm2_g, d_gt1 = acc2[0:1], acc2[1:2], acc2[2:3], acc2[3:4]
    dmerged = _mm(dmix, w_out, "nt", F32, "mm_out_dx")
    g_w_out = _mm(merged, dmix, "tn", BF16, "mm_out_dw")
    d_ga, d_gb, d_bra, d_brb = _merge_bwd(dmerged, br_a, br_b, proj)
    du3 = _mm(d_brb, cf_w_o, "nt", F32, "mm_cf_o_dx")
    g_cf_w_o = _mm(u3, d_brb, "tn", BF16, "mm_cf_o_dw")
    du1, acc_ln = _cf_ln_bwd(du3, u1, cf_ln_g, cf_ln_b)
    d_val, d_gl, g_cf_conv = _cf_conv_bwd(du1, proj, cf_conv_w)
    dog = _mm(d_bra, dn_w_o, "nt", F32, "mm_dn_o_dx")
    g_dn_w_o = _mm(og, d_bra, "tn", BF16, "mm_dn_o_dw")
    do, dz, acc_gn = _dn_post_bwd(dog, o, proj, dn_norm_g)
    dq, dk, dv, dgb, dbb = _delta_bwd(qkvn, gb, gr, bb, st_all, tm_all, do)
    d_pre, g_dn_conv = _dn_pre_bwd(jnp.concatenate([dq, dk, dv], axis=1), proj, dn_conv_w)
    dgates = jnp.concatenate([dbb[:, :, 0].T, dgb[:, :, 0].T, jnp.zeros((S, LANE - 2 * H), F32)], axis=1)
    d_sm, acc_g = _gates_bwd(dgates, proj, alog_v, dtb_v)
    d_proj = jnp.concatenate([d_pre, dz, d_val, d_gl, d_ga, d_gb, d_sm], axis=1)
    dhn1 = _mm(d_proj, w_in_p, "nt", F32, "mm_in_dx", tk=1152)
    g_w_in_p = _mm(hn1, d_proj, "tn", BF16, "mm_in_dw", tn=1152)
    grad_x, acc1 = _norm_mod_bwd(dhn1, x, dx2, norm1_g, sc1, "norm_mod1_bwd")
    d_sh1, d_sc1, d_norm1_g = acc1[0:1], acc1[1:2], acc1[2:3]

    d_mod = jnp.concatenate([d_sh1, d_sc1, d_gt1, d_sh2, d_sc2, d_gt2], axis=1)
    small = dict(mod=d_mod, norm1_g=d_norm1_g, norm2_g=d_norm2_g, final_norm_g=d_final_g,
                 cf_ln_g=acc_ln[0:1], cf_ln_b=acc_ln[1:2], dn_norm_g=acc_gn[0:1],
                 dn_a_log=acc_g[0:1, H:2 * H], dn_dt_bias=acc_g[1:2, H:2 * H])
    big = dict(w_in=g_w_in_p, dn_conv_w=g_dn_conv, dn_w_o=g_dn_w_o, cf_conv_w=g_cf_conv, cf_w_o=g_cf_w_o,
               w_out=g_w_out, ffn_w_up=g_w_up, ffn_conv_w=g_ffn_conv, ffn_w_down=g_w_down)
    return loss, grad_x, big, small


def _dev_index(px, py, pc):
    return 4 * px + 2 * py + pc


def _all_gather(arrs, name):
    n = len(arrs)

    def body(*refs):
        ins, outs = refs[:n], refs[n:2 * n]
        send_sems, recv_sems, loc_sems = refs[2 * n:]
        x, y, c = _my_pos()
        me, sib = (x, y, c), (x, y, 1 - c)
        chips = [(1 - x, y), (x, 1 - y), (1 - x, 1 - y)]

        def cp(i, k, block, to, src=None):
            dst = outs[i].at[_dev_index(*block)]
            return pltpu.make_async_remote_copy(
                src_ref=dst if src is None else src, dst_ref=dst, send_sem=send_sems.at[i, k],
                recv_sem=recv_sems.at[i, k], device_id=to, device_id_type=MESH)

        mine = [pltpu.make_async_copy(ins[i], outs[i].at[_dev_index(*me)], loc_sems.at[i]) for i in range(n)]
        for m in mine:
            m.start()
        sent = []
        for i in range(n):
            sent.append(cp(i, 0, me, sib, src=ins[i]))
            sent += [cp(i, 1 + j, me, (*chip, c), src=ins[i]) for j, chip in enumerate(chips)]
        for s in sent:
            s.start()
        for i in range(n):
            for j, chip in enumerate(chips):
                cp(i, 1 + j, (*chip, c), me).wait_recv()
                fwd = cp(i, 4 + j, (*chip, c), sib)
                fwd.start()
                sent.append(fwd)
        for i in range(n):
            cp(i, 0, sib, me).wait_recv()
            for j, chip in enumerate(chips):
                cp(i, 4 + j, (*chip, 1 - c), me).wait_recv()
        for s in sent:
            s.wait_send()
        for m in mine:
            m.wait()

    outs = pl.pallas_call(
        body, out_shape=[_sds((NDEV,) + a.shape, a.dtype) for a in arrs], in_specs=[ANY] * n, out_specs=[ANY] * n,
        scratch_shapes=[pltpu.SemaphoreType.DMA((n, 7)), pltpu.SemaphoreType.DMA((n, 7)), pltpu.SemaphoreType.DMA((n,))],
        name=name)(*arrs)
    return list(outs)


def _pair_exchange(parts, name):
    n = len(parts)

    def body(*refs):
        ins, outs = refs[:n], refs[n:2 * n]
        send_sems, recv_sems = refs[2 * n:]
        x, y, c = _my_pos()
        copies = []
        for i in range(n):
            for q in range(4):
                copies.append(pltpu.make_async_remote_copy(
                    src_ref=ins[i].at[2 * q + (1 - c)], dst_ref=outs[i].at[q], send_sem=send_sems.at[i, q],
                    recv_sem=recv_sems.at[i, q], device_id=(x, y, 1 - c), device_id_type=MESH))
        for cpy in copies:
            cpy.start()
        for cpy in copies:
            cpy.wait()

    outs = pl.pallas_call(
        body, out_shape=[_sds((4,) + p.shape[1:], p.dtype) for p in parts], in_specs=[ANY] * n, out_specs=[ANY] * n,
        scratch_shapes=[pltpu.SemaphoreType.DMA((n, 4)), pltpu.SemaphoreType.DMA((n, 4))], name=name)(*parts)
    return list(outs)


def _chip_exchange(sums, name):
    n = len(sums)

    def body(*refs):
        ins, outs = refs[:n], refs[n:2 * n]
        send_sems, recv_sems = refs[2 * n:]
        x, y, c = _my_pos()
        chips = [(1 - x, y), (x, 1 - y), (1 - x, 1 - y)]
        copies = []
        for i in range(n):
            for j, (px, py) in enumerate(chips):
                copies.append(pltpu.make_async_remote_copy(
                    src_ref=ins[i].at[2 * px + py], dst_ref=outs[i].at[j], send_sem=send_sems.at[i, j],
                    recv_sem=recv_sems.at[i, j], device_id=(px, py, c), device_id_type=MESH))
        for cpy in copies:
            cpy.start()
        for cpy in copies:
            cpy.wait()

    outs = pl.pallas_call(
        body, out_shape=[_sds((3,) + s.shape[1:], s.dtype) for s in sums], in_specs=[ANY] * n, out_specs=[ANY] * n,
        scratch_shapes=[pltpu.SemaphoreType.DMA((n, 3)), pltpu.SemaphoreType.DMA((n, 3))], name=name)(*sums)
    return list(outs)


def _row_tile(r, itemsize):
    align = 32 // itemsize
    best = r
    for t in range(align, min(r, 256) + 1, align):
        if r % t == 0:
            best = t
    return best


def _pair_sum(part, got, name):
    _, r, cols = part.shape
    tr = _row_tile(r, part.dtype.itemsize)

    def body(p_ref, g_ref, o_ref):
        o_ref[...] = (p_ref[...].astype(F32) + g_ref[...].astype(F32)).astype(o_ref.dtype)

    return _call(body, name, _sds((4, r, cols), part.dtype), grid=(4, r // tr),
                 in_specs=[pl.BlockSpec((None, tr, cols), lambda q, i: (2 * q + lax.axis_index("c"), i, 0)),
                           pl.BlockSpec((None, tr, cols), lambda q, i: (q, i, 0))],
                 out_specs=pl.BlockSpec((None, tr, cols), lambda q, i: (q, i, 0)),
                 sem=("parallel", "parallel"))(part, got)


def _final_sum_adam(sums, got, w, m, v, name):
    r, cols = w.shape
    tr = _row_tile(r, sums.dtype.itemsize)

    def body(s_ref, g_ref, w_ref, m_ref, v_ref, go_ref, dl_ref, nm_ref, nv_ref):
        g = ((s_ref[...].astype(F32) + g_ref[0].astype(F32)) + g_ref[1].astype(F32)) + g_ref[2].astype(F32)
        dl, nm, nv = _adam(w_ref[...], g, m_ref[...], v_ref[...])
        go_ref[...] = g
        dl_ref[...] = dl
        nm_ref[...] = nm
        nv_ref[...] = nv

    big = pl.BlockSpec((tr, cols), lambda i: (i, 0))
    return _call(body, name, [_sds((r, cols))] * 4, grid=(r // tr,),
                 in_specs=[pl.BlockSpec((None, tr, cols), lambda i: (2 * lax.axis_index("x") + lax.axis_index("y"), i, 0)),
                           pl.BlockSpec((3, tr, cols), lambda i: (0, i, 0)), big, big, big],
                 out_specs=[big] * 4, sem=("parallel",))(sums, got, w, m, v)


def _small_adam(g_all, w, m, v):
    npk = w.shape[1]

    def body(g_ref, w_ref, m_ref, v_ref, go_ref, dl_ref, nm_ref, nv_ref):
        g = g_ref[0:1, :]
        for k in range(1, NDEV):
            g = g + g_ref[k:k + 1, :]
        dl, nm, nv = _adam(w_ref[...], g, m_ref[...], v_ref[...])
        go_ref[...] = g
        dl_ref[...] = dl
        nm_ref[...] = nm
        nv_ref[...] = nv

    return _call(body, "small_adam", [_sds((1, npk))] * 4)(g_all, w, m, v)


SMALL = [("b_ada", 6 * D), ("norm1_g", D), ("norm2_g", D), ("final_norm_g", D), ("cf_ln_g", CFW), ("cf_ln_b", CFW),
         ("dn_norm_g", DH), ("dn_a_log", H), ("dn_dt_bias", H)]
BIG = ["w_in", "dn_conv_w", "dn_w_o", "cf_conv_w", "cf_w_o", "w_out", "ffn_w_up", "ffn_conv_w", "ffn_w_down"]
ROW_SHARDED = ("w_out", "ffn_w_down")
NAMES = ["w_ada", "b_ada", "norm1_g", "w_in", "dn_conv_w", "dn_a_log", "dn_dt_bias", "dn_norm_g", "dn_w_o", "cf_conv_w",
         "cf_ln_g", "cf_ln_b", "cf_w_o", "w_out", "norm2_g", "ffn_w_up", "ffn_conv_w", "ffn_w_down", "final_norm_g"]


def _pack_small(d):
    rows = []
    for nm, n in SMALL:
        row = d[nm].reshape(1, n)
        pad = (-n) % LANE
        rows.append(jnp.pad(row, ((0, 0), (0, pad))) if pad else row)
    return jnp.concatenate(rows, axis=1)


def _unpack_small(row, shapes):
    out, off = {}, 0
    for nm, n in SMALL:
        out[nm] = row[0, off:off + n].reshape(shapes[nm])
        off += n + ((-n) % LANE)
    return out


def _cols_from_gathered(g):
    return jnp.transpose(g, (1, 0, 2)).reshape(g.shape[1], NDEV * g.shape[2])


def _cols_to_parts(full):
    r, ctot = full.shape
    return jnp.transpose(full.reshape(r, NDEV, ctot // NDEV), (1, 0, 2))


def kernel(x, c, w_ada, b_ada, norm1_g, w_in, dn_conv_w, dn_a_log, dn_dt_bias, dn_norm_g, dn_w_o, cf_conv_w, cf_ln_g, cf_ln_b, cf_w_o, w_out, norm2_g, ffn_w_up, ffn_conv_w, ffn_w_down, final_norm_g, loss_target, m_w_ada, m_b_ada, m_norm1_g, m_w_in, m_dn_conv_w, m_dn_a_log, m_dn_dt_bias, m_dn_norm_g, m_dn_w_o, m_cf_conv_w, m_cf_ln_g, m_cf_ln_b, m_cf_w_o, m_w_out, m_norm2_g, m_ffn_w_up, m_ffn_conv_w, m_ffn_w_down, m_final_norm_g, v_w_ada, v_b_ada, v_norm1_g, v_w_in, v_dn_conv_w, v_dn_a_log, v_dn_dt_bias, v_dn_norm_g, v_dn_w_o, v_cf_conv_w, v_cf_ln_g, v_cf_ln_b, v_cf_w_o, v_w_out, v_norm2_g, v_ffn_w_up, v_ffn_conv_w, v_ffn_w_down, v_final_norm_g):
    args = locals()
    w = {nm: args[nm] for nm in NAMES}
    mo = {nm: args["m_" + nm] for nm in NAMES}
    vo = {nm: args["v_" + nm] for nm in NAMES}
    shapes = {nm: w[nm].shape for nm in NAMES}
    px, py, pc = _my_pos()
    me = _dev_index(px, py, pc)

    def mat(a):
        return a.reshape(a.shape[-2:])

    conv_names = ("dn_conv_w", "cf_conv_w", "ffn_conv_w")
    send = [mat(w[nm]).astype(F32 if nm in conv_names else BF16) for nm in BIG] + [c]
    got = _all_gather(send, "gather_weights")
    gw = dict(zip(BIG, got[:-1]))
    c_all = got[-1].reshape(NDEV, D)
    full = {}
    for nm in BIG:
        if nm in ROW_SHARDED:
            full[nm] = gw[nm].reshape(NDEV * gw[nm].shape[1], gw[nm].shape[2])
        else:
            full[nm] = _cols_from_gathered(gw[nm])
    w_in_p = _pad_win(full["w_in"])

    ncol = 6 * D // NDEV
    b_sh = lax.dynamic_slice(b_ada.reshape(1, 6 * D), (0, me * ncol), (1, ncol))
    mod_sh = _ada_fwd(c_all, mat(w_ada), b_sh)
    mod_all = _all_gather([mod_sh], "gather_mod")[0]
    mod = lax.dynamic_index_in_dim(mod_all, me, axis=1, keepdims=False).reshape(1, 6 * D)

    vec = lambda a: a.reshape(1, -1)
    loss, grad_x, big, small = _local_step(
        x.reshape(S, D), loss_target.reshape(S, D), mod, vec(norm1_g), vec(norm2_g), vec(final_norm_g), w_in_p,
        full["dn_conv_w"], vec(dn_a_log), vec(dn_dt_bias), vec(dn_norm_g), full["dn_w_o"], full["cf_conv_w"],
        vec(cf_ln_g), vec(cf_ln_b), full["cf_w_o"], full["w_out"], full["ffn_w_up"], full["ffn_conv_w"],
        full["ffn_w_down"])

    big["w_in"] = _unpad_win(big["w_in"])
    parts = []
    for nm in BIG:
        g = big[nm]
        if nm in ROW_SHARDED:
            parts.append(g.reshape(NDEV, g.shape[0] // NDEV, g.shape[1]))
        else:
            parts.append(_cols_to_parts(g))
    from_sib = _pair_exchange(parts, "rs_pair")
    sums = [_pair_sum(p, r, "rs_pair_sum_" + nm) for nm, p, r in zip(BIG, parts, from_sib)]
    from_chips = _chip_exchange(sums, "rs_chips")
    res = {}
    for nm, s, r in zip(BIG, sums, from_chips):
        outs = _final_sum_adam(s, r, mat(w[nm]), mat(mo[nm]), mat(vo[nm]), "adam_" + nm)
        res[nm] = [o.reshape(shapes[nm]) for o in outs]

    small["b_ada"] = small.pop("mod")
    g_small = _all_gather([_pack_small(small)], "gather_small")[0].reshape(NDEV, -1)
    outs = _small_adam(g_small, _pack_small({nm: w[nm] for nm, _ in SMALL}), _pack_small({nm: mo[nm] for nm, _ in SMALL}),
                       _pack_small({nm: vo[nm] for nm, _ in SMALL}))
    unpacked = [_unpack_small(o, shapes) for o in outs]
    for nm, _ in SMALL:
        res[nm] = [u[nm] for u in unpacked]

    dmod_sel = lax.dynamic_slice(g_small[:, :6 * D], (0, me * ncol), (NDEV, ncol))
    outs = _ada_bwd_adam(c_all, dmod_sel, mat(w_ada), mat(m_w_ada), mat(v_w_ada))
    res["w_ada"] = [o.reshape(shapes["w_ada"]) for o in outs]

    loss = lax.psum(loss.reshape(()), ("x", "y", "c"))
    out = [loss, grad_x.reshape(x.shape)]
    for k in range(4):
        out += [res[nm][k] for nm in NAMES]
    return tuple(out)
```

```python
import functools

import jax
import jax.numpy as jnp
from jax import lax
from jax.experimental import pallas as pl
from jax.experimental.pallas import tpu as pltpu

F32 = jnp.float32
BF16 = jnp.bfloat16
HI = lax.Precision.HIGHEST
MESH = pl.DeviceIdType.MESH
ANY = pl.BlockSpec(memory_space=pl.ANY)

NDEV = 8
D = 2048
S = 2048
H = 8
DH = 128
DNW = H * DH
CFW = 1024
CFK = 31
DNK = 4
FFN = 5632
FFK = 3
CH = 64
NCH = S // CH
EPS = 1e-6
NIN = 10256
NINP = 10368
O_Z, O_GLU, O_GA, O_GB, O_SM = 3072, 4096, 6144, 8192, 10240
LANE = 128
TS = 256
VMEM_LIMIT = 56 * 1024 * 1024

ADAM_LR, ADAM_B1, ADAM_B2, ADAM_EPS, ADAM_WD, ADAM_STEP = 0.001, 0.9, 0.999, 1e-08, 0.01, 10


def _call(body, name, out_shape, grid=(), in_specs=None, out_specs=None, scratch=(), sem=None):
    kw = {}
    if in_specs is not None:
        kw["in_specs"] = in_specs
    if out_specs is not None:
        kw["out_specs"] = out_specs
    return pl.pallas_call(
        body, out_shape=out_shape, grid=grid, scratch_shapes=scratch, name=name,
        compiler_params=pltpu.CompilerParams(dimension_semantics=sem, vmem_limit_bytes=VMEM_LIMIT), **kw)


def _sds(shape, dtype=F32):
    return jax.ShapeDtypeStruct(shape, dtype)


def _tile(dim, pref):
    if dim <= pref:
        return dim
    best = None
    for t in range(LANE, pref + 1, LANE):
        if dim % t == 0:
            best = t
    assert best is not None, (dim, pref)
    return best


def _sigmoid(x):
    return 1.0 / (1.0 + jnp.exp(-x))


def _silu(x):
    return x * _sigmoid(x)


def _dsilu(x):
    s = _sigmoid(x)
    return s * (1.0 + x * (1.0 - s))


def _softplus(x):
    return jnp.maximum(x, 0.0) + jnp.log(1.0 + jnp.exp(-jnp.abs(x)))


def _dot(a, b, dims, precision=None):
    return lax.dot_general(a, b, (dims, ((), ())), preferred_element_type=F32, precision=precision)


NN = ((1,), (0,))
NT = ((1,), (1,))
TN = ((0,), (0,))


def _my_pos():
    return lax.axis_index("x"), lax.axis_index("y"), lax.axis_index("c")


def _mm(a, b, mode, out_dtype, name, tm=1024, tn=1024, tk=2048):
    if mode == "nn":
        (m, k), (k2, n) = a.shape, b.shape
    elif mode == "nt":
        (m, k), (n, k2) = a.shape, b.shape
    else:
        (k, m), (k2, n) = a.shape, b.shape
    assert k == k2, (a.shape, b.shape, mode)
    tm, tn, tk = _tile(m, tm), _tile(n, tn), _tile(k, tk)
    nk = k // tk
    dims = {"nn": NN, "nt": NT, "tn": TN}[mode]

    def body(a_ref, b_ref, o_ref, acc_ref):
        kk = pl.program_id(2)
        part = _dot(a_ref[...], b_ref[...], dims)

        @pl.when(kk == 0)
        def _():
            acc_ref[...] = part

        @pl.when(kk > 0)
        def _():
            acc_ref[...] += part

        @pl.when(kk == nk - 1)
        def _():
            o_ref[...] = acc_ref[...].astype(o_ref.dtype)

    if mode == "tn":
        a_spec = pl.BlockSpec((tk, tm), lambda i, j, kk: (kk, i))
    else:
        a_spec = pl.BlockSpec((tm, tk), lambda i, j, kk: (i, kk))
    if mode == "nt":
        b_spec = pl.BlockSpec((tn, tk), lambda i, j, kk: (j, kk))
    else:
        b_spec = pl.BlockSpec((tk, tn), lambda i, j, kk: (kk, j))
    return _call(body, name, _sds((m, n), out_dtype), grid=(m // tm, n // tn, nk),
                 in_specs=[a_spec, b_spec], out_specs=pl.BlockSpec((tm, tn), lambda i, j, kk: (i, j)),
                 scratch=[pltpu.VMEM((tm, tn), F32)], sem=("parallel", "parallel", "arbitrary"))(a, b)


def _ada_fwd(c_all, w_sh, b_sh):
    n = w_sh.shape[1]
    tn = 512

    def body(c_ref, w_ref, b_ref, o_ref):
        ca = _silu(c_ref[...]).astype(BF16)
        o_ref[...] = _dot(ca, w_ref[...].astype(BF16), NN) + b_ref[...]

    return _call(body, "ada_fwd", _sds((NDEV, n)), grid=(n // tn,),
                 in_specs=[pl.BlockSpec((NDEV, D), lambda j: (0, 0)), pl.BlockSpec((D, tn), lambda j: (0, j)),
                           pl.BlockSpec((1, tn), lambda j: (0, j))],
                 out_specs=pl.BlockSpec((NDEV, tn), lambda j: (0, j)), sem=("parallel",))(c_all, w_sh, b_sh)


def _adam(w, g, m, v):
    m = ADAM_B1 * m + (1.0 - ADAM_B1) * g
    v = ADAM_B2 * v + (1.0 - ADAM_B2) * (g * g)
    m_hat = m / (1.0 - ADAM_B1 ** ADAM_STEP)
    v_hat = v / (1.0 - ADAM_B2 ** ADAM_STEP)
    delta = -ADAM_LR * (m_hat / (jnp.sqrt(v_hat) + ADAM_EPS) + ADAM_WD * w)
    return delta, m, v


def _ada_bwd_adam(c_all, dmod_sel, w, m, v):
    r, n = w.shape
    tr = 256

    def body(c_ref, d_ref, w_ref, m_ref, v_ref, g_ref, dl_ref, nm_ref, nv_ref):
        ca = _silu(c_ref[...])
        g = _dot(ca, d_ref[...], TN, precision=HI)
        dl, nm, nv = _adam(w_ref[...], g, m_ref[...], v_ref[...])
        g_ref[...] = g
        dl_ref[...] = dl
        nm_ref[...] = nm
        nv_ref[...] = nv

    big = pl.BlockSpec((tr, n), lambda i: (i, 0))
    return _call(body, "ada_bwd_adam", [_sds((r, n))] * 4, grid=(r // tr,),
                 in_specs=[pl.BlockSpec((NDEV, tr), lambda i: (0, i)), pl.BlockSpec((NDEV, n), lambda i: (0, 0)),
                           big, big, big],
                 out_specs=[big] * 4, sem=("parallel",))(c_all, dmod_sel, w, m, v)


def _row_spec(width=D):
    return pl.BlockSpec((TS, width), lambda i: (i, 0))


def _vec_spec(width=D):
    return pl.BlockSpec((1, width), lambda i: (0, 0))


def _acc_spec(width=D):
    return pl.BlockSpec((8, width), lambda i: (0, 0))


def _norm_mod(x, g, sc, sh, name):
    def body(x_ref, g_ref, sc_ref, sh_ref, o_ref):
        xv = x_ref[...]
        r = lax.rsqrt(jnp.mean(xv * xv, axis=-1, keepdims=True) + EPS)
        o_ref[...] = ((xv * r) * g_ref[...] * (1.0 + sc_ref[...]) + sh_ref[...]).astype(BF16)

    return _call(body, name, _sds((S, D), BF16), grid=(S // TS,),
                 in_specs=[_row_spec(), _vec_spec(), _vec_spec(), _vec_spec()], out_specs=_row_spec(),
                 sem=("parallel",))(x, g, sc, sh)


def _resid_norm_mod(x, mix, gt, g, sc, sh, name):
    def body(x_ref, mix_ref, gt_ref, g_ref, sc_ref, sh_ref, x2_ref, o_ref):
        xv = x_ref[...] + gt_ref[...] * mix_ref[...]
        x2_ref[...] = xv
        r = lax.rsqrt(jnp.mean(xv * xv, axis=-1, keepdims=True) + EPS)
        o_ref[...] = ((xv * r) * g_ref[...] * (1.0 + sc_ref[...]) + sh_ref[...]).astype(BF16)

    return _call(body, name, [_sds((S, D)), _sds((S, D), BF16)], grid=(S // TS,),
                 in_specs=[_row_spec(), _row_spec()] + [_vec_spec()] * 4, out_specs=[_row_spec(), _row_spec()],
                 sem=("parallel",))(x, mix, gt, g, sc, sh)


def _acc_rows(acc_ref, rows):
    @pl.when(pl.program_id(0) == 0)
    def _():
        acc_ref[...] = jnp.zeros_like(acc_ref)

    for k, row in enumerate(rows):
        acc_ref[k:k + 1, :] += row


def _loss_head(x2, f, tgt, gt2, gf):
    def body(x2_ref, f_ref, t_ref, gt_ref, gf_ref, dx_ref, df_ref, acc_ref):
        fv = f_ref[...]
        x3 = x2_ref[...] + gt_ref[...] * fv
        r = lax.rsqrt(jnp.mean(x3 * x3, axis=-1, keepdims=True) + EPS)
        xn = x3 * r
        e = xn * gf_ref[...] - t_ref[...]
        loss = 0.5 * jnp.sum(jnp.mean(e * e, axis=-1, keepdims=True), axis=0, keepdims=True)
        dy = e * (1.0 / D)
        dxn = dy * gf_ref[...]
        dx3 = r * (dxn - xn * jnp.mean(dxn * xn, axis=-1, keepdims=True))
        dx_ref[...] = dx3
        df_ref[...] = (dx3 * gt_ref[...]).astype(BF16)
        _acc_rows(acc_ref, [jnp.sum(dy * xn, axis=0, keepdims=True), jnp.sum(dx3 * fv, axis=0, keepdims=True),
                            jnp.broadcast_to(loss, (1, D))])

    return _call(body, "loss_head", [_sds((S, D)), _sds((S, D), BF16), _sds((8, D))], grid=(S // TS,),
                 in_specs=[_row_spec(), _row_spec(), _row_spec(), _vec_spec(), _vec_spec()],
                 out_specs=[_row_spec(), _row_spec(), _acc_spec()], sem=("arbitrary",))(x2, f, tgt, gt2, gf)


def _norm_mod_bwd(dhn, x, dres, g, sc, name, mix=None, gt=None):
    gated = mix is not None

    def body(*refs):
        if gated:
            dhn_ref, x_ref, dres_ref, g_ref, sc_ref, mix_ref, gt_ref, dx_ref, dmix_ref, acc_ref = refs
        else:
            dhn_ref, x_ref, dres_ref, g_ref, sc_ref, dx_ref, acc_ref = refs
        xv = x_ref[...]
        dh = dhn_ref[...]
        r = lax.rsqrt(jnp.mean(xv * xv, axis=-1, keepdims=True) + EPS)
        xn = xv * r
        gv = g_ref[...]
        sc1 = 1.0 + sc_ref[...]
        dxn = dh * gv * sc1
        dx = dres_ref[...] + r * (dxn - xn * jnp.mean(dxn * xn, axis=-1, keepdims=True))
        dx_ref[...] = dx
        rows = [jnp.sum(dh, axis=0, keepdims=True), jnp.sum(dh * xn * gv, axis=0, keepdims=True),
                jnp.sum(dh * xn * sc1, axis=0, keepdims=True)]
        if gated:
            rows.append(jnp.sum(dx * mix_ref[...], axis=0, keepdims=True))
            dmix_ref[...] = (dx * gt_ref[...]).astype(BF16)
        _acc_rows(acc_ref, rows)

    ins = [dhn, x, dres, g, sc]
    in_specs = [_row_spec(), _row_spec(), _row_spec(), _vec_spec(), _vec_spec()]
    outs = [_sds((S, D))]
    out_specs = [_row_spec()]
    if gated:
        ins += [mix, gt]
        in_specs += [_row_spec(), _vec_spec()]
        outs.append(_sds((S, D), BF16))
        out_specs.append(_row_spec())
    outs.append(_sds((8, D)))
    out_specs.append(_acc_spec())
    return _call(body, name, outs, grid=(S // TS,), in_specs=in_specs, out_specs=out_specs,
                 sem=("arbitrary",))(*ins)


RC = 256


def _conv_fwd_rows(pad_ref, w_ref, kw, head, r0):
    acc = None
    for k in range(kw):
        term = w_ref[k:k + 1, :] * pad_ref[pl.ds(head - (kw - 1) + k + r0, RC), :]
        acc = term if acc is None else acc + term
    return acc


def _conv_bwd_rows(pad2_ref, w_ref, kw, r0):
    acc = None
    for k in range(kw):
        term = w_ref[k:k + 1, :] * pad2_ref[pl.ds(kw - 1 - k + r0, RC), :]
        acc = term if acc is None else acc + term
    return acc


def _conv_dw(pad_ref, dout_ref, dw_ref, kw, head):
    for k in range(kw):
        acc = None
        for r0 in range(0, S, RC):
            term = jnp.sum(pad_ref[pl.ds(head - (kw - 1) + k + r0, RC), :] * dout_ref[pl.ds(r0, RC), :],
                           axis=0, keepdims=True)
            acc = term if acc is None else acc + term
        dw_ref[k:k + 1, :] = acc


def _col_spec(width, off_blocks=0):
    return pl.BlockSpec((S, width), lambda j: (0, j + off_blocks))


def _dn_pre_fwd(proj, conv_w):
    head = 8

    def body(x_ref, w_ref, o_ref, pad_ref):
        j = pl.program_id(0)
        pad_ref[pl.ds(0, head), :] = jnp.zeros((head, DH), F32)
        pad_ref[pl.ds(head, S), :] = x_ref[...]
        scale = jnp.where(j < H, DH ** -0.5, 1.0)
        for r0 in range(0, S, RC):
            y = _silu(_conv_fwd_rows(pad_ref, w_ref, DNK, head, r0))
            rinv = lax.rsqrt(jnp.sum(y * y, axis=-1, keepdims=True) + EPS)
            o_ref[pl.ds(r0, RC), :] = jnp.where(j < 2 * H, y * rinv * scale, y)

    return _call(body, "dn_pre_fwd", _sds((S, 3 * DNW)), grid=(3 * H,),
                 in_specs=[_col_spec(DH), pl.BlockSpec((DNK, DH), lambda j: (0, j))], out_specs=_col_spec(DH),
                 scratch=[pltpu.VMEM((S + head, DH), F32)], sem=("parallel",))(proj, conv_w)


def _dn_pre_bwd(dqkv, proj, conv_w):
    head = 8

    def body(d_ref, x_ref, w_ref, dx_ref, dw_ref, pad_ref, pad2_ref):
        j = pl.program_id(0)
        pad_ref[pl.ds(0, head), :] = jnp.zeros((head, DH), F32)
        pad_ref[pl.ds(head, S), :] = x_ref[...]
        pad2_ref[pl.ds(S, head), :] = jnp.zeros((head, DH), F32)
        scale = jnp.where(j < H, DH ** -0.5, 1.0)
        for r0 in range(0, S, RC):
            xc = _conv_fwd_rows(pad_ref, w_ref, DNK, head, r0)
            y = _silu(xc)
            rinv = lax.rsqrt(jnp.sum(y * y, axis=-1, keepdims=True) + EPS)
            yn = y * rinv
            do = d_ref[pl.ds(r0, RC), :]
            dy_n = scale * rinv * (do - yn * jnp.sum(do * yn, axis=-1, keepdims=True))
            dy = jnp.where(j < 2 * H, dy_n, do)
            pad2_ref[pl.ds(r0, RC), :] = dy * _dsilu(xc)
        for r0 in range(0, S, RC):
            dx_ref[pl.ds(r0, RC), :] = _conv_bwd_rows(pad2_ref, w_ref, DNK, r0).astype(BF16)
        _conv_dw(pad_ref, pad2_ref, dw_ref, DNK, head)

    wspec = pl.BlockSpec((DNK, DH), lambda j: (0, j))
    return _call(body, "dn_pre_bwd", [_sds((S, 3 * DNW), BF16), _sds((DNK, 3 * DNW))], grid=(3 * H,),
                 in_specs=[_col_spec(DH), _col_spec(DH), wspec], out_specs=[_col_spec(DH), wspec],
                 scratch=[pltpu.VMEM((S + head, DH), F32), pltpu.VMEM((S + head, DH), F32)],
                 sem=("parallel",))(dqkv, proj, conv_w)


CF_HEAD = 32
CF_VAL0 = O_GLU // LANE
CF_GL0 = (O_GLU + CFW) // LANE


def _cf_conv_fwd(proj, conv_w):
    def body(val_ref, gl_ref, w_ref, o_ref, pad_ref):
        pad_ref[pl.ds(0, CF_HEAD), :] = jnp.zeros((CF_HEAD, LANE), F32)
        pad_ref[pl.ds(CF_HEAD, S), :] = val_ref[...] * _sigmoid(gl_ref[...])
        for r0 in range(0, S, RC):
            o_ref[pl.ds(r0, RC), :] = _conv_fwd_rows(pad_ref, w_ref, CFK, CF_HEAD, r0)

    wspec = pl.BlockSpec((CFK, LANE), lambda j: (0, j))
    return _call(body, "cf_conv_fwd", _sds((S, CFW)), grid=(CFW // LANE,),
                 in_specs=[_col_spec(LANE, CF_VAL0), _col_spec(LANE, CF_GL0), wspec], out_specs=_col_spec(LANE),
                 scratch=[pltpu.VMEM((S + CF_HEAD, LANE), F32)], sem=("parallel",))(proj, proj, conv_w)


def _cf_conv_bwd(du1, proj, conv_w):
    def body(d_ref, val_ref, gl_ref, w_ref, dval_ref, dgl_ref, dw_ref, pad_ref, pad2_ref):
        sg = _sigmoid(gl_ref[...])
        pad_ref[pl.ds(0, CF_HEAD), :] = jnp.zeros((CF_HEAD, LANE), F32)
        pad_ref[pl.ds(CF_HEAD, S), :] = val_ref[...] * sg
        pad2_ref[pl.ds(0, S), :] = d_ref[...]
        pad2_ref[pl.ds(S, CF_HEAD), :] = jnp.zeros((CF_HEAD, LANE), F32)
        for r0 in range(0, S, RC):
            du0 = _conv_bwd_rows(pad2_ref, w_ref, CFK, r0)
            rows = pl.ds(r0, RC)
            sgr = _sigmoid(gl_ref[rows, :])
            dval_ref[rows, :] = (du0 * sgr).astype(BF16)
            dgl_ref[rows, :] = (du0 * val_ref[rows, :] * sgr * (1.0 - sgr)).astype(BF16)
        _conv_dw(pad_ref, pad2_ref, dw_ref, CFK, CF_HEAD)

    wspec = pl.BlockSpec((CFK, LANE), lambda j: (0, j))
    return _call(body, "cf_conv_bwd", [_sds((S, CFW), BF16), _sds((S, CFW), BF16), _sds((CFK, CFW))],
                 grid=(CFW // LANE,),
                 in_specs=[_col_spec(LANE), _col_spec(LANE, CF_VAL0), _col_spec(LANE, CF_GL0), wspec],
                 out_specs=[_col_spec(LANE), _col_spec(LANE), wspec],
                 scratch=[pltpu.VMEM((S + CF_HEAD, LANE), F32), pltpu.VMEM((S + CF_HEAD, LANE), F32)],
                 sem=("parallel",))(du1, proj, proj, conv_w)


def _cf_ln_fwd(u1, g, b):
    def body(u_ref, g_ref, b_ref, o_ref):
        u = u_ref[...]
        mu = jnp.mean(u, axis=-1, keepdims=True)
        xc = u - mu
        y = xc * lax.rsqrt(jnp.mean(xc * xc, axis=-1, keepdims=True) + EPS)
        o_ref[...] = _silu(y * g_ref[...] + b_ref[...]).astype(BF16)

    return _call(body, "cf_ln_fwd", _sds((S, CFW), BF16), grid=(S // TS,),
                 in_specs=[_row_spec(CFW), _vec_spec(CFW), _vec_spec(CFW)], out_specs=_row_spec(CFW),
                 sem=("parallel",))(u1, g, b)


def _cf_ln_bwd(du3, u1, g, b):
    def body(d_ref, u_ref, g_ref, b_ref, du_ref, acc_ref):
        u = u_ref[...]
        mu = jnp.mean(u, axis=-1, keepdims=True)
        xc = u - mu
        rstd = lax.rsqrt(jnp.mean(xc * xc, axis=-1, keepdims=True) + EPS)
        xh = xc * rstd
        du2 = d_ref[...] * _dsilu(xh * g_ref[...] + b_ref[...])
        dxh = du2 * g_ref[...]
        du_ref[...] = rstd * (dxh - jnp.mean(dxh, axis=-1, keepdims=True)
                              - xh * jnp.mean(dxh * xh, axis=-1, keepdims=True))
        _acc_rows(acc_ref, [jnp.sum(du2 * xh, axis=0, keepdims=True), jnp.sum(du2, axis=0, keepdims=True)])

    return _call(body, "cf_ln_bwd", [_sds((S, CFW)), _sds((8, CFW))], grid=(S // TS,),
                 in_specs=[_row_spec(CFW), _row_spec(CFW), _vec_spec(CFW), _vec_spec(CFW)],
                 out_specs=[_row_spec(CFW), _acc_spec(CFW)], sem=("arbitrary",))(du3, u1, g, b)


FB = 256
FNB = FFN // FB
FF_HEAD = 8


def _ffn_mid_fwd(upall, conv_w):
    def body(gate_ref, up_ref, w_ref, o_ref, pad_ref):
        pad_ref[pl.ds(0, FF_HEAD), :] = jnp.zeros((FF_HEAD, FB), F32)
        pad_ref[pl.ds(FF_HEAD, S), :] = gate_ref[...]
        for r0 in range(0, S, RC):
            gc = _conv_fwd_rows(pad_ref, w_ref, FFK, FF_HEAD, r0)
            o_ref[pl.ds(r0, RC), :] = (_silu(gc) * up_ref[pl.ds(r0, RC), :]).astype(BF16)

    wspec = pl.BlockSpec((FFK, FB), lambda j: (0, j))
    return _call(body, "ffn_mid_fwd", _sds((S, FFN), BF16), grid=(FNB,),
                 in_specs=[_col_spec(FB), _col_spec(FB, FNB), wspec], out_specs=_col_spec(FB),
                 scratch=[pltpu.VMEM((S + FF_HEAD, FB), F32)], sem=("parallel",))(upall, upall, conv_w)


def _ffn_mid_bwd(dh, upall, conv_w):
    def body(d_ref, gate_ref, up_ref, w_ref, dgate_ref, dup_ref, dw_ref, pad_ref, pad2_ref):
        pad_ref[pl.ds(0, FF_HEAD), :] = jnp.zeros((FF_HEAD, FB), F32)
        pad_ref[pl.ds(FF_HEAD, S), :] = gate_ref[...]
        pad2_ref[pl.ds(S, FF_HEAD), :] = jnp.zeros((FF_HEAD, FB), F32)
        for r0 in range(0, S, RC):
            rows = pl.ds(r0, RC)
            gc = _conv_fwd_rows(pad_ref, w_ref, FFK, FF_HEAD, r0)
            dhv = d_ref[rows, :]
            dup_ref[rows, :] = (dhv * _silu(gc)).astype(BF16)
            pad2_ref[rows, :] = dhv * up_ref[rows, :] * _dsilu(gc)
        for r0 in range(0, S, RC):
            dgate_ref[pl.ds(r0, RC), :] = _conv_bwd_rows(pad2_ref, w_ref, FFK, r0).astype(BF16)
        _conv_dw(pad_ref, pad2_ref, dw_ref, FFK, FF_HEAD)

    wspec = pl.BlockSpec((FFK, FB), lambda j: (0, j))
    return _call(body, "ffn_mid_bwd", [_sds((S, FFN), BF16), _sds((S, FFN), BF16), _sds((FFK, FFN))],
                 grid=(FNB,), in_specs=[_col_spec(FB), _col_spec(FB), _col_spec(FB, FNB), wspec],
                 out_specs=[_col_spec(FB), _col_spec(FB), wspec],
                 scratch=[pltpu.VMEM((S + FF_HEAD, FB), F32), pltpu.VMEM((S + FF_HEAD, FB), F32)],
                 sem=("parallel",))(dh, upall, upall, conv_w)


GT = 256
SM_BLK = O_SM // LANE


def _chunk_tri(lower):
    r = lax.broadcasted_iota(jnp.int32, (GT, GT), 0)
    c = lax.broadcasted_iota(jnp.int32, (GT, GT), 1)
    same = (r // CH) == (c // CH)
    tri = (c <= r) if lower else (c >= r)
    return jnp.where(same & tri, 1.0, 0.0).astype(F32)


def _gates_fwd(proj, alog_v, dtb_v):
    def body(sm_ref, al_ref, dt_ref, o_ref):
        lane = lax.broadcasted_iota(jnp.int32, (GT, LANE), 1)
        tri = _chunk_tri(True)
        na = -jnp.exp(al_ref[...])
        for r0 in range(0, S, GT):
            sm = sm_ref[pl.ds(r0, GT), :]
            raw = jnp.where((lane >= H) & (lane < 2 * H), na * _softplus(sm + dt_ref[...]), 0.0)
            gc = _dot(tri, raw, NN, precision=HI)
            o_ref[pl.ds(r0, GT), :] = jnp.where(lane < H, _sigmoid(sm), gc)

    return _call(body, "gates_fwd", _sds((S, LANE)), grid=(1,),
                 in_specs=[pl.BlockSpec((S, LANE), lambda i: (0, SM_BLK)), _vec_spec(LANE), _vec_spec(LANE)],
                 out_specs=pl.BlockSpec((S, LANE), lambda i: (0, 0)), sem=("arbitrary",))(proj, alog_v, dtb_v)


def _gates_bwd(dgb, proj, alog_v, dtb_v):
    def body(d_ref, sm_ref, al_ref, dt_ref, o_ref, acc_ref):
        lane = lax.broadcasted_iota(jnp.int32, (GT, LANE), 1)
        is_g = (lane >= H) & (lane < 2 * H)
        tri = _chunk_tri(False)
        na = -jnp.exp(al_ref[...])
        d_al = jnp.zeros((1, LANE), F32)
        d_dt = jnp.zeros((1, LANE), F32)
        for r0 in range(0, S, GT):
            sm = sm_ref[pl.ds(r0, GT), :]
            dv = d_ref[pl.ds(r0, GT), :]
            z = sm + dt_ref[...]
            draw = _dot(tri, jnp.where(is_g, dv, 0.0), NN, precision=HI)
            dlogit = jnp.where(is_g, draw * na * _sigmoid(z), 0.0)
            d_al = d_al + jnp.sum(jnp.where(is_g, draw * na * _softplus(z), 0.0), axis=0, keepdims=True)
            d_dt = d_dt + jnp.sum(dlogit, axis=0, keepdims=True)
            bt = _sigmoid(sm)
            o_ref[pl.ds(r0, GT), :] = jnp.where(lane < H, dv * bt * (1.0 - bt), dlogit).astype(BF16)
        acc_ref[...] = jnp.zeros_like(acc_ref)
        acc_ref[0:1, :] = d_al
        acc_ref[1:2, :] = d_dt

    return _call(body, "gates_bwd", [_sds((S, LANE), BF16), _sds((8, LANE))], grid=(1,),
                 in_specs=[pl.BlockSpec((S, LANE), lambda i: (0, 0)), pl.BlockSpec((S, LANE), lambda i: (0, SM_BLK)),
                           _vec_spec(LANE), _vec_spec(LANE)],
                 out_specs=[pl.BlockSpec((S, LANE), lambda i: (0, 0)), _acc_spec(LANE)],
                 sem=("arbitrary",))(dgb, proj, alog_v, dtb_v)


def _neumann_inv(a, eye):
    x = -a
    t = eye + x
    p = x
    for _ in range(5):
        p = _dot(p, p, NN, precision=HI)
        t = t + _dot(t, p, NN, precision=HI)
    return t


def _head_specs():
    q = pl.BlockSpec((S, DH), lambda h: (0, h))
    k = pl.BlockSpec((S, DH), lambda h: (0, H + h))
    v = pl.BlockSpec((S, DH), lambda h: (0, 2 * H + h))
    gb = pl.BlockSpec((None, S, DH), lambda h: (h, 0, 0))
    gr = pl.BlockSpec((None, NCH, CH), lambda h: (h, 0, 0))
    return q, k, v, gb, gr


ST_SPEC = pl.BlockSpec((None, NCH, DH, DH), lambda h: (h, 0, 0, 0))
TM_SPEC = pl.BlockSpec((None, NCH, CH, CH), lambda h: (h, 0, 0, 0))


def _delta_fwd(qkvn, gb, gr, bb):
    def body(q_ref, k_ref, v_ref, gb_ref, gr_ref, bb_ref, o_ref, st_ref, tm_ref):
        ri = lax.broadcasted_iota(jnp.int32, (CH, CH), 0)
        ci = lax.broadcasted_iota(jnp.int32, (CH, CH), 1)
        strict = ri > ci
        causal = ri >= ci
        eye = jnp.where(ri == ci, 1.0, 0.0).astype(F32)

        def step(n, st):
            rows = pl.ds(pl.multiple_of(n * CH, CH), CH)
            q, k, v, g, beta = q_ref[rows, :], k_ref[rows, :], v_ref[rows, :], gb_ref[rows, :], bb_ref[rows, :]
            diff = g[:, :CH] - gr_ref[pl.ds(n, 1), :]
            el = jnp.exp(jnp.where(causal, diff, 0.0))
            eg = jnp.exp(g)
            gl = g[CH - 1:CH, :]
            kb = k * beta
            kbf = k.astype(BF16)
            a = jnp.where(strict, _dot(kb.astype(BF16), kbf, NT) * el, 0.0)
            t = _neumann_inv(a, eye)
            tm_ref[n] = t
            st_ref[n] = st
            sb = st.astype(BF16)
            r = v * beta - _dot((kb * eg).astype(BF16), sb, NN)
            ub = _dot(t, r, NN, precision=HI).astype(BF16)
            p = jnp.where(causal, _dot(q.astype(BF16), kbf, NT) * el, 0.0)
            o_ref[rows, :] = _dot((q * eg).astype(BF16), sb, NN) + _dot(p.astype(BF16), ub, NN)
            kd = k * jnp.exp(gl - g)
            return st * jnp.exp(gl) + _dot(kd.astype(BF16), ub, TN)

        lax.fori_loop(0, NCH, step, jnp.zeros((DH, DH), F32))

    q, k, v, gbs, grs = _head_specs()
    return _call(body, "delta_fwd", [_sds((S, DNW)), _sds((H, NCH, DH, DH)), _sds((H, NCH, CH, CH))], grid=(H,),
                 in_specs=[q, k, v, gbs, grs, gbs], out_specs=[pl.BlockSpec((S, DH), lambda h: (0, h)), ST_SPEC, TM_SPEC],
                 sem=("parallel",))(qkvn, qkvn, qkvn, gb, gr, bb)


def _delta_bwd(qkvn, gb, gr, bb, st_all, tm_all, do_all):
    def body(q_ref, k_ref, v_ref, gb_ref, gr_ref, bb_ref, st_ref, tm_ref, do_ref,
             dq_ref, dk_ref, dv_ref, dg_ref, db_ref):
        ri = lax.broadcasted_iota(jnp.int32, (CH, CH), 0)
        ci = lax.broadcasted_iota(jnp.int32, (CH, CH), 1)
        lo_s, lo_c, up_s, up_c = ri > ci, ri >= ci, ri < ci, ri <= ci
        last_row = lax.broadcasted_iota(jnp.int32, (CH, 1), 0) == CH - 1

        def rs(mat):
            return jnp.sum(mat, axis=1, keepdims=True)

        def total(mat):
            return jnp.sum(rs(mat), axis=0, keepdims=True)

        def step(i, ds):
            n = NCH - 1 - i
            rows = pl.ds(pl.multiple_of(n * CH, CH), CH)
            q, k, v, g, beta = q_ref[rows, :], k_ref[rows, :], v_ref[rows, :], gb_ref[rows, :], bb_ref[rows, :]
            do = do_ref[rows, :]
            t = tm_ref[n]
            st = st_ref[n]
            diff = g[:, :CH] - gr_ref[pl.ds(n, 1), :]
            el = jnp.exp(jnp.where(lo_c, diff, 0.0))
            eu = jnp.exp(jnp.where(up_c, -diff, 0.0))
            eg = jnp.exp(g)
            gl = g[CH - 1:CH, :]
            egl = jnp.exp(gl)
            ekd = jnp.exp(gl - g)
            kb = k * beta
            kbg = kb * eg
            qg = q * eg
            kd = k * ekd
            qb, kbf, kbb = q.astype(BF16), k.astype(BF16), kb.astype(BF16)
            kbgb, qgb, kdb = kbg.astype(BF16), qg.astype(BF16), kd.astype(BF16)
            sb, dob, dsb = st.astype(BF16), do.astype(BF16), ds.astype(BF16)
            r = v * beta - _dot(kbgb, sb, NN)
            u = _dot(t, r, NN, precision=HI)
            ub = u.astype(BF16)
            kk, qk = _dot(kbb, kbf, NT), _dot(qb, kbf, NT)
            kkt, qkt = _dot(kbf, kbb, NT), _dot(kbf, qb, NT)
            pt = jnp.where(up_c, qkt * eu, 0.0)
            du = _dot(pt.astype(BF16), dob, NN) + _dot(kdb, dsb, NN)
            dr = _dot(t, du, TN, precision=HI)
            drb = dr.astype(BF16)
            dpg = jnp.where(lo_c, _dot(dob, ub, NT), 0.0) * el
            dpgt = jnp.where(up_c, _dot(ub, dob, NT), 0.0) * eu
            dag = -jnp.where(lo_s, _dot(drb, ub, NT), 0.0) * el
            dagt = -jnp.where(up_s, _dot(ub, drb, NT), 0.0) * eu
            dqg = _dot(dob, sb, NT)
            dkbg = -_dot(drb, sb, NT)
            dkd = _dot(ub, dsb, NT)
            ds_new = _dot(qgb, dob, TN) + egl * ds - _dot(kbgb, drb, TN)
            dkb = _dot(dag.astype(BF16), kbf, NN) + dkbg * eg
            dk = (_dot(dagt.astype(BF16), kbb, NN) + _dot(dpgt.astype(BF16), qb, NN) + dkd * ekd + dkb * beta)
            dq = _dot(dpg.astype(BF16), kbf, NN) + dqg * eg
            dkd_kd = rs(dkd * kd)
            dg = (rs(dag * kk + dpg * qk) - rs(dagt * kkt + dpgt * qkt) + rs(dqg * qg) + rs(dkbg * kbg) - dkd_kd)
            dgl = jnp.sum(dkd_kd, axis=0, keepdims=True) + egl[:, 0:1] * total(ds * st)
            dg = dg + jnp.where(last_row, dgl, 0.0)
            dbeta = rs(dkb * k) + rs(dr * v)
            dq_ref[rows, :] = dq
            dk_ref[rows, :] = dk
            dv_ref[rows, :] = dr * beta
            dg_ref[rows, :] = jnp.broadcast_to(dg, (CH, DH))
            db_ref[rows, :] = jnp.broadcast_to(dbeta, (CH, DH))
            return ds_new

        lax.fori_loop(0, NCH, step, jnp.zeros((DH, DH), F32))

    q, k, v, gbs, grs = _head_specs()
    hcol = pl.BlockSpec((S, DH), lambda h: (0, h))
    return _call(body, "delta_bwd",
                 [_sds((S, DNW)), _sds((S, DNW)), _sds((S, DNW)), _sds((H, S, DH)), _sds((H, S, DH))], grid=(H,),
                 in_specs=[q, k, v, gbs, grs, gbs, ST_SPEC, TM_SPEC, hcol], out_specs=[hcol, hcol, hcol, gbs, gbs],
                 sem=("parallel",))(qkvn, qkvn, qkvn, gb, gr, bb, st_all, tm_all, do_all)


Z_BLK = O_Z // DNW


def _dn_post_fwd(o, proj, gn):
    def body(o_ref, z_ref, gn_ref, og_ref):
        for h in range(H):
            cols = slice(h * DH, (h + 1) * DH)
            ov = o_ref[:, cols]
            on = ov * lax.rsqrt(jnp.mean(ov * ov, axis=-1, keepdims=True) + EPS) * gn_ref[...]
            og_ref[:, cols] = (on * _silu(z_ref[:, cols])).astype(BF16)

    return _call(body, "dn_post_fwd", _sds((S, DNW), BF16), grid=(S // TS,),
                 in_specs=[_row_spec(DNW), pl.BlockSpec((TS, DNW), lambda i: (i, Z_BLK)), _vec_spec(DH)],
                 out_specs=_row_spec(DNW), sem=("parallel",))(o, proj, gn)


def _dn_post_bwd(dog, o, proj, gn):
    def body(d_ref, o_ref, z_ref, gn_ref, do_ref, dz_ref, acc_ref):
        dgn = jnp.zeros((1, DH), F32)
        for h in range(H):
            cols = slice(h * DH, (h + 1) * DH)
            ov, zv, dv = o_ref[:, cols], z_ref[:, cols], d_ref[:, cols]
            rinv = lax.rsqrt(jnp.mean(ov * ov, axis=-1, keepdims=True) + EPS)
            xn = ov * rinv
            don = dv * _silu(zv)
            dz_ref[:, cols] = (dv * xn * gn_ref[...] * _dsilu(zv)).astype(BF16)
            dgn = dgn + jnp.sum(don * xn, axis=0, keepdims=True)
            dxn = don * gn_ref[...]
            do_ref[:, cols] = rinv * (dxn - xn * jnp.mean(dxn * xn, axis=-1, keepdims=True))
        _acc_rows(acc_ref, [dgn])

    return _call(body, "dn_post_bwd", [_sds((S, DNW)), _sds((S, DNW), BF16), _sds((8, DH))], grid=(S // TS,),
                 in_specs=[_row_spec(DNW), _row_spec(DNW), pl.BlockSpec((TS, DNW), lambda i: (i, Z_BLK)), _vec_spec(DH)],
                 out_specs=[_row_spec(DNW), _row_spec(DNW), _acc_spec(DH)], sem=("arbitrary",))(dog, o, proj, gn)


GA_BLK = O_GA // D
GB_BLK = O_GB // D


def _merge_fwd(ba, bb, proj):
    def body(a_ref, b_ref, ga_ref, gb_ref, o_ref):
        o_ref[...] = (_sigmoid(ga_ref[...]) * a_ref[...] + _sigmoid(gb_ref[...]) * b_ref[...]).astype(BF16)

    return _call(body, "merge_fwd", _sds((S, D), BF16), grid=(S // TS,),
                 in_specs=[_row_spec(), _row_spec(), pl.BlockSpec((TS, D), lambda i: (i, GA_BLK)),
                           pl.BlockSpec((TS, D), lambda i: (i, GB_BLK))],
                 out_specs=_row_spec(), sem=("parallel",))(ba, bb, proj, proj)


def _merge_bwd(dm, ba, bb, proj):
    def body(d_ref, a_ref, b_ref, ga_ref, gb_ref, dga_ref, dgb_ref, da_ref, db_ref):
        d = d_ref[...]
        sa, sb = _sigmoid(ga_ref[...]), _sigmoid(gb_ref[...])
        dga_ref[...] = (d * a_ref[...] * sa * (1.0 - sa)).astype(BF16)
        dgb_ref[...] = (d * b_ref[...] * sb * (1.0 - sb)).astype(BF16)
        da_ref[...] = (d * sa).astype(BF16)
        db_ref[...] = (d * sb).astype(BF16)

    return _call(body, "merge_bwd", [_sds((S, D), BF16)] * 4, grid=(S // TS,),
                 in_specs=[_row_spec(), _row_spec(), _row_spec(), pl.BlockSpec((TS, D), lambda i: (i, GA_BLK)),
                           pl.BlockSpec((TS, D), lambda i: (i, GB_BLK))],
                 out_specs=[_row_spec()] * 4, sem=("parallel",))(dm, ba, bb, proj, proj)


def _pad_win(w):
    return jnp.concatenate([w[:, :4096], w[:, 4112:], w[:, 4096:4112], jnp.zeros((w.shape[0], NINP - NIN), w.dtype)], axis=1)


def _unpad_win(w):
    return jnp.concatenate([w[:, :4096], w[:, O_SM:O_SM + 16], w[:, 4096:O_SM]], axis=1)


def _lane_vec(v8, offset):
    return jnp.pad(v8, ((0, 0), (offset, LANE - 8 - offset)))


def _tie(vec, token):
    return vec + token


def _local_step(x, tgt, mod, norm1_g, norm2_g, final_g, w_in_p, dn_conv_w, a_log, dt_bias, dn_norm_g,
                cf_conv_w, cf_ln_g, cf_ln_b, ffn_conv_w, comm):
    sh1, sc1, gt1, sh2, sc2, gt2 = (mod[:, i * D:(i + 1) * D] for i in range(6))
    alog_v, dtb_v = _lane_vec(a_log, H), _lane_vec(dt_bias, H)

    hn1 = _norm_mod(x, norm1_g, sc1, _tie(sh1, comm.token0), "norm_mod1")
    proj = _mm(hn1, w_in_p, "nn", F32, "mm_in", tn=1152)
    qkvn = _dn_pre_fwd(proj, dn_conv_w)
    gates = _gates_fwd(proj, alog_v, dtb_v)
    beta_t = gates[:, 0:H].T
    g_t = gates[:, H:2 * H].T
    gb = jnp.broadcast_to(g_t[:, :, None], (H, S, DH))
    bb = jnp.broadcast_to(beta_t[:, :, None], (H, S, DH))
    gr = g_t.reshape(H, NCH, CH)
    o, st_all, tm_all = _delta_fwd(qkvn, gb, gr, bb)
    og = _dn_post_fwd(o, proj, dn_norm_g)
    u1 = _cf_conv_fwd(proj, cf_conv_w)
    u3 = _cf_ln_fwd(u1, cf_ln_g, cf_ln_b)
    after = og[0:8, 0:LANE].astype(F32) + u3[0:8, 0:LANE].astype(F32)
    dn_w_o, cf_w_o, w_out, ffn_w_up, ffn_w_down = comm.late_weights(after)
    br_a = _mm(og, dn_w_o, "nn", F32, "mm_dn_o")
    br_b = _mm(u3, cf_w_o, "nn", F32, "mm_cf_o")
    merged = _merge_fwd(br_a, br_b, proj)
    mix = _mm(merged, w_out, "nn", F32, "mm_out")
    x2, hn2 = _resid_norm_mod(x, mix, gt1, norm2_g, sc2, sh2, "resid_norm_mod2")
    upall = _mm(hn2, ffn_w_up, "nn", F32, "mm_up")
    hmid = _ffn_mid_fwd(upall, ffn_conv_w)
    f = _mm(hmid, ffn_w_down, "nn", F32, "mm_down")

    dx3, df, acc_f = _loss_head(x2, f, tgt, gt2, final_g)
    d_final_g, d_gt2, loss = acc_f[0:1], acc_f[1:2], acc_f[2:3, 0:1]
    dhmid = _mm(df, ffn_w_down, "nt", F32, "mm_down_dx")
    g_w_down = _mm(hmid, df, "tn", BF16, "mm_down_dw")
    d_gate, d_up, g_ffn_conv = _ffn_mid_bwd(dhmid, upall, ffn_conv_w)
    d_upall = jnp.concatenate([d_gate, d_up], axis=1)
    dhn2 = _mm(d_upall, ffn_w_up, "nt", F32, "mm_up_dx")
    g_w_up = _mm(hn2, d_upall, "tn", BF16, "mm_up_dw")
    tok_a = comm.grads("a", dict(ffn_w_down=g_w_down, ffn_w_up=g_w_up))
    dx2, dmix, acc2 = _norm_mod_bwd(dhn2, x2, dx3, _tie(norm2_g, tok_a), sc2, "norm_mod2_bwd", mix=mix, gt=gt1)
    d_sh2, d_sc2, d_norm2_g, d_gt1 = acc2[0:1], acc2[1:2], acc2[2:3], acc2[3:4]
    dmerged = _mm(dmix, w_out, "nt", F32, "mm_out_dx")
    g_w_out = _mm(merged, dmix, "tn", BF16, "mm_out_dw")
    d_ga, d_gb, d_bra, d_brb = _merge_bwd(dmerged, br_a, br_b, proj)
    du3 = _mm(d_brb, cf_w_o, "nt", F32, "mm_cf_o_dx")
    g_cf_w_o = _mm(u3, d_brb, "tn", BF16, "mm_cf_o_dw")
    du1, acc_ln = _cf_ln_bwd(du3, u1, cf_ln_g, cf_ln_b)
    d_val, d_gl, g_cf_conv = _cf_conv_bwd(du1, proj, cf_conv_w)
    dog = _mm(d_bra, dn_w_o, "nt", F32, "mm_dn_o_dx")
    g_dn_w_o = _mm(og, d_bra, "tn", BF16, "mm_dn_o_dw")
    tok_b = comm.grads("b", dict(w_out=g_w_out, cf_w_o=g_cf_w_o, dn_w_o=g_dn_w_o, ffn_conv_w=g_ffn_conv,
                                 cf_conv_w=g_cf_conv))
    do, dz, acc_gn = _dn_post_bwd(dog, o, proj, _tie(dn_norm_g, tok_b))
    dq, dk, dv, dgb, dbb = _delta_bwd(qkvn, gb, gr, bb, st_all, tm_all, do)
    d_pre, g_dn_conv = _dn_pre_bwd(jnp.concatenate([dq, dk, dv], axis=1), proj, dn_conv_w)
    dgates = jnp.concatenate([dbb[:, :, 0].T, dgb[:, :, 0].T, jnp.zeros((S, LANE - 2 * H), F32)], axis=1)
    d_sm, acc_g = _gates_bwd(dgates, proj, alog_v, dtb_v)
    d_proj = jnp.concatenate([d_pre, dz, d_val, d_gl, d_ga, d_gb, d_sm], axis=1)
    dhn1 = _mm(d_proj, w_in_p, "nt", F32, "mm_in_dx", tk=1152)
    g_w_in_p = _mm(hn1, d_proj, "tn", BF16, "mm_in_dw", tn=1152)
    tok_c = comm.grads("c", dict(w_in=g_w_in_p, dn_conv_w=g_dn_conv))
    grad_x, acc1 = _norm_mod_bwd(dhn1, x, dx2, _tie(norm1_g, tok_c), sc1, "norm_mod1_bwd")
    d_sh1, d_sc1, d_norm1_g = acc1[0:1], acc1[1:2], acc1[2:3]

    d_mod = jnp.concatenate([d_sh1, d_sc1, d_gt1, d_sh2, d_sc2, d_gt2], axis=1)
    small = dict(mod=d_mod, norm1_g=d_norm1_g, norm2_g=d_norm2_g, final_norm_g=d_final_g,
                 cf_ln_g=acc_ln[0:1], cf_ln_b=acc_ln[1:2], dn_norm_g=acc_gn[0:1],
                 dn_a_log=acc_g[0:1, H:2 * H], dn_dt_bias=acc_g[1:2, H:2 * H])
    return loss, grad_x, small


def _dev_index(px, py, pc):
    return 4 * px + 2 * py + pc


def _all_gather(arrs, name):
    n = len(arrs)

    def body(*refs):
        ins, outs = refs[:n], refs[n:2 * n]
        send_sems, recv_sems, loc_sems = refs[2 * n:]
        x, y, c = _my_pos()
        me, sib = (x, y, c), (x, y, 1 - c)
        chips = [(1 - x, y), (x, 1 - y), (1 - x, 1 - y)]

        def cp(i, k, block, to, src=None):
            dst = outs[i].at[_dev_index(*block)]
            return pltpu.make_async_remote_copy(
                src_ref=dst if src is None else src, dst_ref=dst, send_sem=send_sems.at[i, k],
                recv_sem=recv_sems.at[i, k], device_id=to, device_id_type=MESH)

        mine = [pltpu.make_async_copy(ins[i], outs[i].at[_dev_index(*me)], loc_sems.at[i]) for i in range(n)]
        for m in mine:
            m.start()
        sent = []
        for i in range(n):
            sent.append(cp(i, 0, me, sib, src=ins[i]))
            sent += [cp(i, 1 + j, me, (*chip, c), src=ins[i]) for j, chip in enumerate(chips)]
        for s in sent:
            s.start()
        for i in range(n):
            for j, chip in enumerate(chips):
                cp(i, 1 + j, (*chip, c), me).wait_recv()
                fwd = cp(i, 4 + j, (*chip, c), sib)
                fwd.start()
                sent.append(fwd)
        for i in range(n):
            cp(i, 0, sib, me).wait_recv()
            for j, chip in enumerate(chips):
                cp(i, 4 + j, (*chip, 1 - c), me).wait_recv()
        for s in sent:
            s.wait_send()
        for m in mine:
            m.wait()

    outs = pl.pallas_call(
        body, out_shape=[_sds((NDEV,) + a.shape, a.dtype) for a in arrs], in_specs=[ANY] * n, out_specs=[ANY] * n,
        scratch_shapes=[pltpu.SemaphoreType.DMA((n, 7)), pltpu.SemaphoreType.DMA((n, 7)), pltpu.SemaphoreType.DMA((n,))],
        name=name)(*arrs)
    return list(outs)


def _slab(ref, layout, idx):
    kind, n = layout
    if kind == "rows":
        return ref.at[pl.ds(pl.multiple_of(idx * n, n), n), :]
    if kind == "cols":
        return ref.at[:, pl.ds(pl.multiple_of(idx * n, n), n)]
    return ref.at[idx]


def _slab_shape(arr, layout):
    kind, n = layout
    if kind == "rows":
        return (n, arr.shape[1])
    if kind == "cols":
        return (arr.shape[0], n)
    return tuple(arr.shape[1:])


def _pair_exchange(parts, layouts, name):
    n = len(parts)

    def body(*refs):
        ins, outs = refs[:n], refs[n:2 * n]
        send_sems, recv_sems = refs[2 * n:]
        x, y, c = _my_pos()
        copies = []
        for i in range(n):
            for q in range(4):
                copies.append(pltpu.make_async_remote_copy(
                    src_ref=_slab(ins[i], layouts[i], 2 * q + (1 - c)), dst_ref=outs[i].at[q],
                    send_sem=send_sems.at[i, q], recv_sem=recv_sems.at[i, q], device_id=(x, y, 1 - c),
                    device_id_type=MESH))
        for cpy in copies:
            cpy.start()
        for cpy in copies:
            cpy.wait()

    outs = pl.pallas_call(
        body, out_shape=[_sds((4,) + _slab_shape(p, lay), p.dtype) for p, lay in zip(parts, layouts)],
        in_specs=[ANY] * n, out_specs=[ANY] * n,
        scratch_shapes=[pltpu.SemaphoreType.DMA((n, 4)), pltpu.SemaphoreType.DMA((n, 4))], name=name)(*parts)
    return list(outs)


HBM = pl.BlockSpec(memory_space=pltpu.HBM)
SEMS = pl.BlockSpec(memory_space=pltpu.SEMAPHORE)
EFFECT = pltpu.SideEffectType.DATAFLOW_SIDE_EFFECTING
TOKEN = jax.ShapeDtypeStruct((8, LANE), F32)


def _hbm(a):
    return pltpu.with_memory_space_constraint(a, pltpu.HBM)


def _gather_ici_copy(shard_ref, buf_ref, layout, send_sems, recv_sems, i, j, me, chip, c):
    return pltpu.make_async_remote_copy(
        src_ref=shard_ref, dst_ref=_slab(buf_ref, layout, me), send_sem=send_sems.at[3 * i + j],
        recv_sem=recv_sems.at[3 * i + j], device_id=(*chip, c), device_id_type=MESH)


def _gather_ici_start(shards, bufs, layouts, after, name):
    n = len(shards)

    def body(*refs):
        sh, bf = refs[:n], refs[n:2 * n]
        send_sems, recv_sems = refs[2 * n + 1], refs[2 * n + 2]
        token = refs[-1]
        x, y, c = _my_pos()
        me = _dev_index(x, y, c)
        for i in range(n):
            for j, chip in enumerate([(1 - x, y), (x, 1 - y), (1 - x, 1 - y)]):
                _gather_ici_copy(sh[i], bf[i], layouts[i], send_sems, recv_sems, i, j, me, chip, c).start()
        token[...] = jnp.zeros_like(token)

    outs = pl.pallas_call(
        body, name=name,
        out_shape=(pltpu.SemaphoreType.DMA((3 * n,)), pltpu.SemaphoreType.DMA((3 * n,)),
                   *[pltpu.HBM(a.shape, a.dtype) for a in shards], *[pltpu.HBM(a.shape, a.dtype) for a in bufs], TOKEN),
        in_specs=[HBM] * (2 * n) + [ANY],
        out_specs=(SEMS, SEMS, *[HBM] * (2 * n), pl.BlockSpec(memory_space=pltpu.VMEM)),
        input_output_aliases={i: 2 + i for i in range(2 * n)},
        compiler_params=pltpu.CompilerParams(has_side_effects=EFFECT),
    )(*[_hbm(a) for a in shards], *[_hbm(a) for a in bufs], after)
    return outs[0], outs[1], list(outs[2:2 + n]), list(outs[2 + n:2 + 2 * n]), outs[-1]


def _gather_ici_wait(send_sems, recv_sems, shards, bufs, layouts, after, name):
    n = len(shards)

    def body(*refs):
        sh, bf = refs[:n], refs[n:2 * n]
        ssem, rsem = refs[2 * n], refs[2 * n + 1]
        x, y, c = _my_pos()
        me = _dev_index(x, y, c)
        for i in range(n):
            for j, chip in enumerate([(1 - x, y), (x, 1 - y), (1 - x, 1 - y)]):
                cp = _gather_ici_copy(sh[i], bf[i], layouts[i], ssem, rsem, i, j, me, chip, c)
                cp.wait_send()
                cp.wait_recv()

    outs = pl.pallas_call(
        body, name=name,
        out_shape=(*[pltpu.HBM(a.shape, a.dtype) for a in shards], *[pltpu.HBM(a.shape, a.dtype) for a in bufs]),
        in_specs=[HBM] * (2 * n) + [SEMS, SEMS, ANY], out_specs=tuple([HBM] * (2 * n)),
        input_output_aliases={i: i for i in range(2 * n)},
        compiler_params=pltpu.CompilerParams(has_side_effects=EFFECT),
    )(*shards, *bufs, send_sems, recv_sems, after)
    return list(outs[:n]), list(outs[n:])


def _gather_pair(shards, bufs, layouts, name):
    n = len(shards)

    def body(*refs):
        sh, bo = refs[:n], refs[2 * n:3 * n]
        send_sems, recv_sems, loc_sems = refs[3 * n:]
        x, y, c = _my_pos()
        me, sib = _dev_index(x, y, c), (x, y, 1 - c)
        chips = [(1 - x, y), (x, 1 - y), (1 - x, 1 - y)]
        copies = []
        for i in range(n):
            mine = _slab(bo[i], layouts[i], me)
            copies.append(pltpu.make_async_copy(sh[i], mine, loc_sems.at[i]))
            copies.append(pltpu.make_async_remote_copy(
                src_ref=sh[i], dst_ref=mine, send_sem=send_sems.at[i, 0], recv_sem=recv_sems.at[i, 0],
                device_id=sib, device_id_type=MESH))
            for j, (px, py) in enumerate(chips):
                slab = _slab(bo[i], layouts[i], _dev_index(px, py, c))
                copies.append(pltpu.make_async_remote_copy(
                    src_ref=slab, dst_ref=slab, send_sem=send_sems.at[i, 1 + j], recv_sem=recv_sems.at[i, 1 + j],
                    device_id=sib, device_id_type=MESH))
        for cpy in copies:
            cpy.start()
        for cpy in copies:
            cpy.wait()

    outs = pl.pallas_call(
        body, out_shape=[_sds(a.shape, a.dtype) for a in bufs], in_specs=[ANY] * (2 * n), out_specs=[ANY] * n,
        input_output_aliases={n + i: i for i in range(n)},
        scratch_shapes=[pltpu.SemaphoreType.DMA((n, 4)), pltpu.SemaphoreType.DMA((n, 4)), pltpu.SemaphoreType.DMA((n,))],
        name=name)(*shards, *bufs)
    return list(outs)


def _chip_copy(sum_ref, land_ref, send_sems, recv_sems, i, j, chip, c):
    return pltpu.make_async_remote_copy(
        src_ref=sum_ref.at[2 * chip[0] + chip[1]], dst_ref=land_ref.at[j], send_sem=send_sems.at[3 * i + j],
        recv_sem=recv_sems.at[3 * i + j], device_id=(*chip, c), device_id_type=MESH)


def _chip_exchange_start(sums, name):
    n = len(sums)
    lands = [lax.empty((3,) + s.shape[1:], s.dtype) for s in sums]

    def body(*refs):
        sm, ld = refs[:n], refs[n:2 * n]
        send_sems, recv_sems = refs[2 * n], refs[2 * n + 1]
        token = refs[-1]
        x, y, c = _my_pos()
        for i in range(n):
            for j, chip in enumerate([(1 - x, y), (x, 1 - y), (1 - x, 1 - y)]):
                _chip_copy(sm[i], ld[i], send_sems, recv_sems, i, j, chip, c).start()
        token[...] = jnp.zeros_like(token)

    outs = pl.pallas_call(
        body, name=name,
        out_shape=(pltpu.SemaphoreType.DMA((3 * n,)), pltpu.SemaphoreType.DMA((3 * n,)),
                   *[pltpu.HBM(a.shape, a.dtype) for a in sums], *[pltpu.HBM(a.shape, a.dtype) for a in lands], TOKEN),
        in_specs=[HBM] * (2 * n), out_specs=(SEMS, SEMS, *[HBM] * (2 * n), pl.BlockSpec(memory_space=pltpu.VMEM)),
        input_output_aliases={i: 2 + i for i in range(2 * n)},
        compiler_params=pltpu.CompilerParams(has_side_effects=EFFECT),
    )(*[_hbm(a) for a in sums], *[_hbm(a) for a in lands])
    return outs[0], outs[1], list(outs[2:2 + n]), list(outs[2 + n:2 + 2 * n]), outs[-1]


def _chip_exchange_wait(send_sems, recv_sems, sums, lands, after, name):
    n = len(sums)

    def body(*refs):
        sm, ld = refs[:n], refs[n:2 * n]
        ssem, rsem = refs[2 * n], refs[2 * n + 1]
        x, y, c = _my_pos()
        for i in range(n):
            for j, chip in enumerate([(1 - x, y), (x, 1 - y), (1 - x, 1 - y)]):
                cp = _chip_copy(sm[i], ld[i], ssem, rsem, i, j, chip, c)
                cp.wait_send()
                cp.wait_recv()

    outs = pl.pallas_call(
        body, name=name,
        out_shape=(*[pltpu.HBM(a.shape, a.dtype) for a in sums], *[pltpu.HBM(a.shape, a.dtype) for a in lands]),
        in_specs=[HBM] * (2 * n) + [SEMS, SEMS, ANY], out_specs=tuple([HBM] * (2 * n)),
        input_output_aliases={i: i for i in range(2 * n)},
        compiler_params=pltpu.CompilerParams(has_side_effects=EFFECT),
    )(*sums, *lands, send_sems, recv_sems, after)
    return list(outs[:n]), list(outs[n:])


def _row_tile(r, itemsize):
    align = 32 // itemsize
    best = r
    for t in range(align, min(r, 256) + 1, align):
        if r % t == 0:
            best = t
    return best


def _prefetch_call(body, name, out_shape, grid, in_specs, out_specs, sem):
    return pl.pallas_call(
        body, out_shape=out_shape, name=name,
        grid_spec=pltpu.PrefetchScalarGridSpec(num_scalar_prefetch=1, grid=grid, in_specs=in_specs, out_specs=out_specs),
        compiler_params=pltpu.CompilerParams(dimension_semantics=sem, vmem_limit_bytes=VMEM_LIMIT))


def _pair_sum(pos, part, got, layout, name):
    kind, _ = layout
    _, r, cols = got.shape
    tr = _row_tile(r, part.dtype.itemsize)
    nr = r // tr
    if kind == "rows":
        pspec = pl.BlockSpec((tr, cols), lambda q, i, p: ((2 * q + p[0]) * nr + i, 0))
    elif kind == "cols":
        pspec = pl.BlockSpec((tr, cols), lambda q, i, p: (i, 2 * q + p[0]))
    else:
        pspec = pl.BlockSpec((None, tr, cols), lambda q, i, p: (2 * q + p[0], i, 0))

    def body(pos_ref, p_ref, g_ref, o_ref):
        o_ref[...] = (p_ref[...].astype(F32) + g_ref[...].astype(F32)).astype(o_ref.dtype)

    return _prefetch_call(body, name, _sds((4, r, cols), part.dtype), (4, nr),
                          [pspec, pl.BlockSpec((None, tr, cols), lambda q, i, p: (q, i, 0))],
                          pl.BlockSpec((None, tr, cols), lambda q, i, p: (q, i, 0)),
                          ("parallel", "parallel"))(pos, part, got)


def _final_sum_adam(pos, sums, got, w, m, v, name):
    r, cols = w.shape
    tr = _row_tile(r, sums.dtype.itemsize)

    def body(pos_ref, s_ref, g_ref, w_ref, m_ref, v_ref, go_ref, dl_ref, nm_ref, nv_ref):
        g = ((s_ref[...].astype(F32) + g_ref[0].astype(F32)) + g_ref[1].astype(F32)) + g_ref[2].astype(F32)
        dl, nm, nv = _adam(w_ref[...], g, m_ref[...], v_ref[...])
        go_ref[...] = g
        dl_ref[...] = dl
        nm_ref[...] = nm
        nv_ref[...] = nv

    big = pl.BlockSpec((tr, cols), lambda i, p: (i, 0))
    return _prefetch_call(body, name, [_sds((r, cols))] * 4, (r // tr,),
                          [pl.BlockSpec((None, tr, cols), lambda i, p: (p[1], i, 0)),
                           pl.BlockSpec((3, tr, cols), lambda i, p: (0, i, 0)), big, big, big],
                          [big] * 4, ("parallel",))(pos, sums, got, w, m, v)


def _small_adam(g_all, w, m, v):
    npk = w.shape[1]

    def body(g_ref, w_ref, m_ref, v_ref, go_ref, dl_ref, nm_ref, nv_ref):
        g = g_ref[0:1, :]
        for k in range(1, NDEV):
            g = g + g_ref[k:k + 1, :]
        dl, nm, nv = _adam(w_ref[...], g, m_ref[...], v_ref[...])
        go_ref[...] = g
        dl_ref[...] = dl
        nm_ref[...] = nm
        nv_ref[...] = nv

    return _call(body, "small_adam", [_sds((1, npk))] * 4)(g_all, w, m, v)


SMALL = [("b_ada", 6 * D), ("norm1_g", D), ("norm2_g", D), ("final_norm_g", D), ("cf_ln_g", CFW), ("cf_ln_b", CFW),
         ("dn_norm_g", DH), ("dn_a_log", H), ("dn_dt_bias", H)]
LATE = ["dn_w_o", "cf_w_o", "w_out", "ffn_w_up", "ffn_w_down"]
LATE_SHAPE = {"dn_w_o": (DNW, D), "cf_w_o": (CFW, D), "w_out": (D, D), "ffn_w_up": (D, 2 * FFN), "ffn_w_down": (FFN, D)}
LAYOUT = {"dn_w_o": ("cols", D // NDEV), "cf_w_o": ("cols", D // NDEV), "w_out": ("rows", D // NDEV),
          "ffn_w_up": ("cols", 2 * FFN // NDEV), "ffn_w_down": ("rows", FFN // NDEV),
          "w_in": ("lead", NDEV), "dn_conv_w": ("lead", NDEV), "cf_conv_w": ("lead", NDEV), "ffn_conv_w": ("lead", NDEV)}
NAMES = ["w_ada", "b_ada", "norm1_g", "w_in", "dn_conv_w", "dn_a_log", "dn_dt_bias", "dn_norm_g", "dn_w_o", "cf_conv_w",
         "cf_ln_g", "cf_ln_b", "cf_w_o", "w_out", "norm2_g", "ffn_w_up", "ffn_conv_w", "ffn_w_down", "final_norm_g"]


def _pack_small(d):
    rows = []
    for nm, n in SMALL:
        row = d[nm].reshape(1, n)
        pad = (-n) % LANE
        rows.append(jnp.pad(row, ((0, 0), (0, pad))) if pad else row)
    return jnp.concatenate(rows, axis=1)


def _unpack_small(row, shapes):
    out, off = {}, 0
    for nm, n in SMALL:
        out[nm] = row[0, off:off + n].reshape(shapes[nm])
        off += n + ((-n) % LANE)
    return out


def _cols_from_gathered(g):
    return jnp.transpose(g, (1, 0, 2)).reshape(g.shape[1], NDEV * g.shape[2])


def _cols_to_parts(full):
    r, ctot = full.shape
    return jnp.transpose(full.reshape(r, NDEV, ctot // NDEV), (1, 0, 2))


def kernel(x, c, w_ada, b_ada, norm1_g, w_in, dn_conv_w, dn_a_log, dn_dt_bias, dn_norm_g, dn_w_o, cf_conv_w, cf_ln_g, cf_ln_b, cf_w_o, w_out, norm2_g, ffn_w_up, ffn_conv_w, ffn_w_down, final_norm_g, loss_target, m_w_ada, m_b_ada, m_norm1_g, m_w_in, m_dn_conv_w, m_dn_a_log, m_dn_dt_bias, m_dn_norm_g, m_dn_w_o, m_cf_conv_w, m_cf_ln_g, m_cf_ln_b, m_cf_w_o, m_w_out, m_norm2_g, m_ffn_w_up, m_ffn_conv_w, m_ffn_w_down, m_final_norm_g, v_w_ada, v_b_ada, v_norm1_g, v_w_in, v_dn_conv_w, v_dn_a_log, v_dn_dt_bias, v_dn_norm_g, v_dn_w_o, v_cf_conv_w, v_cf_ln_g, v_cf_ln_b, v_cf_w_o, v_w_out, v_norm2_g, v_ffn_w_up, v_ffn_conv_w, v_ffn_w_down, v_final_norm_g):
    args = locals()
    w = {nm: args[nm] for nm in NAMES}
    mo = {nm: args["m_" + nm] for nm in NAMES}
    vo = {nm: args["v_" + nm] for nm in NAMES}
    shapes = {nm: w[nm].shape for nm in NAMES}
    px, py, pc = _my_pos()
    me = _dev_index(px, py, pc)

    def mat(a):
        return a.reshape(a.shape[-2:])

    pos = jnp.stack([pc, 2 * px + py]).astype(jnp.int32)

    first = ["w_in", "dn_conv_w", "cf_conv_w", "ffn_conv_w"]
    got = _all_gather([mat(w["w_in"]).astype(BF16)] + [mat(w[nm]) for nm in first[1:]] + [c], "gather_first")
    full = {nm: _cols_from_gathered(g) for nm, g in zip(first, got[:-1])}
    c_all = got[-1].reshape(NDEV, D)
    w_in_p = _pad_win(full["w_in"])

    late_shards = [mat(w[nm]).astype(BF16) for nm in LATE]
    late_bufs = [lax.empty(LATE_SHAPE[nm], BF16) for nm in LATE]
    late_lay = [LAYOUT[nm] for nm in LATE]
    l_send, l_recv, l_shards, l_bufs, l_token = _gather_ici_start(late_shards, late_bufs, late_lay, c_all, "gather_late_start")

    res = {}

    class Comm:
        token0 = l_token[0, 0]
        pending = {}

        @staticmethod
        def late_weights(after):
            shards, bufs = _gather_ici_wait(l_send, l_recv, l_shards, l_bufs, late_lay, after, "gather_late_wait")
            return _gather_pair(shards, bufs, late_lay, "gather_late_pair")

        @staticmethod
        def grads(group, gd):
            names = list(gd)
            lays = [LAYOUT[nm] for nm in names]
            gl = []
            for nm in names:
                g = _unpad_win(gd[nm]) if nm == "w_in" else gd[nm]
                gl.append(_cols_to_parts(g) if LAYOUT[nm][0] == "lead" else g)
            from_sib = _pair_exchange(gl, lays, "rs_pair_" + group)
            sums = [_pair_sum(pos, g, r, lay, "rs_pair_sum_" + nm) for nm, g, r, lay in zip(names, gl, from_sib, lays)]
            started = _chip_exchange_start(sums, "rs_chips_start_" + group)
            Comm.pending[group] = (names,) + tuple(started[:4])
            return started[4][0, 0]

        @staticmethod
        def finish(group, after):
            names, ssem, rsem, sums, lands = Comm.pending[group]
            sums, lands = _chip_exchange_wait(ssem, rsem, sums, lands, after, "rs_chips_wait_" + group)
            for nm, s, r in zip(names, sums, lands):
                outs = _final_sum_adam(pos, s, r, mat(w[nm]), mat(mo[nm]), mat(vo[nm]), "adam_" + nm)
                res[nm] = [o.reshape(shapes[nm]) for o in outs]
            return res[names[-1]][0]

    ncol = 6 * D // NDEV
    b_sh = lax.dynamic_slice(b_ada.reshape(1, 6 * D), (0, me * ncol), (1, ncol))
    mod_sh = _ada_fwd(c_all, mat(w_ada), b_sh)
    mod_all = _all_gather([mod_sh], "gather_mod")[0]
    mod = lax.dynamic_index_in_dim(mod_all, me, axis=1, keepdims=False).reshape(1, 6 * D)

    vec = lambda a: a.reshape(1, -1)
    loss, grad_x, small = _local_step(
        x.reshape(S, D), loss_target.reshape(S, D), mod, vec(norm1_g), vec(norm2_g), vec(final_norm_g), w_in_p,
        full["dn_conv_w"], vec(dn_a_log), vec(dn_dt_bias), vec(dn_norm_g), full["cf_conv_w"], vec(cf_ln_g),
        vec(cf_ln_b), full["ffn_conv_w"], Comm)

    done_a = Comm.finish("a", grad_x)
    done_b = Comm.finish("b", done_a)

    small["b_ada"] = small.pop("mod")
    g_small = _all_gather([_pack_small(small)], "gather_small")[0].reshape(NDEV, -1)
    outs = _small_adam(g_small, _pack_small({nm: w[nm] for nm, _ in SMALL}), _pack_small({nm: mo[nm] for nm, _ in SMALL}),
                       _pack_small({nm: vo[nm] for nm, _ in SMALL}))
    unpacked = [_unpack_small(o, shapes) for o in outs]
    for nm, _ in SMALL:
        res[nm] = [u[nm] for u in unpacked]

    dmod_sel = lax.dynamic_slice(g_small[:, :6 * D], (0, me * ncol), (NDEV, ncol))
    outs = _ada_bwd_adam(c_all, dmod_sel, mat(w_ada), mat(m_w_ada), mat(v_w_ada))
    res["w_ada"] = [o.reshape(shapes["w_ada"]) for o in outs]
    Comm.finish("c", jnp.concatenate([done_b.reshape(-1)[:LANE], outs[0].reshape(-1)[:LANE]]))

    loss = lax.psum(loss.reshape(()), ("x", "y", "c"))
    out = [loss, grad_x.reshape(x.shape)]
    for k in range(4):
        out += [res[nm][k] for nm in NAMES]
    return tuple(out)
```

```python
import functools

import jax
import jax.numpy as jnp
from jax import lax
from jax.experimental import pallas as pl
from jax.experimental.pallas import tpu as pltpu

F32 = jnp.float32
BF16 = jnp.bfloat16
HI = lax.Precision.HIGHEST
MESH = pl.DeviceIdType.MESH
ANY = pl.BlockSpec(memory_space=pl.ANY)

NDEV = 8
D = 2048
S = 2048
H = 8
DH = 128
DNW = H * DH
CFW = 1024
CFK = 31
DNK = 4
FFN = 5632
FFK = 3
CH = 64
NCH = S // CH
EPS = 1e-6
NIN = 10256
NINP = 10368
O_Z, O_GA, O_GB, O_GLU, O_SM = 3072, 4096, 6144, 8192, 10240
LANE = 128
TS = 256
VMEM_LIMIT = 56 * 1024 * 1024

ADAM_LR, ADAM_B1, ADAM_B2, ADAM_EPS, ADAM_WD, ADAM_STEP = 0.001, 0.9, 0.999, 1e-08, 0.01, 10


def _call(body, name, out_shape, grid=(), in_specs=None, out_specs=None, scratch=(), sem=None, aliases=None):
    kw = {}
    if aliases:
        kw["input_output_aliases"] = aliases
    if in_specs is not None:
        kw["in_specs"] = in_specs
    if out_specs is not None:
        kw["out_specs"] = out_specs
    return pl.pallas_call(
        body, out_shape=out_shape, grid=grid, scratch_shapes=scratch, name=name,
        compiler_params=pltpu.CompilerParams(dimension_semantics=sem, vmem_limit_bytes=VMEM_LIMIT), **kw)


def _sds(shape, dtype=F32):
    return jax.ShapeDtypeStruct(shape, dtype)


def _tile(dim, pref):
    if dim <= pref:
        return dim
    best = None
    for t in range(LANE, pref + 1, LANE):
        if dim % t == 0:
            best = t
    assert best is not None, (dim, pref)
    return best


def _sigmoid(x):
    return 1.0 / (1.0 + jnp.exp(-x))


def _silu(x):
    return x * _sigmoid(x)


def _dsilu(x):
    s = _sigmoid(x)
    return s * (1.0 + x * (1.0 - s))


def _softplus(x):
    return jnp.maximum(x, 0.0) + jnp.log(1.0 + jnp.exp(-jnp.abs(x)))


def _dot(a, b, dims, precision=None):
    return lax.dot_general(a, b, (dims, ((), ())), preferred_element_type=F32, precision=precision)


NN = ((1,), (0,))
NT = ((1,), (1,))
TN = ((0,), (0,))


def _my_pos():
    return lax.axis_index("x"), lax.axis_index("y"), lax.axis_index("c")


def _mm(a, b, mode, out_dtype, name, tm=1024, tn=1024, tk=2048, a2=None, b2=None):
    sharded = b.ndim == 3
    if sharded and mode == "nn":
        cs = b.shape[2]
        (m, k), n = a.shape, NDEV * cs
        gs = max(1, tn // cs)
        tm, tn, tk = _tile(m, tm), gs * cs, _tile(k, tk)
    elif sharded:
        assert mode == "nt"
        cs = b.shape[2]
        m, n, k = a.shape[0], b.shape[1], NDEV * cs
        gs = max(1, tk // cs)
        tm, tn, tk = _tile(m, tm), _tile(n, tn), gs * cs
    else:
        if mode == "nn":
            (m, k), (k2, n) = a.shape, b.shape
        elif mode == "nt":
            (m, k), (n, k2) = a.shape, b.shape
        else:
            (k, m), (k2, n) = a.shape, b.shape
        assert k == k2, (a.shape, b.shape, mode)
        n = n * (2 if b2 is not None else 1)
        tm, tn, tk = _tile(m, tm), _tile(n // (2 if b2 is not None else 1), tn), _tile(k, tk)
    nk, nj = k // tk, n // tn
    halfk, halfj = nk // 2, nj // 2
    dims = {"nn": NN, "nt": NT, "tn": TN}[mode]

    def body(*refs):
        a_ref, b_ref = refs[0], refs[1]
        x_ref = refs[2] if (a2 is not None or b2 is not None) else None
        o_ref, acc_ref = refs[-2], refs[-1]
        j, kk = pl.program_id(1), pl.program_id(2)

        def accumulate(part, cols=slice(None)):
            @pl.when(kk == 0)
            def _():
                acc_ref[:, cols] = part

            @pl.when(kk > 0)
            def _():
                acc_ref[:, cols] += part

        if sharded and mode == "nn":
            for q in range(gs):
                accumulate(_dot(a_ref[...], b_ref[q], NN), slice(q * cs, (q + 1) * cs))
        elif sharded:
            def contract(lhs_ref):
                part = None
                for q in range(gs):
                    term = _dot(lhs_ref[:, q * cs:(q + 1) * cs], b_ref[q], NT)
                    part = term if part is None else part + term
                accumulate(part)

            if a2 is None:
                contract(a_ref)
            else:
                pl.when(kk < halfk)(lambda: contract(a_ref))
                pl.when(kk >= halfk)(lambda: contract(x_ref))
        elif b2 is not None:
            pl.when(j < halfj)(lambda: accumulate(_dot(a_ref[...], b_ref[...], dims)))
            pl.when(j >= halfj)(lambda: accumulate(_dot(a_ref[...], x_ref[...], dims)))
        else:
            accumulate(_dot(a_ref[...], b_ref[...], dims))

        @pl.when(kk == nk - 1)
        def _():
            o_ref[...] = acc_ref[...].astype(o_ref.dtype)

    ins, in_specs = [a], []
    if mode == "tn":
        in_specs.append(pl.BlockSpec((tk, tm), lambda i, j, kk: (kk, i)))
    elif a2 is not None:
        in_specs.append(pl.BlockSpec((tm, tk), lambda i, j, kk: (i, jnp.minimum(kk, halfk - 1))))
    else:
        in_specs.append(pl.BlockSpec((tm, tk), lambda i, j, kk: (i, kk)))
    ins.append(b)
    if sharded and mode == "nn":
        in_specs.append(pl.BlockSpec((gs, tk, cs), lambda i, j, kk: (j, kk, 0)))
    elif sharded:
        in_specs.append(pl.BlockSpec((gs, tn, cs), lambda i, j, kk: (kk, j, 0)))
    elif mode == "nt":
        in_specs.append(pl.BlockSpec((tn, tk), lambda i, j, kk: (j, kk)))
    elif b2 is not None:
        in_specs.append(pl.BlockSpec((tk, tn), lambda i, j, kk: (kk, jnp.minimum(j, halfj - 1))))
    else:
        in_specs.append(pl.BlockSpec((tk, tn), lambda i, j, kk: (kk, j)))
    if a2 is not None:
        ins.append(a2)
        in_specs.append(pl.BlockSpec((tm, tk), lambda i, j, kk: (i, jnp.maximum(kk - halfk, 0))))
    if b2 is not None:
        ins.append(b2)
        in_specs.append(pl.BlockSpec((tk, tn), lambda i, j, kk: (kk, jnp.maximum(j - halfj, 0))))
    return _call(body, name, _sds((m, n), out_dtype), grid=(m // tm, nj, nk),
                 in_specs=in_specs, out_specs=pl.BlockSpec((tm, tn), lambda i, j, kk: (i, j)),
                 scratch=[pltpu.VMEM((tm, tn), F32)], sem=("parallel", "parallel", "arbitrary"))(*ins)


def _ada_fwd(c_all, w_sh, b_sh):
    n = w_sh.shape[1]
    tn = 512

    def body(c_ref, w_ref, b_ref, o_ref):
        ca = _silu(c_ref[...]).astype(BF16)
        o_ref[...] = _dot(ca, w_ref[...].astype(BF16), NN) + b_ref[...]

    return _call(body, "ada_fwd", _sds((NDEV, n)), grid=(n // tn,),
                 in_specs=[pl.BlockSpec((NDEV, D), lambda j: (0, 0)), pl.BlockSpec((D, tn), lambda j: (0, j)),
                           pl.BlockSpec((1, tn), lambda j: (0, j))],
                 out_specs=pl.BlockSpec((NDEV, tn), lambda j: (0, j)), sem=("parallel",))(c_all, w_sh, b_sh)


def _adam(w, g, m, v):
    m = ADAM_B1 * m + (1.0 - ADAM_B1) * g
    v = ADAM_B2 * v + (1.0 - ADAM_B2) * (g * g)
    m_hat = m / (1.0 - ADAM_B1 ** ADAM_STEP)
    v_hat = v / (1.0 - ADAM_B2 ** ADAM_STEP)
    delta = -ADAM_LR * (m_hat / (jnp.sqrt(v_hat) + ADAM_EPS) + ADAM_WD * w)
    return delta, m, v


def _ada_bwd_adam(c_all, dmod_sel, w, m, v):
    r, n = w.shape
    tr = 256

    def body(c_ref, d_ref, w_ref, m_ref, v_ref, g_ref, dl_ref, nm_ref, nv_ref):
        ca = _silu(c_ref[...])
        g = _dot(ca, d_ref[...], TN, precision=HI)
        dl, nm, nv = _adam(w_ref[...], g, m_ref[...], v_ref[...])
        g_ref[...] = g
        dl_ref[...] = dl
        nm_ref[...] = nm
        nv_ref[...] = nv

    big = pl.BlockSpec((tr, n), lambda i: (i, 0))
    return _call(body, "ada_bwd_adam", [_sds((r, n))] * 4, grid=(r // tr,),
                 in_specs=[pl.BlockSpec((NDEV, tr), lambda i: (0, i)), pl.BlockSpec((NDEV, n), lambda i: (0, 0)),
                           big, big, big],
                 out_specs=[big] * 4, sem=("parallel",))(c_all, dmod_sel, w, m, v)


def _row_spec(width=D):
    return pl.BlockSpec((TS, width), lambda i: (i, 0))


def _vec_spec(width=D):
    return pl.BlockSpec((1, width), lambda i: (0, 0))


def _acc_spec(width=D):
    return pl.BlockSpec((8, width), lambda i: (0, 0))


def _norm_mod(x, g, sc, sh, name):
    def body(x_ref, g_ref, sc_ref, sh_ref, o_ref):
        xv = x_ref[...]
        r = lax.rsqrt(jnp.mean(xv * xv, axis=-1, keepdims=True) + EPS)
        o_ref[...] = ((xv * r) * g_ref[...] * (1.0 + sc_ref[...]) + sh_ref[...]).astype(BF16)

    return _call(body, name, _sds((S, D), BF16), grid=(S // TS,),
                 in_specs=[_row_spec(), _vec_spec(), _vec_spec(), _vec_spec()], out_specs=_row_spec(),
                 sem=("parallel",))(x, g, sc, sh)


def _resid_norm_mod(x, mix, gt, g, sc, sh, name):
    def body(x_ref, mix_ref, gt_ref, g_ref, sc_ref, sh_ref, x2_ref, o_ref):
        xv = x_ref[...] + gt_ref[...] * mix_ref[...]
        x2_ref[...] = xv
        r = lax.rsqrt(jnp.mean(xv * xv, axis=-1, keepdims=True) + EPS)
        o_ref[...] = ((xv * r) * g_ref[...] * (1.0 + sc_ref[...]) + sh_ref[...]).astype(BF16)

    return _call(body, name, [_sds((S, D)), _sds((S, D), BF16)], grid=(S // TS,),
                 in_specs=[_row_spec(), _row_spec()] + [_vec_spec()] * 4, out_specs=[_row_spec(), _row_spec()],
                 sem=("parallel",))(x, mix, gt, g, sc, sh)


def _acc_rows(acc_ref, rows):
    @pl.when(pl.program_id(0) == 0)
    def _():
        acc_ref[...] = jnp.zeros_like(acc_ref)

    for k, row in enumerate(rows):
        acc_ref[k:k + 1, :] += row


def _loss_head(x2, f, tgt, gt2, gf):
    def body(x2_ref, f_ref, t_ref, gt_ref, gf_ref, dx_ref, df_ref, acc_ref):
        fv = f_ref[...]
        x3 = x2_ref[...] + gt_ref[...] * fv
        r = lax.rsqrt(jnp.mean(x3 * x3, axis=-1, keepdims=True) + EPS)
        xn = x3 * r
        e = xn * gf_ref[...] - t_ref[...]
        loss = 0.5 * jnp.sum(jnp.mean(e * e, axis=-1, keepdims=True), axis=0, keepdims=True)
        dy = e * (1.0 / D)
        dxn = dy * gf_ref[...]
        dx3 = r * (dxn - xn * jnp.mean(dxn * xn, axis=-1, keepdims=True))
        dx_ref[...] = dx3
        df_ref[...] = (dx3 * gt_ref[...]).astype(BF16)
        _acc_rows(acc_ref, [jnp.sum(dy * xn, axis=0, keepdims=True), jnp.sum(dx3 * fv, axis=0, keepdims=True),
                            jnp.broadcast_to(loss, (1, D))])

    return _call(body, "loss_head", [_sds((S, D)), _sds((S, D), BF16), _sds((8, D))], grid=(S // TS,),
                 in_specs=[_row_spec(), _row_spec(), _row_spec(), _vec_spec(), _vec_spec()],
                 out_specs=[_row_spec(), _row_spec(), _acc_spec()], sem=("arbitrary",))(x2, f, tgt, gt2, gf)


def _norm_mod_bwd(dhn, x, dres, g, sc, name, mix=None, gt=None):
    gated = mix is not None

    def body(*refs):
        if gated:
            dhn_ref, x_ref, dres_ref, g_ref, sc_ref, mix_ref, gt_ref, dx_ref, dmix_ref, acc_ref = refs
        else:
            dhn_ref, x_ref, dres_ref, g_ref, sc_ref, dx_ref, acc_ref = refs
        xv = x_ref[...]
        dh = dhn_ref[...]
        r = lax.rsqrt(jnp.mean(xv * xv, axis=-1, keepdims=True) + EPS)
        xn = xv * r
        gv = g_ref[...]
        sc1 = 1.0 + sc_ref[...]
        dxn = dh * gv * sc1
        dx = dres_ref[...] + r * (dxn - xn * jnp.mean(dxn * xn, axis=-1, keepdims=True))
        dx_ref[...] = dx
        rows = [jnp.sum(dh, axis=0, keepdims=True), jnp.sum(dh * xn * gv, axis=0, keepdims=True),
                jnp.sum(dh * xn * sc1, axis=0, keepdims=True)]
        if gated:
            rows.append(jnp.sum(dx * mix_ref[...], axis=0, keepdims=True))
            dmix_ref[...] = (dx * gt_ref[...]).astype(BF16)
        _acc_rows(acc_ref, rows)

    ins = [dhn, x, dres, g, sc]
    in_specs = [_row_spec(), _row_spec(), _row_spec(), _vec_spec(), _vec_spec()]
    outs = [_sds((S, D))]
    out_specs = [_row_spec()]
    if gated:
        ins += [mix, gt]
        in_specs += [_row_spec(), _vec_spec()]
        outs.append(_sds((S, D), BF16))
        out_specs.append(_row_spec())
    outs.append(_sds((8, D)))
    out_specs.append(_acc_spec())
    return _call(body, name, outs, grid=(S // TS,), in_specs=in_specs, out_specs=out_specs,
                 sem=("arbitrary",))(*ins)


RC = 256


def _conv_fwd_rows(pad_ref, w_ref, kw, head, r0):
    acc = None
    for k in range(kw):
        term = w_ref[k:k + 1, :] * pad_ref[pl.ds(head - (kw - 1) + k + r0, RC), :]
        acc = term if acc is None else acc + term
    return acc


def _conv_bwd_rows(pad2_ref, w_ref, kw, r0):
    acc = None
    for k in range(kw):
        term = w_ref[k:k + 1, :] * pad2_ref[pl.ds(kw - 1 - k + r0, RC), :]
        acc = term if acc is None else acc + term
    return acc


def _conv_dw(pad_ref, dout_ref, dw_ref, kw, head):
    for k in range(kw):
        acc = None
        for r0 in range(0, S, RC):
            term = jnp.sum(pad_ref[pl.ds(head - (kw - 1) + k + r0, RC), :] * dout_ref[pl.ds(r0, RC), :],
                           axis=0, keepdims=True)
            acc = term if acc is None else acc + term
        dw_ref[k:k + 1, :] = acc


def _col_spec(width, off_blocks=0):
    return pl.BlockSpec((S, width), lambda j: (0, j + off_blocks))


def _dn_pre_fwd(proj, conv_w):
    head = 8

    def body(x_ref, w_ref, o_ref, pad_ref):
        j = pl.program_id(0)
        pad_ref[pl.ds(0, head), :] = jnp.zeros((head, DH), F32)
        pad_ref[pl.ds(head, S), :] = x_ref[...]
        scale = jnp.where(j < H, DH ** -0.5, 1.0)
        for r0 in range(0, S, RC):
            y = _silu(_conv_fwd_rows(pad_ref, w_ref, DNK, head, r0))
            rinv = lax.rsqrt(jnp.sum(y * y, axis=-1, keepdims=True) + EPS)
            o_ref[pl.ds(r0, RC), :] = jnp.where(j < 2 * H, y * rinv * scale, y)

    return _call(body, "dn_pre_fwd", _sds((S, 3 * DNW)), grid=(3 * H,),
                 in_specs=[_col_spec(DH), pl.BlockSpec((DNK, DH), lambda j: (0, j))], out_specs=_col_spec(DH),
                 scratch=[pltpu.VMEM((S + head, DH), F32)], sem=("parallel",))(proj, conv_w)


def _dn_pre_bwd(dq, dk, dv, proj, conv_w, dproj):
    head = 8

    def body(dq_ref, dk_ref, dv_ref, x_ref, w_ref, dproj_in, dx_ref, dw_ref, pad_ref, pad2_ref):
        j = pl.program_id(0)
        pad_ref[pl.ds(0, head), :] = jnp.zeros((head, DH), F32)
        pad_ref[pl.ds(head, S), :] = x_ref[...]
        pad2_ref[pl.ds(S, head), :] = jnp.zeros((head, DH), F32)
        scale = jnp.where(j < H, DH ** -0.5, 1.0)
        for r0 in range(0, S, RC):
            xc = _conv_fwd_rows(pad_ref, w_ref, DNK, head, r0)
            y = _silu(xc)
            rinv = lax.rsqrt(jnp.sum(y * y, axis=-1, keepdims=True) + EPS)
            yn = y * rinv
            rows = pl.ds(r0, RC)
            do = jnp.where(j < H, dq_ref[rows, :], jnp.where(j < 2 * H, dk_ref[rows, :], dv_ref[rows, :]))
            dy_n = scale * rinv * (do - yn * jnp.sum(do * yn, axis=-1, keepdims=True))
            dy = jnp.where(j < 2 * H, dy_n, do)
            pad2_ref[rows, :] = dy * _dsilu(xc)
        for r0 in range(0, S, RC):
            dx_ref[pl.ds(r0, RC), :] = _conv_bwd_rows(pad2_ref, w_ref, DNK, r0).astype(BF16)
        _conv_dw(pad_ref, pad2_ref, dw_ref, DNK, head)

    wspec = pl.BlockSpec((DNK, DH), lambda j: (0, j))
    head_col = lambda lo: pl.BlockSpec((S, DH), lambda j: (0, jnp.clip(j - lo, 0, H - 1)))
    return _call(body, "dn_pre_bwd", [_sds((S, NINP), BF16), _sds((DNK, 3 * DNW))], grid=(3 * H,),
                 in_specs=[head_col(0), head_col(H), head_col(2 * H), _col_spec(DH), wspec, ANY],
                 out_specs=[_col_spec(DH), wspec],
                 scratch=[pltpu.VMEM((S + head, DH), F32), pltpu.VMEM((S + head, DH), F32)],
                 sem=("parallel",), aliases={5: 0})(dq, dk, dv, proj, conv_w, dproj)


CF_HEAD = 32
CF_VAL = pl.BlockSpec((S, LANE), lambda j: (0, O_GLU // LANE + 2 * j))
CF_GL = pl.BlockSpec((S, LANE), lambda j: (0, O_GLU // LANE + 2 * j + 1))


def _cf_conv_fwd(proj, conv_w):
    def body(val_ref, gl_ref, w_ref, o_ref, pad_ref):
        pad_ref[pl.ds(0, CF_HEAD), :] = jnp.zeros((CF_HEAD, LANE), F32)
        pad_ref[pl.ds(CF_HEAD, S), :] = val_ref[...] * _sigmoid(gl_ref[...])
        for r0 in range(0, S, RC):
            o_ref[pl.ds(r0, RC), :] = _conv_fwd_rows(pad_ref, w_ref, CFK, CF_HEAD, r0)

    wspec = pl.BlockSpec((CFK, LANE), lambda j: (0, j))
    return _call(body, "cf_conv_fwd", _sds((S, CFW)), grid=(CFW // LANE,),
                 in_specs=[CF_VAL, CF_GL, wspec], out_specs=_col_spec(LANE),
                 scratch=[pltpu.VMEM((S + CF_HEAD, LANE), F32)], sem=("parallel",))(proj, proj, conv_w)


def _cf_conv_bwd(du1, proj, conv_w, dproj):
    def body(d_ref, val_ref, gl_ref, w_ref, dproj_in, dp_ref, dw_ref, pad_ref, pad2_ref):
        sg = _sigmoid(gl_ref[...])
        pad_ref[pl.ds(0, CF_HEAD), :] = jnp.zeros((CF_HEAD, LANE), F32)
        pad_ref[pl.ds(CF_HEAD, S), :] = val_ref[...] * sg
        pad2_ref[pl.ds(0, S), :] = d_ref[...]
        pad2_ref[pl.ds(S, CF_HEAD), :] = jnp.zeros((CF_HEAD, LANE), F32)
        for r0 in range(0, S, RC):
            du0 = _conv_bwd_rows(pad2_ref, w_ref, CFK, r0)
            rows = pl.ds(r0, RC)
            sgr = _sigmoid(gl_ref[rows, :])
            dp_ref[rows, 0:LANE] = (du0 * sgr).astype(BF16)
            dp_ref[rows, LANE:2 * LANE] = (du0 * val_ref[rows, :] * sgr * (1.0 - sgr)).astype(BF16)
        _conv_dw(pad_ref, pad2_ref, dw_ref, CFK, CF_HEAD)

    wspec = pl.BlockSpec((CFK, LANE), lambda j: (0, j))
    return _call(body, "cf_conv_bwd", [_sds((S, NINP), BF16), _sds((CFK, CFW))], grid=(CFW // LANE,),
                 in_specs=[_col_spec(LANE), CF_VAL, CF_GL, wspec, ANY],
                 out_specs=[pl.BlockSpec((S, 2 * LANE), lambda j: (0, O_GLU // (2 * LANE) + j)), wspec],
                 scratch=[pltpu.VMEM((S + CF_HEAD, LANE), F32), pltpu.VMEM((S + CF_HEAD, LANE), F32)],
                 sem=("parallel",), aliases={4: 0})(du1, proj, proj, conv_w, dproj)


def _cf_ln_fwd(u1, g, b):
    def body(u_ref, g_ref, b_ref, o_ref):
        u = u_ref[...]
        mu = jnp.mean(u, axis=-1, keepdims=True)
        xc = u - mu
        y = xc * lax.rsqrt(jnp.mean(xc * xc, axis=-1, keepdims=True) + EPS)
        o_ref[...] = _silu(y * g_ref[...] + b_ref[...]).astype(BF16)

    return _call(body, "cf_ln_fwd", _sds((S, CFW), BF16), grid=(S // TS,),
                 in_specs=[_row_spec(CFW), _vec_spec(CFW), _vec_spec(CFW)], out_specs=_row_spec(CFW),
                 sem=("parallel",))(u1, g, b)


def _cf_ln_bwd(du3, u1, g, b):
    def body(d_ref, u_ref, g_ref, b_ref, du_ref, acc_ref):
        u = u_ref[...]
        mu = jnp.mean(u, axis=-1, keepdims=True)
        xc = u - mu
        rstd = lax.rsqrt(jnp.mean(xc * xc, axis=-1, keepdims=True) + EPS)
        xh = xc * rstd
        du2 = d_ref[...] * _dsilu(xh * g_ref[...] + b_ref[...])
        dxh = du2 * g_ref[...]
        du_ref[...] = rstd * (dxh - jnp.mean(dxh, axis=-1, keepdims=True)
                              - xh * jnp.mean(dxh * xh, axis=-1, keepdims=True))
        _acc_rows(acc_ref, [jnp.sum(du2 * xh, axis=0, keepdims=True), jnp.sum(du2, axis=0, keepdims=True)])

    return _call(body, "cf_ln_bwd", [_sds((S, CFW)), _sds((8, CFW))], grid=(S // TS,),
                 in_specs=[_row_spec(CFW), _row_spec(CFW), _vec_spec(CFW), _vec_spec(CFW)],
                 out_specs=[_row_spec(CFW), _acc_spec(CFW)], sem=("arbitrary",))(du3, u1, g, b)


FB = 256
FNB = FFN // FB
FF_HEAD = 8


def _ffn_mid_fwd(upall, conv_w):
    def body(gate_ref, up_ref, w_ref, o_ref, pad_ref):
        pad_ref[pl.ds(0, FF_HEAD), :] = jnp.zeros((FF_HEAD, FB), F32)
        pad_ref[pl.ds(FF_HEAD, S), :] = gate_ref[...]
        for r0 in range(0, S, RC):
            gc = _conv_fwd_rows(pad_ref, w_ref, FFK, FF_HEAD, r0)
            o_ref[pl.ds(r0, RC), :] = (_silu(gc) * up_ref[pl.ds(r0, RC), :]).astype(BF16)

    wspec = pl.BlockSpec((FFK, FB), lambda j: (0, j))
    return _call(body, "ffn_mid_fwd", _sds((S, FFN), BF16), grid=(FNB,),
                 in_specs=[_col_spec(FB), _col_spec(FB, FNB), wspec], out_specs=_col_spec(FB),
                 scratch=[pltpu.VMEM((S + FF_HEAD, FB), F32)], sem=("parallel",))(upall, upall, conv_w)


def _ffn_mid_bwd(dh, upall, conv_w):
    def body(d_ref, gate_ref, up_ref, w_ref, dgate_ref, dup_ref, dw_ref, pad_ref, pad2_ref):
        pad_ref[pl.ds(0, FF_HEAD), :] = jnp.zeros((FF_HEAD, FB), F32)
        pad_ref[pl.ds(FF_HEAD, S), :] = gate_ref[...]
        pad2_ref[pl.ds(S, FF_HEAD), :] = jnp.zeros((FF_HEAD, FB), F32)
        for r0 in range(0, S, RC):
            rows = pl.ds(r0, RC)
            gc = _conv_fwd_rows(pad_ref, w_ref, FFK, FF_HEAD, r0)
            dhv = d_ref[rows, :]
            dup_ref[rows, :] = (dhv * _silu(gc)).astype(BF16)
            pad2_ref[rows, :] = dhv * up_ref[rows, :] * _dsilu(gc)
        for r0 in range(0, S, RC):
            dgate_ref[pl.ds(r0, RC), :] = _conv_bwd_rows(pad2_ref, w_ref, FFK, r0).astype(BF16)
        _conv_dw(pad_ref, pad2_ref, dw_ref, FFK, FF_HEAD)

    wspec = pl.BlockSpec((FFK, FB), lambda j: (0, j))
    return _call(body, "ffn_mid_bwd", [_sds((S, FFN), BF16), _sds((S, FFN), BF16), _sds((FFK, FFN))],
                 grid=(FNB,), in_specs=[_col_spec(FB), _col_spec(FB), _col_spec(FB, FNB), wspec],
                 out_specs=[_col_spec(FB), _col_spec(FB), wspec],
                 scratch=[pltpu.VMEM((S + FF_HEAD, FB), F32), pltpu.VMEM((S + FF_HEAD, FB), F32)],
                 sem=("parallel",))(dh, upall, upall, conv_w)


GT = 256
SM_BLK = O_SM // LANE


def _chunk_tri(lower):
    r = lax.broadcasted_iota(jnp.int32, (GT, GT), 0)
    c = lax.broadcasted_iota(jnp.int32, (GT, GT), 1)
    same = (r // CH) == (c // CH)
    tri = (c <= r) if lower else (c >= r)
    return jnp.where(same & tri, 1.0, 0.0).astype(F32)


def _gates_fwd(proj, alog_v, dtb_v):
    def body(sm_ref, al_ref, dt_ref, o_ref):
        lane = lax.broadcasted_iota(jnp.int32, (GT, LANE), 1)
        tri = _chunk_tri(True)
        na = -jnp.exp(al_ref[...])
        for r0 in range(0, S, GT):
            sm = sm_ref[pl.ds(r0, GT), :]
            raw = jnp.where((lane >= H) & (lane < 2 * H), na * _softplus(sm + dt_ref[...]), 0.0)
            gc = _dot(tri, raw, NN, precision=HI)
            o_ref[pl.ds(r0, GT), :] = jnp.where(lane < H, _sigmoid(sm), gc)

    return _call(body, "gates_fwd", _sds((S, LANE)), grid=(1,),
                 in_specs=[pl.BlockSpec((S, LANE), lambda i: (0, SM_BLK)), _vec_spec(LANE), _vec_spec(LANE)],
                 out_specs=pl.BlockSpec((S, LANE), lambda i: (0, 0)), sem=("arbitrary",))(proj, alog_v, dtb_v)


def _gates_bwd(dgb, proj, alog_v, dtb_v, dproj):
    def body(d_ref, sm_ref, al_ref, dt_ref, dproj_in, o_ref, acc_ref):
        lane = lax.broadcasted_iota(jnp.int32, (GT, LANE), 1)
        is_g = (lane >= H) & (lane < 2 * H)
        tri = _chunk_tri(False)
        na = -jnp.exp(al_ref[...])
        d_al = jnp.zeros((1, LANE), F32)
        d_dt = jnp.zeros((1, LANE), F32)
        for r0 in range(0, S, GT):
            sm = sm_ref[pl.ds(r0, GT), :]
            dv = d_ref[pl.ds(r0, GT), :]
            z = sm + dt_ref[...]
            draw = _dot(tri, jnp.where(is_g, dv, 0.0), NN, precision=HI)
            dlogit = jnp.where(is_g, draw * na * _sigmoid(z), 0.0)
            d_al = d_al + jnp.sum(jnp.where(is_g, draw * na * _softplus(z), 0.0), axis=0, keepdims=True)
            d_dt = d_dt + jnp.sum(dlogit, axis=0, keepdims=True)
            bt = _sigmoid(sm)
            o_ref[pl.ds(r0, GT), :] = jnp.where(lane < H, dv * bt * (1.0 - bt), dlogit).astype(BF16)
        acc_ref[...] = jnp.zeros_like(acc_ref)
        acc_ref[0:1, :] = d_al
        acc_ref[1:2, :] = d_dt

    return _call(body, "gates_bwd", [_sds((S, NINP), BF16), _sds((8, LANE))], grid=(1,),
                 in_specs=[pl.BlockSpec((S, LANE), lambda i: (0, 0)), pl.BlockSpec((S, LANE), lambda i: (0, SM_BLK)),
                           _vec_spec(LANE), _vec_spec(LANE), ANY],
                 out_specs=[pl.BlockSpec((S, LANE), lambda i: (0, SM_BLK)), _acc_spec(LANE)],
                 sem=("arbitrary",), aliases={4: 0})(dgb, proj, alog_v, dtb_v, dproj)


def _neumann_inv(a, eye):
    x = -a
    t = eye + x
    p = x
    for _ in range(5):
        p = _dot(p, p, NN, precision=HI)
        t = t + _dot(t, p, NN, precision=HI)
    return t


def _head_specs():
    q = pl.BlockSpec((S, DH), lambda h: (0, h))
    k = pl.BlockSpec((S, DH), lambda h: (0, H + h))
    v = pl.BlockSpec((S, DH), lambda h: (0, 2 * H + h))
    gb = pl.BlockSpec((None, S, DH), lambda h: (h, 0, 0))
    gr = pl.BlockSpec((None, NCH, CH), lambda h: (h, 0, 0))
    return q, k, v, gb, gr


ST_SPEC = pl.BlockSpec((None, NCH, DH, DH), lambda h: (h, 0, 0, 0))
TM_SPEC = pl.BlockSpec((None, NCH, CH, CH), lambda h: (h, 0, 0, 0))


def _delta_fwd(qkvn, gb, gr, bb):
    def body(q_ref, k_ref, v_ref, gb_ref, gr_ref, bb_ref, o_ref, st_ref, tm_ref):
        ri = lax.broadcasted_iota(jnp.int32, (CH, CH), 0)
        ci = lax.broadcasted_iota(jnp.int32, (CH, CH), 1)
        strict = ri > ci
        causal = ri >= ci
        eye = jnp.where(ri == ci, 1.0, 0.0).astype(F32)

        def step(n, st):
            rows = pl.ds(pl.multiple_of(n * CH, CH), CH)
            q, k, v, g, beta = q_ref[rows, :], k_ref[rows, :], v_ref[rows, :], gb_ref[rows, :], bb_ref[rows, :]
            diff = g[:, :CH] - gr_ref[pl.ds(n, 1), :]
            el = jnp.exp(jnp.where(causal, diff, 0.0))
            eg = jnp.exp(g)
            gl = g[CH - 1:CH, :]
            kb = k * beta
            kbf = k.astype(BF16)
            a = jnp.where(strict, _dot(kb.astype(BF16), kbf, NT) * el, 0.0)
            t = _neumann_inv(a, eye)
            tm_ref[n] = t
            st_ref[n] = st
            sb = st.astype(BF16)
            r = v * beta - _dot((kb * eg).astype(BF16), sb, NN)
            ub = _dot(t, r, NN, precision=HI).astype(BF16)
            p = jnp.where(causal, _dot(q.astype(BF16), kbf, NT) * el, 0.0)
            o_ref[rows, :] = _dot((q * eg).astype(BF16), sb, NN) + _dot(p.astype(BF16), ub, NN)
            kd = k * jnp.exp(gl - g)
            return st * jnp.exp(gl) + _dot(kd.astype(BF16), ub, TN)

        lax.fori_loop(0, NCH, step, jnp.zeros((DH, DH), F32))

    q, k, v, gbs, grs = _head_specs()
    return _call(body, "delta_fwd", [_sds((S, DNW)), _sds((H, NCH, DH, DH)), _sds((H, NCH, CH, CH))], grid=(H,),
                 in_specs=[q, k, v, gbs, grs, gbs], out_specs=[pl.BlockSpec((S, DH), lambda h: (0, h)), ST_SPEC, TM_SPEC],
                 sem=("parallel",))(qkvn, qkvn, qkvn, gb, gr, bb)


def _delta_bwd(qkvn, gb, gr, bb, st_all, tm_all, do_all):
    def body(q_ref, k_ref, v_ref, gb_ref, gr_ref, bb_ref, st_ref, tm_ref, do_ref,
             dq_ref, dk_ref, dv_ref, dg_ref, db_ref):
        ri = lax.broadcasted_iota(jnp.int32, (CH, CH), 0)
        ci = lax.broadcasted_iota(jnp.int32, (CH, CH), 1)
        lo_s, lo_c, up_s, up_c = ri > ci, ri >= ci, ri < ci, ri <= ci
        last_row = lax.broadcasted_iota(jnp.int32, (CH, 1), 0) == CH - 1

        def rs(mat):
            return jnp.sum(mat, axis=1, keepdims=True)

        def total(mat):
            return jnp.sum(rs(mat), axis=0, keepdims=True)

        def step(i, ds):
            n = NCH - 1 - i
            rows = pl.ds(pl.multiple_of(n * CH, CH), CH)
            q, k, v, g, beta = q_ref[rows, :], k_ref[rows, :], v_ref[rows, :], gb_ref[rows, :], bb_ref[rows, :]
            do = do_ref[rows, :]
            t = tm_ref[n]
            st = st_ref[n]
            diff = g[:, :CH] - gr_ref[pl.ds(n, 1), :]
            el = jnp.exp(jnp.where(lo_c, diff, 0.0))
            eu = jnp.exp(jnp.where(up_c, -diff, 0.0))
            eg = jnp.exp(g)
            gl = g[CH - 1:CH, :]
            egl = jnp.exp(gl)
            ekd = jnp.exp(gl - g)
            kb = k * beta
            kbg = kb * eg
            qg = q * eg
            kd = k * ekd
            qb, kbf, kbb = q.astype(BF16), k.astype(BF16), kb.astype(BF16)
            kbgb, qgb, kdb = kbg.astype(BF16), qg.astype(BF16), kd.astype(BF16)
            sb, dob, dsb = st.astype(BF16), do.astype(BF16), ds.astype(BF16)
            r = v * beta - _dot(kbgb, sb, NN)
            u = _dot(t, r, NN, precision=HI)
            ub = u.astype(BF16)
            kk, qk = _dot(kbb, kbf, NT), _dot(qb, kbf, NT)
            kkt, qkt = _dot(kbf, kbb, NT), _dot(kbf, qb, NT)
            pt = jnp.where(up_c, qkt * eu, 0.0)
            du = _dot(pt.astype(BF16), dob, NN) + _dot(kdb, dsb, NN)
            dr = _dot(t, du, TN, precision=HI)
            drb = dr.astype(BF16)
            dpg = jnp.where(lo_c, _dot(dob, ub, NT), 0.0) * el
            dpgt = jnp.where(up_c, _dot(ub, dob, NT), 0.0) * eu
            dag = -jnp.where(lo_s, _dot(drb, ub, NT), 0.0) * el
            dagt = -jnp.where(up_s, _dot(ub, drb, NT), 0.0) * eu
            dqg = _dot(dob, sb, NT)
            dkbg = -_dot(drb, sb, NT)
            dkd = _dot(ub, dsb, NT)
            ds_new = _dot(qgb, dob, TN) + egl * ds - _dot(kbgb, drb, TN)
            dkb = _dot(dag.astype(BF16), kbf, NN) + dkbg * eg
            dk = (_dot(dagt.astype(BF16), kbb, NN) + _dot(dpgt.astype(BF16), qb, NN) + dkd * ekd + dkb * beta)
            dq = _dot(dpg.astype(BF16), kbf, NN) + dqg * eg
            dkd_kd = rs(dkd * kd)
            dg = (rs(dag * kk + dpg * qk) - rs(dagt * kkt + dpgt * qkt) + rs(dqg * qg) + rs(dkbg * kbg) - dkd_kd)
            dgl = jnp.sum(dkd_kd, axis=0, keepdims=True) + egl[:, 0:1] * total(ds * st)
            dg = dg + jnp.where(last_row, dgl, 0.0)
            dbeta = rs(dkb * k) + rs(dr * v)
            dq_ref[rows, :] = dq
            dk_ref[rows, :] = dk
            dv_ref[rows, :] = dr * beta
            spread = jnp.full((8, DH), 1.0 / DH, F32)
            dg_ref[pl.ds(n, 1), :] = _dot(spread, jnp.broadcast_to(dg, (CH, DH)), NT, precision=HI)[0:1, :]
            db_ref[pl.ds(n, 1), :] = _dot(spread, jnp.broadcast_to(dbeta, (CH, DH)), NT, precision=HI)[0:1, :]
            return ds_new

        lax.fori_loop(0, NCH, step, jnp.zeros((DH, DH), F32))

    q, k, v, gbs, grs = _head_specs()
    hcol = pl.BlockSpec((S, DH), lambda h: (0, h))
    return _call(body, "delta_bwd",
                 [_sds((S, DNW)), _sds((S, DNW)), _sds((S, DNW)), _sds((H, NCH, CH)), _sds((H, NCH, CH))], grid=(H,),
                 in_specs=[q, k, v, gbs, grs, gbs, ST_SPEC, TM_SPEC, hcol], out_specs=[hcol, hcol, hcol, grs, grs],
                 sem=("parallel",))(qkvn, qkvn, qkvn, gb, gr, bb, st_all, tm_all, do_all)


Z_BLK = O_Z // DNW


def _dn_post_fwd(o, proj, gn):
    def body(o_ref, z_ref, gn_ref, og_ref):
        for h in range(H):
            cols = slice(h * DH, (h + 1) * DH)
            ov = o_ref[:, cols]
            on = ov * lax.rsqrt(jnp.mean(ov * ov, axis=-1, keepdims=True) + EPS) * gn_ref[...]
            og_ref[:, cols] = (on * _silu(z_ref[:, cols])).astype(BF16)

    return _call(body, "dn_post_fwd", _sds((S, DNW), BF16), grid=(S // TS,),
                 in_specs=[_row_spec(DNW), pl.BlockSpec((TS, DNW), lambda i: (i, Z_BLK)), _vec_spec(DH)],
                 out_specs=_row_spec(DNW), sem=("parallel",))(o, proj, gn)


def _dn_post_bwd(dog, o, proj, gn, dproj):
    def body(d_ref, o_ref, z_ref, gn_ref, dproj_in, do_ref, dz_ref, acc_ref):
        dgn = jnp.zeros((1, DH), F32)
        for h in range(H):
            cols = slice(h * DH, (h + 1) * DH)
            ov, zv, dv = o_ref[:, cols], z_ref[:, cols], d_ref[:, cols]
            rinv = lax.rsqrt(jnp.mean(ov * ov, axis=-1, keepdims=True) + EPS)
            xn = ov * rinv
            don = dv * _silu(zv)
            dz_ref[:, cols] = (dv * xn * gn_ref[...] * _dsilu(zv)).astype(BF16)
            dgn = dgn + jnp.sum(don * xn, axis=0, keepdims=True)
            dxn = don * gn_ref[...]
            do_ref[:, cols] = rinv * (dxn - xn * jnp.mean(dxn * xn, axis=-1, keepdims=True))
        _acc_rows(acc_ref, [dgn])

    zspec = pl.BlockSpec((TS, DNW), lambda i: (i, Z_BLK))
    return _call(body, "dn_post_bwd", [_sds((S, DNW)), _sds((S, NINP), BF16), _sds((8, DH))], grid=(S // TS,),
                 in_specs=[_row_spec(DNW), _row_spec(DNW), zspec, _vec_spec(DH), ANY],
                 out_specs=[_row_spec(DNW), zspec, _acc_spec(DH)], sem=("arbitrary",),
                 aliases={4: 1})(dog, o, proj, gn, dproj)


GA_BLK = O_GA // D
GB_BLK = O_GB // D


def _merge_fwd(ba, bb, proj):
    def body(a_ref, b_ref, ga_ref, gb_ref, o_ref):
        o_ref[...] = (_sigmoid(ga_ref[...]) * a_ref[...] + _sigmoid(gb_ref[...]) * b_ref[...]).astype(BF16)

    return _call(body, "merge_fwd", _sds((S, D), BF16), grid=(S // TS,),
                 in_specs=[_row_spec(), _row_spec(), pl.BlockSpec((TS, D), lambda i: (i, GA_BLK)),
                           pl.BlockSpec((TS, D), lambda i: (i, GB_BLK))],
                 out_specs=_row_spec(), sem=("parallel",))(ba, bb, proj, proj)


def _merge_bwd(dm, ba, bb, proj, dproj):
    def body(d_ref, a_ref, b_ref, ga_ref, gb_ref, dproj_in, dg_ref, da_ref, db_ref):
        d = d_ref[...]
        sa, sb = _sigmoid(ga_ref[...]), _sigmoid(gb_ref[...])
        dg_ref[:, 0:D] = (d * a_ref[...] * sa * (1.0 - sa)).astype(BF16)
        dg_ref[:, D:2 * D] = (d * b_ref[...] * sb * (1.0 - sb)).astype(BF16)
        da_ref[...] = (d * sa).astype(BF16)
        db_ref[...] = (d * sb).astype(BF16)

    return _call(body, "merge_bwd", [_sds((S, NINP), BF16), _sds((S, D), BF16), _sds((S, D), BF16)], grid=(S // TS,),
                 in_specs=[_row_spec(), _row_spec(), _row_spec(), pl.BlockSpec((TS, D), lambda i: (i, GA_BLK)),
                           pl.BlockSpec((TS, D), lambda i: (i, GB_BLK)), ANY],
                 out_specs=[pl.BlockSpec((TS, 2 * D), lambda i: (i, O_GA // (2 * D))), _row_spec(), _row_spec()],
                 sem=("parallel",), aliases={5: 0})(dm, ba, bb, proj, proj, dproj)


NSH = NIN // NDEV


def _win_pieces():
    pieces = [(0, 0, 4096), (O_GA, 6160, 2 * D), (O_SM, 4096, 16)]
    for j in range(CFW // LANE):
        pieces.append((O_GLU + 2 * LANE * j, 4112 + LANE * j, LANE))
        pieces.append((O_GLU + 2 * LANE * j + LANE, 4112 + CFW + LANE * j, LANE))
    return pieces


def _take_cols(src, lo, hi):
    if src.ndim == 2:
        return [src[:, lo:hi]]
    out = []
    while lo < hi:
        j = lo // NSH
        stop = min(hi, (j + 1) * NSH)
        out.append(src[j, :, lo - j * NSH:stop - j * NSH])
        lo = stop
    return out


def _pad_win(src):
    cols = []
    for _, o, wdt in sorted(_win_pieces()):
        cols += _take_cols(src, o, o + wdt)
    cols.append(jnp.zeros((D, NINP - NIN), src.dtype))
    return jnp.concatenate(cols, axis=1)


def _unpad_win(gp, shards=True):
    by_orig = sorted(_win_pieces(), key=lambda t: t[1])

    def take(lo, hi):
        out = []
        for p, o, wdt in by_orig:
            a, b = max(lo, o), min(hi, o + wdt)
            if a < b:
                out.append(gp[:, p + a - o:p + b - o])
        return jnp.concatenate(out, axis=1)

    if not shards:
        return take(0, NIN)
    return jnp.stack([take(j * NSH, (j + 1) * NSH) for j in range(NDEV)])


def _lane_vec(v8, offset):
    return jnp.pad(v8, ((0, 0), (offset, LANE - 8 - offset)))


def _tie(vec, token):
    return vec + token


def _local_step(x, tgt, mod, norm1_g, norm2_g, final_g, w_in_p, dn_conv_w, a_log, dt_bias, dn_norm_g,
                cf_conv_w, cf_ln_g, cf_ln_b, ffn_conv_w, comm):
    sh1, sc1, gt1, sh2, sc2, gt2 = (mod[:, i * D:(i + 1) * D] for i in range(6))
    alog_v, dtb_v = _lane_vec(a_log, H), _lane_vec(dt_bias, H)

    hn1 = _norm_mod(x, norm1_g, sc1, _tie(sh1, comm.token0), "norm_mod1")
    proj = _mm(hn1, w_in_p, "nn", F32, "mm_in", tn=1152)
    qkvn = _dn_pre_fwd(proj, dn_conv_w)
    gates = _gates_fwd(proj, alog_v, dtb_v)
    beta_t = gates[:, 0:H].T
    g_t = gates[:, H:2 * H].T
    gb = jnp.broadcast_to(g_t[:, :, None], (H, S, DH))
    bb = jnp.broadcast_to(beta_t[:, :, None], (H, S, DH))
    gr = g_t.reshape(H, NCH, CH)
    o, st_all, tm_all = _delta_fwd(qkvn, gb, gr, bb)
    og = _dn_post_fwd(o, proj, dn_norm_g)
    u1 = _cf_conv_fwd(proj, cf_conv_w)
    u3 = _cf_ln_fwd(u1, cf_ln_g, cf_ln_b)
    after = og[0:8, 0:LANE].astype(F32) + u3[0:8, 0:LANE].astype(F32)
    dn_w_o, cf_w_o, w_out, ffn_w_up, ffn_w_down = comm.late_weights(after)
    br_a = _mm(og, dn_w_o, "nn", F32, "mm_dn_o")
    br_b = _mm(u3, cf_w_o, "nn", F32, "mm_cf_o")
    merged = _merge_fwd(br_a, br_b, proj)
    mix = _mm(merged, w_out, "nn", F32, "mm_out")
    x2, hn2 = _resid_norm_mod(x, mix, gt1, norm2_g, sc2, sh2, "resid_norm_mod2")
    upall = _mm(hn2, ffn_w_up, "nn", F32, "mm_up")
    hmid = _ffn_mid_fwd(upall, ffn_conv_w)
    f = _mm(hmid, ffn_w_down, "nn", F32, "mm_down")

    dx3, df, acc_f = _loss_head(x2, f, tgt, gt2, final_g)
    d_final_g, d_gt2, loss = acc_f[0:1], acc_f[1:2], acc_f[2:3, 0:1]
    dhmid = _mm(df, ffn_w_down, "nt", F32, "mm_down_dx")
    g_w_down = _mm(hmid, df, "tn", BF16, "mm_down_dw")
    d_gate, d_up, g_ffn_conv = _ffn_mid_bwd(dhmid, upall, ffn_conv_w)
    dhn2 = _mm(d_gate, ffn_w_up, "nt", F32, "mm_up_dx", a2=d_up)
    g_w_up = _mm(hn2, d_gate, "tn", BF16, "mm_up_dw", tn=512, b2=d_up)
    tok_a = comm.grads("a", dict(ffn_w_down=g_w_down, ffn_w_up=g_w_up))
    dx2, dmix, acc2 = _norm_mod_bwd(dhn2, x2, dx3, _tie(norm2_g, tok_a), sc2, "norm_mod2_bwd", mix=mix, gt=gt1)
    d_sh2, d_sc2, d_norm2_g, d_gt1 = acc2[0:1], acc2[1:2], acc2[2:3], acc2[3:4]
    dmerged = _mm(dmix, w_out, "nt", F32, "mm_out_dx")
    g_w_out = _mm(merged, dmix, "tn", BF16, "mm_out_dw")
    d_proj, d_bra, d_brb = _merge_bwd(dmerged, br_a, br_b, proj, lax.empty((S, NINP), BF16))
    du3 = _mm(d_brb, cf_w_o, "nt", F32, "mm_cf_o_dx")
    g_cf_w_o = _mm(u3, d_brb, "tn", BF16, "mm_cf_o_dw")
    du1, acc_ln = _cf_ln_bwd(du3, u1, cf_ln_g, cf_ln_b)
    d_proj, g_cf_conv = _cf_conv_bwd(du1, proj, cf_conv_w, d_proj)
    dog = _mm(d_bra, dn_w_o, "nt", F32, "mm_dn_o_dx")
    g_dn_w_o = _mm(og, d_bra, "tn", BF16, "mm_dn_o_dw")
    tok_b = comm.grads("b", dict(w_out=g_w_out, cf_w_o=g_cf_w_o, dn_w_o=g_dn_w_o, ffn_conv_w=g_ffn_conv,
                                 cf_conv_w=g_cf_conv))
    do, d_proj, acc_gn = _dn_post_bwd(dog, o, proj, _tie(dn_norm_g, tok_b), d_proj)
    dq, dk, dv, dgr, dbr = _delta_bwd(qkvn, gb, gr, bb, st_all, tm_all, do)
    d_proj, g_dn_conv = _dn_pre_bwd(dq, dk, dv, proj, dn_conv_w, d_proj)
    dgates = jnp.concatenate([dbr.reshape(H, S).T, dgr.reshape(H, S).T, jnp.zeros((S, LANE - 2 * H), F32)], axis=1)
    d_proj, acc_g = _gates_bwd(dgates, proj, alog_v, dtb_v, d_proj)
    dhn1 = _mm(d_proj, w_in_p, "nt", F32, "mm_in_dx", tk=1152)
    g_w_in_p = _mm(hn1, d_proj, "tn", BF16, "mm_in_dw", tn=1152)
    tok_c = comm.grads("c", dict(w_in=g_w_in_p, dn_conv_w=g_dn_conv))
    grad_x, acc1 = _norm_mod_bwd(dhn1, x, dx2, _tie(norm1_g, tok_c), sc1, "norm_mod1_bwd")
    d_sh1, d_sc1, d_norm1_g = acc1[0:1], acc1[1:2], acc1[2:3]

    d_mod = jnp.concatenate([d_sh1, d_sc1, d_gt1, d_sh2, d_sc2, d_gt2], axis=1)
    small = dict(mod=d_mod, norm1_g=d_norm1_g, norm2_g=d_norm2_g, final_norm_g=d_final_g,
                 cf_ln_g=acc_ln[0:1], cf_ln_b=acc_ln[1:2], dn_norm_g=acc_gn[0:1],
                 dn_a_log=acc_g[0:1, H:2 * H], dn_dt_bias=acc_g[1:2, H:2 * H])
    return loss, grad_x, small


def _dev_index(px, py, pc):
    return 4 * px + 2 * py + pc


def _all_gather(arrs, name):
    n = len(arrs)

    def body(*refs):
        ins, outs = refs[:n], refs[n:2 * n]
        send_sems, recv_sems, loc_sems = refs[2 * n:]
        x, y, c = _my_pos()
        me, sib = (x, y, c), (x, y, 1 - c)
        chips = [(1 - x, y), (x, 1 - y), (1 - x, 1 - y)]

        def cp(i, k, block, to, src=None):
            dst = outs[i].at[_dev_index(*block)]
            return pltpu.make_async_remote_copy(
                src_ref=dst if src is None else src, dst_ref=dst, send_sem=send_sems.at[i, k],
                recv_sem=recv_sems.at[i, k], device_id=to, device_id_type=MESH)

        mine = [pltpu.make_async_copy(ins[i], outs[i].at[_dev_index(*me)], loc_sems.at[i]) for i in range(n)]
        for m in mine:
            m.start()
        sent = []
        for i in range(n):
            sent.append(cp(i, 0, me, sib, src=ins[i]))
            sent += [cp(i, 1 + j, me, (*chip, c), src=ins[i]) for j, chip in enumerate(chips)]
        for s in sent:
            s.start()
        for i in range(n):
            for j, chip in enumerate(chips):
                cp(i, 1 + j, (*chip, c), me).wait_recv()
                fwd = cp(i, 4 + j, (*chip, c), sib)
                fwd.start()
                sent.append(fwd)
        for i in range(n):
            cp(i, 0, sib, me).wait_recv()
            for j, chip in enumerate(chips):
                cp(i, 4 + j, (*chip, 1 - c), me).wait_recv()
        for s in sent:
            s.wait_send()
        for m in mine:
            m.wait()

    outs = pl.pallas_call(
        body, out_shape=[_sds((NDEV,) + a.shape, a.dtype) for a in arrs], in_specs=[ANY] * n, out_specs=[ANY] * n,
        scratch_shapes=[pltpu.SemaphoreType.DMA((n, 7)), pltpu.SemaphoreType.DMA((n, 7)), pltpu.SemaphoreType.DMA((n,))],
        name=name)(*arrs)
    return list(outs)


def _slab(ref, layout, idx):
    kind, n = layout
    if kind == "rows":
        return ref.at[pl.ds(pl.multiple_of(idx * n, n), n), :]
    if kind == "cols":
        return ref.at[:, pl.ds(pl.multiple_of(idx * n, n), n)]
    return ref.at[idx]


def _slab_shape(arr, layout):
    kind, n = layout
    if kind == "rows":
        return (n, arr.shape[1])
    if kind == "cols":
        return (arr.shape[0], n)
    return tuple(arr.shape[1:])


def _pair_exchange(parts, layouts, name):
    n = len(parts)

    def body(*refs):
        ins, outs = refs[:n], refs[n:2 * n]
        send_sems, recv_sems = refs[2 * n:]
        x, y, c = _my_pos()
        copies = []
        for i in range(n):
            for q in range(4):
                copies.append(pltpu.make_async_remote_copy(
                    src_ref=_slab(ins[i], layouts[i], 2 * q + (1 - c)), dst_ref=outs[i].at[q],
                    send_sem=send_sems.at[i, q], recv_sem=recv_sems.at[i, q], device_id=(x, y, 1 - c),
                    device_id_type=MESH))
        for cpy in copies:
            cpy.start()
        for cpy in copies:
            cpy.wait()

    outs = pl.pallas_call(
        body, out_shape=[_sds((4,) + _slab_shape(p, lay), p.dtype) for p, lay in zip(parts, layouts)],
        in_specs=[ANY] * n, out_specs=[ANY] * n,
        scratch_shapes=[pltpu.SemaphoreType.DMA((n, 4)), pltpu.SemaphoreType.DMA((n, 4))], name=name)(*parts)
    return list(outs)


HBM = pl.BlockSpec(memory_space=pltpu.HBM)
SEMS = pl.BlockSpec(memory_space=pltpu.SEMAPHORE)
EFFECT = pltpu.SideEffectType.DATAFLOW_SIDE_EFFECTING
TOKEN = jax.ShapeDtypeStruct((8, LANE), F32)


def _hbm(a):
    return pltpu.with_memory_space_constraint(a, pltpu.HBM)


def _gather_ici_copy(shard_ref, buf_ref, layout, send_sems, recv_sems, i, j, me, chip, c):
    return pltpu.make_async_remote_copy(
        src_ref=shard_ref, dst_ref=_slab(buf_ref, layout, me), send_sem=send_sems.at[3 * i + j],
        recv_sem=recv_sems.at[3 * i + j], device_id=(*chip, c), device_id_type=MESH)


def _gather_ici_start(shards, bufs, layouts, after, name):
    n = len(shards)

    def body(*refs):
        sh, bf = refs[:n], refs[n:2 * n]
        send_sems, recv_sems = refs[2 * n + 1], refs[2 * n + 2]
        token = refs[-1]
        x, y, c = _my_pos()
        me = _dev_index(x, y, c)
        for i in range(n):
            for j, chip in enumerate([(1 - x, y), (x, 1 - y), (1 - x, 1 - y)]):
                _gather_ici_copy(sh[i], bf[i], layouts[i], send_sems, recv_sems, i, j, me, chip, c).start()
        token[...] = jnp.zeros_like(token)

    outs = pl.pallas_call(
        body, name=name,
        out_shape=(pltpu.SemaphoreType.DMA((3 * n,)), pltpu.SemaphoreType.DMA((3 * n,)),
                   *[pltpu.HBM(a.shape, a.dtype) for a in shards], *[pltpu.HBM(a.shape, a.dtype) for a in bufs], TOKEN),
        in_specs=[HBM] * (2 * n) + [ANY],
        out_specs=(SEMS, SEMS, *[HBM] * (2 * n), pl.BlockSpec(memory_space=pltpu.VMEM)),
        input_output_aliases={i: 2 + i for i in range(2 * n)},
        compiler_params=pltpu.CompilerParams(has_side_effects=EFFECT),
    )(*[_hbm(a) for a in shards], *[_hbm(a) for a in bufs], after)
    return outs[0], outs[1], list(outs[2:2 + n]), list(outs[2 + n:2 + 2 * n]), outs[-1]


def _gather_ici_wait(send_sems, recv_sems, shards, bufs, layouts, after, name):
    n = len(shards)

    def body(*refs):
        sh, bf = refs[:n], refs[n:2 * n]
        ssem, rsem = refs[2 * n], refs[2 * n + 1]
        x, y, c = _my_pos()
        me = _dev_index(x, y, c)
        for i in range(n):
            for j, chip in enumerate([(1 - x, y), (x, 1 - y), (1 - x, 1 - y)]):
                cp = _gather_ici_copy(sh[i], bf[i], layouts[i], ssem, rsem, i, j, me, chip, c)
                cp.wait_send()
                cp.wait_recv()

    outs = pl.pallas_call(
        body, name=name,
        out_shape=(*[pltpu.HBM(a.shape, a.dtype) for a in shards], *[pltpu.HBM(a.shape, a.dtype) for a in bufs]),
        in_specs=[HBM] * (2 * n) + [SEMS, SEMS, ANY], out_specs=tuple([HBM] * (2 * n)),
        input_output_aliases={i: i for i in range(2 * n)},
        compiler_params=pltpu.CompilerParams(has_side_effects=EFFECT),
    )(*shards, *bufs, send_sems, recv_sems, after)
    return list(outs[:n]), list(outs[n:])


def _gather_pair(shards, bufs, layouts, name):
    n = len(shards)

    def body(*refs):
        sh, bo = refs[:n], refs[2 * n:3 * n]
        send_sems, recv_sems, loc_sems = refs[3 * n:]
        x, y, c = _my_pos()
        me, sib = _dev_index(x, y, c), (x, y, 1 - c)
        chips = [(1 - x, y), (x, 1 - y), (1 - x, 1 - y)]
        copies = []
        for i in range(n):
            mine = _slab(bo[i], layouts[i], me)
            copies.append(pltpu.make_async_copy(sh[i], mine, loc_sems.at[i]))
            copies.append(pltpu.make_async_remote_copy(
                src_ref=sh[i], dst_ref=mine, send_sem=send_sems.at[i, 0], recv_sem=recv_sems.at[i, 0],
                device_id=sib, device_id_type=MESH))
            for j, (px, py) in enumerate(chips):
                slab = _slab(bo[i], layouts[i], _dev_index(px, py, c))
                copies.append(pltpu.make_async_remote_copy(
                    src_ref=slab, dst_ref=slab, send_sem=send_sems.at[i, 1 + j], recv_sem=recv_sems.at[i, 1 + j],
                    device_id=sib, device_id_type=MESH))
        for cpy in copies:
            cpy.start()
        for cpy in copies:
            cpy.wait()

    outs = pl.pallas_call(
        body, out_shape=[_sds(a.shape, a.dtype) for a in bufs], in_specs=[ANY] * (2 * n), out_specs=[ANY] * n,
        input_output_aliases={n + i: i for i in range(n)},
        scratch_shapes=[pltpu.SemaphoreType.DMA((n, 4)), pltpu.SemaphoreType.DMA((n, 4)), pltpu.SemaphoreType.DMA((n,))],
        name=name)(*shards, *bufs)
    return list(outs)


def _chip_copy(sum_ref, land_ref, send_sems, recv_sems, i, j, chip, c):
    return pltpu.make_async_remote_copy(
        src_ref=sum_ref.at[2 * chip[0] + chip[1]], dst_ref=land_ref.at[j], send_sem=send_sems.at[3 * i + j],
        recv_sem=recv_sems.at[3 * i + j], device_id=(*chip, c), device_id_type=MESH)


def _chip_exchange_start(sums, name):
    n = len(sums)
    lands = [lax.empty((3,) + s.shape[1:], s.dtype) for s in sums]

    def body(*refs):
        sm, ld = refs[:n], refs[n:2 * n]
        send_sems, recv_sems = refs[2 * n], refs[2 * n + 1]
        token = refs[-1]
        x, y, c = _my_pos()
        for i in range(n):
            for j, chip in enumerate([(1 - x, y), (x, 1 - y), (1 - x, 1 - y)]):
                _chip_copy(sm[i], ld[i], send_sems, recv_sems, i, j, chip, c).start()
        token[...] = jnp.zeros_like(token)

    outs = pl.pallas_call(
        body, name=name,
        out_shape=(pltpu.SemaphoreType.DMA((3 * n,)), pltpu.SemaphoreType.DMA((3 * n,)),
                   *[pltpu.HBM(a.shape, a.dtype) for a in sums], *[pltpu.HBM(a.shape, a.dtype) for a in lands], TOKEN),
        in_specs=[HBM] * (2 * n), out_specs=(SEMS, SEMS, *[HBM] * (2 * n), pl.BlockSpec(memory_space=pltpu.VMEM)),
        input_output_aliases={i: 2 + i for i in range(2 * n)},
        compiler_params=pltpu.CompilerParams(has_side_effects=EFFECT),
    )(*[_hbm(a) for a in sums], *[_hbm(a) for a in lands])
    return outs[0], outs[1], list(outs[2:2 + n]), list(outs[2 + n:2 + 2 * n]), outs[-1]


def _chip_exchange_wait(send_sems, recv_sems, sums, lands, after, name):
    n = len(sums)

    def body(*refs):
        sm, ld = refs[:n], refs[n:2 * n]
        ssem, rsem = refs[2 * n], refs[2 * n + 1]
        x, y, c = _my_pos()
        for i in range(n):
            for j, chip in enumerate([(1 - x, y), (x, 1 - y), (1 - x, 1 - y)]):
                cp = _chip_copy(sm[i], ld[i], ssem, rsem, i, j, chip, c)
                cp.wait_send()
                cp.wait_recv()

    outs = pl.pallas_call(
        body, name=name,
        out_shape=(*[pltpu.HBM(a.shape, a.dtype) for a in sums], *[pltpu.HBM(a.shape, a.dtype) for a in lands]),
        in_specs=[HBM] * (2 * n) + [SEMS, SEMS, ANY], out_specs=tuple([HBM] * (2 * n)),
        input_output_aliases={i: i for i in range(2 * n)},
        compiler_params=pltpu.CompilerParams(has_side_effects=EFFECT),
    )(*sums, *lands, send_sems, recv_sems, after)
    return list(outs[:n]), list(outs[n:])


def _row_tile(r, itemsize):
    align = 32 // itemsize
    best = r
    for t in range(align, min(r, 256) + 1, align):
        if r % t == 0:
            best = t
    return best


def _prefetch_call(body, name, out_shape, grid, in_specs, out_specs, sem):
    return pl.pallas_call(
        body, out_shape=out_shape, name=name,
        grid_spec=pltpu.PrefetchScalarGridSpec(num_scalar_prefetch=1, grid=grid, in_specs=in_specs, out_specs=out_specs),
        compiler_params=pltpu.CompilerParams(dimension_semantics=sem, vmem_limit_bytes=VMEM_LIMIT))


def _pair_sum(pos, part, got, layout, name):
    kind, _ = layout
    _, r, cols = got.shape
    tr = _row_tile(r, part.dtype.itemsize)
    nr = r // tr
    if kind == "rows":
        pspec = pl.BlockSpec((tr, cols), lambda q, i, p: ((2 * q + p[0]) * nr + i, 0))
    elif kind == "cols":
        pspec = pl.BlockSpec((tr, cols), lambda q, i, p: (i, 2 * q + p[0]))
    else:
        pspec = pl.BlockSpec((None, tr, cols), lambda q, i, p: (2 * q + p[0], i, 0))

    def body(pos_ref, p_ref, g_ref, o_ref):
        o_ref[...] = (p_ref[...].astype(F32) + g_ref[...].astype(F32)).astype(o_ref.dtype)

    return _prefetch_call(body, name, _sds((4, r, cols), part.dtype), (4, nr),
                          [pspec, pl.BlockSpec((None, tr, cols), lambda q, i, p: (q, i, 0))],
                          pl.BlockSpec((None, tr, cols), lambda q, i, p: (q, i, 0)),
                          ("parallel", "parallel"))(pos, part, got)


def _final_sum_adam(pos, sums, got, w, m, v, name):
    _, r, cols = w.shape
    tr = _row_tile(r, sums.dtype.itemsize)

    def body(pos_ref, s_ref, g_ref, w_ref, m_ref, v_ref, go_ref, dl_ref, nm_ref, nv_ref):
        g = ((s_ref[...].astype(F32) + g_ref[0].astype(F32)) + g_ref[1].astype(F32)) + g_ref[2].astype(F32)
        dl, nm, nv = _adam(w_ref[...], g, m_ref[...], v_ref[...])
        go_ref[...] = g
        dl_ref[...] = dl
        nm_ref[...] = nm
        nv_ref[...] = nv

    big = pl.BlockSpec((None, tr, cols), lambda i, p: (0, i, 0))
    return _prefetch_call(body, name, [_sds((1, r, cols))] * 4, (r // tr,),
                          [pl.BlockSpec((None, tr, cols), lambda i, p: (p[1], i, 0)),
                           pl.BlockSpec((3, tr, cols), lambda i, p: (0, i, 0)), big, big, big],
                          [big] * 4, ("parallel",))(pos, sums, got, w, m, v)


def _small_adam(g_all, w, m, v):
    npk = w.shape[1]

    def body(g_ref, w_ref, m_ref, v_ref, go_ref, dl_ref, nm_ref, nv_ref):
        g = g_ref[0:1, :]
        for k in range(1, NDEV):
            g = g + g_ref[k:k + 1, :]
        dl, nm, nv = _adam(w_ref[...], g, m_ref[...], v_ref[...])
        go_ref[...] = g
        dl_ref[...] = dl
        nm_ref[...] = nm
        nv_ref[...] = nv

    return _call(body, "small_adam", [_sds((1, npk))] * 4)(g_all, w, m, v)


SMALL = [("b_ada", 6 * D), ("norm1_g", D), ("norm2_g", D), ("final_norm_g", D), ("cf_ln_g", CFW), ("cf_ln_b", CFW),
         ("dn_norm_g", DH), ("dn_a_log", H), ("dn_dt_bias", H)]
LATE = ["dn_w_o", "cf_w_o", "w_out", "ffn_w_up", "ffn_w_down"]
LATE_SHAPE = {"dn_w_o": (NDEV, DNW, D // NDEV), "cf_w_o": (NDEV, CFW, D // NDEV), "w_out": (D, D),
              "ffn_w_up": (NDEV, D, 2 * FFN // NDEV), "ffn_w_down": (FFN, D)}
LATE_LAYOUT = {"dn_w_o": ("lead", NDEV), "cf_w_o": ("lead", NDEV), "w_out": ("rows", D // NDEV),
               "ffn_w_up": ("lead", NDEV), "ffn_w_down": ("rows", FFN // NDEV)}
LAYOUT = {"dn_w_o": ("cols", D // NDEV), "cf_w_o": ("cols", D // NDEV), "w_out": ("rows", D // NDEV),
          "ffn_w_up": ("cols", 2 * FFN // NDEV), "ffn_w_down": ("rows", FFN // NDEV),
          "w_in": ("lead", NDEV), "dn_conv_w": ("lead", NDEV), "cf_conv_w": ("lead", NDEV), "ffn_conv_w": ("lead", NDEV)}
NAMES = ["w_ada", "b_ada", "norm1_g", "w_in", "dn_conv_w", "dn_a_log", "dn_dt_bias", "dn_norm_g", "dn_w_o", "cf_conv_w",
         "cf_ln_g", "cf_ln_b", "cf_w_o", "w_out", "norm2_g", "ffn_w_up", "ffn_conv_w", "ffn_w_down", "final_norm_g"]


def _pack_small(d):
    rows = []
    for nm, n in SMALL:
        row = d[nm].reshape(1, n)
        pad = (-n) % LANE
        rows.append(jnp.pad(row, ((0, 0), (0, pad))) if pad else row)
    return jnp.concatenate(rows, axis=1)


def _unpack_small(row, shapes):
    out, off = {}, 0
    for nm, n in SMALL:
        out[nm] = row[0, off:off + n].reshape(shapes[nm])
        off += n + ((-n) % LANE)
    return out


def _cols_from_gathered(g):
    return jnp.transpose(g, (1, 0, 2)).reshape(g.shape[1], NDEV * g.shape[2])


def _cols_to_parts(full):
    r, ctot = full.shape
    return jnp.transpose(full.reshape(r, NDEV, ctot // NDEV), (1, 0, 2))


def kernel(x, c, w_ada, b_ada, norm1_g, w_in, dn_conv_w, dn_a_log, dn_dt_bias, dn_norm_g, dn_w_o, cf_conv_w, cf_ln_g, cf_ln_b, cf_w_o, w_out, norm2_g, ffn_w_up, ffn_conv_w, ffn_w_down, final_norm_g, loss_target, m_w_ada, m_b_ada, m_norm1_g, m_w_in, m_dn_conv_w, m_dn_a_log, m_dn_dt_bias, m_dn_norm_g, m_dn_w_o, m_cf_conv_w, m_cf_ln_g, m_cf_ln_b, m_cf_w_o, m_w_out, m_norm2_g, m_ffn_w_up, m_ffn_conv_w, m_ffn_w_down, m_final_norm_g, v_w_ada, v_b_ada, v_norm1_g, v_w_in, v_dn_conv_w, v_dn_a_log, v_dn_dt_bias, v_dn_norm_g, v_dn_w_o, v_cf_conv_w, v_cf_ln_g, v_cf_ln_b, v_cf_w_o, v_w_out, v_norm2_g, v_ffn_w_up, v_ffn_conv_w, v_ffn_w_down, v_final_norm_g):
    args = locals()
    w = {nm: args[nm] for nm in NAMES}
    mo = {nm: args["m_" + nm] for nm in NAMES}
    vo = {nm: args["v_" + nm] for nm in NAMES}
    shapes = {nm: w[nm].shape for nm in NAMES}
    px, py, pc = _my_pos()
    me = _dev_index(px, py, pc)

    def mat(a):
        return a.reshape(a.shape[-2:])

    pos = jnp.stack([pc, 2 * px + py]).astype(jnp.int32)

    first = ["w_in", "dn_conv_w", "cf_conv_w", "ffn_conv_w"]
    got = _all_gather([w["w_in"].astype(BF16)] + [mat(w[nm]) for nm in first[1:]] + [c], "gather_first")
    full = {nm: _cols_from_gathered(g) for nm, g in zip(first[1:], got[1:-1])}
    c_all = got[-1].reshape(NDEV, D)
    w_in_p = _pad_win(got[0].reshape(NDEV, D, NSH))

    late_shards = [mat(w[nm]).astype(BF16) for nm in LATE]
    late_bufs = [lax.empty(LATE_SHAPE[nm], BF16) for nm in LATE]
    late_lay = [LATE_LAYOUT[nm] for nm in LATE]
    l_send, l_recv, l_shards, l_bufs, l_token = _gather_ici_start(late_shards, late_bufs, late_lay, c_all, "gather_late_start")

    res = {}

    class Comm:
        token0 = l_token[0, 0]
        pending = {}

        @staticmethod
        def late_weights(after):
            shards, bufs = _gather_ici_wait(l_send, l_recv, l_shards, l_bufs, late_lay, after, "gather_late_wait")
            return _gather_pair(shards, bufs, late_lay, "gather_late_pair")

        @staticmethod
        def grads(group, gd):
            names = list(gd)
            lays = [LAYOUT[nm] for nm in names]
            gl = []
            for nm in names:
                if nm == "w_in":
                    gl.append(_unpad_win(gd[nm]))
                else:
                    gl.append(_cols_to_parts(gd[nm]) if LAYOUT[nm][0] == "lead" else gd[nm])
            from_sib = _pair_exchange(gl, lays, "rs_pair_" + group)
            sums = [_pair_sum(pos, g, r, lay, "rs_pair_sum_" + nm) for nm, g, r, lay in zip(names, gl, from_sib, lays)]
            started = _chip_exchange_start(sums, "rs_chips_start_" + group)
            Comm.pending[group] = (names,) + tuple(started[:4])
            return started[4][0, 0]

        @staticmethod
        def finish(group, after):
            names, ssem, rsem, sums, lands = Comm.pending[group]
            sums, lands = _chip_exchange_wait(ssem, rsem, sums, lands, after, "rs_chips_wait_" + group)
            for nm, s, r in zip(names, sums, lands):
                res[nm] = _final_sum_adam(pos, s, r, w[nm], mo[nm], vo[nm], "adam_" + nm)
            return res[names[-1]][0]

    ncol = 6 * D // NDEV
    b_sh = lax.dynamic_slice(b_ada.reshape(1, 6 * D), (0, me * ncol), (1, ncol))
    mod_sh = _ada_fwd(c_all, mat(w_ada), b_sh)
    mod_all = _all_gather([mod_sh], "gather_mod")[0]
    mod = lax.dynamic_index_in_dim(mod_all, me, axis=1, keepdims=False).reshape(1, 6 * D)

    vec = lambda a: a.reshape(1, -1)
    loss, grad_x, small = _local_step(
        x.reshape(S, D), loss_target.reshape(S, D), mod, vec(norm1_g), vec(norm2_g), vec(final_norm_g), w_in_p,
        full["dn_conv_w"], vec(dn_a_log), vec(dn_dt_bias), vec(dn_norm_g), full["cf_conv_w"], vec(cf_ln_g),
        vec(cf_ln_b), full["ffn_conv_w"], Comm)

    done_a = Comm.finish("a", grad_x)
    done_b = Comm.finish("b", done_a)

    small["b_ada"] = small.pop("mod")
    g_small = _all_gather([_pack_small(small)], "gather_small")[0].reshape(NDEV, -1)
    outs = _small_adam(g_small, _pack_small({nm: w[nm] for nm, _ in SMALL}), _pack_small({nm: mo[nm] for nm, _ in SMALL}),
                       _pack_small({nm: vo[nm] for nm, _ in SMALL}))
    unpacked = [_unpack_small(o, shapes) for o in outs]
    for nm, _ in SMALL:
        res[nm] = [u[nm] for u in unpacked]

    dmod_sel = lax.dynamic_slice(g_small[:, :6 * D], (0, me * ncol), (NDEV, ncol))
    outs = _ada_bwd_adam(c_all, dmod_sel, mat(w_ada), mat(m_w_ada), mat(v_w_ada))
    res["w_ada"] = [o.reshape(shapes["w_ada"]) for o in outs]
    Comm.finish("c", jnp.concatenate([done_b.reshape(-1)[:LANE], outs[0].reshape(-1)[:LANE]]))

    loss = lax.psum(loss.reshape(()), ("x", "y", "c"))
    out = [loss, grad_x.reshape(x.shape)]
    for k in range(4):
        out += [res[nm][k] for nm in NAMES]
    return tuple(out)
```

```python
import functools

import jax
import jax.numpy as jnp
from jax import lax
from jax.experimental import pallas as pl
from jax.experimental.pallas import tpu as pltpu

F32 = jnp.float32
BF16 = jnp.bfloat16
HI = lax.Precision.HIGHEST
MESH = pl.DeviceIdType.MESH
ANY = pl.BlockSpec(memory_space=pl.ANY)

NDEV = 8
D = 2048
S = 2048
H = 8
DH = 128
DNW = H * DH
CFW = 1024
CFK = 31
DNK = 4
FFN = 5632
FFK = 3
CH = 64
NCH = S // CH
EPS = 1e-6
NIN = 10256
NINP = 10368
O_Z, O_GA, O_GB, O_GLU, O_SM = 3072, 4096, 6144, 8192, 10240
LANE = 128
TS = 256
VMEM_LIMIT = 56 * 1024 * 1024

ADAM_LR, ADAM_B1, ADAM_B2, ADAM_EPS, ADAM_WD, ADAM_STEP = 0.001, 0.9, 0.999, 1e-08, 0.01, 10


def _call(body, name, out_shape, grid=(), in_specs=None, out_specs=None, scratch=(), sem=None, aliases=None):
    kw = {}
    if aliases:
        kw["input_output_aliases"] = aliases
    if in_specs is not None:
        kw["in_specs"] = in_specs
    if out_specs is not None:
        kw["out_specs"] = out_specs
    return pl.pallas_call(
        body, out_shape=out_shape, grid=grid, scratch_shapes=scratch, name=name,
        compiler_params=pltpu.CompilerParams(dimension_semantics=sem, vmem_limit_bytes=VMEM_LIMIT), **kw)


def _sds(shape, dtype=F32):
    return jax.ShapeDtypeStruct(shape, dtype)


def _tile(dim, pref):
    if dim <= pref:
        return dim
    best = None
    for t in range(LANE, pref + 1, LANE):
        if dim % t == 0:
            best = t
    assert best is not None, (dim, pref)
    return best


def _sigmoid(x):
    return 1.0 / (1.0 + jnp.exp(-x))


def _silu(x):
    return x * _sigmoid(x)


def _dsilu(x):
    s = _sigmoid(x)
    return s * (1.0 + x * (1.0 - s))


def _softplus(x):
    return jnp.maximum(x, 0.0) + jnp.log(1.0 + jnp.exp(-jnp.abs(x)))


def _dot(a, b, dims, precision=None):
    return lax.dot_general(a, b, (dims, ((), ())), preferred_element_type=F32, precision=precision)


NN = ((1,), (0,))
NT = ((1,), (1,))
TN = ((0,), (0,))


def _my_pos():
    return lax.axis_index("x"), lax.axis_index("y"), lax.axis_index("c")


def _mm(a, b, mode, out_dtype, name, tm=1024, tn=1024, tk=2048, a2=None, b2=None):
    sharded = b.ndim == 3
    if sharded and mode == "nn":
        cs = b.shape[2]
        (m, k), n = a.shape, NDEV * cs
        gs = max(1, tn // cs)
        tm, tn, tk = _tile(m, tm), gs * cs, _tile(k, tk)
    elif sharded:
        assert mode == "nt"
        cs = b.shape[2]
        m, n, k = a.shape[0], b.shape[1], NDEV * cs
        gs = max(1, tk // cs)
        tm, tn, tk = _tile(m, tm), _tile(n, tn), gs * cs
    else:
        if mode == "nn":
            (m, k), (k2, n) = a.shape, b.shape
        elif mode == "nt":
            (m, k), (n, k2) = a.shape, b.shape
        else:
            (k, m), (k2, n) = a.shape, b.shape
        assert k == k2, (a.shape, b.shape, mode)
        n = n * (2 if b2 is not None else 1)
        tm, tn, tk = _tile(m, tm), _tile(n // (2 if b2 is not None else 1), tn), _tile(k, tk)
    nk, nj = k // tk, n // tn
    halfk, halfj = nk // 2, nj // 2
    dims = {"nn": NN, "nt": NT, "tn": TN}[mode]

    def body(*refs):
        a_ref, b_ref = refs[0], refs[1]
        x_ref = refs[2] if (a2 is not None or b2 is not None) else None
        o_ref, acc_ref = refs[-2], refs[-1]
        j, kk = pl.program_id(1), pl.program_id(2)

        def accumulate(part, cols=slice(None)):
            @pl.when(kk == 0)
            def _():
                acc_ref[:, cols] = part

            @pl.when(kk > 0)
            def _():
                acc_ref[:, cols] += part

        if sharded and mode == "nn":
            for q in range(gs):
                accumulate(_dot(a_ref[...], b_ref[q], NN), slice(q * cs, (q + 1) * cs))
        elif sharded:
            def contract(lhs_ref):
                part = None
                for q in range(gs):
                    term = _dot(lhs_ref[:, q * cs:(q + 1) * cs], b_ref[q], NT)
                    part = term if part is None else part + term
                accumulate(part)

            if a2 is None:
                contract(a_ref)
            else:
                pl.when(kk < halfk)(lambda: contract(a_ref))
                pl.when(kk >= halfk)(lambda: contract(x_ref))
        elif b2 is not None:
            pl.when(j < halfj)(lambda: accumulate(_dot(a_ref[...], b_ref[...], dims)))
            pl.when(j >= halfj)(lambda: accumulate(_dot(a_ref[...], x_ref[...], dims)))
        else:
            accumulate(_dot(a_ref[...], b_ref[...], dims))

        @pl.when(kk == nk - 1)
        def _():
            o_ref[...] = acc_ref[...].astype(o_ref.dtype)

    ins, in_specs = [a], []
    if mode == "tn":
        in_specs.append(pl.BlockSpec((tk, tm), lambda i, j, kk: (kk, i)))
    elif a2 is not None:
        in_specs.append(pl.BlockSpec((tm, tk), lambda i, j, kk: (i, jnp.minimum(kk, halfk - 1))))
    else:
        in_specs.append(pl.BlockSpec((tm, tk), lambda i, j, kk: (i, kk)))
    ins.append(b)
    if sharded and mode == "nn":
        in_specs.append(pl.BlockSpec((gs, tk, cs), lambda i, j, kk: (j, kk, 0)))
    elif sharded:
        in_specs.append(pl.BlockSpec((gs, tn, cs), lambda i, j, kk: (kk, j, 0)))
    elif mode == "nt":
        in_specs.append(pl.BlockSpec((tn, tk), lambda i, j, kk: (j, kk)))
    elif b2 is not None:
        in_specs.append(pl.BlockSpec((tk, tn), lambda i, j, kk: (kk, jnp.minimum(j, halfj - 1))))
    else:
        in_specs.append(pl.BlockSpec((tk, tn), lambda i, j, kk: (kk, j)))
    if a2 is not None:
        ins.append(a2)
        in_specs.append(pl.BlockSpec((tm, tk), lambda i, j, kk: (i, jnp.maximum(kk - halfk, 0))))
    if b2 is not None:
        ins.append(b2)
        in_specs.append(pl.BlockSpec((tk, tn), lambda i, j, kk: (kk, jnp.maximum(j - halfj, 0))))
    return _call(body, name, _sds((m, n), out_dtype), grid=(m // tm, nj, nk),
                 in_specs=in_specs, out_specs=pl.BlockSpec((tm, tn), lambda i, j, kk: (i, j)),
                 scratch=[pltpu.VMEM((tm, tn), F32)], sem=("parallel", "parallel", "arbitrary"))(*ins)


def _ada_fwd(c_all, w_sh, b_sh):
    n = w_sh.shape[1]
    tn = 512

    def body(c_ref, w_ref, b_ref, o_ref):
        ca = _silu(c_ref[...]).astype(BF16)
        o_ref[...] = _dot(ca, w_ref[...].astype(BF16), NN) + b_ref[...]

    return _call(body, "ada_fwd", _sds((NDEV, n)), grid=(n // tn,),
                 in_specs=[pl.BlockSpec((NDEV, D), lambda j: (0, 0)), pl.BlockSpec((D, tn), lambda j: (0, j)),
                           pl.BlockSpec((1, tn), lambda j: (0, j))],
                 out_specs=pl.BlockSpec((NDEV, tn), lambda j: (0, j)), sem=("parallel",))(c_all, w_sh, b_sh)


def _adam(w, g, m, v):
    m = ADAM_B1 * m + (1.0 - ADAM_B1) * g
    v = ADAM_B2 * v + (1.0 - ADAM_B2) * (g * g)
    m_hat = m / (1.0 - ADAM_B1 ** ADAM_STEP)
    v_hat = v / (1.0 - ADAM_B2 ** ADAM_STEP)
    delta = -ADAM_LR * (m_hat / (jnp.sqrt(v_hat) + ADAM_EPS) + ADAM_WD * w)
    return delta, m, v


def _ada_bwd_adam(c_all, dmod_sel, w, m, v):
    r, n = w.shape
    tr = 256

    def body(c_ref, d_ref, w_ref, m_ref, v_ref, g_ref, dl_ref, nm_ref, nv_ref):
        ca = _silu(c_ref[...])
        g = _dot(ca, d_ref[...], TN, precision=HI)
        dl, nm, nv = _adam(w_ref[...], g, m_ref[...], v_ref[...])
        g_ref[...] = g
        dl_ref[...] = dl
        nm_ref[...] = nm
        nv_ref[...] = nv

    big = pl.BlockSpec((tr, n), lambda i: (i, 0))
    return _call(body, "ada_bwd_adam", [_sds((r, n))] * 4, grid=(r // tr,),
                 in_specs=[pl.BlockSpec((NDEV, tr), lambda i: (0, i)), pl.BlockSpec((NDEV, n), lambda i: (0, 0)),
                           big, big, big],
                 out_specs=[big] * 4, sem=("parallel",))(c_all, dmod_sel, w, m, v)


def _row_spec(width=D):
    return pl.BlockSpec((TS, width), lambda i: (i, 0))


def _vec_spec(width=D):
    return pl.BlockSpec((1, width), lambda i: (0, 0))


def _acc_spec(width=D):
    return pl.BlockSpec((8, width), lambda i: (0, 0))


def _norm_mod(x, g, sc, sh, name):
    def body(x_ref, g_ref, sc_ref, sh_ref, o_ref):
        xv = x_ref[...]
        r = lax.rsqrt(jnp.mean(xv * xv, axis=-1, keepdims=True) + EPS)
        o_ref[...] = ((xv * r) * g_ref[...] * (1.0 + sc_ref[...]) + sh_ref[...]).astype(BF16)

    return _call(body, name, _sds((S, D), BF16), grid=(S // TS,),
                 in_specs=[_row_spec(), _vec_spec(), _vec_spec(), _vec_spec()], out_specs=_row_spec(),
                 sem=("parallel",))(x, g, sc, sh)


def _resid_norm_mod(x, mix, gt, g, sc, sh, name):
    def body(x_ref, mix_ref, gt_ref, g_ref, sc_ref, sh_ref, x2_ref, o_ref):
        xv = x_ref[...] + gt_ref[...] * mix_ref[...]
        x2_ref[...] = xv
        r = lax.rsqrt(jnp.mean(xv * xv, axis=-1, keepdims=True) + EPS)
        o_ref[...] = ((xv * r) * g_ref[...] * (1.0 + sc_ref[...]) + sh_ref[...]).astype(BF16)

    return _call(body, name, [_sds((S, D)), _sds((S, D), BF16)], grid=(S // TS,),
                 in_specs=[_row_spec(), _row_spec()] + [_vec_spec()] * 4, out_specs=[_row_spec(), _row_spec()],
                 sem=("parallel",))(x, mix, gt, g, sc, sh)


def _acc_rows(acc_ref, rows):
    @pl.when(pl.program_id(0) == 0)
    def _():
        acc_ref[...] = jnp.zeros_like(acc_ref)

    for k, row in enumerate(rows):
        acc_ref[k:k + 1, :] += row


def _loss_head(x2, f, tgt, gt2, gf):
    def body(x2_ref, f_ref, t_ref, gt_ref, gf_ref, dx_ref, df_ref, acc_ref):
        fv = f_ref[...]
        x3 = x2_ref[...] + gt_ref[...] * fv
        r = lax.rsqrt(jnp.mean(x3 * x3, axis=-1, keepdims=True) + EPS)
        xn = x3 * r
        e = xn * gf_ref[...] - t_ref[...]
        loss = 0.5 * jnp.sum(jnp.mean(e * e, axis=-1, keepdims=True), axis=0, keepdims=True)
        dy = e * (1.0 / D)
        dxn = dy * gf_ref[...]
        dx3 = r * (dxn - xn * jnp.mean(dxn * xn, axis=-1, keepdims=True))
        dx_ref[...] = dx3
        df_ref[...] = (dx3 * gt_ref[...]).astype(BF16)
        _acc_rows(acc_ref, [jnp.sum(dy * xn, axis=0, keepdims=True), jnp.sum(dx3 * fv, axis=0, keepdims=True),
                            jnp.broadcast_to(loss, (1, D))])

    return _call(body, "loss_head", [_sds((S, D)), _sds((S, D), BF16), _sds((8, D))], grid=(S // TS,),
                 in_specs=[_row_spec(), _row_spec(), _row_spec(), _vec_spec(), _vec_spec()],
                 out_specs=[_row_spec(), _row_spec(), _acc_spec()], sem=("arbitrary",))(x2, f, tgt, gt2, gf)


def _norm_mod_bwd(dhn, x, dres, g, sc, name, mix=None, gt=None):
    gated = mix is not None

    def body(*refs):
        if gated:
            dhn_ref, x_ref, dres_ref, g_ref, sc_ref, mix_ref, gt_ref, dx_ref, dmix_ref, acc_ref = refs
        else:
            dhn_ref, x_ref, dres_ref, g_ref, sc_ref, dx_ref, acc_ref = refs
        xv = x_ref[...]
        dh = dhn_ref[...]
        r = lax.rsqrt(jnp.mean(xv * xv, axis=-1, keepdims=True) + EPS)
        xn = xv * r
        gv = g_ref[...]
        sc1 = 1.0 + sc_ref[...]
        dxn = dh * gv * sc1
        dx = dres_ref[...] + r * (dxn - xn * jnp.mean(dxn * xn, axis=-1, keepdims=True))
        dx_ref[...] = dx
        rows = [jnp.sum(dh, axis=0, keepdims=True), jnp.sum(dh * xn * gv, axis=0, keepdims=True),
                jnp.sum(dh * xn * sc1, axis=0, keepdims=True)]
        if gated:
            rows.append(jnp.sum(dx * mix_ref[...], axis=0, keepdims=True))
            dmix_ref[...] = (dx * gt_ref[...]).astype(BF16)
        _acc_rows(acc_ref, rows)

    ins = [dhn, x, dres, g, sc]
    in_specs = [_row_spec(), _row_spec(), _row_spec(), _vec_spec(), _vec_spec()]
    outs = [_sds((S, D))]
    out_specs = [_row_spec()]
    if gated:
        ins += [mix, gt]
        in_specs += [_row_spec(), _vec_spec()]
        outs.append(_sds((S, D), BF16))
        out_specs.append(_row_spec())
    outs.append(_sds((8, D)))
    out_specs.append(_acc_spec())
    return _call(body, name, outs, grid=(S // TS,), in_specs=in_specs, out_specs=out_specs,
                 sem=("arbitrary",))(*ins)


RC = 256


def _conv_fwd_rows(pad_ref, w_ref, kw, head, r0):
    acc = None
    for k in range(kw):
        term = w_ref[k:k + 1, :] * pad_ref[pl.ds(head - (kw - 1) + k + r0, RC), :]
        acc = term if acc is None else acc + term
    return acc


def _conv_bwd_rows(pad2_ref, w_ref, kw, r0):
    acc = None
    for k in range(kw):
        term = w_ref[k:k + 1, :] * pad2_ref[pl.ds(kw - 1 - k + r0, RC), :]
        acc = term if acc is None else acc + term
    return acc


def _conv_dw(pad_ref, dout_ref, dw_ref, kw, head):
    for k in range(kw):
        acc = None
        for r0 in range(0, S, RC):
            term = jnp.sum(pad_ref[pl.ds(head - (kw - 1) + k + r0, RC), :] * dout_ref[pl.ds(r0, RC), :],
                           axis=0, keepdims=True)
            acc = term if acc is None else acc + term
        dw_ref[k:k + 1, :] = acc


def _col_spec(width, off_blocks=0):
    return pl.BlockSpec((S, width), lambda j: (0, j + off_blocks))


def _dn_pre_fwd(proj, conv_w):
    head = 8

    def body(x_ref, w_ref, o_ref, pad_ref):
        j = pl.program_id(0)
        pad_ref[pl.ds(0, head), :] = jnp.zeros((head, DH), F32)
        pad_ref[pl.ds(head, S), :] = x_ref[...]
        scale = jnp.where(j < H, DH ** -0.5, 1.0)
        for r0 in range(0, S, RC):
            y = _silu(_conv_fwd_rows(pad_ref, w_ref, DNK, head, r0))
            rinv = lax.rsqrt(jnp.sum(y * y, axis=-1, keepdims=True) + EPS)
            o_ref[pl.ds(r0, RC), :] = jnp.where(j < 2 * H, y * rinv * scale, y)

    return _call(body, "dn_pre_fwd", _sds((S, 3 * DNW)), grid=(3 * H,),
                 in_specs=[_col_spec(DH), pl.BlockSpec((DNK, DH), lambda j: (0, j))], out_specs=_col_spec(DH),
                 scratch=[pltpu.VMEM((S + head, DH), F32)], sem=("parallel",))(proj, conv_w)


def _dn_pre_bwd(dq, dk, dv, proj, conv_w, dproj):
    head = 8

    def body(dq_ref, dk_ref, dv_ref, x_ref, w_ref, dproj_in, dx_ref, dw_ref, pad_ref, pad2_ref):
        j = pl.program_id(0)
        pad_ref[pl.ds(0, head), :] = jnp.zeros((head, DH), F32)
        pad_ref[pl.ds(head, S), :] = x_ref[...]
        pad2_ref[pl.ds(S, head), :] = jnp.zeros((head, DH), F32)
        scale = jnp.where(j < H, DH ** -0.5, 1.0)
        for r0 in range(0, S, RC):
            xc = _conv_fwd_rows(pad_ref, w_ref, DNK, head, r0)
            y = _silu(xc)
            rinv = lax.rsqrt(jnp.sum(y * y, axis=-1, keepdims=True) + EPS)
            yn = y * rinv
            rows = pl.ds(r0, RC)
            do = jnp.where(j < H, dq_ref[rows, :], jnp.where(j < 2 * H, dk_ref[rows, :], dv_ref[rows, :]))
            dy_n = scale * rinv * (do - yn * jnp.sum(do * yn, axis=-1, keepdims=True))
            dy = jnp.where(j < 2 * H, dy_n, do)
            pad2_ref[rows, :] = dy * _dsilu(xc)
        for r0 in range(0, S, RC):
            dx_ref[pl.ds(r0, RC), :] = _conv_bwd_rows(pad2_ref, w_ref, DNK, r0).astype(BF16)
        _conv_dw(pad_ref, pad2_ref, dw_ref, DNK, head)

    wspec = pl.BlockSpec((DNK, DH), lambda j: (0, j))
    head_col = lambda lo: pl.BlockSpec((S, DH), lambda j: (0, jnp.clip(j - lo, 0, H - 1)))
    return _call(body, "dn_pre_bwd", [_sds((S, NINP), BF16), _sds((DNK, 3 * DNW))], grid=(3 * H,),
                 in_specs=[head_col(0), head_col(H), head_col(2 * H), _col_spec(DH), wspec, ANY],
                 out_specs=[_col_spec(DH), wspec],
                 scratch=[pltpu.VMEM((S + head, DH), F32), pltpu.VMEM((S + head, DH), F32)],
                 sem=("parallel",), aliases={5: 0})(dq, dk, dv, proj, conv_w, dproj)


CF_HEAD = 32
CF_VAL = pl.BlockSpec((S, LANE), lambda j: (0, O_GLU // LANE + 2 * j))
CF_GL = pl.BlockSpec((S, LANE), lambda j: (0, O_GLU // LANE + 2 * j + 1))


def _cf_conv_fwd(proj, conv_w):
    def body(val_ref, gl_ref, w_ref, o_ref, pad_ref):
        pad_ref[pl.ds(0, CF_HEAD), :] = jnp.zeros((CF_HEAD, LANE), F32)
        pad_ref[pl.ds(CF_HEAD, S), :] = val_ref[...] * _sigmoid(gl_ref[...])
        for r0 in range(0, S, RC):
            o_ref[pl.ds(r0, RC), :] = _conv_fwd_rows(pad_ref, w_ref, CFK, CF_HEAD, r0)

    wspec = pl.BlockSpec((CFK, LANE), lambda j: (0, j))
    return _call(body, "cf_conv_fwd", _sds((S, CFW)), grid=(CFW // LANE,),
                 in_specs=[CF_VAL, CF_GL, wspec], out_specs=_col_spec(LANE),
                 scratch=[pltpu.VMEM((S + CF_HEAD, LANE), F32)], sem=("parallel",))(proj, proj, conv_w)


def _cf_conv_bwd(du1, proj, conv_w, dproj):
    def body(d_ref, val_ref, gl_ref, w_ref, dproj_in, dp_ref, dw_ref, pad_ref, pad2_ref):
        sg = _sigmoid(gl_ref[...])
        pad_ref[pl.ds(0, CF_HEAD), :] = jnp.zeros((CF_HEAD, LANE), F32)
        pad_ref[pl.ds(CF_HEAD, S), :] = val_ref[...] * sg
        pad2_ref[pl.ds(0, S), :] = d_ref[...]
        pad2_ref[pl.ds(S, CF_HEAD), :] = jnp.zeros((CF_HEAD, LANE), F32)
        for r0 in range(0, S, RC):
            du0 = _conv_bwd_rows(pad2_ref, w_ref, CFK, r0)
            rows = pl.ds(r0, RC)
            sgr = _sigmoid(gl_ref[rows, :])
            dp_ref[rows, 0:LANE] = (du0 * sgr).astype(BF16)
            dp_ref[rows, LANE:2 * LANE] = (du0 * val_ref[rows, :] * sgr * (1.0 - sgr)).astype(BF16)
        _conv_dw(pad_ref, pad2_ref, dw_ref, CFK, CF_HEAD)

    wspec = pl.BlockSpec((CFK, LANE), lambda j: (0, j))
    return _call(body, "cf_conv_bwd", [_sds((S, NINP), BF16), _sds((CFK, CFW))], grid=(CFW // LANE,),
                 in_specs=[_col_spec(LANE), CF_VAL, CF_GL, wspec, ANY],
                 out_specs=[pl.BlockSpec((S, 2 * LANE), lambda j: (0, O_GLU // (2 * LANE) + j)), wspec],
                 scratch=[pltpu.VMEM((S + CF_HEAD, LANE), F32), pltpu.VMEM((S + CF_HEAD, LANE), F32)],
                 sem=("parallel",), aliases={4: 0})(du1, proj, proj, conv_w, dproj)


def _cf_ln_fwd(u1, g, b):
    def body(u_ref, g_ref, b_ref, o_ref):
        u = u_ref[...]
        mu = jnp.mean(u, axis=-1, keepdims=True)
        xc = u - mu
        y = xc * lax.rsqrt(jnp.mean(xc * xc, axis=-1, keepdims=True) + EPS)
        o_ref[...] = _silu(y * g_ref[...] + b_ref[...]).astype(BF16)

    return _call(body, "cf_ln_fwd", _sds((S, CFW), BF16), grid=(S // TS,),
                 in_specs=[_row_spec(CFW), _vec_spec(CFW), _vec_spec(CFW)], out_specs=_row_spec(CFW),
                 sem=("parallel",))(u1, g, b)


def _cf_ln_bwd(du3, u1, g, b):
    def body(d_ref, u_ref, g_ref, b_ref, du_ref, acc_ref):
        u = u_ref[...]
        mu = jnp.mean(u, axis=-1, keepdims=True)
        xc = u - mu
        rstd = lax.rsqrt(jnp.mean(xc * xc, axis=-1, keepdims=True) + EPS)
        xh = xc * rstd
        du2 = d_ref[...] * _dsilu(xh * g_ref[...] + b_ref[...])
        dxh = du2 * g_ref[...]
        du_ref[...] = rstd * (dxh - jnp.mean(dxh, axis=-1, keepdims=True)
                              - xh * jnp.mean(dxh * xh, axis=-1, keepdims=True))
        _acc_rows(acc_ref, [jnp.sum(du2 * xh, axis=0, keepdims=True), jnp.sum(du2, axis=0, keepdims=True)])

    return _call(body, "cf_ln_bwd", [_sds((S, CFW)), _sds((8, CFW))], grid=(S // TS,),
                 in_specs=[_row_spec(CFW), _row_spec(CFW), _vec_spec(CFW), _vec_spec(CFW)],
                 out_specs=[_row_spec(CFW), _acc_spec(CFW)], sem=("arbitrary",))(du3, u1, g, b)


FB = 256
FNB = FFN // FB
FF_HEAD = 8


def _ffn_mid_fwd(upall, conv_w):
    def body(gate_ref, up_ref, w_ref, o_ref, pad_ref):
        pad_ref[pl.ds(0, FF_HEAD), :] = jnp.zeros((FF_HEAD, FB), F32)
        pad_ref[pl.ds(FF_HEAD, S), :] = gate_ref[...]
        for r0 in range(0, S, RC):
            gc = _conv_fwd_rows(pad_ref, w_ref, FFK, FF_HEAD, r0)
            o_ref[pl.ds(r0, RC), :] = (_silu(gc) * up_ref[pl.ds(r0, RC), :]).astype(BF16)

    wspec = pl.BlockSpec((FFK, FB), lambda j: (0, j))
    return _call(body, "ffn_mid_fwd", _sds((S, FFN), BF16), grid=(FNB,),
                 in_specs=[_col_spec(FB), _col_spec(FB, FNB), wspec], out_specs=_col_spec(FB),
                 scratch=[pltpu.VMEM((S + FF_HEAD, FB), F32)], sem=("parallel",))(upall, upall, conv_w)


def _ffn_mid_bwd(dh, upall, conv_w):
    def body(d_ref, gate_ref, up_ref, w_ref, dgate_ref, dup_ref, dw_ref, pad_ref, pad2_ref):
        pad_ref[pl.ds(0, FF_HEAD), :] = jnp.zeros((FF_HEAD, FB), F32)
        pad_ref[pl.ds(FF_HEAD, S), :] = gate_ref[...]
        pad2_ref[pl.ds(S, FF_HEAD), :] = jnp.zeros((FF_HEAD, FB), F32)
        for r0 in range(0, S, RC):
            rows = pl.ds(r0, RC)
            gc = _conv_fwd_rows(pad_ref, w_ref, FFK, FF_HEAD, r0)
            dhv = d_ref[rows, :]
            dup_ref[rows, :] = (dhv * _silu(gc)).astype(BF16)
            pad2_ref[rows, :] = dhv * up_ref[rows, :] * _dsilu(gc)
        for r0 in range(0, S, RC):
            dgate_ref[pl.ds(r0, RC), :] = _conv_bwd_rows(pad2_ref, w_ref, FFK, r0).astype(BF16)
        _conv_dw(pad_ref, pad2_ref, dw_ref, FFK, FF_HEAD)

    wspec = pl.BlockSpec((FFK, FB), lambda j: (0, j))
    return _call(body, "ffn_mid_bwd", [_sds((S, FFN), BF16), _sds((S, FFN), BF16), _sds((FFK, FFN))],
                 grid=(FNB,), in_specs=[_col_spec(FB), _col_spec(FB), _col_spec(FB, FNB), wspec],
                 out_specs=[_col_spec(FB), _col_spec(FB), wspec],
                 scratch=[pltpu.VMEM((S + FF_HEAD, FB), F32), pltpu.VMEM((S + FF_HEAD, FB), F32)],
                 sem=("parallel",))(dh, upall, upall, conv_w)


GT = 256
SM_BLK = O_SM // LANE


def _chunk_tri(lower):
    r = lax.broadcasted_iota(jnp.int32, (GT, GT), 0)
    c = lax.broadcasted_iota(jnp.int32, (GT, GT), 1)
    same = (r // CH) == (c // CH)
    tri = (c <= r) if lower else (c >= r)
    return jnp.where(same & tri, 1.0, 0.0).astype(F32)


def _gates_fwd(proj, alog_v, dtb_v):
    def body(sm_ref, al_ref, dt_ref, o_ref):
        lane = lax.broadcasted_iota(jnp.int32, (GT, LANE), 1)
        tri = _chunk_tri(True)
        na = -jnp.exp(al_ref[...])
        for r0 in range(0, S, GT):
            sm = sm_ref[pl.ds(r0, GT), :]
            raw = jnp.where((lane >= H) & (lane < 2 * H), na * _softplus(sm + dt_ref[...]), 0.0)
            gc = _dot(tri, raw, NN, precision=HI)
            o_ref[pl.ds(r0, GT), :] = jnp.where(lane < H, _sigmoid(sm), gc)

    return _call(body, "gates_fwd", _sds((S, LANE)), grid=(1,),
                 in_specs=[pl.BlockSpec((S, LANE), lambda i: (0, SM_BLK)), _vec_spec(LANE), _vec_spec(LANE)],
                 out_specs=pl.BlockSpec((S, LANE), lambda i: (0, 0)), sem=("arbitrary",))(proj, alog_v, dtb_v)


def _gates_bwd(dgb, proj, alog_v, dtb_v, dproj):
    def body(d_ref, sm_ref, al_ref, dt_ref, dproj_in, o_ref, acc_ref):
        lane = lax.broadcasted_iota(jnp.int32, (GT, LANE), 1)
        is_g = (lane >= H) & (lane < 2 * H)
        tri = _chunk_tri(False)
        na = -jnp.exp(al_ref[...])
        d_al = jnp.zeros((1, LANE), F32)
        d_dt = jnp.zeros((1, LANE), F32)
        for r0 in range(0, S, GT):
            sm = sm_ref[pl.ds(r0, GT), :]
            dv = d_ref[pl.ds(r0, GT), :]
            z = sm + dt_ref[...]
            draw = _dot(tri, jnp.where(is_g, dv, 0.0), NN, precision=HI)
            dlogit = jnp.where(is_g, draw * na * _sigmoid(z), 0.0)
            d_al = d_al + jnp.sum(jnp.where(is_g, draw * na * _softplus(z), 0.0), axis=0, keepdims=True)
            d_dt = d_dt + jnp.sum(dlogit, axis=0, keepdims=True)
            bt = _sigmoid(sm)
            o_ref[pl.ds(r0, GT), :] = jnp.where(lane < H, dv * bt * (1.0 - bt), dlogit).astype(BF16)
        acc_ref[...] = jnp.zeros_like(acc_ref)
        acc_ref[0:1, :] = d_al
        acc_ref[1:2, :] = d_dt

    return _call(body, "gates_bwd", [_sds((S, NINP), BF16), _sds((8, LANE))], grid=(1,),
                 in_specs=[pl.BlockSpec((S, LANE), lambda i: (0, 0)), pl.BlockSpec((S, LANE), lambda i: (0, SM_BLK)),
                           _vec_spec(LANE), _vec_spec(LANE), ANY],
                 out_specs=[pl.BlockSpec((S, LANE), lambda i: (0, SM_BLK)), _acc_spec(LANE)],
                 sem=("arbitrary",), aliases={4: 0})(dgb, proj, alog_v, dtb_v, dproj)


def _neumann_inv(a, eye):
    x = -a
    t = eye + x
    p = x
    for _ in range(5):
        p = _dot(p, p, NN, precision=HI)
        t = t + _dot(t, p, NN, precision=HI)
    return t


def _head_specs():
    q = pl.BlockSpec((S, DH), lambda h: (0, h))
    k = pl.BlockSpec((S, DH), lambda h: (0, H + h))
    v = pl.BlockSpec((S, DH), lambda h: (0, 2 * H + h))
    gb = pl.BlockSpec((None, S, DH), lambda h: (h, 0, 0))
    gr = pl.BlockSpec((None, NCH, CH), lambda h: (h, 0, 0))
    return q, k, v, gb, gr


ST_SPEC = pl.BlockSpec((None, NCH, DH, DH), lambda h: (h, 0, 0, 0))
TM_SPEC = pl.BlockSpec((None, NCH, CH, CH), lambda h: (h, 0, 0, 0))


def _delta_fwd(qkvn, gb, gr, bb):
    def body(q_ref, k_ref, v_ref, gb_ref, gr_ref, bb_ref, o_ref, st_ref, tm_ref):
        ri = lax.broadcasted_iota(jnp.int32, (CH, CH), 0)
        ci = lax.broadcasted_iota(jnp.int32, (CH, CH), 1)
        strict = ri > ci
        causal = ri >= ci
        eye = jnp.where(ri == ci, 1.0, 0.0).astype(F32)

        def step(n, st):
            rows = pl.ds(pl.multiple_of(n * CH, CH), CH)
            q, k, v, g, beta = q_ref[rows, :], k_ref[rows, :], v_ref[rows, :], gb_ref[rows, :], bb_ref[rows, :]
            diff = g[:, :CH] - gr_ref[pl.ds(n, 1), :]
            el = jnp.exp(jnp.where(causal, diff, 0.0))
            eg = jnp.exp(g)
            gl = g[CH - 1:CH, :]
            kb = k * beta
            kbf = k.astype(BF16)
            a = jnp.where(strict, _dot(kb.astype(BF16), kbf, NT) * el, 0.0)
            t = _neumann_inv(a, eye)
            tm_ref[n] = t
            st_ref[n] = st
            sb = st.astype(BF16)
            r = v * beta - _dot((kb * eg).astype(BF16), sb, NN)
            ub = _dot(t, r, NN, precision=HI).astype(BF16)
            p = jnp.where(causal, _dot(q.astype(BF16), kbf, NT) * el, 0.0)
            o_ref[rows, :] = _dot((q * eg).astype(BF16), sb, NN) + _dot(p.astype(BF16), ub, NN)
            kd = k * jnp.exp(gl - g)
            return st * jnp.exp(gl) + _dot(kd.astype(BF16), ub, TN)

        lax.fori_loop(0, NCH, step, jnp.zeros((DH, DH), F32))

    q, k, v, gbs, grs = _head_specs()
    return _call(body, "delta_fwd", [_sds((S, DNW)), _sds((H, NCH, DH, DH)), _sds((H, NCH, CH, CH))], grid=(H,),
                 in_specs=[q, k, v, gbs, grs, gbs], out_specs=[pl.BlockSpec((S, DH), lambda h: (0, h)), ST_SPEC, TM_SPEC],
                 sem=("parallel",))(qkvn, qkvn, qkvn, gb, gr, bb)


def _delta_bwd(qkvn, gb, gr, bb, st_all, tm_all, do_all):
    def body(q_ref, k_ref, v_ref, gb_ref, gr_ref, bb_ref, st_ref, tm_ref, do_ref,
             dq_ref, dk_ref, dv_ref, dg_ref, db_ref):
        ri = lax.broadcasted_iota(jnp.int32, (CH, CH), 0)
        ci = lax.broadcasted_iota(jnp.int32, (CH, CH), 1)
        lo_s, lo_c, up_s, up_c = ri > ci, ri >= ci, ri < ci, ri <= ci
        last_row = lax.broadcasted_iota(jnp.int32, (CH, 1), 0) == CH - 1

        def rs(mat):
            return jnp.sum(mat, axis=1, keepdims=True)

        def total(mat):
            return jnp.sum(rs(mat), axis=0, keepdims=True)

        def step(i, ds):
            n = NCH - 1 - i
            rows = pl.ds(pl.multiple_of(n * CH, CH), CH)
            q, k, v, g, beta = q_ref[rows, :], k_ref[rows, :], v_ref[rows, :], gb_ref[rows, :], bb_ref[rows, :]
            do = do_ref[rows, :]
            t = tm_ref[n]
            st = st_ref[n]
            diff = g[:, :CH] - gr_ref[pl.ds(n, 1), :]
            el = jnp.exp(jnp.where(lo_c, diff, 0.0))
            eu = jnp.exp(jnp.where(up_c, -diff, 0.0))
            eg = jnp.exp(g)
            gl = g[CH - 1:CH, :]
            egl = jnp.exp(gl)
            ekd = jnp.exp(gl - g)
            kb = k * beta
            kbg = kb * eg
            qg = q * eg
            kd = k * ekd
            qb, kbf, kbb = q.astype(BF16), k.astype(BF16), kb.astype(BF16)
            kbgb, qgb, kdb = kbg.astype(BF16), qg.astype(BF16), kd.astype(BF16)
            sb, dob, dsb = st.astype(BF16), do.astype(BF16), ds.astype(BF16)
            r = v * beta - _dot(kbgb, sb, NN)
            u = _dot(t, r, NN, precision=HI)
            ub = u.astype(BF16)
            kk, qk = _dot(kbb, kbf, NT), _dot(qb, kbf, NT)
            kkt, qkt = _dot(kbf, kbb, NT), _dot(kbf, qb, NT)
            pt = jnp.where(up_c, qkt * eu, 0.0)
            du = _dot(pt.astype(BF16), dob, NN) + _dot(kdb, dsb, NN)
            dr = _dot(t, du, TN, precision=HI)
            drb = dr.astype(BF16)
            dpg = jnp.where(lo_c, _dot(dob, ub, NT), 0.0) * el
            dpgt = jnp.where(up_c, _dot(ub, dob, NT), 0.0) * eu
            dag = -jnp.where(lo_s, _dot(drb, ub, NT), 0.0) * el
            dagt = -jnp.where(up_s, _dot(ub, drb, NT), 0.0) * eu
            dqg = _dot(dob, sb, NT)
            dkbg = -_dot(drb, sb, NT)
            dkd = _dot(ub, dsb, NT)
            ds_new = _dot(qgb, dob, TN) + egl * ds - _dot(kbgb, drb, TN)
            dkb = _dot(dag.astype(BF16), kbf, NN) + dkbg * eg
            dk = (_dot(dagt.astype(BF16), kbb, NN) + _dot(dpgt.astype(BF16), qb, NN) + dkd * ekd + dkb * beta)
            dq = _dot(dpg.astype(BF16), kbf, NN) + dqg * eg
            dkd_kd = rs(dkd * kd)
            dg = (rs(dag * kk + dpg * qk) - rs(dagt * kkt + dpgt * qkt) + rs(dqg * qg) + rs(dkbg * kbg) - dkd_kd)
            dgl = jnp.sum(dkd_kd, axis=0, keepdims=True) + egl[:, 0:1] * total(ds * st)
            dg = dg + jnp.where(last_row, dgl, 0.0)
            dbeta = rs(dkb * k) + rs(dr * v)
            dq_ref[rows, :] = dq
            dk_ref[rows, :] = dk
            dv_ref[rows, :] = dr * beta
            spread = jnp.full((8, DH), 1.0 / DH, F32)
            dg_ref[pl.ds(n, 1), :] = _dot(spread, jnp.broadcast_to(dg, (CH, DH)), NT, precision=HI)[0:1, :]
            db_ref[pl.ds(n, 1), :] = _dot(spread, jnp.broadcast_to(dbeta, (CH, DH)), NT, precision=HI)[0:1, :]
            return ds_new

        lax.fori_loop(0, NCH, step, jnp.zeros((DH, DH), F32))

    q, k, v, gbs, grs = _head_specs()
    hcol = pl.BlockSpec((S, DH), lambda h: (0, h))
    return _call(body, "delta_bwd",
                 [_sds((S, DNW)), _sds((S, DNW)), _sds((S, DNW)), _sds((H, NCH, CH)), _sds((H, NCH, CH))], grid=(H,),
                 in_specs=[q, k, v, gbs, grs, gbs, ST_SPEC, TM_SPEC, hcol], out_specs=[hcol, hcol, hcol, grs, grs],
                 sem=("parallel",))(qkvn, qkvn, qkvn, gb, gr, bb, st_all, tm_all, do_all)


Z_BLK = O_Z // DNW


def _dn_post_fwd(o, proj, gn):
    def body(o_ref, z_ref, gn_ref, og_ref):
        for h in range(H):
            cols = slice(h * DH, (h + 1) * DH)
            ov = o_ref[:, cols]
            on = ov * lax.rsqrt(jnp.mean(ov * ov, axis=-1, keepdims=True) + EPS) * gn_ref[...]
            og_ref[:, cols] = (on * _silu(z_ref[:, cols])).astype(BF16)

    return _call(body, "dn_post_fwd", _sds((S, DNW), BF16), grid=(S // TS,),
                 in_specs=[_row_spec(DNW), pl.BlockSpec((TS, DNW), lambda i: (i, Z_BLK)), _vec_spec(DH)],
                 out_specs=_row_spec(DNW), sem=("parallel",))(o, proj, gn)


def _dn_post_bwd(dog, o, proj, gn, dproj):
    def body(d_ref, o_ref, z_ref, gn_ref, dproj_in, do_ref, dz_ref, acc_ref):
        dgn = jnp.zeros((1, DH), F32)
        for h in range(H):
            cols = slice(h * DH, (h + 1) * DH)
            ov, zv, dv = o_ref[:, cols], z_ref[:, cols], d_ref[:, cols]
            rinv = lax.rsqrt(jnp.mean(ov * ov, axis=-1, keepdims=True) + EPS)
            xn = ov * rinv
            don = dv * _silu(zv)
            dz_ref[:, cols] = (dv * xn * gn_ref[...] * _dsilu(zv)).astype(BF16)
            dgn = dgn + jnp.sum(don * xn, axis=0, keepdims=True)
            dxn = don * gn_ref[...]
            do_ref[:, cols] = rinv * (dxn - xn * jnp.mean(dxn * xn, axis=-1, keepdims=True))
        _acc_rows(acc_ref, [dgn])

    zspec = pl.BlockSpec((TS, DNW), lambda i: (i, Z_BLK))
    return _call(body, "dn_post_bwd", [_sds((S, DNW)), _sds((S, NINP), BF16), _sds((8, DH))], grid=(S // TS,),
                 in_specs=[_row_spec(DNW), _row_spec(DNW), zspec, _vec_spec(DH), ANY],
                 out_specs=[_row_spec(DNW), zspec, _acc_spec(DH)], sem=("arbitrary",),
                 aliases={4: 1})(dog, o, proj, gn, dproj)


GA_BLK = O_GA // D
GB_BLK = O_GB // D


def _merge_fwd(ba, bb, proj):
    def body(a_ref, b_ref, ga_ref, gb_ref, o_ref):
        o_ref[...] = (_sigmoid(ga_ref[...]) * a_ref[...] + _sigmoid(gb_ref[...]) * b_ref[...]).astype(BF16)

    return _call(body, "merge_fwd", _sds((S, D), BF16), grid=(S // TS,),
                 in_specs=[_row_spec(), _row_spec(), pl.BlockSpec((TS, D), lambda i: (i, GA_BLK)),
                           pl.BlockSpec((TS, D), lambda i: (i, GB_BLK))],
                 out_specs=_row_spec(), sem=("parallel",))(ba, bb, proj, proj)


def _merge_bwd(dm, ba, bb, proj, dproj):
    def body(d_ref, a_ref, b_ref, ga_ref, gb_ref, dproj_in, dg_ref, da_ref, db_ref):
        d = d_ref[...]
        sa, sb = _sigmoid(ga_ref[...]), _sigmoid(gb_ref[...])
        dg_ref[:, 0:D] = (d * a_ref[...] * sa * (1.0 - sa)).astype(BF16)
        dg_ref[:, D:2 * D] = (d * b_ref[...] * sb * (1.0 - sb)).astype(BF16)
        da_ref[...] = (d * sa).astype(BF16)
        db_ref[...] = (d * sb).astype(BF16)

    return _call(body, "merge_bwd", [_sds((S, NINP), BF16), _sds((S, D), BF16), _sds((S, D), BF16)], grid=(S // TS,),
                 in_specs=[_row_spec(), _row_spec(), _row_spec(), pl.BlockSpec((TS, D), lambda i: (i, GA_BLK)),
                           pl.BlockSpec((TS, D), lambda i: (i, GB_BLK)), ANY],
                 out_specs=[pl.BlockSpec((TS, 2 * D), lambda i: (i, O_GA // (2 * D))), _row_spec(), _row_spec()],
                 sem=("parallel",), aliases={5: 0})(dm, ba, bb, proj, proj, dproj)


NSH = NIN // NDEV


def _win_pieces():
    pieces = [(0, 0, 4096), (O_GA, 6160, 2 * D), (O_SM, 4096, 16)]
    for j in range(CFW // LANE):
        pieces.append((O_GLU + 2 * LANE * j, 4112 + LANE * j, LANE))
        pieces.append((O_GLU + 2 * LANE * j + LANE, 4112 + CFW + LANE * j, LANE))
    return pieces


def _pad_win(wt):
    rows = [wt[o:o + wdt] for _, o, wdt in sorted(_win_pieces())]
    rows.append(jnp.zeros((NINP - NIN, wt.shape[1]), wt.dtype))
    return jnp.concatenate(rows, axis=0)


def _unpad_win(gpt):
    return jnp.concatenate([gpt[p:p + wdt] for p, o, wdt in sorted(_win_pieces(), key=lambda t: t[1])], axis=0)


def _lane_vec(v8, offset):
    return jnp.pad(v8, ((0, 0), (offset, LANE - 8 - offset)))


def _tie(vec, token):
    return vec + token


def _local_step(x, tgt, mod, norm1_g, norm2_g, final_g, w_in_p, dn_conv_w, a_log, dt_bias, dn_norm_g,
                cf_conv_w, cf_ln_g, cf_ln_b, ffn_conv_w, comm):
    sh1, sc1, gt1, sh2, sc2, gt2 = (mod[:, i * D:(i + 1) * D] for i in range(6))
    alog_v, dtb_v = _lane_vec(a_log, H), _lane_vec(dt_bias, H)

    hn1 = _norm_mod(x, norm1_g, sc1, _tie(sh1, comm.token0), "norm_mod1")
    proj = _mm(hn1, w_in_p, "nt", F32, "mm_in", tn=1152)
    qkvn = _dn_pre_fwd(proj, dn_conv_w)
    gates = _gates_fwd(proj, alog_v, dtb_v)
    beta_t = gates[:, 0:H].T
    g_t = gates[:, H:2 * H].T
    gb = jnp.broadcast_to(g_t[:, :, None], (H, S, DH))
    bb = jnp.broadcast_to(beta_t[:, :, None], (H, S, DH))
    gr = g_t.reshape(H, NCH, CH)
    o, st_all, tm_all = _delta_fwd(qkvn, gb, gr, bb)
    og = _dn_post_fwd(o, proj, dn_norm_g)
    u1 = _cf_conv_fwd(proj, cf_conv_w)
    u3 = _cf_ln_fwd(u1, cf_ln_g, cf_ln_b)
    after = og[0:8, 0:LANE].astype(F32) + u3[0:8, 0:LANE].astype(F32)
    dn_w_o, cf_w_o, w_out, ffn_w_up, ffn_w_down = comm.late_weights(after)
    br_a = _mm(og, dn_w_o, "nn", F32, "mm_dn_o")
    br_b = _mm(u3, cf_w_o, "nn", F32, "mm_cf_o")
    merged = _merge_fwd(br_a, br_b, proj)
    mix = _mm(merged, w_out, "nn", F32, "mm_out")
    x2, hn2 = _resid_norm_mod(x, mix, gt1, norm2_g, sc2, sh2, "resid_norm_mod2")
    upall = _mm(hn2, ffn_w_up, "nn", F32, "mm_up")
    hmid = _ffn_mid_fwd(upall, ffn_conv_w)
    f = _mm(hmid, ffn_w_down, "nn", F32, "mm_down")

    dx3, df, acc_f = _loss_head(x2, f, tgt, gt2, final_g)
    d_final_g, d_gt2, loss = acc_f[0:1], acc_f[1:2], acc_f[2:3, 0:1]
    dhmid = _mm(df, ffn_w_down, "nt", F32, "mm_down_dx")
    g_w_down = _mm(hmid, df, "tn", BF16, "mm_down_dw")
    d_gate, d_up, g_ffn_conv = _ffn_mid_bwd(dhmid, upall, ffn_conv_w)
    dhn2 = _mm(d_gate, ffn_w_up, "nt", F32, "mm_up_dx", a2=d_up)
    g_w_up = _mm(hn2, d_gate, "tn", BF16, "mm_up_dw", tn=2 * FFN // NDEV, b2=d_up)
    tok_a = comm.grads("a", dict(ffn_w_down=g_w_down, ffn_w_up=g_w_up))
    dx2, dmix, acc2 = _norm_mod_bwd(dhn2, x2, dx3, _tie(norm2_g, tok_a), sc2, "norm_mod2_bwd", mix=mix, gt=gt1)
    d_sh2, d_sc2, d_norm2_g, d_gt1 = acc2[0:1], acc2[1:2], acc2[2:3], acc2[3:4]
    dmerged = _mm(dmix, w_out, "nt", F32, "mm_out_dx")
    g_w_out = _mm(merged, dmix, "tn", BF16, "mm_out_dw")
    d_proj, d_bra, d_brb = _merge_bwd(dmerged, br_a, br_b, proj, lax.empty((S, NINP), BF16))
    du3 = _mm(d_brb, cf_w_o, "nt", F32, "mm_cf_o_dx")
    g_cf_w_o = _mm(u3, d_brb, "tn", BF16, "mm_cf_o_dw")
    du1, acc_ln = _cf_ln_bwd(du3, u1, cf_ln_g, cf_ln_b)
    d_proj, g_cf_conv = _cf_conv_bwd(du1, proj, cf_conv_w, d_proj)
    dog = _mm(d_bra, dn_w_o, "nt", F32, "mm_dn_o_dx")
    g_dn_w_o = _mm(og, d_bra, "tn", BF16, "mm_dn_o_dw")
    tok_b = comm.grads("b", dict(w_out=g_w_out, cf_w_o=g_cf_w_o, dn_w_o=g_dn_w_o, ffn_conv_w=g_ffn_conv,
                                 cf_conv_w=g_cf_conv))
    do, d_proj, acc_gn = _dn_post_bwd(dog, o, proj, _tie(dn_norm_g, tok_b), d_proj)
    dq, dk, dv, dgr, dbr = _delta_bwd(qkvn, gb, gr, bb, st_all, tm_all, do)
    d_proj, g_dn_conv = _dn_pre_bwd(dq, dk, dv, proj, dn_conv_w, d_proj)
    dgates = jnp.concatenate([dbr.reshape(H, S).T, dgr.reshape(H, S).T, jnp.zeros((S, LANE - 2 * H), F32)], axis=1)
    d_proj, acc_g = _gates_bwd(dgates, proj, alog_v, dtb_v, d_proj)
    dhn1 = _mm(d_proj, w_in_p, "nn", F32, "mm_in_dx", tk=1152)
    g_w_in_p = _mm(d_proj, hn1, "tn", BF16, "mm_in_dw", tm=1152)
    tok_c = comm.grads("c", dict(w_in=g_w_in_p, dn_conv_w=g_dn_conv))
    grad_x, acc1 = _norm_mod_bwd(dhn1, x, dx2, _tie(norm1_g, tok_c), sc1, "norm_mod1_bwd")
    d_sh1, d_sc1, d_norm1_g = acc1[0:1], acc1[1:2], acc1[2:3]

    d_mod = jnp.concatenate([d_sh1, d_sc1, d_gt1, d_sh2, d_sc2, d_gt2], axis=1)
    small = dict(mod=d_mod, norm1_g=d_norm1_g, norm2_g=d_norm2_g, final_norm_g=d_final_g,
                 cf_ln_g=acc_ln[0:1], cf_ln_b=acc_ln[1:2], dn_norm_g=acc_gn[0:1],
                 dn_a_log=acc_g[0:1, H:2 * H], dn_dt_bias=acc_g[1:2, H:2 * H])
    return loss, grad_x, small


def _dev_index(px, py, pc):
    return 4 * px + 2 * py + pc


def _all_gather(arrs, name):
    n = len(arrs)

    def body(*refs):
        ins, outs = refs[:n], refs[n:2 * n]
        send_sems, recv_sems, loc_sems = refs[2 * n:]
        x, y, c = _my_pos()
        me, sib = (x, y, c), (x, y, 1 - c)
        chips = [(1 - x, y), (x, 1 - y), (1 - x, 1 - y)]

        def cp(i, k, block, to, src=None):
            dst = outs[i].at[_dev_index(*block)]
            return pltpu.make_async_remote_copy(
                src_ref=dst if src is None else src, dst_ref=dst, send_sem=send_sems.at[i, k],
                recv_sem=recv_sems.at[i, k], device_id=to, device_id_type=MESH)

        mine = [pltpu.make_async_copy(ins[i], outs[i].at[_dev_index(*me)], loc_sems.at[i]) for i in range(n)]
        for m in mine:
            m.start()
        sent = []
        for i in range(n):
            sent.append(cp(i, 0, me, sib, src=ins[i]))
            sent += [cp(i, 1 + j, me, (*chip, c), src=ins[i]) for j, chip in enumerate(chips)]
        for s in sent:
            s.start()
        for i in range(n):
            for j, chip in enumerate(chips):
                cp(i, 1 + j, (*chip, c), me).wait_recv()
                fwd = cp(i, 4 + j, (*chip, c), sib)
                fwd.start()
                sent.append(fwd)
        for i in range(n):
            cp(i, 0, sib, me).wait_recv()
            for j, chip in enumerate(chips):
                cp(i, 4 + j, (*chip, 1 - c), me).wait_recv()
        for s in sent:
            s.wait_send()
        for m in mine:
            m.wait()

    outs = pl.pallas_call(
        body, out_shape=[_sds((NDEV,) + a.shape, a.dtype) for a in arrs], in_specs=[ANY] * n, out_specs=[ANY] * n,
        scratch_shapes=[pltpu.SemaphoreType.DMA((n, 7)), pltpu.SemaphoreType.DMA((n, 7)), pltpu.SemaphoreType.DMA((n,))],
        name=name)(*arrs)
    return list(outs)


def _slab(ref, layout, idx):
    kind, n = layout
    if kind == "rows":
        return ref.at[pl.ds(pl.multiple_of(idx * n, n), n), :]
    if kind == "cols":
        return ref.at[:, pl.ds(pl.multiple_of(idx * n, n), n)]
    return ref.at[idx]


def _slab_shape(arr, layout):
    kind, n = layout
    if kind == "rows":
        return (n, arr.shape[1])
    if kind == "cols":
        return (arr.shape[0], n)
    return tuple(arr.shape[1:])


def _pair_exchange(parts, layouts, name):
    n = len(parts)

    def body(*refs):
        ins, outs = refs[:n], refs[n:2 * n]
        send_sems, recv_sems = refs[2 * n:]
        x, y, c = _my_pos()
        copies = []
        for i in range(n):
            for q in range(4):
                copies.append(pltpu.make_async_remote_copy(
                    src_ref=_slab(ins[i], layouts[i], 2 * q + (1 - c)), dst_ref=outs[i].at[q],
                    send_sem=send_sems.at[i, q], recv_sem=recv_sems.at[i, q], device_id=(x, y, 1 - c),
                    device_id_type=MESH))
        for cpy in copies:
            cpy.start()
        for cpy in copies:
            cpy.wait()

    outs = pl.pallas_call(
        body, out_shape=[_sds((4,) + _slab_shape(p, lay), p.dtype) for p, lay in zip(parts, layouts)],
        in_specs=[ANY] * n, out_specs=[ANY] * n,
        scratch_shapes=[pltpu.SemaphoreType.DMA((n, 4)), pltpu.SemaphoreType.DMA((n, 4))], name=name)(*parts)
    return list(outs)


HBM = pl.BlockSpec(memory_space=pltpu.HBM)
SEMS = pl.BlockSpec(memory_space=pltpu.SEMAPHORE)
EFFECT = pltpu.SideEffectType.DATAFLOW_SIDE_EFFECTING
TOKEN = jax.ShapeDtypeStruct((8, LANE), F32)


def _hbm(a):
    return pltpu.with_memory_space_constraint(a, pltpu.HBM)


def _gather_ici_copy(shard_ref, buf_ref, layout, send_sems, recv_sems, i, j, me, chip, c):
    return pltpu.make_async_remote_copy(
        src_ref=shard_ref, dst_ref=buf_ref.at[j], send_sem=send_sems.at[3 * i + j],
        recv_sem=recv_sems.at[3 * i + j], device_id=(*chip, c), device_id_type=MESH)


def _gather_ici_start(shards, bufs, layouts, after, name):
    n = len(shards)

    def body(*refs):
        sh, bf = refs[:n], refs[n:2 * n]
        send_sems, recv_sems = refs[2 * n + 1], refs[2 * n + 2]
        token = refs[-1]
        x, y, c = _my_pos()
        me = _dev_index(x, y, c)
        for i in range(n):
            for j, chip in enumerate([(1 - x, y), (x, 1 - y), (1 - x, 1 - y)]):
                _gather_ici_copy(sh[i], bf[i], layouts[i], send_sems, recv_sems, i, j, me, chip, c).start()
        token[...] = jnp.zeros_like(token)

    outs = pl.pallas_call(
        body, name=name,
        out_shape=(pltpu.SemaphoreType.DMA((3 * n,)), pltpu.SemaphoreType.DMA((3 * n,)),
                   *[pltpu.HBM(a.shape, a.dtype) for a in shards], *[pltpu.HBM(a.shape, a.dtype) for a in bufs], TOKEN),
        in_specs=[HBM] * (2 * n) + [ANY],
        out_specs=(SEMS, SEMS, *[HBM] * (2 * n), pl.BlockSpec(memory_space=pltpu.VMEM)),
        input_output_aliases={i: 2 + i for i in range(2 * n)},
        compiler_params=pltpu.CompilerParams(has_side_effects=EFFECT),
    )(*[_hbm(a) for a in shards], *[_hbm(a) for a in bufs], after)
    return outs[0], outs[1], list(outs[2:2 + n]), list(outs[2 + n:2 + 2 * n]), outs[-1]


def _gather_ici_wait(send_sems, recv_sems, shards, bufs, layouts, after, name):
    n = len(shards)

    def body(*refs):
        sh, bf = refs[:n], refs[n:2 * n]
        ssem, rsem = refs[2 * n], refs[2 * n + 1]
        x, y, c = _my_pos()
        me = _dev_index(x, y, c)
        for i in range(n):
            for j, chip in enumerate([(1 - x, y), (x, 1 - y), (1 - x, 1 - y)]):
                cp = _gather_ici_copy(sh[i], bf[i], layouts[i], ssem, rsem, i, j, me, chip, c)
                cp.wait_send()
                cp.wait_recv()

    outs = pl.pallas_call(
        body, name=name,
        out_shape=(*[pltpu.HBM(a.shape, a.dtype) for a in shards], *[pltpu.HBM(a.shape, a.dtype) for a in bufs]),
        in_specs=[HBM] * (2 * n) + [SEMS, SEMS, ANY], out_specs=tuple([HBM] * (2 * n)),
        input_output_aliases={i: i for i in range(2 * n)},
        compiler_params=pltpu.CompilerParams(has_side_effects=EFFECT),
    )(*shards, *bufs, send_sems, recv_sems, after)
    return list(outs[:n]), list(outs[n:])


def _gather_pair(shards, lands, layouts, shapes, name):
    n = len(shards)

    def body(*refs):
        sh, ld, bo = refs[:n], refs[n:2 * n], refs[2 * n:3 * n]
        send_sems, recv_sems, loc_sems = refs[3 * n:]
        x, y, c = _my_pos()
        sib = (x, y, 1 - c)
        chips = [(x, y), (1 - x, y), (x, 1 - y), (1 - x, 1 - y)]
        copies = []
        for i in range(n):
            for k, (px, py) in enumerate(chips):
                src = sh[i] if k == 0 else ld[i].at[k - 1]
                dst = _slab(bo[i], layouts[i], _dev_index(px, py, c))
                copies.append(pltpu.make_async_copy(src, dst, loc_sems.at[i, k]))
                copies.append(pltpu.make_async_remote_copy(
                    src_ref=src, dst_ref=dst, send_sem=send_sems.at[i, k], recv_sem=recv_sems.at[i, k],
                    device_id=sib, device_id_type=MESH))
        for cpy in copies:
            cpy.start()
        for cpy in copies:
            cpy.wait()

    outs = pl.pallas_call(
        body, out_shape=[_sds(shp, a.dtype) for shp, a in zip(shapes, shards)], in_specs=[ANY] * (2 * n),
        out_specs=[ANY] * n,
        scratch_shapes=[pltpu.SemaphoreType.DMA((n, 4)), pltpu.SemaphoreType.DMA((n, 4)), pltpu.SemaphoreType.DMA((n, 4))],
        name=name)(*shards, *lands)
    return list(outs)


def _chip_copy(sum_ref, land_ref, send_sems, recv_sems, i, j, chip, c):
    return pltpu.make_async_remote_copy(
        src_ref=sum_ref.at[2 * chip[0] + chip[1]], dst_ref=land_ref.at[j], send_sem=send_sems.at[3 * i + j],
        recv_sem=recv_sems.at[3 * i + j], device_id=(*chip, c), device_id_type=MESH)


def _chip_exchange_start(sums, name):
    n = len(sums)
    lands = [lax.empty((3,) + s.shape[1:], s.dtype) for s in sums]

    def body(*refs):
        sm, ld = refs[:n], refs[n:2 * n]
        send_sems, recv_sems = refs[2 * n], refs[2 * n + 1]
        token = refs[-1]
        x, y, c = _my_pos()
        for i in range(n):
            for j, chip in enumerate([(1 - x, y), (x, 1 - y), (1 - x, 1 - y)]):
                _chip_copy(sm[i], ld[i], send_sems, recv_sems, i, j, chip, c).start()
        token[...] = jnp.zeros_like(token)

    outs = pl.pallas_call(
        body, name=name,
        out_shape=(pltpu.SemaphoreType.DMA((3 * n,)), pltpu.SemaphoreType.DMA((3 * n,)),
                   *[pltpu.HBM(a.shape, a.dtype) for a in sums], *[pltpu.HBM(a.shape, a.dtype) for a in lands], TOKEN),
        in_specs=[HBM] * (2 * n), out_specs=(SEMS, SEMS, *[HBM] * (2 * n), pl.BlockSpec(memory_space=pltpu.VMEM)),
        input_output_aliases={i: 2 + i for i in range(2 * n)},
        compiler_params=pltpu.CompilerParams(has_side_effects=EFFECT),
    )(*[_hbm(a) for a in sums], *[_hbm(a) for a in lands])
    return outs[0], outs[1], list(outs[2:2 + n]), list(outs[2 + n:2 + 2 * n]), outs[-1]


def _chip_exchange_wait(send_sems, recv_sems, sums, lands, after, name):
    n = len(sums)

    def body(*refs):
        sm, ld = refs[:n], refs[n:2 * n]
        ssem, rsem = refs[2 * n], refs[2 * n + 1]
        x, y, c = _my_pos()
        for i in range(n):
            for j, chip in enumerate([(1 - x, y), (x, 1 - y), (1 - x, 1 - y)]):
                cp = _chip_copy(sm[i], ld[i], ssem, rsem, i, j, chip, c)
                cp.wait_send()
                cp.wait_recv()

    outs = pl.pallas_call(
        body, name=name,
        out_shape=(*[pltpu.HBM(a.shape, a.dtype) for a in sums], *[pltpu.HBM(a.shape, a.dtype) for a in lands]),
        in_specs=[HBM] * (2 * n) + [SEMS, SEMS, ANY], out_specs=tuple([HBM] * (2 * n)),
        input_output_aliases={i: i for i in range(2 * n)},
        compiler_params=pltpu.CompilerParams(has_side_effects=EFFECT),
    )(*sums, *lands, send_sems, recv_sems, after)
    return list(outs[:n]), list(outs[n:])


def _row_tile(r, itemsize):
    align = 32 // itemsize
    best = r
    for t in range(align, min(r, 256) + 1, align):
        if r % t == 0:
            best = t
    return best


def _prefetch_call(body, name, out_shape, grid, in_specs, out_specs, sem):
    return pl.pallas_call(
        body, out_shape=out_shape, name=name,
        grid_spec=pltpu.PrefetchScalarGridSpec(num_scalar_prefetch=1, grid=grid, in_specs=in_specs, out_specs=out_specs),
        compiler_params=pltpu.CompilerParams(dimension_semantics=sem, vmem_limit_bytes=VMEM_LIMIT))


def _pair_sum(pos, part, got, layout, name):
    kind, _ = layout
    _, r, cols = got.shape
    tr, tc = _tiles(r, cols, part.dtype.itemsize)
    nr, nc = r // tr, cols // tc
    if kind == "rows":
        pspec = pl.BlockSpec((tr, tc), lambda q, i, j, p: ((2 * q + p[0]) * nr + i, j))
    elif kind == "cols":
        pspec = pl.BlockSpec((tr, tc), lambda q, i, j, p: (i, (2 * q + p[0]) * nc + j))
    else:
        pspec = pl.BlockSpec((None, tr, tc), lambda q, i, j, p: (2 * q + p[0], i, j))

    def body(pos_ref, p_ref, g_ref, o_ref):
        o_ref[...] = (p_ref[...].astype(F32) + g_ref[...].astype(F32)).astype(o_ref.dtype)

    blk = pl.BlockSpec((None, tr, tc), lambda q, i, j, p: (q, i, j))
    return _prefetch_call(body, name, _sds((4, r, cols), part.dtype), (4, nr, nc), [pspec, blk], blk,
                          ("parallel", "parallel", "parallel"))(pos, part, got)


def _tiles(r, cols, itemsize):
    tr = _row_tile(r, itemsize)
    if tr < r or r * cols * 4 <= (2 << 20) or cols % 256:
        return tr, cols
    return r, 256


def _final_sum_adam(pos, sums, got, w, m, v, name):
    _, r, cols = w.shape
    tr, tc = _tiles(r, cols, sums.dtype.itemsize)

    def body(pos_ref, s_ref, g_ref, w_ref, m_ref, v_ref, go_ref, dl_ref, nm_ref, nv_ref):
        g = ((s_ref[...].astype(F32) + g_ref[0].astype(F32)) + g_ref[1].astype(F32)) + g_ref[2].astype(F32)
        dl, nm, nv = _adam(w_ref[...], g, m_ref[...], v_ref[...])
        go_ref[...] = g
        dl_ref[...] = dl
        nm_ref[...] = nm
        nv_ref[...] = nv

    big = pl.BlockSpec((None, tr, tc), lambda i, j, p: (0, i, j))
    return _prefetch_call(body, name, [_sds((1, r, cols))] * 4, (r // tr, cols // tc),
                          [pl.BlockSpec((None, tr, tc), lambda i, j, p: (p[1], i, j)),
                           pl.BlockSpec((3, tr, tc), lambda i, j, p: (0, i, j)), big, big, big],
                          [big] * 4, ("parallel", "parallel"))(pos, sums, got, w, m, v)


def _small_adam(g_all, w, m, v):
    npk = w.shape[1]

    def body(g_ref, w_ref, m_ref, v_ref, go_ref, dl_ref, nm_ref, nv_ref):
        g = g_ref[0:1, :]
        for k in range(1, NDEV):
            g = g + g_ref[k:k + 1, :]
        dl, nm, nv = _adam(w_ref[...], g, m_ref[...], v_ref[...])
        go_ref[...] = g
        dl_ref[...] = dl
        nm_ref[...] = nm
        nv_ref[...] = nv

    return _call(body, "small_adam", [_sds((1, npk))] * 4)(g_all, w, m, v)


SMALL = [("b_ada", 6 * D), ("norm1_g", D), ("norm2_g", D), ("final_norm_g", D), ("cf_ln_g", CFW), ("cf_ln_b", CFW),
         ("dn_norm_g", DH), ("dn_a_log", H), ("dn_dt_bias", H)]
LATE = ["dn_w_o", "cf_w_o", "w_out", "ffn_w_up", "ffn_w_down"]
LATE_SHAPE = {"dn_w_o": (NDEV, DNW, D // NDEV), "cf_w_o": (NDEV, CFW, D // NDEV), "w_out": (D, D),
              "ffn_w_up": (NDEV, D, 2 * FFN // NDEV), "ffn_w_down": (FFN, D)}
LATE_LAYOUT = {"dn_w_o": ("lead", NDEV), "cf_w_o": ("lead", NDEV), "w_out": ("rows", D // NDEV),
               "ffn_w_up": ("lead", NDEV), "ffn_w_down": ("rows", FFN // NDEV)}
LAYOUT = {"dn_w_o": ("cols", D // NDEV), "cf_w_o": ("cols", D // NDEV), "w_out": ("rows", D // NDEV),
          "ffn_w_up": ("cols", 2 * FFN // NDEV), "ffn_w_down": ("rows", FFN // NDEV),
          "w_in": ("lead", NDEV), "dn_conv_w": ("lead", NDEV), "cf_conv_w": ("lead", NDEV), "ffn_conv_w": ("lead", NDEV)}
NAMES = ["w_ada", "b_ada", "norm1_g", "w_in", "dn_conv_w", "dn_a_log", "dn_dt_bias", "dn_norm_g", "dn_w_o", "cf_conv_w",
         "cf_ln_g", "cf_ln_b", "cf_w_o", "w_out", "norm2_g", "ffn_w_up", "ffn_conv_w", "ffn_w_down", "final_norm_g"]


def _pack_small(d):
    rows = []
    for nm, n in SMALL:
        row = d[nm].reshape(1, n)
        pad = (-n) % LANE
        rows.append(jnp.pad(row, ((0, 0), (0, pad))) if pad else row)
    return jnp.concatenate(rows, axis=1)


def _unpack_small(row, shapes):
    out, off = {}, 0
    for nm, n in SMALL:
        out[nm] = row[0, off:off + n].reshape(shapes[nm])
        off += n + ((-n) % LANE)
    return out


def _cols_from_gathered(g):
    return jnp.transpose(g, (1, 0, 2)).reshape(g.shape[1], NDEV * g.shape[2])


def _cols_to_parts(full):
    r, ctot = full.shape
    return jnp.transpose(full.reshape(r, NDEV, ctot // NDEV), (1, 0, 2))


def kernel(x, c, w_ada, b_ada, norm1_g, w_in, dn_conv_w, dn_a_log, dn_dt_bias, dn_norm_g, dn_w_o, cf_conv_w, cf_ln_g, cf_ln_b, cf_w_o, w_out, norm2_g, ffn_w_up, ffn_conv_w, ffn_w_down, final_norm_g, loss_target, m_w_ada, m_b_ada, m_norm1_g, m_w_in, m_dn_conv_w, m_dn_a_log, m_dn_dt_bias, m_dn_norm_g, m_dn_w_o, m_cf_conv_w, m_cf_ln_g, m_cf_ln_b, m_cf_w_o, m_w_out, m_norm2_g, m_ffn_w_up, m_ffn_conv_w, m_ffn_w_down, m_final_norm_g, v_w_ada, v_b_ada, v_norm1_g, v_w_in, v_dn_conv_w, v_dn_a_log, v_dn_dt_bias, v_dn_norm_g, v_dn_w_o, v_cf_conv_w, v_cf_ln_g, v_cf_ln_b, v_cf_w_o, v_w_out, v_norm2_g, v_ffn_w_up, v_ffn_conv_w, v_ffn_w_down, v_final_norm_g):
    args = locals()
    w = {nm: args[nm] for nm in NAMES}
    mo = {nm: args["m_" + nm] for nm in NAMES}
    vo = {nm: args["v_" + nm] for nm in NAMES}
    shapes = {nm: w[nm].shape for nm in NAMES}
    px, py, pc = _my_pos()
    me = _dev_index(px, py, pc)

    def mat(a):
        return a.reshape(a.shape[-2:])

    pos = jnp.stack([pc, 2 * px + py]).astype(jnp.int32)

    first = ["w_in", "dn_conv_w", "cf_conv_w", "ffn_conv_w"]
    tr_in = lambda a: jnp.transpose(a, (0, 2, 1))
    got = _all_gather([tr_in(w["w_in"]).astype(BF16)] + [mat(w[nm]) for nm in first[1:]] + [c], "gather_first")
    full = {nm: _cols_from_gathered(g) for nm, g in zip(first[1:], got[1:-1])}
    c_all = got[-1].reshape(NDEV, D)
    w_in_p = _pad_win(got[0].reshape(NIN, D))

    late_shards = [mat(w[nm]).astype(BF16) for nm in LATE]
    late_bufs = [lax.empty((3,) + s.shape, BF16) for s in late_shards]
    late_lay = [LATE_LAYOUT[nm] for nm in LATE]
    l_send, l_recv, l_shards, l_bufs, l_token = _gather_ici_start(late_shards, late_bufs, late_lay, c_all, "gather_late_start")

    res = {}

    class Comm:
        token0 = l_token[0, 0]
        pending = {}

        @staticmethod
        def late_weights(after):
            shards, bufs = _gather_ici_wait(l_send, l_recv, l_shards, l_bufs, late_lay, after, "gather_late_wait")
            return _gather_pair(shards, bufs, late_lay, [LATE_SHAPE[nm] for nm in LATE], "gather_late_pair")

        @staticmethod
        def grads(group, gd):
            names = list(gd)
            lays = [LAYOUT[nm] for nm in names]
            gl = []
            for nm in names:
                if nm == "w_in":
                    gl.append(_unpad_win(gd[nm]).reshape(NDEV, NSH, D))
                else:
                    gl.append(_cols_to_parts(gd[nm]) if LAYOUT[nm][0] == "lead" else gd[nm])
            from_sib = _pair_exchange(gl, lays, "rs_pair_" + group)
            sums = [_pair_sum(pos, g, r, lay, "rs_pair_sum_" + nm) for nm, g, r, lay in zip(names, gl, from_sib, lays)]
            started = _chip_exchange_start(sums, "rs_chips_start_" + group)
            Comm.pending[group] = (names,) + tuple(started[:4])
            return started[4][0, 0]

        @staticmethod
        def finish(group, after):
            names, ssem, rsem, sums, lands = Comm.pending[group]
            sums, lands = _chip_exchange_wait(ssem, rsem, sums, lands, after, "rs_chips_wait_" + group)
            for nm, s, r in zip(names, sums, lands):
                if nm == "w_in":
                    outs = _final_sum_adam(pos, s, r, tr_in(w[nm]), tr_in(mo[nm]), tr_in(vo[nm]), "adam_" + nm)
                    res[nm] = [tr_in(o) for o in outs]
                else:
                    res[nm] = _final_sum_adam(pos, s, r, w[nm], mo[nm], vo[nm], "adam_" + nm)
            return res[names[-1]][0]

    ncol = 6 * D // NDEV
    b_sh = lax.dynamic_slice(b_ada.reshape(1, 6 * D), (0, me * ncol), (1, ncol))
    mod_sh = _ada_fwd(c_all, mat(w_ada), b_sh)
    mod_all = _all_gather([mod_sh], "gather_mod")[0]
    mod = lax.dynamic_index_in_dim(mod_all, me, axis=1, keepdims=False).reshape(1, 6 * D)

    vec = lambda a: a.reshape(1, -1)
    loss, grad_x, small = _local_step(
        x.reshape(S, D), loss_target.reshape(S, D), mod, vec(norm1_g), vec(norm2_g), vec(final_norm_g), w_in_p,
        full["dn_conv_w"], vec(dn_a_log), vec(dn_dt_bias), vec(dn_norm_g), full["cf_conv_w"], vec(cf_ln_g),
        vec(cf_ln_b), full["ffn_conv_w"], Comm)

    done_a = Comm.finish("a", grad_x)
    done_b = Comm.finish("b", done_a)

    small["b_ada"] = small.pop("mod")
    g_small = _all_gather([_pack_small(small)], "gather_small")[0].reshape(NDEV, -1)
    outs = _small_adam(g_small, _pack_small({nm: w[nm] for nm, _ in SMALL}), _pack_small({nm: mo[nm] for nm, _ in SMALL}),
                       _pack_small({nm: vo[nm] for nm, _ in SMALL}))
    unpacked = [_unpack_small(o, shapes) for o in outs]
    for nm, _ in SMALL:
        res[nm] = [u[nm] for u in unpacked]

    dmod_sel = lax.dynamic_slice(g_small[:, :6 * D], (0, me * ncol), (NDEV, ncol))
    outs = _ada_bwd_adam(c_all, dmod_sel, mat(w_ada), mat(m_w_ada), mat(v_w_ada))
    res["w_ada"] = [o.reshape(shapes["w_ada"]) for o in outs]
    Comm.finish("c", jnp.concatenate([done_b.reshape(-1)[:LANE], outs[0].reshape(-1)[:LANE]]))

    loss = lax.psum(loss.reshape(()), ("x", "y", "c"))
    out = [loss, grad_x.reshape(x.shape)]
    for k in range(4):
        out += [res[nm][k] for nm in NAMES]
    return tuple(out)
```

```python
import functools

import jax
import jax.numpy as jnp
from jax import lax
from jax.experimental import pallas as pl
from jax.experimental.pallas import tpu as pltpu

F32 = jnp.float32
BF16 = jnp.bfloat16
HI = lax.Precision.HIGHEST
MESH = pl.DeviceIdType.MESH
ANY = pl.BlockSpec(memory_space=pl.ANY)

NDEV = 8
D = 2048
S = 2048
H = 8
DH = 128
DNW = H * DH
CFW = 1024
CFK = 31
DNK = 4
FFN = 5632
FFK = 3
CH = 64
NCH = S // CH
EPS = 1e-6
NIN = 10256
NINP = 10368
O_Z, O_GA, O_GB, O_GLU, O_SM = 3072, 4096, 6144, 8192, 10240
LANE = 128
TS = 256
VMEM_LIMIT = 56 * 1024 * 1024

ADAM_LR, ADAM_B1, ADAM_B2, ADAM_EPS, ADAM_WD, ADAM_STEP = 0.001, 0.9, 0.999, 1e-08, 0.01, 10


def _call(body, name, out_shape, grid=(), in_specs=None, out_specs=None, scratch=(), sem=None, aliases=None):
    kw = {}
    if aliases:
        kw["input_output_aliases"] = aliases
    if in_specs is not None:
        kw["in_specs"] = in_specs
    if out_specs is not None:
        kw["out_specs"] = out_specs
    return pl.pallas_call(
        body, out_shape=out_shape, grid=grid, scratch_shapes=scratch, name=name,
        compiler_params=pltpu.CompilerParams(dimension_semantics=sem, vmem_limit_bytes=VMEM_LIMIT), **kw)


def _sds(shape, dtype=F32):
    return jax.ShapeDtypeStruct(shape, dtype)


def _tile(dim, pref):
    if dim <= pref:
        return dim
    best = None
    for t in range(LANE, pref + 1, LANE):
        if dim % t == 0:
            best = t
    assert best is not None, (dim, pref)
    return best


def _sigmoid(x):
    return 1.0 / (1.0 + jnp.exp(-x))


def _silu(x):
    return x * _sigmoid(x)


def _dsilu(x):
    s = _sigmoid(x)
    return s * (1.0 + x * (1.0 - s))


def _softplus(x):
    return jnp.maximum(x, 0.0) + jnp.log(1.0 + jnp.exp(-jnp.abs(x)))


def _dot(a, b, dims, precision=None):
    return lax.dot_general(a, b, (dims, ((), ())), preferred_element_type=F32, precision=precision)


NN = ((1,), (0,))
NT = ((1,), (1,))
TN = ((0,), (0,))


def _my_pos():
    return lax.axis_index("x"), lax.axis_index("y"), lax.axis_index("c")


def _mm(a, b, mode, out_dtype, name, tm=1024, tn=1024, tk=2048, a2=None, b2=None):
    sharded = b.ndim == 3
    if sharded and mode == "nn":
        cs = b.shape[2]
        (m, k), n = a.shape, NDEV * cs
        gs = max(1, tn // cs)
        tm, tn, tk = _tile(m, tm), gs * cs, _tile(k, tk)
    elif sharded:
        assert mode == "nt"
        cs = b.shape[2]
        m, n, k = a.shape[0], b.shape[1], NDEV * cs
        gs = max(1, tk // cs)
        tm, tn, tk = _tile(m, tm), _tile(n, tn), gs * cs
    else:
        if mode == "nn":
            (m, k), (k2, n) = a.shape, b.shape
        elif mode == "nt":
            (m, k), (n, k2) = a.shape, b.shape
        else:
            (k, m), (k2, n) = a.shape, b.shape
        assert k == k2, (a.shape, b.shape, mode)
        n = n * (2 if b2 is not None else 1)
        tm, tn, tk = _tile(m, tm), _tile(n // (2 if b2 is not None else 1), tn), _tile(k, tk)
    nk, nj = k // tk, n // tn
    halfk, halfj = nk // 2, nj // 2
    dims = {"nn": NN, "nt": NT, "tn": TN}[mode]

    def body(*refs):
        a_ref, b_ref = refs[0], refs[1]
        x_ref = refs[2] if (a2 is not None or b2 is not None) else None
        o_ref, acc_ref = refs[-2], refs[-1]
        j, kk = pl.program_id(1), pl.program_id(2)

        def accumulate(part, cols=slice(None)):
            @pl.when(kk == 0)
            def _():
                acc_ref[:, cols] = part

            @pl.when(kk > 0)
            def _():
                acc_ref[:, cols] += part

        if sharded and mode == "nn":
            for q in range(gs):
                accumulate(_dot(a_ref[...], b_ref[q], NN), slice(q * cs, (q + 1) * cs))
        elif sharded:
            def contract(lhs_ref):
                part = None
                for q in range(gs):
                    term = _dot(lhs_ref[:, q * cs:(q + 1) * cs], b_ref[q], NT)
                    part = term if part is None else part + term
                accumulate(part)

            if a2 is None:
                contract(a_ref)
            else:
                pl.when(kk < halfk)(lambda: contract(a_ref))
                pl.when(kk >= halfk)(lambda: contract(x_ref))
        elif b2 is not None:
            pl.when(j < halfj)(lambda: accumulate(_dot(a_ref[...], b_ref[...], dims)))
            pl.when(j >= halfj)(lambda: accumulate(_dot(a_ref[...], x_ref[...], dims)))
        else:
            accumulate(_dot(a_ref[...], b_ref[...], dims))

        @pl.when(kk == nk - 1)
        def _():
            o_ref[...] = acc_ref[...].astype(o_ref.dtype)

    ins, in_specs = [a], []
    if mode == "tn":
        in_specs.append(pl.BlockSpec((tk, tm), lambda i, j, kk: (kk, i)))
    elif a2 is not None:
        in_specs.append(pl.BlockSpec((tm, tk), lambda i, j, kk: (i, jnp.minimum(kk, halfk - 1))))
    else:
        in_specs.append(pl.BlockSpec((tm, tk), lambda i, j, kk: (i, kk)))
    ins.append(b)
    if sharded and mode == "nn":
        in_specs.append(pl.BlockSpec((gs, tk, cs), lambda i, j, kk: (j, kk, 0)))
    elif sharded:
        in_specs.append(pl.BlockSpec((gs, tn, cs), lambda i, j, kk: (kk, j, 0)))
    elif mode == "nt":
        in_specs.append(pl.BlockSpec((tn, tk), lambda i, j, kk: (j, kk)))
    elif b2 is not None:
        in_specs.append(pl.BlockSpec((tk, tn), lambda i, j, kk: (kk, jnp.minimum(j, halfj - 1))))
    else:
        in_specs.append(pl.BlockSpec((tk, tn), lambda i, j, kk: (kk, j)))
    if a2 is not None:
        ins.append(a2)
        in_specs.append(pl.BlockSpec((tm, tk), lambda i, j, kk: (i, jnp.maximum(kk - halfk, 0))))
    if b2 is not None:
        ins.append(b2)
        in_specs.append(pl.BlockSpec((tk, tn), lambda i, j, kk: (kk, jnp.maximum(j - halfj, 0))))
    return _call(body, name, _sds((m, n), out_dtype), grid=(m // tm, nj, nk),
                 in_specs=in_specs, out_specs=pl.BlockSpec((tm, tn), lambda i, j, kk: (i, j)),
                 scratch=[pltpu.VMEM((tm, tn), F32)], sem=("parallel", "parallel", "arbitrary"))(*ins)


def _ada_fwd(c_all, w_sh, b_sh):
    n = w_sh.shape[1]
    tn = 512

    def body(c_ref, w_ref, b_ref, o_ref):
        ca = _silu(c_ref[...]).astype(BF16)
        o_ref[...] = _dot(ca, w_ref[...].astype(BF16), NN) + b_ref[...]

    return _call(body, "ada_fwd", _sds((NDEV, n)), grid=(n // tn,),
                 in_specs=[pl.BlockSpec((NDEV, D), lambda j: (0, 0)), pl.BlockSpec((D, tn), lambda j: (0, j)),
                           pl.BlockSpec((1, tn), lambda j: (0, j))],
                 out_specs=pl.BlockSpec((NDEV, tn), lambda j: (0, j)), sem=("parallel",))(c_all, w_sh, b_sh)


def _adam(w, g, m, v):
    m = ADAM_B1 * m + (1.0 - ADAM_B1) * g
    v = ADAM_B2 * v + (1.0 - ADAM_B2) * (g * g)
    m_hat = m / (1.0 - ADAM_B1 ** ADAM_STEP)
    v_hat = v / (1.0 - ADAM_B2 ** ADAM_STEP)
    delta = -ADAM_LR * (m_hat / (jnp.sqrt(v_hat) + ADAM_EPS) + ADAM_WD * w)
    return delta, m, v


def _ada_bwd_adam(c_all, dmod_sel, w, m, v):
    r, n = w.shape
    tr = 256

    def body(c_ref, d_ref, w_ref, m_ref, v_ref, g_ref, dl_ref, nm_ref, nv_ref):
        ca = _silu(c_ref[...])
        g = _dot(ca, d_ref[...], TN, precision=HI)
        dl, nm, nv = _adam(w_ref[...], g, m_ref[...], v_ref[...])
        g_ref[...] = g
        dl_ref[...] = dl
        nm_ref[...] = nm
        nv_ref[...] = nv

    big = pl.BlockSpec((tr, n), lambda i: (i, 0))
    return _call(body, "ada_bwd_adam", [_sds((r, n))] * 4, grid=(r // tr,),
                 in_specs=[pl.BlockSpec((NDEV, tr), lambda i: (0, i)), pl.BlockSpec((NDEV, n), lambda i: (0, 0)),
                           big, big, big],
                 out_specs=[big] * 4, sem=("parallel",))(c_all, dmod_sel, w, m, v)


def _row_spec(width=D):
    return pl.BlockSpec((TS, width), lambda i: (i, 0))


def _vec_spec(width=D):
    return pl.BlockSpec((1, width), lambda i: (0, 0))


def _acc_spec(width=D):
    return pl.BlockSpec((8, width), lambda i: (0, 0))


def _norm_mod(x, g, sc, sh, name):
    def body(x_ref, g_ref, sc_ref, sh_ref, o_ref):
        xv = x_ref[...]
        r = lax.rsqrt(jnp.mean(xv * xv, axis=-1, keepdims=True) + EPS)
        o_ref[...] = ((xv * r) * g_ref[...] * (1.0 + sc_ref[...]) + sh_ref[...]).astype(BF16)

    return _call(body, name, _sds((S, D), BF16), grid=(S // TS,),
                 in_specs=[_row_spec(), _vec_spec(), _vec_spec(), _vec_spec()], out_specs=_row_spec(),
                 sem=("parallel",))(x, g, sc, sh)


def _resid_norm_mod(x, mix, gt, g, sc, sh, name):
    def body(x_ref, mix_ref, gt_ref, g_ref, sc_ref, sh_ref, x2_ref, o_ref):
        xv = x_ref[...] + gt_ref[...] * mix_ref[...]
        x2_ref[...] = xv
        r = lax.rsqrt(jnp.mean(xv * xv, axis=-1, keepdims=True) + EPS)
        o_ref[...] = ((xv * r) * g_ref[...] * (1.0 + sc_ref[...]) + sh_ref[...]).astype(BF16)

    return _call(body, name, [_sds((S, D)), _sds((S, D), BF16)], grid=(S // TS,),
                 in_specs=[_row_spec(), _row_spec()] + [_vec_spec()] * 4, out_specs=[_row_spec(), _row_spec()],
                 sem=("parallel",))(x, mix, gt, g, sc, sh)


def _acc_rows(acc_ref, rows):
    @pl.when(pl.program_id(0) == 0)
    def _():
        acc_ref[...] = jnp.zeros_like(acc_ref)

    for k, row in enumerate(rows):
        acc_ref[k:k + 1, :] += row


def _loss_head(x2, f, tgt, gt2, gf):
    def body(x2_ref, f_ref, t_ref, gt_ref, gf_ref, dx_ref, df_ref, acc_ref):
        fv = f_ref[...]
        x3 = x2_ref[...] + gt_ref[...] * fv
        r = lax.rsqrt(jnp.mean(x3 * x3, axis=-1, keepdims=True) + EPS)
        xn = x3 * r
        e = xn * gf_ref[...] - t_ref[...]
        loss = 0.5 * jnp.sum(jnp.mean(e * e, axis=-1, keepdims=True), axis=0, keepdims=True)
        dy = e * (1.0 / D)
        dxn = dy * gf_ref[...]
        dx3 = r * (dxn - xn * jnp.mean(dxn * xn, axis=-1, keepdims=True))
        dx_ref[...] = dx3
        df_ref[...] = (dx3 * gt_ref[...]).astype(BF16)
        _acc_rows(acc_ref, [jnp.sum(dy * xn, axis=0, keepdims=True), jnp.sum(dx3 * fv, axis=0, keepdims=True),
                            jnp.broadcast_to(loss, (1, D))])

    return _call(body, "loss_head", [_sds((S, D)), _sds((S, D), BF16), _sds((8, D))], grid=(S // TS,),
                 in_specs=[_row_spec(), _row_spec(), _row_spec(), _vec_spec(), _vec_spec()],
                 out_specs=[_row_spec(), _row_spec(), _acc_spec()], sem=("arbitrary",))(x2, f, tgt, gt2, gf)


def _norm_mod_bwd(dhn, x, dres, g, sc, name, mix=None, gt=None):
    gated = mix is not None

    def body(*refs):
        if gated:
            dhn_ref, x_ref, dres_ref, g_ref, sc_ref, mix_ref, gt_ref, dx_ref, dmix_ref, acc_ref = refs
        else:
            dhn_ref, x_ref, dres_ref, g_ref, sc_ref, dx_ref, acc_ref = refs
        xv = x_ref[...]
        dh = dhn_ref[...]
        r = lax.rsqrt(jnp.mean(xv * xv, axis=-1, keepdims=True) + EPS)
        xn = xv * r
        gv = g_ref[...]
        sc1 = 1.0 + sc_ref[...]
        dxn = dh * gv * sc1
        dx = dres_ref[...] + r * (dxn - xn * jnp.mean(dxn * xn, axis=-1, keepdims=True))
        dx_ref[...] = dx
        rows = [jnp.sum(dh, axis=0, keepdims=True), jnp.sum(dh * xn * gv, axis=0, keepdims=True),
                jnp.sum(dh * xn * sc1, axis=0, keepdims=True)]
        if gated:
            rows.append(jnp.sum(dx * mix_ref[...], axis=0, keepdims=True))
            dmix_ref[...] = (dx * gt_ref[...]).astype(BF16)
        _acc_rows(acc_ref, rows)

    ins = [dhn, x, dres, g, sc]
    in_specs = [_row_spec(), _row_spec(), _row_spec(), _vec_spec(), _vec_spec()]
    outs = [_sds((S, D))]
    out_specs = [_row_spec()]
    if gated:
        ins += [mix, gt]
        in_specs += [_row_spec(), _vec_spec()]
        outs.append(_sds((S, D), BF16))
        out_specs.append(_row_spec())
    outs.append(_sds((8, D)))
    out_specs.append(_acc_spec())
    return _call(body, name, outs, grid=(S // TS,), in_specs=in_specs, out_specs=out_specs,
                 sem=("arbitrary",))(*ins)


RC = 256


def _conv_fwd_rows(pad_ref, w_ref, kw, head, r0):
    acc = None
    for k in range(kw):
        term = w_ref[k:k + 1, :] * pad_ref[pl.ds(head - (kw - 1) + k + r0, RC), :]
        acc = term if acc is None else acc + term
    return acc


def _conv_bwd_rows(pad2_ref, w_ref, kw, r0):
    acc = None
    for k in range(kw):
        term = w_ref[k:k + 1, :] * pad2_ref[pl.ds(kw - 1 - k + r0, RC), :]
        acc = term if acc is None else acc + term
    return acc


def _conv_dw(pad_ref, dout_ref, dw_ref, kw, head):
    for k in range(kw):
        acc = None
        for r0 in range(0, S, RC):
            term = jnp.sum(pad_ref[pl.ds(head - (kw - 1) + k + r0, RC), :] * dout_ref[pl.ds(r0, RC), :],
                           axis=0, keepdims=True)
            acc = term if acc is None else acc + term
        dw_ref[k:k + 1, :] = acc


def _col_spec(width, off_blocks=0):
    return pl.BlockSpec((S, width), lambda j: (0, j + off_blocks))


def _dn_pre_fwd(proj, conv_w):
    head = 8

    def body(x_ref, w_ref, o_ref, pad_ref):
        j = pl.program_id(0)
        pad_ref[pl.ds(0, head), :] = jnp.zeros((head, DH), F32)
        pad_ref[pl.ds(head, S), :] = x_ref[...]
        scale = jnp.where(j < H, DH ** -0.5, 1.0)
        for r0 in range(0, S, RC):
            y = _silu(_conv_fwd_rows(pad_ref, w_ref, DNK, head, r0))
            rinv = lax.rsqrt(jnp.sum(y * y, axis=-1, keepdims=True) + EPS)
            o_ref[pl.ds(r0, RC), :] = jnp.where(j < 2 * H, y * rinv * scale, y)

    return _call(body, "dn_pre_fwd", _sds((S, 3 * DNW)), grid=(3 * H,),
                 in_specs=[_col_spec(DH), pl.BlockSpec((DNK, DH), lambda j: (0, j))], out_specs=_col_spec(DH),
                 scratch=[pltpu.VMEM((S + head, DH), F32)], sem=("parallel",))(proj, conv_w)


def _dn_pre_bwd(dq, dk, dv, proj, conv_w, dproj):
    head = 8

    def body(dq_ref, dk_ref, dv_ref, x_ref, w_ref, dproj_in, dx_ref, dw_ref, pad_ref, pad2_ref):
        j = pl.program_id(0)
        pad_ref[pl.ds(0, head), :] = jnp.zeros((head, DH), F32)
        pad_ref[pl.ds(head, S), :] = x_ref[...]
        pad2_ref[pl.ds(S, head), :] = jnp.zeros((head, DH), F32)
        scale = jnp.where(j < H, DH ** -0.5, 1.0)
        for r0 in range(0, S, RC):
            xc = _conv_fwd_rows(pad_ref, w_ref, DNK, head, r0)
            y = _silu(xc)
            rinv = lax.rsqrt(jnp.sum(y * y, axis=-1, keepdims=True) + EPS)
            yn = y * rinv
            rows = pl.ds(r0, RC)
            do = jnp.where(j < H, dq_ref[rows, :], jnp.where(j < 2 * H, dk_ref[rows, :], dv_ref[rows, :]))
            dy_n = scale * rinv * (do - yn * jnp.sum(do * yn, axis=-1, keepdims=True))
            dy = jnp.where(j < 2 * H, dy_n, do)
            pad2_ref[rows, :] = dy * _dsilu(xc)
        for r0 in range(0, S, RC):
            dx_ref[pl.ds(r0, RC), :] = _conv_bwd_rows(pad2_ref, w_ref, DNK, r0).astype(BF16)
        _conv_dw(pad_ref, pad2_ref, dw_ref, DNK, head)

    wspec = pl.BlockSpec((DNK, DH), lambda j: (0, j))
    head_col = lambda lo: pl.BlockSpec((S, DH), lambda j: (0, jnp.clip(j - lo, 0, H - 1)))
    return _call(body, "dn_pre_bwd", [_sds((S, NINP), BF16), _sds((DNK, 3 * DNW))], grid=(3 * H,),
                 in_specs=[head_col(0), head_col(H), head_col(2 * H), _col_spec(DH), wspec, ANY],
                 out_specs=[_col_spec(DH), wspec],
                 scratch=[pltpu.VMEM((S + head, DH), F32), pltpu.VMEM((S + head, DH), F32)],
                 sem=("parallel",), aliases={5: 0})(dq, dk, dv, proj, conv_w, dproj)


CF_HEAD = 32
CF_VAL = pl.BlockSpec((S, LANE), lambda j: (0, O_GLU // LANE + 2 * j))
CF_GL = pl.BlockSpec((S, LANE), lambda j: (0, O_GLU // LANE + 2 * j + 1))


def _cf_conv_fwd(proj, conv_w):
    def body(val_ref, gl_ref, w_ref, o_ref, pad_ref):
        pad_ref[pl.ds(0, CF_HEAD), :] = jnp.zeros((CF_HEAD, LANE), F32)
        pad_ref[pl.ds(CF_HEAD, S), :] = val_ref[...] * _sigmoid(gl_ref[...])
        for r0 in range(0, S, RC):
            o_ref[pl.ds(r0, RC), :] = _conv_fwd_rows(pad_ref, w_ref, CFK, CF_HEAD, r0)

    wspec = pl.BlockSpec((CFK, LANE), lambda j: (0, j))
    return _call(body, "cf_conv_fwd", _sds((S, CFW)), grid=(CFW // LANE,),
                 in_specs=[CF_VAL, CF_GL, wspec], out_specs=_col_spec(LANE),
                 scratch=[pltpu.VMEM((S + CF_HEAD, LANE), F32)], sem=("parallel",))(proj, proj, conv_w)


def _cf_conv_bwd(du1, proj, conv_w, dproj):
    def body(d_ref, val_ref, gl_ref, w_ref, dproj_in, dp_ref, dw_ref, pad_ref, pad2_ref):
        sg = _sigmoid(gl_ref[...])
        pad_ref[pl.ds(0, CF_HEAD), :] = jnp.zeros((CF_HEAD, LANE), F32)
        pad_ref[pl.ds(CF_HEAD, S), :] = val_ref[...] * sg
        pad2_ref[pl.ds(0, S), :] = d_ref[...]
        pad2_ref[pl.ds(S, CF_HEAD), :] = jnp.zeros((CF_HEAD, LANE), F32)
        for r0 in range(0, S, RC):
            du0 = _conv_bwd_rows(pad2_ref, w_ref, CFK, r0)
            rows = pl.ds(r0, RC)
            sgr = _sigmoid(gl_ref[rows, :])
            dp_ref[rows, 0:LANE] = (du0 * sgr).astype(BF16)
            dp_ref[rows, LANE:2 * LANE] = (du0 * val_ref[rows, :] * sgr * (1.0 - sgr)).astype(BF16)
        _conv_dw(pad_ref, pad2_ref, dw_ref, CFK, CF_HEAD)

    wspec = pl.BlockSpec((CFK, LANE), lambda j: (0, j))
    return _call(body, "cf_conv_bwd", [_sds((S, NINP), BF16), _sds((CFK, CFW))], grid=(CFW // LANE,),
                 in_specs=[_col_spec(LANE), CF_VAL, CF_GL, wspec, ANY],
                 out_specs=[pl.BlockSpec((S, 2 * LANE), lambda j: (0, O_GLU // (2 * LANE) + j)), wspec],
                 scratch=[pltpu.VMEM((S + CF_HEAD, LANE), F32), pltpu.VMEM((S + CF_HEAD, LANE), F32)],
                 sem=("parallel",), aliases={4: 0})(du1, proj, proj, conv_w, dproj)


def _cf_ln_fwd(u1, g, b):
    def body(u_ref, g_ref, b_ref, o_ref):
        u = u_ref[...]
        mu = jnp.mean(u, axis=-1, keepdims=True)
        xc = u - mu
        y = xc * lax.rsqrt(jnp.mean(xc * xc, axis=-1, keepdims=True) + EPS)
        o_ref[...] = _silu(y * g_ref[...] + b_ref[...]).astype(BF16)

    return _call(body, "cf_ln_fwd", _sds((S, CFW), BF16), grid=(S // TS,),
                 in_specs=[_row_spec(CFW), _vec_spec(CFW), _vec_spec(CFW)], out_specs=_row_spec(CFW),
                 sem=("parallel",))(u1, g, b)


def _cf_ln_bwd(du3, u1, g, b):
    def body(d_ref, u_ref, g_ref, b_ref, du_ref, acc_ref):
        u = u_ref[...]
        mu = jnp.mean(u, axis=-1, keepdims=True)
        xc = u - mu
        rstd = lax.rsqrt(jnp.mean(xc * xc, axis=-1, keepdims=True) + EPS)
        xh = xc * rstd
        du2 = d_ref[...] * _dsilu(xh * g_ref[...] + b_ref[...])
        dxh = du2 * g_ref[...]
        du_ref[...] = rstd * (dxh - jnp.mean(dxh, axis=-1, keepdims=True)
                              - xh * jnp.mean(dxh * xh, axis=-1, keepdims=True))
        _acc_rows(acc_ref, [jnp.sum(du2 * xh, axis=0, keepdims=True), jnp.sum(du2, axis=0, keepdims=True)])

    return _call(body, "cf_ln_bwd", [_sds((S, CFW)), _sds((8, CFW))], grid=(S // TS,),
                 in_specs=[_row_spec(CFW), _row_spec(CFW), _vec_spec(CFW), _vec_spec(CFW)],
                 out_specs=[_row_spec(CFW), _acc_spec(CFW)], sem=("arbitrary",))(du3, u1, g, b)


FB = 256
FNB = FFN // FB
FF_HEAD = 8


def _ffn_mid_fwd(upall, conv_w):
    def body(gate_ref, up_ref, w_ref, o_ref, pad_ref):
        pad_ref[pl.ds(0, FF_HEAD), :] = jnp.zeros((FF_HEAD, FB), F32)
        pad_ref[pl.ds(FF_HEAD, S), :] = gate_ref[...]
        for r0 in range(0, S, RC):
            gc = _conv_fwd_rows(pad_ref, w_ref, FFK, FF_HEAD, r0)
            o_ref[pl.ds(r0, RC), :] = (_silu(gc) * up_ref[pl.ds(r0, RC), :]).astype(BF16)

    wspec = pl.BlockSpec((FFK, FB), lambda j: (0, j))
    return _call(body, "ffn_mid_fwd", _sds((S, FFN), BF16), grid=(FNB,),
                 in_specs=[_col_spec(FB), _col_spec(FB, FNB), wspec], out_specs=_col_spec(FB),
                 scratch=[pltpu.VMEM((S + FF_HEAD, FB), F32)], sem=("parallel",))(upall, upall, conv_w)


def _ffn_mid_bwd(dh, upall, conv_w):
    def body(d_ref, gate_ref, up_ref, w_ref, dgate_ref, dup_ref, dw_ref, pad_ref, pad2_ref):
        pad_ref[pl.ds(0, FF_HEAD), :] = jnp.zeros((FF_HEAD, FB), F32)
        pad_ref[pl.ds(FF_HEAD, S), :] = gate_ref[...]
        pad2_ref[pl.ds(S, FF_HEAD), :] = jnp.zeros((FF_HEAD, FB), F32)
        for r0 in range(0, S, RC):
            rows = pl.ds(r0, RC)
            gc = _conv_fwd_rows(pad_ref, w_ref, FFK, FF_HEAD, r0)
            dhv = d_ref[rows, :]
            dup_ref[rows, :] = (dhv * _silu(gc)).astype(BF16)
            pad2_ref[rows, :] = dhv * up_ref[rows, :] * _dsilu(gc)
        for r0 in range(0, S, RC):
            dgate_ref[pl.ds(r0, RC), :] = _conv_bwd_rows(pad2_ref, w_ref, FFK, r0).astype(BF16)
        _conv_dw(pad_ref, pad2_ref, dw_ref, FFK, FF_HEAD)

    wspec = pl.BlockSpec((FFK, FB), lambda j: (0, j))
    return _call(body, "ffn_mid_bwd", [_sds((S, FFN), BF16), _sds((S, FFN), BF16), _sds((FFK, FFN))],
                 grid=(FNB,), in_specs=[_col_spec(FB), _col_spec(FB), _col_spec(FB, FNB), wspec],
                 out_specs=[_col_spec(FB), _col_spec(FB), wspec],
                 scratch=[pltpu.VMEM((S + FF_HEAD, FB), F32), pltpu.VMEM((S + FF_HEAD, FB), F32)],
                 sem=("parallel",))(dh, upall, upall, conv_w)


GT = 256
SM_BLK = O_SM // LANE


def _chunk_tri(lower):
    r = lax.broadcasted_iota(jnp.int32, (GT, GT), 0)
    c = lax.broadcasted_iota(jnp.int32, (GT, GT), 1)
    same = (r // CH) == (c // CH)
    tri = (c <= r) if lower else (c >= r)
    return jnp.where(same & tri, 1.0, 0.0).astype(F32)


def _gates_fwd(proj, alog_v, dtb_v):
    def body(sm_ref, al_ref, dt_ref, o_ref):
        lane = lax.broadcasted_iota(jnp.int32, (GT, LANE), 1)
        tri = _chunk_tri(True)
        na = -jnp.exp(al_ref[...])
        for r0 in range(0, S, GT):
            sm = sm_ref[pl.ds(r0, GT), :]
            raw = jnp.where((lane >= H) & (lane < 2 * H), na * _softplus(sm + dt_ref[...]), 0.0)
            gc = _dot(tri, raw, NN, precision=HI)
            o_ref[pl.ds(r0, GT), :] = jnp.where(lane < H, _sigmoid(sm), gc)

    return _call(body, "gates_fwd", _sds((S, LANE)), grid=(1,),
                 in_specs=[pl.BlockSpec((S, LANE), lambda i: (0, SM_BLK)), _vec_spec(LANE), _vec_spec(LANE)],
                 out_specs=pl.BlockSpec((S, LANE), lambda i: (0, 0)), sem=("arbitrary",))(proj, alog_v, dtb_v)


def _gates_bwd(dgb, proj, alog_v, dtb_v, dproj):
    def body(d_ref, sm_ref, al_ref, dt_ref, dproj_in, o_ref, acc_ref):
        lane = lax.broadcasted_iota(jnp.int32, (GT, LANE), 1)
        is_g = (lane >= H) & (lane < 2 * H)
        tri = _chunk_tri(False)
        na = -jnp.exp(al_ref[...])
        d_al = jnp.zeros((1, LANE), F32)
        d_dt = jnp.zeros((1, LANE), F32)
        for r0 in range(0, S, GT):
            sm = sm_ref[pl.ds(r0, GT), :]
            dv = d_ref[pl.ds(r0, GT), :]
            z = sm + dt_ref[...]
            draw = _dot(tri, jnp.where(is_g, dv, 0.0), NN, precision=HI)
            dlogit = jnp.where(is_g, draw * na * _sigmoid(z), 0.0)
            d_al = d_al + jnp.sum(jnp.where(is_g, draw * na * _softplus(z), 0.0), axis=0, keepdims=True)
            d_dt = d_dt + jnp.sum(dlogit, axis=0, keepdims=True)
            bt = _sigmoid(sm)
            o_ref[pl.ds(r0, GT), :] = jnp.where(lane < H, dv * bt * (1.0 - bt), dlogit).astype(BF16)
        acc_ref[...] = jnp.zeros_like(acc_ref)
        acc_ref[0:1, :] = d_al
        acc_ref[1:2, :] = d_dt

    return _call(body, "gates_bwd", [_sds((S, NINP), BF16), _sds((8, LANE))], grid=(1,),
                 in_specs=[pl.BlockSpec((S, LANE), lambda i: (0, 0)), pl.BlockSpec((S, LANE), lambda i: (0, SM_BLK)),
                           _vec_spec(LANE), _vec_spec(LANE), ANY],
                 out_specs=[pl.BlockSpec((S, LANE), lambda i: (0, SM_BLK)), _acc_spec(LANE)],
                 sem=("arbitrary",), aliases={4: 0})(dgb, proj, alog_v, dtb_v, dproj)


def _neumann_inv(a, eye):
    x = -a
    t = eye + x
    p = x
    for _ in range(5):
        p = _dot(p, p, NN, precision=HI)
        t = t + _dot(t, p, NN, precision=HI)
    return t


def _head_specs():
    q = pl.BlockSpec((S, DH), lambda h: (0, h))
    k = pl.BlockSpec((S, DH), lambda h: (0, H + h))
    v = pl.BlockSpec((S, DH), lambda h: (0, 2 * H + h))
    gb = pl.BlockSpec((None, S, DH), lambda h: (h, 0, 0))
    gr = pl.BlockSpec((None, NCH, CH), lambda h: (h, 0, 0))
    return q, k, v, gb, gr


ST_SPEC = pl.BlockSpec((None, NCH, DH, DH), lambda h: (h, 0, 0, 0))
TM_SPEC = pl.BlockSpec((None, NCH, CH, CH), lambda h: (h, 0, 0, 0))


def _delta_fwd(qkvn, gb, gr, bb):
    def body(q_ref, k_ref, v_ref, gb_ref, gr_ref, bb_ref, o_ref, st_ref, tm_ref):
        ri = lax.broadcasted_iota(jnp.int32, (CH, CH), 0)
        ci = lax.broadcasted_iota(jnp.int32, (CH, CH), 1)
        strict = ri > ci
        causal = ri >= ci
        eye = jnp.where(ri == ci, 1.0, 0.0).astype(F32)

        def step(n, st):
            rows = pl.ds(pl.multiple_of(n * CH, CH), CH)
            q, k, v, g, beta = q_ref[rows, :], k_ref[rows, :], v_ref[rows, :], gb_ref[rows, :], bb_ref[rows, :]
            diff = g[:, :CH] - gr_ref[pl.ds(n, 1), :]
            el = jnp.exp(jnp.where(causal, diff, 0.0))
            eg = jnp.exp(g)
            gl = g[CH - 1:CH, :]
            kb = k * beta
            kbf = k.astype(BF16)
            a = jnp.where(strict, _dot(kb.astype(BF16), kbf, NT) * el, 0.0)
            t = _neumann_inv(a, eye)
            tm_ref[n] = t
            st_ref[n] = st
            sb = st.astype(BF16)
            r = v * beta - _dot((kb * eg).astype(BF16), sb, NN)
            ub = _dot(t, r, NN, precision=HI).astype(BF16)
            p = jnp.where(causal, _dot(q.astype(BF16), kbf, NT) * el, 0.0)
            o_ref[rows, :] = _dot((q * eg).astype(BF16), sb, NN) + _dot(p.astype(BF16), ub, NN)
            kd = k * jnp.exp(gl - g)
            return st * jnp.exp(gl) + _dot(kd.astype(BF16), ub, TN)

        lax.fori_loop(0, NCH, step, jnp.zeros((DH, DH), F32))

    q, k, v, gbs, grs = _head_specs()
    return _call(body, "delta_fwd", [_sds((S, DNW)), _sds((H, NCH, DH, DH)), _sds((H, NCH, CH, CH))], grid=(H,),
                 in_specs=[q, k, v, gbs, grs, gbs], out_specs=[pl.BlockSpec((S, DH), lambda h: (0, h)), ST_SPEC, TM_SPEC],
                 sem=("parallel",))(qkvn, qkvn, qkvn, gb, gr, bb)


def _delta_bwd(qkvn, gb, gr, bb, st_all, tm_all, do_all):
    def body(q_ref, k_ref, v_ref, gb_ref, gr_ref, bb_ref, st_ref, tm_ref, do_ref,
             dq_ref, dk_ref, dv_ref, dg_ref, db_ref):
        ri = lax.broadcasted_iota(jnp.int32, (CH, CH), 0)
        ci = lax.broadcasted_iota(jnp.int32, (CH, CH), 1)
        lo_s, lo_c, up_s, up_c = ri > ci, ri >= ci, ri < ci, ri <= ci
        last_row = lax.broadcasted_iota(jnp.int32, (CH, 1), 0) == CH - 1

        def rs(mat):
            return jnp.sum(mat, axis=1, keepdims=True)

        def total(mat):
            return jnp.sum(rs(mat), axis=0, keepdims=True)

        def step(i, ds):
            n = NCH - 1 - i
            rows = pl.ds(pl.multiple_of(n * CH, CH), CH)
            q, k, v, g, beta = q_ref[rows, :], k_ref[rows, :], v_ref[rows, :], gb_ref[rows, :], bb_ref[rows, :]
            do = do_ref[rows, :]
            t = tm_ref[n]
            st = st_ref[n]
            diff = g[:, :CH] - gr_ref[pl.ds(n, 1), :]
            el = jnp.exp(jnp.where(lo_c, diff, 0.0))
            eu = jnp.exp(jnp.where(up_c, -diff, 0.0))
            eg = jnp.exp(g)
            gl = g[CH - 1:CH, :]
            egl = jnp.exp(gl)
            ekd = jnp.exp(gl - g)
            kb = k * beta
            kbg = kb * eg
            qg = q * eg
            kd = k * ekd
            qb, kbf, kbb = q.astype(BF16), k.astype(BF16), kb.astype(BF16)
            kbgb, qgb, kdb = kbg.astype(BF16), qg.astype(BF16), kd.astype(BF16)
            sb, dob, dsb = st.astype(BF16), do.astype(BF16), ds.astype(BF16)
            r = v * beta - _dot(kbgb, sb, NN)
            u = _dot(t, r, NN, precision=HI)
            ub = u.astype(BF16)
            kk, qk = _dot(kbb, kbf, NT), _dot(qb, kbf, NT)
            kkt, qkt = _dot(kbf, kbb, NT), _dot(kbf, qb, NT)
            pt = jnp.where(up_c, qkt * eu, 0.0)
            du = _dot(pt.astype(BF16), dob, NN) + _dot(kdb, dsb, NN)
            dr = _dot(t, du, TN, precision=HI)
            drb = dr.astype(BF16)
            dpg = jnp.where(lo_c, _dot(dob, ub, NT), 0.0) * el
            dpgt = jnp.where(up_c, _dot(ub, dob, NT), 0.0) * eu
            dag = -jnp.where(lo_s, _dot(drb, ub, NT), 0.0) * el
            dagt = -jnp.where(up_s, _dot(ub, drb, NT), 0.0) * eu
            dqg = _dot(dob, sb, NT)
            dkbg = -_dot(drb, sb, NT)
            dkd = _dot(ub, dsb, NT)
            ds_new = _dot(qgb, dob, TN) + egl * ds - _dot(kbgb, drb, TN)
            dkb = _dot(dag.astype(BF16), kbf, NN) + dkbg * eg
            dk = (_dot(dagt.astype(BF16), kbb, NN) + _dot(dpgt.astype(BF16), qb, NN) + dkd * ekd + dkb * beta)
            dq = _dot(dpg.astype(BF16), kbf, NN) + dqg * eg
            dkd_kd = rs(dkd * kd)
            dg = (rs(dag * kk + dpg * qk) - rs(dagt * kkt + dpgt * qkt) + rs(dqg * qg) + rs(dkbg * kbg) - dkd_kd)
            dgl = jnp.sum(dkd_kd, axis=0, keepdims=True) + egl[:, 0:1] * total(ds * st)
            dg = dg + jnp.where(last_row, dgl, 0.0)
            dbeta = rs(dkb * k) + rs(dr * v)
            dq_ref[rows, :] = dq
            dk_ref[rows, :] = dk
            dv_ref[rows, :] = dr * beta
            spread = jnp.full((8, DH), 1.0 / DH, F32)
            dg_ref[pl.ds(n, 1), :] = _dot(spread, jnp.broadcast_to(dg, (CH, DH)), NT, precision=HI)[0:1, :]
            db_ref[pl.ds(n, 1), :] = _dot(spread, jnp.broadcast_to(dbeta, (CH, DH)), NT, precision=HI)[0:1, :]
            return ds_new

        lax.fori_loop(0, NCH, step, jnp.zeros((DH, DH), F32))

    q, k, v, gbs, grs = _head_specs()
    hcol = pl.BlockSpec((S, DH), lambda h: (0, h))
    return _call(body, "delta_bwd",
                 [_sds((S, DNW)), _sds((S, DNW)), _sds((S, DNW)), _sds((H, NCH, CH)), _sds((H, NCH, CH))], grid=(H,),
                 in_specs=[q, k, v, gbs, grs, gbs, ST_SPEC, TM_SPEC, hcol], out_specs=[hcol, hcol, hcol, grs, grs],
                 sem=("parallel",))(qkvn, qkvn, qkvn, gb, gr, bb, st_all, tm_all, do_all)


Z_BLK = O_Z // DNW


def _dn_post_fwd(o, proj, gn):
    def body(o_ref, z_ref, gn_ref, og_ref):
        for h in range(H):
            cols = slice(h * DH, (h + 1) * DH)
            ov = o_ref[:, cols]
            on = ov * lax.rsqrt(jnp.mean(ov * ov, axis=-1, keepdims=True) + EPS) * gn_ref[...]
            og_ref[:, cols] = (on * _silu(z_ref[:, cols])).astype(BF16)

    return _call(body, "dn_post_fwd", _sds((S, DNW), BF16), grid=(S // TS,),
                 in_specs=[_row_spec(DNW), pl.BlockSpec((TS, DNW), lambda i: (i, Z_BLK)), _vec_spec(DH)],
                 out_specs=_row_spec(DNW), sem=("parallel",))(o, proj, gn)


def _dn_post_bwd(dog, o, proj, gn, dproj):
    def body(d_ref, o_ref, z_ref, gn_ref, dproj_in, do_ref, dz_ref, acc_ref):
        dgn = jnp.zeros((1, DH), F32)
        for h in range(H):
            cols = slice(h * DH, (h + 1) * DH)
            ov, zv, dv = o_ref[:, cols], z_ref[:, cols], d_ref[:, cols]
            rinv = lax.rsqrt(jnp.mean(ov * ov, axis=-1, keepdims=True) + EPS)
            xn = ov * rinv
            don = dv * _silu(zv)
            dz_ref[:, cols] = (dv * xn * gn_ref[...] * _dsilu(zv)).astype(BF16)
            dgn = dgn + jnp.sum(don * xn, axis=0, keepdims=True)
            dxn = don * gn_ref[...]
            do_ref[:, cols] = rinv * (dxn - xn * jnp.mean(dxn * xn, axis=-1, keepdims=True))
        _acc_rows(acc_ref, [dgn])

    zspec = pl.BlockSpec((TS, DNW), lambda i: (i, Z_BLK))
    return _call(body, "dn_post_bwd", [_sds((S, DNW)), _sds((S, NINP), BF16), _sds((8, DH))], grid=(S // TS,),
                 in_specs=[_row_spec(DNW), _row_spec(DNW), zspec, _vec_spec(DH), ANY],
                 out_specs=[_row_spec(DNW), zspec, _acc_spec(DH)], sem=("arbitrary",),
                 aliases={4: 1})(dog, o, proj, gn, dproj)


GA_BLK = O_GA // D
GB_BLK = O_GB // D


def _merge_fwd(ba, bb, proj):
    def body(a_ref, b_ref, ga_ref, gb_ref, o_ref):
        o_ref[...] = (_sigmoid(ga_ref[...]) * a_ref[...] + _sigmoid(gb_ref[...]) * b_ref[...]).astype(BF16)

    return _call(body, "merge_fwd", _sds((S, D), BF16), grid=(S // TS,),
                 in_specs=[_row_spec(), _row_spec(), pl.BlockSpec((TS, D), lambda i: (i, GA_BLK)),
                           pl.BlockSpec((TS, D), lambda i: (i, GB_BLK))],
                 out_specs=_row_spec(), sem=("parallel",))(ba, bb, proj, proj)


def _merge_bwd(dm, ba, bb, proj, dproj):
    def body(d_ref, a_ref, b_ref, ga_ref, gb_ref, dproj_in, dg_ref, da_ref, db_ref):
        d = d_ref[...]
        sa, sb = _sigmoid(ga_ref[...]), _sigmoid(gb_ref[...])
        dg_ref[:, 0:D] = (d * a_ref[...] * sa * (1.0 - sa)).astype(BF16)
        dg_ref[:, D:2 * D] = (d * b_ref[...] * sb * (1.0 - sb)).astype(BF16)
        da_ref[...] = (d * sa).astype(BF16)
        db_ref[...] = (d * sb).astype(BF16)

    return _call(body, "merge_bwd", [_sds((S, NINP), BF16), _sds((S, D), BF16), _sds((S, D), BF16)], grid=(S // TS,),
                 in_specs=[_row_spec(), _row_spec(), _row_spec(), pl.BlockSpec((TS, D), lambda i: (i, GA_BLK)),
                           pl.BlockSpec((TS, D), lambda i: (i, GB_BLK)), ANY],
                 out_specs=[pl.BlockSpec((TS, 2 * D), lambda i: (i, O_GA // (2 * D))), _row_spec(), _row_spec()],
                 sem=("parallel",), aliases={5: 0})(dm, ba, bb, proj, proj, dproj)


NSH = NIN // NDEV


def _win_pieces():
    pieces = [(0, 0, 4096), (O_GA, 6160, 2 * D), (O_SM, 4096, 16)]
    for j in range(CFW // LANE):
        pieces.append((O_GLU + 2 * LANE * j, 4112 + LANE * j, LANE))
        pieces.append((O_GLU + 2 * LANE * j + LANE, 4112 + CFW + LANE * j, LANE))
    return pieces


def _pad_win(wt):
    rows = [wt[o:o + wdt] for _, o, wdt in sorted(_win_pieces())]
    rows.append(jnp.zeros((NINP - NIN, wt.shape[1]), wt.dtype))
    return jnp.concatenate(rows, axis=0)


def _unpad_win(gpt):
    return jnp.concatenate([gpt[p:p + wdt] for p, o, wdt in sorted(_win_pieces(), key=lambda t: t[1])], axis=0)


def _lane_vec(v8, offset):
    return jnp.pad(v8, ((0, 0), (offset, LANE - 8 - offset)))


def _tie(vec, token):
    return vec + token


def _local_step(x, tgt, mod, norm1_g, norm2_g, final_g, w_in_p, dn_conv_w, a_log, dt_bias, dn_norm_g,
                cf_conv_w, cf_ln_g, cf_ln_b, ffn_conv_w, comm):
    sh1, sc1, gt1, sh2, sc2, gt2 = (mod[:, i * D:(i + 1) * D] for i in range(6))
    alog_v, dtb_v = _lane_vec(a_log, H), _lane_vec(dt_bias, H)

    hn1 = _norm_mod(x, norm1_g, sc1, _tie(sh1, comm.token0), "norm_mod1")
    proj = _mm(hn1, w_in_p, "nt", F32, "mm_in", tn=1152)
    qkvn = _dn_pre_fwd(proj, dn_conv_w)
    gates = _gates_fwd(proj, alog_v, dtb_v)
    beta_t = gates[:, 0:H].T
    g_t = gates[:, H:2 * H].T
    gb = jnp.broadcast_to(g_t[:, :, None], (H, S, DH))
    bb = jnp.broadcast_to(beta_t[:, :, None], (H, S, DH))
    gr = g_t.reshape(H, NCH, CH)
    o, st_all, tm_all = _delta_fwd(qkvn, gb, gr, bb)
    og = _dn_post_fwd(o, proj, dn_norm_g)
    u1 = _cf_conv_fwd(proj, cf_conv_w)
    u3 = _cf_ln_fwd(u1, cf_ln_g, cf_ln_b)
    after = og[0:8, 0:LANE].astype(F32) + u3[0:8, 0:LANE].astype(F32)
    dn_w_o, cf_w_o, w_out, ffn_w_up, ffn_w_down = comm.late_weights(after)
    br_a = _mm(og, dn_w_o, "nn", F32, "mm_dn_o")
    br_b = _mm(u3, cf_w_o, "nn", F32, "mm_cf_o")
    merged = _merge_fwd(br_a, br_b, proj)
    mix = _mm(merged, w_out, "nn", F32, "mm_out")
    x2, hn2 = _resid_norm_mod(x, mix, gt1, norm2_g, sc2, sh2, "resid_norm_mod2")
    upall = _mm(hn2, ffn_w_up, "nn", F32, "mm_up")
    hmid = _ffn_mid_fwd(upall, ffn_conv_w)
    f = _mm(hmid, ffn_w_down, "nn", F32, "mm_down")

    dx3, df, acc_f = _loss_head(x2, f, tgt, gt2, final_g)
    d_final_g, d_gt2, loss = acc_f[0:1], acc_f[1:2], acc_f[2:3, 0:1]
    dhmid = _mm(df, ffn_w_down, "nt", F32, "mm_down_dx")
    g_w_down = _mm(hmid, df, "tn", BF16, "mm_down_dw")
    d_gate, d_up, g_ffn_conv = _ffn_mid_bwd(dhmid, upall, ffn_conv_w)
    dhn2 = _mm(d_gate, ffn_w_up, "nt", F32, "mm_up_dx", a2=d_up)
    g_w_up = _mm(hn2, d_gate, "tn", BF16, "mm_up_dw", tn=2 * FFN // NDEV, b2=d_up)
    tok_a = comm.grads("a", dict(ffn_w_down=g_w_down, ffn_w_up=g_w_up))
    dx2, dmix, acc2 = _norm_mod_bwd(dhn2, x2, dx3, _tie(norm2_g, tok_a), sc2, "norm_mod2_bwd", mix=mix, gt=gt1)
    d_sh2, d_sc2, d_norm2_g, d_gt1 = acc2[0:1], acc2[1:2], acc2[2:3], acc2[3:4]
    dmerged = _mm(dmix, w_out, "nt", F32, "mm_out_dx")
    g_w_out = _mm(merged, dmix, "tn", BF16, "mm_out_dw")
    d_proj, d_bra, d_brb = _merge_bwd(dmerged, br_a, br_b, proj, lax.empty((S, NINP), BF16))
    du3 = _mm(d_brb, cf_w_o, "nt", F32, "mm_cf_o_dx")
    g_cf_w_o = _mm(u3, d_brb, "tn", BF16, "mm_cf_o_dw")
    du1, acc_ln = _cf_ln_bwd(du3, u1, cf_ln_g, cf_ln_b)
    d_proj, g_cf_conv = _cf_conv_bwd(du1, proj, cf_conv_w, d_proj)
    dog = _mm(d_bra, dn_w_o, "nt", F32, "mm_dn_o_dx")
    g_dn_w_o = _mm(og, d_bra, "tn", BF16, "mm_dn_o_dw")
    tok_b = comm.grads("b", dict(w_out=g_w_out, cf_w_o=g_cf_w_o, dn_w_o=g_dn_w_o, ffn_conv_w=g_ffn_conv,
                                 cf_conv_w=g_cf_conv))
    do, d_proj, acc_gn = _dn_post_bwd(dog, o, proj, _tie(dn_norm_g, tok_b), d_proj)
    dq, dk, dv, dgr, dbr = _delta_bwd(qkvn, gb, gr, bb, st_all, tm_all, do)
    d_proj, g_dn_conv = _dn_pre_bwd(dq, dk, dv, proj, dn_conv_w, d_proj)
    dgates = jnp.concatenate([dbr.reshape(H, S).T, dgr.reshape(H, S).T, jnp.zeros((S, LANE - 2 * H), F32)], axis=1)
    d_proj, acc_g = _gates_bwd(dgates, proj, alog_v, dtb_v, d_proj)
    dhn1 = _mm(d_proj, w_in_p, "nn", F32, "mm_in_dx", tk=1152)
    g_w_in_p = _mm(d_proj, hn1, "tn", BF16, "mm_in_dw", tm=1152)
    tok_c = comm.grads("c", dict(w_in=g_w_in_p, dn_conv_w=g_dn_conv))
    grad_x, acc1 = _norm_mod_bwd(dhn1, x, dx2, _tie(norm1_g, tok_c), sc1, "norm_mod1_bwd")
    d_sh1, d_sc1, d_norm1_g = acc1[0:1], acc1[1:2], acc1[2:3]

    d_mod = jnp.concatenate([d_sh1, d_sc1, d_gt1, d_sh2, d_sc2, d_gt2], axis=1)
    small = dict(mod=d_mod, norm1_g=d_norm1_g, norm2_g=d_norm2_g, final_norm_g=d_final_g,
                 cf_ln_g=acc_ln[0:1], cf_ln_b=acc_ln[1:2], dn_norm_g=acc_gn[0:1],
                 dn_a_log=acc_g[0:1, H:2 * H], dn_dt_bias=acc_g[1:2, H:2 * H])
    return loss, grad_x, small


def _dev_index(px, py, pc):
    return 4 * px + 2 * py + pc


def _all_gather(arrs, name):
    n = len(arrs)

    def body(*refs):
        ins, outs = refs[:n], refs[n:2 * n]
        send_sems, recv_sems, loc_sems = refs[2 * n:]
        x, y, c = _my_pos()
        me, sib = (x, y, c), (x, y, 1 - c)
        chips = [(1 - x, y), (x, 1 - y), (1 - x, 1 - y)]

        def cp(i, k, block, to, src=None):
            dst = outs[i].at[_dev_index(*block)]
            return pltpu.make_async_remote_copy(
                src_ref=dst if src is None else src, dst_ref=dst, send_sem=send_sems.at[i, k],
                recv_sem=recv_sems.at[i, k], device_id=to, device_id_type=MESH)

        mine = [pltpu.make_async_copy(ins[i], outs[i].at[_dev_index(*me)], loc_sems.at[i]) for i in range(n)]
        for m in mine:
            m.start()
        sent = []
        for i in range(n):
            sent.append(cp(i, 0, me, sib, src=ins[i]))
            sent += [cp(i, 1 + j, me, (*chip, c), src=ins[i]) for j, chip in enumerate(chips)]
        for s in sent:
            s.start()
        for i in range(n):
            for j, chip in enumerate(chips):
                cp(i, 1 + j, (*chip, c), me).wait_recv()
                fwd = cp(i, 4 + j, (*chip, c), sib)
                fwd.start()
                sent.append(fwd)
        for i in range(n):
            cp(i, 0, sib, me).wait_recv()
            for j, chip in enumerate(chips):
                cp(i, 4 + j, (*chip, 1 - c), me).wait_recv()
        for s in sent:
            s.wait_send()
        for m in mine:
            m.wait()

    outs = pl.pallas_call(
        body, out_shape=[_sds((NDEV,) + a.shape, a.dtype) for a in arrs], in_specs=[ANY] * n, out_specs=[ANY] * n,
        scratch_shapes=[pltpu.SemaphoreType.DMA((n, 7)), pltpu.SemaphoreType.DMA((n, 7)), pltpu.SemaphoreType.DMA((n,))],
        name=name)(*arrs)
    return list(outs)


def _slab(ref, layout, idx):
    kind, n = layout
    if kind == "rows":
        return ref.at[pl.ds(pl.multiple_of(idx * n, n), n), :]
    if kind == "cols":
        return ref.at[:, pl.ds(pl.multiple_of(idx * n, n), n)]
    return ref.at[idx]


def _slab_shape(arr, layout):
    kind, n = layout
    if kind == "rows":
        return (n, arr.shape[1])
    if kind == "cols":
        return (arr.shape[0], n)
    return tuple(arr.shape[1:])


def _pair_exchange(parts, layouts, name):
    n = len(parts)

    def body(*refs):
        ins, outs = refs[:n], refs[n:2 * n]
        send_sems, recv_sems = refs[2 * n:]
        x, y, c = _my_pos()
        copies = []
        for i in range(n):
            for q in range(4):
                copies.append(pltpu.make_async_remote_copy(
                    src_ref=_slab(ins[i], layouts[i], 2 * q + (1 - c)), dst_ref=outs[i].at[q],
                    send_sem=send_sems.at[i, q], recv_sem=recv_sems.at[i, q], device_id=(x, y, 1 - c),
                    device_id_type=MESH))
        for cpy in copies:
            cpy.start()
        for cpy in copies:
            cpy.wait()

    outs = pl.pallas_call(
        body, out_shape=[_sds((4,) + _slab_shape(p, lay), p.dtype) for p, lay in zip(parts, layouts)],
        in_specs=[ANY] * n, out_specs=[ANY] * n,
        scratch_shapes=[pltpu.SemaphoreType.DMA((n, 4)), pltpu.SemaphoreType.DMA((n, 4))], name=name)(*parts)
    return list(outs)


HBM = pl.BlockSpec(memory_space=pltpu.HBM)
SEMS = pl.BlockSpec(memory_space=pltpu.SEMAPHORE)
EFFECT = pltpu.SideEffectType.DATAFLOW_SIDE_EFFECTING
TOKEN = jax.ShapeDtypeStruct((8, LANE), F32)


def _hbm(a):
    return pltpu.with_memory_space_constraint(a, pltpu.HBM)


def _gather_ici_copy(shard_ref, buf_ref, layout, send_sems, recv_sems, i, j, me, chip, c):
    return pltpu.make_async_remote_copy(
        src_ref=shard_ref, dst_ref=_slab(buf_ref, layout, me), send_sem=send_sems.at[3 * i + j],
        recv_sem=recv_sems.at[3 * i + j], device_id=(*chip, c), device_id_type=MESH)


def _gather_ici_start(shards, bufs, layouts, after, name):
    n = len(shards)

    def body(*refs):
        sh, bf = refs[:n], refs[n:2 * n]
        send_sems, recv_sems = refs[2 * n + 1], refs[2 * n + 2]
        token = refs[-1]
        x, y, c = _my_pos()
        me = _dev_index(x, y, c)
        for i in range(n):
            for j, chip in enumerate([(1 - x, y), (x, 1 - y), (1 - x, 1 - y)]):
                _gather_ici_copy(sh[i], bf[i], layouts[i], send_sems, recv_sems, i, j, me, chip, c).start()
        token[...] = jnp.zeros_like(token)

    outs = pl.pallas_call(
        body, name=name,
        out_shape=(pltpu.SemaphoreType.DMA((3 * n,)), pltpu.SemaphoreType.DMA((3 * n,)),
                   *[pltpu.HBM(a.shape, a.dtype) for a in shards], *[pltpu.HBM(a.shape, a.dtype) for a in bufs], TOKEN),
        in_specs=[HBM] * (2 * n) + [ANY],
        out_specs=(SEMS, SEMS, *[HBM] * (2 * n), pl.BlockSpec(memory_space=pltpu.VMEM)),
        input_output_aliases={i: 2 + i for i in range(2 * n)},
        compiler_params=pltpu.CompilerParams(has_side_effects=EFFECT),
    )(*[_hbm(a) for a in shards], *[_hbm(a) for a in bufs], after)
    return outs[0], outs[1], list(outs[2:2 + n]), list(outs[2 + n:2 + 2 * n]), outs[-1]


def _gather_ici_wait(send_sems, recv_sems, shards, bufs, layouts, after, name):
    n = len(shards)

    def body(*refs):
        sh, bf = refs[:n], refs[n:2 * n]
        ssem, rsem = refs[2 * n], refs[2 * n + 1]
        x, y, c = _my_pos()
        me = _dev_index(x, y, c)
        for i in range(n):
            for j, chip in enumerate([(1 - x, y), (x, 1 - y), (1 - x, 1 - y)]):
                cp = _gather_ici_copy(sh[i], bf[i], layouts[i], ssem, rsem, i, j, me, chip, c)
                cp.wait_send()
                cp.wait_recv()

    outs = pl.pallas_call(
        body, name=name,
        out_shape=(*[pltpu.HBM(a.shape, a.dtype) for a in shards], *[pltpu.HBM(a.shape, a.dtype) for a in bufs]),
        in_specs=[HBM] * (2 * n) + [SEMS, SEMS, ANY], out_specs=tuple([HBM] * (2 * n)),
        input_output_aliases={i: i for i in range(2 * n)},
        compiler_params=pltpu.CompilerParams(has_side_effects=EFFECT),
    )(*shards, *bufs, send_sems, recv_sems, after)
    return list(outs[:n]), list(outs[n:])


def _place_own(pos, shard, buf, layout, name):
    kind, n = layout
    r, cols = shard.shape
    tr = _row_tile(r, shard.dtype.itemsize)
    nr = r // tr
    if kind == "rows":
        ospec = pl.BlockSpec((tr, cols), lambda i, p: (p[2] * nr + i, 0))
    else:
        assert kind == "lead"
        ospec = pl.BlockSpec((None, tr, cols), lambda i, p: (p[2], i, 0))

    def body(pos_ref, s_ref, buf_in, o_ref):
        o_ref[...] = s_ref[...]

    return pl.pallas_call(
        body, out_shape=_sds(buf.shape, buf.dtype), name=name, input_output_aliases={2: 0},
        grid_spec=pltpu.PrefetchScalarGridSpec(
            num_scalar_prefetch=1, grid=(nr,), in_specs=[pl.BlockSpec((tr, cols), lambda i, p: (i, 0)), ANY],
            out_specs=ospec),
        compiler_params=pltpu.CompilerParams(dimension_semantics=("parallel",), vmem_limit_bytes=VMEM_LIMIT),
    )(pos, shard, buf)


def _gather_pair(shards, bufs, layouts, name):
    n = len(shards)

    def body(*refs):
        sh, bo = refs[:n], refs[2 * n:3 * n]
        send_sems, recv_sems = refs[3 * n:]
        x, y, c = _my_pos()
        sib = (x, y, 1 - c)
        copies = []
        for i in range(n):
            for k, (px, py) in enumerate([(x, y), (1 - x, y), (x, 1 - y), (1 - x, 1 - y)]):
                slab = _slab(bo[i], layouts[i], _dev_index(px, py, c))
                copies.append(pltpu.make_async_remote_copy(
                    src_ref=sh[i] if k == 0 else slab, dst_ref=slab, send_sem=send_sems.at[i, k],
                    recv_sem=recv_sems.at[i, k], device_id=sib, device_id_type=MESH))
        for cpy in copies:
            cpy.start()
        for cpy in copies:
            cpy.wait()

    outs = pl.pallas_call(
        body, out_shape=[_sds(a.shape, a.dtype) for a in bufs], in_specs=[ANY] * (2 * n), out_specs=[ANY] * n,
        input_output_aliases={n + i: i for i in range(n)},
        scratch_shapes=[pltpu.SemaphoreType.DMA((n, 4)), pltpu.SemaphoreType.DMA((n, 4))], name=name)(*shards, *bufs)
    return list(outs)


def _chip_copy(sum_ref, land_ref, send_sems, recv_sems, i, j, chip, c):
    return pltpu.make_async_remote_copy(
        src_ref=sum_ref.at[2 * chip[0] + chip[1]], dst_ref=land_ref.at[j], send_sem=send_sems.at[3 * i + j],
        recv_sem=recv_sems.at[3 * i + j], device_id=(*chip, c), device_id_type=MESH)


def _chip_exchange_start(sums, name):
    n = len(sums)
    lands = [lax.empty((3,) + s.shape[1:], s.dtype) for s in sums]

    def body(*refs):
        sm, ld = refs[:n], refs[n:2 * n]
        send_sems, recv_sems = refs[2 * n], refs[2 * n + 1]
        token = refs[-1]
        x, y, c = _my_pos()
        for i in range(n):
            for j, chip in enumerate([(1 - x, y), (x, 1 - y), (1 - x, 1 - y)]):
                _chip_copy(sm[i], ld[i], send_sems, recv_sems, i, j, chip, c).start()
        token[...] = jnp.zeros_like(token)

    outs = pl.pallas_call(
        body, name=name,
        out_shape=(pltpu.SemaphoreType.DMA((3 * n,)), pltpu.SemaphoreType.DMA((3 * n,)),
                   *[pltpu.HBM(a.shape, a.dtype) for a in sums], *[pltpu.HBM(a.shape, a.dtype) for a in lands], TOKEN),
        in_specs=[HBM] * (2 * n), out_specs=(SEMS, SEMS, *[HBM] * (2 * n), pl.BlockSpec(memory_space=pltpu.VMEM)),
        input_output_aliases={i: 2 + i for i in range(2 * n)},
        compiler_params=pltpu.CompilerParams(has_side_effects=EFFECT),
    )(*[_hbm(a) for a in sums], *[_hbm(a) for a in lands])
    return outs[0], outs[1], list(outs[2:2 + n]), list(outs[2 + n:2 + 2 * n]), outs[-1]


def _chip_exchange_wait(send_sems, recv_sems, sums, lands, after, name):
    n = len(sums)

    def body(*refs):
        sm, ld = refs[:n], refs[n:2 * n]
        ssem, rsem = refs[2 * n], refs[2 * n + 1]
        x, y, c = _my_pos()
        for i in range(n):
            for j, chip in enumerate([(1 - x, y), (x, 1 - y), (1 - x, 1 - y)]):
                cp = _chip_copy(sm[i], ld[i], ssem, rsem, i, j, chip, c)
                cp.wait_send()
                cp.wait_recv()

    outs = pl.pallas_call(
        body, name=name,
        out_shape=(*[pltpu.HBM(a.shape, a.dtype) for a in sums], *[pltpu.HBM(a.shape, a.dtype) for a in lands]),
        in_specs=[HBM] * (2 * n) + [SEMS, SEMS, ANY], out_specs=tuple([HBM] * (2 * n)),
        input_output_aliases={i: i for i in range(2 * n)},
        compiler_params=pltpu.CompilerParams(has_side_effects=EFFECT),
    )(*sums, *lands, send_sems, recv_sems, after)
    return list(outs[:n]), list(outs[n:])


def _row_tile(r, itemsize):
    align = 32 // itemsize
    best = r
    for t in range(align, min(r, 256) + 1, align):
        if r % t == 0:
            best = t
    return best


def _prefetch_call(body, name, out_shape, grid, in_specs, out_specs, sem):
    return pl.pallas_call(
        body, out_shape=out_shape, name=name,
        grid_spec=pltpu.PrefetchScalarGridSpec(num_scalar_prefetch=1, grid=grid, in_specs=in_specs, out_specs=out_specs),
        compiler_params=pltpu.CompilerParams(dimension_semantics=sem, vmem_limit_bytes=VMEM_LIMIT))


def _pair_sum(pos, part, got, layout, name):
    kind, _ = layout
    _, r, cols = got.shape
    tr, tc = _tiles(r, cols, part.dtype.itemsize)
    nr, nc = r // tr, cols // tc
    if kind == "rows":
        pspec = pl.BlockSpec((tr, tc), lambda q, i, j, p: ((2 * q + p[0]) * nr + i, j))
    elif kind == "cols":
        pspec = pl.BlockSpec((tr, tc), lambda q, i, j, p: (i, (2 * q + p[0]) * nc + j))
    else:
        pspec = pl.BlockSpec((None, tr, tc), lambda q, i, j, p: (2 * q + p[0], i, j))

    def body(pos_ref, p_ref, g_ref, o_ref):
        o_ref[...] = (p_ref[...].astype(F32) + g_ref[...].astype(F32)).astype(o_ref.dtype)

    blk = pl.BlockSpec((None, tr, tc), lambda q, i, j, p: (q, i, j))
    return _prefetch_call(body, name, _sds((4, r, cols), part.dtype), (4, nr, nc), [pspec, blk], blk,
                          ("parallel", "parallel", "parallel"))(pos, part, got)


def _tiles(r, cols, itemsize):
    tr = _row_tile(r, itemsize)
    if tr < r or r * cols * 4 <= (2 << 20) or cols % 256:
        return tr, cols
    return r, 256


def _final_sum_adam(pos, sums, got, w, m, v, name):
    _, r, cols = w.shape
    tr, tc = _tiles(r, cols, sums.dtype.itemsize)

    def body(pos_ref, s_ref, g_ref, w_ref, m_ref, v_ref, go_ref, dl_ref, nm_ref, nv_ref):
        g = ((s_ref[...].astype(F32) + g_ref[0].astype(F32)) + g_ref[1].astype(F32)) + g_ref[2].astype(F32)
        dl, nm, nv = _adam(w_ref[...], g, m_ref[...], v_ref[...])
        go_ref[...] = g
        dl_ref[...] = dl
        nm_ref[...] = nm
        nv_ref[...] = nv

    big = pl.BlockSpec((None, tr, tc), lambda i, j, p: (0, i, j))
    return _prefetch_call(body, name, [_sds((1, r, cols))] * 4, (r // tr, cols // tc),
                          [pl.BlockSpec((None, tr, tc), lambda i, j, p: (p[1], i, j)),
                           pl.BlockSpec((3, tr, tc), lambda i, j, p: (0, i, j)), big, big, big],
                          [big] * 4, ("parallel", "parallel"))(pos, sums, got, w, m, v)


def _small_adam(g_all, w, m, v):
    npk = w.shape[1]

    def body(g_ref, w_ref, m_ref, v_ref, go_ref, dl_ref, nm_ref, nv_ref):
        g = g_ref[0:1, :]
        for k in range(1, NDEV):
            g = g + g_ref[k:k + 1, :]
        dl, nm, nv = _adam(w_ref[...], g, m_ref[...], v_ref[...])
        go_ref[...] = g
        dl_ref[...] = dl
        nm_ref[...] = nm
        nv_ref[...] = nv

    return _call(body, "small_adam", [_sds((1, npk))] * 4)(g_all, w, m, v)


SMALL = [("b_ada", 6 * D), ("norm1_g", D), ("norm2_g", D), ("final_norm_g", D), ("cf_ln_g", CFW), ("cf_ln_b", CFW),
         ("dn_norm_g", DH), ("dn_a_log", H), ("dn_dt_bias", H)]
LATE = ["dn_w_o", "cf_w_o", "w_out", "ffn_w_up", "ffn_w_down"]
LATE_SHAPE = {"dn_w_o": (NDEV, DNW, D // NDEV), "cf_w_o": (NDEV, CFW, D // NDEV), "w_out": (D, D),
              "ffn_w_up": (NDEV, D, 2 * FFN // NDEV), "ffn_w_down": (FFN, D)}
LATE_LAYOUT = {"dn_w_o": ("lead", NDEV), "cf_w_o": ("lead", NDEV), "w_out": ("rows", D // NDEV),
               "ffn_w_up": ("lead", NDEV), "ffn_w_down": ("rows", FFN // NDEV)}
LAYOUT = {"dn_w_o": ("cols", D // NDEV), "cf_w_o": ("cols", D // NDEV), "w_out": ("rows", D // NDEV),
          "ffn_w_up": ("cols", 2 * FFN // NDEV), "ffn_w_down": ("rows", FFN // NDEV),
          "w_in": ("lead", NDEV), "dn_conv_w": ("lead", NDEV), "cf_conv_w": ("lead", NDEV), "ffn_conv_w": ("lead", NDEV)}
NAMES = ["w_ada", "b_ada", "norm1_g", "w_in", "dn_conv_w", "dn_a_log", "dn_dt_bias", "dn_norm_g", "dn_w_o", "cf_conv_w",
         "cf_ln_g", "cf_ln_b", "cf_w_o", "w_out", "norm2_g", "ffn_w_up", "ffn_conv_w", "ffn_w_down", "final_norm_g"]


def _pack_small(d):
    rows = []
    for nm, n in SMALL:
        row = d[nm].reshape(1, n)
        pad = (-n) % LANE
        rows.append(jnp.pad(row, ((0, 0), (0, pad))) if pad else row)
    return jnp.concatenate(rows, axis=1)


def _unpack_small(row, shapes):
    out, off = {}, 0
    for nm, n in SMALL:
        out[nm] = row[0, off:off + n].reshape(shapes[nm])
        off += n + ((-n) % LANE)
    return out


def _cols_from_gathered(g):
    return jnp.transpose(g, (1, 0, 2)).reshape(g.shape[1], NDEV * g.shape[2])


def _cols_to_parts(full):
    r, ctot = full.shape
    return jnp.transpose(full.reshape(r, NDEV, ctot // NDEV), (1, 0, 2))


def kernel(x, c, w_ada, b_ada, norm1_g, w_in, dn_conv_w, dn_a_log, dn_dt_bias, dn_norm_g, dn_w_o, cf_conv_w, cf_ln_g, cf_ln_b, cf_w_o, w_out, norm2_g, ffn_w_up, ffn_conv_w, ffn_w_down, final_norm_g, loss_target, m_w_ada, m_b_ada, m_norm1_g, m_w_in, m_dn_conv_w, m_dn_a_log, m_dn_dt_bias, m_dn_norm_g, m_dn_w_o, m_cf_conv_w, m_cf_ln_g, m_cf_ln_b, m_cf_w_o, m_w_out, m_norm2_g, m_ffn_w_up, m_ffn_conv_w, m_ffn_w_down, m_final_norm_g, v_w_ada, v_b_ada, v_norm1_g, v_w_in, v_dn_conv_w, v_dn_a_log, v_dn_dt_bias, v_dn_norm_g, v_dn_w_o, v_cf_conv_w, v_cf_ln_g, v_cf_ln_b, v_cf_w_o, v_w_out, v_norm2_g, v_ffn_w_up, v_ffn_conv_w, v_ffn_w_down, v_final_norm_g):
    args = locals()
    w = {nm: args[nm] for nm in NAMES}
    mo = {nm: args["m_" + nm] for nm in NAMES}
    vo = {nm: args["v_" + nm] for nm in NAMES}
    shapes = {nm: w[nm].shape for nm in NAMES}
    px, py, pc = _my_pos()
    me = _dev_index(px, py, pc)

    def mat(a):
        return a.reshape(a.shape[-2:])

    pos = jnp.stack([pc, 2 * px + py, me]).astype(jnp.int32)

    first = ["w_in", "dn_conv_w", "cf_conv_w", "ffn_conv_w"]
    tr_in = lambda a: jnp.transpose(a, (0, 2, 1))
    got = _all_gather([tr_in(w["w_in"]).astype(BF16)] + [mat(w[nm]) for nm in first[1:]] + [c], "gather_first")
    full = {nm: _cols_from_gathered(g) for nm, g in zip(first[1:], got[1:-1])}
    c_all = got[-1].reshape(NDEV, D)
    w_in_p = _pad_win(got[0].reshape(NIN, D))

    late_shards = [mat(w[nm]).astype(BF16) for nm in LATE]
    late_lay = [LATE_LAYOUT[nm] for nm in LATE]
    late_bufs = [_place_own(pos, s, lax.empty(LATE_SHAPE[nm], BF16), lay, "place_" + nm)
                 for nm, s, lay in zip(LATE, late_shards, late_lay)]
    l_send, l_recv, l_shards, l_bufs, l_token = _gather_ici_start(late_shards, late_bufs, late_lay, c_all, "gather_late_start")

    res = {}

    class Comm:
        token0 = l_token[0, 0]
        pending = {}

        @staticmethod
        def late_weights(after):
            shards, bufs = _gather_ici_wait(l_send, l_recv, l_shards, l_bufs, late_lay, after, "gather_late_wait")
            return _gather_pair(shards, bufs, late_lay, "gather_late_pair")

        @staticmethod
        def grads(group, gd):
            names = list(gd)
            lays = [LAYOUT[nm] for nm in names]
            gl = []
            for nm in names:
                if nm == "w_in":
                    gl.append(_unpad_win(gd[nm]).reshape(NDEV, NSH, D))
                else:
                    gl.append(_cols_to_parts(gd[nm]) if LAYOUT[nm][0] == "lead" else gd[nm])
            from_sib = _pair_exchange(gl, lays, "rs_pair_" + group)
            sums = [_pair_sum(pos, g, r, lay, "rs_pair_sum_" + nm) for nm, g, r, lay in zip(names, gl, from_sib, lays)]
            started = _chip_exchange_start(sums, "rs_chips_start_" + group)
            Comm.pending[group] = (names,) + tuple(started[:4])
            return started[4][0, 0]

        @staticmethod
        def finish(group, after):
            names, ssem, rsem, sums, lands = Comm.pending[group]
            sums, lands = _chip_exchange_wait(ssem, rsem, sums, lands, after, "rs_chips_wait_" + group)
            for nm, s, r in zip(names, sums, lands):
                if nm == "w_in":
                    outs = _final_sum_adam(pos, s, r, tr_in(w[nm]), tr_in(mo[nm]), tr_in(vo[nm]), "adam_" + nm)
                    res[nm] = [tr_in(o) for o in outs]
                else:
                    res[nm] = _final_sum_adam(pos, s, r, w[nm], mo[nm], vo[nm], "adam_" + nm)
            return res[names[-1]][0]

    ncol = 6 * D // NDEV
    b_sh = lax.dynamic_slice(b_ada.reshape(1, 6 * D), (0, me * ncol), (1, ncol))
    mod_sh = _ada_fwd(c_all, mat(w_ada), b_sh)
    mod_all = _all_gather([mod_sh], "gather_mod")[0]
    mod = lax.dynamic_index_in_dim(mod_all, me, axis=1, keepdims=False).reshape(1, 6 * D)

    vec = lambda a: a.reshape(1, -1)
    loss, grad_x, small = _local_step(
        x.reshape(S, D), loss_target.reshape(S, D), mod, vec(norm1_g), vec(norm2_g), vec(final_norm_g), w_in_p,
        full["dn_conv_w"], vec(dn_a_log), vec(dn_dt_bias), vec(dn_norm_g), full["cf_conv_w"], vec(cf_ln_g),
        vec(cf_ln_b), full["ffn_conv_w"], Comm)

    done_a = Comm.finish("a", grad_x)
    done_b = Comm.finish("b", done_a)

    small["b_ada"] = small.pop("mod")
    g_small = _all_gather([_pack_small(small)], "gather_small")[0].reshape(NDEV, -1)
    outs = _small_adam(g_small, _pack_small({nm: w[nm] for nm, _ in SMALL}), _pack_small({nm: mo[nm] for nm, _ in SMALL}),
                       _pack_small({nm: vo[nm] for nm, _ in SMALL}))
    unpacked = [_unpack_small(o, shapes) for o in outs]
    for nm, _ in SMALL:
        res[nm] = [u[nm] for u in unpacked]

    dmod_sel = lax.dynamic_slice(g_small[:, :6 * D], (0, me * ncol), (NDEV, ncol))
    outs = _ada_bwd_adam(c_all, dmod_sel, mat(w_ada), mat(m_w_ada), mat(v_w_ada))
    res["w_ada"] = [o.reshape(shapes["w_ada"]) for o in outs]
    Comm.finish("c", jnp.concatenate([done_b.reshape(-1)[:LANE], outs[0].reshape(-1)[:LANE]]))

    loss = lax.psum(loss.reshape(()), ("x", "y", "c"))
    out = [loss, grad_x.reshape(x.shape)]
    for k in range(4):
        out += [res[nm][k] for nm in NAMES]
    return tuple(out)
```

```python
import functools

import jax
import jax.numpy as jnp
from jax import lax
from jax.experimental import pallas as pl
from jax.experimental.pallas import tpu as pltpu

F32 = jnp.float32
BF16 = jnp.bfloat16
HI = lax.Precision.HIGHEST
MESH = pl.DeviceIdType.MESH
ANY = pl.BlockSpec(memory_space=pl.ANY)

NDEV = 8
D = 2048
S = 2048
H = 8
DH = 128
DNW = H * DH
CFW = 1024
CFK = 31
DNK = 4
FFN = 5632
FFK = 3
CH = 64
NCH = S // CH
EPS = 1e-6
NIN = 10256
NINP = 10368
O_Z, O_GA, O_GB, O_GLU, O_SM = 3072, 4096, 6144, 8192, 10240
LANE = 128
TS = 256
VMEM_LIMIT = 56 * 1024 * 1024

ADAM_LR, ADAM_B1, ADAM_B2, ADAM_EPS, ADAM_WD, ADAM_STEP = 0.001, 0.9, 0.999, 1e-08, 0.01, 10


def _call(body, name, out_shape, grid=(), in_specs=None, out_specs=None, scratch=(), sem=None, aliases=None):
    kw = {}
    if aliases:
        kw["input_output_aliases"] = aliases
    if in_specs is not None:
        kw["in_specs"] = in_specs
    if out_specs is not None:
        kw["out_specs"] = out_specs
    return pl.pallas_call(
        body, out_shape=out_shape, grid=grid, scratch_shapes=scratch, name=name,
        compiler_params=pltpu.CompilerParams(dimension_semantics=sem, vmem_limit_bytes=VMEM_LIMIT), **kw)


def _sds(shape, dtype=F32):
    return jax.ShapeDtypeStruct(shape, dtype)


def _tile(dim, pref):
    if dim <= pref:
        return dim
    best = None
    for t in range(LANE, pref + 1, LANE):
        if dim % t == 0:
            best = t
    assert best is not None, (dim, pref)
    return best


def _sigmoid(x):
    return 1.0 / (1.0 + jnp.exp(-x))


def _silu(x):
    return x * _sigmoid(x)


def _dsilu(x):
    s = _sigmoid(x)
    return s * (1.0 + x * (1.0 - s))


def _softplus(x):
    return jnp.maximum(x, 0.0) + jnp.log(1.0 + jnp.exp(-jnp.abs(x)))


def _dot(a, b, dims, precision=None):
    return lax.dot_general(a, b, (dims, ((), ())), preferred_element_type=F32, precision=precision)


NN = ((1,), (0,))
NT = ((1,), (1,))
TN = ((0,), (0,))


def _my_pos():
    return lax.axis_index("x"), lax.axis_index("y"), lax.axis_index("c")


def _mm(a, b, mode, out_dtype, name, tm=1024, tn=1024, tk=2048, a2=None, b2=None, dep=None):
    sharded = b.ndim == 3
    if sharded and mode == "nn":
        cs = b.shape[2]
        (m, k), n = a.shape, NDEV * cs
        gs = max(1, tn // cs)
        tm, tn, tk = _tile(m, tm), gs * cs, _tile(k, tk)
    elif sharded:
        assert mode == "nt"
        cs = b.shape[2]
        m, n, k = a.shape[0], b.shape[1], NDEV * cs
        gs = max(1, tk // cs)
        tm, tn, tk = _tile(m, tm), _tile(n, tn), gs * cs
    else:
        if mode == "nn":
            (m, k), (k2, n) = a.shape, b.shape
        elif mode == "nt":
            (m, k), (n, k2) = a.shape, b.shape
        else:
            (k, m), (k2, n) = a.shape, b.shape
        assert k == k2, (a.shape, b.shape, mode)
        n = n * (2 if b2 is not None else 1)
        tm, tn, tk = _tile(m, tm), _tile(n // (2 if b2 is not None else 1), tn), _tile(k, tk)
    nk, nj = k // tk, n // tn
    halfk, halfj = nk // 2, nj // 2
    dims = {"nn": NN, "nt": NT, "tn": TN}[mode]

    def body(*refs):
        a_ref, b_ref = refs[0], refs[1]
        x_ref = refs[2] if (a2 is not None or b2 is not None) else None
        o_ref, acc_ref = refs[-2], refs[-1]
        j, kk = pl.program_id(1), pl.program_id(2)

        def accumulate(part, cols=slice(None)):
            @pl.when(kk == 0)
            def _():
                acc_ref[:, cols] = part

            @pl.when(kk > 0)
            def _():
                acc_ref[:, cols] += part

        if sharded and mode == "nn":
            for q in range(gs):
                accumulate(_dot(a_ref[...], b_ref[q], NN), slice(q * cs, (q + 1) * cs))
        elif sharded:
            def contract(lhs_ref):
                part = None
                for q in range(gs):
                    term = _dot(lhs_ref[:, q * cs:(q + 1) * cs], b_ref[q], NT)
                    part = term if part is None else part + term
                accumulate(part)

            if a2 is None:
                contract(a_ref)
            else:
                pl.when(kk < halfk)(lambda: contract(a_ref))
                pl.when(kk >= halfk)(lambda: contract(x_ref))
        elif b2 is not None:
            pl.when(j < halfj)(lambda: accumulate(_dot(a_ref[...], b_ref[...], dims)))
            pl.when(j >= halfj)(lambda: accumulate(_dot(a_ref[...], x_ref[...], dims)))
        else:
            accumulate(_dot(a_ref[...], b_ref[...], dims))

        @pl.when(kk == nk - 1)
        def _():
            o_ref[...] = acc_ref[...].astype(o_ref.dtype)

    ins, in_specs = [a], []
    if mode == "tn":
        in_specs.append(pl.BlockSpec((tk, tm), lambda i, j, kk: (kk, i)))
    elif a2 is not None:
        in_specs.append(pl.BlockSpec((tm, tk), lambda i, j, kk: (i, jnp.minimum(kk, halfk - 1))))
    else:
        in_specs.append(pl.BlockSpec((tm, tk), lambda i, j, kk: (i, kk)))
    ins.append(b)
    if sharded and mode == "nn":
        in_specs.append(pl.BlockSpec((gs, tk, cs), lambda i, j, kk: (j, kk, 0)))
    elif sharded:
        in_specs.append(pl.BlockSpec((gs, tn, cs), lambda i, j, kk: (kk, j, 0)))
    elif mode == "nt":
        in_specs.append(pl.BlockSpec((tn, tk), lambda i, j, kk: (j, kk)))
    elif b2 is not None:
        in_specs.append(pl.BlockSpec((tk, tn), lambda i, j, kk: (kk, jnp.minimum(j, halfj - 1))))
    else:
        in_specs.append(pl.BlockSpec((tk, tn), lambda i, j, kk: (kk, j)))
    if a2 is not None:
        ins.append(a2)
        in_specs.append(pl.BlockSpec((tm, tk), lambda i, j, kk: (i, jnp.maximum(kk - halfk, 0))))
    if b2 is not None:
        ins.append(b2)
        in_specs.append(pl.BlockSpec((tk, tn), lambda i, j, kk: (kk, jnp.maximum(j - halfj, 0))))
    if dep is not None:
        ins.append(dep)
        in_specs.append(ANY)
    return _call(body, name, _sds((m, n), out_dtype), grid=(m // tm, nj, nk),
                 in_specs=in_specs, out_specs=pl.BlockSpec((tm, tn), lambda i, j, kk: (i, j)),
                 scratch=[pltpu.VMEM((tm, tn), F32)], sem=("parallel", "parallel", "arbitrary"))(*ins)


def _ada_fwd(c_all, w_sh, b_sh):
    n = w_sh.shape[1]
    tn = 512

    def body(c_ref, w_ref, b_ref, o_ref):
        ca = _silu(c_ref[...]).astype(BF16)
        o_ref[...] = _dot(ca, w_ref[...].astype(BF16), NN) + b_ref[...]

    return _call(body, "ada_fwd", _sds((NDEV, n)), grid=(n // tn,),
                 in_specs=[pl.BlockSpec((NDEV, D), lambda j: (0, 0)), pl.BlockSpec((D, tn), lambda j: (0, j)),
                           pl.BlockSpec((1, tn), lambda j: (0, j))],
                 out_specs=pl.BlockSpec((NDEV, tn), lambda j: (0, j)), sem=("parallel",))(c_all, w_sh, b_sh)


def _adam(w, g, m, v):
    m = ADAM_B1 * m + (1.0 - ADAM_B1) * g
    v = ADAM_B2 * v + (1.0 - ADAM_B2) * (g * g)
    m_hat = m / (1.0 - ADAM_B1 ** ADAM_STEP)
    v_hat = v / (1.0 - ADAM_B2 ** ADAM_STEP)
    delta = -ADAM_LR * (m_hat / (jnp.sqrt(v_hat) + ADAM_EPS) + ADAM_WD * w)
    return delta, m, v


def _ada_bwd_adam(c_all, dmod_sel, w, m, v):
    r, n = w.shape
    tr = 256

    def body(c_ref, d_ref, w_ref, m_ref, v_ref, g_ref, dl_ref, nm_ref, nv_ref):
        ca = _silu(c_ref[...])
        g = _dot(ca, d_ref[...], TN, precision=HI)
        dl, nm, nv = _adam(w_ref[...], g, m_ref[...], v_ref[...])
        g_ref[...] = g
        dl_ref[...] = dl
        nm_ref[...] = nm
        nv_ref[...] = nv

    big = pl.BlockSpec((tr, n), lambda i: (i, 0))
    return _call(body, "ada_bwd_adam", [_sds((r, n))] * 4, grid=(r // tr,),
                 in_specs=[pl.BlockSpec((NDEV, tr), lambda i: (0, i)), pl.BlockSpec((NDEV, n), lambda i: (0, 0)),
                           big, big, big],
                 out_specs=[big] * 4, sem=("parallel",))(c_all, dmod_sel, w, m, v)


def _row_spec(width=D):
    return pl.BlockSpec((TS, width), lambda i: (i, 0))


def _vec_spec(width=D):
    return pl.BlockSpec((1, width), lambda i: (0, 0))


def _acc_spec(width=D):
    return pl.BlockSpec((8, width), lambda i: (0, 0))


def _norm_mod(x, g, sc, sh, name):
    def body(x_ref, g_ref, sc_ref, sh_ref, o_ref):
        xv = x_ref[...]
        r = lax.rsqrt(jnp.mean(xv * xv, axis=-1, keepdims=True) + EPS)
        o_ref[...] = ((xv * r) * g_ref[...] * (1.0 + sc_ref[...]) + sh_ref[...]).astype(BF16)

    return _call(body, name, _sds((S, D), BF16), grid=(S // TS,),
                 in_specs=[_row_spec(), _vec_spec(), _vec_spec(), _vec_spec()], out_specs=_row_spec(),
                 sem=("parallel",))(x, g, sc, sh)


def _resid_norm_mod(x, mix, gt, g, sc, sh, name):
    def body(x_ref, mix_ref, gt_ref, g_ref, sc_ref, sh_ref, x2_ref, o_ref):
        xv = x_ref[...] + gt_ref[...] * mix_ref[...]
        x2_ref[...] = xv
        r = lax.rsqrt(jnp.mean(xv * xv, axis=-1, keepdims=True) + EPS)
        o_ref[...] = ((xv * r) * g_ref[...] * (1.0 + sc_ref[...]) + sh_ref[...]).astype(BF16)

    return _call(body, name, [_sds((S, D)), _sds((S, D), BF16)], grid=(S // TS,),
                 in_specs=[_row_spec(), _row_spec()] + [_vec_spec()] * 4, out_specs=[_row_spec(), _row_spec()],
                 sem=("parallel",))(x, mix, gt, g, sc, sh)


def _acc_rows(acc_ref, rows):
    @pl.when(pl.program_id(0) == 0)
    def _():
        acc_ref[...] = jnp.zeros_like(acc_ref)

    for k, row in enumerate(rows):
        acc_ref[k:k + 1, :] += row


def _loss_head(x2, f, tgt, gt2, gf):
    def body(x2_ref, f_ref, t_ref, gt_ref, gf_ref, dx_ref, df_ref, acc_ref):
        fv = f_ref[...]
        x3 = x2_ref[...] + gt_ref[...] * fv
        r = lax.rsqrt(jnp.mean(x3 * x3, axis=-1, keepdims=True) + EPS)
        xn = x3 * r
        e = xn * gf_ref[...] - t_ref[...]
        loss = 0.5 * jnp.sum(jnp.mean(e * e, axis=-1, keepdims=True), axis=0, keepdims=True)
        dy = e * (1.0 / D)
        dxn = dy * gf_ref[...]
        dx3 = r * (dxn - xn * jnp.mean(dxn * xn, axis=-1, keepdims=True))
        dx_ref[...] = dx3
        df_ref[...] = (dx3 * gt_ref[...]).astype(BF16)
        _acc_rows(acc_ref, [jnp.sum(dy * xn, axis=0, keepdims=True), jnp.sum(dx3 * fv, axis=0, keepdims=True),
                            jnp.broadcast_to(loss, (1, D))])

    return _call(body, "loss_head", [_sds((S, D)), _sds((S, D), BF16), _sds((8, D))], grid=(S // TS,),
                 in_specs=[_row_spec(), _row_spec(), _row_spec(), _vec_spec(), _vec_spec()],
                 out_specs=[_row_spec(), _row_spec(), _acc_spec()], sem=("arbitrary",))(x2, f, tgt, gt2, gf)


def _norm_mod_bwd(dhn, x, dres, g, sc, name, mix=None, gt=None):
    gated = mix is not None

    def body(*refs):
        if gated:
            dhn_ref, x_ref, dres_ref, g_ref, sc_ref, mix_ref, gt_ref, dx_ref, dmix_ref, acc_ref = refs
        else:
            dhn_ref, x_ref, dres_ref, g_ref, sc_ref, dx_ref, acc_ref = refs
        xv = x_ref[...]
        dh = dhn_ref[...]
        r = lax.rsqrt(jnp.mean(xv * xv, axis=-1, keepdims=True) + EPS)
        xn = xv * r
        gv = g_ref[...]
        sc1 = 1.0 + sc_ref[...]
        dxn = dh * gv * sc1
        dx = dres_ref[...] + r * (dxn - xn * jnp.mean(dxn * xn, axis=-1, keepdims=True))
        dx_ref[...] = dx
        rows = [jnp.sum(dh, axis=0, keepdims=True), jnp.sum(dh * xn * gv, axis=0, keepdims=True),
                jnp.sum(dh * xn * sc1, axis=0, keepdims=True)]
        if gated:
            rows.append(jnp.sum(dx * mix_ref[...], axis=0, keepdims=True))
            dmix_ref[...] = (dx * gt_ref[...]).astype(BF16)
        _acc_rows(acc_ref, rows)

    ins = [dhn, x, dres, g, sc]
    in_specs = [_row_spec(), _row_spec(), _row_spec(), _vec_spec(), _vec_spec()]
    outs = [_sds((S, D))]
    out_specs = [_row_spec()]
    if gated:
        ins += [mix, gt]
        in_specs += [_row_spec(), _vec_spec()]
        outs.append(_sds((S, D), BF16))
        out_specs.append(_row_spec())
    outs.append(_sds((8, D)))
    out_specs.append(_acc_spec())
    return _call(body, name, outs, grid=(S // TS,), in_specs=in_specs, out_specs=out_specs,
                 sem=("arbitrary",))(*ins)


RC = 256


def _conv_fwd_rows(pad_ref, w_ref, kw, head, r0):
    acc = None
    for k in range(kw):
        term = w_ref[k:k + 1, :] * pad_ref[pl.ds(head - (kw - 1) + k + r0, RC), :]
        acc = term if acc is None else acc + term
    return acc


def _conv_bwd_rows(pad2_ref, w_ref, kw, r0):
    acc = None
    for k in range(kw):
        term = w_ref[k:k + 1, :] * pad2_ref[pl.ds(kw - 1 - k + r0, RC), :]
        acc = term if acc is None else acc + term
    return acc


def _conv_dw(pad_ref, dout_ref, dw_ref, kw, head):
    for k in range(kw):
        acc = None
        for r0 in range(0, S, RC):
            term = jnp.sum(pad_ref[pl.ds(head - (kw - 1) + k + r0, RC), :] * dout_ref[pl.ds(r0, RC), :],
                           axis=0, keepdims=True)
            acc = term if acc is None else acc + term
        dw_ref[k:k + 1, :] = acc


def _col_spec(width, off_blocks=0):
    return pl.BlockSpec((S, width), lambda j: (0, j + off_blocks))


def _dn_pre_fwd(proj, conv_w):
    head = 8

    def body(x_ref, w_ref, o_ref, pad_ref):
        j = pl.program_id(0)
        pad_ref[pl.ds(0, head), :] = jnp.zeros((head, DH), F32)
        pad_ref[pl.ds(head, S), :] = x_ref[...]
        scale = jnp.where(j < H, DH ** -0.5, 1.0)
        for r0 in range(0, S, RC):
            y = _silu(_conv_fwd_rows(pad_ref, w_ref, DNK, head, r0))
            rinv = lax.rsqrt(jnp.sum(y * y, axis=-1, keepdims=True) + EPS)
            o_ref[pl.ds(r0, RC), :] = jnp.where(j < 2 * H, y * rinv * scale, y)

    return _call(body, "dn_pre_fwd", _sds((S, 3 * DNW)), grid=(3 * H,),
                 in_specs=[_col_spec(DH), pl.BlockSpec((DNK, DH), lambda j: (0, j))], out_specs=_col_spec(DH),
                 scratch=[pltpu.VMEM((S + head, DH), F32)], sem=("parallel",))(proj, conv_w)


def _dn_pre_bwd(dq, dk, dv, proj, conv_w, dproj):
    head = 8

    def body(dq_ref, dk_ref, dv_ref, x_ref, w_ref, dproj_in, dx_ref, dw_ref, pad_ref, pad2_ref):
        j = pl.program_id(0)
        pad_ref[pl.ds(0, head), :] = jnp.zeros((head, DH), F32)
        pad_ref[pl.ds(head, S), :] = x_ref[...]
        pad2_ref[pl.ds(S, head), :] = jnp.zeros((head, DH), F32)
        scale = jnp.where(j < H, DH ** -0.5, 1.0)
        for r0 in range(0, S, RC):
            xc = _conv_fwd_rows(pad_ref, w_ref, DNK, head, r0)
            y = _silu(xc)
            rinv = lax.rsqrt(jnp.sum(y * y, axis=-1, keepdims=True) + EPS)
            yn = y * rinv
            rows = pl.ds(r0, RC)
            do = jnp.where(j < H, dq_ref[rows, :], jnp.where(j < 2 * H, dk_ref[rows, :], dv_ref[rows, :]))
            dy_n = scale * rinv * (do - yn * jnp.sum(do * yn, axis=-1, keepdims=True))
            dy = jnp.where(j < 2 * H, dy_n, do)
            pad2_ref[rows, :] = dy * _dsilu(xc)
        for r0 in range(0, S, RC):
            dx_ref[pl.ds(r0, RC), :] = _conv_bwd_rows(pad2_ref, w_ref, DNK, r0).astype(BF16)
        _conv_dw(pad_ref, pad2_ref, dw_ref, DNK, head)

    wspec = pl.BlockSpec((DNK, DH), lambda j: (0, j))
    head_col = lambda lo: pl.BlockSpec((S, DH), lambda j: (0, jnp.clip(j - lo, 0, H - 1)))
    return _call(body, "dn_pre_bwd", [_sds((S, NINP), BF16), _sds((DNK, 3 * DNW))], grid=(3 * H,),
                 in_specs=[head_col(0), head_col(H), head_col(2 * H), _col_spec(DH), wspec, ANY],
                 out_specs=[_col_spec(DH), wspec],
                 scratch=[pltpu.VMEM((S + head, DH), F32), pltpu.VMEM((S + head, DH), F32)],
                 sem=("parallel",), aliases={5: 0})(dq, dk, dv, proj, conv_w, dproj)


CF_HEAD = 32
CF_VAL = pl.BlockSpec((S, LANE), lambda j: (0, O_GLU // LANE + 2 * j))
CF_GL = pl.BlockSpec((S, LANE), lambda j: (0, O_GLU // LANE + 2 * j + 1))


def _cf_conv_fwd(proj, conv_w):
    def body(val_ref, gl_ref, w_ref, o_ref, pad_ref):
        pad_ref[pl.ds(0, CF_HEAD), :] = jnp.zeros((CF_HEAD, LANE), F32)
        pad_ref[pl.ds(CF_HEAD, S), :] = val_ref[...] * _sigmoid(gl_ref[...])
        for r0 in range(0, S, RC):
            o_ref[pl.ds(r0, RC), :] = _conv_fwd_rows(pad_ref, w_ref, CFK, CF_HEAD, r0)

    wspec = pl.BlockSpec((CFK, LANE), lambda j: (0, j))
    return _call(body, "cf_conv_fwd", _sds((S, CFW)), grid=(CFW // LANE,),
                 in_specs=[CF_VAL, CF_GL, wspec], out_specs=_col_spec(LANE),
                 scratch=[pltpu.VMEM((S + CF_HEAD, LANE), F32)], sem=("parallel",))(proj, proj, conv_w)


def _cf_conv_bwd(du1, proj, conv_w, dproj):
    def body(d_ref, val_ref, gl_ref, w_ref, dproj_in, dp_ref, dw_ref, pad_ref, pad2_ref):
        sg = _sigmoid(gl_ref[...])
        pad_ref[pl.ds(0, CF_HEAD), :] = jnp.zeros((CF_HEAD, LANE), F32)
        pad_ref[pl.ds(CF_HEAD, S), :] = val_ref[...] * sg
        pad2_ref[pl.ds(0, S), :] = d_ref[...]
        pad2_ref[pl.ds(S, CF_HEAD), :] = jnp.zeros((CF_HEAD, LANE), F32)
        for r0 in range(0, S, RC):
            du0 = _conv_bwd_rows(pad2_ref, w_ref, CFK, r0)
            rows = pl.ds(r0, RC)
            sgr = _sigmoid(gl_ref[rows, :])
            dp_ref[rows, 0:LANE] = (du0 * sgr).astype(BF16)
            dp_ref[rows, LANE:2 * LANE] = (du0 * val_ref[rows, :] * sgr * (1.0 - sgr)).astype(BF16)
        _conv_dw(pad_ref, pad2_ref, dw_ref, CFK, CF_HEAD)

    wspec = pl.BlockSpec((CFK, LANE), lambda j: (0, j))
    return _call(body, "cf_conv_bwd", [_sds((S, NINP), BF16), _sds((CFK, CFW))], grid=(CFW // LANE,),
                 in_specs=[_col_spec(LANE), CF_VAL, CF_GL, wspec, ANY],
                 out_specs=[pl.BlockSpec((S, 2 * LANE), lambda j: (0, O_GLU // (2 * LANE) + j)), wspec],
                 scratch=[pltpu.VMEM((S + CF_HEAD, LANE), F32), pltpu.VMEM((S + CF_HEAD, LANE), F32)],
                 sem=("parallel",), aliases={4: 0})(du1, proj, proj, conv_w, dproj)


def _cf_ln_fwd(u1, g, b):
    def body(u_ref, g_ref, b_ref, o_ref):
        u = u_ref[...]
        mu = jnp.mean(u, axis=-1, keepdims=True)
        xc = u - mu
        y = xc * lax.rsqrt(jnp.mean(xc * xc, axis=-1, keepdims=True) + EPS)
        o_ref[...] = _silu(y * g_ref[...] + b_ref[...]).astype(BF16)

    return _call(body, "cf_ln_fwd", _sds((S, CFW), BF16), grid=(S // TS,),
                 in_specs=[_row_spec(CFW), _vec_spec(CFW), _vec_spec(CFW)], out_specs=_row_spec(CFW),
                 sem=("parallel",))(u1, g, b)


def _cf_ln_bwd(du3, u1, g, b):
    def body(d_ref, u_ref, g_ref, b_ref, du_ref, acc_ref):
        u = u_ref[...]
        mu = jnp.mean(u, axis=-1, keepdims=True)
        xc = u - mu
        rstd = lax.rsqrt(jnp.mean(xc * xc, axis=-1, keepdims=True) + EPS)
        xh = xc * rstd
        du2 = d_ref[...] * _dsilu(xh * g_ref[...] + b_ref[...])
        dxh = du2 * g_ref[...]
        du_ref[...] = rstd * (dxh - jnp.mean(dxh, axis=-1, keepdims=True)
                              - xh * jnp.mean(dxh * xh, axis=-1, keepdims=True))
        _acc_rows(acc_ref, [jnp.sum(du2 * xh, axis=0, keepdims=True), jnp.sum(du2, axis=0, keepdims=True)])

    return _call(body, "cf_ln_bwd", [_sds((S, CFW)), _sds((8, CFW))], grid=(S // TS,),
                 in_specs=[_row_spec(CFW), _row_spec(CFW), _vec_spec(CFW), _vec_spec(CFW)],
                 out_specs=[_row_spec(CFW), _acc_spec(CFW)], sem=("arbitrary",))(du3, u1, g, b)


FB = 256
FNB = FFN // FB
FF_HEAD = 8


def _ffn_mid_fwd(upall, conv_w):
    def body(gate_ref, up_ref, w_ref, o_ref, pad_ref):
        pad_ref[pl.ds(0, FF_HEAD), :] = jnp.zeros((FF_HEAD, FB), F32)
        pad_ref[pl.ds(FF_HEAD, S), :] = gate_ref[...]
        for r0 in range(0, S, RC):
            gc = _conv_fwd_rows(pad_ref, w_ref, FFK, FF_HEAD, r0)
            o_ref[pl.ds(r0, RC), :] = (_silu(gc) * up_ref[pl.ds(r0, RC), :]).astype(BF16)

    wspec = pl.BlockSpec((FFK, FB), lambda j: (0, j))
    return _call(body, "ffn_mid_fwd", _sds((S, FFN), BF16), grid=(FNB,),
                 in_specs=[_col_spec(FB), _col_spec(FB, FNB), wspec], out_specs=_col_spec(FB),
                 scratch=[pltpu.VMEM((S + FF_HEAD, FB), F32)], sem=("parallel",))(upall, upall, conv_w)


def _ffn_mid_bwd(dh, upall, conv_w):
    def body(d_ref, gate_ref, up_ref, w_ref, dgate_ref, dup_ref, dw_ref, pad_ref, pad2_ref):
        pad_ref[pl.ds(0, FF_HEAD), :] = jnp.zeros((FF_HEAD, FB), F32)
        pad_ref[pl.ds(FF_HEAD, S), :] = gate_ref[...]
        pad2_ref[pl.ds(S, FF_HEAD), :] = jnp.zeros((FF_HEAD, FB), F32)
        for r0 in range(0, S, RC):
            rows = pl.ds(r0, RC)
            gc = _conv_fwd_rows(pad_ref, w_ref, FFK, FF_HEAD, r0)
            dhv = d_ref[rows, :]
            dup_ref[rows, :] = (dhv * _silu(gc)).astype(BF16)
            pad2_ref[rows, :] = dhv * up_ref[rows, :] * _dsilu(gc)
        for r0 in range(0, S, RC):
            dgate_ref[pl.ds(r0, RC), :] = _conv_bwd_rows(pad2_ref, w_ref, FFK, r0).astype(BF16)
        _conv_dw(pad_ref, pad2_ref, dw_ref, FFK, FF_HEAD)

    wspec = pl.BlockSpec((FFK, FB), lambda j: (0, j))
    return _call(body, "ffn_mid_bwd", [_sds((S, FFN), BF16), _sds((S, FFN), BF16), _sds((FFK, FFN))],
                 grid=(FNB,), in_specs=[_col_spec(FB), _col_spec(FB), _col_spec(FB, FNB), wspec],
                 out_specs=[_col_spec(FB), _col_spec(FB), wspec],
                 scratch=[pltpu.VMEM((S + FF_HEAD, FB), F32), pltpu.VMEM((S + FF_HEAD, FB), F32)],
                 sem=("parallel",))(dh, upall, upall, conv_w)


GT = 256
SM_BLK = O_SM // LANE


def _chunk_tri(lower):
    r = lax.broadcasted_iota(jnp.int32, (GT, GT), 0)
    c = lax.broadcasted_iota(jnp.int32, (GT, GT), 1)
    same = (r // CH) == (c // CH)
    tri = (c <= r) if lower else (c >= r)
    return jnp.where(same & tri, 1.0, 0.0).astype(F32)


def _gates_fwd(proj, alog_v, dtb_v):
    def body(sm_ref, al_ref, dt_ref, o_ref):
        lane = lax.broadcasted_iota(jnp.int32, (GT, LANE), 1)
        tri = _chunk_tri(True)
        na = -jnp.exp(al_ref[...])
        for r0 in range(0, S, GT):
            sm = sm_ref[pl.ds(r0, GT), :]
            raw = jnp.where((lane >= H) & (lane < 2 * H), na * _softplus(sm + dt_ref[...]), 0.0)
            gc = _dot(tri, raw, NN, precision=HI)
            o_ref[pl.ds(r0, GT), :] = jnp.where(lane < H, _sigmoid(sm), gc)

    return _call(body, "gates_fwd", _sds((S, LANE)), grid=(1,),
                 in_specs=[pl.BlockSpec((S, LANE), lambda i: (0, SM_BLK)), _vec_spec(LANE), _vec_spec(LANE)],
                 out_specs=pl.BlockSpec((S, LANE), lambda i: (0, 0)), sem=("arbitrary",))(proj, alog_v, dtb_v)


def _gates_bwd(dgb, proj, alog_v, dtb_v, dproj):
    def body(d_ref, sm_ref, al_ref, dt_ref, dproj_in, o_ref, acc_ref):
        lane = lax.broadcasted_iota(jnp.int32, (GT, LANE), 1)
        is_g = (lane >= H) & (lane < 2 * H)
        tri = _chunk_tri(False)
        na = -jnp.exp(al_ref[...])
        d_al = jnp.zeros((1, LANE), F32)
        d_dt = jnp.zeros((1, LANE), F32)
        for r0 in range(0, S, GT):
            sm = sm_ref[pl.ds(r0, GT), :]
            dv = d_ref[pl.ds(r0, GT), :]
            z = sm + dt_ref[...]
            draw = _dot(tri, jnp.where(is_g, dv, 0.0), NN, precision=HI)
            dlogit = jnp.where(is_g, draw * na * _sigmoid(z), 0.0)
            d_al = d_al + jnp.sum(jnp.where(is_g, draw * na * _softplus(z), 0.0), axis=0, keepdims=True)
            d_dt = d_dt + jnp.sum(dlogit, axis=0, keepdims=True)
            bt = _sigmoid(sm)
            o_ref[pl.ds(r0, GT), :] = jnp.where(lane < H, dv * bt * (1.0 - bt), dlogit).astype(BF16)
        acc_ref[...] = jnp.zeros_like(acc_ref)
        acc_ref[0:1, :] = d_al
        acc_ref[1:2, :] = d_dt

    return _call(body, "gates_bwd", [_sds((S, NINP), BF16), _sds((8, LANE))], grid=(1,),
                 in_specs=[pl.BlockSpec((S, LANE), lambda i: (0, 0)), pl.BlockSpec((S, LANE), lambda i: (0, SM_BLK)),
                           _vec_spec(LANE), _vec_spec(LANE), ANY],
                 out_specs=[pl.BlockSpec((S, LANE), lambda i: (0, SM_BLK)), _acc_spec(LANE)],
                 sem=("arbitrary",), aliases={4: 0})(dgb, proj, alog_v, dtb_v, dproj)


def _neumann_inv(a, eye):
    x = -a
    t = eye + x
    p = x
    for _ in range(5):
        p = _dot(p, p, NN, precision=HI)
        t = t + _dot(t, p, NN, precision=HI)
    return t


def _head_specs():
    q = pl.BlockSpec((S, DH), lambda h: (0, h))
    k = pl.BlockSpec((S, DH), lambda h: (0, H + h))
    v = pl.BlockSpec((S, DH), lambda h: (0, 2 * H + h))
    gb = pl.BlockSpec((None, S, DH), lambda h: (h, 0, 0))
    gr = pl.BlockSpec((None, NCH, CH), lambda h: (h, 0, 0))
    return q, k, v, gb, gr


ST_SPEC = pl.BlockSpec((None, NCH, DH, DH), lambda h: (h, 0, 0, 0))
TM_SPEC = pl.BlockSpec((None, NCH, CH, CH), lambda h: (h, 0, 0, 0))


def _delta_fwd(qkvn, gb, gr, bb):
    def body(q_ref, k_ref, v_ref, gb_ref, gr_ref, bb_ref, o_ref, st_ref, tm_ref):
        ri = lax.broadcasted_iota(jnp.int32, (CH, CH), 0)
        ci = lax.broadcasted_iota(jnp.int32, (CH, CH), 1)
        strict = ri > ci
        causal = ri >= ci
        eye = jnp.where(ri == ci, 1.0, 0.0).astype(F32)

        def step(n, st):
            rows = pl.ds(pl.multiple_of(n * CH, CH), CH)
            q, k, v, g, beta = q_ref[rows, :], k_ref[rows, :], v_ref[rows, :], gb_ref[rows, :], bb_ref[rows, :]
            diff = g[:, :CH] - gr_ref[pl.ds(n, 1), :]
            el = jnp.exp(jnp.where(causal, diff, 0.0))
            eg = jnp.exp(g)
            gl = g[CH - 1:CH, :]
            kb = k * beta
            kbf = k.astype(BF16)
            a = jnp.where(strict, _dot(kb.astype(BF16), kbf, NT) * el, 0.0)
            t = _neumann_inv(a, eye)
            tm_ref[n] = t
            st_ref[n] = st
            sb = st.astype(BF16)
            r = v * beta - _dot((kb * eg).astype(BF16), sb, NN)
            ub = _dot(t, r, NN, precision=HI).astype(BF16)
            p = jnp.where(causal, _dot(q.astype(BF16), kbf, NT) * el, 0.0)
            o_ref[rows, :] = _dot((q * eg).astype(BF16), sb, NN) + _dot(p.astype(BF16), ub, NN)
            kd = k * jnp.exp(gl - g)
            return st * jnp.exp(gl) + _dot(kd.astype(BF16), ub, TN)

        lax.fori_loop(0, NCH, step, jnp.zeros((DH, DH), F32))

    q, k, v, gbs, grs = _head_specs()
    return _call(body, "delta_fwd", [_sds((S, DNW)), _sds((H, NCH, DH, DH)), _sds((H, NCH, CH, CH))], grid=(H,),
                 in_specs=[q, k, v, gbs, grs, gbs], out_specs=[pl.BlockSpec((S, DH), lambda h: (0, h)), ST_SPEC, TM_SPEC],
                 sem=("parallel",))(qkvn, qkvn, qkvn, gb, gr, bb)


def _delta_bwd(qkvn, gb, gr, bb, st_all, tm_all, do_all):
    def body(q_ref, k_ref, v_ref, gb_ref, gr_ref, bb_ref, st_ref, tm_ref, do_ref,
             dq_ref, dk_ref, dv_ref, dg_ref, db_ref):
        ri = lax.broadcasted_iota(jnp.int32, (CH, CH), 0)
        ci = lax.broadcasted_iota(jnp.int32, (CH, CH), 1)
        lo_s, lo_c, up_s, up_c = ri > ci, ri >= ci, ri < ci, ri <= ci
        last_row = lax.broadcasted_iota(jnp.int32, (CH, 1), 0) == CH - 1

        def rs(mat):
            return jnp.sum(mat, axis=1, keepdims=True)

        def total(mat):
            return jnp.sum(rs(mat), axis=0, keepdims=True)

        def step(i, ds):
            n = NCH - 1 - i
            rows = pl.ds(pl.multiple_of(n * CH, CH), CH)
            q, k, v, g, beta = q_ref[rows, :], k_ref[rows, :], v_ref[rows, :], gb_ref[rows, :], bb_ref[rows, :]
            do = do_ref[rows, :]
            t = tm_ref[n]
            st = st_ref[n]
            diff = g[:, :CH] - gr_ref[pl.ds(n, 1), :]
            el = jnp.exp(jnp.where(lo_c, diff, 0.0))
            eu = jnp.exp(jnp.where(up_c, -diff, 0.0))
            eg = jnp.exp(g)
            gl = g[CH - 1:CH, :]
            egl = jnp.exp(gl)
            ekd = jnp.exp(gl - g)
            kb = k * beta
            kbg = kb * eg
            qg = q * eg
            kd = k * ekd
            qb, kbf, kbb = q.astype(BF16), k.astype(BF16), kb.astype(BF16)
            kbgb, qgb, kdb = kbg.astype(BF16), qg.astype(BF16), kd.astype(BF16)
            sb, dob, dsb = st.astype(BF16), do.astype(BF16), ds.astype(BF16)
            r = v * beta - _dot(kbgb, sb, NN)
            u = _dot(t, r, NN, precision=HI)
            ub = u.astype(BF16)
            kk, qk = _dot(kbb, kbf, NT), _dot(qb, kbf, NT)
            kkt, qkt = _dot(kbf, kbb, NT), _dot(kbf, qb, NT)
            pt = jnp.where(up_c, qkt * eu, 0.0)
            du = _dot(pt.astype(BF16), dob, NN) + _dot(kdb, dsb, NN)
            dr = _dot(t, du, TN, precision=HI)
            drb = dr.astype(BF16)
            dpg = jnp.where(lo_c, _dot(dob, ub, NT), 0.0) * el
            dpgt = jnp.where(up_c, _dot(ub, dob, NT), 0.0) * eu
            dag = -jnp.where(lo_s, _dot(drb, ub, NT), 0.0) * el
            dagt = -jnp.where(up_s, _dot(ub, drb, NT), 0.0) * eu
            dqg = _dot(dob, sb, NT)
            dkbg = -_dot(drb, sb, NT)
            dkd = _dot(ub, dsb, NT)
            ds_new = _dot(qgb, dob, TN) + egl * ds - _dot(kbgb, drb, TN)
            dkb = _dot(dag.astype(BF16), kbf, NN) + dkbg * eg
            dk = (_dot(dagt.astype(BF16), kbb, NN) + _dot(dpgt.astype(BF16), qb, NN) + dkd * ekd + dkb * beta)
            dq = _dot(dpg.astype(BF16), kbf, NN) + dqg * eg
            dkd_kd = rs(dkd * kd)
            dg = (rs(dag * kk + dpg * qk) - rs(dagt * kkt + dpgt * qkt) + rs(dqg * qg) + rs(dkbg * kbg) - dkd_kd)
            dgl = jnp.sum(dkd_kd, axis=0, keepdims=True) + egl[:, 0:1] * total(ds * st)
            dg = dg + jnp.where(last_row, dgl, 0.0)
            dbeta = rs(dkb * k) + rs(dr * v)
            dq_ref[rows, :] = dq
            dk_ref[rows, :] = dk
            dv_ref[rows, :] = dr * beta
            spread = jnp.full((8, DH), 1.0 / DH, F32)
            dg_ref[pl.ds(n, 1), :] = _dot(spread, jnp.broadcast_to(dg, (CH, DH)), NT, precision=HI)[0:1, :]
            db_ref[pl.ds(n, 1), :] = _dot(spread, jnp.broadcast_to(dbeta, (CH, DH)), NT, precision=HI)[0:1, :]
            return ds_new

        lax.fori_loop(0, NCH, step, jnp.zeros((DH, DH), F32))

    q, k, v, gbs, grs = _head_specs()
    hcol = pl.BlockSpec((S, DH), lambda h: (0, h))
    return _call(body, "delta_bwd",
                 [_sds((S, DNW)), _sds((S, DNW)), _sds((S, DNW)), _sds((H, NCH, CH)), _sds((H, NCH, CH))], grid=(H,),
                 in_specs=[q, k, v, gbs, grs, gbs, ST_SPEC, TM_SPEC, hcol], out_specs=[hcol, hcol, hcol, grs, grs],
                 sem=("parallel",))(qkvn, qkvn, qkvn, gb, gr, bb, st_all, tm_all, do_all)


Z_BLK = O_Z // DNW


def _dn_post_fwd(o, proj, gn):
    def body(o_ref, z_ref, gn_ref, og_ref):
        for h in range(H):
            cols = slice(h * DH, (h + 1) * DH)
            ov = o_ref[:, cols]
            on = ov * lax.rsqrt(jnp.mean(ov * ov, axis=-1, keepdims=True) + EPS) * gn_ref[...]
            og_ref[:, cols] = (on * _silu(z_ref[:, cols])).astype(BF16)

    return _call(body, "dn_post_fwd", _sds((S, DNW), BF16), grid=(S // TS,),
                 in_specs=[_row_spec(DNW), pl.BlockSpec((TS, DNW), lambda i: (i, Z_BLK)), _vec_spec(DH)],
                 out_specs=_row_spec(DNW), sem=("parallel",))(o, proj, gn)


def _dn_post_bwd(dog, o, proj, gn, dproj):
    def body(d_ref, o_ref, z_ref, gn_ref, dproj_in, do_ref, dz_ref, acc_ref):
        dgn = jnp.zeros((1, DH), F32)
        for h in range(H):
            cols = slice(h * DH, (h + 1) * DH)
            ov, zv, dv = o_ref[:, cols], z_ref[:, cols], d_ref[:, cols]
            rinv = lax.rsqrt(jnp.mean(ov * ov, axis=-1, keepdims=True) + EPS)
            xn = ov * rinv
            don = dv * _silu(zv)
            dz_ref[:, cols] = (dv * xn * gn_ref[...] * _dsilu(zv)).astype(BF16)
            dgn = dgn + jnp.sum(don * xn, axis=0, keepdims=True)
            dxn = don * gn_ref[...]
            do_ref[:, cols] = rinv * (dxn - xn * jnp.mean(dxn * xn, axis=-1, keepdims=True))
        _acc_rows(acc_ref, [dgn])

    zspec = pl.BlockSpec((TS, DNW), lambda i: (i, Z_BLK))
    return _call(body, "dn_post_bwd", [_sds((S, DNW)), _sds((S, NINP), BF16), _sds((8, DH))], grid=(S // TS,),
                 in_specs=[_row_spec(DNW), _row_spec(DNW), zspec, _vec_spec(DH), ANY],
                 out_specs=[_row_spec(DNW), zspec, _acc_spec(DH)], sem=("arbitrary",),
                 aliases={4: 1})(dog, o, proj, gn, dproj)


GA_BLK = O_GA // D
GB_BLK = O_GB // D


def _merge_fwd(ba, bb, proj):
    def body(a_ref, b_ref, ga_ref, gb_ref, o_ref):
        o_ref[...] = (_sigmoid(ga_ref[...]) * a_ref[...] + _sigmoid(gb_ref[...]) * b_ref[...]).astype(BF16)

    return _call(body, "merge_fwd", _sds((S, D), BF16), grid=(S // TS,),
                 in_specs=[_row_spec(), _row_spec(), pl.BlockSpec((TS, D), lambda i: (i, GA_BLK)),
                           pl.BlockSpec((TS, D), lambda i: (i, GB_BLK))],
                 out_specs=_row_spec(), sem=("parallel",))(ba, bb, proj, proj)


def _merge_bwd(dm, ba, bb, proj, dproj):
    def body(d_ref, a_ref, b_ref, ga_ref, gb_ref, dproj_in, dg_ref, da_ref, db_ref):
        d = d_ref[...]
        sa, sb = _sigmoid(ga_ref[...]), _sigmoid(gb_ref[...])
        dg_ref[:, 0:D] = (d * a_ref[...] * sa * (1.0 - sa)).astype(BF16)
        dg_ref[:, D:2 * D] = (d * b_ref[...] * sb * (1.0 - sb)).astype(BF16)
        da_ref[...] = (d * sa).astype(BF16)
        db_ref[...] = (d * sb).astype(BF16)

    return _call(body, "merge_bwd", [_sds((S, NINP), BF16), _sds((S, D), BF16), _sds((S, D), BF16)], grid=(S // TS,),
                 in_specs=[_row_spec(), _row_spec(), _row_spec(), pl.BlockSpec((TS, D), lambda i: (i, GA_BLK)),
                           pl.BlockSpec((TS, D), lambda i: (i, GB_BLK)), ANY],
                 out_specs=[pl.BlockSpec((TS, 2 * D), lambda i: (i, O_GA // (2 * D))), _row_spec(), _row_spec()],
                 sem=("parallel",), aliases={5: 0})(dm, ba, bb, proj, proj, dproj)


NSH = NIN // NDEV


def _win_pieces():
    pieces = [(0, 0, 4096), (O_GA, 6160, 2 * D), (O_SM, 4096, 16)]
    for j in range(CFW // LANE):
        pieces.append((O_GLU + 2 * LANE * j, 4112 + LANE * j, LANE))
        pieces.append((O_GLU + 2 * LANE * j + LANE, 4112 + CFW + LANE * j, LANE))
    return pieces


def _pad_win(wt):
    rows = [wt[o:o + wdt] for _, o, wdt in sorted(_win_pieces())]
    rows.append(jnp.zeros((NINP - NIN, wt.shape[1]), wt.dtype))
    return jnp.concatenate(rows, axis=0)


def _unpad_win(gpt):
    return jnp.concatenate([gpt[p:p + wdt] for p, o, wdt in sorted(_win_pieces(), key=lambda t: t[1])], axis=0)


def _lane_vec(v8, offset):
    return jnp.pad(v8, ((0, 0), (offset, LANE - 8 - offset)))


def _tie(vec, token):
    return vec + token


def _local_step(x, tgt, mod, norm1_g, norm2_g, final_g, w_in_p, dn_conv_w, a_log, dt_bias, dn_norm_g,
                cf_conv_w, cf_ln_g, cf_ln_b, ffn_conv_w, comm):
    sh1, sc1, gt1, sh2, sc2, gt2 = (mod[:, i * D:(i + 1) * D] for i in range(6))
    alog_v, dtb_v = _lane_vec(a_log, H), _lane_vec(dt_bias, H)

    hn1 = _norm_mod(x, norm1_g, sc1, _tie(sh1, comm.token0), "norm_mod1")
    proj = _mm(hn1, w_in_p, "nt", F32, "mm_in", tn=1152)
    qkvn = _dn_pre_fwd(proj, dn_conv_w)
    gates = _gates_fwd(proj, alog_v, dtb_v)
    beta_t = gates[:, 0:H].T
    g_t = gates[:, H:2 * H].T
    gb = jnp.broadcast_to(g_t[:, :, None], (H, S, DH))
    bb = jnp.broadcast_to(beta_t[:, :, None], (H, S, DH))
    gr = g_t.reshape(H, NCH, CH)
    o, st_all, tm_all = _delta_fwd(qkvn, gb, gr, bb)
    og = _dn_post_fwd(o, proj, dn_norm_g)
    u1 = _cf_conv_fwd(proj, cf_conv_w)
    u3 = _cf_ln_fwd(u1, cf_ln_g, cf_ln_b)
    after = og[0:8, 0:LANE].astype(F32) + u3[0:8, 0:LANE].astype(F32)
    dn_w_o, cf_w_o, w_out, ffn_w_up, ffn_w_down = comm.late_weights(after)
    br_a = _mm(og, dn_w_o, "nn", F32, "mm_dn_o")
    br_b = _mm(u3, cf_w_o, "nn", F32, "mm_cf_o")
    merged = _merge_fwd(br_a, br_b, proj)
    mix = _mm(merged, w_out, "nn", F32, "mm_out")
    x2, hn2 = _resid_norm_mod(x, mix, gt1, norm2_g, sc2, sh2, "resid_norm_mod2")
    upall = _mm(hn2, ffn_w_up, "nn", F32, "mm_up")
    hmid = _ffn_mid_fwd(upall, ffn_conv_w)
    f = _mm(hmid, ffn_w_down, "nn", F32, "mm_down")

    dx3, df, acc_f = _loss_head(x2, f, tgt, gt2, final_g)
    d_final_g, d_gt2, loss = acc_f[0:1], acc_f[1:2], acc_f[2:3, 0:1]
    dhmid = _mm(df, ffn_w_down, "nt", F32, "mm_down_dx")
    g_w_down = _mm(hmid, df, "tn", BF16, "mm_down_dw")
    d_gate, d_up, g_ffn_conv = _ffn_mid_bwd(dhmid, upall, ffn_conv_w)
    dhn2 = _mm(d_gate, ffn_w_up, "nt", F32, "mm_up_dx", a2=d_up)
    g_w_up = _mm(hn2, d_gate, "tn", BF16, "mm_up_dw", tn=2 * FFN // NDEV, b2=d_up)
    tok_a = comm.grads("a", dict(ffn_w_down=g_w_down, ffn_w_up=g_w_up))
    dx2, dmix, acc2 = _norm_mod_bwd(dhn2, x2, dx3, _tie(norm2_g, tok_a), sc2, "norm_mod2_bwd", mix=mix, gt=gt1)
    d_sh2, d_sc2, d_norm2_g, d_gt1 = acc2[0:1], acc2[1:2], acc2[2:3], acc2[3:4]
    dmerged = _mm(dmix, w_out, "nt", F32, "mm_out_dx")
    g_w_out = _mm(merged, dmix, "tn", BF16, "mm_out_dw")
    d_proj, d_bra, d_brb = _merge_bwd(dmerged, br_a, br_b, proj, lax.empty((S, NINP), BF16))
    du3 = _mm(d_brb, cf_w_o, "nt", F32, "mm_cf_o_dx")
    g_cf_w_o = _mm(u3, d_brb, "tn", BF16, "mm_cf_o_dw")
    du1, acc_ln = _cf_ln_bwd(du3, u1, cf_ln_g, cf_ln_b)
    d_proj, g_cf_conv = _cf_conv_bwd(du1, proj, cf_conv_w, d_proj)
    dog = _mm(d_bra, dn_w_o, "nt", F32, "mm_dn_o_dx")
    g_dn_w_o = _mm(og, d_bra, "tn", BF16, "mm_dn_o_dw")
    tok_b = comm.grads("b", dict(w_out=g_w_out, cf_w_o=g_cf_w_o, dn_w_o=g_dn_w_o, ffn_conv_w=g_ffn_conv,
                                 cf_conv_w=g_cf_conv))
    do, d_proj, acc_gn = _dn_post_bwd(dog, o, proj, _tie(dn_norm_g, tok_b), d_proj)
    dq, dk, dv, dgr, dbr = _delta_bwd(qkvn, gb, gr, bb, st_all, tm_all, do)
    d_proj, g_dn_conv = _dn_pre_bwd(dq, dk, dv, proj, dn_conv_w, d_proj)
    dgates = jnp.concatenate([dbr.reshape(H, S).T, dgr.reshape(H, S).T, jnp.zeros((S, LANE - 2 * H), F32)], axis=1)
    d_proj, acc_g = _gates_bwd(dgates, proj, alog_v, dtb_v, d_proj)
    g_w_in_p = _mm(d_proj, hn1, "tn", BF16, "mm_in_dw", tm=1152)
    tok_c = comm.grads("c", dict(w_in=g_w_in_p, dn_conv_w=g_dn_conv))
    dhn1 = _mm(d_proj, w_in_p, "nn", F32, "mm_in_dx", tk=1152, dep=jnp.broadcast_to(tok_c, (8, LANE)))
    grad_x, acc1 = _norm_mod_bwd(dhn1, x, dx2, norm1_g, sc1, "norm_mod1_bwd")
    d_sh1, d_sc1, d_norm1_g = acc1[0:1], acc1[1:2], acc1[2:3]

    d_mod = jnp.concatenate([d_sh1, d_sc1, d_gt1, d_sh2, d_sc2, d_gt2], axis=1)
    small = dict(mod=d_mod, norm1_g=d_norm1_g, norm2_g=d_norm2_g, final_norm_g=d_final_g,
                 cf_ln_g=acc_ln[0:1], cf_ln_b=acc_ln[1:2], dn_norm_g=acc_gn[0:1],
                 dn_a_log=acc_g[0:1, H:2 * H], dn_dt_bias=acc_g[1:2, H:2 * H])
    return loss, grad_x, small


def _dev_index(px, py, pc):
    return 4 * px + 2 * py + pc


def _all_gather(arrs, name):
    n = len(arrs)

    def body(*refs):
        ins, outs = refs[:n], refs[n:2 * n]
        send_sems, recv_sems, loc_sems = refs[2 * n:]
        x, y, c = _my_pos()
        me, sib = (x, y, c), (x, y, 1 - c)
        chips = [(1 - x, y), (x, 1 - y), (1 - x, 1 - y)]

        def cp(i, k, block, to, src=None):
            dst = outs[i].at[_dev_index(*block)]
            return pltpu.make_async_remote_copy(
                src_ref=dst if src is None else src, dst_ref=dst, send_sem=send_sems.at[i, k],
                recv_sem=recv_sems.at[i, k], device_id=to, device_id_type=MESH)

        mine = [pltpu.make_async_copy(ins[i], outs[i].at[_dev_index(*me)], loc_sems.at[i]) for i in range(n)]
        for m in mine:
            m.start()
        sent = []
        for i in range(n):
            sent.append(cp(i, 0, me, sib, src=ins[i]))
            sent += [cp(i, 1 + j, me, (*chip, c), src=ins[i]) for j, chip in enumerate(chips)]
        for s in sent:
            s.start()
        for i in range(n):
            for j, chip in enumerate(chips):
                cp(i, 1 + j, (*chip, c), me).wait_recv()
                fwd = cp(i, 4 + j, (*chip, c), sib)
                fwd.start()
                sent.append(fwd)
        for i in range(n):
            cp(i, 0, sib, me).wait_recv()
            for j, chip in enumerate(chips):
                cp(i, 4 + j, (*chip, 1 - c), me).wait_recv()
        for s in sent:
            s.wait_send()
        for m in mine:
            m.wait()

    outs = pl.pallas_call(
        body, out_shape=[_sds((NDEV,) + a.shape, a.dtype) for a in arrs], in_specs=[ANY] * n, out_specs=[ANY] * n,
        scratch_shapes=[pltpu.SemaphoreType.DMA((n, 7)), pltpu.SemaphoreType.DMA((n, 7)), pltpu.SemaphoreType.DMA((n,))],
        name=name)(*arrs)
    return list(outs)


def _slab(ref, layout, idx):
    kind, n = layout
    if kind == "rows":
        return ref.at[pl.ds(pl.multiple_of(idx * n, n), n), :]
    if kind == "cols":
        return ref.at[:, pl.ds(pl.multiple_of(idx * n, n), n)]
    return ref.at[idx]


def _slab_shape(arr, layout):
    kind, n = layout
    if kind == "rows":
        return (n, arr.shape[1])
    if kind == "cols":
        return (arr.shape[0], n)
    return tuple(arr.shape[1:])


def _pair_exchange(parts, layouts, name):
    n = len(parts)

    def body(*refs):
        ins, outs = refs[:n], refs[n:2 * n]
        send_sems, recv_sems = refs[2 * n:]
        x, y, c = _my_pos()
        copies = []
        for i in range(n):
            for q in range(4):
                copies.append(pltpu.make_async_remote_copy(
                    src_ref=_slab(ins[i], layouts[i], 2 * q + (1 - c)), dst_ref=outs[i].at[q],
                    send_sem=send_sems.at[i, q], recv_sem=recv_sems.at[i, q], device_id=(x, y, 1 - c),
                    device_id_type=MESH))
        for cpy in copies:
            cpy.start()
        for cpy in copies:
            cpy.wait()

    outs = pl.pallas_call(
        body, out_shape=[_sds((4,) + _slab_shape(p, lay), p.dtype) for p, lay in zip(parts, layouts)],
        in_specs=[ANY] * n, out_specs=[ANY] * n,
        scratch_shapes=[pltpu.SemaphoreType.DMA((n, 4)), pltpu.SemaphoreType.DMA((n, 4))], name=name)(*parts)
    return list(outs)


HBM = pl.BlockSpec(memory_space=pltpu.HBM)
SEMS = pl.BlockSpec(memory_space=pltpu.SEMAPHORE)
EFFECT = pltpu.SideEffectType.DATAFLOW_SIDE_EFFECTING
TOKEN = jax.ShapeDtypeStruct((8, LANE), F32)


def _hbm(a):
    return pltpu.with_memory_space_constraint(a, pltpu.HBM)


def _gather_ici_copy(shard_ref, buf_ref, layout, send_sems, recv_sems, i, j, me, chip, c):
    return pltpu.make_async_remote_copy(
        src_ref=shard_ref, dst_ref=_slab(buf_ref, layout, me), send_sem=send_sems.at[3 * i + j],
        recv_sem=recv_sems.at[3 * i + j], device_id=(*chip, c), device_id_type=MESH)


def _gather_ici_start(shards, bufs, layouts, after, name):
    n = len(shards)

    def body(*refs):
        sh, bf = refs[:n], refs[n:2 * n]
        send_sems, recv_sems = refs[2 * n + 1], refs[2 * n + 2]
        token = refs[-1]
        x, y, c = _my_pos()
        me = _dev_index(x, y, c)
        for i in range(n):
            for j, chip in enumerate([(1 - x, y), (x, 1 - y), (1 - x, 1 - y)]):
                _gather_ici_copy(sh[i], bf[i], layouts[i], send_sems, recv_sems, i, j, me, chip, c).start()
        token[...] = jnp.zeros_like(token)

    outs = pl.pallas_call(
        body, name=name,
        out_shape=(pltpu.SemaphoreType.DMA((3 * n,)), pltpu.SemaphoreType.DMA((3 * n,)),
                   *[pltpu.HBM(a.shape, a.dtype) for a in shards], *[pltpu.HBM(a.shape, a.dtype) for a in bufs], TOKEN),
        in_specs=[HBM] * (2 * n) + [ANY],
        out_specs=(SEMS, SEMS, *[HBM] * (2 * n), pl.BlockSpec(memory_space=pltpu.VMEM)),
        input_output_aliases={i: 2 + i for i in range(2 * n)},
        compiler_params=pltpu.CompilerParams(has_side_effects=EFFECT),
    )(*[_hbm(a) for a in shards], *[_hbm(a) for a in bufs], after)
    return outs[0], outs[1], list(outs[2:2 + n]), list(outs[2 + n:2 + 2 * n]), outs[-1]


def _gather_ici_wait(send_sems, recv_sems, shards, bufs, layouts, after, name):
    n = len(shards)

    def body(*refs):
        sh, bf = refs[:n], refs[n:2 * n]
        ssem, rsem = refs[2 * n], refs[2 * n + 1]
        x, y, c = _my_pos()
        me = _dev_index(x, y, c)
        for i in range(n):
            for j, chip in enumerate([(1 - x, y), (x, 1 - y), (1 - x, 1 - y)]):
                cp = _gather_ici_copy(sh[i], bf[i], layouts[i], ssem, rsem, i, j, me, chip, c)
                cp.wait_send()
                cp.wait_recv()

    outs = pl.pallas_call(
        body, name=name,
        out_shape=(*[pltpu.HBM(a.shape, a.dtype) for a in shards], *[pltpu.HBM(a.shape, a.dtype) for a in bufs]),
        in_specs=[HBM] * (2 * n) + [SEMS, SEMS, ANY], out_specs=tuple([HBM] * (2 * n)),
        input_output_aliases={i: i for i in range(2 * n)},
        compiler_params=pltpu.CompilerParams(has_side_effects=EFFECT),
    )(*shards, *bufs, send_sems, recv_sems, after)
    return list(outs[:n]), list(outs[n:])


def _place_own(pos, shard, buf, layout, name):
    kind, n = layout
    r, cols = shard.shape
    tr = _row_tile(r, shard.dtype.itemsize)
    nr = r // tr
    if kind == "rows":
        ospec = pl.BlockSpec((tr, cols), lambda i, p: (p[2] * nr + i, 0))
    else:
        assert kind == "lead"
        ospec = pl.BlockSpec((None, tr, cols), lambda i, p: (p[2], i, 0))

    def body(pos_ref, s_ref, buf_in, o_ref):
        o_ref[...] = s_ref[...]

    return pl.pallas_call(
        body, out_shape=_sds(buf.shape, buf.dtype), name=name, input_output_aliases={2: 0},
        grid_spec=pltpu.PrefetchScalarGridSpec(
            num_scalar_prefetch=1, grid=(nr,), in_specs=[pl.BlockSpec((tr, cols), lambda i, p: (i, 0)), ANY],
            out_specs=ospec),
        compiler_params=pltpu.CompilerParams(dimension_semantics=("parallel",), vmem_limit_bytes=VMEM_LIMIT),
    )(pos, shard, buf)


def _gather_pair(shards, bufs, layouts, name):
    n = len(shards)

    def body(*refs):
        sh, bo = refs[:n], refs[2 * n:3 * n]
        send_sems, recv_sems = refs[3 * n:]
        x, y, c = _my_pos()
        sib = (x, y, 1 - c)
        copies = []
        for i in range(n):
            for k, (px, py) in enumerate([(x, y), (1 - x, y), (x, 1 - y), (1 - x, 1 - y)]):
                slab = _slab(bo[i], layouts[i], _dev_index(px, py, c))
                copies.append(pltpu.make_async_remote_copy(
                    src_ref=sh[i] if k == 0 else slab, dst_ref=slab, send_sem=send_sems.at[i, k],
                    recv_sem=recv_sems.at[i, k], device_id=sib, device_id_type=MESH))
        for cpy in copies:
            cpy.start()
        for cpy in copies:
            cpy.wait()

    outs = pl.pallas_call(
        body, out_shape=[_sds(a.shape, a.dtype) for a in bufs], in_specs=[ANY] * (2 * n), out_specs=[ANY] * n,
        input_output_aliases={n + i: i for i in range(n)},
        scratch_shapes=[pltpu.SemaphoreType.DMA((n, 4)), pltpu.SemaphoreType.DMA((n, 4))], name=name)(*shards, *bufs)
    return list(outs)


def _chip_copy(sum_ref, land_ref, send_sems, recv_sems, i, j, chip, c):
    return pltpu.make_async_remote_copy(
        src_ref=sum_ref.at[2 * chip[0] + chip[1]], dst_ref=land_ref.at[j], send_sem=send_sems.at[3 * i + j],
        recv_sem=recv_sems.at[3 * i + j], device_id=(*chip, c), device_id_type=MESH)


def _chip_exchange_start(sums, name):
    n = len(sums)
    lands = [lax.empty((3,) + s.shape[1:], s.dtype) for s in sums]

    def body(*refs):
        sm, ld = refs[:n], refs[n:2 * n]
        send_sems, recv_sems = refs[2 * n], refs[2 * n + 1]
        token = refs[-1]
        x, y, c = _my_pos()
        for i in range(n):
            for j, chip in enumerate([(1 - x, y), (x, 1 - y), (1 - x, 1 - y)]):
                _chip_copy(sm[i], ld[i], send_sems, recv_sems, i, j, chip, c).start()
        token[...] = jnp.zeros_like(token)

    outs = pl.pallas_call(
        body, name=name,
        out_shape=(pltpu.SemaphoreType.DMA((3 * n,)), pltpu.SemaphoreType.DMA((3 * n,)),
                   *[pltpu.HBM(a.shape, a.dtype) for a in sums], *[pltpu.HBM(a.shape, a.dtype) for a in lands], TOKEN),
        in_specs=[HBM] * (2 * n), out_specs=(SEMS, SEMS, *[HBM] * (2 * n), pl.BlockSpec(memory_space=pltpu.VMEM)),
        input_output_aliases={i: 2 + i for i in range(2 * n)},
        compiler_params=pltpu.CompilerParams(has_side_effects=EFFECT),
    )(*[_hbm(a) for a in sums], *[_hbm(a) for a in lands])
    return outs[0], outs[1], list(outs[2:2 + n]), list(outs[2 + n:2 + 2 * n]), outs[-1]


def _chip_exchange_wait(send_sems, recv_sems, sums, lands, after, name):
    n = len(sums)

    def body(*refs):
        sm, ld = refs[:n], refs[n:2 * n]
        ssem, rsem = refs[2 * n], refs[2 * n + 1]
        x, y, c = _my_pos()
        for i in range(n):
            for j, chip in enumerate([(1 - x, y), (x, 1 - y), (1 - x, 1 - y)]):
                cp = _chip_copy(sm[i], ld[i], ssem, rsem, i, j, chip, c)
                cp.wait_send()
                cp.wait_recv()

    outs = pl.pallas_call(
        body, name=name,
        out_shape=(*[pltpu.HBM(a.shape, a.dtype) for a in sums], *[pltpu.HBM(a.shape, a.dtype) for a in lands]),
        in_specs=[HBM] * (2 * n) + [SEMS, SEMS, ANY], out_specs=tuple([HBM] * (2 * n)),
        input_output_aliases={i: i for i in range(2 * n)},
        compiler_params=pltpu.CompilerParams(has_side_effects=EFFECT),
    )(*sums, *lands, send_sems, recv_sems, after)
    return list(outs[:n]), list(outs[n:])


def _row_tile(r, itemsize):
    align = 32 // itemsize
    best = r
    for t in range(align, min(r, 256) + 1, align):
        if r % t == 0:
            best = t
    return best


def _prefetch_call(body, name, out_shape, grid, in_specs, out_specs, sem):
    return pl.pallas_call(
        body, out_shape=out_shape, name=name,
        grid_spec=pltpu.PrefetchScalarGridSpec(num_scalar_prefetch=1, grid=grid, in_specs=in_specs, out_specs=out_specs),
        compiler_params=pltpu.CompilerParams(dimension_semantics=sem, vmem_limit_bytes=VMEM_LIMIT))


def _pair_sum(pos, part, got, layout, name):
    kind, _ = layout
    _, r, cols = got.shape
    tr, tc = _tiles(r, cols, part.dtype.itemsize)
    nr, nc = r // tr, cols // tc
    if kind == "rows":
        pspec = pl.BlockSpec((tr, tc), lambda q, i, j, p: ((2 * q + p[0]) * nr + i, j))
    elif kind == "cols":
        pspec = pl.BlockSpec((tr, tc), lambda q, i, j, p: (i, (2 * q + p[0]) * nc + j))
    else:
        pspec = pl.BlockSpec((None, tr, tc), lambda q, i, j, p: (2 * q + p[0], i, j))

    def body(pos_ref, p_ref, g_ref, o_ref):
        o_ref[...] = (p_ref[...].astype(F32) + g_ref[...].astype(F32)).astype(o_ref.dtype)

    blk = pl.BlockSpec((None, tr, tc), lambda q, i, j, p: (q, i, j))
    return _prefetch_call(body, name, _sds((4, r, cols), part.dtype), (4, nr, nc), [pspec, blk], blk,
                          ("parallel", "parallel", "parallel"))(pos, part, got)


def _tiles(r, cols, itemsize):
    tr = _row_tile(r, itemsize)
    if tr < r or r * cols * 4 <= (2 << 20) or cols % 256:
        return tr, cols
    return r, 256


def _final_sum_adam(pos, sums, got, w, m, v, name):
    _, r, cols = w.shape
    tr, tc = _tiles(r, cols, sums.dtype.itemsize)

    def body(pos_ref, s_ref, g_ref, w_ref, m_ref, v_ref, go_ref, dl_ref, nm_ref, nv_ref):
        g = ((s_ref[...].astype(F32) + g_ref[0].astype(F32)) + g_ref[1].astype(F32)) + g_ref[2].astype(F32)
        dl, nm, nv = _adam(w_ref[...], g, m_ref[...], v_ref[...])
        go_ref[...] = g
        dl_ref[...] = dl
        nm_ref[...] = nm
        nv_ref[...] = nv

    big = pl.BlockSpec((None, tr, tc), lambda i, j, p: (0, i, j))
    return _prefetch_call(body, name, [_sds((1, r, cols))] * 4, (r // tr, cols // tc),
                          [pl.BlockSpec((None, tr, tc), lambda i, j, p: (p[1], i, j)),
                           pl.BlockSpec((3, tr, tc), lambda i, j, p: (0, i, j)), big, big, big],
                          [big] * 4, ("parallel", "parallel"))(pos, sums, got, w, m, v)


def _small_adam(g_all, w, m, v):
    npk = w.shape[1]

    def body(g_ref, w_ref, m_ref, v_ref, go_ref, dl_ref, nm_ref, nv_ref):
        g = g_ref[0:1, :]
        for k in range(1, NDEV):
            g = g + g_ref[k:k + 1, :]
        dl, nm, nv = _adam(w_ref[...], g, m_ref[...], v_ref[...])
        go_ref[...] = g
        dl_ref[...] = dl
        nm_ref[...] = nm
        nv_ref[...] = nv

    return _call(body, "small_adam", [_sds((1, npk))] * 4)(g_all, w, m, v)


SMALL = [("b_ada", 6 * D), ("norm1_g", D), ("norm2_g", D), ("final_norm_g", D), ("cf_ln_g", CFW), ("cf_ln_b", CFW),
         ("dn_norm_g", DH), ("dn_a_log", H), ("dn_dt_bias", H)]
LATE = ["dn_w_o", "cf_w_o", "w_out", "ffn_w_up", "ffn_w_down"]
LATE_SHAPE = {"dn_w_o": (NDEV, DNW, D // NDEV), "cf_w_o": (NDEV, CFW, D // NDEV), "w_out": (D, D),
              "ffn_w_up": (NDEV, D, 2 * FFN // NDEV), "ffn_w_down": (FFN, D)}
LATE_LAYOUT = {"dn_w_o": ("lead", NDEV), "cf_w_o": ("lead", NDEV), "w_out": ("rows", D // NDEV),
               "ffn_w_up": ("lead", NDEV), "ffn_w_down": ("rows", FFN // NDEV)}
LAYOUT = {"dn_w_o": ("cols", D // NDEV), "cf_w_o": ("cols", D // NDEV), "w_out": ("rows", D // NDEV),
          "ffn_w_up": ("cols", 2 * FFN // NDEV), "ffn_w_down": ("rows", FFN // NDEV),
          "w_in": ("lead", NDEV), "dn_conv_w": ("lead", NDEV), "cf_conv_w": ("lead", NDEV), "ffn_conv_w": ("lead", NDEV)}
NAMES = ["w_ada", "b_ada", "norm1_g", "w_in", "dn_conv_w", "dn_a_log", "dn_dt_bias", "dn_norm_g", "dn_w_o", "cf_conv_w",
         "cf_ln_g", "cf_ln_b", "cf_w_o", "w_out", "norm2_g", "ffn_w_up", "ffn_conv_w", "ffn_w_down", "final_norm_g"]


def _pack_small(d):
    rows = []
    for nm, n in SMALL:
        row = d[nm].reshape(1, n)
        pad = (-n) % LANE
        rows.append(jnp.pad(row, ((0, 0), (0, pad))) if pad else row)
    return jnp.concatenate(rows, axis=1)


def _unpack_small(row, shapes):
    out, off = {}, 0
    for nm, n in SMALL:
        out[nm] = row[0, off:off + n].reshape(shapes[nm])
        off += n + ((-n) % LANE)
    return out


def _cols_from_gathered(g):
    return jnp.transpose(g, (1, 0, 2)).reshape(g.shape[1], NDEV * g.shape[2])


def _cols_to_parts(full):
    r, ctot = full.shape
    return jnp.transpose(full.reshape(r, NDEV, ctot // NDEV), (1, 0, 2))


def kernel(x, c, w_ada, b_ada, norm1_g, w_in, dn_conv_w, dn_a_log, dn_dt_bias, dn_norm_g, dn_w_o, cf_conv_w, cf_ln_g, cf_ln_b, cf_w_o, w_out, norm2_g, ffn_w_up, ffn_conv_w, ffn_w_down, final_norm_g, loss_target, m_w_ada, m_b_ada, m_norm1_g, m_w_in, m_dn_conv_w, m_dn_a_log, m_dn_dt_bias, m_dn_norm_g, m_dn_w_o, m_cf_conv_w, m_cf_ln_g, m_cf_ln_b, m_cf_w_o, m_w_out, m_norm2_g, m_ffn_w_up, m_ffn_conv_w, m_ffn_w_down, m_final_norm_g, v_w_ada, v_b_ada, v_norm1_g, v_w_in, v_dn_conv_w, v_dn_a_log, v_dn_dt_bias, v_dn_norm_g, v_dn_w_o, v_cf_conv_w, v_cf_ln_g, v_cf_ln_b, v_cf_w_o, v_w_out, v_norm2_g, v_ffn_w_up, v_ffn_conv_w, v_ffn_w_down, v_final_norm_g):
    args = locals()
    w = {nm: args[nm] for nm in NAMES}
    mo = {nm: args["m_" + nm] for nm in NAMES}
    vo = {nm: args["v_" + nm] for nm in NAMES}
    shapes = {nm: w[nm].shape for nm in NAMES}
    px, py, pc = _my_pos()
    me = _dev_index(px, py, pc)

    def mat(a):
        return a.reshape(a.shape[-2:])

    pos = jnp.stack([pc, 2 * px + py, me]).astype(jnp.int32)

    first = ["w_in", "dn_conv_w", "cf_conv_w", "ffn_conv_w"]
    tr_in = lambda a: jnp.transpose(a, (0, 2, 1))
    got = _all_gather([tr_in(w["w_in"]).astype(BF16)] + [mat(w[nm]) for nm in first[1:]] + [c], "gather_first")
    full = {nm: _cols_from_gathered(g) for nm, g in zip(first[1:], got[1:-1])}
    c_all = got[-1].reshape(NDEV, D)
    w_in_p = _pad_win(got[0].reshape(NIN, D))

    ncol = 6 * D // NDEV
    b_sh = lax.dynamic_slice(b_ada.reshape(1, 6 * D), (0, me * ncol), (1, ncol))
    mod_sh = _ada_fwd(c_all, mat(w_ada), b_sh)
    mod_all = _all_gather([mod_sh], "gather_mod")[0]
    mod = lax.dynamic_index_in_dim(mod_all, me, axis=1, keepdims=False).reshape(1, 6 * D)

    late_shards = [mat(w[nm]).astype(BF16) for nm in LATE]
    late_lay = [LATE_LAYOUT[nm] for nm in LATE]
    late_bufs = [_place_own(pos, s, lax.empty(LATE_SHAPE[nm], BF16), lay, "place_" + nm)
                 for nm, s, lay in zip(LATE, late_shards, late_lay)]
    l_send, l_recv, l_shards, l_bufs, l_token = _gather_ici_start(late_shards, late_bufs, late_lay, mod_all, "gather_late_start")

    res = {}

    class Comm:
        token0 = l_token[0, 0]
        pending = {}

        @staticmethod
        def late_weights(after):
            shards, bufs = _gather_ici_wait(l_send, l_recv, l_shards, l_bufs, late_lay, after, "gather_late_wait")
            return _gather_pair(shards, bufs, late_lay, "gather_late_pair")

        @staticmethod
        def grads(group, gd):
            names = list(gd)
            lays = [LAYOUT[nm] for nm in names]
            gl = []
            for nm in names:
                if nm == "w_in":
                    gl.append(_unpad_win(gd[nm]).reshape(NDEV, NSH, D))
                else:
                    gl.append(_cols_to_parts(gd[nm]) if LAYOUT[nm][0] == "lead" else gd[nm])
            from_sib = _pair_exchange(gl, lays, "rs_pair_" + group)
            sums = [_pair_sum(pos, g, r, lay, "rs_pair_sum_" + nm) for nm, g, r, lay in zip(names, gl, from_sib, lays)]
            started = _chip_exchange_start(sums, "rs_chips_start_" + group)
            Comm.pending[group] = (names,) + tuple(started[:4])
            return started[4][0, 0]

        @staticmethod
        def finish(group, after):
            names, ssem, rsem, sums, lands = Comm.pending[group]
            sums, lands = _chip_exchange_wait(ssem, rsem, sums, lands, after, "rs_chips_wait_" + group)
            for nm, s, r in zip(names, sums, lands):
                if nm == "w_in":
                    outs = _final_sum_adam(pos, s, r, tr_in(w[nm]), tr_in(mo[nm]), tr_in(vo[nm]), "adam_" + nm)
                    res[nm] = [tr_in(o) for o in outs]
                else:
                    res[nm] = _final_sum_adam(pos, s, r, w[nm], mo[nm], vo[nm], "adam_" + nm)
            return res[names[-1]][0]

    vec = lambda a: a.reshape(1, -1)
    loss, grad_x, small = _local_step(
        x.reshape(S, D), loss_target.reshape(S, D), mod, vec(norm1_g), vec(norm2_g), vec(final_norm_g), w_in_p,
        full["dn_conv_w"], vec(dn_a_log), vec(dn_dt_bias), vec(dn_norm_g), full["cf_conv_w"], vec(cf_ln_g),
        vec(cf_ln_b), full["ffn_conv_w"], Comm)

    done_a = Comm.finish("a", grad_x)
    done_b = Comm.finish("b", done_a)

    small["b_ada"] = small.pop("mod")
    packed = _pack_small(small) + 0.0 * done_b.reshape(-1)[0]
    g_small = _all_gather([packed], "gather_small")[0].reshape(NDEV, -1)
    outs = _small_adam(g_small, _pack_small({nm: w[nm] for nm, _ in SMALL}), _pack_small({nm: mo[nm] for nm, _ in SMALL}),
                       _pack_small({nm: vo[nm] for nm, _ in SMALL}))
    unpacked = [_unpack_small(o, shapes) for o in outs]
    for nm, _ in SMALL:
        res[nm] = [u[nm] for u in unpacked]

    dmod_sel = lax.dynamic_slice(g_small[:, :6 * D], (0, me * ncol), (NDEV, ncol))
    outs = _ada_bwd_adam(c_all, dmod_sel, mat(w_ada), mat(m_w_ada), mat(v_w_ada))
    res["w_ada"] = [o.reshape(shapes["w_ada"]) for o in outs]
    Comm.finish("c", jnp.concatenate([done_b.reshape(-1)[:LANE], outs[0].reshape(-1)[:LANE]]))

    loss = lax.psum(loss.reshape(()), ("x", "y", "c"))
    out = [loss, grad_x.reshape(x.shape)]
    for k in range(4):
        out += [res[nm][k] for nm in NAMES]
    return tuple(out)
```

```python
import functools

import jax
import jax.numpy as jnp
from jax import lax
from jax.experimental import pallas as pl
from jax.experimental.pallas import tpu as pltpu

F32 = jnp.float32
BF16 = jnp.bfloat16
HI = lax.Precision.HIGHEST
MESH = pl.DeviceIdType.MESH
ANY = pl.BlockSpec(memory_space=pl.ANY)

NDEV = 8
D = 2048
S = 2048
H = 8
DH = 128
DNW = H * DH
CFW = 1024
CFK = 31
DNK = 4
FFN = 5632
FFK = 3
CH = 64
NCH = S // CH
EPS = 1e-6
NIN = 10256
NINP = 10368
O_Z, O_GA, O_GB, O_GLU, O_SM = 3072, 4096, 6144, 8192, 10240
LANE = 128
TS = 256
VMEM_LIMIT = 56 * 1024 * 1024

ADAM_LR, ADAM_B1, ADAM_B2, ADAM_EPS, ADAM_WD, ADAM_STEP = 0.001, 0.9, 0.999, 1e-08, 0.01, 10


def _call(body, name, out_shape, grid=(), in_specs=None, out_specs=None, scratch=(), sem=None, aliases=None):
    kw = {}
    if aliases:
        kw["input_output_aliases"] = aliases
    if in_specs is not None:
        kw["in_specs"] = in_specs
    if out_specs is not None:
        kw["out_specs"] = out_specs
    return pl.pallas_call(
        body, out_shape=out_shape, grid=grid, scratch_shapes=scratch, name=name,
        compiler_params=pltpu.CompilerParams(dimension_semantics=sem, vmem_limit_bytes=VMEM_LIMIT), **kw)


def _sds(shape, dtype=F32):
    return jax.ShapeDtypeStruct(shape, dtype)


def _tile(dim, pref):
    if dim <= pref:
        return dim
    best = None
    for t in range(LANE, pref + 1, LANE):
        if dim % t == 0:
            best = t
    assert best is not None, (dim, pref)
    return best


def _sigmoid(x):
    return 1.0 / (1.0 + jnp.exp(-x))


def _silu(x):
    return x * _sigmoid(x)


def _dsilu(x):
    s = _sigmoid(x)
    return s * (1.0 + x * (1.0 - s))


def _softplus(x):
    return jnp.maximum(x, 0.0) + jnp.log(1.0 + jnp.exp(-jnp.abs(x)))


def _dot(a, b, dims, precision=None):
    return lax.dot_general(a, b, (dims, ((), ())), preferred_element_type=F32, precision=precision)


NN = ((1,), (0,))
NT = ((1,), (1,))
TN = ((0,), (0,))


def _my_pos():
    return lax.axis_index("x"), lax.axis_index("y"), lax.axis_index("c")


def _mm(a, b, mode, out_dtype, name, tm=1024, tn=1024, tk=2048, a2=None, b2=None, dep=None):
    sharded = b.ndim == 3
    if sharded and mode == "nn":
        cs = b.shape[2]
        (m, k), n = a.shape, NDEV * cs
        gs = max(1, tn // cs)
        tm, tn, tk = _tile(m, tm), gs * cs, _tile(k, tk)
    elif sharded:
        assert mode == "nt"
        cs = b.shape[2]
        m, n, k = a.shape[0], b.shape[1], NDEV * cs
        gs = max(1, tk // cs)
        tm, tn, tk = _tile(m, tm), _tile(n, tn), gs * cs
    else:
        if mode == "nn":
            (m, k), (k2, n) = a.shape, b.shape
        elif mode == "nt":
            (m, k), (n, k2) = a.shape, b.shape
        else:
            (k, m), (k2, n) = a.shape, b.shape
        assert k == k2, (a.shape, b.shape, mode)
        n = n * (2 if b2 is not None else 1)
        tm, tn, tk = _tile(m, tm), _tile(n // (2 if b2 is not None else 1), tn), _tile(k, tk)
    nk, nj = k // tk, n // tn
    halfk, halfj = nk // 2, nj // 2
    dims = {"nn": NN, "nt": NT, "tn": TN}[mode]

    def body(*refs):
        a_ref, b_ref = refs[0], refs[1]
        x_ref = refs[2] if (a2 is not None or b2 is not None) else None
        o_ref, acc_ref = refs[-2], refs[-1]
        j, kk = pl.program_id(1), pl.program_id(2)

        def accumulate(part, cols=slice(None)):
            @pl.when(kk == 0)
            def _():
                acc_ref[:, cols] = part

            @pl.when(kk > 0)
            def _():
                acc_ref[:, cols] += part

        if sharded and mode == "nn":
            for q in range(gs):
                accumulate(_dot(a_ref[...], b_ref[q], NN), slice(q * cs, (q + 1) * cs))
        elif sharded:
            def contract(lhs_ref):
                part = None
                for q in range(gs):
                    term = _dot(lhs_ref[:, q * cs:(q + 1) * cs], b_ref[q], NT)
                    part = term if part is None else part + term
                accumulate(part)

            if a2 is None:
                contract(a_ref)
            else:
                pl.when(kk < halfk)(lambda: contract(a_ref))
                pl.when(kk >= halfk)(lambda: contract(x_ref))
        elif b2 is not None:
            pl.when(j < halfj)(lambda: accumulate(_dot(a_ref[...], b_ref[...], dims)))
            pl.when(j >= halfj)(lambda: accumulate(_dot(a_ref[...], x_ref[...], dims)))
        else:
            accumulate(_dot(a_ref[...], b_ref[...], dims))

        @pl.when(kk == nk - 1)
        def _():
            o_ref[...] = acc_ref[...].astype(o_ref.dtype)

    ins, in_specs = [a], []
    if mode == "tn":
        in_specs.append(pl.BlockSpec((tk, tm), lambda i, j, kk: (kk, i)))
    elif a2 is not None:
        in_specs.append(pl.BlockSpec((tm, tk), lambda i, j, kk: (i, jnp.minimum(kk, halfk - 1))))
    else:
        in_specs.append(pl.BlockSpec((tm, tk), lambda i, j, kk: (i, kk)))
    ins.append(b)
    if sharded and mode == "nn":
        in_specs.append(pl.BlockSpec((gs, tk, cs), lambda i, j, kk: (j, kk, 0)))
    elif sharded:
        in_specs.append(pl.BlockSpec((gs, tn, cs), lambda i, j, kk: (kk, j, 0)))
    elif mode == "nt":
        in_specs.append(pl.BlockSpec((tn, tk), lambda i, j, kk: (j, kk)))
    elif b2 is not None:
        in_specs.append(pl.BlockSpec((tk, tn), lambda i, j, kk: (kk, jnp.minimum(j, halfj - 1))))
    else:
        in_specs.append(pl.BlockSpec((tk, tn), lambda i, j, kk: (kk, j)))
    if a2 is not None:
        ins.append(a2)
        in_specs.append(pl.BlockSpec((tm, tk), lambda i, j, kk: (i, jnp.maximum(kk - halfk, 0))))
    if b2 is not None:
        ins.append(b2)
        in_specs.append(pl.BlockSpec((tk, tn), lambda i, j, kk: (kk, jnp.maximum(j - halfj, 0))))
    if dep is not None:
        ins.append(dep)
        in_specs.append(ANY)
    return _call(body, name, _sds((m, n), out_dtype), grid=(m // tm, nj, nk),
                 in_specs=in_specs, out_specs=pl.BlockSpec((tm, tn), lambda i, j, kk: (i, j)),
                 scratch=[pltpu.VMEM((tm, tn), F32)], sem=("parallel", "parallel", "arbitrary"))(*ins)


def _ada_fwd(c_all, w_sh, b_sh):
    n = w_sh.shape[1]
    tn = 512

    def body(c_ref, w_ref, b_ref, o_ref):
        ca = _silu(c_ref[...]).astype(BF16)
        o_ref[...] = _dot(ca, w_ref[...].astype(BF16), NN) + b_ref[...]

    return _call(body, "ada_fwd", _sds((NDEV, n)), grid=(n // tn,),
                 in_specs=[pl.BlockSpec((NDEV, D), lambda j: (0, 0)), pl.BlockSpec((D, tn), lambda j: (0, j)),
                           pl.BlockSpec((1, tn), lambda j: (0, j))],
                 out_specs=pl.BlockSpec((NDEV, tn), lambda j: (0, j)), sem=("parallel",))(c_all, w_sh, b_sh)


def _adam(w, g, m, v):
    m = ADAM_B1 * m + (1.0 - ADAM_B1) * g
    v = ADAM_B2 * v + (1.0 - ADAM_B2) * (g * g)
    m_hat = m / (1.0 - ADAM_B1 ** ADAM_STEP)
    v_hat = v / (1.0 - ADAM_B2 ** ADAM_STEP)
    delta = -ADAM_LR * (m_hat / (jnp.sqrt(v_hat) + ADAM_EPS) + ADAM_WD * w)
    return delta, m, v


def _ada_bwd_adam(c_all, dmod_sel, w, m, v):
    r, n = w.shape
    tr = 256

    def body(c_ref, d_ref, w_ref, m_ref, v_ref, g_ref, dl_ref, nm_ref, nv_ref):
        ca = _silu(c_ref[...])
        g = _dot(ca, d_ref[...], TN, precision=HI)
        dl, nm, nv = _adam(w_ref[...], g, m_ref[...], v_ref[...])
        g_ref[...] = g
        dl_ref[...] = dl
        nm_ref[...] = nm
        nv_ref[...] = nv

    big = pl.BlockSpec((tr, n), lambda i: (i, 0))
    return _call(body, "ada_bwd_adam", [_sds((r, n))] * 4, grid=(r // tr,),
                 in_specs=[pl.BlockSpec((NDEV, tr), lambda i: (0, i)), pl.BlockSpec((NDEV, n), lambda i: (0, 0)),
                           big, big, big],
                 out_specs=[big] * 4, sem=("parallel",))(c_all, dmod_sel, w, m, v)


def _row_spec(width=D):
    return pl.BlockSpec((TS, width), lambda i: (i, 0))


def _vec_spec(width=D):
    return pl.BlockSpec((1, width), lambda i: (0, 0))


def _acc_spec(width=D):
    return pl.BlockSpec((8, width), lambda i: (0, 0))


def _norm_mod(x, g, sc, sh, name):
    def body(x_ref, g_ref, sc_ref, sh_ref, o_ref):
        xv = x_ref[...]
        r = lax.rsqrt(jnp.mean(xv * xv, axis=-1, keepdims=True) + EPS)
        o_ref[...] = ((xv * r) * g_ref[...] * (1.0 + sc_ref[...]) + sh_ref[...]).astype(BF16)

    return _call(body, name, _sds((S, D), BF16), grid=(S // TS,),
                 in_specs=[_row_spec(), _vec_spec(), _vec_spec(), _vec_spec()], out_specs=_row_spec(),
                 sem=("parallel",))(x, g, sc, sh)


def _resid_norm_mod(x, mix, gt, g, sc, sh, name):
    def body(x_ref, mix_ref, gt_ref, g_ref, sc_ref, sh_ref, x2_ref, o_ref):
        xv = x_ref[...] + gt_ref[...] * mix_ref[...]
        x2_ref[...] = xv
        r = lax.rsqrt(jnp.mean(xv * xv, axis=-1, keepdims=True) + EPS)
        o_ref[...] = ((xv * r) * g_ref[...] * (1.0 + sc_ref[...]) + sh_ref[...]).astype(BF16)

    return _call(body, name, [_sds((S, D)), _sds((S, D), BF16)], grid=(S // TS,),
                 in_specs=[_row_spec(), _row_spec()] + [_vec_spec()] * 4, out_specs=[_row_spec(), _row_spec()],
                 sem=("parallel",))(x, mix, gt, g, sc, sh)


def _acc_rows(acc_ref, rows):
    @pl.when(pl.program_id(0) == 0)
    def _():
        acc_ref[...] = jnp.zeros_like(acc_ref)

    for k, row in enumerate(rows):
        acc_ref[k:k + 1, :] += row


def _loss_head(x2, f, tgt, gt2, gf):
    def body(x2_ref, f_ref, t_ref, gt_ref, gf_ref, dx_ref, df_ref, acc_ref):
        fv = f_ref[...]
        x3 = x2_ref[...] + gt_ref[...] * fv
        r = lax.rsqrt(jnp.mean(x3 * x3, axis=-1, keepdims=True) + EPS)
        xn = x3 * r
        e = xn * gf_ref[...] - t_ref[...]
        loss = 0.5 * jnp.sum(jnp.mean(e * e, axis=-1, keepdims=True), axis=0, keepdims=True)
        dy = e * (1.0 / D)
        dxn = dy * gf_ref[...]
        dx3 = r * (dxn - xn * jnp.mean(dxn * xn, axis=-1, keepdims=True))
        dx_ref[...] = dx3
        df_ref[...] = (dx3 * gt_ref[...]).astype(BF16)
        _acc_rows(acc_ref, [jnp.sum(dy * xn, axis=0, keepdims=True), jnp.sum(dx3 * fv, axis=0, keepdims=True),
                            jnp.broadcast_to(loss, (1, D))])

    return _call(body, "loss_head", [_sds((S, D)), _sds((S, D), BF16), _sds((8, D))], grid=(S // TS,),
                 in_specs=[_row_spec(), _row_spec(), _row_spec(), _vec_spec(), _vec_spec()],
                 out_specs=[_row_spec(), _row_spec(), _acc_spec()], sem=("arbitrary",))(x2, f, tgt, gt2, gf)


def _norm_mod_bwd(dhn, x, dres, g, sc, name, mix=None, gt=None):
    gated = mix is not None

    def body(*refs):
        if gated:
            dhn_ref, x_ref, dres_ref, g_ref, sc_ref, mix_ref, gt_ref, dx_ref, dmix_ref, acc_ref = refs
        else:
            dhn_ref, x_ref, dres_ref, g_ref, sc_ref, dx_ref, acc_ref = refs
        xv = x_ref[...]
        dh = dhn_ref[...]
        r = lax.rsqrt(jnp.mean(xv * xv, axis=-1, keepdims=True) + EPS)
        xn = xv * r
        gv = g_ref[...]
        sc1 = 1.0 + sc_ref[...]
        dxn = dh * gv * sc1
        dx = dres_ref[...] + r * (dxn - xn * jnp.mean(dxn * xn, axis=-1, keepdims=True))
        dx_ref[...] = dx
        rows = [jnp.sum(dh, axis=0, keepdims=True), jnp.sum(dh * xn * gv, axis=0, keepdims=True),
                jnp.sum(dh * xn * sc1, axis=0, keepdims=True)]
        if gated:
            rows.append(jnp.sum(dx * mix_ref[...], axis=0, keepdims=True))
            dmix_ref[...] = (dx * gt_ref[...]).astype(BF16)
        _acc_rows(acc_ref, rows)

    ins = [dhn, x, dres, g, sc]
    in_specs = [_row_spec(), _row_spec(), _row_spec(), _vec_spec(), _vec_spec()]
    outs = [_sds((S, D))]
    out_specs = [_row_spec()]
    if gated:
        ins += [mix, gt]
        in_specs += [_row_spec(), _vec_spec()]
        outs.append(_sds((S, D), BF16))
        out_specs.append(_row_spec())
    outs.append(_sds((8, D)))
    out_specs.append(_acc_spec())
    return _call(body, name, outs, grid=(S // TS,), in_specs=in_specs, out_specs=out_specs,
                 sem=("arbitrary",))(*ins)


RC = 256


def _conv_fwd_rows(pad_ref, w_ref, kw, head, r0):
    acc = None
    for k in range(kw):
        term = w_ref[k:k + 1, :] * pad_ref[pl.ds(head - (kw - 1) + k + r0, RC), :]
        acc = term if acc is None else acc + term
    return acc


def _conv_bwd_rows(pad2_ref, w_ref, kw, r0):
    acc = None
    for k in range(kw):
        term = w_ref[k:k + 1, :] * pad2_ref[pl.ds(kw - 1 - k + r0, RC), :]
        acc = term if acc is None else acc + term
    return acc


def _conv_dw(pad_ref, dout_ref, dw_ref, kw, head):
    for k in range(kw):
        acc = None
        for r0 in range(0, S, RC):
            term = jnp.sum(pad_ref[pl.ds(head - (kw - 1) + k + r0, RC), :] * dout_ref[pl.ds(r0, RC), :],
                           axis=0, keepdims=True)
            acc = term if acc is None else acc + term
        dw_ref[k:k + 1, :] = acc


def _col_spec(width, off_blocks=0):
    return pl.BlockSpec((S, width), lambda j: (0, j + off_blocks))


def _dn_pre_fwd(proj, conv_w):
    head = 8

    def body(x_ref, w_ref, o_ref, pad_ref):
        j = pl.program_id(0)
        pad_ref[pl.ds(0, head), :] = jnp.zeros((head, DH), F32)
        pad_ref[pl.ds(head, S), :] = x_ref[...]
        scale = jnp.where(j < H, DH ** -0.5, 1.0)
        for r0 in range(0, S, RC):
            y = _silu(_conv_fwd_rows(pad_ref, w_ref, DNK, head, r0))
            rinv = lax.rsqrt(jnp.sum(y * y, axis=-1, keepdims=True) + EPS)
            o_ref[pl.ds(r0, RC), :] = jnp.where(j < 2 * H, y * rinv * scale, y)

    return _call(body, "dn_pre_fwd", _sds((S, 3 * DNW)), grid=(3 * H,),
                 in_specs=[_col_spec(DH), pl.BlockSpec((DNK, DH), lambda j: (0, j))], out_specs=_col_spec(DH),
                 scratch=[pltpu.VMEM((S + head, DH), F32)], sem=("parallel",))(proj, conv_w)


def _dn_pre_bwd(dq, dk, dv, proj, conv_w, dproj):
    head = 8

    def body(dq_ref, dk_ref, dv_ref, x_ref, w_ref, dproj_in, dx_ref, dw_ref, pad_ref, pad2_ref):
        j = pl.program_id(0)
        pad_ref[pl.ds(0, head), :] = jnp.zeros((head, DH), F32)
        pad_ref[pl.ds(head, S), :] = x_ref[...]
        pad2_ref[pl.ds(S, head), :] = jnp.zeros((head, DH), F32)
        scale = jnp.where(j < H, DH ** -0.5, 1.0)
        for r0 in range(0, S, RC):
            xc = _conv_fwd_rows(pad_ref, w_ref, DNK, head, r0)
            y = _silu(xc)
            rinv = lax.rsqrt(jnp.sum(y * y, axis=-1, keepdims=True) + EPS)
            yn = y * rinv
            rows = pl.ds(r0, RC)
            do = jnp.where(j < H, dq_ref[rows, :], jnp.where(j < 2 * H, dk_ref[rows, :], dv_ref[rows, :]))
            dy_n = scale * rinv * (do - yn * jnp.sum(do * yn, axis=-1, keepdims=True))
            dy = jnp.where(j < 2 * H, dy_n, do)
            pad2_ref[rows, :] = dy * _dsilu(xc)
        for r0 in range(0, S, RC):
            dx_ref[pl.ds(r0, RC), :] = _conv_bwd_rows(pad2_ref, w_ref, DNK, r0).astype(BF16)
        _conv_dw(pad_ref, pad2_ref, dw_ref, DNK, head)

    wspec = pl.BlockSpec((DNK, DH), lambda j: (0, j))
    head_col = lambda lo: pl.BlockSpec((S, DH), lambda j: (0, jnp.clip(j - lo, 0, H - 1)))
    return _call(body, "dn_pre_bwd", [_sds((S, NINP), BF16), _sds((DNK, 3 * DNW))], grid=(3 * H,),
                 in_specs=[head_col(0), head_col(H), head_col(2 * H), _col_spec(DH), wspec, ANY],
                 out_specs=[_col_spec(DH), wspec],
                 scratch=[pltpu.VMEM((S + head, DH), F32), pltpu.VMEM((S + head, DH), F32)],
                 sem=("parallel",), aliases={5: 0})(dq, dk, dv, proj, conv_w, dproj)


CF_HEAD = 32
CF_VAL = pl.BlockSpec((S, LANE), lambda j: (0, O_GLU // LANE + 2 * j))
CF_GL = pl.BlockSpec((S, LANE), lambda j: (0, O_GLU // LANE + 2 * j + 1))


def _cf_conv_fwd(proj, conv_w):
    def body(val_ref, gl_ref, w_ref, o_ref, pad_ref):
        pad_ref[pl.ds(0, CF_HEAD), :] = jnp.zeros((CF_HEAD, LANE), F32)
        pad_ref[pl.ds(CF_HEAD, S), :] = val_ref[...] * _sigmoid(gl_ref[...])
        for r0 in range(0, S, RC):
            o_ref[pl.ds(r0, RC), :] = _conv_fwd_rows(pad_ref, w_ref, CFK, CF_HEAD, r0)

    wspec = pl.BlockSpec((CFK, LANE), lambda j: (0, j))
    return _call(body, "cf_conv_fwd", _sds((S, CFW)), grid=(CFW // LANE,),
                 in_specs=[CF_VAL, CF_GL, wspec], out_specs=_col_spec(LANE),
                 scratch=[pltpu.VMEM((S + CF_HEAD, LANE), F32)], sem=("parallel",))(proj, proj, conv_w)


def _cf_conv_bwd(du1, proj, conv_w, dproj):
    def body(d_ref, val_ref, gl_ref, w_ref, dproj_in, dp_ref, dw_ref, pad_ref, pad2_ref):
        sg = _sigmoid(gl_ref[...])
        pad_ref[pl.ds(0, CF_HEAD), :] = jnp.zeros((CF_HEAD, LANE), F32)
        pad_ref[pl.ds(CF_HEAD, S), :] = val_ref[...] * sg
        pad2_ref[pl.ds(0, S), :] = d_ref[...]
        pad2_ref[pl.ds(S, CF_HEAD), :] = jnp.zeros((CF_HEAD, LANE), F32)
        for r0 in range(0, S, RC):
            du0 = _conv_bwd_rows(pad2_ref, w_ref, CFK, r0)
            rows = pl.ds(r0, RC)
            sgr = _sigmoid(gl_ref[rows, :])
            dp_ref[rows, 0:LANE] = (du0 * sgr).astype(BF16)
            dp_ref[rows, LANE:2 * LANE] = (du0 * val_ref[rows, :] * sgr * (1.0 - sgr)).astype(BF16)
        _conv_dw(pad_ref, pad2_ref, dw_ref, CFK, CF_HEAD)

    wspec = pl.BlockSpec((CFK, LANE), lambda j: (0, j))
    return _call(body, "cf_conv_bwd", [_sds((S, NINP), BF16), _sds((CFK, CFW))], grid=(CFW // LANE,),
                 in_specs=[_col_spec(LANE), CF_VAL, CF_GL, wspec, ANY],
                 out_specs=[pl.BlockSpec((S, 2 * LANE), lambda j: (0, O_GLU // (2 * LANE) + j)), wspec],
                 scratch=[pltpu.VMEM((S + CF_HEAD, LANE), F32), pltpu.VMEM((S + CF_HEAD, LANE), F32)],
                 sem=("parallel",), aliases={4: 0})(du1, proj, proj, conv_w, dproj)


def _cf_ln_fwd(u1, g, b):
    def body(u_ref, g_ref, b_ref, o_ref):
        u = u_ref[...]
        mu = jnp.mean(u, axis=-1, keepdims=True)
        xc = u - mu
        y = xc * lax.rsqrt(jnp.mean(xc * xc, axis=-1, keepdims=True) + EPS)
        o_ref[...] = _silu(y * g_ref[...] + b_ref[...]).astype(BF16)

    return _call(body, "cf_ln_fwd", _sds((S, CFW), BF16), grid=(S // TS,),
                 in_specs=[_row_spec(CFW), _vec_spec(CFW), _vec_spec(CFW)], out_specs=_row_spec(CFW),
                 sem=("parallel",))(u1, g, b)


def _cf_ln_bwd(du3, u1, g, b):
    def body(d_ref, u_ref, g_ref, b_ref, du_ref, acc_ref):
        u = u_ref[...]
        mu = jnp.mean(u, axis=-1, keepdims=True)
        xc = u - mu
        rstd = lax.rsqrt(jnp.mean(xc * xc, axis=-1, keepdims=True) + EPS)
        xh = xc * rstd
        du2 = d_ref[...] * _dsilu(xh * g_ref[...] + b_ref[...])
        dxh = du2 * g_ref[...]
        du_ref[...] = rstd * (dxh - jnp.mean(dxh, axis=-1, keepdims=True)
                              - xh * jnp.mean(dxh * xh, axis=-1, keepdims=True))
        _acc_rows(acc_ref, [jnp.sum(du2 * xh, axis=0, keepdims=True), jnp.sum(du2, axis=0, keepdims=True)])

    return _call(body, "cf_ln_bwd", [_sds((S, CFW)), _sds((8, CFW))], grid=(S // TS,),
                 in_specs=[_row_spec(CFW), _row_spec(CFW), _vec_spec(CFW), _vec_spec(CFW)],
                 out_specs=[_row_spec(CFW), _acc_spec(CFW)], sem=("arbitrary",))(du3, u1, g, b)


FB = 256
FNB = FFN // FB
FF_HEAD = 8


def _ffn_mid_fwd(upall, conv_w):
    def body(gate_ref, up_ref, w_ref, o_ref, pad_ref):
        pad_ref[pl.ds(0, FF_HEAD), :] = jnp.zeros((FF_HEAD, FB), F32)
        pad_ref[pl.ds(FF_HEAD, S), :] = gate_ref[...]
        for r0 in range(0, S, RC):
            gc = _conv_fwd_rows(pad_ref, w_ref, FFK, FF_HEAD, r0)
            o_ref[pl.ds(r0, RC), :] = (_silu(gc) * up_ref[pl.ds(r0, RC), :]).astype(BF16)

    wspec = pl.BlockSpec((FFK, FB), lambda j: (0, j))
    return _call(body, "ffn_mid_fwd", _sds((S, FFN), BF16), grid=(FNB,),
                 in_specs=[_col_spec(FB), _col_spec(FB, FNB), wspec], out_specs=_col_spec(FB),
                 scratch=[pltpu.VMEM((S + FF_HEAD, FB), F32)], sem=("parallel",))(upall, upall, conv_w)


def _ffn_mid_bwd(dh, upall, conv_w):
    def body(d_ref, gate_ref, up_ref, w_ref, dgate_ref, dup_ref, dw_ref, pad_ref, pad2_ref):
        pad_ref[pl.ds(0, FF_HEAD), :] = jnp.zeros((FF_HEAD, FB), F32)
        pad_ref[pl.ds(FF_HEAD, S), :] = gate_ref[...]
        pad2_ref[pl.ds(S, FF_HEAD), :] = jnp.zeros((FF_HEAD, FB), F32)
        for r0 in range(0, S, RC):
            rows = pl.ds(r0, RC)
            gc = _conv_fwd_rows(pad_ref, w_ref, FFK, FF_HEAD, r0)
            dhv = d_ref[rows, :]
            dup_ref[rows, :] = (dhv * _silu(gc)).astype(BF16)
            pad2_ref[rows, :] = dhv * up_ref[rows, :] * _dsilu(gc)
        for r0 in range(0, S, RC):
            dgate_ref[pl.ds(r0, RC), :] = _conv_bwd_rows(pad2_ref, w_ref, FFK, r0).astype(BF16)
        _conv_dw(pad_ref, pad2_ref, dw_ref, FFK, FF_HEAD)

    wspec = pl.BlockSpec((FFK, FB), lambda j: (0, j))
    return _call(body, "ffn_mid_bwd", [_sds((S, FFN), BF16), _sds((S, FFN), BF16), _sds((FFK, FFN))],
                 grid=(FNB,), in_specs=[_col_spec(FB), _col_spec(FB), _col_spec(FB, FNB), wspec],
                 out_specs=[_col_spec(FB), _col_spec(FB), wspec],
                 scratch=[pltpu.VMEM((S + FF_HEAD, FB), F32), pltpu.VMEM((S + FF_HEAD, FB), F32)],
                 sem=("parallel",))(dh, upall, upall, conv_w)


GT = 256
SM_BLK = O_SM // LANE


def _chunk_tri(lower):
    r = lax.broadcasted_iota(jnp.int32, (GT, GT), 0)
    c = lax.broadcasted_iota(jnp.int32, (GT, GT), 1)
    same = (r // CH) == (c // CH)
    tri = (c <= r) if lower else (c >= r)
    return jnp.where(same & tri, 1.0, 0.0).astype(F32)


def _gates_fwd(proj, alog_v, dtb_v):
    def body(sm_ref, al_ref, dt_ref, o_ref):
        lane = lax.broadcasted_iota(jnp.int32, (GT, LANE), 1)
        tri = _chunk_tri(True)
        na = -jnp.exp(al_ref[...])
        for r0 in range(0, S, GT):
            sm = sm_ref[pl.ds(r0, GT), :]
            raw = jnp.where((lane >= H) & (lane < 2 * H), na * _softplus(sm + dt_ref[...]), 0.0)
            gc = _dot(tri, raw, NN, precision=HI)
            o_ref[pl.ds(r0, GT), :] = jnp.where(lane < H, _sigmoid(sm), gc)

    return _call(body, "gates_fwd", _sds((S, LANE)), grid=(1,),
                 in_specs=[pl.BlockSpec((S, LANE), lambda i: (0, SM_BLK)), _vec_spec(LANE), _vec_spec(LANE)],
                 out_specs=pl.BlockSpec((S, LANE), lambda i: (0, 0)), sem=("arbitrary",))(proj, alog_v, dtb_v)


def _gates_bwd(dgb, proj, alog_v, dtb_v, dproj):
    def body(d_ref, sm_ref, al_ref, dt_ref, dproj_in, o_ref, acc_ref):
        lane = lax.broadcasted_iota(jnp.int32, (GT, LANE), 1)
        is_g = (lane >= H) & (lane < 2 * H)
        tri = _chunk_tri(False)
        na = -jnp.exp(al_ref[...])
        d_al = jnp.zeros((1, LANE), F32)
        d_dt = jnp.zeros((1, LANE), F32)
        for r0 in range(0, S, GT):
            sm = sm_ref[pl.ds(r0, GT), :]
            dv = d_ref[pl.ds(r0, GT), :]
            z = sm + dt_ref[...]
            draw = _dot(tri, jnp.where(is_g, dv, 0.0), NN, precision=HI)
            dlogit = jnp.where(is_g, draw * na * _sigmoid(z), 0.0)
            d_al = d_al + jnp.sum(jnp.where(is_g, draw * na * _softplus(z), 0.0), axis=0, keepdims=True)
            d_dt = d_dt + jnp.sum(dlogit, axis=0, keepdims=True)
            bt = _sigmoid(sm)
            o_ref[pl.ds(r0, GT), :] = jnp.where(lane < H, dv * bt * (1.0 - bt), dlogit).astype(BF16)
        acc_ref[...] = jnp.zeros_like(acc_ref)
        acc_ref[0:1, :] = d_al
        acc_ref[1:2, :] = d_dt

    return _call(body, "gates_bwd", [_sds((S, NINP), BF16), _sds((8, LANE))], grid=(1,),
                 in_specs=[pl.BlockSpec((S, LANE), lambda i: (0, 0)), pl.BlockSpec((S, LANE), lambda i: (0, SM_BLK)),
                           _vec_spec(LANE), _vec_spec(LANE), ANY],
                 out_specs=[pl.BlockSpec((S, LANE), lambda i: (0, SM_BLK)), _acc_spec(LANE)],
                 sem=("arbitrary",), aliases={4: 0})(dgb, proj, alog_v, dtb_v, dproj)


HB = 2


def _each(fn, *lists):
    return [fn(*args) for args in zip(*lists)]


def _neumann_inv(a, eye):
    p = _each(lambda m: -m, a)
    t = _each(lambda m: eye + m, p)
    for _ in range(5):
        p = _each(lambda m: _dot(m, m, NN, precision=HI), p)
        t = _each(lambda tt, pp: tt + _dot(tt, pp, NN, precision=HI), t, p)
    return t


def _head_specs():
    q = pl.BlockSpec((S, HB * DH), lambda h: (0, h))
    k = pl.BlockSpec((S, HB * DH), lambda h: (0, H // HB + h))
    v = pl.BlockSpec((S, HB * DH), lambda h: (0, 2 * H // HB + h))
    gb = pl.BlockSpec((HB, S, DH), lambda h: (h, 0, 0))
    gr = pl.BlockSpec((HB, NCH, CH), lambda h: (h, 0, 0))
    return q, k, v, gb, gr


ST_SPEC = pl.BlockSpec((HB, NCH, DH, DH), lambda h: (h, 0, 0, 0))
TM_SPEC = pl.BlockSpec((HB, NCH, CH, CH), lambda h: (h, 0, 0, 0))
HCOL = pl.BlockSpec((S, HB * DH), lambda h: (0, h))


def _delta_fwd(qkvn, gb, gr, bb):
    def body(q_ref, k_ref, v_ref, gb_ref, gr_ref, bb_ref, o_ref, st_ref, tm_ref):
        ri = lax.broadcasted_iota(jnp.int32, (CH, CH), 0)
        ci = lax.broadcasted_iota(jnp.int32, (CH, CH), 1)
        strict = ri > ci
        causal = ri >= ci
        eye = jnp.where(ri == ci, 1.0, 0.0).astype(F32)

        hs = list(range(HB))
        cols = [slice(hh * DH, (hh + 1) * DH) for hh in hs]
        bf = lambda m: m.astype(BF16)

        def local(n):
            rows = pl.ds(pl.multiple_of(n * CH, CH), CH)
            c = dict(rows=rows, n=n)
            c["q"] = [q_ref[rows, cc] for cc in cols]
            c["k"] = [k_ref[rows, cc] for cc in cols]
            c["v"] = [v_ref[rows, cc] for cc in cols]
            c["g"] = [gb_ref[hh, rows, :] for hh in hs]
            c["beta"] = [bb_ref[hh, rows, :] for hh in hs]
            diff = [c["g"][hh][:, :CH] - gr_ref[hh, pl.ds(n, 1), :] for hh in hs]
            c["el"] = _each(lambda d: jnp.exp(jnp.where(causal, d, 0.0)), diff)
            c["eg"] = _each(jnp.exp, c["g"])
            c["gl"] = _each(lambda m: m[CH - 1:CH, :], c["g"])
            c["kb"] = _each(lambda x, y: x * y, c["k"], c["beta"])
            c["kbf"] = _each(bf, c["k"])
            c["a"] = _each(lambda x, y, e: jnp.where(strict, _dot(bf(x), y, NT) * e, 0.0), c["kb"], c["kbf"], c["el"])
            return c

        def advance(c, t, sts):
            n, rows = c["n"], c["rows"]
            for hh in hs:
                tm_ref[hh, n] = t[hh]
                st_ref[hh, n] = sts[hh]
            sb = _each(bf, sts)
            r = _each(lambda vv, bb_, kk, ee, ss: vv * bb_ - _dot(bf(kk * ee), ss, NN), c["v"], c["beta"], c["kb"], c["eg"], sb)
            ub = _each(lambda tt, rr: bf(_dot(tt, rr, NN, precision=HI)), t, r)
            p = _each(lambda qq, kk, e: jnp.where(causal, _dot(bf(qq), kk, NT) * e, 0.0), c["q"], c["kbf"], c["el"])
            o = _each(lambda qq, ee, ss, pp, uu: _dot(bf(qq * ee), ss, NN) + _dot(bf(pp), uu, NN), c["q"], c["eg"], sb, p, ub)
            for hh in hs:
                o_ref[rows, cols[hh]] = o[hh]
            kd = _each(lambda kk, l, gg: kk * jnp.exp(l - gg), c["k"], c["gl"], c["g"])
            return _each(lambda st, l, kk, uu: st * jnp.exp(l) + _dot(bf(kk), uu, TN), sts, c["gl"], kd, ub)

        def step(i, sts):
            c0, c1 = local(2 * i), local(2 * i + 1)
            t = _neumann_inv(c0["a"] + c1["a"], eye)
            sts = advance(c0, t[:HB], list(sts))
            return tuple(advance(c1, t[HB:], sts))

        lax.fori_loop(0, NCH // 2, step, tuple(jnp.zeros((DH, DH), F32) for _ in hs))

    q, k, v, gbs, grs = _head_specs()
    return _call(body, "delta_fwd", [_sds((S, DNW)), _sds((H, NCH, DH, DH)), _sds((H, NCH, CH, CH))], grid=(H // HB,),
                 in_specs=[q, k, v, gbs, grs, gbs], out_specs=[HCOL, ST_SPEC, TM_SPEC],
                 sem=("parallel",))(qkvn, qkvn, qkvn, gb, gr, bb)


def _delta_bwd(qkvn, gb, gr, bb, st_all, tm_all, do_all):
    def body(q_ref, k_ref, v_ref, gb_ref, gr_ref, bb_ref, st_ref, tm_ref, do_ref,
             dq_ref, dk_ref, dv_ref, dg_ref, db_ref):
        ri = lax.broadcasted_iota(jnp.int32, (CH, CH), 0)
        ci = lax.broadcasted_iota(jnp.int32, (CH, CH), 1)
        lo_s, lo_c, up_s, up_c = ri > ci, ri >= ci, ri < ci, ri <= ci
        last_row = lax.broadcasted_iota(jnp.int32, (CH, 1), 0) == CH - 1

        def rs(mat):
            return jnp.sum(mat, axis=1, keepdims=True)

        def total(mat):
            return jnp.sum(rs(mat), axis=0, keepdims=True)

        hs = list(range(HB))
        cols = [slice(hh * DH, (hh + 1) * DH) for hh in hs]
        bf = lambda m: m.astype(BF16)
        mul = lambda x, y: x * y
        spread = jnp.full((8, DH), 1.0 / DH, F32)

        def as_row(col):
            return _dot(spread, jnp.broadcast_to(col, (CH, DH)), NT, precision=HI)[0:1, :]

        def step(i, dss):
            ns = [NCH - 1 - 2 * i, NCH - 2 - 2 * i]
            rws = [pl.ds(pl.multiple_of(n * CH, CH), CH) for n in ns]
            idx = [(cc, hh) for cc in range(2) for hh in hs]
            q = [q_ref[rws[cc], cols[hh]] for cc, hh in idx]
            k = [k_ref[rws[cc], cols[hh]] for cc, hh in idx]
            v = [v_ref[rws[cc], cols[hh]] for cc, hh in idx]
            do = [do_ref[rws[cc], cols[hh]] for cc, hh in idx]
            g = [gb_ref[hh, rws[cc], :] for cc, hh in idx]
            beta = [bb_ref[hh, rws[cc], :] for cc, hh in idx]
            t = [tm_ref[hh, ns[cc]] for cc, hh in idx]
            st = [st_ref[hh, ns[cc]] for cc, hh in idx]
            diff = [gg[:, :CH] - gr_ref[hh, pl.ds(ns[cc], 1), :] for gg, (cc, hh) in zip(g, idx)]
            el = _each(lambda d: jnp.exp(jnp.where(lo_c, d, 0.0)), diff)
            eu = _each(lambda d: jnp.exp(jnp.where(up_c, -d, 0.0)), diff)
            eg = _each(jnp.exp, g)
            gl = _each(lambda m: m[CH - 1:CH, :], g)
            egl = _each(jnp.exp, gl)
            ekd = _each(lambda l, m: jnp.exp(l - m), gl, g)
            kb = _each(mul, k, beta)
            kbg = _each(mul, kb, eg)
            qg = _each(mul, q, eg)
            kd = _each(mul, k, ekd)
            qb, kbf, kbb = _each(bf, q), _each(bf, k), _each(bf, kb)
            kbgb, qgb, kdb = _each(bf, kbg), _each(bf, qg), _each(bf, kd)
            sb, dob = _each(bf, st), _each(bf, do)
            r = _each(lambda vv, b, x, s: vv * b - _dot(x, s, NN), v, beta, kbgb, sb)
            u = _each(lambda tt, rr: _dot(tt, rr, NN, precision=HI), t, r)
            ub = _each(bf, u)
            kk = _each(lambda x, y: _dot(x, y, NT), kbb, kbf)
            qk = _each(lambda x, y: _dot(x, y, NT), qb, kbf)
            kkt = _each(lambda x, y: _dot(x, y, NT), kbf, kbb)
            qkt = _each(lambda x, y: _dot(x, y, NT), kbf, qb)
            pt = _each(lambda m, e: jnp.where(up_c, m * e, 0.0), qkt, eu)
            ds, du, dr, drb, ds_new = [], [], [], [], list(dss)
            for cc in range(2):
                sl = slice(cc * HB, (cc + 1) * HB)
                ds_c = ds_new
                dsb_c = _each(bf, ds_c)
                du_c = _each(lambda p, d, x, s: _dot(bf(p), d, NN) + _dot(x, s, NN), pt[sl], dob[sl], kdb[sl], dsb_c)
                dr_c = _each(lambda tt, d: _dot(tt, d, TN, precision=HI), t[sl], du_c)
                drb_c = _each(bf, dr_c)
                ds_new = _each(lambda x, d, e, s, y, z: _dot(x, d, TN) + e * s - _dot(y, z, TN),
                               qgb[sl], dob[sl], egl[sl], ds_c, kbgb[sl], drb_c)
                ds, du, dr, drb = ds + ds_c, du + du_c, dr + dr_c, drb + drb_c
            dsb = _each(bf, ds)
            dpg = _each(lambda d, uu, e: jnp.where(lo_c, _dot(d, uu, NT), 0.0) * e, dob, ub, el)
            dpgt = _each(lambda uu, d, e: jnp.where(up_c, _dot(uu, d, NT), 0.0) * e, ub, dob, eu)
            dag = _each(lambda d, uu, e: -jnp.where(lo_s, _dot(d, uu, NT), 0.0) * e, drb, ub, el)
            dagt = _each(lambda uu, d, e: -jnp.where(up_s, _dot(uu, d, NT), 0.0) * e, ub, drb, eu)
            dqg = _each(lambda d, s: _dot(d, s, NT), dob, sb)
            dkbg = _each(lambda d, s: -_dot(d, s, NT), drb, sb)
            dkd = _each(lambda uu, s: _dot(uu, s, NT), ub, dsb)
            dkb =_each(lambda a, x, y, e: _dot(bf(a), x, NN) + y * e, dag, kbf, dkbg, eg)
            dk = _each(lambda a, x, p, y, z, e, w, b: _dot(bf(a), x, NN) + _dot(bf(p), y, NN) + z * e + w * b,
                       dagt, kbb, dpgt, qb, dkd, ekd, dkb, beta)
            dq = _each(lambda p, x, y, e: _dot(bf(p), x, NN) + y * e, dpg, kbf, dqg, eg)
            dkd_kd = _each(lambda x, y: rs(x * y), dkd, kd)
            dg = _each(lambda a, x, p, y, at, xt, pt_, yt, z, w, c, d, e:
                       rs(a * x + p * y) - rs(at * xt + pt_ * yt) + rs(z * w) + rs(c * d) - e,
                       dag, kk, dpg, qk, dagt, kkt, dpgt, qkt, dqg, qg, dkbg, kbg, dkd_kd)
            dgl = _each(lambda x, e, s, y: jnp.sum(x, axis=0, keepdims=True) + e[:, 0:1] * total(s * y), dkd_kd, egl, ds, st)
            dg = _each(lambda x, y: x + jnp.where(last_row, y, 0.0), dg, dgl)
            dbeta = _each(lambda x, y, z, w: rs(x * y) + rs(z * w), dkb, k, dr, v)
            for j, (cc, hh) in enumerate(idx):
                dq_ref[rws[cc], cols[hh]] = dq[j]
                dk_ref[rws[cc], cols[hh]] = dk[j]
                dv_ref[rws[cc], cols[hh]] = dr[j] * beta[j]
                dg_ref[hh, pl.ds(ns[cc], 1), :] = as_row(dg[j])
                db_ref[hh, pl.ds(ns[cc], 1), :] = as_row(dbeta[j])
            return tuple(ds_new)

        lax.fori_loop(0, NCH // 2, step, tuple(jnp.zeros((DH, DH), F32) for _ in hs))

    q, k, v, gbs, grs = _head_specs()
    return _call(body, "delta_bwd",
                 [_sds((S, DNW)), _sds((S, DNW)), _sds((S, DNW)), _sds((H, NCH, CH)), _sds((H, NCH, CH))], grid=(H // HB,),
                 in_specs=[q, k, v, gbs, grs, gbs, ST_SPEC, TM_SPEC, HCOL], out_specs=[HCOL, HCOL, HCOL, grs, grs],
                 sem=("parallel",))(qkvn, qkvn, qkvn, gb, gr, bb, st_all, tm_all, do_all)


Z_BLK = O_Z // DNW


def _dn_post_fwd(o, proj, gn):
    def body(o_ref, z_ref, gn_ref, og_ref):
        for h in range(H):
            cols = slice(h * DH, (h + 1) * DH)
            ov = o_ref[:, cols]
            on = ov * lax.rsqrt(jnp.mean(ov * ov, axis=-1, keepdims=True) + EPS) * gn_ref[...]
            og_ref[:, cols] = (on * _silu(z_ref[:, cols])).astype(BF16)

    return _call(body, "dn_post_fwd", _sds((S, DNW), BF16), grid=(S // TS,),
                 in_specs=[_row_spec(DNW), pl.BlockSpec((TS, DNW), lambda i: (i, Z_BLK)), _vec_spec(DH)],
                 out_specs=_row_spec(DNW), sem=("parallel",))(o, proj, gn)


def _dn_post_bwd(dog, o, proj, gn, dproj):
    def body(d_ref, o_ref, z_ref, gn_ref, dproj_in, do_ref, dz_ref, acc_ref):
        dgn = jnp.zeros((1, DH), F32)
        for h in range(H):
            cols = slice(h * DH, (h + 1) * DH)
            ov, zv, dv = o_ref[:, cols], z_ref[:, cols], d_ref[:, cols]
            rinv = lax.rsqrt(jnp.mean(ov * ov, axis=-1, keepdims=True) + EPS)
            xn = ov * rinv
            don = dv * _silu(zv)
            dz_ref[:, cols] = (dv * xn * gn_ref[...] * _dsilu(zv)).astype(BF16)
            dgn = dgn + jnp.sum(don * xn, axis=0, keepdims=True)
            dxn = don * gn_ref[...]
            do_ref[:, cols] = rinv * (dxn - xn * jnp.mean(dxn * xn, axis=-1, keepdims=True))
        _acc_rows(acc_ref, [dgn])

    zspec = pl.BlockSpec((TS, DNW), lambda i: (i, Z_BLK))
    return _call(body, "dn_post_bwd", [_sds((S, DNW)), _sds((S, NINP), BF16), _sds((8, DH))], grid=(S // TS,),
                 in_specs=[_row_spec(DNW), _row_spec(DNW), zspec, _vec_spec(DH), ANY],
                 out_specs=[_row_spec(DNW), zspec, _acc_spec(DH)], sem=("arbitrary",),
                 aliases={4: 1})(dog, o, proj, gn, dproj)


GA_BLK = O_GA // D
GB_BLK = O_GB // D


def _merge_fwd(ba, bb, proj):
    def body(a_ref, b_ref, ga_ref, gb_ref, o_ref):
        o_ref[...] = (_sigmoid(ga_ref[...]) * a_ref[...] + _sigmoid(gb_ref[...]) * b_ref[...]).astype(BF16)

    return _call(body, "merge_fwd", _sds((S, D), BF16), grid=(S // TS,),
                 in_specs=[_row_spec(), _row_spec(), pl.BlockSpec((TS, D), lambda i: (i, GA_BLK)),
                           pl.BlockSpec((TS, D), lambda i: (i, GB_BLK))],
                 out_specs=_row_spec(), sem=("parallel",))(ba, bb, proj, proj)


def _merge_bwd(dm, ba, bb, proj, dproj):
    def body(d_ref, a_ref, b_ref, ga_ref, gb_ref, dproj_in, dg_ref, da_ref, db_ref):
        d = d_ref[...]
        sa, sb = _sigmoid(ga_ref[...]), _sigmoid(gb_ref[...])
        dg_ref[:, 0:D] = (d * a_ref[...] * sa * (1.0 - sa)).astype(BF16)
        dg_ref[:, D:2 * D] = (d * b_ref[...] * sb * (1.0 - sb)).astype(BF16)
        da_ref[...] = (d * sa).astype(BF16)
        db_ref[...] = (d * sb).astype(BF16)

    return _call(body, "merge_bwd", [_sds((S, NINP), BF16), _sds((S, D), BF16), _sds((S, D), BF16)], grid=(S // TS,),
                 in_specs=[_row_spec(), _row_spec(), _row_spec(), pl.BlockSpec((TS, D), lambda i: (i, GA_BLK)),
                           pl.BlockSpec((TS, D), lambda i: (i, GB_BLK)), ANY],
                 out_specs=[pl.BlockSpec((TS, 2 * D), lambda i: (i, O_GA // (2 * D))), _row_spec(), _row_spec()],
                 sem=("parallel",), aliases={5: 0})(dm, ba, bb, proj, proj, dproj)


NSH = NIN // NDEV


def _win_pieces():
    pieces = [(0, 0, 4096), (O_GA, 6160, 2 * D), (O_SM, 4096, 16)]
    for j in range(CFW // LANE):
        pieces.append((O_GLU + 2 * LANE * j, 4112 + LANE * j, LANE))
        pieces.append((O_GLU + 2 * LANE * j + LANE, 4112 + CFW + LANE * j, LANE))
    return pieces


def _pad_win(wt):
    rows = [wt[o:o + wdt] for _, o, wdt in sorted(_win_pieces())]
    rows.append(jnp.zeros((NINP - NIN, wt.shape[1]), wt.dtype))
    return jnp.concatenate(rows, axis=0)


def _unpad_win(gpt):
    return jnp.concatenate([gpt[p:p + wdt] for p, o, wdt in sorted(_win_pieces(), key=lambda t: t[1])], axis=0)


def _lane_vec(v8, offset):
    return jnp.pad(v8, ((0, 0), (offset, LANE - 8 - offset)))


def _tie(vec, token):
    return vec + token


def _local_step(x, tgt, mod, norm1_g, norm2_g, final_g, w_in_p, dn_conv_w, a_log, dt_bias, dn_norm_g,
                cf_conv_w, cf_ln_g, cf_ln_b, ffn_conv_w, comm):
    sh1, sc1, gt1, sh2, sc2, gt2 = (mod[:, i * D:(i + 1) * D] for i in range(6))
    alog_v, dtb_v = _lane_vec(a_log, H), _lane_vec(dt_bias, H)

    hn1 = _norm_mod(x, norm1_g, sc1, _tie(sh1, comm.token0), "norm_mod1")
    proj = _mm(hn1, w_in_p, "nt", F32, "mm_in", tn=1152)
    qkvn = _dn_pre_fwd(proj, dn_conv_w)
    gates = _gates_fwd(proj, alog_v, dtb_v)
    beta_t = gates[:, 0:H].T
    g_t = gates[:, H:2 * H].T
    gb = jnp.broadcast_to(g_t[:, :, None], (H, S, DH))
    bb = jnp.broadcast_to(beta_t[:, :, None], (H, S, DH))
    gr = g_t.reshape(H, NCH, CH)
    o, st_all, tm_all = _delta_fwd(qkvn, gb, gr, bb)
    og = _dn_post_fwd(o, proj, dn_norm_g)
    u1 = _cf_conv_fwd(proj, cf_conv_w)
    u3 = _cf_ln_fwd(u1, cf_ln_g, cf_ln_b)
    after = og[0:8, 0:LANE].astype(F32) + u3[0:8, 0:LANE].astype(F32)
    dn_w_o, cf_w_o, w_out, ffn_w_up, ffn_w_down = comm.late_weights(after)
    br_a = _mm(og, dn_w_o, "nn", F32, "mm_dn_o")
    br_b = _mm(u3, cf_w_o, "nn", F32, "mm_cf_o")
    merged = _merge_fwd(br_a, br_b, proj)
    mix = _mm(merged, w_out, "nn", F32, "mm_out")
    x2, hn2 = _resid_norm_mod(x, mix, gt1, norm2_g, sc2, sh2, "resid_norm_mod2")
    upall = _mm(hn2, ffn_w_up, "nn", F32, "mm_up")
    hmid = _ffn_mid_fwd(upall, ffn_conv_w)
    f = _mm(hmid, ffn_w_down, "nn", F32, "mm_down")

    dx3, df, acc_f = _loss_head(x2, f, tgt, gt2, final_g)
    d_final_g, d_gt2, loss = acc_f[0:1], acc_f[1:2], acc_f[2:3, 0:1]
    dhmid = _mm(df, ffn_w_down, "nt", F32, "mm_down_dx")
    g_w_down = _mm(hmid, df, "tn", BF16, "mm_down_dw")
    d_gate, d_up, g_ffn_conv = _ffn_mid_bwd(dhmid, upall, ffn_conv_w)
    dhn2 = _mm(d_gate, ffn_w_up, "nt", F32, "mm_up_dx", a2=d_up)
    g_w_up = _mm(hn2, d_gate, "tn", BF16, "mm_up_dw", tn=2 * FFN // NDEV, b2=d_up)
    tok_a = comm.grads("a", dict(ffn_w_down=g_w_down, ffn_w_up=g_w_up))
    dx2, dmix, acc2 = _norm_mod_bwd(dhn2, x2, dx3, _tie(norm2_g, tok_a), sc2, "norm_mod2_bwd", mix=mix, gt=gt1)
    d_sh2, d_sc2, d_norm2_g, d_gt1 = acc2[0:1], acc2[1:2], acc2[2:3], acc2[3:4]
    dmerged = _mm(dmix, w_out, "nt", F32, "mm_out_dx")
    g_w_out = _mm(merged, dmix, "tn", BF16, "mm_out_dw")
    d_proj, d_bra, d_brb = _merge_bwd(dmerged, br_a, br_b, proj, lax.empty((S, NINP), BF16))
    du3 = _mm(d_brb, cf_w_o, "nt", F32, "mm_cf_o_dx")
    g_cf_w_o = _mm(u3, d_brb, "tn", BF16, "mm_cf_o_dw")
    du1, acc_ln = _cf_ln_bwd(du3, u1, cf_ln_g, cf_ln_b)
    d_proj, g_cf_conv = _cf_conv_bwd(du1, proj, cf_conv_w, d_proj)
    dog = _mm(d_bra, dn_w_o, "nt", F32, "mm_dn_o_dx")
    g_dn_w_o = _mm(og, d_bra, "tn", BF16, "mm_dn_o_dw")
    tok_b = comm.grads("b", dict(w_out=g_w_out, cf_w_o=g_cf_w_o, dn_w_o=g_dn_w_o, ffn_conv_w=g_ffn_conv,
                                 cf_conv_w=g_cf_conv))
    do, d_proj, acc_gn = _dn_post_bwd(dog, o, proj, _tie(dn_norm_g, tok_b), d_proj)
    dq, dk, dv, dgr, dbr = _delta_bwd(qkvn, gb, gr, bb, st_all, tm_all, do)
    d_proj, g_dn_conv = _dn_pre_bwd(dq, dk, dv, proj, dn_conv_w, d_proj)
    dgates = jnp.concatenate([dbr.reshape(H, S).T, dgr.reshape(H, S).T, jnp.zeros((S, LANE - 2 * H), F32)], axis=1)
    d_proj, acc_g = _gates_bwd(dgates, proj, alog_v, dtb_v, d_proj)
    g_w_in_p = _mm(d_proj, hn1, "tn", BF16, "mm_in_dw", tm=1152)
    tok_c = comm.grads("c", dict(w_in=g_w_in_p, dn_conv_w=g_dn_conv))
    dhn1 = _mm(d_proj, w_in_p, "nn", F32, "mm_in_dx", tk=1152, dep=jnp.broadcast_to(tok_c, (8, LANE)))
    grad_x, acc1 = _norm_mod_bwd(dhn1, x, dx2, norm1_g, sc1, "norm_mod1_bwd")
    d_sh1, d_sc1, d_norm1_g = acc1[0:1], acc1[1:2], acc1[2:3]

    d_mod = jnp.concatenate([d_sh1, d_sc1, d_gt1, d_sh2, d_sc2, d_gt2], axis=1)
    small = dict(mod=d_mod, norm1_g=d_norm1_g, norm2_g=d_norm2_g, final_norm_g=d_final_g,
                 cf_ln_g=acc_ln[0:1], cf_ln_b=acc_ln[1:2], dn_norm_g=acc_gn[0:1],
                 dn_a_log=acc_g[0:1, H:2 * H], dn_dt_bias=acc_g[1:2, H:2 * H])
    return loss, grad_x, small


def _dev_index(px, py, pc):
    return 4 * px + 2 * py + pc


def _all_gather(arrs, name):
    n = len(arrs)

    def body(*refs):
        ins, outs = refs[:n], refs[n:2 * n]
        send_sems, recv_sems, loc_sems = refs[2 * n:]
        x, y, c = _my_pos()
        me, sib = (x, y, c), (x, y, 1 - c)
        chips = [(1 - x, y), (x, 1 - y), (1 - x, 1 - y)]

        def cp(i, k, block, to, src=None):
            dst = outs[i].at[_dev_index(*block)]
            return pltpu.make_async_remote_copy(
                src_ref=dst if src is None else src, dst_ref=dst, send_sem=send_sems.at[i, k],
                recv_sem=recv_sems.at[i, k], device_id=to, device_id_type=MESH)

        mine = [pltpu.make_async_copy(ins[i], outs[i].at[_dev_index(*me)], loc_sems.at[i]) for i in range(n)]
        for m in mine:
            m.start()
        sent = []
        for i in range(n):
            sent.append(cp(i, 0, me, sib, src=ins[i]))
            sent += [cp(i, 1 + j, me, (*chip, c), src=ins[i]) for j, chip in enumerate(chips)]
        for s in sent:
            s.start()
        for i in range(n):
            for j, chip in enumerate(chips):
                cp(i, 1 + j, (*chip, c), me).wait_recv()
                fwd = cp(i, 4 + j, (*chip, c), sib)
                fwd.start()
                sent.append(fwd)
        for i in range(n):
            cp(i, 0, sib, me).wait_recv()
            for j, chip in enumerate(chips):
                cp(i, 4 + j, (*chip, 1 - c), me).wait_recv()
        for s in sent:
            s.wait_send()
        for m in mine:
            m.wait()

    outs = pl.pallas_call(
        body, out_shape=[_sds((NDEV,) + a.shape, a.dtype) for a in arrs], in_specs=[ANY] * n, out_specs=[ANY] * n,
        scratch_shapes=[pltpu.SemaphoreType.DMA((n, 7)), pltpu.SemaphoreType.DMA((n, 7)), pltpu.SemaphoreType.DMA((n,))],
        name=name)(*arrs)
    return list(outs)


def _slab(ref, layout, idx):
    kind, n = layout
    if kind == "rows":
        return ref.at[pl.ds(pl.multiple_of(idx * n, n), n), :]
    if kind == "cols":
        return ref.at[:, pl.ds(pl.multiple_of(idx * n, n), n)]
    return ref.at[idx]


def _slab_shape(arr, layout):
    kind, n = layout
    if kind == "rows":
        return (n, arr.shape[1])
    if kind == "cols":
        return (arr.shape[0], n)
    return tuple(arr.shape[1:])


def _pair_exchange(parts, layouts, name):
    n = len(parts)

    def body(*refs):
        ins, outs = refs[:n], refs[n:2 * n]
        send_sems, recv_sems = refs[2 * n:]
        x, y, c = _my_pos()
        copies = []
        for i in range(n):
            for q in range(4):
                copies.append(pltpu.make_async_remote_copy(
                    src_ref=_slab(ins[i], layouts[i], 2 * q + (1 - c)), dst_ref=outs[i].at[q],
                    send_sem=send_sems.at[i, q], recv_sem=recv_sems.at[i, q], device_id=(x, y, 1 - c),
                    device_id_type=MESH))
        for cpy in copies:
            cpy.start()
        for cpy in copies:
            cpy.wait()

    outs = pl.pallas_call(
        body, out_shape=[_sds((4,) + _slab_shape(p, lay), p.dtype) for p, lay in zip(parts, layouts)],
        in_specs=[ANY] * n, out_specs=[ANY] * n,
        scratch_shapes=[pltpu.SemaphoreType.DMA((n, 4)), pltpu.SemaphoreType.DMA((n, 4))], name=name)(*parts)
    return list(outs)


HBM = pl.BlockSpec(memory_space=pltpu.HBM)
SEMS = pl.BlockSpec(memory_space=pltpu.SEMAPHORE)
EFFECT = pltpu.SideEffectType.DATAFLOW_SIDE_EFFECTING
TOKEN = jax.ShapeDtypeStruct((8, LANE), F32)


def _hbm(a):
    return pltpu.with_memory_space_constraint(a, pltpu.HBM)


def _gather_ici_copy(shard_ref, buf_ref, layout, send_sems, recv_sems, i, j, me, chip, c):
    return pltpu.make_async_remote_copy(
        src_ref=shard_ref, dst_ref=_slab(buf_ref, layout, me), send_sem=send_sems.at[3 * i + j],
        recv_sem=recv_sems.at[3 * i + j], device_id=(*chip, c), device_id_type=MESH)


def _gather_ici_start(shards, bufs, layouts, after, name):
    n = len(shards)

    def body(*refs):
        sh, bf = refs[:n], refs[n:2 * n]
        send_sems, recv_sems = refs[2 * n + 1], refs[2 * n + 2]
        token = refs[-1]
        x, y, c = _my_pos()
        me = _dev_index(x, y, c)
        for i in range(n):
            for j, chip in enumerate([(1 - x, y), (x, 1 - y), (1 - x, 1 - y)]):
                _gather_ici_copy(sh[i], bf[i], layouts[i], send_sems, recv_sems, i, j, me, chip, c).start()
        token[...] = jnp.zeros_like(token)

    outs = pl.pallas_call(
        body, name=name,
        out_shape=(pltpu.SemaphoreType.DMA((3 * n,)), pltpu.SemaphoreType.DMA((3 * n,)),
                   *[pltpu.HBM(a.shape, a.dtype) for a in shards], *[pltpu.HBM(a.shape, a.dtype) for a in bufs], TOKEN),
        in_specs=[HBM] * (2 * n) + [ANY],
        out_specs=(SEMS, SEMS, *[HBM] * (2 * n), pl.BlockSpec(memory_space=pltpu.VMEM)),
        input_output_aliases={i: 2 + i for i in range(2 * n)},
        compiler_params=pltpu.CompilerParams(has_side_effects=EFFECT),
    )(*[_hbm(a) for a in shards], *[_hbm(a) for a in bufs], after)
    return outs[0], outs[1], list(outs[2:2 + n]), list(outs[2 + n:2 + 2 * n]), outs[-1]


def _gather_ici_wait(send_sems, recv_sems, shards, bufs, layouts, after, name):
    n = len(shards)

    def body(*refs):
        sh, bf = refs[:n], refs[n:2 * n]
        ssem, rsem = refs[2 * n], refs[2 * n + 1]
        x, y, c = _my_pos()
        me = _dev_index(x, y, c)
        for i in range(n):
            for j, chip in enumerate([(1 - x, y), (x, 1 - y), (1 - x, 1 - y)]):
                cp = _gather_ici_copy(sh[i], bf[i], layouts[i], ssem, rsem, i, j, me, chip, c)
                cp.wait_send()
                cp.wait_recv()

    outs = pl.pallas_call(
        body, name=name,
        out_shape=(*[pltpu.HBM(a.shape, a.dtype) for a in shards], *[pltpu.HBM(a.shape, a.dtype) for a in bufs]),
        in_specs=[HBM] * (2 * n) + [SEMS, SEMS, ANY], out_specs=tuple([HBM] * (2 * n)),
        input_output_aliases={i: i for i in range(2 * n)},
        compiler_params=pltpu.CompilerParams(has_side_effects=EFFECT),
    )(*shards, *bufs, send_sems, recv_sems, after)
    return list(outs[:n]), list(outs[n:])


def _place_own(pos, shard, buf, layout, name):
    kind, n = layout
    r, cols = shard.shape
    tr = _row_tile(r, shard.dtype.itemsize)
    nr = r // tr
    if kind == "rows":
        ospec = pl.BlockSpec((tr, cols), lambda i, p: (p[2] * nr + i, 0))
    else:
        assert kind == "lead"
        ospec = pl.BlockSpec((None, tr, cols), lambda i, p: (p[2], i, 0))

    def body(pos_ref, s_ref, buf_in, o_ref):
        o_ref[...] = s_ref[...]

    return pl.pallas_call(
        body, out_shape=_sds(buf.shape, buf.dtype), name=name, input_output_aliases={2: 0},
        grid_spec=pltpu.PrefetchScalarGridSpec(
            num_scalar_prefetch=1, grid=(nr,), in_specs=[pl.BlockSpec((tr, cols), lambda i, p: (i, 0)), ANY],
            out_specs=ospec),
        compiler_params=pltpu.CompilerParams(dimension_semantics=("parallel",), vmem_limit_bytes=VMEM_LIMIT),
    )(pos, shard, buf)


def _gather_pair(shards, bufs, layouts, name):
    n = len(shards)

    def body(*refs):
        sh, bo = refs[:n], refs[2 * n:3 * n]
        send_sems, recv_sems = refs[3 * n:]
        x, y, c = _my_pos()
        sib = (x, y, 1 - c)
        copies = []
        for i in range(n):
            for k, (px, py) in enumerate([(x, y), (1 - x, y), (x, 1 - y), (1 - x, 1 - y)]):
                slab = _slab(bo[i], layouts[i], _dev_index(px, py, c))
                copies.append(pltpu.make_async_remote_copy(
                    src_ref=sh[i] if k == 0 else slab, dst_ref=slab, send_sem=send_sems.at[i, k],
                    recv_sem=recv_sems.at[i, k], device_id=sib, device_id_type=MESH))
        for cpy in copies:
            cpy.start()
        for cpy in copies:
            cpy.wait()

    outs = pl.pallas_call(
        body, out_shape=[_sds(a.shape, a.dtype) for a in bufs], in_specs=[ANY] * (2 * n), out_specs=[ANY] * n,
        input_output_aliases={n + i: i for i in range(n)},
        scratch_shapes=[pltpu.SemaphoreType.DMA((n, 4)), pltpu.SemaphoreType.DMA((n, 4))], name=name)(*shards, *bufs)
    return list(outs)


def _chip_copy(sum_ref, land_ref, send_sems, recv_sems, i, j, chip, c):
    return pltpu.make_async_remote_copy(
        src_ref=sum_ref.at[2 * chip[0] + chip[1]], dst_ref=land_ref.at[j], send_sem=send_sems.at[3 * i + j],
        recv_sem=recv_sems.at[3 * i + j], device_id=(*chip, c), device_id_type=MESH)


def _chip_exchange_start(sums, name):
    n = len(sums)
    lands = [lax.empty((3,) + s.shape[1:], s.dtype) for s in sums]

    def body(*refs):
        sm, ld = refs[:n], refs[n:2 * n]
        send_sems, recv_sems = refs[2 * n], refs[2 * n + 1]
        token = refs[-1]
        x, y, c = _my_pos()
        for i in range(n):
            for j, chip in enumerate([(1 - x, y), (x, 1 - y), (1 - x, 1 - y)]):
                _chip_copy(sm[i], ld[i], send_sems, recv_sems, i, j, chip, c).start()
        token[...] = jnp.zeros_like(token)

    outs = pl.pallas_call(
        body, name=name,
        out_shape=(pltpu.SemaphoreType.DMA((3 * n,)), pltpu.SemaphoreType.DMA((3 * n,)),
                   *[pltpu.HBM(a.shape, a.dtype) for a in sums], *[pltpu.HBM(a.shape, a.dtype) for a in lands], TOKEN),
        in_specs=[HBM] * (2 * n), out_specs=(SEMS, SEMS, *[HBM] * (2 * n), pl.BlockSpec(memory_space=pltpu.VMEM)),
        input_output_aliases={i: 2 + i for i in range(2 * n)},
        compiler_params=pltpu.CompilerParams(has_side_effects=EFFECT),
    )(*[_hbm(a) for a in sums], *[_hbm(a) for a in lands])
    return outs[0], outs[1], list(outs[2:2 + n]), list(outs[2 + n:2 + 2 * n]), outs[-1]


def _chip_exchange_wait(send_sems, recv_sems, sums, lands, after, name):
    n = len(sums)

    def body(*refs):
        sm, ld = refs[:n], refs[n:2 * n]
        ssem, rsem = refs[2 * n], refs[2 * n + 1]
        x, y, c = _my_pos()
        for i in range(n):
            for j, chip in enumerate([(1 - x, y), (x, 1 - y), (1 - x, 1 - y)]):
                cp = _chip_copy(sm[i], ld[i], ssem, rsem, i, j, chip, c)
                cp.wait_send()
                cp.wait_recv()

    outs = pl.pallas_call(
        body, name=name,
        out_shape=(*[pltpu.HBM(a.shape, a.dtype) for a in sums], *[pltpu.HBM(a.shape, a.dtype) for a in lands]),
        in_specs=[HBM] * (2 * n) + [SEMS, SEMS, ANY], out_specs=tuple([HBM] * (2 * n)),
        input_output_aliases={i: i for i in range(2 * n)},
        compiler_params=pltpu.CompilerParams(has_side_effects=EFFECT),
    )(*sums, *lands, send_sems, recv_sems, after)
    return list(outs[:n]), list(outs[n:])


def _row_tile(r, itemsize):
    align = 32 // itemsize
    best = r
    for t in range(align, min(r, 256) + 1, align):
        if r % t == 0:
            best = t
    return best


def _prefetch_call(body, name, out_shape, grid, in_specs, out_specs, sem):
    return pl.pallas_call(
        body, out_shape=out_shape, name=name,
        grid_spec=pltpu.PrefetchScalarGridSpec(num_scalar_prefetch=1, grid=grid, in_specs=in_specs, out_specs=out_specs),
        compiler_params=pltpu.CompilerParams(dimension_semantics=sem, vmem_limit_bytes=VMEM_LIMIT))


def _pair_sum(pos, part, got, layout, name):
    kind, _ = layout
    _, r, cols = got.shape
    tr, tc = _tiles(r, cols, part.dtype.itemsize)
    nr, nc = r // tr, cols // tc
    if kind == "rows":
        pspec = pl.BlockSpec((tr, tc), lambda q, i, j, p: ((2 * q + p[0]) * nr + i, j))
    elif kind == "cols":
        pspec = pl.BlockSpec((tr, tc), lambda q, i, j, p: (i, (2 * q + p[0]) * nc + j))
    else:
        pspec = pl.BlockSpec((None, tr, tc), lambda q, i, j, p: (2 * q + p[0], i, j))

    def body(pos_ref, p_ref, g_ref, o_ref):
        o_ref[...] = (p_ref[...].astype(F32) + g_ref[...].astype(F32)).astype(o_ref.dtype)

    blk = pl.BlockSpec((None, tr, tc), lambda q, i, j, p: (q, i, j))
    return _prefetch_call(body, name, _sds((4, r, cols), part.dtype), (4, nr, nc), [pspec, blk], blk,
                          ("parallel", "parallel", "parallel"))(pos, part, got)


def _tiles(r, cols, itemsize):
    tr = _row_tile(r, itemsize)
    if tr < r or r * cols * 4 <= (2 << 20) or cols % 256:
        return tr, cols
    return r, 256


def _final_sum_adam(pos, sums, got, w, m, v, name):
    _, r, cols = w.shape
    tr, tc = _tiles(r, cols, sums.dtype.itemsize)

    def body(pos_ref, s_ref, g_ref, w_ref, m_ref, v_ref, go_ref, dl_ref, nm_ref, nv_ref):
        g = ((s_ref[...].astype(F32) + g_ref[0].astype(F32)) + g_ref[1].astype(F32)) + g_ref[2].astype(F32)
        dl, nm, nv = _adam(w_ref[...], g, m_ref[...], v_ref[...])
        go_ref[...] = g
        dl_ref[...] = dl
        nm_ref[...] = nm
        nv_ref[...] = nv

    big = pl.BlockSpec((None, tr, tc), lambda i, j, p: (0, i, j))
    return _prefetch_call(body, name, [_sds((1, r, cols))] * 4, (r // tr, cols // tc),
                          [pl.BlockSpec((None, tr, tc), lambda i, j, p: (p[1], i, j)),
                           pl.BlockSpec((3, tr, tc), lambda i, j, p: (0, i, j)), big, big, big],
                          [big] * 4, ("parallel", "parallel"))(pos, sums, got, w, m, v)


def _small_adam(g_all, w, m, v):
    npk = w.shape[1]

    def body(g_ref, w_ref, m_ref, v_ref, go_ref, dl_ref, nm_ref, nv_ref):
        g = g_ref[0:1, :]
        for k in range(1, NDEV):
            g = g + g_ref[k:k + 1, :]
        dl, nm, nv = _adam(w_ref[...], g, m_ref[...], v_ref[...])
        go_ref[...] = g
        dl_ref[...] = dl
        nm_ref[...] = nm
        nv_ref[...] = nv

    return _call(body, "small_adam", [_sds((1, npk))] * 4)(g_all, w, m, v)


SMALL = [("b_ada", 6 * D), ("norm1_g", D), ("norm2_g", D), ("final_norm_g", D), ("cf_ln_g", CFW), ("cf_ln_b", CFW),
         ("dn_norm_g", DH), ("dn_a_log", H), ("dn_dt_bias", H)]
LATE = ["dn_w_o", "cf_w_o", "w_out", "ffn_w_up", "ffn_w_down"]
LATE_SHAPE = {"dn_w_o": (NDEV, DNW, D // NDEV), "cf_w_o": (NDEV, CFW, D // NDEV), "w_out": (D, D),
              "ffn_w_up": (NDEV, D, 2 * FFN // NDEV), "ffn_w_down": (FFN, D)}
LATE_LAYOUT = {"dn_w_o": ("lead", NDEV), "cf_w_o": ("lead", NDEV), "w_out": ("rows", D // NDEV),
               "ffn_w_up": ("lead", NDEV), "ffn_w_down": ("rows", FFN // NDEV)}
LAYOUT = {"dn_w_o": ("cols", D // NDEV), "cf_w_o": ("cols", D // NDEV), "w_out": ("rows", D // NDEV),
          "ffn_w_up": ("cols", 2 * FFN // NDEV), "ffn_w_down": ("rows", FFN // NDEV),
          "w_in": ("lead", NDEV), "dn_conv_w": ("lead", NDEV), "cf_conv_w": ("lead", NDEV), "ffn_conv_w": ("lead", NDEV)}
NAMES = ["w_ada", "b_ada", "norm1_g", "w_in", "dn_conv_w", "dn_a_log", "dn_dt_bias", "dn_norm_g", "dn_w_o", "cf_conv_w",
         "cf_ln_g", "cf_ln_b", "cf_w_o", "w_out", "norm2_g", "ffn_w_up", "ffn_conv_w", "ffn_w_down", "final_norm_g"]


def _pack_small(d):
    rows = []
    for nm, n in SMALL:
        row = d[nm].reshape(1, n)
        pad = (-n) % LANE
        rows.append(jnp.pad(row, ((0, 0), (0, pad))) if pad else row)
    return jnp.concatenate(rows, axis=1)


def _unpack_small(row, shapes):
    out, off = {}, 0
    for nm, n in SMALL:
        out[nm] = row[0, off:off + n].reshape(shapes[nm])
        off += n + ((-n) % LANE)
    return out


def _cols_from_gathered(g):
    return jnp.transpose(g, (1, 0, 2)).reshape(g.shape[1], NDEV * g.shape[2])


def _cols_to_parts(full):
    r, ctot = full.shape
    return jnp.transpose(full.reshape(r, NDEV, ctot // NDEV), (1, 0, 2))


def kernel(x, c, w_ada, b_ada, norm1_g, w_in, dn_conv_w, dn_a_log, dn_dt_bias, dn_norm_g, dn_w_o, cf_conv_w, cf_ln_g, cf_ln_b, cf_w_o, w_out, norm2_g, ffn_w_up, ffn_conv_w, ffn_w_down, final_norm_g, loss_target, m_w_ada, m_b_ada, m_norm1_g, m_w_in, m_dn_conv_w, m_dn_a_log, m_dn_dt_bias, m_dn_norm_g, m_dn_w_o, m_cf_conv_w, m_cf_ln_g, m_cf_ln_b, m_cf_w_o, m_w_out, m_norm2_g, m_ffn_w_up, m_ffn_conv_w, m_ffn_w_down, m_final_norm_g, v_w_ada, v_b_ada, v_norm1_g, v_w_in, v_dn_conv_w, v_dn_a_log, v_dn_dt_bias, v_dn_norm_g, v_dn_w_o, v_cf_conv_w, v_cf_ln_g, v_cf_ln_b, v_cf_w_o, v_w_out, v_norm2_g, v_ffn_w_up, v_ffn_conv_w, v_ffn_w_down, v_final_norm_g):
    args = locals()
    w = {nm: args[nm] for nm in NAMES}
    mo = {nm: args["m_" + nm] for nm in NAMES}
    vo = {nm: args["v_" + nm] for nm in NAMES}
    shapes = {nm: w[nm].shape for nm in NAMES}
    px, py, pc = _my_pos()
    me = _dev_index(px, py, pc)

    def mat(a):
        return a.reshape(a.shape[-2:])

    pos = jnp.stack([pc, 2 * px + py, me]).astype(jnp.int32)

    first = ["w_in", "dn_conv_w", "cf_conv_w", "ffn_conv_w"]
    tr_in = lambda a: jnp.transpose(a, (0, 2, 1))
    got = _all_gather([tr_in(w["w_in"]).astype(BF16)] + [mat(w[nm]) for nm in first[1:]] + [c], "gather_first")
    full = {nm: _cols_from_gathered(g) for nm, g in zip(first[1:], got[1:-1])}
    c_all = got[-1].reshape(NDEV, D)
    w_in_p = _pad_win(got[0].reshape(NIN, D))

    ncol = 6 * D // NDEV
    b_sh = lax.dynamic_slice(b_ada.reshape(1, 6 * D), (0, me * ncol), (1, ncol))
    mod_sh = _ada_fwd(c_all, mat(w_ada), b_sh)
    mod_all = _all_gather([mod_sh], "gather_mod")[0]
    mod = lax.dynamic_index_in_dim(mod_all, me, axis=1, keepdims=False).reshape(1, 6 * D)

    late_shards = [mat(w[nm]).astype(BF16) for nm in LATE]
    late_lay = [LATE_LAYOUT[nm] for nm in LATE]
    late_bufs = [_place_own(pos, s, lax.empty(LATE_SHAPE[nm], BF16), lay, "place_" + nm)
                 for nm, s, lay in zip(LATE, late_shards, late_lay)]
    l_send, l_recv, l_shards, l_bufs, l_token = _gather_ici_start(late_shards, late_bufs, late_lay, mod_all, "gather_late_start")

    res = {}

    class Comm:
        token0 = l_token[0, 0]
        pending = {}

        @staticmethod
        def late_weights(after):
            shards, bufs = _gather_ici_wait(l_send, l_recv, l_shards, l_bufs, late_lay, after, "gather_late_wait")
            return _gather_pair(shards, bufs, late_lay, "gather_late_pair")

        @staticmethod
        def grads(group, gd):
            names = list(gd)
            lays = [LAYOUT[nm] for nm in names]
            gl = []
            for nm in names:
                if nm == "w_in":
                    gl.append(_unpad_win(gd[nm]).reshape(NDEV, NSH, D))
                else:
                    gl.append(_cols_to_parts(gd[nm]) if LAYOUT[nm][0] == "lead" else gd[nm])
            from_sib = _pair_exchange(gl, lays, "rs_pair_" + group)
            sums = [_pair_sum(pos, g, r, lay, "rs_pair_sum_" + nm) for nm, g, r, lay in zip(names, gl, from_sib, lays)]
            started = _chip_exchange_start(sums, "rs_chips_start_" + group)
            Comm.pending[group] = (names,) + tuple(started[:4])
            return started[4][0, 0]

        @staticmethod
        def finish(group, after):
            names, ssem, rsem, sums, lands = Comm.pending[group]
            sums, lands = _chip_exchange_wait(ssem, rsem, sums, lands, after, "rs_chips_wait_" + group)
            for nm, s, r in zip(names, sums, lands):
                if nm == "w_in":
                    outs = _final_sum_adam(pos, s, r, tr_in(w[nm]), tr_in(mo[nm]), tr_in(vo[nm]), "adam_" + nm)
                    res[nm] = [tr_in(o) for o in outs]
                else:
                    res[nm] = _final_sum_adam(pos, s, r, w[nm], mo[nm], vo[nm], "adam_" + nm)
            return res[names[-1]][0]

    vec = lambda a: a.reshape(1, -1)
    loss, grad_x, small = _local_step(
        x.reshape(S, D), loss_target.reshape(S, D), mod, vec(norm1_g), vec(norm2_g), vec(final_norm_g), w_in_p,
        full["dn_conv_w"], vec(dn_a_log), vec(dn_dt_bias), vec(dn_norm_g), full["cf_conv_w"], vec(cf_ln_g),
        vec(cf_ln_b), full["ffn_conv_w"], Comm)

    done_a = Comm.finish("a", grad_x)
    done_b = Comm.finish("b", done_a)

    small["b_ada"] = small.pop("mod")
    packed = _pack_small(small) + 0.0 * done_b.reshape(-1)[0]
    g_small = _all_gather([packed], "gather_small")[0].reshape(NDEV, -1)
    outs = _small_adam(g_small, _pack_small({nm: w[nm] for nm, _ in SMALL}), _pack_small({nm: mo[nm] for nm, _ in SMALL}),
                       _pack_small({nm: vo[nm] for nm, _ in SMALL}))
    unpacked = [_unpack_small(o, shapes) for o in outs]
    for nm, _ in SMALL:
        res[nm] = [u[nm] for u in unpacked]

    dmod_sel = lax.dynamic_slice(g_small[:, :6 * D], (0, me * ncol), (NDEV, ncol))
    outs = _ada_bwd_adam(c_all, dmod_sel, mat(w_ada), mat(m_w_ada), mat(v_w_ada))
    res["w_ada"] = [o.reshape(shapes["w_ada"]) for o in outs]
    Comm.finish("c", jnp.concatenate([done_b.reshape(-1)[:LANE], outs[0].reshape(-1)[:LANE]]))

    loss = lax.psum(loss.reshape(()), ("x", "y", "c"))
    out = [loss, grad_x.reshape(x.shape)]
    for k in range(4):
        out += [res[nm][k] for nm in NAMES]
    return tuple(out)
```

```python
import functools

import jax
import jax.numpy as jnp
from jax import lax
from jax.experimental import pallas as pl
from jax.experimental.pallas import tpu as pltpu

F32 = jnp.float32
BF16 = jnp.bfloat16
HI = lax.Precision.HIGHEST
MESH = pl.DeviceIdType.MESH
ANY = pl.BlockSpec(memory_space=pl.ANY)

NDEV = 8
D = 2048
S = 2048
H = 8
DH = 128
DNW = H * DH
CFW = 1024
CFK = 31
DNK = 4
FFN = 5632
FFK = 3
CH = 64
NCH = S // CH
EPS = 1e-6
NIN = 10256
NINP = 10368
O_Z, O_GA, O_GB, O_GLU, O_SM = 3072, 4096, 6144, 8192, 10240
LANE = 128
TS = 256
VMEM_LIMIT = 56 * 1024 * 1024

ADAM_LR, ADAM_B1, ADAM_B2, ADAM_EPS, ADAM_WD, ADAM_STEP = 0.001, 0.9, 0.999, 1e-08, 0.01, 10


def _call(body, name, out_shape, grid=(), in_specs=None, out_specs=None, scratch=(), sem=None, aliases=None):
    kw = {}
    if aliases:
        kw["input_output_aliases"] = aliases
    if in_specs is not None:
        kw["in_specs"] = in_specs
    if out_specs is not None:
        kw["out_specs"] = out_specs
    return pl.pallas_call(
        body, out_shape=out_shape, grid=grid, scratch_shapes=scratch, name=name,
        compiler_params=pltpu.CompilerParams(dimension_semantics=sem, vmem_limit_bytes=VMEM_LIMIT), **kw)


def _sds(shape, dtype=F32):
    return jax.ShapeDtypeStruct(shape, dtype)


def _tile(dim, pref):
    if dim <= pref:
        return dim
    best = None
    for t in range(LANE, pref + 1, LANE):
        if dim % t == 0:
            best = t
    assert best is not None, (dim, pref)
    return best


def _sigmoid(x):
    return 1.0 / (1.0 + jnp.exp(-x))


def _silu(x):
    return x * _sigmoid(x)


def _dsilu(x):
    s = _sigmoid(x)
    return s * (1.0 + x * (1.0 - s))


def _softplus(x):
    return jnp.maximum(x, 0.0) + jnp.log(1.0 + jnp.exp(-jnp.abs(x)))


def _dot(a, b, dims, precision=None):
    return lax.dot_general(a, b, (dims, ((), ())), preferred_element_type=F32, precision=precision)


NN = ((1,), (0,))
NT = ((1,), (1,))
TN = ((0,), (0,))


def _my_pos():
    return lax.axis_index("x"), lax.axis_index("y"), lax.axis_index("c")


def _mm(a, b, mode, out_dtype, name, tm=1024, tn=1024, tk=2048, a2=None, b2=None, dep=None):
    sharded = b.ndim == 3
    if sharded and mode == "nn":
        cs = b.shape[2]
        (m, k), n = a.shape, NDEV * cs
        gs = max(1, tn // cs)
        tm, tn, tk = _tile(m, tm), gs * cs, _tile(k, tk)
    elif sharded:
        assert mode == "nt"
        cs = b.shape[2]
        m, n, k = a.shape[0], b.shape[1], NDEV * cs
        gs = max(1, tk // cs)
        tm, tn, tk = _tile(m, tm), _tile(n, tn), gs * cs
    else:
        if mode == "nn":
            (m, k), (k2, n) = a.shape, b.shape
        elif mode == "nt":
            (m, k), (n, k2) = a.shape, b.shape
        else:
            (k, m), (k2, n) = a.shape, b.shape
        assert k == k2, (a.shape, b.shape, mode)
        n = n * (2 if b2 is not None else 1)
        tm, tn, tk = _tile(m, tm), _tile(n // (2 if b2 is not None else 1), tn), _tile(k, tk)
    nk, nj = k // tk, n // tn
    halfk, halfj = nk // 2, nj // 2
    dims = {"nn": NN, "nt": NT, "tn": TN}[mode]

    n_in = 2 + (a2 is not None) + (b2 is not None) + (dep is not None)

    def body(*refs):
        a_ref, b_ref = refs[0], refs[1]
        x_ref = refs[2] if (a2 is not None or b2 is not None) else None
        o_ref = refs[n_in]
        acc_ref = refs[n_in + 1] if nk > 1 else None
        j, kk = pl.program_id(1), pl.program_id(2)

        if nk > 1:
            @pl.when(kk == 0)
            def _():
                acc_ref[...] = jnp.zeros_like(acc_ref)

        def accumulate(product, cols=slice(None)):
            if nk == 1:
                o_ref[:, cols] = product().astype(o_ref.dtype)
            else:
                acc_ref[:, cols] += product()

        if sharded and mode == "nn":
            for q in range(gs):
                accumulate(lambda q=q: _dot(a_ref[...], b_ref[q], NN), slice(q * cs, (q + 1) * cs))
        elif sharded:
            def contract(lhs_ref):
                def product():
                    part = None
                    for q in range(gs):
                        term = _dot(lhs_ref[:, q * cs:(q + 1) * cs], b_ref[q], NT)
                        part = term if part is None else part + term
                    return part
                accumulate(product)

            if a2 is None:
                contract(a_ref)
            else:
                pl.when(kk < halfk)(lambda: contract(a_ref))
                pl.when(kk >= halfk)(lambda: contract(x_ref))
        elif b2 is not None:
            pl.when(j < halfj)(lambda: accumulate(lambda: _dot(a_ref[...], b_ref[...], dims)))
            pl.when(j >= halfj)(lambda: accumulate(lambda: _dot(a_ref[...], x_ref[...], dims)))
        else:
            accumulate(lambda: _dot(a_ref[...], b_ref[...], dims))

        if nk > 1:
            @pl.when(kk == nk - 1)
            def _():
                o_ref[...] = acc_ref[...].astype(o_ref.dtype)

    ins, in_specs = [a], []
    if mode == "tn":
        in_specs.append(pl.BlockSpec((tk, tm), lambda i, j, kk: (kk, i)))
    elif a2 is not None:
        in_specs.append(pl.BlockSpec((tm, tk), lambda i, j, kk: (i, jnp.minimum(kk, halfk - 1))))
    else:
        in_specs.append(pl.BlockSpec((tm, tk), lambda i, j, kk: (i, kk)))
    ins.append(b)
    if sharded and mode == "nn":
        in_specs.append(pl.BlockSpec((gs, tk, cs), lambda i, j, kk: (j, kk, 0)))
    elif sharded:
        in_specs.append(pl.BlockSpec((gs, tn, cs), lambda i, j, kk: (kk, j, 0)))
    elif mode == "nt":
        in_specs.append(pl.BlockSpec((tn, tk), lambda i, j, kk: (j, kk)))
    elif b2 is not None:
        in_specs.append(pl.BlockSpec((tk, tn), lambda i, j, kk: (kk, jnp.minimum(j, halfj - 1))))
    else:
        in_specs.append(pl.BlockSpec((tk, tn), lambda i, j, kk: (kk, j)))
    if a2 is not None:
        ins.append(a2)
        in_specs.append(pl.BlockSpec((tm, tk), lambda i, j, kk: (i, jnp.maximum(kk - halfk, 0))))
    if b2 is not None:
        ins.append(b2)
        in_specs.append(pl.BlockSpec((tk, tn), lambda i, j, kk: (kk, jnp.maximum(j - halfj, 0))))
    if dep is not None:
        ins.append(dep)
        in_specs.append(ANY)
    return _call(body, name, _sds((m, n), out_dtype), grid=(m // tm, nj, nk),
                 in_specs=in_specs, out_specs=pl.BlockSpec((tm, tn), lambda i, j, kk: (i, j)),
                 scratch=[pltpu.VMEM((tm, tn), F32)] if nk > 1 else [],
                 sem=("parallel", "parallel", "arbitrary"))(*ins)


def _ada_fwd(c_all, w_sh, b_sh):
    n = w_sh.shape[1]
    tn = 512

    def body(c_ref, w_ref, b_ref, o_ref):
        ca = _silu(c_ref[...]).astype(BF16)
        o_ref[...] = _dot(ca, w_ref[...].astype(BF16), NN) + b_ref[...]

    return _call(body, "ada_fwd", _sds((NDEV, n)), grid=(n // tn,),
                 in_specs=[pl.BlockSpec((NDEV, D), lambda j: (0, 0)), pl.BlockSpec((D, tn), lambda j: (0, j)),
                           pl.BlockSpec((1, tn), lambda j: (0, j))],
                 out_specs=pl.BlockSpec((NDEV, tn), lambda j: (0, j)), sem=("parallel",))(c_all, w_sh, b_sh)


def _adam(w, g, m, v):
    m = ADAM_B1 * m + (1.0 - ADAM_B1) * g
    v = ADAM_B2 * v + (1.0 - ADAM_B2) * (g * g)
    m_hat = m / (1.0 - ADAM_B1 ** ADAM_STEP)
    v_hat = v / (1.0 - ADAM_B2 ** ADAM_STEP)
    delta = -ADAM_LR * (m_hat / (jnp.sqrt(v_hat) + ADAM_EPS) + ADAM_WD * w)
    return delta, m, v


def _ada_bwd_adam(c_all, dmod_sel, w, m, v):
    r, n = w.shape
    tr = 256

    def body(c_ref, d_ref, w_ref, m_ref, v_ref, g_ref, dl_ref, nm_ref, nv_ref):
        ca = _silu(c_ref[...])
        g = _dot(ca, d_ref[...], TN, precision=HI)
        dl, nm, nv = _adam(w_ref[...], g, m_ref[...], v_ref[...])
        g_ref[...] = g
        dl_ref[...] = dl
        nm_ref[...] = nm
        nv_ref[...] = nv

    big = pl.BlockSpec((tr, n), lambda i: (i, 0))
    return _call(body, "ada_bwd_adam", [_sds((r, n))] * 4, grid=(r // tr,),
                 in_specs=[pl.BlockSpec((NDEV, tr), lambda i: (0, i)), pl.BlockSpec((NDEV, n), lambda i: (0, 0)),
                           big, big, big],
                 out_specs=[big] * 4, sem=("parallel",))(c_all, dmod_sel, w, m, v)


def _row_spec(width=D):
    return pl.BlockSpec((TS, width), lambda i: (i, 0))


def _vec_spec(width=D):
    return pl.BlockSpec((1, width), lambda i: (0, 0))


def _acc_spec(width=D):
    return pl.BlockSpec((8, width), lambda i: (0, 0))


def _norm_mod(x, g, sc, sh, name):
    def body(x_ref, g_ref, sc_ref, sh_ref, o_ref):
        xv = x_ref[...]
        r = lax.rsqrt(jnp.mean(xv * xv, axis=-1, keepdims=True) + EPS)
        o_ref[...] = ((xv * r) * g_ref[...] * (1.0 + sc_ref[...]) + sh_ref[...]).astype(BF16)

    return _call(body, name, _sds((S, D), BF16), grid=(S // TS,),
                 in_specs=[_row_spec(), _vec_spec(), _vec_spec(), _vec_spec()], out_specs=_row_spec(),
                 sem=("parallel",))(x, g, sc, sh)


def _resid_norm_mod(x, mix, gt, g, sc, sh, name):
    def body(x_ref, mix_ref, gt_ref, g_ref, sc_ref, sh_ref, x2_ref, o_ref):
        xv = x_ref[...] + gt_ref[...] * mix_ref[...]
        x2_ref[...] = xv
        r = lax.rsqrt(jnp.mean(xv * xv, axis=-1, keepdims=True) + EPS)
        o_ref[...] = ((xv * r) * g_ref[...] * (1.0 + sc_ref[...]) + sh_ref[...]).astype(BF16)

    return _call(body, name, [_sds((S, D)), _sds((S, D), BF16)], grid=(S // TS,),
                 in_specs=[_row_spec(), _row_spec()] + [_vec_spec()] * 4, out_specs=[_row_spec(), _row_spec()],
                 sem=("parallel",))(x, mix, gt, g, sc, sh)


def _acc_rows(acc_ref, rows):
    @pl.when(pl.program_id(0) == 0)
    def _():
        acc_ref[...] = jnp.zeros_like(acc_ref)

    for k, row in enumerate(rows):
        acc_ref[k:k + 1, :] += row


def _loss_head(x2, f, tgt, gt2, gf):
    def body(x2_ref, f_ref, t_ref, gt_ref, gf_ref, dx_ref, df_ref, acc_ref):
        fv = f_ref[...]
        x3 = x2_ref[...] + gt_ref[...] * fv
        r = lax.rsqrt(jnp.mean(x3 * x3, axis=-1, keepdims=True) + EPS)
        xn = x3 * r
        e = xn * gf_ref[...] - t_ref[...]
        loss = 0.5 * jnp.sum(jnp.mean(e * e, axis=-1, keepdims=True), axis=0, keepdims=True)
        dy = e * (1.0 / D)
        dxn = dy * gf_ref[...]
        dx3 = r * (dxn - xn * jnp.mean(dxn * xn, axis=-1, keepdims=True))
        dx_ref[...] = dx3
        df_ref[...] = (dx3 * gt_ref[...]).astype(BF16)
        _acc_rows(acc_ref, [jnp.sum(dy * xn, axis=0, keepdims=True), jnp.sum(dx3 * fv, axis=0, keepdims=True),
                            jnp.broadcast_to(loss, (1, D))])

    return _call(body, "loss_head", [_sds((S, D)), _sds((S, D), BF16), _sds((8, D))], grid=(S // TS,),
                 in_specs=[_row_spec(), _row_spec(), _row_spec(), _vec_spec(), _vec_spec()],
                 out_specs=[_row_spec(), _row_spec(), _acc_spec()], sem=("arbitrary",))(x2, f, tgt, gt2, gf)


def _norm_mod_bwd(dhn, x, dres, g, sc, name, mix=None, gt=None):
    gated = mix is not None

    def body(*refs):
        if gated:
            dhn_ref, x_ref, dres_ref, g_ref, sc_ref, mix_ref, gt_ref, dx_ref, dmix_ref, acc_ref = refs
        else:
            dhn_ref, x_ref, dres_ref, g_ref, sc_ref, dx_ref, acc_ref = refs
        xv = x_ref[...]
        dh = dhn_ref[...]
        r = lax.rsqrt(jnp.mean(xv * xv, axis=-1, keepdims=True) + EPS)
        xn = xv * r
        gv = g_ref[...]
        sc1 = 1.0 + sc_ref[...]
        dxn = dh * gv * sc1
        dx = dres_ref[...] + r * (dxn - xn * jnp.mean(dxn * xn, axis=-1, keepdims=True))
        dx_ref[...] = dx
        rows = [jnp.sum(dh, axis=0, keepdims=True), jnp.sum(dh * xn * gv, axis=0, keepdims=True),
                jnp.sum(dh * xn * sc1, axis=0, keepdims=True)]
        if gated:
            rows.append(jnp.sum(dx * mix_ref[...], axis=0, keepdims=True))
            dmix_ref[...] = (dx * gt_ref[...]).astype(BF16)
        _acc_rows(acc_ref, rows)

    ins = [dhn, x, dres, g, sc]
    in_specs = [_row_spec(), _row_spec(), _row_spec(), _vec_spec(), _vec_spec()]
    outs = [_sds((S, D))]
    out_specs = [_row_spec()]
    if gated:
        ins += [mix, gt]
        in_specs += [_row_spec(), _vec_spec()]
        outs.append(_sds((S, D), BF16))
        out_specs.append(_row_spec())
    outs.append(_sds((8, D)))
    out_specs.append(_acc_spec())
    return _call(body, name, outs, grid=(S // TS,), in_specs=in_specs, out_specs=out_specs,
                 sem=("arbitrary",))(*ins)


RC = 256


def _conv_fwd_rows(pad_ref, w_ref, kw, head, r0):
    acc = None
    for k in range(kw):
        term = w_ref[k:k + 1, :] * pad_ref[pl.ds(head - (kw - 1) + k + r0, RC), :]
        acc = term if acc is None else acc + term
    return acc


def _conv_bwd_rows(pad2_ref, w_ref, kw, r0):
    acc = None
    for k in range(kw):
        term = w_ref[k:k + 1, :] * pad2_ref[pl.ds(kw - 1 - k + r0, RC), :]
        acc = term if acc is None else acc + term
    return acc


def _conv_dw(pad_ref, dout_ref, dw_ref, kw, head):
    for k in range(kw):
        acc = None
        for r0 in range(0, S, RC):
            term = jnp.sum(pad_ref[pl.ds(head - (kw - 1) + k + r0, RC), :] * dout_ref[pl.ds(r0, RC), :],
                           axis=0, keepdims=True)
            acc = term if acc is None else acc + term
        dw_ref[k:k + 1, :] = acc


def _col_spec(width, off_blocks=0):
    return pl.BlockSpec((S, width), lambda j: (0, j + off_blocks))


def _dn_pre_fwd(proj, conv_w):
    head = 8

    def body(x_ref, w_ref, o_ref, pad_ref):
        j = pl.program_id(0)
        pad_ref[pl.ds(0, head), :] = jnp.zeros((head, DH), F32)
        pad_ref[pl.ds(head, S), :] = x_ref[...]
        scale = jnp.where(j < H, DH ** -0.5, 1.0)
        for r0 in range(0, S, RC):
            y = _silu(_conv_fwd_rows(pad_ref, w_ref, DNK, head, r0))
            rinv = lax.rsqrt(jnp.sum(y * y, axis=-1, keepdims=True) + EPS)
            o_ref[pl.ds(r0, RC), :] = jnp.where(j < 2 * H, y * rinv * scale, y)

    return _call(body, "dn_pre_fwd", _sds((S, 3 * DNW)), grid=(3 * H,),
                 in_specs=[_col_spec(DH), pl.BlockSpec((DNK, DH), lambda j: (0, j))], out_specs=_col_spec(DH),
                 scratch=[pltpu.VMEM((S + head, DH), F32)], sem=("parallel",))(proj, conv_w)


def _dn_pre_bwd(dq, dk, dv, proj, conv_w, dproj):
    head = 8

    def body(dq_ref, dk_ref, dv_ref, x_ref, w_ref, dproj_in, dx_ref, dw_ref, pad_ref, pad2_ref):
        j = pl.program_id(0)
        pad_ref[pl.ds(0, head), :] = jnp.zeros((head, DH), F32)
        pad_ref[pl.ds(head, S), :] = x_ref[...]
        pad2_ref[pl.ds(S, head), :] = jnp.zeros((head, DH), F32)
        scale = jnp.where(j < H, DH ** -0.5, 1.0)
        for r0 in range(0, S, RC):
            xc = _conv_fwd_rows(pad_ref, w_ref, DNK, head, r0)
            y = _silu(xc)
            rinv = lax.rsqrt(jnp.sum(y * y, axis=-1, keepdims=True) + EPS)
            yn = y * rinv
            rows = pl.ds(r0, RC)
            do = jnp.where(j < H, dq_ref[rows, :], jnp.where(j < 2 * H, dk_ref[rows, :], dv_ref[rows, :]))
            dy_n = scale * rinv * (do - yn * jnp.sum(do * yn, axis=-1, keepdims=True))
            dy = jnp.where(j < 2 * H, dy_n, do)
            pad2_ref[rows, :] = dy * _dsilu(xc)
        for r0 in range(0, S, RC):
            dx_ref[pl.ds(r0, RC), :] = _conv_bwd_rows(pad2_ref, w_ref, DNK, r0).astype(BF16)
        _conv_dw(pad_ref, pad2_ref, dw_ref, DNK, head)

    wspec = pl.BlockSpec((DNK, DH), lambda j: (0, j))
    head_col = lambda lo: pl.BlockSpec((S, DH), lambda j: (0, jnp.clip(j - lo, 0, H - 1)))
    return _call(body, "dn_pre_bwd", [_sds((S, NINP), BF16), _sds((DNK, 3 * DNW))], grid=(3 * H,),
                 in_specs=[head_col(0), head_col(H), head_col(2 * H), _col_spec(DH), wspec, ANY],
                 out_specs=[_col_spec(DH), wspec],
                 scratch=[pltpu.VMEM((S + head, DH), F32), pltpu.VMEM((S + head, DH), F32)],
                 sem=("parallel",), aliases={5: 0})(dq, dk, dv, proj, conv_w, dproj)


CF_HEAD = 32
CF_VAL = pl.BlockSpec((S, LANE), lambda j: (0, O_GLU // LANE + 2 * j))
CF_GL = pl.BlockSpec((S, LANE), lambda j: (0, O_GLU // LANE + 2 * j + 1))


def _cf_conv_fwd(proj, conv_w):
    def body(val_ref, gl_ref, w_ref, o_ref, pad_ref):
        pad_ref[pl.ds(0, CF_HEAD), :] = jnp.zeros((CF_HEAD, LANE), F32)
        pad_ref[pl.ds(CF_HEAD, S), :] = val_ref[...] * _sigmoid(gl_ref[...])
        for r0 in range(0, S, RC):
            o_ref[pl.ds(r0, RC), :] = _conv_fwd_rows(pad_ref, w_ref, CFK, CF_HEAD, r0)

    wspec = pl.BlockSpec((CFK, LANE), lambda j: (0, j))
    return _call(body, "cf_conv_fwd", _sds((S, CFW)), grid=(CFW // LANE,),
                 in_specs=[CF_VAL, CF_GL, wspec], out_specs=_col_spec(LANE),
                 scratch=[pltpu.VMEM((S + CF_HEAD, LANE), F32)], sem=("parallel",))(proj, proj, conv_w)


def _cf_conv_bwd(du1, proj, conv_w, dproj):
    def body(d_ref, val_ref, gl_ref, w_ref, dproj_in, dp_ref, dw_ref, pad_ref, pad2_ref):
        sg = _sigmoid(gl_ref[...])
        pad_ref[pl.ds(0, CF_HEAD), :] = jnp.zeros((CF_HEAD, LANE), F32)
        pad_ref[pl.ds(CF_HEAD, S), :] = val_ref[...] * sg
        pad2_ref[pl.ds(0, S), :] = d_ref[...]
        pad2_ref[pl.ds(S, CF_HEAD), :] = jnp.zeros((CF_HEAD, LANE), F32)
        for r0 in range(0, S, RC):
            du0 = _conv_bwd_rows(pad2_ref, w_ref, CFK, r0)
            rows = pl.ds(r0, RC)
            sgr = _sigmoid(gl_ref[rows, :])
            dp_ref[rows, 0:LANE] = (du0 * sgr).astype(BF16)
            dp_ref[rows, LANE:2 * LANE] = (du0 * val_ref[rows, :] * sgr * (1.0 - sgr)).astype(BF16)
        _conv_dw(pad_ref, pad2_ref, dw_ref, CFK, CF_HEAD)

    wspec = pl.BlockSpec((CFK, LANE), lambda j: (0, j))
    return _call(body, "cf_conv_bwd", [_sds((S, NINP), BF16), _sds((CFK, CFW))], grid=(CFW // LANE,),
                 in_specs=[_col_spec(LANE), CF_VAL, CF_GL, wspec, ANY],
                 out_specs=[pl.BlockSpec((S, 2 * LANE), lambda j: (0, O_GLU // (2 * LANE) + j)), wspec],
                 scratch=[pltpu.VMEM((S + CF_HEAD, LANE), F32), pltpu.VMEM((S + CF_HEAD, LANE), F32)],
                 sem=("parallel",), aliases={4: 0})(du1, proj, proj, conv_w, dproj)


def _cf_ln_fwd(u1, g, b):
    def body(u_ref, g_ref, b_ref, o_ref):
        u = u_ref[...]
        mu = jnp.mean(u, axis=-1, keepdims=True)
        xc = u - mu
        y = xc * lax.rsqrt(jnp.mean(xc * xc, axis=-1, keepdims=True) + EPS)
        o_ref[...] = _silu(y * g_ref[...] + b_ref[...]).astype(BF16)

    return _call(body, "cf_ln_fwd", _sds((S, CFW), BF16), grid=(S // TS,),
                 in_specs=[_row_spec(CFW), _vec_spec(CFW), _vec_spec(CFW)], out_specs=_row_spec(CFW),
                 sem=("parallel",))(u1, g, b)


def _cf_ln_bwd(du3, u1, g, b):
    def body(d_ref, u_ref, g_ref, b_ref, du_ref, acc_ref):
        u = u_ref[...]
        mu = jnp.mean(u, axis=-1, keepdims=True)
        xc = u - mu
        rstd = lax.rsqrt(jnp.mean(xc * xc, axis=-1, keepdims=True) + EPS)
        xh = xc * rstd
        du2 = d_ref[...] * _dsilu(xh * g_ref[...] + b_ref[...])
        dxh = du2 * g_ref[...]
        du_ref[...] = rstd * (dxh - jnp.mean(dxh, axis=-1, keepdims=True)
                              - xh * jnp.mean(dxh * xh, axis=-1, keepdims=True))
        _acc_rows(acc_ref, [jnp.sum(du2 * xh, axis=0, keepdims=True), jnp.sum(du2, axis=0, keepdims=True)])

    return _call(body, "cf_ln_bwd", [_sds((S, CFW)), _sds((8, CFW))], grid=(S // TS,),
                 in_specs=[_row_spec(CFW), _row_spec(CFW), _vec_spec(CFW), _vec_spec(CFW)],
                 out_specs=[_row_spec(CFW), _acc_spec(CFW)], sem=("arbitrary",))(du3, u1, g, b)


FB = 256
FNB = FFN // FB
FF_HEAD = 8


def _ffn_mid_fwd(upall, conv_w):
    def body(gate_ref, up_ref, w_ref, o_ref, pad_ref):
        pad_ref[pl.ds(0, FF_HEAD), :] = jnp.zeros((FF_HEAD, FB), F32)
        pad_ref[pl.ds(FF_HEAD, S), :] = gate_ref[...]
        for r0 in range(0, S, RC):
            gc = _conv_fwd_rows(pad_ref, w_ref, FFK, FF_HEAD, r0)
            o_ref[pl.ds(r0, RC), :] = (_silu(gc) * up_ref[pl.ds(r0, RC), :]).astype(BF16)

    wspec = pl.BlockSpec((FFK, FB), lambda j: (0, j))
    return _call(body, "ffn_mid_fwd", _sds((S, FFN), BF16), grid=(FNB,),
                 in_specs=[_col_spec(FB), _col_spec(FB, FNB), wspec], out_specs=_col_spec(FB),
                 scratch=[pltpu.VMEM((S + FF_HEAD, FB), F32)], sem=("parallel",))(upall, upall, conv_w)


def _ffn_mid_bwd(dh, upall, conv_w):
    def body(d_ref, gate_ref, up_ref, w_ref, dgate_ref, dup_ref, dw_ref, pad_ref, pad2_ref):
        pad_ref[pl.ds(0, FF_HEAD), :] = jnp.zeros((FF_HEAD, FB), F32)
        pad_ref[pl.ds(FF_HEAD, S), :] = gate_ref[...]
        pad2_ref[pl.ds(S, FF_HEAD), :] = jnp.zeros((FF_HEAD, FB), F32)
        for r0 in range(0, S, RC):
            rows = pl.ds(r0, RC)
            gc = _conv_fwd_rows(pad_ref, w_ref, FFK, FF_HEAD, r0)
            dhv = d_ref[rows, :]
            dup_ref[rows, :] = (dhv * _silu(gc)).astype(BF16)
            pad2_ref[rows, :] = dhv * up_ref[rows, :] * _dsilu(gc)
        for r0 in range(0, S, RC):
            dgate_ref[pl.ds(r0, RC), :] = _conv_bwd_rows(pad2_ref, w_ref, FFK, r0).astype(BF16)
        _conv_dw(pad_ref, pad2_ref, dw_ref, FFK, FF_HEAD)

    wspec = pl.BlockSpec((FFK, FB), lambda j: (0, j))
    return _call(body, "ffn_mid_bwd", [_sds((S, FFN), BF16), _sds((S, FFN), BF16), _sds((FFK, FFN))],
                 grid=(FNB,), in_specs=[_col_spec(FB), _col_spec(FB), _col_spec(FB, FNB), wspec],
                 out_specs=[_col_spec(FB), _col_spec(FB), wspec],
                 scratch=[pltpu.VMEM((S + FF_HEAD, FB), F32), pltpu.VMEM((S + FF_HEAD, FB), F32)],
                 sem=("parallel",))(dh, upall, upall, conv_w)


GT = 256
SM_BLK = O_SM // LANE


def _chunk_tri(lower):
    r = lax.broadcasted_iota(jnp.int32, (GT, GT), 0)
    c = lax.broadcasted_iota(jnp.int32, (GT, GT), 1)
    same = (r // CH) == (c // CH)
    tri = (c <= r) if lower else (c >= r)
    return jnp.where(same & tri, 1.0, 0.0).astype(F32)


def _gates_fwd(proj, alog_v, dtb_v):
    def body(sm_ref, al_ref, dt_ref, o_ref):
        lane = lax.broadcasted_iota(jnp.int32, (GT, LANE), 1)
        tri = _chunk_tri(True)
        na = -jnp.exp(al_ref[...])
        for r0 in range(0, S, GT):
            sm = sm_ref[pl.ds(r0, GT), :]
            raw = jnp.where((lane >= H) & (lane < 2 * H), na * _softplus(sm + dt_ref[...]), 0.0)
            gc = _dot(tri, raw, NN, precision=HI)
            o_ref[pl.ds(r0, GT), :] = jnp.where(lane < H, _sigmoid(sm), gc)

    return _call(body, "gates_fwd", _sds((S, LANE)), grid=(1,),
                 in_specs=[pl.BlockSpec((S, LANE), lambda i: (0, SM_BLK)), _vec_spec(LANE), _vec_spec(LANE)],
                 out_specs=pl.BlockSpec((S, LANE), lambda i: (0, 0)), sem=("arbitrary",))(proj, alog_v, dtb_v)


def _gates_bwd(dgb, proj, alog_v, dtb_v, dproj):
    def body(d_ref, sm_ref, al_ref, dt_ref, dproj_in, o_ref, acc_ref):
        lane = lax.broadcasted_iota(jnp.int32, (GT, LANE), 1)
        is_g = (lane >= H) & (lane < 2 * H)
        tri = _chunk_tri(False)
        na = -jnp.exp(al_ref[...])
        d_al = jnp.zeros((1, LANE), F32)
        d_dt = jnp.zeros((1, LANE), F32)
        for r0 in range(0, S, GT):
            sm = sm_ref[pl.ds(r0, GT), :]
            dv = d_ref[pl.ds(r0, GT), :]
            z = sm + dt_ref[...]
            draw = _dot(tri, jnp.where(is_g, dv, 0.0), NN, precision=HI)
            dlogit = jnp.where(is_g, draw * na * _sigmoid(z), 0.0)
            d_al = d_al + jnp.sum(jnp.where(is_g, draw * na * _softplus(z), 0.0), axis=0, keepdims=True)
            d_dt = d_dt + jnp.sum(dlogit, axis=0, keepdims=True)
            bt = _sigmoid(sm)
            o_ref[pl.ds(r0, GT), :] = jnp.where(lane < H, dv * bt * (1.0 - bt), dlogit).astype(BF16)
        acc_ref[...] = jnp.zeros_like(acc_ref)
        acc_ref[0:1, :] = d_al
        acc_ref[1:2, :] = d_dt

    return _call(body, "gates_bwd", [_sds((S, NINP), BF16), _sds((8, LANE))], grid=(1,),
                 in_specs=[pl.BlockSpec((S, LANE), lambda i: (0, 0)), pl.BlockSpec((S, LANE), lambda i: (0, SM_BLK)),
                           _vec_spec(LANE), _vec_spec(LANE), ANY],
                 out_specs=[pl.BlockSpec((S, LANE), lambda i: (0, SM_BLK)), _acc_spec(LANE)],
                 sem=("arbitrary",), aliases={4: 0})(dgb, proj, alog_v, dtb_v, dproj)


HB = 2


def _each(fn, *lists):
    return [fn(*args) for args in zip(*lists)]


def _neumann_inv(a, eye):
    p = _each(lambda m: -m, a)
    t = _each(lambda m: eye + m, p)
    for _ in range(5):
        p = _each(lambda m: _dot(m, m, NN, precision=HI), p)
        t = _each(lambda tt, pp: tt + _dot(tt, pp, NN, precision=HI), t, p)
    return t


def _head_specs():
    q = pl.BlockSpec((S, HB * DH), lambda h: (0, h))
    k = pl.BlockSpec((S, HB * DH), lambda h: (0, H // HB + h))
    v = pl.BlockSpec((S, HB * DH), lambda h: (0, 2 * H // HB + h))
    gb = pl.BlockSpec((HB, S, DH), lambda h: (h, 0, 0))
    gr = pl.BlockSpec((HB, NCH, CH), lambda h: (h, 0, 0))
    return q, k, v, gb, gr


ST_SPEC = pl.BlockSpec((HB, NCH, DH, DH), lambda h: (h, 0, 0, 0))
TM_SPEC = pl.BlockSpec((HB, NCH, CH, CH), lambda h: (h, 0, 0, 0))
HCOL = pl.BlockSpec((S, HB * DH), lambda h: (0, h))


def _delta_fwd(qkvn, gb, gr, bb):
    def body(q_ref, k_ref, v_ref, gb_ref, gr_ref, bb_ref, o_ref, st_ref, tm_ref):
        ri = lax.broadcasted_iota(jnp.int32, (CH, CH), 0)
        ci = lax.broadcasted_iota(jnp.int32, (CH, CH), 1)
        strict = ri > ci
        causal = ri >= ci
        eye = jnp.where(ri == ci, 1.0, 0.0).astype(F32)

        hs = list(range(HB))
        cols = [slice(hh * DH, (hh + 1) * DH) for hh in hs]
        bf = lambda m: m.astype(BF16)

        def local(n):
            rows = pl.ds(pl.multiple_of(n * CH, CH), CH)
            c = dict(rows=rows, n=n)
            c["q"] = [q_ref[rows, cc] for cc in cols]
            c["k"] = [k_ref[rows, cc] for cc in cols]
            c["v"] = [v_ref[rows, cc] for cc in cols]
            c["g"] = [gb_ref[hh, rows, :] for hh in hs]
            c["beta"] = [bb_ref[hh, rows, :] for hh in hs]
            diff = [c["g"][hh][:, :CH] - gr_ref[hh, pl.ds(n, 1), :] for hh in hs]
            c["el"] = _each(lambda d: jnp.exp(jnp.where(causal, d, 0.0)), diff)
            c["eg"] = _each(jnp.exp, c["g"])
            c["gl"] = _each(lambda m: m[CH - 1:CH, :], c["g"])
            c["kb"] = _each(lambda x, y: x * y, c["k"], c["beta"])
            c["kbf"] = _each(bf, c["k"])
            c["a"] = _each(lambda x, y, e: jnp.where(strict, _dot(bf(x), y, NT) * e, 0.0), c["kb"], c["kbf"], c["el"])
            return c

        def advance(c, t, sts):
            n, rows = c["n"], c["rows"]
            for hh in hs:
                tm_ref[hh, n] = t[hh]
                st_ref[hh, n] = sts[hh]
            sb = _each(bf, sts)
            r = _each(lambda vv, bb_, kk, ee, ss: vv * bb_ - _dot(bf(kk * ee), ss, NN), c["v"], c["beta"], c["kb"], c["eg"], sb)
            ub = _each(lambda tt, rr: bf(_dot(tt, rr, NN, precision=HI)), t, r)
            p = _each(lambda qq, kk, e: jnp.where(causal, _dot(bf(qq), kk, NT) * e, 0.0), c["q"], c["kbf"], c["el"])
            o = _each(lambda qq, ee, ss, pp, uu: _dot(bf(qq * ee), ss, NN) + _dot(bf(pp), uu, NN), c["q"], c["eg"], sb, p, ub)
            for hh in hs:
                o_ref[rows, cols[hh]] = o[hh]
            kd = _each(lambda kk, l, gg: kk * jnp.exp(l - gg), c["k"], c["gl"], c["g"])
            return _each(lambda st, l, kk, uu: st * jnp.exp(l) + _dot(bf(kk), uu, TN), sts, c["gl"], kd, ub)

        def step(i, sts):
            c0, c1 = local(2 * i), local(2 * i + 1)
            t = _neumann_inv(c0["a"] + c1["a"], eye)
            sts = advance(c0, t[:HB], list(sts))
            return tuple(advance(c1, t[HB:], sts))

        lax.fori_loop(0, NCH // 2, step, tuple(jnp.zeros((DH, DH), F32) for _ in hs))

    q, k, v, gbs, grs = _head_specs()
    return _call(body, "delta_fwd", [_sds((S, DNW)), _sds((H, NCH, DH, DH)), _sds((H, NCH, CH, CH))], grid=(H // HB,),
                 in_specs=[q, k, v, gbs, grs, gbs], out_specs=[HCOL, ST_SPEC, TM_SPEC],
                 sem=("parallel",))(qkvn, qkvn, qkvn, gb, gr, bb)


def _delta_bwd(qkvn, gb, gr, bb, st_all, tm_all, do_all):
    def body(q_ref, k_ref, v_ref, gb_ref, gr_ref, bb_ref, st_ref, tm_ref, do_ref,
             dq_ref, dk_ref, dv_ref, dg_ref, db_ref):
        ri = lax.broadcasted_iota(jnp.int32, (CH, CH), 0)
        ci = lax.broadcasted_iota(jnp.int32, (CH, CH), 1)
        lo_s, lo_c, up_s, up_c = ri > ci, ri >= ci, ri < ci, ri <= ci
        last_row = lax.broadcasted_iota(jnp.int32, (CH, 1), 0) == CH - 1

        def rs(mat):
            return jnp.sum(mat, axis=1, keepdims=True)

        def total(mat):
            return jnp.sum(rs(mat), axis=0, keepdims=True)

        hs = list(range(HB))
        cols = [slice(hh * DH, (hh + 1) * DH) for hh in hs]
        bf = lambda m: m.astype(BF16)
        mul = lambda x, y: x * y
        spread = jnp.full((8, DH), 1.0 / DH, F32)

        def as_row(col):
            return _dot(spread, jnp.broadcast_to(col, (CH, DH)), NT, precision=HI)[0:1, :]

        def step(i, dss):
            ns = [NCH - 1 - 2 * i, NCH - 2 - 2 * i]
            rws = [pl.ds(pl.multiple_of(n * CH, CH), CH) for n in ns]
            idx = [(cc, hh) for cc in range(2) for hh in hs]
            q = [q_ref[rws[cc], cols[hh]] for cc, hh in idx]
            k = [k_ref[rws[cc], cols[hh]] for cc, hh in idx]
            v = [v_ref[rws[cc], cols[hh]] for cc, hh in idx]
            do = [do_ref[rws[cc], cols[hh]] for cc, hh in idx]
            g = [gb_ref[hh, rws[cc], :] for cc, hh in idx]
            beta = [bb_ref[hh, rws[cc], :] for cc, hh in idx]
            t = [tm_ref[hh, ns[cc]] for cc, hh in idx]
            st = [st_ref[hh, ns[cc]] for cc, hh in idx]
            diff = [gg[:, :CH] - gr_ref[hh, pl.ds(ns[cc], 1), :] for gg, (cc, hh) in zip(g, idx)]
            el = _each(lambda d: jnp.exp(jnp.where(lo_c, d, 0.0)), diff)
            eu = _each(lambda d: jnp.exp(jnp.where(up_c, -d, 0.0)), diff)
            eg = _each(jnp.exp, g)
            gl = _each(lambda m: m[CH - 1:CH, :], g)
            egl = _each(jnp.exp, gl)
            ekd = _each(lambda l, m: jnp.exp(l - m), gl, g)
            kb = _each(mul, k, beta)
            kbg = _each(mul, kb, eg)
            qg = _each(mul, q, eg)
            kd = _each(mul, k, ekd)
            qb, kbf, kbb = _each(bf, q), _each(bf, k), _each(bf, kb)
            kbgb, qgb, kdb = _each(bf, kbg), _each(bf, qg), _each(bf, kd)
            sb, dob = _each(bf, st), _each(bf, do)
            r = _each(lambda vv, b, x, s: vv * b - _dot(x, s, NN), v, beta, kbgb, sb)
            u = _each(lambda tt, rr: _dot(tt, rr, NN, precision=HI), t, r)
            ub = _each(bf, u)
            kk = _each(lambda x, y: _dot(x, y, NT), kbb, kbf)
            qk = _each(lambda x, y: _dot(x, y, NT), qb, kbf)
            kkt = _each(lambda x, y: _dot(x, y, NT), kbf, kbb)
            qkt = _each(lambda x, y: _dot(x, y, NT), kbf, qb)
            pt = _each(lambda m, e: jnp.where(up_c, m * e, 0.0), qkt, eu)
            ds, du, dr, drb, ds_new = [], [], [], [], list(dss)
            for cc in range(2):
                sl = slice(cc * HB, (cc + 1) * HB)
                ds_c = ds_new
                dsb_c = _each(bf, ds_c)
                du_c = _each(lambda p, d, x, s: _dot(bf(p), d, NN) + _dot(x, s, NN), pt[sl], dob[sl], kdb[sl], dsb_c)
                dr_c = _each(lambda tt, d: _dot(tt, d, TN, precision=HI), t[sl], du_c)
                drb_c = _each(bf, dr_c)
                ds_new = _each(lambda x, d, e, s, y, z: _dot(x, d, TN) + e * s - _dot(y, z, TN),
                               qgb[sl], dob[sl], egl[sl], ds_c, kbgb[sl], drb_c)
                ds, du, dr, drb = ds + ds_c, du + du_c, dr + dr_c, drb + drb_c
            dsb = _each(bf, ds)
            dpg = _each(lambda d, uu, e: jnp.where(lo_c, _dot(d, uu, NT), 0.0) * e, dob, ub, el)
            dpgt = _each(lambda uu, d, e: jnp.where(up_c, _dot(uu, d, NT), 0.0) * e, ub, dob, eu)
            dag = _each(lambda d, uu, e: -jnp.where(lo_s, _dot(d, uu, NT), 0.0) * e, drb, ub, el)
            dagt = _each(lambda uu, d, e: -jnp.where(up_s, _dot(uu, d, NT), 0.0) * e, ub, drb, eu)
            dqg = _each(lambda d, s: _dot(d, s, NT), dob, sb)
            dkbg = _each(lambda d, s: -_dot(d, s, NT), drb, sb)
            dkd = _each(lambda uu, s: _dot(uu, s, NT), ub, dsb)
            dkb =_each(lambda a, x, y, e: _dot(bf(a), x, NN) + y * e, dag, kbf, dkbg, eg)
            dk = _each(lambda a, x, p, y, z, e, w, b: _dot(bf(a), x, NN) + _dot(bf(p), y, NN) + z * e + w * b,
                       dagt, kbb, dpgt, qb, dkd, ekd, dkb, beta)
            dq = _each(lambda p, x, y, e: _dot(bf(p), x, NN) + y * e, dpg, kbf, dqg, eg)
            dkd_kd = _each(lambda x, y: rs(x * y), dkd, kd)
            dg = _each(lambda a, x, p, y, at, xt, pt_, yt, z, w, c, d, e:
                       rs(a * x + p * y) - rs(at * xt + pt_ * yt) + rs(z * w) + rs(c * d) - e,
                       dag, kk, dpg, qk, dagt, kkt, dpgt, qkt, dqg, qg, dkbg, kbg, dkd_kd)
            dgl = _each(lambda x, e, s, y: jnp.sum(x, axis=0, keepdims=True) + e[:, 0:1] * total(s * y), dkd_kd, egl, ds, st)
            dg = _each(lambda x, y: x + jnp.where(last_row, y, 0.0), dg, dgl)
            dbeta = _each(lambda x, y, z, w: rs(x * y) + rs(z * w), dkb, k, dr, v)
            for j, (cc, hh) in enumerate(idx):
                dq_ref[rws[cc], cols[hh]] = dq[j]
                dk_ref[rws[cc], cols[hh]] = dk[j]
                dv_ref[rws[cc], cols[hh]] = dr[j] * beta[j]
                dg_ref[hh, pl.ds(ns[cc], 1), :] = as_row(dg[j])
                db_ref[hh, pl.ds(ns[cc], 1), :] = as_row(dbeta[j])
            return tuple(ds_new)

        lax.fori_loop(0, NCH // 2, step, tuple(jnp.zeros((DH, DH), F32) for _ in hs))

    q, k, v, gbs, grs = _head_specs()
    return _call(body, "delta_bwd",
                 [_sds((S, DNW)), _sds((S, DNW)), _sds((S, DNW)), _sds((H, NCH, CH)), _sds((H, NCH, CH))], grid=(H // HB,),
                 in_specs=[q, k, v, gbs, grs, gbs, ST_SPEC, TM_SPEC, HCOL], out_specs=[HCOL, HCOL, HCOL, grs, grs],
                 sem=("parallel",))(qkvn, qkvn, qkvn, gb, gr, bb, st_all, tm_all, do_all)


Z_BLK = O_Z // DNW


def _dn_post_fwd(o, proj, gn):
    def body(o_ref, z_ref, gn_ref, og_ref):
        for h in range(H):
            cols = slice(h * DH, (h + 1) * DH)
            ov = o_ref[:, cols]
            on = ov * lax.rsqrt(jnp.mean(ov * ov, axis=-1, keepdims=True) + EPS) * gn_ref[...]
            og_ref[:, cols] = (on * _silu(z_ref[:, cols])).astype(BF16)

    return _call(body, "dn_post_fwd", _sds((S, DNW), BF16), grid=(S // TS,),
                 in_specs=[_row_spec(DNW), pl.BlockSpec((TS, DNW), lambda i: (i, Z_BLK)), _vec_spec(DH)],
                 out_specs=_row_spec(DNW), sem=("parallel",))(o, proj, gn)


def _dn_post_bwd(dog, o, proj, gn, dproj):
    def body(d_ref, o_ref, z_ref, gn_ref, dproj_in, do_ref, dz_ref, acc_ref):
        dgn = jnp.zeros((1, DH), F32)
        for h in range(H):
            cols = slice(h * DH, (h + 1) * DH)
            ov, zv, dv = o_ref[:, cols], z_ref[:, cols], d_ref[:, cols]
            rinv = lax.rsqrt(jnp.mean(ov * ov, axis=-1, keepdims=True) + EPS)
            xn = ov * rinv
            don = dv * _silu(zv)
            dz_ref[:, cols] = (dv * xn * gn_ref[...] * _dsilu(zv)).astype(BF16)
            dgn = dgn + jnp.sum(don * xn, axis=0, keepdims=True)
            dxn = don * gn_ref[...]
            do_ref[:, cols] = rinv * (dxn - xn * jnp.mean(dxn * xn, axis=-1, keepdims=True))
        _acc_rows(acc_ref, [dgn])

    zspec = pl.BlockSpec((TS, DNW), lambda i: (i, Z_BLK))
    return _call(body, "dn_post_bwd", [_sds((S, DNW)), _sds((S, NINP), BF16), _sds((8, DH))], grid=(S // TS,),
                 in_specs=[_row_spec(DNW), _row_spec(DNW), zspec, _vec_spec(DH), ANY],
                 out_specs=[_row_spec(DNW), zspec, _acc_spec(DH)], sem=("arbitrary",),
                 aliases={4: 1})(dog, o, proj, gn, dproj)


GA_BLK = O_GA // D
GB_BLK = O_GB // D


def _merge_fwd(ba, bb, proj):
    def body(a_ref, b_ref, ga_ref, gb_ref, o_ref):
        o_ref[...] = (_sigmoid(ga_ref[...]) * a_ref[...] + _sigmoid(gb_ref[...]) * b_ref[...]).astype(BF16)

    return _call(body, "merge_fwd", _sds((S, D), BF16), grid=(S // TS,),
                 in_specs=[_row_spec(), _row_spec(), pl.BlockSpec((TS, D), lambda i: (i, GA_BLK)),
                           pl.BlockSpec((TS, D), lambda i: (i, GB_BLK))],
                 out_specs=_row_spec(), sem=("parallel",))(ba, bb, proj, proj)


def _merge_bwd(dm, ba, bb, proj, dproj):
    def body(d_ref, a_ref, b_ref, ga_ref, gb_ref, dproj_in, dg_ref, da_ref, db_ref):
        d = d_ref[...]
        sa, sb = _sigmoid(ga_ref[...]), _sigmoid(gb_ref[...])
        dg_ref[:, 0:D] = (d * a_ref[...] * sa * (1.0 - sa)).astype(BF16)
        dg_ref[:, D:2 * D] = (d * b_ref[...] * sb * (1.0 - sb)).astype(BF16)
        da_ref[...] = (d * sa).astype(BF16)
        db_ref[...] = (d * sb).astype(BF16)

    return _call(body, "merge_bwd", [_sds((S, NINP), BF16), _sds((S, D), BF16), _sds((S, D), BF16)], grid=(S // TS,),
                 in_specs=[_row_spec(), _row_spec(), _row_spec(), pl.BlockSpec((TS, D), lambda i: (i, GA_BLK)),
                           pl.BlockSpec((TS, D), lambda i: (i, GB_BLK)), ANY],
                 out_specs=[pl.BlockSpec((TS, 2 * D), lambda i: (i, O_GA // (2 * D))), _row_spec(), _row_spec()],
                 sem=("parallel",), aliases={5: 0})(dm, ba, bb, proj, proj, dproj)


NSH = NIN // NDEV


def _win_pieces():
    pieces = [(0, 0, 4096), (O_GA, 6160, 2 * D), (O_SM, 4096, 16)]
    for j in range(CFW // LANE):
        pieces.append((O_GLU + 2 * LANE * j, 4112 + LANE * j, LANE))
        pieces.append((O_GLU + 2 * LANE * j + LANE, 4112 + CFW + LANE * j, LANE))
    return pieces


def _pad_win(wt):
    rows = [wt[o:o + wdt] for _, o, wdt in sorted(_win_pieces())]
    rows.append(jnp.zeros((NINP - NIN, wt.shape[1]), wt.dtype))
    return jnp.concatenate(rows, axis=0)


def _unpad_win(gpt):
    return jnp.concatenate([gpt[p:p + wdt] for p, o, wdt in sorted(_win_pieces(), key=lambda t: t[1])], axis=0)


def _lane_vec(v8, offset):
    return jnp.pad(v8, ((0, 0), (offset, LANE - 8 - offset)))


def _tie(vec, token):
    return vec + token


def _local_step(x, tgt, mod, norm1_g, norm2_g, final_g, w_in_p, dn_conv_w, a_log, dt_bias, dn_norm_g,
                cf_conv_w, cf_ln_g, cf_ln_b, ffn_conv_w, comm):
    sh1, sc1, gt1, sh2, sc2, gt2 = (mod[:, i * D:(i + 1) * D] for i in range(6))
    alog_v, dtb_v = _lane_vec(a_log, H), _lane_vec(dt_bias, H)

    hn1 = _norm_mod(x, norm1_g, sc1, _tie(sh1, comm.token0), "norm_mod1")
    proj = _mm(hn1, w_in_p, "nt", F32, "mm_in", tn=1152)
    qkvn = _dn_pre_fwd(proj, dn_conv_w)
    gates = _gates_fwd(proj, alog_v, dtb_v)
    beta_t = gates[:, 0:H].T
    g_t = gates[:, H:2 * H].T
    gb = jnp.broadcast_to(g_t[:, :, None], (H, S, DH))
    bb = jnp.broadcast_to(beta_t[:, :, None], (H, S, DH))
    gr = g_t.reshape(H, NCH, CH)
    o, st_all, tm_all = _delta_fwd(qkvn, gb, gr, bb)
    og = _dn_post_fwd(o, proj, dn_norm_g)
    u1 = _cf_conv_fwd(proj, cf_conv_w)
    u3 = _cf_ln_fwd(u1, cf_ln_g, cf_ln_b)
    after = og[0:8, 0:LANE].astype(F32) + u3[0:8, 0:LANE].astype(F32)
    dn_w_o, cf_w_o, w_out, ffn_w_up, ffn_w_down = comm.late_weights(after)
    br_a = _mm(og, dn_w_o, "nn", F32, "mm_dn_o")
    br_b = _mm(u3, cf_w_o, "nn", F32, "mm_cf_o")
    merged = _merge_fwd(br_a, br_b, proj)
    mix = _mm(merged, w_out, "nn", F32, "mm_out")
    x2, hn2 = _resid_norm_mod(x, mix, gt1, norm2_g, sc2, sh2, "resid_norm_mod2")
    upall = _mm(hn2, ffn_w_up, "nn", F32, "mm_up")
    hmid = _ffn_mid_fwd(upall, ffn_conv_w)
    f = _mm(hmid, ffn_w_down, "nn", F32, "mm_down", tm=2048)

    dx3, df, acc_f = _loss_head(x2, f, tgt, gt2, final_g)
    d_final_g, d_gt2, loss = acc_f[0:1], acc_f[1:2], acc_f[2:3, 0:1]
    dhmid = _mm(df, ffn_w_down, "nt", F32, "mm_down_dx")
    g_w_down = _mm(hmid, df, "tn", BF16, "mm_down_dw")
    d_gate, d_up, g_ffn_conv = _ffn_mid_bwd(dhmid, upall, ffn_conv_w)
    g_w_up = _mm(hn2, d_gate, "tn", BF16, "mm_up_dw", tn=2 * FFN // NDEV, b2=d_up)
    tok_a = comm.grads_begin("a", dict(ffn_w_down=g_w_down, ffn_w_up=g_w_up))
    dhn2 = _mm(d_gate, ffn_w_up, "nt", F32, "mm_up_dx", a2=d_up, dep=jnp.broadcast_to(tok_a, (8, LANE)))
    tok_a = comm.grads_continue("a", dhn2)
    dx2, dmix, acc2 = _norm_mod_bwd(dhn2, x2, dx3, _tie(norm2_g, tok_a), sc2, "norm_mod2_bwd", mix=mix, gt=gt1)
    d_sh2, d_sc2, d_norm2_g, d_gt1 = acc2[0:1], acc2[1:2], acc2[2:3], acc2[3:4]
    dmerged = _mm(dmix, w_out, "nt", F32, "mm_out_dx")
    g_w_out = _mm(merged, dmix, "tn", BF16, "mm_out_dw")
    d_proj, d_bra, d_brb = _merge_bwd(dmerged, br_a, br_b, proj, lax.empty((S, NINP), BF16))
    du3 = _mm(d_brb, cf_w_o, "nt", F32, "mm_cf_o_dx")
    g_cf_w_o = _mm(u3, d_brb, "tn", BF16, "mm_cf_o_dw")
    du1, acc_ln = _cf_ln_bwd(du3, u1, cf_ln_g, cf_ln_b)
    d_proj, g_cf_conv = _cf_conv_bwd(du1, proj, cf_conv_w, d_proj)
    dog = _mm(d_bra, dn_w_o, "nt", F32, "mm_dn_o_dx")
    g_dn_w_o = _mm(og, d_bra, "tn", BF16, "mm_dn_o_dw")
    tok_b = comm.grads_begin("b", dict(w_out=g_w_out, cf_w_o=g_cf_w_o, dn_w_o=g_dn_w_o, ffn_conv_w=g_ffn_conv,
                                       cf_conv_w=g_cf_conv))
    do, d_proj, acc_gn = _dn_post_bwd(dog, o, proj, _tie(dn_norm_g, tok_b), d_proj)
    tok_b = comm.grads_continue("b", do)
    dq, dk, dv, dgr, dbr = _delta_bwd(qkvn, gb, _tie(gr, tok_b), bb, st_all, tm_all, do)
    d_proj, g_dn_conv = _dn_pre_bwd(dq, dk, dv, proj, dn_conv_w, d_proj)
    dgates = jnp.concatenate([dbr.reshape(H, S).T, dgr.reshape(H, S).T, jnp.zeros((S, LANE - 2 * H), F32)], axis=1)
    d_proj, acc_g = _gates_bwd(dgates, proj, alog_v, dtb_v, d_proj)
    g_w_in_p = _mm(d_proj, hn1, "tn", BF16, "mm_in_dw", tm=1152)
    comm.grads_begin("c", dict(w_in=g_w_in_p, dn_conv_w=g_dn_conv))
    tok_c = comm.grads_continue("c", g_dn_conv)
    dhn1 = _mm(d_proj, w_in_p, "nn", F32, "mm_in_dx", tm=2048, tk=1152, dep=jnp.broadcast_to(tok_c, (8, LANE)))
    grad_x, acc1 = _norm_mod_bwd(dhn1, x, dx2, norm1_g, sc1, "norm_mod1_bwd")
    d_sh1, d_sc1, d_norm1_g = acc1[0:1], acc1[1:2], acc1[2:3]

    d_mod = jnp.concatenate([d_sh1, d_sc1, d_gt1, d_sh2, d_sc2, d_gt2], axis=1)
    small = dict(mod=d_mod, norm1_g=d_norm1_g, norm2_g=d_norm2_g, final_norm_g=d_final_g,
                 cf_ln_g=acc_ln[0:1], cf_ln_b=acc_ln[1:2], dn_norm_g=acc_gn[0:1],
                 dn_a_log=acc_g[0:1, H:2 * H], dn_dt_bias=acc_g[1:2, H:2 * H])
    return loss, grad_x, small


def _dev_index(px, py, pc):
    return 4 * px + 2 * py + pc


def _all_gather(arrs, name):
    n = len(arrs)

    def body(*refs):
        ins, outs = refs[:n], refs[n:2 * n]
        send_sems, recv_sems, loc_sems = refs[2 * n:]
        x, y, c = _my_pos()
        me, sib = (x, y, c), (x, y, 1 - c)
        chips = [(1 - x, y), (x, 1 - y), (1 - x, 1 - y)]

        def cp(i, k, block, to, src=None):
            dst = outs[i].at[_dev_index(*block)]
            return pltpu.make_async_remote_copy(
                src_ref=dst if src is None else src, dst_ref=dst, send_sem=send_sems.at[i, k],
                recv_sem=recv_sems.at[i, k], device_id=to, device_id_type=MESH)

        mine = [pltpu.make_async_copy(ins[i], outs[i].at[_dev_index(*me)], loc_sems.at[i]) for i in range(n)]
        for m in mine:
            m.start()
        sent = []
        for i in range(n):
            sent.append(cp(i, 0, me, sib, src=ins[i]))
            sent += [cp(i, 1 + j, me, (*chip, c), src=ins[i]) for j, chip in enumerate(chips)]
        for s in sent:
            s.start()
        for i in range(n):
            for j, chip in enumerate(chips):
                cp(i, 1 + j, (*chip, c), me).wait_recv()
                fwd = cp(i, 4 + j, (*chip, c), sib)
                fwd.start()
                sent.append(fwd)
        for i in range(n):
            cp(i, 0, sib, me).wait_recv()
            for j, chip in enumerate(chips):
                cp(i, 4 + j, (*chip, 1 - c), me).wait_recv()
        for s in sent:
            s.wait_send()
        for m in mine:
            m.wait()

    outs = pl.pallas_call(
        body, out_shape=[_sds((NDEV,) + a.shape, a.dtype) for a in arrs], in_specs=[ANY] * n, out_specs=[ANY] * n,
        scratch_shapes=[pltpu.SemaphoreType.DMA((n, 7)), pltpu.SemaphoreType.DMA((n, 7)), pltpu.SemaphoreType.DMA((n,))],
        name=name)(*arrs)
    return list(outs)


def _slab(ref, layout, idx):
    kind, n = layout
    if kind == "rows":
        return ref.at[pl.ds(pl.multiple_of(idx * n, n), n), :]
    if kind == "cols":
        return ref.at[:, pl.ds(pl.multiple_of(idx * n, n), n)]
    return ref.at[idx]


def _slab_shape(arr, layout):
    kind, n = layout
    if kind == "rows":
        return (n, arr.shape[1])
    if kind == "cols":
        return (arr.shape[0], n)
    return tuple(arr.shape[1:])


HBM = pl.BlockSpec(memory_space=pltpu.HBM)
SEMS = pl.BlockSpec(memory_space=pltpu.SEMAPHORE)
EFFECT = pltpu.SideEffectType.DATAFLOW_SIDE_EFFECTING
TOKEN = jax.ShapeDtypeStruct((8, LANE), F32)


def _hbm(a):
    return pltpu.with_memory_space_constraint(a, pltpu.HBM)


def _gather_ici_copy(shard_ref, buf_ref, layout, send_sems, recv_sems, i, j, me, chip, c):
    return pltpu.make_async_remote_copy(
        src_ref=shard_ref, dst_ref=_slab(buf_ref, layout, me), send_sem=send_sems.at[3 * i + j],
        recv_sem=recv_sems.at[3 * i + j], device_id=(*chip, c), device_id_type=MESH)


def _gather_ici_start(shards, bufs, layouts, after, name):
    n = len(shards)

    def body(*refs):
        sh, bf = refs[:n], refs[n:2 * n]
        send_sems, recv_sems = refs[2 * n + 1], refs[2 * n + 2]
        token = refs[-1]
        x, y, c = _my_pos()
        me = _dev_index(x, y, c)
        for i in range(n):
            for j, chip in enumerate([(1 - x, y), (x, 1 - y), (1 - x, 1 - y)]):
                _gather_ici_copy(sh[i], bf[i], layouts[i], send_sems, recv_sems, i, j, me, chip, c).start()
        token[...] = jnp.zeros_like(token)

    outs = pl.pallas_call(
        body, name=name,
        out_shape=(pltpu.SemaphoreType.DMA((3 * n,)), pltpu.SemaphoreType.DMA((3 * n,)),
                   *[pltpu.HBM(a.shape, a.dtype) for a in shards], *[pltpu.HBM(a.shape, a.dtype) for a in bufs], TOKEN),
        in_specs=[HBM] * (2 * n) + [ANY],
        out_specs=(SEMS, SEMS, *[HBM] * (2 * n), pl.BlockSpec(memory_space=pltpu.VMEM)),
        input_output_aliases={i: 2 + i for i in range(2 * n)},
        compiler_params=pltpu.CompilerParams(has_side_effects=EFFECT),
    )(*[_hbm(a) for a in shards], *[_hbm(a) for a in bufs], after)
    return outs[0], outs[1], list(outs[2:2 + n]), list(outs[2 + n:2 + 2 * n]), outs[-1]


def _gather_ici_wait(send_sems, recv_sems, shards, bufs, layouts, after, name):
    n = len(shards)

    def body(*refs):
        sh, bf = refs[:n], refs[n:2 * n]
        ssem, rsem = refs[2 * n], refs[2 * n + 1]
        x, y, c = _my_pos()
        me = _dev_index(x, y, c)
        for i in range(n):
            for j, chip in enumerate([(1 - x, y), (x, 1 - y), (1 - x, 1 - y)]):
                cp = _gather_ici_copy(sh[i], bf[i], layouts[i], ssem, rsem, i, j, me, chip, c)
                cp.wait_send()
                cp.wait_recv()

    outs = pl.pallas_call(
        body, name=name,
        out_shape=(*[pltpu.HBM(a.shape, a.dtype) for a in shards], *[pltpu.HBM(a.shape, a.dtype) for a in bufs]),
        in_specs=[HBM] * (2 * n) + [SEMS, SEMS, ANY], out_specs=tuple([HBM] * (2 * n)),
        input_output_aliases={i: i for i in range(2 * n)},
        compiler_params=pltpu.CompilerParams(has_side_effects=EFFECT),
    )(*shards, *bufs, send_sems, recv_sems, after)
    return list(outs[:n]), list(outs[n:])


def _place_own(pos, shard, buf, layout, name):
    kind, n = layout
    r, cols = shard.shape
    tr = _row_tile(r, shard.dtype.itemsize)
    nr = r // tr
    if kind == "rows":
        ospec = pl.BlockSpec((tr, cols), lambda i, p: (p[2] * nr + i, 0))
    else:
        assert kind == "lead"
        ospec = pl.BlockSpec((None, tr, cols), lambda i, p: (p[2], i, 0))

    def body(pos_ref, s_ref, buf_in, o_ref):
        o_ref[...] = s_ref[...]

    return pl.pallas_call(
        body, out_shape=_sds(buf.shape, buf.dtype), name=name, input_output_aliases={2: 0},
        grid_spec=pltpu.PrefetchScalarGridSpec(
            num_scalar_prefetch=1, grid=(nr,), in_specs=[pl.BlockSpec((tr, cols), lambda i, p: (i, 0)), ANY],
            out_specs=ospec),
        compiler_params=pltpu.CompilerParams(dimension_semantics=("parallel",), vmem_limit_bytes=VMEM_LIMIT),
    )(pos, shard, buf)


def _gather_pair(shards, bufs, layouts, name):
    n = len(shards)

    def body(*refs):
        sh, bo = refs[:n], refs[2 * n:3 * n]
        send_sems, recv_sems = refs[3 * n:]
        x, y, c = _my_pos()
        sib = (x, y, 1 - c)
        copies = []
        for i in range(n):
            for k, (px, py) in enumerate([(x, y), (1 - x, y), (x, 1 - y), (1 - x, 1 - y)]):
                slab = _slab(bo[i], layouts[i], _dev_index(px, py, c))
                copies.append(pltpu.make_async_remote_copy(
                    src_ref=sh[i] if k == 0 else slab, dst_ref=slab, send_sem=send_sems.at[i, k],
                    recv_sem=recv_sems.at[i, k], device_id=sib, device_id_type=MESH))
        for cpy in copies:
            cpy.start()
        for cpy in copies:
            cpy.wait()

    outs = pl.pallas_call(
        body, out_shape=[_sds(a.shape, a.dtype) for a in bufs], in_specs=[ANY] * (2 * n), out_specs=[ANY] * n,
        input_output_aliases={n + i: i for i in range(n)},
        scratch_shapes=[pltpu.SemaphoreType.DMA((n, 4)), pltpu.SemaphoreType.DMA((n, 4))], name=name)(*shards, *bufs)
    return list(outs)


def _pair_copy(part_ref, land_ref, layout, send_sems, recv_sems, i, q, x, y, c):
    return pltpu.make_async_remote_copy(
        src_ref=_slab(part_ref, layout, 2 * q + (1 - c)), dst_ref=land_ref.at[q], send_sem=send_sems.at[4 * i + q],
        recv_sem=recv_sems.at[4 * i + q], device_id=(x, y, 1 - c), device_id_type=MESH)


def _pair_exchange_start(parts, layouts, name):
    n = len(parts)
    lands = [lax.empty((4,) + _slab_shape(p, lay), p.dtype) for p, lay in zip(parts, layouts)]

    def body(*refs):
        pt, ld = refs[:n], refs[n:2 * n]
        send_sems, recv_sems = refs[2 * n], refs[2 * n + 1]
        token = refs[-1]
        x, y, c = _my_pos()
        for i in range(n):
            for q in range(4):
                _pair_copy(pt[i], ld[i], layouts[i], send_sems, recv_sems, i, q, x, y, c).start()
        token[...] = jnp.zeros_like(token)

    outs = pl.pallas_call(
        body, name=name,
        out_shape=(pltpu.SemaphoreType.DMA((4 * n,)), pltpu.SemaphoreType.DMA((4 * n,)),
                   *[pltpu.HBM(a.shape, a.dtype) for a in parts], *[pltpu.HBM(a.shape, a.dtype) for a in lands], TOKEN),
        in_specs=[HBM] * (2 * n), out_specs=(SEMS, SEMS, *[HBM] * (2 * n), pl.BlockSpec(memory_space=pltpu.VMEM)),
        input_output_aliases={i: 2 + i for i in range(2 * n)},
        compiler_params=pltpu.CompilerParams(has_side_effects=EFFECT),
    )(*[_hbm(a) for a in parts], *[_hbm(a) for a in lands])
    return outs[0], outs[1], list(outs[2:2 + n]), list(outs[2 + n:2 + 2 * n]), outs[-1]


def _pair_exchange_wait(send_sems, recv_sems, parts, lands, layouts, after, name):
    n = len(parts)

    def body(*refs):
        pt, ld = refs[:n], refs[n:2 * n]
        ssem, rsem = refs[2 * n], refs[2 * n + 1]
        x, y, c = _my_pos()
        for i in range(n):
            for q in range(4):
                cp = _pair_copy(pt[i], ld[i], layouts[i], ssem, rsem, i, q, x, y, c)
                cp.wait_send()
                cp.wait_recv()

    outs = pl.pallas_call(
        body, name=name,
        out_shape=(*[pltpu.HBM(a.shape, a.dtype) for a in parts], *[pltpu.HBM(a.shape, a.dtype) for a in lands]),
        in_specs=[HBM] * (2 * n) + [SEMS, SEMS, ANY], out_specs=tuple([HBM] * (2 * n)),
        input_output_aliases={i: i for i in range(2 * n)},
        compiler_params=pltpu.CompilerParams(has_side_effects=EFFECT),
    )(*parts, *lands, send_sems, recv_sems, after)
    return list(outs[:n]), list(outs[n:])


def _chip_copy(sum_ref, land_ref, send_sems, recv_sems, i, j, chip, c):
    return pltpu.make_async_remote_copy(
        src_ref=sum_ref.at[2 * chip[0] + chip[1]], dst_ref=land_ref.at[j], send_sem=send_sems.at[3 * i + j],
        recv_sem=recv_sems.at[3 * i + j], device_id=(*chip, c), device_id_type=MESH)


def _chip_exchange_start(sums, name):
    n = len(sums)
    lands = [lax.empty((3,) + s.shape[1:], s.dtype) for s in sums]

    def body(*refs):
        sm, ld = refs[:n], refs[n:2 * n]
        send_sems, recv_sems = refs[2 * n], refs[2 * n + 1]
        token = refs[-1]
        x, y, c = _my_pos()
        for i in range(n):
            for j, chip in enumerate([(1 - x, y), (x, 1 - y), (1 - x, 1 - y)]):
                _chip_copy(sm[i], ld[i], send_sems, recv_sems, i, j, chip, c).start()
        token[...] = jnp.zeros_like(token)

    outs = pl.pallas_call(
        body, name=name,
        out_shape=(pltpu.SemaphoreType.DMA((3 * n,)), pltpu.SemaphoreType.DMA((3 * n,)),
                   *[pltpu.HBM(a.shape, a.dtype) for a in sums], *[pltpu.HBM(a.shape, a.dtype) for a in lands], TOKEN),
        in_specs=[HBM] * (2 * n), out_specs=(SEMS, SEMS, *[HBM] * (2 * n), pl.BlockSpec(memory_space=pltpu.VMEM)),
        input_output_aliases={i: 2 + i for i in range(2 * n)},
        compiler_params=pltpu.CompilerParams(has_side_effects=EFFECT),
    )(*[_hbm(a) for a in sums], *[_hbm(a) for a in lands])
    return outs[0], outs[1], list(outs[2:2 + n]), list(outs[2 + n:2 + 2 * n]), outs[-1]


def _chip_exchange_wait(send_sems, recv_sems, sums, lands, after, name):
    n = len(sums)

    def body(*refs):
        sm, ld = refs[:n], refs[n:2 * n]
        ssem, rsem = refs[2 * n], refs[2 * n + 1]
        x, y, c = _my_pos()
        for i in range(n):
            for j, chip in enumerate([(1 - x, y), (x, 1 - y), (1 - x, 1 - y)]):
                cp = _chip_copy(sm[i], ld[i], ssem, rsem, i, j, chip, c)
                cp.wait_send()
                cp.wait_recv()

    outs = pl.pallas_call(
        body, name=name,
        out_shape=(*[pltpu.HBM(a.shape, a.dtype) for a in sums], *[pltpu.HBM(a.shape, a.dtype) for a in lands]),
        in_specs=[HBM] * (2 * n) + [SEMS, SEMS, ANY], out_specs=tuple([HBM] * (2 * n)),
        input_output_aliases={i: i for i in range(2 * n)},
        compiler_params=pltpu.CompilerParams(has_side_effects=EFFECT),
    )(*sums, *lands, send_sems, recv_sems, after)
    return list(outs[:n]), list(outs[n:])


def _row_tile(r, itemsize):
    align = 32 // itemsize
    best = r
    for t in range(align, min(r, 256) + 1, align):
        if r % t == 0:
            best = t
    return best


def _prefetch_call(body, name, out_shape, grid, in_specs, out_specs, sem):
    return pl.pallas_call(
        body, out_shape=out_shape, name=name,
        grid_spec=pltpu.PrefetchScalarGridSpec(num_scalar_prefetch=1, grid=grid, in_specs=in_specs, out_specs=out_specs),
        compiler_params=pltpu.CompilerParams(dimension_semantics=sem, vmem_limit_bytes=VMEM_LIMIT))


def _pair_sum(pos, part, got, layout, name):
    kind, _ = layout
    _, r, cols = got.shape
    tr, tc = _tiles(r, cols, part.dtype.itemsize)
    nr, nc = r // tr, cols // tc
    if kind == "rows":
        pspec = pl.BlockSpec((tr, tc), lambda q, i, j, p: ((2 * q + p[0]) * nr + i, j))
    elif kind == "cols":
        pspec = pl.BlockSpec((tr, tc), lambda q, i, j, p: (i, (2 * q + p[0]) * nc + j))
    else:
        pspec = pl.BlockSpec((None, tr, tc), lambda q, i, j, p: (2 * q + p[0], i, j))

    def body(pos_ref, p_ref, g_ref, o_ref):
        o_ref[...] = (p_ref[...].astype(F32) + g_ref[...].astype(F32)).astype(o_ref.dtype)

    blk = pl.BlockSpec((None, tr, tc), lambda q, i, j, p: (q, i, j))
    return _prefetch_call(body, name, _sds((4, r, cols), part.dtype), (4, nr, nc), [pspec, blk], blk,
                          ("parallel", "parallel", "parallel"))(pos, part, got)


def _tiles(r, cols, itemsize):
    tr = _row_tile(r, itemsize)
    if tr < r or r * cols * 4 <= (2 << 20) or cols % 256:
        return tr, cols
    return r, 256


def _final_sum_adam(pos, sums, got, w, m, v, name):
    _, r, cols = w.shape
    tr, tc = _tiles(r, cols, sums.dtype.itemsize)

    def body(pos_ref, s_ref, g_ref, w_ref, m_ref, v_ref, go_ref, dl_ref, nm_ref, nv_ref):
        g = ((s_ref[...].astype(F32) + g_ref[0].astype(F32)) + g_ref[1].astype(F32)) + g_ref[2].astype(F32)
        dl, nm, nv = _adam(w_ref[...], g, m_ref[...], v_ref[...])
        go_ref[...] = g
        dl_ref[...] = dl
        nm_ref[...] = nm
        nv_ref[...] = nv

    big = pl.BlockSpec((None, tr, tc), lambda i, j, p: (0, i, j))
    return _prefetch_call(body, name, [_sds((1, r, cols))] * 4, (r // tr, cols // tc),
                          [pl.BlockSpec((None, tr, tc), lambda i, j, p: (p[1], i, j)),
                           pl.BlockSpec((3, tr, tc), lambda i, j, p: (0, i, j)), big, big, big],
                          [big] * 4, ("parallel", "parallel"))(pos, sums, got, w, m, v)


def _small_adam(g_all, w, m, v):
    npk = w.shape[1]

    def body(g_ref, w_ref, m_ref, v_ref, go_ref, dl_ref, nm_ref, nv_ref):
        g = g_ref[0:1, :]
        for k in range(1, NDEV):
            g = g + g_ref[k:k + 1, :]
        dl, nm, nv = _adam(w_ref[...], g, m_ref[...], v_ref[...])
        go_ref[...] = g
        dl_ref[...] = dl
        nm_ref[...] = nm
        nv_ref[...] = nv

    return _call(body, "small_adam", [_sds((1, npk))] * 4)(g_all, w, m, v)


SMALL = [("b_ada", 6 * D), ("norm1_g", D), ("norm2_g", D), ("final_norm_g", D), ("cf_ln_g", CFW), ("cf_ln_b", CFW),
         ("dn_norm_g", DH), ("dn_a_log", H), ("dn_dt_bias", H)]
LATE = ["dn_w_o", "cf_w_o", "w_out", "ffn_w_up", "ffn_w_down"]
LATE_SHAPE = {"dn_w_o": (NDEV, DNW, D // NDEV), "cf_w_o": (NDEV, CFW, D // NDEV), "w_out": (D, D),
              "ffn_w_up": (NDEV, D, 2 * FFN // NDEV), "ffn_w_down": (FFN, D)}
LATE_LAYOUT = {"dn_w_o": ("lead", NDEV), "cf_w_o": ("lead", NDEV), "w_out": ("rows", D // NDEV),
               "ffn_w_up": ("lead", NDEV), "ffn_w_down": ("rows", FFN // NDEV)}
LAYOUT = {"dn_w_o": ("cols", D // NDEV), "cf_w_o": ("cols", D // NDEV), "w_out": ("rows", D // NDEV),
          "ffn_w_up": ("cols", 2 * FFN // NDEV), "ffn_w_down": ("rows", FFN // NDEV),
          "w_in": ("lead", NDEV), "dn_conv_w": ("lead", NDEV), "cf_conv_w": ("lead", NDEV), "ffn_conv_w": ("lead", NDEV)}
NAMES = ["w_ada", "b_ada", "norm1_g", "w_in", "dn_conv_w", "dn_a_log", "dn_dt_bias", "dn_norm_g", "dn_w_o", "cf_conv_w",
         "cf_ln_g", "cf_ln_b", "cf_w_o", "w_out", "norm2_g", "ffn_w_up", "ffn_conv_w", "ffn_w_down", "final_norm_g"]


def _pack_small(d):
    rows = []
    for nm, n in SMALL:
        row = d[nm].reshape(1, n)
        pad = (-n) % LANE
        rows.append(jnp.pad(row, ((0, 0), (0, pad))) if pad else row)
    return jnp.concatenate(rows, axis=1)


def _unpack_small(row, shapes):
    out, off = {}, 0
    for nm, n in SMALL:
        out[nm] = row[0, off:off + n].reshape(shapes[nm])
        off += n + ((-n) % LANE)
    return out


def _cols_from_gathered(g):
    return jnp.transpose(g, (1, 0, 2)).reshape(g.shape[1], NDEV * g.shape[2])


def _cols_to_parts(full):
    r, ctot = full.shape
    return jnp.transpose(full.reshape(r, NDEV, ctot // NDEV), (1, 0, 2))


def kernel(x, c, w_ada, b_ada, norm1_g, w_in, dn_conv_w, dn_a_log, dn_dt_bias, dn_norm_g, dn_w_o, cf_conv_w, cf_ln_g, cf_ln_b, cf_w_o, w_out, norm2_g, ffn_w_up, ffn_conv_w, ffn_w_down, final_norm_g, loss_target, m_w_ada, m_b_ada, m_norm1_g, m_w_in, m_dn_conv_w, m_dn_a_log, m_dn_dt_bias, m_dn_norm_g, m_dn_w_o, m_cf_conv_w, m_cf_ln_g, m_cf_ln_b, m_cf_w_o, m_w_out, m_norm2_g, m_ffn_w_up, m_ffn_conv_w, m_ffn_w_down, m_final_norm_g, v_w_ada, v_b_ada, v_norm1_g, v_w_in, v_dn_conv_w, v_dn_a_log, v_dn_dt_bias, v_dn_norm_g, v_dn_w_o, v_cf_conv_w, v_cf_ln_g, v_cf_ln_b, v_cf_w_o, v_w_out, v_norm2_g, v_ffn_w_up, v_ffn_conv_w, v_ffn_w_down, v_final_norm_g):
    args = locals()
    w = {nm: args[nm] for nm in NAMES}
    mo = {nm: args["m_" + nm] for nm in NAMES}
    vo = {nm: args["v_" + nm] for nm in NAMES}
    shapes = {nm: w[nm].shape for nm in NAMES}
    px, py, pc = _my_pos()
    me = _dev_index(px, py, pc)

    def mat(a):
        return a.reshape(a.shape[-2:])

    pos = jnp.stack([pc, 2 * px + py, me]).astype(jnp.int32)

    first = ["w_in", "dn_conv_w", "cf_conv_w", "ffn_conv_w"]
    tr_in = lambda a: jnp.transpose(a, (0, 2, 1))
    got = _all_gather([tr_in(w["w_in"]).astype(BF16)] + [mat(w[nm]) for nm in first[1:]] + [c], "gather_first")
    full = {nm: _cols_from_gathered(g) for nm, g in zip(first[1:], got[1:-1])}
    c_all = got[-1].reshape(NDEV, D)
    w_in_p = _pad_win(got[0].reshape(NIN, D))

    ncol = 6 * D // NDEV
    b_sh = lax.dynamic_slice(b_ada.reshape(1, 6 * D), (0, me * ncol), (1, ncol))
    mod_sh = _ada_fwd(c_all, mat(w_ada), b_sh)
    mod_all = _all_gather([mod_sh], "gather_mod")[0]
    mod = lax.dynamic_index_in_dim(mod_all, me, axis=1, keepdims=False).reshape(1, 6 * D)

    late_shards = [mat(w[nm]).astype(BF16) for nm in LATE]
    late_lay = [LATE_LAYOUT[nm] for nm in LATE]
    late_bufs = [_place_own(pos, s, lax.empty(LATE_SHAPE[nm], BF16), lay, "place_" + nm)
                 for nm, s, lay in zip(LATE, late_shards, late_lay)]
    l_send, l_recv, l_shards, l_bufs, l_token = _gather_ici_start(late_shards, late_bufs, late_lay, mod_all, "gather_late_start")

    res = {}

    class Comm:
        token0 = l_token[0, 0]
        pending = {}

        @staticmethod
        def late_weights(after):
            shards, bufs = _gather_ici_wait(l_send, l_recv, l_shards, l_bufs, late_lay, after, "gather_late_wait")
            return _gather_pair(shards, bufs, late_lay, "gather_late_pair")

        @staticmethod
        def grads_begin(group, gd):
            names = list(gd)
            lays = [LAYOUT[nm] for nm in names]
            gl = []
            for nm in names:
                if nm == "w_in":
                    gl.append(_unpad_win(gd[nm]).reshape(NDEV, NSH, D))
                else:
                    gl.append(_cols_to_parts(gd[nm]) if LAYOUT[nm][0] == "lead" else gd[nm])
            started = _pair_exchange_start(gl, lays, "rs_pair_start_" + group)
            Comm.pending[group] = (names, lays) + tuple(started[:4])
            return started[4][0, 0]

        @staticmethod
        def grads_continue(group, after):
            names, lays, ssem, rsem, gl, lands = Comm.pending[group]
            gl, from_sib = _pair_exchange_wait(ssem, rsem, gl, lands, lays, after, "rs_pair_wait_" + group)
            sums = [_pair_sum(pos, g, r, lay, "rs_pair_sum_" + nm) for nm, g, r, lay in zip(names, gl, from_sib, lays)]
            started = _chip_exchange_start(sums, "rs_chips_start_" + group)
            Comm.pending[group] = (names,) + tuple(started[:4])
            return started[4][0, 0]

        @staticmethod
        def finish(group, after):
            names, ssem, rsem, sums, lands = Comm.pending[group]
            sums, lands = _chip_exchange_wait(ssem, rsem, sums, lands, after, "rs_chips_wait_" + group)
            for nm, s, r in zip(names, sums, lands):
                if nm == "w_in":
                    outs = _final_sum_adam(pos, s, r, tr_in(w[nm]), tr_in(mo[nm]), tr_in(vo[nm]), "adam_" + nm)
                    res[nm] = [tr_in(o) for o in outs]
                else:
                    res[nm] = _final_sum_adam(pos, s, r, w[nm], mo[nm], vo[nm], "adam_" + nm)
            return res[names[-1]][0]

    vec = lambda a: a.reshape(1, -1)
    loss, grad_x, small = _local_step(
        x.reshape(S, D), loss_target.reshape(S, D), mod, vec(norm1_g), vec(norm2_g), vec(final_norm_g), w_in_p,
        full["dn_conv_w"], vec(dn_a_log), vec(dn_dt_bias), vec(dn_norm_g), full["cf_conv_w"], vec(cf_ln_g),
        vec(cf_ln_b), full["ffn_conv_w"], Comm)

    done_a = Comm.finish("a", grad_x)
    done_b = Comm.finish("b", done_a)

    small["b_ada"] = small.pop("mod")
    packed = _pack_small(small) + 0.0 * done_b.reshape(-1)[0]
    g_small = _all_gather([packed], "gather_small")[0].reshape(NDEV, -1)
    outs = _small_adam(g_small, _pack_small({nm: w[nm] for nm, _ in SMALL}), _pack_small({nm: mo[nm] for nm, _ in SMALL}),
                       _pack_small({nm: vo[nm] for nm, _ in SMALL}))
    unpacked = [_unpack_small(o, shapes) for o in outs]
    for nm, _ in SMALL:
        res[nm] = [u[nm] for u in unpacked]

    dmod_sel = lax.dynamic_slice(g_small[:, :6 * D], (0, me * ncol), (NDEV, ncol))
    outs = _ada_bwd_adam(c_all, dmod_sel, mat(w_ada), mat(m_w_ada), mat(v_w_ada))
    res["w_ada"] = [o.reshape(shapes["w_ada"]) for o in outs]
    Comm.finish("c", jnp.concatenate([done_b.reshape(-1)[:LANE], outs[0].reshape(-1)[:LANE]]))

    loss = lax.psum(loss.reshape(()), ("x", "y", "c"))
    out = [loss, grad_x.reshape(x.shape)]
    for k in range(4):
        out += [res[nm][k] for nm in NAMES]
    return tuple(out)
```

```python
import functools

import jax
import jax.numpy as jnp
from jax import lax
from jax.experimental import pallas as pl
from jax.experimental.pallas import tpu as pltpu

F32 = jnp.float32
BF16 = jnp.bfloat16
HI = lax.Precision.HIGHEST
MESH = pl.DeviceIdType.MESH
ANY = pl.BlockSpec(memory_space=pl.ANY)

NDEV = 8
D = 2048
S = 2048
H = 8
DH = 128
DNW = H * DH
CFW = 1024
CFK = 31
DNK = 4
FFN = 5632
FFK = 3
CH = 64
NCH = S // CH
EPS = 1e-6
NIN = 10256
NINP = 10368
O_Z, O_GA, O_GB, O_GLU, O_SM = 3072, 4096, 6144, 8192, 10240
LANE = 128
TS = 256
VMEM_LIMIT = 56 * 1024 * 1024

ADAM_LR, ADAM_B1, ADAM_B2, ADAM_EPS, ADAM_WD, ADAM_STEP = 0.001, 0.9, 0.999, 1e-08, 0.01, 10


def _call(body, name, out_shape, grid=(), in_specs=None, out_specs=None, scratch=(), sem=None, aliases=None):
    kw = {}
    if aliases:
        kw["input_output_aliases"] = aliases
    if in_specs is not None:
        kw["in_specs"] = in_specs
    if out_specs is not None:
        kw["out_specs"] = out_specs
    return pl.pallas_call(
        body, out_shape=out_shape, grid=grid, scratch_shapes=scratch, name=name,
        compiler_params=pltpu.CompilerParams(dimension_semantics=sem, vmem_limit_bytes=VMEM_LIMIT), **kw)


def _sds(shape, dtype=F32):
    return jax.ShapeDtypeStruct(shape, dtype)


def _tile(dim, pref):
    if dim <= pref:
        return dim
    best = None
    for t in range(LANE, pref + 1, LANE):
        if dim % t == 0:
            best = t
    assert best is not None, (dim, pref)
    return best


def _sigmoid(x):
    return 1.0 / (1.0 + jnp.exp(-x))


def _silu(x):
    return x * _sigmoid(x)


def _dsilu(x):
    s = _sigmoid(x)
    return s * (1.0 + x * (1.0 - s))


def _softplus(x):
    return jnp.maximum(x, 0.0) + jnp.log(1.0 + jnp.exp(-jnp.abs(x)))


def _dot(a, b, dims, precision=None):
    return lax.dot_general(a, b, (dims, ((), ())), preferred_element_type=F32, precision=precision)


NN = ((1,), (0,))
NT = ((1,), (1,))
TN = ((0,), (0,))


def _my_pos():
    return lax.axis_index("x"), lax.axis_index("y"), lax.axis_index("c")


def _mm(a, b, mode, out_dtype, name, tm=1024, tn=1024, tk=2048, a2=None, b2=None, dep=None):
    sharded = b.ndim == 3
    if sharded and mode == "nn":
        cs = b.shape[2]
        (m, k), n = a.shape, NDEV * cs
        gs = max(1, tn // cs)
        tm, tn, tk = _tile(m, tm), gs * cs, _tile(k, tk)
    elif sharded:
        assert mode == "nt"
        cs = b.shape[2]
        m, n, k = a.shape[0], b.shape[1], NDEV * cs
        gs = max(1, tk // cs)
        tm, tn, tk = _tile(m, tm), _tile(n, tn), gs * cs
    else:
        if mode == "nn":
            (m, k), (k2, n) = a.shape, b.shape
        elif mode == "nt":
            (m, k), (n, k2) = a.shape, b.shape
        else:
            (k, m), (k2, n) = a.shape, b.shape
        assert k == k2, (a.shape, b.shape, mode)
        n = n * (2 if b2 is not None else 1)
        tm, tn, tk = _tile(m, tm), _tile(n // (2 if b2 is not None else 1), tn), _tile(k, tk)
    nk, nj = k // tk, n // tn
    halfk, halfj = nk // 2, nj // 2
    dims = {"nn": NN, "nt": NT, "tn": TN}[mode]

    n_in = 2 + (a2 is not None) + (b2 is not None) + (dep is not None)

    def body(*refs):
        a_ref, b_ref = refs[0], refs[1]
        x_ref = refs[2] if (a2 is not None or b2 is not None) else None
        o_ref = refs[n_in]
        acc_ref = refs[n_in + 1] if nk > 1 else None
        j, kk = pl.program_id(1), pl.program_id(2)

        if nk > 1:
            @pl.when(kk == 0)
            def _():
                acc_ref[...] = jnp.zeros_like(acc_ref)

        def accumulate(product, cols=slice(None)):
            if nk == 1:
                o_ref[:, cols] = product().astype(o_ref.dtype)
            else:
                acc_ref[:, cols] += product()

        if sharded and mode == "nn":
            for q in range(gs):
                accumulate(lambda q=q: _dot(a_ref[...], b_ref[q], NN), slice(q * cs, (q + 1) * cs))
        elif sharded:
            def contract(lhs_ref):
                def product():
                    part = None
                    for q in range(gs):
                        term = _dot(lhs_ref[:, q * cs:(q + 1) * cs], b_ref[q], NT)
                        part = term if part is None else part + term
                    return part
                accumulate(product)

            if a2 is None:
                contract(a_ref)
            else:
                pl.when(kk < halfk)(lambda: contract(a_ref))
                pl.when(kk >= halfk)(lambda: contract(x_ref))
        elif b2 is not None:
            pl.when(j < halfj)(lambda: accumulate(lambda: _dot(a_ref[...], b_ref[...], dims)))
            pl.when(j >= halfj)(lambda: accumulate(lambda: _dot(a_ref[...], x_ref[...], dims)))
        else:
            accumulate(lambda: _dot(a_ref[...], b_ref[...], dims))

        if nk > 1:
            @pl.when(kk == nk - 1)
            def _():
                o_ref[...] = acc_ref[...].astype(o_ref.dtype)

    ins, in_specs = [a], []
    if mode == "tn":
        in_specs.append(pl.BlockSpec((tk, tm), lambda i, j, kk: (kk, i)))
    elif a2 is not None:
        in_specs.append(pl.BlockSpec((tm, tk), lambda i, j, kk: (i, jnp.minimum(kk, halfk - 1))))
    else:
        in_specs.append(pl.BlockSpec((tm, tk), lambda i, j, kk: (i, kk)))
    ins.append(b)
    if sharded and mode == "nn":
        in_specs.append(pl.BlockSpec((gs, tk, cs), lambda i, j, kk: (j, kk, 0)))
    elif sharded:
        in_specs.append(pl.BlockSpec((gs, tn, cs), lambda i, j, kk: (kk, j, 0)))
    elif mode == "nt":
        in_specs.append(pl.BlockSpec((tn, tk), lambda i, j, kk: (j, kk)))
    elif b2 is not None:
        in_specs.append(pl.BlockSpec((tk, tn), lambda i, j, kk: (kk, jnp.minimum(j, halfj - 1))))
    else:
        in_specs.append(pl.BlockSpec((tk, tn), lambda i, j, kk: (kk, j)))
    if a2 is not None:
        ins.append(a2)
        in_specs.append(pl.BlockSpec((tm, tk), lambda i, j, kk: (i, jnp.maximum(kk - halfk, 0))))
    if b2 is not None:
        ins.append(b2)
        in_specs.append(pl.BlockSpec((tk, tn), lambda i, j, kk: (kk, jnp.maximum(j - halfj, 0))))
    if dep is not None:
        ins.append(dep)
        in_specs.append(ANY)
    return _call(body, name, _sds((m, n), out_dtype), grid=(m // tm, nj, nk),
                 in_specs=in_specs, out_specs=pl.BlockSpec((tm, tn), lambda i, j, kk: (i, j)),
                 scratch=[pltpu.VMEM((tm, tn), F32)] if nk > 1 else [],
                 sem=("parallel", "parallel", "arbitrary"))(*ins)


def _ada_fwd(c_all, w_sh, b_sh):
    n = w_sh.shape[1]
    tn = 512

    def body(c_ref, w_ref, b_ref, o_ref):
        ca = _silu(c_ref[...]).astype(BF16)
        o_ref[...] = _dot(ca, w_ref[...].astype(BF16), NN) + b_ref[...]

    return _call(body, "ada_fwd", _sds((NDEV, n)), grid=(n // tn,),
                 in_specs=[pl.BlockSpec((NDEV, D), lambda j: (0, 0)), pl.BlockSpec((D, tn), lambda j: (0, j)),
                           pl.BlockSpec((1, tn), lambda j: (0, j))],
                 out_specs=pl.BlockSpec((NDEV, tn), lambda j: (0, j)), sem=("parallel",))(c_all, w_sh, b_sh)


def _adam(w, g, m, v):
    m = ADAM_B1 * m + (1.0 - ADAM_B1) * g
    v = ADAM_B2 * v + (1.0 - ADAM_B2) * (g * g)
    m_hat = m / (1.0 - ADAM_B1 ** ADAM_STEP)
    v_hat = v / (1.0 - ADAM_B2 ** ADAM_STEP)
    delta = -ADAM_LR * (m_hat / (jnp.sqrt(v_hat) + ADAM_EPS) + ADAM_WD * w)
    return delta, m, v


def _ada_bwd_adam(c_all, dmod_sel, w, m, v):
    r, n = w.shape
    tr = 256

    def body(c_ref, d_ref, w_ref, m_ref, v_ref, g_ref, dl_ref, nm_ref, nv_ref):
        ca = _silu(c_ref[...])
        g = _dot(ca, d_ref[...], TN, precision=HI)
        dl, nm, nv = _adam(w_ref[...], g, m_ref[...], v_ref[...])
        g_ref[...] = g
        dl_ref[...] = dl
        nm_ref[...] = nm
        nv_ref[...] = nv

    big = pl.BlockSpec((tr, n), lambda i: (i, 0))
    return _call(body, "ada_bwd_adam", [_sds((r, n))] * 4, grid=(r // tr,),
                 in_specs=[pl.BlockSpec((NDEV, tr), lambda i: (0, i)), pl.BlockSpec((NDEV, n), lambda i: (0, 0)),
                           big, big, big],
                 out_specs=[big] * 4, sem=("parallel",))(c_all, dmod_sel, w, m, v)


def _row_spec(width=D):
    return pl.BlockSpec((TS, width), lambda i: (i, 0))


def _vec_spec(width=D):
    return pl.BlockSpec((1, width), lambda i: (0, 0))


def _acc_spec(width=D):
    return pl.BlockSpec((8, width), lambda i: (0, 0))


def _norm_mod(x, g, sc, sh, name):
    def body(x_ref, g_ref, sc_ref, sh_ref, o_ref):
        xv = x_ref[...]
        r = lax.rsqrt(jnp.mean(xv * xv, axis=-1, keepdims=True) + EPS)
        o_ref[...] = ((xv * r) * g_ref[...] * (1.0 + sc_ref[...]) + sh_ref[...]).astype(BF16)

    return _call(body, name, _sds((S, D), BF16), grid=(S // TS,),
                 in_specs=[_row_spec(), _vec_spec(), _vec_spec(), _vec_spec()], out_specs=_row_spec(),
                 sem=("parallel",))(x, g, sc, sh)


def _resid_norm_mod(x, mix, gt, g, sc, sh, name):
    def body(x_ref, mix_ref, gt_ref, g_ref, sc_ref, sh_ref, x2_ref, o_ref):
        xv = x_ref[...] + gt_ref[...] * mix_ref[...]
        x2_ref[...] = xv
        r = lax.rsqrt(jnp.mean(xv * xv, axis=-1, keepdims=True) + EPS)
        o_ref[...] = ((xv * r) * g_ref[...] * (1.0 + sc_ref[...]) + sh_ref[...]).astype(BF16)

    return _call(body, name, [_sds((S, D)), _sds((S, D), BF16)], grid=(S // TS,),
                 in_specs=[_row_spec(), _row_spec()] + [_vec_spec()] * 4, out_specs=[_row_spec(), _row_spec()],
                 sem=("parallel",))(x, mix, gt, g, sc, sh)


def _acc_rows(acc_ref, rows):
    @pl.when(pl.program_id(0) == 0)
    def _():
        acc_ref[...] = jnp.zeros_like(acc_ref)

    for k, row in enumerate(rows):
        acc_ref[k:k + 1, :] += row


def _loss_head(x2, f, tgt, gt2, gf):
    def body(x2_ref, f_ref, t_ref, gt_ref, gf_ref, dx_ref, df_ref, acc_ref):
        fv = f_ref[...]
        x3 = x2_ref[...] + gt_ref[...] * fv
        r = lax.rsqrt(jnp.mean(x3 * x3, axis=-1, keepdims=True) + EPS)
        xn = x3 * r
        e = xn * gf_ref[...] - t_ref[...]
        loss = 0.5 * jnp.sum(jnp.mean(e * e, axis=-1, keepdims=True), axis=0, keepdims=True)
        dy = e * (1.0 / D)
        dxn = dy * gf_ref[...]
        dx3 = r * (dxn - xn * jnp.mean(dxn * xn, axis=-1, keepdims=True))
        dx_ref[...] = dx3
        df_ref[...] = (dx3 * gt_ref[...]).astype(BF16)
        _acc_rows(acc_ref, [jnp.sum(dy * xn, axis=0, keepdims=True), jnp.sum(dx3 * fv, axis=0, keepdims=True),
                            jnp.broadcast_to(loss, (1, D))])

    return _call(body, "loss_head", [_sds((S, D)), _sds((S, D), BF16), _sds((8, D))], grid=(S // TS,),
                 in_specs=[_row_spec(), _row_spec(), _row_spec(), _vec_spec(), _vec_spec()],
                 out_specs=[_row_spec(), _row_spec(), _acc_spec()], sem=("arbitrary",))(x2, f, tgt, gt2, gf)


def _norm_mod_bwd(dhn, x, dres, g, sc, name, mix=None, gt=None):
    gated = mix is not None

    def body(*refs):
        if gated:
            dhn_ref, x_ref, dres_ref, g_ref, sc_ref, mix_ref, gt_ref, dx_ref, dmix_ref, acc_ref = refs
        else:
            dhn_ref, x_ref, dres_ref, g_ref, sc_ref, dx_ref, acc_ref = refs
        xv = x_ref[...]
        dh = dhn_ref[...]
        r = lax.rsqrt(jnp.mean(xv * xv, axis=-1, keepdims=True) + EPS)
        xn = xv * r
        gv = g_ref[...]
        sc1 = 1.0 + sc_ref[...]
        dxn = dh * gv * sc1
        dx = dres_ref[...] + r * (dxn - xn * jnp.mean(dxn * xn, axis=-1, keepdims=True))
        dx_ref[...] = dx
        rows = [jnp.sum(dh, axis=0, keepdims=True), jnp.sum(dh * xn * gv, axis=0, keepdims=True),
                jnp.sum(dh * xn * sc1, axis=0, keepdims=True)]
        if gated:
            rows.append(jnp.sum(dx * mix_ref[...], axis=0, keepdims=True))
            dmix_ref[...] = (dx * gt_ref[...]).astype(BF16)
        _acc_rows(acc_ref, rows)

    ins = [dhn, x, dres, g, sc]
    in_specs = [_row_spec(), _row_spec(), _row_spec(), _vec_spec(), _vec_spec()]
    outs = [_sds((S, D))]
    out_specs = [_row_spec()]
    if gated:
        ins += [mix, gt]
        in_specs += [_row_spec(), _vec_spec()]
        outs.append(_sds((S, D), BF16))
        out_specs.append(_row_spec())
    outs.append(_sds((8, D)))
    out_specs.append(_acc_spec())
    return _call(body, name, outs, grid=(S // TS,), in_specs=in_specs, out_specs=out_specs,
                 sem=("arbitrary",))(*ins)


RC = 256


def _conv_fwd_rows(pad_ref, w_ref, kw, head, r0):
    acc = None
    for k in range(kw):
        term = w_ref[k:k + 1, :] * pad_ref[pl.ds(head - (kw - 1) + k + r0, RC), :]
        acc = term if acc is None else acc + term
    return acc


def _conv_bwd_rows(pad2_ref, w_ref, kw, r0):
    acc = None
    for k in range(kw):
        term = w_ref[k:k + 1, :] * pad2_ref[pl.ds(kw - 1 - k + r0, RC), :]
        acc = term if acc is None else acc + term
    return acc


def _conv_dw(pad_ref, dout_ref, dw_ref, kw, head):
    for k in range(kw):
        acc = None
        for r0 in range(0, S, RC):
            term = jnp.sum(pad_ref[pl.ds(head - (kw - 1) + k + r0, RC), :] * dout_ref[pl.ds(r0, RC), :],
                           axis=0, keepdims=True)
            acc = term if acc is None else acc + term
        dw_ref[k:k + 1, :] = acc


def _col_spec(width, off_blocks=0):
    return pl.BlockSpec((S, width), lambda j: (0, j + off_blocks))


def _dn_pre_fwd(proj, conv_w):
    head = 8

    def body(x_ref, w_ref, o_ref, pad_ref):
        j = pl.program_id(0)
        pad_ref[pl.ds(0, head), :] = jnp.zeros((head, DH), F32)
        pad_ref[pl.ds(head, S), :] = x_ref[...]
        scale = jnp.where(j < H, DH ** -0.5, 1.0)
        for r0 in range(0, S, RC):
            y = _silu(_conv_fwd_rows(pad_ref, w_ref, DNK, head, r0))
            rinv = lax.rsqrt(jnp.sum(y * y, axis=-1, keepdims=True) + EPS)
            o_ref[pl.ds(r0, RC), :] = jnp.where(j < 2 * H, y * rinv * scale, y)

    return _call(body, "dn_pre_fwd", _sds((S, 3 * DNW)), grid=(3 * H,),
                 in_specs=[_col_spec(DH), pl.BlockSpec((DNK, DH), lambda j: (0, j))], out_specs=_col_spec(DH),
                 scratch=[pltpu.VMEM((S + head, DH), F32)], sem=("parallel",))(proj, conv_w)


def _dn_pre_bwd(dq, dk, dv, proj, conv_w, dproj):
    head = 8

    def body(dq_ref, dk_ref, dv_ref, x_ref, w_ref, dproj_in, dx_ref, dw_ref, pad_ref, pad2_ref):
        j = pl.program_id(0)
        pad_ref[pl.ds(0, head), :] = jnp.zeros((head, DH), F32)
        pad_ref[pl.ds(head, S), :] = x_ref[...]
        pad2_ref[pl.ds(S, head), :] = jnp.zeros((head, DH), F32)
        scale = jnp.where(j < H, DH ** -0.5, 1.0)
        for r0 in range(0, S, RC):
            xc = _conv_fwd_rows(pad_ref, w_ref, DNK, head, r0)
            y = _silu(xc)
            rinv = lax.rsqrt(jnp.sum(y * y, axis=-1, keepdims=True) + EPS)
            yn = y * rinv
            rows = pl.ds(r0, RC)
            do = jnp.where(j < H, dq_ref[rows, :], jnp.where(j < 2 * H, dk_ref[rows, :], dv_ref[rows, :]))
            dy_n = scale * rinv * (do - yn * jnp.sum(do * yn, axis=-1, keepdims=True))
            dy = jnp.where(j < 2 * H, dy_n, do)
            pad2_ref[rows, :] = dy * _dsilu(xc)
        for r0 in range(0, S, RC):
            dx_ref[pl.ds(r0, RC), :] = _conv_bwd_rows(pad2_ref, w_ref, DNK, r0).astype(BF16)
        _conv_dw(pad_ref, pad2_ref, dw_ref, DNK, head)

    wspec = pl.BlockSpec((DNK, DH), lambda j: (0, j))
    head_col = lambda lo: pl.BlockSpec((S, DH), lambda j: (0, jnp.clip(j - lo, 0, H - 1)))
    return _call(body, "dn_pre_bwd", [_sds((S, NINP), BF16), _sds((DNK, 3 * DNW))], grid=(3 * H,),
                 in_specs=[head_col(0), head_col(H), head_col(2 * H), _col_spec(DH), wspec, ANY],
                 out_specs=[_col_spec(DH), wspec],
                 scratch=[pltpu.VMEM((S + head, DH), F32), pltpu.VMEM((S + head, DH), F32)],
                 sem=("parallel",), aliases={5: 0})(dq, dk, dv, proj, conv_w, dproj)


CF_HEAD = 32
CF_VAL = pl.BlockSpec((S, LANE), lambda j: (0, O_GLU // LANE + 2 * j))
CF_GL = pl.BlockSpec((S, LANE), lambda j: (0, O_GLU // LANE + 2 * j + 1))


def _cf_conv_fwd(proj, conv_w):
    def body(val_ref, gl_ref, w_ref, o_ref, pad_ref):
        pad_ref[pl.ds(0, CF_HEAD), :] = jnp.zeros((CF_HEAD, LANE), F32)
        pad_ref[pl.ds(CF_HEAD, S), :] = val_ref[...] * _sigmoid(gl_ref[...])
        for r0 in range(0, S, RC):
            o_ref[pl.ds(r0, RC), :] = _conv_fwd_rows(pad_ref, w_ref, CFK, CF_HEAD, r0)

    wspec = pl.BlockSpec((CFK, LANE), lambda j: (0, j))
    return _call(body, "cf_conv_fwd", _sds((S, CFW)), grid=(CFW // LANE,),
                 in_specs=[CF_VAL, CF_GL, wspec], out_specs=_col_spec(LANE),
                 scratch=[pltpu.VMEM((S + CF_HEAD, LANE), F32)], sem=("parallel",))(proj, proj, conv_w)


def _cf_conv_bwd(du1, proj, conv_w, dproj):
    def body(d_ref, val_ref, gl_ref, w_ref, dproj_in, dp_ref, dw_ref, pad_ref, pad2_ref):
        sg = _sigmoid(gl_ref[...])
        pad_ref[pl.ds(0, CF_HEAD), :] = jnp.zeros((CF_HEAD, LANE), F32)
        pad_ref[pl.ds(CF_HEAD, S), :] = val_ref[...] * sg
        pad2_ref[pl.ds(0, S), :] = d_ref[...]
        pad2_ref[pl.ds(S, CF_HEAD), :] = jnp.zeros((CF_HEAD, LANE), F32)
        for r0 in range(0, S, RC):
            du0 = _conv_bwd_rows(pad2_ref, w_ref, CFK, r0)
            rows = pl.ds(r0, RC)
            sgr = _sigmoid(gl_ref[rows, :])
            dp_ref[rows, 0:LANE] = (du0 * sgr).astype(BF16)
            dp_ref[rows, LANE:2 * LANE] = (du0 * val_ref[rows, :] * sgr * (1.0 - sgr)).astype(BF16)
        _conv_dw(pad_ref, pad2_ref, dw_ref, CFK, CF_HEAD)

    wspec = pl.BlockSpec((CFK, LANE), lambda j: (0, j))
    return _call(body, "cf_conv_bwd", [_sds((S, NINP), BF16), _sds((CFK, CFW))], grid=(CFW // LANE,),
                 in_specs=[_col_spec(LANE), CF_VAL, CF_GL, wspec, ANY],
                 out_specs=[pl.BlockSpec((S, 2 * LANE), lambda j: (0, O_GLU // (2 * LANE) + j)), wspec],
                 scratch=[pltpu.VMEM((S + CF_HEAD, LANE), F32), pltpu.VMEM((S + CF_HEAD, LANE), F32)],
                 sem=("parallel",), aliases={4: 0})(du1, proj, proj, conv_w, dproj)


def _cf_ln_fwd(u1, g, b):
    def body(u_ref, g_ref, b_ref, o_ref):
        u = u_ref[...]
        mu = jnp.mean(u, axis=-1, keepdims=True)
        xc = u - mu
        y = xc * lax.rsqrt(jnp.mean(xc * xc, axis=-1, keepdims=True) + EPS)
        o_ref[...] = _silu(y * g_ref[...] + b_ref[...]).astype(BF16)

    return _call(body, "cf_ln_fwd", _sds((S, CFW), BF16), grid=(S // TS,),
                 in_specs=[_row_spec(CFW), _vec_spec(CFW), _vec_spec(CFW)], out_specs=_row_spec(CFW),
                 sem=("parallel",))(u1, g, b)


def _cf_ln_bwd(du3, u1, g, b):
    def body(d_ref, u_ref, g_ref, b_ref, du_ref, acc_ref):
        u = u_ref[...]
        mu = jnp.mean(u, axis=-1, keepdims=True)
        xc = u - mu
        rstd = lax.rsqrt(jnp.mean(xc * xc, axis=-1, keepdims=True) + EPS)
        xh = xc * rstd
        du2 = d_ref[...] * _dsilu(xh * g_ref[...] + b_ref[...])
        dxh = du2 * g_ref[...]
        du_ref[...] = rstd * (dxh - jnp.mean(dxh, axis=-1, keepdims=True)
                              - xh * jnp.mean(dxh * xh, axis=-1, keepdims=True))
        _acc_rows(acc_ref, [jnp.sum(du2 * xh, axis=0, keepdims=True), jnp.sum(du2, axis=0, keepdims=True)])

    return _call(body, "cf_ln_bwd", [_sds((S, CFW)), _sds((8, CFW))], grid=(S // TS,),
                 in_specs=[_row_spec(CFW), _row_spec(CFW), _vec_spec(CFW), _vec_spec(CFW)],
                 out_specs=[_row_spec(CFW), _acc_spec(CFW)], sem=("arbitrary",))(du3, u1, g, b)


FB = 256
FNB = FFN // FB
FF_HEAD = 8


def _ffn_mid_fwd(upall, conv_w):
    def body(gate_ref, up_ref, w_ref, o_ref, pad_ref):
        pad_ref[pl.ds(0, FF_HEAD), :] = jnp.zeros((FF_HEAD, FB), F32)
        pad_ref[pl.ds(FF_HEAD, S), :] = gate_ref[...]
        for r0 in range(0, S, RC):
            gc = _conv_fwd_rows(pad_ref, w_ref, FFK, FF_HEAD, r0)
            o_ref[pl.ds(r0, RC), :] = (_silu(gc) * up_ref[pl.ds(r0, RC), :]).astype(BF16)

    wspec = pl.BlockSpec((FFK, FB), lambda j: (0, j))
    return _call(body, "ffn_mid_fwd", _sds((S, FFN), BF16), grid=(FNB,),
                 in_specs=[_col_spec(FB), _col_spec(FB, FNB), wspec], out_specs=_col_spec(FB),
                 scratch=[pltpu.VMEM((S + FF_HEAD, FB), F32)], sem=("parallel",))(upall, upall, conv_w)


def _ffn_mid_bwd(dh, upall, conv_w):
    def body(d_ref, gate_ref, up_ref, w_ref, dgate_ref, dup_ref, dw_ref, pad_ref, pad2_ref):
        pad_ref[pl.ds(0, FF_HEAD), :] = jnp.zeros((FF_HEAD, FB), F32)
        pad_ref[pl.ds(FF_HEAD, S), :] = gate_ref[...]
        pad2_ref[pl.ds(S, FF_HEAD), :] = jnp.zeros((FF_HEAD, FB), F32)
        for r0 in range(0, S, RC):
            rows = pl.ds(r0, RC)
            gc = _conv_fwd_rows(pad_ref, w_ref, FFK, FF_HEAD, r0)
            dhv = d_ref[rows, :]
            dup_ref[rows, :] = (dhv * _silu(gc)).astype(BF16)
            pad2_ref[rows, :] = dhv * up_ref[rows, :] * _dsilu(gc)
        for r0 in range(0, S, RC):
            dgate_ref[pl.ds(r0, RC), :] = _conv_bwd_rows(pad2_ref, w_ref, FFK, r0).astype(BF16)
        _conv_dw(pad_ref, pad2_ref, dw_ref, FFK, FF_HEAD)

    wspec = pl.BlockSpec((FFK, FB), lambda j: (0, j))
    return _call(body, "ffn_mid_bwd", [_sds((S, FFN), BF16), _sds((S, FFN), BF16), _sds((FFK, FFN))],
                 grid=(FNB,), in_specs=[_col_spec(FB), _col_spec(FB), _col_spec(FB, FNB), wspec],
                 out_specs=[_col_spec(FB), _col_spec(FB), wspec],
                 scratch=[pltpu.VMEM((S + FF_HEAD, FB), F32), pltpu.VMEM((S + FF_HEAD, FB), F32)],
                 sem=("parallel",))(dh, upall, upall, conv_w)


GT = 256
SM_BLK = O_SM // LANE


def _chunk_tri(lower):
    r = lax.broadcasted_iota(jnp.int32, (GT, GT), 0)
    c = lax.broadcasted_iota(jnp.int32, (GT, GT), 1)
    same = (r // CH) == (c // CH)
    tri = (c <= r) if lower else (c >= r)
    return jnp.where(same & tri, 1.0, 0.0).astype(F32)


def _gates_fwd(proj, alog_v, dtb_v):
    def body(sm_ref, al_ref, dt_ref, o_ref):
        lane = lax.broadcasted_iota(jnp.int32, (GT, LANE), 1)
        tri = _chunk_tri(True)
        na = -jnp.exp(al_ref[...])
        for r0 in range(0, S, GT):
            sm = sm_ref[pl.ds(r0, GT), :]
            raw = jnp.where((lane >= H) & (lane < 2 * H), na * _softplus(sm + dt_ref[...]), 0.0)
            gc = _dot(tri, raw, NN, precision=HI)
            o_ref[pl.ds(r0, GT), :] = jnp.where(lane < H, _sigmoid(sm), gc)

    return _call(body, "gates_fwd", _sds((S, LANE)), grid=(1,),
                 in_specs=[pl.BlockSpec((S, LANE), lambda i: (0, SM_BLK)), _vec_spec(LANE), _vec_spec(LANE)],
                 out_specs=pl.BlockSpec((S, LANE), lambda i: (0, 0)), sem=("arbitrary",))(proj, alog_v, dtb_v)


def _gates_bwd(dgb, proj, alog_v, dtb_v, dproj):
    def body(d_ref, sm_ref, al_ref, dt_ref, dproj_in, o_ref, acc_ref):
        lane = lax.broadcasted_iota(jnp.int32, (GT, LANE), 1)
        is_g = (lane >= H) & (lane < 2 * H)
        tri = _chunk_tri(False)
        na = -jnp.exp(al_ref[...])
        d_al = jnp.zeros((1, LANE), F32)
        d_dt = jnp.zeros((1, LANE), F32)
        for r0 in range(0, S, GT):
            sm = sm_ref[pl.ds(r0, GT), :]
            dv = d_ref[pl.ds(r0, GT), :]
            z = sm + dt_ref[...]
            draw = _dot(tri, jnp.where(is_g, dv, 0.0), NN, precision=HI)
            dlogit = jnp.where(is_g, draw * na * _sigmoid(z), 0.0)
            d_al = d_al + jnp.sum(jnp.where(is_g, draw * na * _softplus(z), 0.0), axis=0, keepdims=True)
            d_dt = d_dt + jnp.sum(dlogit, axis=0, keepdims=True)
            bt = _sigmoid(sm)
            o_ref[pl.ds(r0, GT), :] = jnp.where(lane < H, dv * bt * (1.0 - bt), dlogit).astype(BF16)
        acc_ref[...] = jnp.zeros_like(acc_ref)
        acc_ref[0:1, :] = d_al
        acc_ref[1:2, :] = d_dt

    return _call(body, "gates_bwd", [_sds((S, NINP), BF16), _sds((8, LANE))], grid=(1,),
                 in_specs=[pl.BlockSpec((S, LANE), lambda i: (0, 0)), pl.BlockSpec((S, LANE), lambda i: (0, SM_BLK)),
                           _vec_spec(LANE), _vec_spec(LANE), ANY],
                 out_specs=[pl.BlockSpec((S, LANE), lambda i: (0, SM_BLK)), _acc_spec(LANE)],
                 sem=("arbitrary",), aliases={4: 0})(dgb, proj, alog_v, dtb_v, dproj)


HB = 2


def _each(fn, *lists):
    return [fn(*args) for args in zip(*lists)]


def _neumann_inv(a, eye):
    p = _each(lambda m: -m, a)
    t = _each(lambda m: eye + m, p)
    for _ in range(5):
        p = _each(lambda m: _dot(m, m, NN, precision=HI), p)
        t = _each(lambda tt, pp: tt + _dot(tt, pp, NN, precision=HI), t, p)
    return t


def _head_specs():
    q = pl.BlockSpec((S, HB * DH), lambda h: (0, h))
    k = pl.BlockSpec((S, HB * DH), lambda h: (0, H // HB + h))
    v = pl.BlockSpec((S, HB * DH), lambda h: (0, 2 * H // HB + h))
    gb = pl.BlockSpec((HB, S, DH), lambda h: (h, 0, 0))
    gr = pl.BlockSpec((HB, NCH, CH), lambda h: (h, 0, 0))
    return q, k, v, gb, gr


ST_SPEC = pl.BlockSpec((HB, NCH, DH, DH), lambda h: (h, 0, 0, 0))
TM_SPEC = pl.BlockSpec((HB, NCH, CH, CH), lambda h: (h, 0, 0, 0))
HCOL = pl.BlockSpec((S, HB * DH), lambda h: (0, h))


def _delta_fwd(qkvn, gb, gr, bb):
    def body(q_ref, k_ref, v_ref, gb_ref, gr_ref, bb_ref, o_ref, st_ref, tm_ref):
        ri = lax.broadcasted_iota(jnp.int32, (CH, CH), 0)
        ci = lax.broadcasted_iota(jnp.int32, (CH, CH), 1)
        strict = ri > ci
        causal = ri >= ci
        eye = jnp.where(ri == ci, 1.0, 0.0).astype(F32)

        hs = list(range(HB))
        cols = [slice(hh * DH, (hh + 1) * DH) for hh in hs]
        bf = lambda m: m.astype(BF16)

        def local(n):
            rows = pl.ds(pl.multiple_of(n * CH, CH), CH)
            c = dict(rows=rows, n=n)
            c["q"] = [q_ref[rows, cc] for cc in cols]
            c["k"] = [k_ref[rows, cc] for cc in cols]
            c["v"] = [v_ref[rows, cc] for cc in cols]
            c["g"] = [gb_ref[hh, rows, :] for hh in hs]
            c["beta"] = [bb_ref[hh, rows, :] for hh in hs]
            diff = [c["g"][hh][:, :CH] - gr_ref[hh, pl.ds(n, 1), :] for hh in hs]
            c["el"] = _each(lambda d: jnp.exp(jnp.where(causal, d, 0.0)), diff)
            c["eg"] = _each(jnp.exp, c["g"])
            c["gl"] = _each(lambda m: m[CH - 1:CH, :], c["g"])
            c["kb"] = _each(lambda x, y: x * y, c["k"], c["beta"])
            c["kbf"] = _each(bf, c["k"])
            c["a"] = _each(lambda x, y, e: jnp.where(strict, _dot(bf(x), y, NT) * e, 0.0), c["kb"], c["kbf"], c["el"])
            return c

        def advance(c, t, sts):
            n, rows = c["n"], c["rows"]
            for hh in hs:
                tm_ref[hh, n] = t[hh]
                st_ref[hh, n] = sts[hh]
            sb = _each(bf, sts)
            r = _each(lambda vv, bb_, kk, ee, ss: vv * bb_ - _dot(bf(kk * ee), ss, NN), c["v"], c["beta"], c["kb"], c["eg"], sb)
            ub = _each(lambda tt, rr: bf(_dot(tt, rr, NN, precision=HI)), t, r)
            p = _each(lambda qq, kk, e: jnp.where(causal, _dot(bf(qq), kk, NT) * e, 0.0), c["q"], c["kbf"], c["el"])
            o = _each(lambda qq, ee, ss, pp, uu: _dot(bf(qq * ee), ss, NN) + _dot(bf(pp), uu, NN), c["q"], c["eg"], sb, p, ub)
            for hh in hs:
                o_ref[rows, cols[hh]] = o[hh]
            kd = _each(lambda kk, l, gg: kk * jnp.exp(l - gg), c["k"], c["gl"], c["g"])
            return _each(lambda st, l, kk, uu: st * jnp.exp(l) + _dot(bf(kk), uu, TN), sts, c["gl"], kd, ub)

        def step(i, sts):
            c0, c1 = local(2 * i), local(2 * i + 1)
            t = _neumann_inv(c0["a"] + c1["a"], eye)
            sts = advance(c0, t[:HB], list(sts))
            return tuple(advance(c1, t[HB:], sts))

        lax.fori_loop(0, NCH // 2, step, tuple(jnp.zeros((DH, DH), F32) for _ in hs))

    q, k, v, gbs, grs = _head_specs()
    return _call(body, "delta_fwd", [_sds((S, DNW)), _sds((H, NCH, DH, DH)), _sds((H, NCH, CH, CH))], grid=(H // HB,),
                 in_specs=[q, k, v, gbs, grs, gbs], out_specs=[HCOL, ST_SPEC, TM_SPEC],
                 sem=("parallel",))(qkvn, qkvn, qkvn, gb, gr, bb)


def _delta_bwd(qkvn, gb, gr, bb, st_all, tm_all, do_all):
    def body(q_ref, k_ref, v_ref, gb_ref, gr_ref, bb_ref, st_ref, tm_ref, do_ref,
             dq_ref, dk_ref, dv_ref, dg_ref, db_ref):
        ri = lax.broadcasted_iota(jnp.int32, (CH, CH), 0)
        ci = lax.broadcasted_iota(jnp.int32, (CH, CH), 1)
        lo_s, lo_c, up_s, up_c = ri > ci, ri >= ci, ri < ci, ri <= ci
        last_row = lax.broadcasted_iota(jnp.int32, (CH, 1), 0) == CH - 1

        def rs(mat):
            return jnp.sum(mat, axis=1, keepdims=True)

        def total(mat):
            return jnp.sum(rs(mat), axis=0, keepdims=True)

        hs = list(range(HB))
        cols = [slice(hh * DH, (hh + 1) * DH) for hh in hs]
        bf = lambda m: m.astype(BF16)
        mul = lambda x, y: x * y
        spread = jnp.full((8, DH), 1.0 / DH, F32)

        def as_row(col):
            return _dot(spread, jnp.broadcast_to(col, (CH, DH)), NT, precision=HI)[0:1, :]

        def step(i, dss):
            ns = [NCH - 1 - 2 * i, NCH - 2 - 2 * i]
            rws = [pl.ds(pl.multiple_of(n * CH, CH), CH) for n in ns]
            idx = [(cc, hh) for cc in range(2) for hh in hs]
            q = [q_ref[rws[cc], cols[hh]] for cc, hh in idx]
            k = [k_ref[rws[cc], cols[hh]] for cc, hh in idx]
            v = [v_ref[rws[cc], cols[hh]] for cc, hh in idx]
            do = [do_ref[rws[cc], cols[hh]] for cc, hh in idx]
            g = [gb_ref[hh, rws[cc], :] for cc, hh in idx]
            beta = [bb_ref[hh, rws[cc], :] for cc, hh in idx]
            t = [tm_ref[hh, ns[cc]] for cc, hh in idx]
            st = [st_ref[hh, ns[cc]] for cc, hh in idx]
            diff = [gg[:, :CH] - gr_ref[hh, pl.ds(ns[cc], 1), :] for gg, (cc, hh) in zip(g, idx)]
            el = _each(lambda d: jnp.exp(jnp.where(lo_c, d, 0.0)), diff)
            eu = _each(lambda d: jnp.exp(jnp.where(up_c, -d, 0.0)), diff)
            eg = _each(jnp.exp, g)
            gl = _each(lambda m: m[CH - 1:CH, :], g)
            egl = _each(jnp.exp, gl)
            ekd = _each(lambda l, m: jnp.exp(l - m), gl, g)
            kb = _each(mul, k, beta)
            kbg = _each(mul, kb, eg)
            qg = _each(mul, q, eg)
            kd = _each(mul, k, ekd)
            qb, kbf, kbb = _each(bf, q), _each(bf, k), _each(bf, kb)
            kbgb, qgb, kdb = _each(bf, kbg), _each(bf, qg), _each(bf, kd)
            sb, dob = _each(bf, st), _each(bf, do)
            r = _each(lambda vv, b, x, s: vv * b - _dot(x, s, NN), v, beta, kbgb, sb)
            u = _each(lambda tt, rr: _dot(tt, rr, NN, precision=HI), t, r)
            ub = _each(bf, u)
            kk = _each(lambda x, y: _dot(x, y, NT), kbb, kbf)
            qk = _each(lambda x, y: _dot(x, y, NT), qb, kbf)
            kkt = _each(lambda x, y: _dot(x, y, NT), kbf, kbb)
            qkt = _each(lambda x, y: _dot(x, y, NT), kbf, qb)
            pt = _each(lambda m, e: jnp.where(up_c, m * e, 0.0), qkt, eu)
            ds, du, dr, drb, ds_new = [], [], [], [], list(dss)
            for cc in range(2):
                sl = slice(cc * HB, (cc + 1) * HB)
                ds_c = ds_new
                dsb_c = _each(bf, ds_c)
                du_c = _each(lambda p, d, x, s: _dot(bf(p), d, NN) + _dot(x, s, NN), pt[sl], dob[sl], kdb[sl], dsb_c)
                dr_c = _each(lambda tt, d: _dot(tt, d, TN, precision=HI), t[sl], du_c)
                drb_c = _each(bf, dr_c)
                ds_new = _each(lambda x, d, e, s, y, z: _dot(x, d, TN) + e * s - _dot(y, z, TN),
                               qgb[sl], dob[sl], egl[sl], ds_c, kbgb[sl], drb_c)
                ds, du, dr, drb = ds + ds_c, du + du_c, dr + dr_c, drb + drb_c
            dsb = _each(bf, ds)
            dpg = _each(lambda d, uu, e: jnp.where(lo_c, _dot(d, uu, NT), 0.0) * e, dob, ub, el)
            dpgt = _each(lambda uu, d, e: jnp.where(up_c, _dot(uu, d, NT), 0.0) * e, ub, dob, eu)
            dag = _each(lambda d, uu, e: -jnp.where(lo_s, _dot(d, uu, NT), 0.0) * e, drb, ub, el)
            dagt = _each(lambda uu, d, e: -jnp.where(up_s, _dot(uu, d, NT), 0.0) * e, ub, drb, eu)
            dqg = _each(lambda d, s: _dot(d, s, NT), dob, sb)
            dkbg = _each(lambda d, s: -_dot(d, s, NT), drb, sb)
            dkd = _each(lambda uu, s: _dot(uu, s, NT), ub, dsb)
            dkb =_each(lambda a, x, y, e: _dot(bf(a), x, NN) + y * e, dag, kbf, dkbg, eg)
            dk = _each(lambda a, x, p, y, z, e, w, b: _dot(bf(a), x, NN) + _dot(bf(p), y, NN) + z * e + w * b,
                       dagt, kbb, dpgt, qb, dkd, ekd, dkb, beta)
            dq = _each(lambda p, x, y, e: _dot(bf(p), x, NN) + y * e, dpg, kbf, dqg, eg)
            dkd_kd = _each(lambda x, y: rs(x * y), dkd, kd)
            dg = _each(lambda a, x, p, y, at, xt, pt_, yt, z, w, c, d, e:
                       rs(a * x + p * y) - rs(at * xt + pt_ * yt) + rs(z * w) + rs(c * d) - e,
                       dag, kk, dpg, qk, dagt, kkt, dpgt, qkt, dqg, qg, dkbg, kbg, dkd_kd)
            dgl = _each(lambda x, e, s, y: jnp.sum(x, axis=0, keepdims=True) + e[:, 0:1] * total(s * y), dkd_kd, egl, ds, st)
            dg = _each(lambda x, y: x + jnp.where(last_row, y, 0.0), dg, dgl)
            dbeta = _each(lambda x, y, z, w: rs(x * y) + rs(z * w), dkb, k, dr, v)
            for j, (cc, hh) in enumerate(idx):
                dq_ref[rws[cc], cols[hh]] = dq[j]
                dk_ref[rws[cc], cols[hh]] = dk[j]
                dv_ref[rws[cc], cols[hh]] = dr[j] * beta[j]
                dg_ref[hh, pl.ds(ns[cc], 1), :] = as_row(dg[j])
                db_ref[hh, pl.ds(ns[cc], 1), :] = as_row(dbeta[j])
            return tuple(ds_new)

        lax.fori_loop(0, NCH // 2, step, tuple(jnp.zeros((DH, DH), F32) for _ in hs))

    q, k, v, gbs, grs = _head_specs()
    return _call(body, "delta_bwd",
                 [_sds((S, DNW)), _sds((S, DNW)), _sds((S, DNW)), _sds((H, NCH, CH)), _sds((H, NCH, CH))], grid=(H // HB,),
                 in_specs=[q, k, v, gbs, grs, gbs, ST_SPEC, TM_SPEC, HCOL], out_specs=[HCOL, HCOL, HCOL, grs, grs],
                 sem=("parallel",))(qkvn, qkvn, qkvn, gb, gr, bb, st_all, tm_all, do_all)


Z_BLK = O_Z // DNW


def _dn_post_fwd(o, proj, gn):
    def body(o_ref, z_ref, gn_ref, og_ref):
        for h in range(H):
            cols = slice(h * DH, (h + 1) * DH)
            ov = o_ref[:, cols]
            on = ov * lax.rsqrt(jnp.mean(ov * ov, axis=-1, keepdims=True) + EPS) * gn_ref[...]
            og_ref[:, cols] = (on * _silu(z_ref[:, cols])).astype(BF16)

    return _call(body, "dn_post_fwd", _sds((S, DNW), BF16), grid=(S // TS,),
                 in_specs=[_row_spec(DNW), pl.BlockSpec((TS, DNW), lambda i: (i, Z_BLK)), _vec_spec(DH)],
                 out_specs=_row_spec(DNW), sem=("parallel",))(o, proj, gn)


def _dn_post_bwd(dog, o, proj, gn, dproj):
    def body(d_ref, o_ref, z_ref, gn_ref, dproj_in, do_ref, dz_ref, acc_ref):
        dgn = jnp.zeros((1, DH), F32)
        for h in range(H):
            cols = slice(h * DH, (h + 1) * DH)
            ov, zv, dv = o_ref[:, cols], z_ref[:, cols], d_ref[:, cols]
            rinv = lax.rsqrt(jnp.mean(ov * ov, axis=-1, keepdims=True) + EPS)
            xn = ov * rinv
            don = dv * _silu(zv)
            dz_ref[:, cols] = (dv * xn * gn_ref[...] * _dsilu(zv)).astype(BF16)
            dgn = dgn + jnp.sum(don * xn, axis=0, keepdims=True)
            dxn = don * gn_ref[...]
            do_ref[:, cols] = rinv * (dxn - xn * jnp.mean(dxn * xn, axis=-1, keepdims=True))
        _acc_rows(acc_ref, [dgn])

    zspec = pl.BlockSpec((TS, DNW), lambda i: (i, Z_BLK))
    return _call(body, "dn_post_bwd", [_sds((S, DNW)), _sds((S, NINP), BF16), _sds((8, DH))], grid=(S // TS,),
                 in_specs=[_row_spec(DNW), _row_spec(DNW), zspec, _vec_spec(DH), ANY],
                 out_specs=[_row_spec(DNW), zspec, _acc_spec(DH)], sem=("arbitrary",),
                 aliases={4: 1})(dog, o, proj, gn, dproj)


GA_BLK = O_GA // D
GB_BLK = O_GB // D


def _merge_fwd(ba, bb, proj):
    def body(a_ref, b_ref, ga_ref, gb_ref, o_ref):
        o_ref[...] = (_sigmoid(ga_ref[...]) * a_ref[...] + _sigmoid(gb_ref[...]) * b_ref[...]).astype(BF16)

    return _call(body, "merge_fwd", _sds((S, D), BF16), grid=(S // TS,),
                 in_specs=[_row_spec(), _row_spec(), pl.BlockSpec((TS, D), lambda i: (i, GA_BLK)),
                           pl.BlockSpec((TS, D), lambda i: (i, GB_BLK))],
                 out_specs=_row_spec(), sem=("parallel",))(ba, bb, proj, proj)


def _merge_bwd(dm, ba, bb, proj, dproj):
    def body(d_ref, a_ref, b_ref, ga_ref, gb_ref, dproj_in, dg_ref, da_ref, db_ref):
        d = d_ref[...]
        sa, sb = _sigmoid(ga_ref[...]), _sigmoid(gb_ref[...])
        dg_ref[:, 0:D] = (d * a_ref[...] * sa * (1.0 - sa)).astype(BF16)
        dg_ref[:, D:2 * D] = (d * b_ref[...] * sb * (1.0 - sb)).astype(BF16)
        da_ref[...] = (d * sa).astype(BF16)
        db_ref[...] = (d * sb).astype(BF16)

    return _call(body, "merge_bwd", [_sds((S, NINP), BF16), _sds((S, D), BF16), _sds((S, D), BF16)], grid=(S // TS,),
                 in_specs=[_row_spec(), _row_spec(), _row_spec(), pl.BlockSpec((TS, D), lambda i: (i, GA_BLK)),
                           pl.BlockSpec((TS, D), lambda i: (i, GB_BLK)), ANY],
                 out_specs=[pl.BlockSpec((TS, 2 * D), lambda i: (i, O_GA // (2 * D))), _row_spec(), _row_spec()],
                 sem=("parallel",), aliases={5: 0})(dm, ba, bb, proj, proj, dproj)


NSH = NIN // NDEV


def _pad_win(wt):
    nb, cols = CFW // LANE, wt.shape[1]
    glu = wt[4112:4112 + 2 * CFW].reshape(2, nb, LANE, cols)
    glu = jnp.transpose(glu, (1, 0, 2, 3)).reshape(2 * CFW, cols)
    return jnp.concatenate([wt[0:4096], wt[6160:6160 + 2 * D], glu, wt[4096:4112],
                            jnp.zeros((NINP - NIN, cols), wt.dtype)], axis=0)


def _unpad_win(gpt):
    nb, cols = CFW // LANE, gpt.shape[1]
    glu = gpt[O_GLU:O_GLU + 2 * CFW].reshape(nb, 2, LANE, cols)
    glu = jnp.transpose(glu, (1, 0, 2, 3)).reshape(2 * CFW, cols)
    return jnp.concatenate([gpt[0:4096], gpt[O_SM:O_SM + 16], glu, gpt[O_GA:O_GA + 2 * D]], axis=0)


def _lane_vec(v8, offset):
    return jnp.pad(v8, ((0, 0), (offset, LANE - 8 - offset)))


def _tie(vec, token):
    return vec + token


def _local_step(x, tgt, mod, norm1_g, norm2_g, final_g, w_in_p, dn_conv_w, a_log, dt_bias, dn_norm_g,
                cf_conv_w, cf_ln_g, cf_ln_b, ffn_conv_w, comm):
    sh1, sc1, gt1, sh2, sc2, gt2 = (mod[:, i * D:(i + 1) * D] for i in range(6))
    alog_v, dtb_v = _lane_vec(a_log, H), _lane_vec(dt_bias, H)

    hn1 = _norm_mod(x, norm1_g, sc1, _tie(sh1, comm.token0), "norm_mod1")
    proj = _mm(hn1, w_in_p, "nt", F32, "mm_in", tn=1152)
    qkvn = _dn_pre_fwd(proj, dn_conv_w)
    gates = _gates_fwd(proj, alog_v, dtb_v)
    beta_t = gates[:, 0:H].T
    g_t = gates[:, H:2 * H].T
    gb = jnp.broadcast_to(g_t[:, :, None], (H, S, DH))
    bb = jnp.broadcast_to(beta_t[:, :, None], (H, S, DH))
    gr = g_t.reshape(H, NCH, CH)
    o, st_all, tm_all = _delta_fwd(qkvn, gb, gr, bb)
    og = _dn_post_fwd(o, proj, dn_norm_g)
    u1 = _cf_conv_fwd(proj, cf_conv_w)
    u3 = _cf_ln_fwd(u1, cf_ln_g, cf_ln_b)
    after = og[0:8, 0:LANE].astype(F32) + u3[0:8, 0:LANE].astype(F32)
    dn_w_o, cf_w_o, w_out, ffn_w_up, ffn_w_down = comm.late_weights(after)
    br_a = _mm(og, dn_w_o, "nn", F32, "mm_dn_o")
    br_b = _mm(u3, cf_w_o, "nn", F32, "mm_cf_o")
    merged = _merge_fwd(br_a, br_b, proj)
    mix = _mm(merged, w_out, "nn", F32, "mm_out")
    x2, hn2 = _resid_norm_mod(x, mix, gt1, norm2_g, sc2, sh2, "resid_norm_mod2")
    upall = _mm(hn2, ffn_w_up, "nn", F32, "mm_up")
    hmid = _ffn_mid_fwd(upall, ffn_conv_w)
    f = _mm(hmid, ffn_w_down, "nn", F32, "mm_down", tm=2048)

    dx3, df, acc_f = _loss_head(x2, f, tgt, gt2, final_g)
    d_final_g, d_gt2, loss = acc_f[0:1], acc_f[1:2], acc_f[2:3, 0:1]
    dhmid = _mm(df, ffn_w_down, "nt", F32, "mm_down_dx")
    g_w_down = _mm(hmid, df, "tn", BF16, "mm_down_dw", tm=FFN // 4)
    d_gate, d_up, g_ffn_conv = _ffn_mid_bwd(dhmid, upall, ffn_conv_w)
    g_w_up = _mm(hn2, d_gate, "tn", BF16, "mm_up_dw", tn=2 * FFN // NDEV, b2=d_up)
    tok_a = comm.grads_begin("a", dict(ffn_w_down=g_w_down, ffn_w_up=g_w_up))
    dhn2 = _mm(d_gate, ffn_w_up, "nt", F32, "mm_up_dx", a2=d_up, dep=jnp.broadcast_to(tok_a, (8, LANE)))
    tok_a = comm.grads_continue("a", dhn2)
    dx2, dmix, acc2 = _norm_mod_bwd(dhn2, x2, dx3, _tie(norm2_g, tok_a), sc2, "norm_mod2_bwd", mix=mix, gt=gt1)
    d_sh2, d_sc2, d_norm2_g, d_gt1 = acc2[0:1], acc2[1:2], acc2[2:3], acc2[3:4]
    dmerged = _mm(dmix, w_out, "nt", F32, "mm_out_dx")
    g_w_out = _mm(merged, dmix, "tn", BF16, "mm_out_dw")
    d_proj, d_bra, d_brb = _merge_bwd(dmerged, br_a, br_b, proj, lax.empty((S, NINP), BF16))
    du3 = _mm(d_brb, cf_w_o, "nt", F32, "mm_cf_o_dx")
    g_cf_w_o = _mm(u3, d_brb, "tn", BF16, "mm_cf_o_dw")
    du1, acc_ln = _cf_ln_bwd(du3, u1, cf_ln_g, cf_ln_b)
    d_proj, g_cf_conv = _cf_conv_bwd(du1, proj, cf_conv_w, d_proj)
    dog = _mm(d_bra, dn_w_o, "nt", F32, "mm_dn_o_dx")
    g_dn_w_o = _mm(og, d_bra, "tn", BF16, "mm_dn_o_dw")
    tok_b = comm.grads_begin("b", dict(w_out=g_w_out, cf_w_o=g_cf_w_o, dn_w_o=g_dn_w_o, ffn_conv_w=g_ffn_conv,
                                       cf_conv_w=g_cf_conv))
    do, d_proj, acc_gn = _dn_post_bwd(dog, o, proj, _tie(dn_norm_g, tok_b), d_proj)
    tok_b = comm.grads_continue("b", do)
    dq, dk, dv, dgr, dbr = _delta_bwd(qkvn, gb, _tie(gr, tok_b), bb, st_all, tm_all, do)
    d_proj, g_dn_conv = _dn_pre_bwd(dq, dk, dv, proj, dn_conv_w, d_proj)
    dgates = jnp.concatenate([dbr.reshape(H, S).T, dgr.reshape(H, S).T, jnp.zeros((S, LANE - 2 * H), F32)], axis=1)
    d_proj, acc_g = _gates_bwd(dgates, proj, alog_v, dtb_v, d_proj)
    g_w_in_p = _mm(d_proj, hn1, "tn", BF16, "mm_in_dw", tm=1152)
    comm.grads_begin("c", dict(w_in=g_w_in_p, dn_conv_w=g_dn_conv))
    tok_c = comm.grads_continue("c", g_dn_conv)
    dhn1 = _mm(d_proj, w_in_p, "nn", F32, "mm_in_dx", tk=NINP // 3, dep=jnp.broadcast_to(tok_c, (8, LANE)))
    grad_x, acc1 = _norm_mod_bwd(dhn1, x, dx2, norm1_g, sc1, "norm_mod1_bwd")
    d_sh1, d_sc1, d_norm1_g = acc1[0:1], acc1[1:2], acc1[2:3]

    d_mod = jnp.concatenate([d_sh1, d_sc1, d_gt1, d_sh2, d_sc2, d_gt2], axis=1)
    small = dict(mod=d_mod, norm1_g=d_norm1_g, norm2_g=d_norm2_g, final_norm_g=d_final_g,
                 cf_ln_g=acc_ln[0:1], cf_ln_b=acc_ln[1:2], dn_norm_g=acc_gn[0:1],
                 dn_a_log=acc_g[0:1, H:2 * H], dn_dt_bias=acc_g[1:2, H:2 * H])
    return loss, grad_x, small


def _dev_index(px, py, pc):
    return 4 * px + 2 * py + pc


def _all_gather(arrs, name):
    n = len(arrs)

    def body(*refs):
        ins, outs = refs[:n], refs[n:2 * n]
        send_sems, recv_sems, loc_sems = refs[2 * n:]
        x, y, c = _my_pos()
        me, sib = (x, y, c), (x, y, 1 - c)
        chips = [(1 - x, y), (x, 1 - y), (1 - x, 1 - y)]

        def cp(i, k, block, to, src=None):
            dst = outs[i].at[_dev_index(*block)]
            return pltpu.make_async_remote_copy(
                src_ref=dst if src is None else src, dst_ref=dst, send_sem=send_sems.at[i, k],
                recv_sem=recv_sems.at[i, k], device_id=to, device_id_type=MESH)

        mine = [pltpu.make_async_copy(ins[i], outs[i].at[_dev_index(*me)], loc_sems.at[i]) for i in range(n)]
        for m in mine:
            m.start()
        sent = []
        for i in range(n):
            sent.append(cp(i, 0, me, sib, src=ins[i]))
            sent += [cp(i, 1 + j, me, (*chip, c), src=ins[i]) for j, chip in enumerate(chips)]
        for s in sent:
            s.start()
        for i in range(n):
            for j, chip in enumerate(chips):
                cp(i, 1 + j, (*chip, c), me).wait_recv()
                fwd = cp(i, 4 + j, (*chip, c), sib)
                fwd.start()
                sent.append(fwd)
        for i in range(n):
            cp(i, 0, sib, me).wait_recv()
            for j, chip in enumerate(chips):
                cp(i, 4 + j, (*chip, 1 - c), me).wait_recv()
        for s in sent:
            s.wait_send()
        for m in mine:
            m.wait()

    outs = pl.pallas_call(
        body, out_shape=[_sds((NDEV,) + a.shape, a.dtype) for a in arrs], in_specs=[ANY] * n, out_specs=[ANY] * n,
        scratch_shapes=[pltpu.SemaphoreType.DMA((n, 7)), pltpu.SemaphoreType.DMA((n, 7)), pltpu.SemaphoreType.DMA((n,))],
        name=name)(*arrs)
    return list(outs)


def _slab(ref, layout, idx):
    kind, n = layout
    if kind == "rows":
        return ref.at[pl.ds(pl.multiple_of(idx * n, n), n), :]
    if kind == "cols":
        return ref.at[:, pl.ds(pl.multiple_of(idx * n, n), n)]
    return ref.at[idx]


def _slab_shape(arr, layout):
    kind, n = layout
    if kind == "rows":
        return (n, arr.shape[1])
    if kind == "cols":
        return (arr.shape[0], n)
    return tuple(arr.shape[1:])


HBM = pl.BlockSpec(memory_space=pltpu.HBM)
SEMS = pl.BlockSpec(memory_space=pltpu.SEMAPHORE)
EFFECT = pltpu.SideEffectType.DATAFLOW_SIDE_EFFECTING
TOKEN = jax.ShapeDtypeStruct((8, LANE), F32)


def _hbm(a):
    return pltpu.with_memory_space_constraint(a, pltpu.HBM)


def _gather_ici_copy(shard_ref, buf_ref, layout, send_sems, recv_sems, i, j, me, chip, c):
    return pltpu.make_async_remote_copy(
        src_ref=shard_ref, dst_ref=_slab(buf_ref, layout, me), send_sem=send_sems.at[3 * i + j],
        recv_sem=recv_sems.at[3 * i + j], device_id=(*chip, c), device_id_type=MESH)


def _gather_ici_start(shards, bufs, layouts, after, name):
    n = len(shards)

    def body(*refs):
        sh, bf = refs[:n], refs[n:2 * n]
        send_sems, recv_sems = refs[2 * n + 1], refs[2 * n + 2]
        token = refs[-1]
        x, y, c = _my_pos()
        me = _dev_index(x, y, c)
        for i in range(n):
            for j, chip in enumerate([(1 - x, y), (x, 1 - y), (1 - x, 1 - y)]):
                _gather_ici_copy(sh[i], bf[i], layouts[i], send_sems, recv_sems, i, j, me, chip, c).start()
        token[...] = jnp.zeros_like(token)

    outs = pl.pallas_call(
        body, name=name,
        out_shape=(pltpu.SemaphoreType.DMA((3 * n,)), pltpu.SemaphoreType.DMA((3 * n,)),
                   *[pltpu.HBM(a.shape, a.dtype) for a in shards], *[pltpu.HBM(a.shape, a.dtype) for a in bufs], TOKEN),
        in_specs=[HBM] * (2 * n) + [ANY],
        out_specs=(SEMS, SEMS, *[HBM] * (2 * n), pl.BlockSpec(memory_space=pltpu.VMEM)),
        input_output_aliases={i: 2 + i for i in range(2 * n)},
        compiler_params=pltpu.CompilerParams(has_side_effects=EFFECT),
    )(*[_hbm(a) for a in shards], *[_hbm(a) for a in bufs], after)
    return outs[0], outs[1], list(outs[2:2 + n]), list(outs[2 + n:2 + 2 * n]), outs[-1]


def _gather_ici_wait(send_sems, recv_sems, shards, bufs, layouts, after, name):
    n = len(shards)

    def body(*refs):
        sh, bf = refs[:n], refs[n:2 * n]
        ssem, rsem = refs[2 * n], refs[2 * n + 1]
        x, y, c = _my_pos()
        me = _dev_index(x, y, c)
        for i in range(n):
            for j, chip in enumerate([(1 - x, y), (x, 1 - y), (1 - x, 1 - y)]):
                cp = _gather_ici_copy(sh[i], bf[i], layouts[i], ssem, rsem, i, j, me, chip, c)
                cp.wait_send()
                cp.wait_recv()

    outs = pl.pallas_call(
        body, name=name,
        out_shape=(*[pltpu.HBM(a.shape, a.dtype) for a in shards], *[pltpu.HBM(a.shape, a.dtype) for a in bufs]),
        in_specs=[HBM] * (2 * n) + [SEMS, SEMS, ANY], out_specs=tuple([HBM] * (2 * n)),
        input_output_aliases={i: i for i in range(2 * n)},
        compiler_params=pltpu.CompilerParams(has_side_effects=EFFECT),
    )(*shards, *bufs, send_sems, recv_sems, after)
    return list(outs[:n]), list(outs[n:])


def _place_own(pos, shard, buf, layout, name):
    kind, n = layout
    r, cols = shard.shape
    tr = _row_tile(r, shard.dtype.itemsize)
    nr = r // tr
    if kind == "rows":
        ospec = pl.BlockSpec((tr, cols), lambda i, p: (p[2] * nr + i, 0))
    else:
        assert kind == "lead"
        ospec = pl.BlockSpec((None, tr, cols), lambda i, p: (p[2], i, 0))

    def body(pos_ref, s_ref, buf_in, o_ref):
        o_ref[...] = s_ref[...]

    return pl.pallas_call(
        body, out_shape=_sds(buf.shape, buf.dtype), name=name, input_output_aliases={2: 0},
        grid_spec=pltpu.PrefetchScalarGridSpec(
            num_scalar_prefetch=1, grid=(nr,), in_specs=[pl.BlockSpec((tr, cols), lambda i, p: (i, 0)), ANY],
            out_specs=ospec),
        compiler_params=pltpu.CompilerParams(dimension_semantics=("parallel",), vmem_limit_bytes=VMEM_LIMIT),
    )(pos, shard, buf)


def _gather_pair(shards, bufs, layouts, name):
    n = len(shards)

    def body(*refs):
        sh, bo = refs[:n], refs[2 * n:3 * n]
        send_sems, recv_sems = refs[3 * n:]
        x, y, c = _my_pos()
        sib = (x, y, 1 - c)
        copies = []
        for i in range(n):
            for k, (px, py) in enumerate([(x, y), (1 - x, y), (x, 1 - y), (1 - x, 1 - y)]):
                slab = _slab(bo[i], layouts[i], _dev_index(px, py, c))
                copies.append(pltpu.make_async_remote_copy(
                    src_ref=sh[i] if k == 0 else slab, dst_ref=slab, send_sem=send_sems.at[i, k],
                    recv_sem=recv_sems.at[i, k], device_id=sib, device_id_type=MESH))
        for cpy in copies:
            cpy.start()
        for cpy in copies:
            cpy.wait()

    outs = pl.pallas_call(
        body, out_shape=[_sds(a.shape, a.dtype) for a in bufs], in_specs=[ANY] * (2 * n), out_specs=[ANY] * n,
        input_output_aliases={n + i: i for i in range(n)},
        scratch_shapes=[pltpu.SemaphoreType.DMA((n, 4)), pltpu.SemaphoreType.DMA((n, 4))], name=name)(*shards, *bufs)
    return list(outs)


def _pair_copy(part_ref, land_ref, layout, send_sems, recv_sems, i, q, x, y, c):
    return pltpu.make_async_remote_copy(
        src_ref=_slab(part_ref, layout, 2 * q + (1 - c)), dst_ref=land_ref.at[q], send_sem=send_sems.at[4 * i + q],
        recv_sem=recv_sems.at[4 * i + q], device_id=(x, y, 1 - c), device_id_type=MESH)


def _pair_exchange_start(parts, layouts, name):
    n = len(parts)
    lands = [lax.empty((4,) + _slab_shape(p, lay), p.dtype) for p, lay in zip(parts, layouts)]

    def body(*refs):
        pt, ld = refs[:n], refs[n:2 * n]
        send_sems, recv_sems = refs[2 * n], refs[2 * n + 1]
        token = refs[-1]
        x, y, c = _my_pos()
        for i in range(n):
            for q in range(4):
                _pair_copy(pt[i], ld[i], layouts[i], send_sems, recv_sems, i, q, x, y, c).start()
        token[...] = jnp.zeros_like(token)

    outs = pl.pallas_call(
        body, name=name,
        out_shape=(pltpu.SemaphoreType.DMA((4 * n,)), pltpu.SemaphoreType.DMA((4 * n,)),
                   *[pltpu.HBM(a.shape, a.dtype) for a in parts], *[pltpu.HBM(a.shape, a.dtype) for a in lands], TOKEN),
        in_specs=[HBM] * (2 * n), out_specs=(SEMS, SEMS, *[HBM] * (2 * n), pl.BlockSpec(memory_space=pltpu.VMEM)),
        input_output_aliases={i: 2 + i for i in range(2 * n)},
        compiler_params=pltpu.CompilerParams(has_side_effects=EFFECT),
    )(*[_hbm(a) for a in parts], *[_hbm(a) for a in lands])
    return outs[0], outs[1], list(outs[2:2 + n]), list(outs[2 + n:2 + 2 * n]), outs[-1]


def _pair_exchange_wait(send_sems, recv_sems, parts, lands, layouts, after, name):
    n = len(parts)

    def body(*refs):
        pt, ld = refs[:n], refs[n:2 * n]
        ssem, rsem = refs[2 * n], refs[2 * n + 1]
        x, y, c = _my_pos()
        for i in range(n):
            for q in range(4):
                cp = _pair_copy(pt[i], ld[i], layouts[i], ssem, rsem, i, q, x, y, c)
                cp.wait_send()
                cp.wait_recv()

    outs = pl.pallas_call(
        body, name=name,
        out_shape=(*[pltpu.HBM(a.shape, a.dtype) for a in parts], *[pltpu.HBM(a.shape, a.dtype) for a in lands]),
        in_specs=[HBM] * (2 * n) + [SEMS, SEMS, ANY], out_specs=tuple([HBM] * (2 * n)),
        input_output_aliases={i: i for i in range(2 * n)},
        compiler_params=pltpu.CompilerParams(has_side_effects=EFFECT),
    )(*parts, *lands, send_sems, recv_sems, after)
    return list(outs[:n]), list(outs[n:])


def _chip_copy(sum_ref, land_ref, send_sems, recv_sems, i, j, chip, c):
    return pltpu.make_async_remote_copy(
        src_ref=sum_ref.at[2 * chip[0] + chip[1]], dst_ref=land_ref.at[j], send_sem=send_sems.at[3 * i + j],
        recv_sem=recv_sems.at[3 * i + j], device_id=(*chip, c), device_id_type=MESH)


def _chip_exchange_start(sums, name):
    n = len(sums)
    lands = [lax.empty((3,) + s.shape[1:], s.dtype) for s in sums]

    def body(*refs):
        sm, ld = refs[:n], refs[n:2 * n]
        send_sems, recv_sems = refs[2 * n], refs[2 * n + 1]
        token = refs[-1]
        x, y, c = _my_pos()
        for i in range(n):
            for j, chip in enumerate([(1 - x, y), (x, 1 - y), (1 - x, 1 - y)]):
                _chip_copy(sm[i], ld[i], send_sems, recv_sems, i, j, chip, c).start()
        token[...] = jnp.zeros_like(token)

    outs = pl.pallas_call(
        body, name=name,
        out_shape=(pltpu.SemaphoreType.DMA((3 * n,)), pltpu.SemaphoreType.DMA((3 * n,)),
                   *[pltpu.HBM(a.shape, a.dtype) for a in sums], *[pltpu.HBM(a.shape, a.dtype) for a in lands], TOKEN),
        in_specs=[HBM] * (2 * n), out_specs=(SEMS, SEMS, *[HBM] * (2 * n), pl.BlockSpec(memory_space=pltpu.VMEM)),
        input_output_aliases={i: 2 + i for i in range(2 * n)},
        compiler_params=pltpu.CompilerParams(has_side_effects=EFFECT),
    )(*[_hbm(a) for a in sums], *[_hbm(a) for a in lands])
    return outs[0], outs[1], list(outs[2:2 + n]), list(outs[2 + n:2 + 2 * n]), outs[-1]


def _chip_exchange_wait(send_sems, recv_sems, sums, lands, after, name):
    n = len(sums)

    def body(*refs):
        sm, ld = refs[:n], refs[n:2 * n]
        ssem, rsem = refs[2 * n], refs[2 * n + 1]
        x, y, c = _my_pos()
        for i in range(n):
            for j, chip in enumerate([(1 - x, y), (x, 1 - y), (1 - x, 1 - y)]):
                cp = _chip_copy(sm[i], ld[i], ssem, rsem, i, j, chip, c)
                cp.wait_send()
                cp.wait_recv()

    outs = pl.pallas_call(
        body, name=name,
        out_shape=(*[pltpu.HBM(a.shape, a.dtype) for a in sums], *[pltpu.HBM(a.shape, a.dtype) for a in lands]),
        in_specs=[HBM] * (2 * n) + [SEMS, SEMS, ANY], out_specs=tuple([HBM] * (2 * n)),
        input_output_aliases={i: i for i in range(2 * n)},
        compiler_params=pltpu.CompilerParams(has_side_effects=EFFECT),
    )(*sums, *lands, send_sems, recv_sems, after)
    return list(outs[:n]), list(outs[n:])


def _row_tile(r, itemsize):
    align = 32 // itemsize
    best = r
    for t in range(align, min(r, 256) + 1, align):
        if r % t == 0:
            best = t
    return best


def _prefetch_call(body, name, out_shape, grid, in_specs, out_specs, sem):
    return pl.pallas_call(
        body, out_shape=out_shape, name=name,
        grid_spec=pltpu.PrefetchScalarGridSpec(num_scalar_prefetch=1, grid=grid, in_specs=in_specs, out_specs=out_specs),
        compiler_params=pltpu.CompilerParams(dimension_semantics=sem, vmem_limit_bytes=VMEM_LIMIT))


def _pair_sum(pos, part, got, layout, name):
    kind, _ = layout
    _, r, cols = got.shape
    tr, tc = _tiles(r, cols, part.dtype.itemsize)
    nr, nc = r // tr, cols // tc
    if kind == "rows":
        pspec = pl.BlockSpec((tr, tc), lambda q, i, j, p: ((2 * q + p[0]) * nr + i, j))
    elif kind == "cols":
        pspec = pl.BlockSpec((tr, tc), lambda q, i, j, p: (i, (2 * q + p[0]) * nc + j))
    else:
        pspec = pl.BlockSpec((None, tr, tc), lambda q, i, j, p: (2 * q + p[0], i, j))

    def body(pos_ref, p_ref, g_ref, o_ref):
        o_ref[...] = (p_ref[...].astype(F32) + g_ref[...].astype(F32)).astype(o_ref.dtype)

    blk = pl.BlockSpec((None, tr, tc), lambda q, i, j, p: (q, i, j))
    return _prefetch_call(body, name, _sds((4, r, cols), part.dtype), (4, nr, nc), [pspec, blk], blk,
                          ("parallel", "parallel", "parallel"))(pos, part, got)


def _tiles(r, cols, itemsize):
    tr = _row_tile(r, itemsize)
    if tr < r or r * cols * 4 <= (2 << 20) or cols % 256:
        return tr, cols
    return r, 256


def _final_sum_adam(pos, sums, got, w, m, v, name):
    _, r, cols = w.shape
    tr, tc = _tiles(r, cols, sums.dtype.itemsize)

    def body(pos_ref, s_ref, g_ref, w_ref, m_ref, v_ref, go_ref, dl_ref, nm_ref, nv_ref):
        g = ((s_ref[...].astype(F32) + g_ref[0].astype(F32)) + g_ref[1].astype(F32)) + g_ref[2].astype(F32)
        dl, nm, nv = _adam(w_ref[...], g, m_ref[...], v_ref[...])
        go_ref[...] = g
        dl_ref[...] = dl
        nm_ref[...] = nm
        nv_ref[...] = nv

    big = pl.BlockSpec((None, tr, tc), lambda i, j, p: (0, i, j))
    return _prefetch_call(body, name, [_sds((1, r, cols))] * 4, (r // tr, cols // tc),
                          [pl.BlockSpec((None, tr, tc), lambda i, j, p: (p[1], i, j)),
                           pl.BlockSpec((3, tr, tc), lambda i, j, p: (0, i, j)), big, big, big],
                          [big] * 4, ("parallel", "parallel"))(pos, sums, got, w, m, v)


def _small_adam(g_all, w, m, v):
    npk = w.shape[1]

    def body(g_ref, w_ref, m_ref, v_ref, go_ref, dl_ref, nm_ref, nv_ref):
        g = g_ref[0:1, :]
        for k in range(1, NDEV):
            g = g + g_ref[k:k + 1, :]
        dl, nm, nv = _adam(w_ref[...], g, m_ref[...], v_ref[...])
        go_ref[...] = g
        dl_ref[...] = dl
        nm_ref[...] = nm
        nv_ref[...] = nv

    return _call(body, "small_adam", [_sds((1, npk))] * 4)(g_all, w, m, v)


SMALL = [("b_ada", 6 * D), ("norm1_g", D), ("norm2_g", D), ("final_norm_g", D), ("cf_ln_g", CFW), ("cf_ln_b", CFW),
         ("dn_norm_g", DH), ("dn_a_log", H), ("dn_dt_bias", H)]
LATE = ["dn_w_o", "cf_w_o", "w_out", "ffn_w_up", "ffn_w_down"]
LATE_SHAPE = {"dn_w_o": (NDEV, DNW, D // NDEV), "cf_w_o": (NDEV, CFW, D // NDEV), "w_out": (D, D),
              "ffn_w_up": (NDEV, D, 2 * FFN // NDEV), "ffn_w_down": (FFN, D)}
LATE_LAYOUT = {"dn_w_o": ("lead", NDEV), "cf_w_o": ("lead", NDEV), "w_out": ("rows", D // NDEV),
               "ffn_w_up": ("lead", NDEV), "ffn_w_down": ("rows", FFN // NDEV)}
LAYOUT = {"dn_w_o": ("cols", D // NDEV), "cf_w_o": ("cols", D // NDEV), "w_out": ("rows", D // NDEV),
          "ffn_w_up": ("cols", 2 * FFN // NDEV), "ffn_w_down": ("rows", FFN // NDEV),
          "w_in": ("lead", NDEV), "dn_conv_w": ("lead", NDEV), "cf_conv_w": ("lead", NDEV), "ffn_conv_w": ("lead", NDEV)}
NAMES = ["w_ada", "b_ada", "norm1_g", "w_in", "dn_conv_w", "dn_a_log", "dn_dt_bias", "dn_norm_g", "dn_w_o", "cf_conv_w",
         "cf_ln_g", "cf_ln_b", "cf_w_o", "w_out", "norm2_g", "ffn_w_up", "ffn_conv_w", "ffn_w_down", "final_norm_g"]


def _pack_small(d):
    rows = []
    for nm, n in SMALL:
        row = d[nm].reshape(1, n)
        pad = (-n) % LANE
        rows.append(jnp.pad(row, ((0, 0), (0, pad))) if pad else row)
    return jnp.concatenate(rows, axis=1)


def _unpack_small(row, shapes):
    out, off = {}, 0
    for nm, n in SMALL:
        out[nm] = row[0, off:off + n].reshape(shapes[nm])
        off += n + ((-n) % LANE)
    return out


def _cols_from_gathered(g):
    return jnp.transpose(g, (1, 0, 2)).reshape(g.shape[1], NDEV * g.shape[2])


def _cols_to_parts(full):
    r, ctot = full.shape
    return jnp.transpose(full.reshape(r, NDEV, ctot // NDEV), (1, 0, 2))


def kernel(x, c, w_ada, b_ada, norm1_g, w_in, dn_conv_w, dn_a_log, dn_dt_bias, dn_norm_g, dn_w_o, cf_conv_w, cf_ln_g, cf_ln_b, cf_w_o, w_out, norm2_g, ffn_w_up, ffn_conv_w, ffn_w_down, final_norm_g, loss_target, m_w_ada, m_b_ada, m_norm1_g, m_w_in, m_dn_conv_w, m_dn_a_log, m_dn_dt_bias, m_dn_norm_g, m_dn_w_o, m_cf_conv_w, m_cf_ln_g, m_cf_ln_b, m_cf_w_o, m_w_out, m_norm2_g, m_ffn_w_up, m_ffn_conv_w, m_ffn_w_down, m_final_norm_g, v_w_ada, v_b_ada, v_norm1_g, v_w_in, v_dn_conv_w, v_dn_a_log, v_dn_dt_bias, v_dn_norm_g, v_dn_w_o, v_cf_conv_w, v_cf_ln_g, v_cf_ln_b, v_cf_w_o, v_w_out, v_norm2_g, v_ffn_w_up, v_ffn_conv_w, v_ffn_w_down, v_final_norm_g):
    args = locals()
    w = {nm: args[nm] for nm in NAMES}
    mo = {nm: args["m_" + nm] for nm in NAMES}
    vo = {nm: args["v_" + nm] for nm in NAMES}
    shapes = {nm: w[nm].shape for nm in NAMES}
    px, py, pc = _my_pos()
    me = _dev_index(px, py, pc)

    def mat(a):
        return a.reshape(a.shape[-2:])

    pos = jnp.stack([pc, 2 * px + py, me]).astype(jnp.int32)

    first = ["w_in", "dn_conv_w", "cf_conv_w", "ffn_conv_w"]
    tr_in = lambda a: jnp.transpose(a, (0, 2, 1))
    got = _all_gather([tr_in(w["w_in"]).astype(BF16)] + [mat(w[nm]) for nm in first[1:]] + [c], "gather_first")
    full = {nm: _cols_from_gathered(g) for nm, g in zip(first[1:], got[1:-1])}
    c_all = got[-1].reshape(NDEV, D)
    w_in_p = _pad_win(got[0].reshape(NIN, D))

    ncol = 6 * D // NDEV
    b_sh = lax.dynamic_slice(b_ada.reshape(1, 6 * D), (0, me * ncol), (1, ncol))
    mod_sh = _ada_fwd(c_all, mat(w_ada), b_sh)
    mod_all = _all_gather([mod_sh], "gather_mod")[0]
    mod = lax.dynamic_index_in_dim(mod_all, me, axis=1, keepdims=False).reshape(1, 6 * D)

    late_shards = [mat(w[nm]).astype(BF16) for nm in LATE]
    late_lay = [LATE_LAYOUT[nm] for nm in LATE]
    late_bufs = [_place_own(pos, s, lax.empty(LATE_SHAPE[nm], BF16), lay, "place_" + nm)
                 for nm, s, lay in zip(LATE, late_shards, late_lay)]
    l_send, l_recv, l_shards, l_bufs, l_token = _gather_ici_start(late_shards, late_bufs, late_lay, mod_all, "gather_late_start")

    res = {}

    class Comm:
        token0 = l_token[0, 0]
        pending = {}

        @staticmethod
        def late_weights(after):
            shards, bufs = _gather_ici_wait(l_send, l_recv, l_shards, l_bufs, late_lay, after, "gather_late_wait")
            return _gather_pair(shards, bufs, late_lay, "gather_late_pair")

        @staticmethod
        def grads_begin(group, gd):
            names = list(gd)
            lays = [LAYOUT[nm] for nm in names]
            gl = []
            for nm in names:
                if nm == "w_in":
                    gl.append(_unpad_win(gd[nm]).reshape(NDEV, NSH, D))
                else:
                    gl.append(_cols_to_parts(gd[nm]) if LAYOUT[nm][0] == "lead" else gd[nm])
            started = _pair_exchange_start(gl, lays, "rs_pair_start_" + group)
            Comm.pending[group] = (names, lays) + tuple(started[:4])
            return started[4][0, 0]

        @staticmethod
        def grads_continue(group, after):
            names, lays, ssem, rsem, gl, lands = Comm.pending[group]
            gl, from_sib = _pair_exchange_wait(ssem, rsem, gl, lands, lays, after, "rs_pair_wait_" + group)
            sums = [_pair_sum(pos, g, r, lay, "rs_pair_sum_" + nm) for nm, g, r, lay in zip(names, gl, from_sib, lays)]
            started = _chip_exchange_start(sums, "rs_chips_start_" + group)
            Comm.pending[group] = (names,) + tuple(started[:4])
            return started[4][0, 0]

        @staticmethod
        def finish(group, after):
            names, ssem, rsem, sums, lands = Comm.pending[group]
            sums, lands = _chip_exchange_wait(ssem, rsem, sums, lands, after, "rs_chips_wait_" + group)
            for nm, s, r in zip(names, sums, lands):
                if nm == "w_in":
                    outs = _final_sum_adam(pos, s, r, tr_in(w[nm]), tr_in(mo[nm]), tr_in(vo[nm]), "adam_" + nm)
                    res[nm] = [tr_in(o) for o in outs]
                else:
                    res[nm] = _final_sum_adam(pos, s, r, w[nm], mo[nm], vo[nm], "adam_" + nm)
            return res[names[-1]][0]

    vec = lambda a: a.reshape(1, -1)
    loss, grad_x, small = _local_step(
        x.reshape(S, D), loss_target.reshape(S, D), mod, vec(norm1_g), vec(norm2_g), vec(final_norm_g), w_in_p,
        full["dn_conv_w"], vec(dn_a_log), vec(dn_dt_bias), vec(dn_norm_g), full["cf_conv_w"], vec(cf_ln_g),
        vec(cf_ln_b), full["ffn_conv_w"], Comm)

    done_a = Comm.finish("a", grad_x)
    done_b = Comm.finish("b", done_a)

    small["b_ada"] = small.pop("mod")
    packed = _pack_small(small) + 0.0 * done_b.reshape(-1)[0]
    g_small = _all_gather([packed], "gather_small")[0].reshape(NDEV, -1)
    outs = _small_adam(g_small, _pack_small({nm: w[nm] for nm, _ in SMALL}), _pack_small({nm: mo[nm] for nm, _ in SMALL}),
                       _pack_small({nm: vo[nm] for nm, _ in SMALL}))
    unpacked = [_unpack_small(o, shapes) for o in outs]
    for nm, _ in SMALL:
        res[nm] = [u[nm] for u in unpacked]

    dmod_sel = lax.dynamic_slice(g_small[:, :6 * D], (0, me * ncol), (NDEV, ncol))
    outs = _ada_bwd_adam(c_all, dmod_sel, mat(w_ada), mat(m_w_ada), mat(v_w_ada))
    res["w_ada"] = [o.reshape(shapes["w_ada"]) for o in outs]
    Comm.finish("c", jnp.concatenate([done_b.reshape(-1)[:LANE], outs[0].reshape(-1)[:LANE]]))

    loss = lax.psum(loss.reshape(()), ("x", "y", "c"))
    out = [loss, grad_x.reshape(x.shape)]
    for k in range(4):
        out += [res[nm][k] for nm in NAMES]
    return tuple(out)
```

```python
import functools

import jax
import jax.numpy as jnp
from jax import lax
from jax.experimental import pallas as pl
from jax.experimental.pallas import tpu as pltpu

F32 = jnp.float32
BF16 = jnp.bfloat16
HI = lax.Precision.HIGHEST
MESH = pl.DeviceIdType.MESH
ANY = pl.BlockSpec(memory_space=pl.ANY)

NDEV = 8
D = 2048
S = 2048
H = 8
DH = 128
DNW = H * DH
CFW = 1024
CFK = 31
DNK = 4
FFN = 5632
FFK = 3
CH = 64
NCH = S // CH
EPS = 1e-6
NIN = 10256
NINP = 10368
O_Z, O_GA, O_GB, O_GLU, O_SM = 3072, 4096, 6144, 8192, 10240
LANE = 128
TS = 256
VMEM_LIMIT = 56 * 1024 * 1024

ADAM_LR, ADAM_B1, ADAM_B2, ADAM_EPS, ADAM_WD, ADAM_STEP = 0.001, 0.9, 0.999, 1e-08, 0.01, 10


def _call(body, name, out_shape, grid=(), in_specs=None, out_specs=None, scratch=(), sem=None, aliases=None):
    kw = {}
    if aliases:
        kw["input_output_aliases"] = aliases
    if in_specs is not None:
        kw["in_specs"] = in_specs
    if out_specs is not None:
        kw["out_specs"] = out_specs
    return pl.pallas_call(
        body, out_shape=out_shape, grid=grid, scratch_shapes=scratch, name=name,
        compiler_params=pltpu.CompilerParams(dimension_semantics=sem, vmem_limit_bytes=VMEM_LIMIT), **kw)


def _sds(shape, dtype=F32):
    return jax.ShapeDtypeStruct(shape, dtype)


def _tile(dim, pref):
    if dim <= pref:
        return dim
    best = None
    for t in range(LANE, pref + 1, LANE):
        if dim % t == 0:
            best = t
    assert best is not None, (dim, pref)
    return best


def _sigmoid(x):
    return 1.0 / (1.0 + jnp.exp(-x))


def _silu(x):
    return x * _sigmoid(x)


def _dsilu(x):
    s = _sigmoid(x)
    return s * (1.0 + x * (1.0 - s))


def _silu_both(x):
    s = _sigmoid(x)
    return x * s, s * (1.0 + x * (1.0 - s))


def _softplus(x):
    return jnp.maximum(x, 0.0) + jnp.log(1.0 + jnp.exp(-jnp.abs(x)))


def _dot(a, b, dims, precision=None):
    return lax.dot_general(a, b, (dims, ((), ())), preferred_element_type=F32, precision=precision)


NN = ((1,), (0,))
NT = ((1,), (1,))
TN = ((0,), (0,))


def _my_pos():
    return lax.axis_index("x"), lax.axis_index("y"), lax.axis_index("c")


def _mm(a, b, mode, out_dtype, name, tm=1024, tn=1024, tk=2048, a2=None, b2=None, dep=None):
    sharded = b.ndim == 3
    if sharded and mode == "nn":
        cs = b.shape[2]
        (m, k), n = a.shape, NDEV * cs
        gs = max(1, tn // cs)
        tm, tn, tk = _tile(m, tm), gs * cs, _tile(k, tk)
    elif sharded:
        assert mode == "nt"
        cs = b.shape[2]
        m, n, k = a.shape[0], b.shape[1], NDEV * cs
        gs = max(1, tk // cs)
        tm, tn, tk = _tile(m, tm), _tile(n, tn), gs * cs
    else:
        if mode == "nn":
            (m, k), (k2, n) = a.shape, b.shape
        elif mode == "nt":
            (m, k), (n, k2) = a.shape, b.shape
        else:
            (k, m), (k2, n) = a.shape, b.shape
        assert k == k2, (a.shape, b.shape, mode)
        n = n * (2 if b2 is not None else 1)
        tm, tn, tk = _tile(m, tm), _tile(n // (2 if b2 is not None else 1), tn), _tile(k, tk)
    nk, nj = k // tk, n // tn
    halfk, halfj = nk // 2, nj // 2
    dims = {"nn": NN, "nt": NT, "tn": TN}[mode]

    n_in = 2 + (a2 is not None) + (b2 is not None) + (dep is not None)

    def body(*refs):
        a_ref, b_ref = refs[0], refs[1]
        x_ref = refs[2] if (a2 is not None or b2 is not None) else None
        o_ref = refs[n_in]
        acc_ref = refs[n_in + 1] if nk > 1 else None
        j, kk = pl.program_id(1), pl.program_id(2)

        if nk > 1:
            @pl.when(kk == 0)
            def _():
                acc_ref[...] = jnp.zeros_like(acc_ref)

        def accumulate(product, cols=slice(None)):
            if nk == 1:
                o_ref[:, cols] = product().astype(o_ref.dtype)
            else:
                acc_ref[:, cols] += product()

        if sharded and mode == "nn":
            for q in range(gs):
                accumulate(lambda q=q: _dot(a_ref[...], b_ref[q], NN), slice(q * cs, (q + 1) * cs))
        elif sharded:
            def contract(lhs_ref):
                def product():
                    part = None
                    for q in range(gs):
                        term = _dot(lhs_ref[:, q * cs:(q + 1) * cs], b_ref[q], NT)
                        part = term if part is None else part + term
                    return part
                accumulate(product)

            if a2 is None:
                contract(a_ref)
            else:
                pl.when(kk < halfk)(lambda: contract(a_ref))
                pl.when(kk >= halfk)(lambda: contract(x_ref))
        elif b2 is not None:
            pl.when(j < halfj)(lambda: accumulate(lambda: _dot(a_ref[...], b_ref[...], dims)))
            pl.when(j >= halfj)(lambda: accumulate(lambda: _dot(a_ref[...], x_ref[...], dims)))
        else:
            accumulate(lambda: _dot(a_ref[...], b_ref[...], dims))

        if nk > 1:
            @pl.when(kk == nk - 1)
            def _():
                o_ref[...] = acc_ref[...].astype(o_ref.dtype)

    ins, in_specs = [a], []
    if mode == "tn":
        in_specs.append(pl.BlockSpec((tk, tm), lambda i, j, kk: (kk, i)))
    elif a2 is not None:
        in_specs.append(pl.BlockSpec((tm, tk), lambda i, j, kk: (i, jnp.minimum(kk, halfk - 1))))
    else:
        in_specs.append(pl.BlockSpec((tm, tk), lambda i, j, kk: (i, kk)))
    ins.append(b)
    if sharded and mode == "nn":
        in_specs.append(pl.BlockSpec((gs, tk, cs), lambda i, j, kk: (j, kk, 0)))
    elif sharded:
        in_specs.append(pl.BlockSpec((gs, tn, cs), lambda i, j, kk: (kk, j, 0)))
    elif mode == "nt":
        in_specs.append(pl.BlockSpec((tn, tk), lambda i, j, kk: (j, kk)))
    elif b2 is not None:
        in_specs.append(pl.BlockSpec((tk, tn), lambda i, j, kk: (kk, jnp.minimum(j, halfj - 1))))
    else:
        in_specs.append(pl.BlockSpec((tk, tn), lambda i, j, kk: (kk, j)))
    if a2 is not None:
        ins.append(a2)
        in_specs.append(pl.BlockSpec((tm, tk), lambda i, j, kk: (i, jnp.maximum(kk - halfk, 0))))
    if b2 is not None:
        ins.append(b2)
        in_specs.append(pl.BlockSpec((tk, tn), lambda i, j, kk: (kk, jnp.maximum(j - halfj, 0))))
    if dep is not None:
        ins.append(dep)
        in_specs.append(ANY)
    return _call(body, name, _sds((m, n), out_dtype), grid=(m // tm, nj, nk),
                 in_specs=in_specs, out_specs=pl.BlockSpec((tm, tn), lambda i, j, kk: (i, j)),
                 scratch=[pltpu.VMEM((tm, tn), F32)] if nk > 1 else [],
                 sem=("parallel", "parallel", "arbitrary"))(*ins)


def _ada_fwd(c_all, w_sh, b_sh):
    n = w_sh.shape[1]
    tn = 512

    def body(c_ref, w_ref, b_ref, o_ref):
        ca = _silu(c_ref[...]).astype(BF16)
        o_ref[...] = _dot(ca, w_ref[...].astype(BF16), NN) + b_ref[...]

    return _call(body, "ada_fwd", _sds((NDEV, n)), grid=(n // tn,),
                 in_specs=[pl.BlockSpec((NDEV, D), lambda j: (0, 0)), pl.BlockSpec((D, tn), lambda j: (0, j)),
                           pl.BlockSpec((1, tn), lambda j: (0, j))],
                 out_specs=pl.BlockSpec((NDEV, tn), lambda j: (0, j)), sem=("parallel",))(c_all, w_sh, b_sh)


def _adam(w, g, m, v):
    m = ADAM_B1 * m + (1.0 - ADAM_B1) * g
    v = ADAM_B2 * v + (1.0 - ADAM_B2) * (g * g)
    m_hat = m / (1.0 - ADAM_B1 ** ADAM_STEP)
    v_hat = v / (1.0 - ADAM_B2 ** ADAM_STEP)
    delta = -ADAM_LR * (m_hat / (jnp.sqrt(v_hat) + ADAM_EPS) + ADAM_WD * w)
    return delta, m, v


def _ada_bwd_adam(c_all, dmod_sel, w, m, v):
    r, n = w.shape
    tr = 256

    def body(c_ref, d_ref, w_ref, m_ref, v_ref, g_ref, dl_ref, nm_ref, nv_ref):
        ca = _silu(c_ref[...])
        g = _dot(ca, d_ref[...], TN, precision=HI)
        dl, nm, nv = _adam(w_ref[...], g, m_ref[...], v_ref[...])
        g_ref[...] = g
        dl_ref[...] = dl
        nm_ref[...] = nm
        nv_ref[...] = nv

    big = pl.BlockSpec((tr, n), lambda i: (i, 0))
    return _call(body, "ada_bwd_adam", [_sds((r, n))] * 4, grid=(r // tr,),
                 in_specs=[pl.BlockSpec((NDEV, tr), lambda i: (0, i)), pl.BlockSpec((NDEV, n), lambda i: (0, 0)),
                           big, big, big],
                 out_specs=[big] * 4, sem=("parallel",))(c_all, dmod_sel, w, m, v)


def _row_spec(width=D):
    return pl.BlockSpec((TS, width), lambda i: (i, 0))


def _vec_spec(width=D):
    return pl.BlockSpec((1, width), lambda i: (0, 0))


def _acc_spec(width=D):
    return pl.BlockSpec((8, width), lambda i: (0, 0))


def _norm_mod(x, g, sc, sh, name):
    def body(x_ref, g_ref, sc_ref, sh_ref, o_ref):
        xv = x_ref[...]
        r = lax.rsqrt(jnp.mean(xv * xv, axis=-1, keepdims=True) + EPS)
        o_ref[...] = ((xv * r) * g_ref[...] * (1.0 + sc_ref[...]) + sh_ref[...]).astype(BF16)

    return _call(body, name, _sds((S, D), BF16), grid=(S // TS,),
                 in_specs=[_row_spec(), _vec_spec(), _vec_spec(), _vec_spec()], out_specs=_row_spec(),
                 sem=("parallel",))(x, g, sc, sh)


def _resid_norm_mod(x, mix, gt, g, sc, sh, name):
    def body(x_ref, mix_ref, gt_ref, g_ref, sc_ref, sh_ref, x2_ref, o_ref):
        xv = x_ref[...] + gt_ref[...] * mix_ref[...]
        x2_ref[...] = xv
        r = lax.rsqrt(jnp.mean(xv * xv, axis=-1, keepdims=True) + EPS)
        o_ref[...] = ((xv * r) * g_ref[...] * (1.0 + sc_ref[...]) + sh_ref[...]).astype(BF16)

    return _call(body, name, [_sds((S, D)), _sds((S, D), BF16)], grid=(S // TS,),
                 in_specs=[_row_spec(), _row_spec()] + [_vec_spec()] * 4, out_specs=[_row_spec(), _row_spec()],
                 sem=("parallel",))(x, mix, gt, g, sc, sh)


def _acc_rows(acc_ref, rows):
    @pl.when(pl.program_id(0) == 0)
    def _():
        acc_ref[...] = jnp.zeros_like(acc_ref)

    for k, row in enumerate(rows):
        acc_ref[k:k + 1, :] += row


def _loss_head(x2, f, tgt, gt2, gf):
    def body(x2_ref, f_ref, t_ref, gt_ref, gf_ref, dx_ref, df_ref, acc_ref):
        fv = f_ref[...]
        x3 = x2_ref[...] + gt_ref[...] * fv
        r = lax.rsqrt(jnp.mean(x3 * x3, axis=-1, keepdims=True) + EPS)
        xn = x3 * r
        e = xn * gf_ref[...] - t_ref[...]
        loss = 0.5 * jnp.sum(jnp.mean(e * e, axis=-1, keepdims=True), axis=0, keepdims=True)
        dy = e * (1.0 / D)
        dxn = dy * gf_ref[...]
        dx3 = r * (dxn - xn * jnp.mean(dxn * xn, axis=-1, keepdims=True))
        dx_ref[...] = dx3
        df_ref[...] = (dx3 * gt_ref[...]).astype(BF16)
        _acc_rows(acc_ref, [jnp.sum(dy * xn, axis=0, keepdims=True), jnp.sum(dx3 * fv, axis=0, keepdims=True),
                            jnp.broadcast_to(loss, (1, D))])

    return _call(body, "loss_head", [_sds((S, D)), _sds((S, D), BF16), _sds((8, D))], grid=(S // TS,),
                 in_specs=[_row_spec(), _row_spec(), _row_spec(), _vec_spec(), _vec_spec()],
                 out_specs=[_row_spec(), _row_spec(), _acc_spec()], sem=("arbitrary",))(x2, f, tgt, gt2, gf)


def _norm_mod_bwd(dhn, x, dres, g, sc, name, mix=None, gt=None):
    gated = mix is not None

    def body(*refs):
        if gated:
            dhn_ref, x_ref, dres_ref, g_ref, sc_ref, mix_ref, gt_ref, dx_ref, dmix_ref, acc_ref = refs
        else:
            dhn_ref, x_ref, dres_ref, g_ref, sc_ref, dx_ref, acc_ref = refs
        xv = x_ref[...]
        dh = dhn_ref[...]
        r = lax.rsqrt(jnp.mean(xv * xv, axis=-1, keepdims=True) + EPS)
        xn = xv * r
        gv = g_ref[...]
        sc1 = 1.0 + sc_ref[...]
        dxn = dh * gv * sc1
        dx = dres_ref[...] + r * (dxn - xn * jnp.mean(dxn * xn, axis=-1, keepdims=True))
        dx_ref[...] = dx
        rows = [jnp.sum(dh, axis=0, keepdims=True), jnp.sum(dh * xn * gv, axis=0, keepdims=True),
                jnp.sum(dh * xn * sc1, axis=0, keepdims=True)]
        if gated:
            rows.append(jnp.sum(dx * mix_ref[...], axis=0, keepdims=True))
            dmix_ref[...] = (dx * gt_ref[...]).astype(BF16)
        _acc_rows(acc_ref, rows)

    ins = [dhn, x, dres, g, sc]
    in_specs = [_row_spec(), _row_spec(), _row_spec(), _vec_spec(), _vec_spec()]
    outs = [_sds((S, D))]
    out_specs = [_row_spec()]
    if gated:
        ins += [mix, gt]
        in_specs += [_row_spec(), _vec_spec()]
        outs.append(_sds((S, D), BF16))
        out_specs.append(_row_spec())
    outs.append(_sds((8, D)))
    out_specs.append(_acc_spec())
    return _call(body, name, outs, grid=(S // TS,), in_specs=in_specs, out_specs=out_specs,
                 sem=("arbitrary",))(*ins)


RC = 256


def _conv_fwd_rows(pad_ref, w_ref, kw, head, r0):
    acc = None
    for k in range(kw):
        term = w_ref[k:k + 1, :] * pad_ref[pl.ds(head - (kw - 1) + k + r0, RC), :]
        acc = term if acc is None else acc + term
    return acc


def _conv_bwd_rows(pad2_ref, w_ref, kw, r0):
    acc = None
    for k in range(kw):
        term = w_ref[k:k + 1, :] * pad2_ref[pl.ds(kw - 1 - k + r0, RC), :]
        acc = term if acc is None else acc + term
    return acc


def _conv_dw(pad_ref, dout_ref, dw_ref, kw, head):
    for k in range(kw):
        acc = None
        for r0 in range(0, S, RC):
            term = jnp.sum(pad_ref[pl.ds(head - (kw - 1) + k + r0, RC), :] * dout_ref[pl.ds(r0, RC), :],
                           axis=0, keepdims=True)
            acc = term if acc is None else acc + term
        dw_ref[k:k + 1, :] = acc


def _col_spec(width, off_blocks=0):
    return pl.BlockSpec((S, width), lambda j: (0, j + off_blocks))


def _dn_pre_fwd(proj, conv_w):
    head = 8

    def body(x_ref, w_ref, o_ref, pad_ref):
        j = pl.program_id(0)
        pad_ref[pl.ds(0, head), :] = jnp.zeros((head, DH), F32)
        pad_ref[pl.ds(head, S), :] = x_ref[...]
        scale = jnp.where(j < H, DH ** -0.5, 1.0)
        for r0 in range(0, S, RC):
            y = _silu(_conv_fwd_rows(pad_ref, w_ref, DNK, head, r0))
            rinv = lax.rsqrt(jnp.sum(y * y, axis=-1, keepdims=True) + EPS)
            o_ref[pl.ds(r0, RC), :] = jnp.where(j < 2 * H, y * rinv * scale, y)

    return _call(body, "dn_pre_fwd", _sds((S, 3 * DNW)), grid=(3 * H,),
                 in_specs=[_col_spec(DH), pl.BlockSpec((DNK, DH), lambda j: (0, j))], out_specs=_col_spec(DH),
                 scratch=[pltpu.VMEM((S + head, DH), F32)], sem=("parallel",))(proj, conv_w)


def _dn_pre_bwd(dq, dk, dv, proj, conv_w, dproj):
    head = 8

    def body(dq_ref, dk_ref, dv_ref, x_ref, w_ref, dproj_in, dx_ref, dw_ref, pad_ref, pad2_ref):
        j = pl.program_id(0)
        pad_ref[pl.ds(0, head), :] = jnp.zeros((head, DH), F32)
        pad_ref[pl.ds(head, S), :] = x_ref[...]
        pad2_ref[pl.ds(S, head), :] = jnp.zeros((head, DH), F32)
        scale = jnp.where(j < H, DH ** -0.5, 1.0)
        for r0 in range(0, S, RC):
            xc = _conv_fwd_rows(pad_ref, w_ref, DNK, head, r0)
            y, dy_dxc = _silu_both(xc)
            rinv = lax.rsqrt(jnp.sum(y * y, axis=-1, keepdims=True) + EPS)
            yn = y * rinv
            rows = pl.ds(r0, RC)
            do = jnp.where(j < H, dq_ref[rows, :], jnp.where(j < 2 * H, dk_ref[rows, :], dv_ref[rows, :]))
            dy_n = scale * rinv * (do - yn * jnp.sum(do * yn, axis=-1, keepdims=True))
            dy = jnp.where(j < 2 * H, dy_n, do)
            pad2_ref[rows, :] = dy * dy_dxc
        for r0 in range(0, S, RC):
            dx_ref[pl.ds(r0, RC), :] = _conv_bwd_rows(pad2_ref, w_ref, DNK, r0).astype(BF16)
        _conv_dw(pad_ref, pad2_ref, dw_ref, DNK, head)

    wspec = pl.BlockSpec((DNK, DH), lambda j: (0, j))
    head_col = lambda lo: pl.BlockSpec((S, DH), lambda j: (0, jnp.clip(j - lo, 0, H - 1)))
    return _call(body, "dn_pre_bwd", [_sds((S, NINP), BF16), _sds((DNK, 3 * DNW))], grid=(3 * H,),
                 in_specs=[head_col(0), head_col(H), head_col(2 * H), _col_spec(DH), wspec, ANY],
                 out_specs=[_col_spec(DH), wspec],
                 scratch=[pltpu.VMEM((S + head, DH), F32), pltpu.VMEM((S + head, DH), F32)],
                 sem=("parallel",), aliases={5: 0})(dq, dk, dv, proj, conv_w, dproj)


CF_HEAD = 32
CF_VAL = pl.BlockSpec((S, LANE), lambda j: (0, O_GLU // LANE + 2 * j))
CF_GL = pl.BlockSpec((S, LANE), lambda j: (0, O_GLU // LANE + 2 * j + 1))


def _cf_conv_fwd(proj, conv_w):
    def body(val_ref, gl_ref, w_ref, o_ref, pad_ref):
        pad_ref[pl.ds(0, CF_HEAD), :] = jnp.zeros((CF_HEAD, LANE), F32)
        pad_ref[pl.ds(CF_HEAD, S), :] = val_ref[...] * _sigmoid(gl_ref[...])
        for r0 in range(0, S, RC):
            o_ref[pl.ds(r0, RC), :] = _conv_fwd_rows(pad_ref, w_ref, CFK, CF_HEAD, r0)

    wspec = pl.BlockSpec((CFK, LANE), lambda j: (0, j))
    return _call(body, "cf_conv_fwd", _sds((S, CFW)), grid=(CFW // LANE,),
                 in_specs=[CF_VAL, CF_GL, wspec], out_specs=_col_spec(LANE),
                 scratch=[pltpu.VMEM((S + CF_HEAD, LANE), F32)], sem=("parallel",))(proj, proj, conv_w)


def _cf_conv_bwd(du1, proj, conv_w, dproj):
    def body(d_ref, val_ref, gl_ref, w_ref, dproj_in, dp_ref, dw_ref, pad_ref, pad2_ref):
        sg = _sigmoid(gl_ref[...])
        pad_ref[pl.ds(0, CF_HEAD), :] = jnp.zeros((CF_HEAD, LANE), F32)
        pad_ref[pl.ds(CF_HEAD, S), :] = val_ref[...] * sg
        pad2_ref[pl.ds(0, S), :] = d_ref[...]
        pad2_ref[pl.ds(S, CF_HEAD), :] = jnp.zeros((CF_HEAD, LANE), F32)
        for r0 in range(0, S, RC):
            du0 = _conv_bwd_rows(pad2_ref, w_ref, CFK, r0)
            rows = pl.ds(r0, RC)
            sgr = _sigmoid(gl_ref[rows, :])
            dp_ref[rows, 0:LANE] = (du0 * sgr).astype(BF16)
            dp_ref[rows, LANE:2 * LANE] = (du0 * val_ref[rows, :] * sgr * (1.0 - sgr)).astype(BF16)
        _conv_dw(pad_ref, pad2_ref, dw_ref, CFK, CF_HEAD)

    wspec = pl.BlockSpec((CFK, LANE), lambda j: (0, j))
    return _call(body, "cf_conv_bwd", [_sds((S, NINP), BF16), _sds((CFK, CFW))], grid=(CFW // LANE,),
                 in_specs=[_col_spec(LANE), CF_VAL, CF_GL, wspec, ANY],
                 out_specs=[pl.BlockSpec((S, 2 * LANE), lambda j: (0, O_GLU // (2 * LANE) + j)), wspec],
                 scratch=[pltpu.VMEM((S + CF_HEAD, LANE), F32), pltpu.VMEM((S + CF_HEAD, LANE), F32)],
                 sem=("parallel",), aliases={4: 0})(du1, proj, proj, conv_w, dproj)


def _cf_ln_fwd(u1, g, b):
    def body(u_ref, g_ref, b_ref, o_ref):
        u = u_ref[...]
        mu = jnp.mean(u, axis=-1, keepdims=True)
        xc = u - mu
        y = xc * lax.rsqrt(jnp.mean(xc * xc, axis=-1, keepdims=True) + EPS)
        o_ref[...] = _silu(y * g_ref[...] + b_ref[...]).astype(BF16)

    return _call(body, "cf_ln_fwd", _sds((S, CFW), BF16), grid=(S // TS,),
                 in_specs=[_row_spec(CFW), _vec_spec(CFW), _vec_spec(CFW)], out_specs=_row_spec(CFW),
                 sem=("parallel",))(u1, g, b)


def _cf_ln_bwd(du3, u1, g, b):
    def body(d_ref, u_ref, g_ref, b_ref, du_ref, acc_ref):
        u = u_ref[...]
        mu = jnp.mean(u, axis=-1, keepdims=True)
        xc = u - mu
        rstd = lax.rsqrt(jnp.mean(xc * xc, axis=-1, keepdims=True) + EPS)
        xh = xc * rstd
        du2 = d_ref[...] * _dsilu(xh * g_ref[...] + b_ref[...])
        dxh = du2 * g_ref[...]
        du_ref[...] = rstd * (dxh - jnp.mean(dxh, axis=-1, keepdims=True)
                              - xh * jnp.mean(dxh * xh, axis=-1, keepdims=True))
        _acc_rows(acc_ref, [jnp.sum(du2 * xh, axis=0, keepdims=True), jnp.sum(du2, axis=0, keepdims=True)])

    return _call(body, "cf_ln_bwd", [_sds((S, CFW)), _sds((8, CFW))], grid=(S // TS,),
                 in_specs=[_row_spec(CFW), _row_spec(CFW), _vec_spec(CFW), _vec_spec(CFW)],
                 out_specs=[_row_spec(CFW), _acc_spec(CFW)], sem=("arbitrary",))(du3, u1, g, b)


FB = 256
FNB = FFN // FB
FF_HEAD = 8


def _ffn_mid_fwd(upall, conv_w):
    def body(gate_ref, up_ref, w_ref, o_ref, pad_ref):
        pad_ref[pl.ds(0, FF_HEAD), :] = jnp.zeros((FF_HEAD, FB), F32)
        pad_ref[pl.ds(FF_HEAD, S), :] = gate_ref[...]
        for r0 in range(0, S, RC):
            gc = _conv_fwd_rows(pad_ref, w_ref, FFK, FF_HEAD, r0)
            o_ref[pl.ds(r0, RC), :] = (_silu(gc) * up_ref[pl.ds(r0, RC), :]).astype(BF16)

    wspec = pl.BlockSpec((FFK, FB), lambda j: (0, j))
    return _call(body, "ffn_mid_fwd", _sds((S, FFN), BF16), grid=(FNB,),
                 in_specs=[_col_spec(FB), _col_spec(FB, FNB), wspec], out_specs=_col_spec(FB),
                 scratch=[pltpu.VMEM((S + FF_HEAD, FB), F32)], sem=("parallel",))(upall, upall, conv_w)


def _ffn_mid_bwd(dh, upall, conv_w):
    def body(d_ref, gate_ref, up_ref, w_ref, dgate_ref, dup_ref, dw_ref, pad_ref, pad2_ref):
        pad_ref[pl.ds(0, FF_HEAD), :] = jnp.zeros((FF_HEAD, FB), F32)
        pad_ref[pl.ds(FF_HEAD, S), :] = gate_ref[...]
        pad2_ref[pl.ds(S, FF_HEAD), :] = jnp.zeros((FF_HEAD, FB), F32)
        for r0 in range(0, S, RC):
            rows = pl.ds(r0, RC)
            gc = _conv_fwd_rows(pad_ref, w_ref, FFK, FF_HEAD, r0)
            dhv = d_ref[rows, :]
            act, dact = _silu_both(gc)
            dup_ref[rows, :] = (dhv * act).astype(BF16)
            pad2_ref[rows, :] = dhv * up_ref[rows, :] * dact
        for r0 in range(0, S, RC):
            dgate_ref[pl.ds(r0, RC), :] = _conv_bwd_rows(pad2_ref, w_ref, FFK, r0).astype(BF16)
        _conv_dw(pad_ref, pad2_ref, dw_ref, FFK, FF_HEAD)

    wspec = pl.BlockSpec((FFK, FB), lambda j: (0, j))
    return _call(body, "ffn_mid_bwd", [_sds((S, FFN), BF16), _sds((S, FFN), BF16), _sds((FFK, FFN))],
                 grid=(FNB,), in_specs=[_col_spec(FB), _col_spec(FB), _col_spec(FB, FNB), wspec],
                 out_specs=[_col_spec(FB), _col_spec(FB), wspec],
                 scratch=[pltpu.VMEM((S + FF_HEAD, FB), F32), pltpu.VMEM((S + FF_HEAD, FB), F32)],
                 sem=("parallel",))(dh, upall, upall, conv_w)


GT = 256
SM_BLK = O_SM // LANE


def _chunk_tri(lower):
    r = lax.broadcasted_iota(jnp.int32, (GT, GT), 0)
    c = lax.broadcasted_iota(jnp.int32, (GT, GT), 1)
    same = (r // CH) == (c // CH)
    tri = (c <= r) if lower else (c >= r)
    return jnp.where(same & tri, 1.0, 0.0).astype(F32)


def _gates_fwd(proj, alog_v, dtb_v):
    def body(sm_ref, al_ref, dt_ref, o_ref):
        lane = lax.broadcasted_iota(jnp.int32, (GT, LANE), 1)
        tri = _chunk_tri(True)
        na = -jnp.exp(al_ref[...])
        for r0 in range(0, S, GT):
            sm = sm_ref[pl.ds(r0, GT), :]
            raw = jnp.where((lane >= H) & (lane < 2 * H), na * _softplus(sm + dt_ref[...]), 0.0)
            gc = _dot(tri, raw, NN, precision=HI)
            o_ref[pl.ds(r0, GT), :] = jnp.where(lane < H, _sigmoid(sm), gc)

    return _call(body, "gates_fwd", _sds((S, LANE)), grid=(1,),
                 in_specs=[pl.BlockSpec((S, LANE), lambda i: (0, SM_BLK)), _vec_spec(LANE), _vec_spec(LANE)],
                 out_specs=pl.BlockSpec((S, LANE), lambda i: (0, 0)), sem=("arbitrary",))(proj, alog_v, dtb_v)


def _gates_bwd(dgb, proj, alog_v, dtb_v, dproj):
    def body(d_ref, sm_ref, al_ref, dt_ref, dproj_in, o_ref, acc_ref):
        lane = lax.broadcasted_iota(jnp.int32, (GT, LANE), 1)
        is_g = (lane >= H) & (lane < 2 * H)
        tri = _chunk_tri(False)
        na = -jnp.exp(al_ref[...])
        d_al = jnp.zeros((1, LANE), F32)
        d_dt = jnp.zeros((1, LANE), F32)
        for r0 in range(0, S, GT):
            sm = sm_ref[pl.ds(r0, GT), :]
            dv = d_ref[pl.ds(r0, GT), :]
            z = sm + dt_ref[...]
            draw = _dot(tri, jnp.where(is_g, dv, 0.0), NN, precision=HI)
            dlogit = jnp.where(is_g, draw * na * _sigmoid(z), 0.0)
            d_al = d_al + jnp.sum(jnp.where(is_g, draw * na * _softplus(z), 0.0), axis=0, keepdims=True)
            d_dt = d_dt + jnp.sum(dlogit, axis=0, keepdims=True)
            bt = _sigmoid(sm)
            o_ref[pl.ds(r0, GT), :] = jnp.where(lane < H, dv * bt * (1.0 - bt), dlogit).astype(BF16)
        acc_ref[...] = jnp.zeros_like(acc_ref)
        acc_ref[0:1, :] = d_al
        acc_ref[1:2, :] = d_dt

    return _call(body, "gates_bwd", [_sds((S, NINP), BF16), _sds((8, LANE))], grid=(1,),
                 in_specs=[pl.BlockSpec((S, LANE), lambda i: (0, 0)), pl.BlockSpec((S, LANE), lambda i: (0, SM_BLK)),
                           _vec_spec(LANE), _vec_spec(LANE), ANY],
                 out_specs=[pl.BlockSpec((S, LANE), lambda i: (0, SM_BLK)), _acc_spec(LANE)],
                 sem=("arbitrary",), aliases={4: 0})(dgb, proj, alog_v, dtb_v, dproj)


HB = 2


def _each(fn, *lists):
    return [fn(*args) for args in zip(*lists)]


def _neumann_inv(a, eye):
    p = _each(lambda m: -m, a)
    t = _each(lambda m: eye + m, p)
    for _ in range(5):
        p = _each(lambda m: _dot(m, m, NN, precision=HI), p)
        t = _each(lambda tt, pp: tt + _dot(tt, pp, NN, precision=HI), t, p)
    return t


def _head_specs():
    q = pl.BlockSpec((S, HB * DH), lambda h: (0, h))
    k = pl.BlockSpec((S, HB * DH), lambda h: (0, H // HB + h))
    v = pl.BlockSpec((S, HB * DH), lambda h: (0, 2 * H // HB + h))
    gb = pl.BlockSpec((HB, S, DH), lambda h: (h, 0, 0))
    gr = pl.BlockSpec((HB, NCH, CH), lambda h: (h, 0, 0))
    return q, k, v, gb, gr


ST_SPEC = pl.BlockSpec((HB, NCH, DH, DH), lambda h: (h, 0, 0, 0))
TM_SPEC = pl.BlockSpec((HB, NCH, CH, CH), lambda h: (h, 0, 0, 0))
HCOL = pl.BlockSpec((S, HB * DH), lambda h: (0, h))


def _delta_fwd(qkvn, gb, gr, bb):
    def body(q_ref, k_ref, v_ref, gb_ref, gr_ref, bb_ref, o_ref, st_ref, tm_ref):
        ri = lax.broadcasted_iota(jnp.int32, (CH, CH), 0)
        ci = lax.broadcasted_iota(jnp.int32, (CH, CH), 1)
        strict = ri > ci
        causal = ri >= ci
        eye = jnp.where(ri == ci, 1.0, 0.0).astype(F32)

        hs = list(range(HB))
        cols = [slice(hh * DH, (hh + 1) * DH) for hh in hs]
        bf = lambda m: m.astype(BF16)

        def local(n):
            rows = pl.ds(pl.multiple_of(n * CH, CH), CH)
            c = dict(rows=rows, n=n)
            c["q"] = [q_ref[rows, cc] for cc in cols]
            c["k"] = [k_ref[rows, cc] for cc in cols]
            c["v"] = [v_ref[rows, cc] for cc in cols]
            c["g"] = [gb_ref[hh, rows, :] for hh in hs]
            c["beta"] = [bb_ref[hh, rows, :] for hh in hs]
            diff = [c["g"][hh][:, :CH] - gr_ref[hh, pl.ds(n, 1), :] for hh in hs]
            c["el"] = _each(lambda d: jnp.exp(jnp.where(causal, d, 0.0)), diff)
            c["eg"] = _each(jnp.exp, c["g"])
            c["gl"] = _each(lambda m: m[CH - 1:CH, :], c["g"])
            c["kb"] = _each(lambda x, y: x * y, c["k"], c["beta"])
            c["kbf"] = _each(bf, c["k"])
            c["a"] = _each(lambda x, y, e: jnp.where(strict, _dot(bf(x), y, NT) * e, 0.0), c["kb"], c["kbf"], c["el"])
            return c

        def advance(c, t, sts):
            n, rows = c["n"], c["rows"]
            for hh in hs:
                tm_ref[hh, n] = t[hh]
                st_ref[hh, n] = sts[hh]
            sb = _each(bf, sts)
            r = _each(lambda vv, bb_, kk, ee, ss: vv * bb_ - _dot(bf(kk * ee), ss, NN), c["v"], c["beta"], c["kb"], c["eg"], sb)
            ub = _each(lambda tt, rr: bf(_dot(tt, rr, NN, precision=HI)), t, r)
            p = _each(lambda qq, kk, e: jnp.where(causal, _dot(bf(qq), kk, NT) * e, 0.0), c["q"], c["kbf"], c["el"])
            o = _each(lambda qq, ee, ss, pp, uu: _dot(bf(qq * ee), ss, NN) + _dot(bf(pp), uu, NN), c["q"], c["eg"], sb, p, ub)
            for hh in hs:
                o_ref[rows, cols[hh]] = o[hh]
            kd = _each(lambda kk, l, gg: kk * jnp.exp(l - gg), c["k"], c["gl"], c["g"])
            return _each(lambda st, l, kk, uu: st * jnp.exp(l) + _dot(bf(kk), uu, TN), sts, c["gl"], kd, ub)

        def step(i, sts):
            c0, c1 = local(2 * i), local(2 * i + 1)
            t = _neumann_inv(c0["a"] + c1["a"], eye)
            sts = advance(c0, t[:HB], list(sts))
            return tuple(advance(c1, t[HB:], sts))

        lax.fori_loop(0, NCH // 2, step, tuple(jnp.zeros((DH, DH), F32) for _ in hs))

    q, k, v, gbs, grs = _head_specs()
    return _call(body, "delta_fwd", [_sds((S, DNW)), _sds((H, NCH, DH, DH)), _sds((H, NCH, CH, CH))], grid=(H // HB,),
                 in_specs=[q, k, v, gbs, grs, gbs], out_specs=[HCOL, ST_SPEC, TM_SPEC],
                 sem=("parallel",))(qkvn, qkvn, qkvn, gb, gr, bb)


def _delta_bwd(qkvn, gb, gr, bb, st_all, tm_all, do_all):
    def body(q_ref, k_ref, v_ref, gb_ref, gr_ref, bb_ref, st_ref, tm_ref, do_ref,
             dq_ref, dk_ref, dv_ref, dg_ref, db_ref):
        ri = lax.broadcasted_iota(jnp.int32, (CH, CH), 0)
        ci = lax.broadcasted_iota(jnp.int32, (CH, CH), 1)
        lo_s, lo_c, up_s, up_c = ri > ci, ri >= ci, ri < ci, ri <= ci
        last_row = lax.broadcasted_iota(jnp.int32, (CH, 1), 0) == CH - 1

        def rs(mat):
            return jnp.sum(mat, axis=1, keepdims=True)

        def total(mat):
            return jnp.sum(rs(mat), axis=0, keepdims=True)

        hs = list(range(HB))
        cols = [slice(hh * DH, (hh + 1) * DH) for hh in hs]
        bf = lambda m: m.astype(BF16)
        mul = lambda x, y: x * y
        spread = jnp.full((8, DH), 1.0 / DH, F32)

        def as_row(col):
            return _dot(spread, jnp.broadcast_to(col, (CH, DH)), NT, precision=HI)[0:1, :]

        def step(i, dss):
            ns = [NCH - 1 - 2 * i, NCH - 2 - 2 * i]
            rws = [pl.ds(pl.multiple_of(n * CH, CH), CH) for n in ns]
            idx = [(cc, hh) for cc in range(2) for hh in hs]
            q = [q_ref[rws[cc], cols[hh]] for cc, hh in idx]
            k = [k_ref[rws[cc], cols[hh]] for cc, hh in idx]
            v = [v_ref[rws[cc], cols[hh]] for cc, hh in idx]
            do = [do_ref[rws[cc], cols[hh]] for cc, hh in idx]
            g = [gb_ref[hh, rws[cc], :] for cc, hh in idx]
            beta = [bb_ref[hh, rws[cc], :] for cc, hh in idx]
            t = [tm_ref[hh, ns[cc]] for cc, hh in idx]
            st = [st_ref[hh, ns[cc]] for cc, hh in idx]
            diff = [gg[:, :CH] - gr_ref[hh, pl.ds(ns[cc], 1), :] for gg, (cc, hh) in zip(g, idx)]
            el = _each(lambda d: jnp.exp(jnp.where(lo_c, d, 0.0)), diff)
            eu = _each(lambda d: jnp.exp(jnp.where(up_c, -d, 0.0)), diff)
            eg = _each(jnp.exp, g)
            gl = _each(lambda m: m[CH - 1:CH, :], g)
            egl = _each(jnp.exp, gl)
            ekd = _each(lambda l, m: jnp.exp(l - m), gl, g)
            kb = _each(mul, k, beta)
            kbg = _each(mul, kb, eg)
            qg = _each(mul, q, eg)
            kd = _each(mul, k, ekd)
            qb, kbf, kbb = _each(bf, q), _each(bf, k), _each(bf, kb)
            kbgb, qgb, kdb = _each(bf, kbg), _each(bf, qg), _each(bf, kd)
            sb, dob = _each(bf, st), _each(bf, do)
            r = _each(lambda vv, b, x, s: vv * b - _dot(x, s, NN), v, beta, kbgb, sb)
            u = _each(lambda tt, rr: _dot(tt, rr, NN, precision=HI), t, r)
            ub = _each(bf, u)
            kk = _each(lambda x, y: _dot(x, y, NT), kbb, kbf)
            qk = _each(lambda x, y: _dot(x, y, NT), qb, kbf)
            kkt = _each(lambda x, y: _dot(x, y, NT), kbf, kbb)
            qkt = _each(lambda x, y: _dot(x, y, NT), kbf, qb)
            pt = _each(lambda m, e: jnp.where(up_c, m * e, 0.0), qkt, eu)
            ds, du, dr, drb, ds_new = [], [], [], [], list(dss)
            for cc in range(2):
                sl = slice(cc * HB, (cc + 1) * HB)
                ds_c = ds_new
                dsb_c = _each(bf, ds_c)
                du_c = _each(lambda p, d, x, s: _dot(bf(p), d, NN) + _dot(x, s, NN), pt[sl], dob[sl], kdb[sl], dsb_c)
                dr_c = _each(lambda tt, d: _dot(tt, d, TN, precision=HI), t[sl], du_c)
                drb_c = _each(bf, dr_c)
                ds_new = _each(lambda x, d, e, s, y, z: _dot(x, d, TN) + e * s - _dot(y, z, TN),
                               qgb[sl], dob[sl], egl[sl], ds_c, kbgb[sl], drb_c)
                ds, du, dr, drb = ds + ds_c, du + du_c, dr + dr_c, drb + drb_c
            dsb = _each(bf, ds)
            dpg = _each(lambda d, uu, e: jnp.where(lo_c, _dot(d, uu, NT), 0.0) * e, dob, ub, el)
            dpgt = _each(lambda uu, d, e: jnp.where(up_c, _dot(uu, d, NT), 0.0) * e, ub, dob, eu)
            dag = _each(lambda d, uu, e: -jnp.where(lo_s, _dot(d, uu, NT), 0.0) * e, drb, ub, el)
            dagt = _each(lambda uu, d, e: -jnp.where(up_s, _dot(uu, d, NT), 0.0) * e, ub, drb, eu)
            dqg = _each(lambda d, s: _dot(d, s, NT), dob, sb)
            dkbg = _each(lambda d, s: -_dot(d, s, NT), drb, sb)
            dkd = _each(lambda uu, s: _dot(uu, s, NT), ub, dsb)
            dkb =_each(lambda a, x, y, e: _dot(bf(a), x, NN) + y * e, dag, kbf, dkbg, eg)
            dk = _each(lambda a, x, p, y, z, e, w, b: _dot(bf(a), x, NN) + _dot(bf(p), y, NN) + z * e + w * b,
                       dagt, kbb, dpgt, qb, dkd, ekd, dkb, beta)
            dq = _each(lambda p, x, y, e: _dot(bf(p), x, NN) + y * e, dpg, kbf, dqg, eg)
            dkd_kd = _each(lambda x, y: rs(x * y), dkd, kd)
            dg = _each(lambda a, x, p, y, at, xt, pt_, yt, z, w, c, d, e:
                       rs(a * x + p * y) - rs(at * xt + pt_ * yt) + rs(z * w) + rs(c * d) - e,
                       dag, kk, dpg, qk, dagt, kkt, dpgt, qkt, dqg, qg, dkbg, kbg, dkd_kd)
            dgl = _each(lambda x, e, s, y: jnp.sum(x, axis=0, keepdims=True) + e[:, 0:1] * total(s * y), dkd_kd, egl, ds, st)
            dg = _each(lambda x, y: x + jnp.where(last_row, y, 0.0), dg, dgl)
            dbeta = _each(lambda x, y, z, w: rs(x * y) + rs(z * w), dkb, k, dr, v)
            for j, (cc, hh) in enumerate(idx):
                dq_ref[rws[cc], cols[hh]] = dq[j]
                dk_ref[rws[cc], cols[hh]] = dk[j]
                dv_ref[rws[cc], cols[hh]] = dr[j] * beta[j]
                dg_ref[hh, pl.ds(ns[cc], 1), :] = as_row(dg[j])
                db_ref[hh, pl.ds(ns[cc], 1), :] = as_row(dbeta[j])
            return tuple(ds_new)

        lax.fori_loop(0, NCH // 2, step, tuple(jnp.zeros((DH, DH), F32) for _ in hs))

    q, k, v, gbs, grs = _head_specs()
    return _call(body, "delta_bwd",
                 [_sds((S, DNW)), _sds((S, DNW)), _sds((S, DNW)), _sds((H, NCH, CH)), _sds((H, NCH, CH))], grid=(H // HB,),
                 in_specs=[q, k, v, gbs, grs, gbs, ST_SPEC, TM_SPEC, HCOL], out_specs=[HCOL, HCOL, HCOL, grs, grs],
                 sem=("parallel",))(qkvn, qkvn, qkvn, gb, gr, bb, st_all, tm_all, do_all)


Z_BLK = O_Z // DNW


def _dn_post_fwd(o, proj, gn):
    def body(o_ref, z_ref, gn_ref, og_ref):
        for h in range(H):
            cols = slice(h * DH, (h + 1) * DH)
            ov = o_ref[:, cols]
            on = ov * lax.rsqrt(jnp.mean(ov * ov, axis=-1, keepdims=True) + EPS) * gn_ref[...]
            og_ref[:, cols] = (on * _silu(z_ref[:, cols])).astype(BF16)

    return _call(body, "dn_post_fwd", _sds((S, DNW), BF16), grid=(S // TS,),
                 in_specs=[_row_spec(DNW), pl.BlockSpec((TS, DNW), lambda i: (i, Z_BLK)), _vec_spec(DH)],
                 out_specs=_row_spec(DNW), sem=("parallel",))(o, proj, gn)


def _dn_post_bwd(dog, o, proj, gn, dproj):
    def body(d_ref, o_ref, z_ref, gn_ref, dproj_in, do_ref, dz_ref, acc_ref):
        dgn = jnp.zeros((1, DH), F32)
        for h in range(H):
            cols = slice(h * DH, (h + 1) * DH)
            ov, zv, dv = o_ref[:, cols], z_ref[:, cols], d_ref[:, cols]
            rinv = lax.rsqrt(jnp.mean(ov * ov, axis=-1, keepdims=True) + EPS)
            xn = ov * rinv
            act, dact = _silu_both(zv)
            don = dv * act
            dz_ref[:, cols] = (dv * xn * gn_ref[...] * dact).astype(BF16)
            dgn = dgn + jnp.sum(don * xn, axis=0, keepdims=True)
            dxn = don * gn_ref[...]
            do_ref[:, cols] = rinv * (dxn - xn * jnp.mean(dxn * xn, axis=-1, keepdims=True))
        _acc_rows(acc_ref, [dgn])

    zspec = pl.BlockSpec((TS, DNW), lambda i: (i, Z_BLK))
    return _call(body, "dn_post_bwd", [_sds((S, DNW)), _sds((S, NINP), BF16), _sds((8, DH))], grid=(S // TS,),
                 in_specs=[_row_spec(DNW), _row_spec(DNW), zspec, _vec_spec(DH), ANY],
                 out_specs=[_row_spec(DNW), zspec, _acc_spec(DH)], sem=("arbitrary",),
                 aliases={4: 1})(dog, o, proj, gn, dproj)


GA_BLK = O_GA // D
GB_BLK = O_GB // D


def _merge_fwd(ba, bb, proj):
    def body(a_ref, b_ref, ga_ref, gb_ref, o_ref):
        o_ref[...] = (_sigmoid(ga_ref[...]) * a_ref[...] + _sigmoid(gb_ref[...]) * b_ref[...]).astype(BF16)

    return _call(body, "merge_fwd", _sds((S, D), BF16), grid=(S // TS,),
                 in_specs=[_row_spec(), _row_spec(), pl.BlockSpec((TS, D), lambda i: (i, GA_BLK)),
                           pl.BlockSpec((TS, D), lambda i: (i, GB_BLK))],
                 out_specs=_row_spec(), sem=("parallel",))(ba, bb, proj, proj)


def _merge_bwd(dm, ba, bb, proj, dproj):
    def body(d_ref, a_ref, b_ref, ga_ref, gb_ref, dproj_in, dg_ref, da_ref, db_ref):
        d = d_ref[...]
        sa, sb = _sigmoid(ga_ref[...]), _sigmoid(gb_ref[...])
        dg_ref[:, 0:D] = (d * a_ref[...] * sa * (1.0 - sa)).astype(BF16)
        dg_ref[:, D:2 * D] = (d * b_ref[...] * sb * (1.0 - sb)).astype(BF16)
        da_ref[...] = (d * sa).astype(BF16)
        db_ref[...] = (d * sb).astype(BF16)

    return _call(body, "merge_bwd", [_sds((S, NINP), BF16), _sds((S, D), BF16), _sds((S, D), BF16)], grid=(S // TS,),
                 in_specs=[_row_spec(), _row_spec(), _row_spec(), pl.BlockSpec((TS, D), lambda i: (i, GA_BLK)),
                           pl.BlockSpec((TS, D), lambda i: (i, GB_BLK)), ANY],
                 out_specs=[pl.BlockSpec((TS, 2 * D), lambda i: (i, O_GA // (2 * D))), _row_spec(), _row_spec()],
                 sem=("parallel",), aliases={5: 0})(dm, ba, bb, proj, proj, dproj)


NSH = NIN // NDEV


def _pad_win(wt):
    rows = [wt[0:4096], wt[6160:6160 + 2 * D]]
    for j in range(CFW // LANE):
        rows += [wt[4112 + LANE * j:4112 + LANE * (j + 1)], wt[4112 + CFW + LANE * j:4112 + CFW + LANE * (j + 1)]]
    rows += [wt[4096:4112], jnp.zeros((NINP - NIN, wt.shape[1]), wt.dtype)]
    return jnp.concatenate(rows, axis=0)


def _unpad_win(gpt):
    nb, cols = CFW // LANE, gpt.shape[1]
    glu = gpt[O_GLU:O_GLU + 2 * CFW].reshape(nb, 2, LANE, cols)
    glu = jnp.transpose(glu, (1, 0, 2, 3)).reshape(2 * CFW, cols)
    return jnp.concatenate([gpt[0:4096], gpt[O_SM:O_SM + 16], glu, gpt[O_GA:O_GA + 2 * D]], axis=0)


def _lane_vec(v8, offset):
    return jnp.pad(v8, ((0, 0), (offset, LANE - 8 - offset)))


def _tie(vec, token):
    return vec + token


def _local_step(x, tgt, mod, norm1_g, norm2_g, final_g, w_in_p, dn_conv_w, a_log, dt_bias, dn_norm_g,
                cf_conv_w, cf_ln_g, cf_ln_b, ffn_conv_w, comm):
    sh1, sc1, gt1, sh2, sc2, gt2 = (mod[:, i * D:(i + 1) * D] for i in range(6))
    alog_v, dtb_v = _lane_vec(a_log, H), _lane_vec(dt_bias, H)

    hn1 = _norm_mod(x, norm1_g, sc1, _tie(sh1, comm.token0), "norm_mod1")
    proj = _mm(hn1, w_in_p, "nt", F32, "mm_in", tn=1152)
    qkvn = _dn_pre_fwd(proj, dn_conv_w)
    gates = _gates_fwd(proj, alog_v, dtb_v)
    beta_t = gates[:, 0:H].T
    g_t = gates[:, H:2 * H].T
    gb = jnp.broadcast_to(g_t[:, :, None], (H, S, DH))
    bb = jnp.broadcast_to(beta_t[:, :, None], (H, S, DH))
    gr = g_t.reshape(H, NCH, CH)
    o, st_all, tm_all = _delta_fwd(qkvn, gb, gr, bb)
    og = _dn_post_fwd(o, proj, dn_norm_g)
    u1 = _cf_conv_fwd(proj, cf_conv_w)
    u3 = _cf_ln_fwd(u1, cf_ln_g, cf_ln_b)
    after = og[0:8, 0:LANE].astype(F32) + u3[0:8, 0:LANE].astype(F32)
    dn_w_o, cf_w_o, w_out, ffn_w_up, ffn_w_down = comm.late_weights(after)
    br_a = _mm(og, dn_w_o, "nn", F32, "mm_dn_o")
    br_b = _mm(u3, cf_w_o, "nn", F32, "mm_cf_o")
    merged = _merge_fwd(br_a, br_b, proj)
    mix = _mm(merged, w_out, "nn", F32, "mm_out")
    x2, hn2 = _resid_norm_mod(x, mix, gt1, norm2_g, sc2, sh2, "resid_norm_mod2")
    upall = _mm(hn2, ffn_w_up, "nn", F32, "mm_up")
    hmid = _ffn_mid_fwd(upall, ffn_conv_w)
    f = _mm(hmid, ffn_w_down, "nn", F32, "mm_down", tm=2048)

    dx3, df, acc_f = _loss_head(x2, f, tgt, gt2, final_g)
    d_final_g, d_gt2, loss = acc_f[0:1], acc_f[1:2], acc_f[2:3, 0:1]
    dhmid = _mm(df, ffn_w_down, "nt", F32, "mm_down_dx")
    g_w_down = _mm(hmid, df, "tn", BF16, "mm_down_dw", tm=FFN // 4)
    d_gate, d_up, g_ffn_conv = _ffn_mid_bwd(dhmid, upall, ffn_conv_w)
    g_w_up = _mm(hn2, d_gate, "tn", BF16, "mm_up_dw", tn=2 * FFN // NDEV, b2=d_up)
    tok_a = comm.grads_begin("a", dict(ffn_w_down=g_w_down, ffn_w_up=g_w_up))
    dhn2 = _mm(d_gate, ffn_w_up, "nt", F32, "mm_up_dx", a2=d_up, dep=jnp.broadcast_to(tok_a, (8, LANE)))
    tok_a = comm.grads_continue("a", dhn2)
    dx2, dmix, acc2 = _norm_mod_bwd(dhn2, x2, dx3, _tie(norm2_g, tok_a), sc2, "norm_mod2_bwd", mix=mix, gt=gt1)
    d_sh2, d_sc2, d_norm2_g, d_gt1 = acc2[0:1], acc2[1:2], acc2[2:3], acc2[3:4]
    dmerged = _mm(dmix, w_out, "nt", F32, "mm_out_dx")
    g_w_out = _mm(merged, dmix, "tn", BF16, "mm_out_dw")
    d_proj, d_bra, d_brb = _merge_bwd(dmerged, br_a, br_b, proj, lax.empty((S, NINP), BF16))
    du3 = _mm(d_brb, cf_w_o, "nt", F32, "mm_cf_o_dx")
    g_cf_w_o = _mm(u3, d_brb, "tn", BF16, "mm_cf_o_dw")
    du1, acc_ln = _cf_ln_bwd(du3, u1, cf_ln_g, cf_ln_b)
    d_proj, g_cf_conv = _cf_conv_bwd(du1, proj, cf_conv_w, d_proj)
    dog = _mm(d_bra, dn_w_o, "nt", F32, "mm_dn_o_dx")
    g_dn_w_o = _mm(og, d_bra, "tn", BF16, "mm_dn_o_dw")
    tok_b = comm.grads_begin("b", dict(w_out=g_w_out, cf_w_o=g_cf_w_o, dn_w_o=g_dn_w_o, ffn_conv_w=g_ffn_conv,
                                       cf_conv_w=g_cf_conv))
    do, d_proj, acc_gn = _dn_post_bwd(dog, o, proj, _tie(dn_norm_g, tok_b), d_proj)
    tok_b = comm.grads_continue("b", do)
    dq, dk, dv, dgr, dbr = _delta_bwd(qkvn, gb, _tie(gr, tok_b), bb, st_all, tm_all, do)
    d_proj, g_dn_conv = _dn_pre_bwd(dq, dk, dv, proj, dn_conv_w, d_proj)
    dgates = jnp.concatenate([dbr.reshape(H, S).T, dgr.reshape(H, S).T, jnp.zeros((S, LANE - 2 * H), F32)], axis=1)
    d_proj, acc_g = _gates_bwd(dgates, proj, alog_v, dtb_v, d_proj)
    g_w_in_p = _mm(d_proj, hn1, "tn", BF16, "mm_in_dw", tm=1152)
    comm.grads_begin("c", dict(w_in=g_w_in_p, dn_conv_w=g_dn_conv))
    tok_c = comm.grads_continue("c", g_dn_conv)
    dhn1 = _mm(d_proj, w_in_p, "nn", F32, "mm_in_dx", tk=NINP // 3, dep=jnp.broadcast_to(tok_c, (8, LANE)))
    grad_x, acc1 = _norm_mod_bwd(dhn1, x, dx2, norm1_g, sc1, "norm_mod1_bwd")
    d_sh1, d_sc1, d_norm1_g = acc1[0:1], acc1[1:2], acc1[2:3]

    d_mod = jnp.concatenate([d_sh1, d_sc1, d_gt1, d_sh2, d_sc2, d_gt2], axis=1)
    small = dict(mod=d_mod, norm1_g=d_norm1_g, norm2_g=d_norm2_g, final_norm_g=d_final_g,
                 cf_ln_g=acc_ln[0:1], cf_ln_b=acc_ln[1:2], dn_norm_g=acc_gn[0:1],
                 dn_a_log=acc_g[0:1, H:2 * H], dn_dt_bias=acc_g[1:2, H:2 * H])
    return loss, grad_x, small


def _dev_index(px, py, pc):
    return 4 * px + 2 * py + pc


def _all_gather(arrs, name):
    n = len(arrs)

    def body(*refs):
        ins, outs = refs[:n], refs[n:2 * n]
        send_sems, recv_sems, loc_sems = refs[2 * n:]
        x, y, c = _my_pos()
        me, sib = (x, y, c), (x, y, 1 - c)
        chips = [(1 - x, y), (x, 1 - y), (1 - x, 1 - y)]

        def cp(i, k, block, to, src=None):
            dst = outs[i].at[_dev_index(*block)]
            return pltpu.make_async_remote_copy(
                src_ref=dst if src is None else src, dst_ref=dst, send_sem=send_sems.at[i, k],
                recv_sem=recv_sems.at[i, k], device_id=to, device_id_type=MESH)

        mine = [pltpu.make_async_copy(ins[i], outs[i].at[_dev_index(*me)], loc_sems.at[i]) for i in range(n)]
        for m in mine:
            m.start()
        sent = []
        for i in range(n):
            sent.append(cp(i, 0, me, sib, src=ins[i]))
            sent += [cp(i, 1 + j, me, (*chip, c), src=ins[i]) for j, chip in enumerate(chips)]
        for s in sent:
            s.start()
        for i in range(n):
            for j, chip in enumerate(chips):
                cp(i, 1 + j, (*chip, c), me).wait_recv()
                fwd = cp(i, 4 + j, (*chip, c), sib)
                fwd.start()
                sent.append(fwd)
        for i in range(n):
            cp(i, 0, sib, me).wait_recv()
            for j, chip in enumerate(chips):
                cp(i, 4 + j, (*chip, 1 - c), me).wait_recv()
        for s in sent:
            s.wait_send()
        for m in mine:
            m.wait()

    outs = pl.pallas_call(
        body, out_shape=[_sds((NDEV,) + a.shape, a.dtype) for a in arrs], in_specs=[ANY] * n, out_specs=[ANY] * n,
        scratch_shapes=[pltpu.SemaphoreType.DMA((n, 7)), pltpu.SemaphoreType.DMA((n, 7)), pltpu.SemaphoreType.DMA((n,))],
        name=name)(*arrs)
    return list(outs)


def _slab(ref, layout, idx):
    kind, n = layout
    if kind == "rows":
        return ref.at[pl.ds(pl.multiple_of(idx * n, n), n), :]
    if kind == "cols":
        return ref.at[:, pl.ds(pl.multiple_of(idx * n, n), n)]
    return ref.at[idx]


def _slab_shape(arr, layout):
    kind, n = layout
    if kind == "rows":
        return (n, arr.shape[1])
    if kind == "cols":
        return (arr.shape[0], n)
    return tuple(arr.shape[1:])


HBM = pl.BlockSpec(memory_space=pltpu.HBM)
SEMS = pl.BlockSpec(memory_space=pltpu.SEMAPHORE)
EFFECT = pltpu.SideEffectType.DATAFLOW_SIDE_EFFECTING
TOKEN = jax.ShapeDtypeStruct((8, LANE), F32)


def _hbm(a):
    return pltpu.with_memory_space_constraint(a, pltpu.HBM)


def _gather_ici_copy(shard_ref, buf_ref, layout, send_sems, recv_sems, i, j, me, chip, c):
    return pltpu.make_async_remote_copy(
        src_ref=shard_ref, dst_ref=_slab(buf_ref, layout, me), send_sem=send_sems.at[3 * i + j],
        recv_sem=recv_sems.at[3 * i + j], device_id=(*chip, c), device_id_type=MESH)


def _gather_ici_start(shards, bufs, layouts, after, name):
    n = len(shards)

    def body(*refs):
        sh, bf = refs[:n], refs[n:2 * n]
        send_sems, recv_sems = refs[2 * n + 1], refs[2 * n + 2]
        token = refs[-1]
        x, y, c = _my_pos()
        me = _dev_index(x, y, c)
        for i in range(n):
            for j, chip in enumerate([(1 - x, y), (x, 1 - y), (1 - x, 1 - y)]):
                _gather_ici_copy(sh[i], bf[i], layouts[i], send_sems, recv_sems, i, j, me, chip, c).start()
        token[...] = jnp.zeros_like(token)

    outs = pl.pallas_call(
        body, name=name,
        out_shape=(pltpu.SemaphoreType.DMA((3 * n,)), pltpu.SemaphoreType.DMA((3 * n,)),
                   *[pltpu.HBM(a.shape, a.dtype) for a in shards], *[pltpu.HBM(a.shape, a.dtype) for a in bufs], TOKEN),
        in_specs=[HBM] * (2 * n) + [ANY],
        out_specs=(SEMS, SEMS, *[HBM] * (2 * n), pl.BlockSpec(memory_space=pltpu.VMEM)),
        input_output_aliases={i: 2 + i for i in range(2 * n)},
        compiler_params=pltpu.CompilerParams(has_side_effects=EFFECT),
    )(*[_hbm(a) for a in shards], *[_hbm(a) for a in bufs], after)
    return outs[0], outs[1], list(outs[2:2 + n]), list(outs[2 + n:2 + 2 * n]), outs[-1]


def _gather_ici_wait(send_sems, recv_sems, shards, bufs, layouts, after, name):
    n = len(shards)

    def body(*refs):
        sh, bf = refs[:n], refs[n:2 * n]
        ssem, rsem = refs[2 * n], refs[2 * n + 1]
        x, y, c = _my_pos()
        me = _dev_index(x, y, c)
        for i in range(n):
            for j, chip in enumerate([(1 - x, y), (x, 1 - y), (1 - x, 1 - y)]):
                cp = _gather_ici_copy(sh[i], bf[i], layouts[i], ssem, rsem, i, j, me, chip, c)
                cp.wait_send()
                cp.wait_recv()

    outs = pl.pallas_call(
        body, name=name,
        out_shape=(*[pltpu.HBM(a.shape, a.dtype) for a in shards], *[pltpu.HBM(a.shape, a.dtype) for a in bufs]),
        in_specs=[HBM] * (2 * n) + [SEMS, SEMS, ANY], out_specs=tuple([HBM] * (2 * n)),
        input_output_aliases={i: i for i in range(2 * n)},
        compiler_params=pltpu.CompilerParams(has_side_effects=EFFECT),
    )(*shards, *bufs, send_sems, recv_sems, after)
    return list(outs[:n]), list(outs[n:])


def _place_own(pos, shard, buf, layout, name):
    kind, n = layout
    r, cols = shard.shape
    tr = _row_tile(r, shard.dtype.itemsize)
    nr = r // tr
    if kind == "rows":
        ospec = pl.BlockSpec((tr, cols), lambda i, p: (p[2] * nr + i, 0))
    else:
        assert kind == "lead"
        ospec = pl.BlockSpec((None, tr, cols), lambda i, p: (p[2], i, 0))

    def body(pos_ref, s_ref, buf_in, o_ref):
        o_ref[...] = s_ref[...]

    return pl.pallas_call(
        body, out_shape=_sds(buf.shape, buf.dtype), name=name, input_output_aliases={2: 0},
        grid_spec=pltpu.PrefetchScalarGridSpec(
            num_scalar_prefetch=1, grid=(nr,), in_specs=[pl.BlockSpec((tr, cols), lambda i, p: (i, 0)), ANY],
            out_specs=ospec),
        compiler_params=pltpu.CompilerParams(dimension_semantics=("parallel",), vmem_limit_bytes=VMEM_LIMIT),
    )(pos, shard, buf)


def _gather_pair(shards, bufs, layouts, name):
    n = len(shards)

    def body(*refs):
        sh, bo = refs[:n], refs[2 * n:3 * n]
        send_sems, recv_sems = refs[3 * n:]
        x, y, c = _my_pos()
        sib = (x, y, 1 - c)
        copies = []
        for i in range(n):
            for k, (px, py) in enumerate([(x, y), (1 - x, y), (x, 1 - y), (1 - x, 1 - y)]):
                slab = _slab(bo[i], layouts[i], _dev_index(px, py, c))
                copies.append(pltpu.make_async_remote_copy(
                    src_ref=sh[i] if k == 0 else slab, dst_ref=slab, send_sem=send_sems.at[i, k],
                    recv_sem=recv_sems.at[i, k], device_id=sib, device_id_type=MESH))
        for cpy in copies:
            cpy.start()
        for cpy in copies:
            cpy.wait()

    outs = pl.pallas_call(
        body, out_shape=[_sds(a.shape, a.dtype) for a in bufs], in_specs=[ANY] * (2 * n), out_specs=[ANY] * n,
        input_output_aliases={n + i: i for i in range(n)},
        scratch_shapes=[pltpu.SemaphoreType.DMA((n, 4)), pltpu.SemaphoreType.DMA((n, 4))], name=name)(*shards, *bufs)
    return list(outs)


def _pair_copy(part_ref, land_ref, layout, send_sems, recv_sems, i, q, x, y, c):
    return pltpu.make_async_remote_copy(
        src_ref=_slab(part_ref, layout, 2 * q + (1 - c)), dst_ref=land_ref.at[q], send_sem=send_sems.at[4 * i + q],
        recv_sem=recv_sems.at[4 * i + q], device_id=(x, y, 1 - c), device_id_type=MESH)


def _pair_exchange_start(parts, layouts, name):
    n = len(parts)
    lands = [lax.empty((4,) + _slab_shape(p, lay), p.dtype) for p, lay in zip(parts, layouts)]

    def body(*refs):
        pt, ld = refs[:n], refs[n:2 * n]
        send_sems, recv_sems = refs[2 * n], refs[2 * n + 1]
        token = refs[-1]
        x, y, c = _my_pos()
        for i in range(n):
            for q in range(4):
                _pair_copy(pt[i], ld[i], layouts[i], send_sems, recv_sems, i, q, x, y, c).start()
        token[...] = jnp.zeros_like(token)

    outs = pl.pallas_call(
        body, name=name,
        out_shape=(pltpu.SemaphoreType.DMA((4 * n,)), pltpu.SemaphoreType.DMA((4 * n,)),
                   *[pltpu.HBM(a.shape, a.dtype) for a in parts], *[pltpu.HBM(a.shape, a.dtype) for a in lands], TOKEN),
        in_specs=[HBM] * (2 * n), out_specs=(SEMS, SEMS, *[HBM] * (2 * n), pl.BlockSpec(memory_space=pltpu.VMEM)),
        input_output_aliases={i: 2 + i for i in range(2 * n)},
        compiler_params=pltpu.CompilerParams(has_side_effects=EFFECT),
    )(*[_hbm(a) for a in parts], *[_hbm(a) for a in lands])
    return outs[0], outs[1], list(outs[2:2 + n]), list(outs[2 + n:2 + 2 * n]), outs[-1]


def _pair_exchange_wait(send_sems, recv_sems, parts, lands, layouts, after, name):
    n = len(parts)

    def body(*refs):
        pt, ld = refs[:n], refs[n:2 * n]
        ssem, rsem = refs[2 * n], refs[2 * n + 1]
        x, y, c = _my_pos()
        for i in range(n):
            for q in range(4):
                cp = _pair_copy(pt[i], ld[i], layouts[i], ssem, rsem, i, q, x, y, c)
                cp.wait_send()
                cp.wait_recv()

    outs = pl.pallas_call(
        body, name=name,
        out_shape=(*[pltpu.HBM(a.shape, a.dtype) for a in parts], *[pltpu.HBM(a.shape, a.dtype) for a in lands]),
        in_specs=[HBM] * (2 * n) + [SEMS, SEMS, ANY], out_specs=tuple([HBM] * (2 * n)),
        input_output_aliases={i: i for i in range(2 * n)},
        compiler_params=pltpu.CompilerParams(has_side_effects=EFFECT),
    )(*parts, *lands, send_sems, recv_sems, after)
    return list(outs[:n]), list(outs[n:])


def _chip_copy(sum_ref, land_ref, send_sems, recv_sems, i, j, chip, c):
    return pltpu.make_async_remote_copy(
        src_ref=sum_ref.at[2 * chip[0] + chip[1]], dst_ref=land_ref.at[j], send_sem=send_sems.at[3 * i + j],
        recv_sem=recv_sems.at[3 * i + j], device_id=(*chip, c), device_id_type=MESH)


def _chip_exchange_start(sums, name):
    n = len(sums)
    lands = [lax.empty((3,) + s.shape[1:], s.dtype) for s in sums]

    def body(*refs):
        sm, ld = refs[:n], refs[n:2 * n]
        send_sems, recv_sems = refs[2 * n], refs[2 * n + 1]
        token = refs[-1]
        x, y, c = _my_pos()
        for i in range(n):
            for j, chip in enumerate([(1 - x, y), (x, 1 - y), (1 - x, 1 - y)]):
                _chip_copy(sm[i], ld[i], send_sems, recv_sems, i, j, chip, c).start()
        token[...] = jnp.zeros_like(token)

    outs = pl.pallas_call(
        body, name=name,
        out_shape=(pltpu.SemaphoreType.DMA((3 * n,)), pltpu.SemaphoreType.DMA((3 * n,)),
                   *[pltpu.HBM(a.shape, a.dtype) for a in sums], *[pltpu.HBM(a.shape, a.dtype) for a in lands], TOKEN),
        in_specs=[HBM] * (2 * n), out_specs=(SEMS, SEMS, *[HBM] * (2 * n), pl.BlockSpec(memory_space=pltpu.VMEM)),
        input_output_aliases={i: 2 + i for i in range(2 * n)},
        compiler_params=pltpu.CompilerParams(has_side_effects=EFFECT),
    )(*[_hbm(a) for a in sums], *[_hbm(a) for a in lands])
    return outs[0], outs[1], list(outs[2:2 + n]), list(outs[2 + n:2 + 2 * n]), outs[-1]


def _chip_exchange_wait(send_sems, recv_sems, sums, lands, after, name):
    n = len(sums)

    def body(*refs):
        sm, ld = refs[:n], refs[n:2 * n]
        ssem, rsem = refs[2 * n], refs[2 * n + 1]
        x, y, c = _my_pos()
        for i in range(n):
            for j, chip in enumerate([(1 - x, y), (x, 1 - y), (1 - x, 1 - y)]):
                cp = _chip_copy(sm[i], ld[i], ssem, rsem, i, j, chip, c)
                cp.wait_send()
                cp.wait_recv()

    outs = pl.pallas_call(
        body, name=name,
        out_shape=(*[pltpu.HBM(a.shape, a.dtype) for a in sums], *[pltpu.HBM(a.shape, a.dtype) for a in lands]),
        in_specs=[HBM] * (2 * n) + [SEMS, SEMS, ANY], out_specs=tuple([HBM] * (2 * n)),
        input_output_aliases={i: i for i in range(2 * n)},
        compiler_params=pltpu.CompilerParams(has_side_effects=EFFECT),
    )(*sums, *lands, send_sems, recv_sems, after)
    return list(outs[:n]), list(outs[n:])


def _row_tile(r, itemsize):
    align = 32 // itemsize
    best = r
    for t in range(align, min(r, 256) + 1, align):
        if r % t == 0:
            best = t
    return best


def _prefetch_call(body, name, out_shape, grid, in_specs, out_specs, sem):
    return pl.pallas_call(
        body, out_shape=out_shape, name=name,
        grid_spec=pltpu.PrefetchScalarGridSpec(num_scalar_prefetch=1, grid=grid, in_specs=in_specs, out_specs=out_specs),
        compiler_params=pltpu.CompilerParams(dimension_semantics=sem, vmem_limit_bytes=VMEM_LIMIT))


def _pair_sum(pos, part, got, layout, name):
    kind, _ = layout
    _, r, cols = got.shape
    tr, tc = _tiles(r, cols, part.dtype.itemsize)
    nr, nc = r // tr, cols // tc
    if kind == "rows":
        pspec = pl.BlockSpec((tr, tc), lambda q, i, j, p: ((2 * q + p[0]) * nr + i, j))
    elif kind == "cols":
        pspec = pl.BlockSpec((tr, tc), lambda q, i, j, p: (i, (2 * q + p[0]) * nc + j))
    else:
        pspec = pl.BlockSpec((None, tr, tc), lambda q, i, j, p: (2 * q + p[0], i, j))

    def body(pos_ref, p_ref, g_ref, o_ref):
        o_ref[...] = (p_ref[...].astype(F32) + g_ref[...].astype(F32)).astype(o_ref.dtype)

    blk = pl.BlockSpec((None, tr, tc), lambda q, i, j, p: (q, i, j))
    return _prefetch_call(body, name, _sds((4, r, cols), part.dtype), (4, nr, nc), [pspec, blk], blk,
                          ("parallel", "parallel", "parallel"))(pos, part, got)


def _tiles(r, cols, itemsize):
    tr = _row_tile(r, itemsize)
    if tr < r or r * cols * 4 <= (2 << 20) or cols % 256:
        return tr, cols
    return r, 256


def _final_sum_adam(pos, sums, got, w, m, v, name):
    _, r, cols = w.shape
    tr, tc = _tiles(r, cols, sums.dtype.itemsize)

    def body(pos_ref, s_ref, g_ref, w_ref, m_ref, v_ref, go_ref, dl_ref, nm_ref, nv_ref):
        g = ((s_ref[...].astype(F32) + g_ref[0].astype(F32)) + g_ref[1].astype(F32)) + g_ref[2].astype(F32)
        dl, nm, nv = _adam(w_ref[...], g, m_ref[...], v_ref[...])
        go_ref[...] = g
        dl_ref[...] = dl
        nm_ref[...] = nm
        nv_ref[...] = nv

    big = pl.BlockSpec((None, tr, tc), lambda i, j, p: (0, i, j))
    return _prefetch_call(body, name, [_sds((1, r, cols))] * 4, (r // tr, cols // tc),
                          [pl.BlockSpec((None, tr, tc), lambda i, j, p: (p[1], i, j)),
                           pl.BlockSpec((3, tr, tc), lambda i, j, p: (0, i, j)), big, big, big],
                          [big] * 4, ("parallel", "parallel"))(pos, sums, got, w, m, v)


def _small_adam(g_all, w, m, v):
    npk = w.shape[1]

    def body(g_ref, w_ref, m_ref, v_ref, go_ref, dl_ref, nm_ref, nv_ref):
        g = g_ref[0:1, :]
        for k in range(1, NDEV):
            g = g + g_ref[k:k + 1, :]
        dl, nm, nv = _adam(w_ref[...], g, m_ref[...], v_ref[...])
        go_ref[...] = g
        dl_ref[...] = dl
        nm_ref[...] = nm
        nv_ref[...] = nv

    return _call(body, "small_adam", [_sds((1, npk))] * 4)(g_all, w, m, v)


SMALL = [("b_ada", 6 * D), ("norm1_g", D), ("norm2_g", D), ("final_norm_g", D), ("cf_ln_g", CFW), ("cf_ln_b", CFW),
         ("dn_norm_g", DH), ("dn_a_log", H), ("dn_dt_bias", H)]
LATE = ["dn_w_o", "cf_w_o", "w_out", "ffn_w_up", "ffn_w_down"]
LATE_SHAPE = {"dn_w_o": (NDEV, DNW, D // NDEV), "cf_w_o": (NDEV, CFW, D // NDEV), "w_out": (D, D),
              "ffn_w_up": (NDEV, D, 2 * FFN // NDEV), "ffn_w_down": (FFN, D)}
LATE_LAYOUT = {"dn_w_o": ("lead", NDEV), "cf_w_o": ("lead", NDEV), "w_out": ("rows", D // NDEV),
               "ffn_w_up": ("lead", NDEV), "ffn_w_down": ("rows", FFN // NDEV)}
LAYOUT = {"dn_w_o": ("cols", D // NDEV), "cf_w_o": ("cols", D // NDEV), "w_out": ("rows", D // NDEV),
          "ffn_w_up": ("cols", 2 * FFN // NDEV), "ffn_w_down": ("rows", FFN // NDEV),
          "w_in": ("lead", NDEV), "dn_conv_w": ("lead", NDEV), "cf_conv_w": ("lead", NDEV), "ffn_conv_w": ("lead", NDEV)}
NAMES = ["w_ada", "b_ada", "norm1_g", "w_in", "dn_conv_w", "dn_a_log", "dn_dt_bias", "dn_norm_g", "dn_w_o", "cf_conv_w",
         "cf_ln_g", "cf_ln_b", "cf_w_o", "w_out", "norm2_g", "ffn_w_up", "ffn_conv_w", "ffn_w_down", "final_norm_g"]


def _pack_small(d):
    rows = []
    for nm, n in SMALL:
        row = d[nm].reshape(1, n)
        pad = (-n) % LANE
        rows.append(jnp.pad(row, ((0, 0), (0, pad))) if pad else row)
    return jnp.concatenate(rows, axis=1)


def _unpack_small(row, shapes):
    out, off = {}, 0
    for nm, n in SMALL:
        out[nm] = row[0, off:off + n].reshape(shapes[nm])
        off += n + ((-n) % LANE)
    return out


def _cols_from_gathered(g):
    return jnp.transpose(g, (1, 0, 2)).reshape(g.shape[1], NDEV * g.shape[2])


def _cols_to_parts(full):
    r, ctot = full.shape
    return jnp.transpose(full.reshape(r, NDEV, ctot // NDEV), (1, 0, 2))


def kernel(x, c, w_ada, b_ada, norm1_g, w_in, dn_conv_w, dn_a_log, dn_dt_bias, dn_norm_g, dn_w_o, cf_conv_w, cf_ln_g, cf_ln_b, cf_w_o, w_out, norm2_g, ffn_w_up, ffn_conv_w, ffn_w_down, final_norm_g, loss_target, m_w_ada, m_b_ada, m_norm1_g, m_w_in, m_dn_conv_w, m_dn_a_log, m_dn_dt_bias, m_dn_norm_g, m_dn_w_o, m_cf_conv_w, m_cf_ln_g, m_cf_ln_b, m_cf_w_o, m_w_out, m_norm2_g, m_ffn_w_up, m_ffn_conv_w, m_ffn_w_down, m_final_norm_g, v_w_ada, v_b_ada, v_norm1_g, v_w_in, v_dn_conv_w, v_dn_a_log, v_dn_dt_bias, v_dn_norm_g, v_dn_w_o, v_cf_conv_w, v_cf_ln_g, v_cf_ln_b, v_cf_w_o, v_w_out, v_norm2_g, v_ffn_w_up, v_ffn_conv_w, v_ffn_w_down, v_final_norm_g):
    args = locals()
    w = {nm: args[nm] for nm in NAMES}
    mo = {nm: args["m_" + nm] for nm in NAMES}
    vo = {nm: args["v_" + nm] for nm in NAMES}
    shapes = {nm: w[nm].shape for nm in NAMES}
    px, py, pc = _my_pos()
    me = _dev_index(px, py, pc)

    def mat(a):
        return a.reshape(a.shape[-2:])

    pos = jnp.stack([pc, 2 * px + py, me]).astype(jnp.int32)

    first = ["w_in", "dn_conv_w", "cf_conv_w", "ffn_conv_w"]
    tr_in = lambda a: jnp.transpose(a, (0, 2, 1))
    got = _all_gather([tr_in(w["w_in"]).astype(BF16)] + [mat(w[nm]) for nm in first[1:]] + [c], "gather_first")
    full = {nm: _cols_from_gathered(g) for nm, g in zip(first[1:], got[1:-1])}
    c_all = got[-1].reshape(NDEV, D)
    w_in_p = _pad_win(got[0].reshape(NIN, D))

    ncol = 6 * D // NDEV
    b_sh = lax.dynamic_slice(b_ada.reshape(1, 6 * D), (0, me * ncol), (1, ncol))
    mod_sh = _ada_fwd(c_all, mat(w_ada), b_sh)
    mod_all = _all_gather([mod_sh], "gather_mod")[0]
    mod = lax.dynamic_index_in_dim(mod_all, me, axis=1, keepdims=False).reshape(1, 6 * D)

    late_shards = [mat(w[nm]).astype(BF16) for nm in LATE]
    late_lay = [LATE_LAYOUT[nm] for nm in LATE]
    late_bufs = [_place_own(pos, s, lax.empty(LATE_SHAPE[nm], BF16), lay, "place_" + nm)
                 for nm, s, lay in zip(LATE, late_shards, late_lay)]
    l_send, l_recv, l_shards, l_bufs, l_token = _gather_ici_start(late_shards, late_bufs, late_lay, mod_all, "gather_late_start")

    res = {}

    class Comm:
        token0 = l_token[0, 0]
        pending = {}

        @staticmethod
        def late_weights(after):
            shards, bufs = _gather_ici_wait(l_send, l_recv, l_shards, l_bufs, late_lay, after, "gather_late_wait")
            return _gather_pair(shards, bufs, late_lay, "gather_late_pair")

        @staticmethod
        def grads_begin(group, gd):
            names = list(gd)
            lays = [LAYOUT[nm] for nm in names]
            gl = []
            for nm in names:
                if nm == "w_in":
                    gl.append(_unpad_win(gd[nm]).reshape(NDEV, NSH, D))
                else:
                    gl.append(_cols_to_parts(gd[nm]) if LAYOUT[nm][0] == "lead" else gd[nm])
            started = _pair_exchange_start(gl, lays, "rs_pair_start_" + group)
            Comm.pending[group] = (names, lays) + tuple(started[:4])
            return started[4][0, 0]

        @staticmethod
        def grads_continue(group, after):
            names, lays, ssem, rsem, gl, lands = Comm.pending[group]
            gl, from_sib = _pair_exchange_wait(ssem, rsem, gl, lands, lays, after, "rs_pair_wait_" + group)
            sums = [_pair_sum(pos, g, r, lay, "rs_pair_sum_" + nm) for nm, g, r, lay in zip(names, gl, from_sib, lays)]
            started = _chip_exchange_start(sums, "rs_chips_start_" + group)
            Comm.pending[group] = (names,) + tuple(started[:4])
            return started[4][0, 0]

        @staticmethod
        def finish(group, after):
            names, ssem, rsem, sums, lands = Comm.pending[group]
            sums, lands = _chip_exchange_wait(ssem, rsem, sums, lands, after, "rs_chips_wait_" + group)
            for nm, s, r in zip(names, sums, lands):
                if nm == "w_in":
                    outs = _final_sum_adam(pos, s, r, tr_in(w[nm]), tr_in(mo[nm]), tr_in(vo[nm]), "adam_" + nm)
                    res[nm] = [tr_in(o) for o in outs]
                else:
                    res[nm] = _final_sum_adam(pos, s, r, w[nm], mo[nm], vo[nm], "adam_" + nm)
            return res[names[-1]][0]

    vec = lambda a: a.reshape(1, -1)
    loss, grad_x, small = _local_step(
        x.reshape(S, D), loss_target.reshape(S, D), mod, vec(norm1_g), vec(norm2_g), vec(final_norm_g), w_in_p,
        full["dn_conv_w"], vec(dn_a_log), vec(dn_dt_bias), vec(dn_norm_g), full["cf_conv_w"], vec(cf_ln_g),
        vec(cf_ln_b), full["ffn_conv_w"], Comm)

    done_a = Comm.finish("a", grad_x)
    done_b = Comm.finish("b", done_a)

    small["b_ada"] = small.pop("mod")
    packed = _pack_small(small) + 0.0 * done_b.reshape(-1)[0]
    g_small = _all_gather([packed], "gather_small")[0].reshape(NDEV, -1)
    outs = _small_adam(g_small, _pack_small({nm: w[nm] for nm, _ in SMALL}), _pack_small({nm: mo[nm] for nm, _ in SMALL}),
                       _pack_small({nm: vo[nm] for nm, _ in SMALL}))
    unpacked = [_unpack_small(o, shapes) for o in outs]
    for nm, _ in SMALL:
        res[nm] = [u[nm] for u in unpacked]

    dmod_sel = lax.dynamic_slice(g_small[:, :6 * D], (0, me * ncol), (NDEV, ncol))
    outs = _ada_bwd_adam(c_all, dmod_sel, mat(w_ada), mat(m_w_ada), mat(v_w_ada))
    res["w_ada"] = [o.reshape(shapes["w_ada"]) for o in outs]
    Comm.finish("c", jnp.concatenate([done_b.reshape(-1)[:LANE], outs[0].reshape(-1)[:LANE]]))

    loss = lax.psum(loss.reshape(()), ("x", "y", "c"))
    out = [loss, grad_x.reshape(x.shape)]
    for k in range(4):
        out += [res[nm][k] for nm in NAMES]
    return tuple(out)
```

```python
import functools

import jax
import jax.numpy as jnp
from jax import lax
from jax.experimental import pallas as pl
from jax.experimental.pallas import tpu as pltpu

F32 = jnp.float32
BF16 = jnp.bfloat16
HI = lax.Precision.HIGHEST
MESH = pl.DeviceIdType.MESH
ANY = pl.BlockSpec(memory_space=pl.ANY)

NDEV = 8
D = 2048
S = 2048
H = 8
DH = 128
DNW = H * DH
CFW = 1024
CFK = 31
DNK = 4
FFN = 5632
FFK = 3
CH = 64
NCH = S // CH
EPS = 1e-6
NIN = 10256
NINP = 10368
O_Z, O_GA, O_GB, O_GLU, O_SM = 3072, 4096, 6144, 8192, 10240
LANE = 128
TS = 256
VMEM_LIMIT = 56 * 1024 * 1024

ADAM_LR, ADAM_B1, ADAM_B2, ADAM_EPS, ADAM_WD, ADAM_STEP = 0.001, 0.9, 0.999, 1e-08, 0.01, 10


def _call(body, name, out_shape, grid=(), in_specs=None, out_specs=None, scratch=(), sem=None, aliases=None):
    kw = {}
    if aliases:
        kw["input_output_aliases"] = aliases
    if in_specs is not None:
        kw["in_specs"] = in_specs
    if out_specs is not None:
        kw["out_specs"] = out_specs
    return pl.pallas_call(
        body, out_shape=out_shape, grid=grid, scratch_shapes=scratch, name=name,
        compiler_params=pltpu.CompilerParams(dimension_semantics=sem, vmem_limit_bytes=VMEM_LIMIT), **kw)


def _sds(shape, dtype=F32):
    return jax.ShapeDtypeStruct(shape, dtype)


def _tile(dim, pref):
    if dim <= pref:
        return dim
    best = None
    for t in range(LANE, pref + 1, LANE):
        if dim % t == 0:
            best = t
    assert best is not None, (dim, pref)
    return best


def _sigmoid(x):
    return 1.0 / (1.0 + jnp.exp(-x))


def _silu(x):
    return x * _sigmoid(x)


def _dsilu(x):
    s = _sigmoid(x)
    return s * (1.0 + x * (1.0 - s))


def _silu_both(x):
    s = _sigmoid(x)
    return x * s, s * (1.0 + x * (1.0 - s))


def _softplus(x):
    return jnp.maximum(x, 0.0) + jnp.log(1.0 + jnp.exp(-jnp.abs(x)))


def _dot(a, b, dims, precision=None):
    return lax.dot_general(a, b, (dims, ((), ())), preferred_element_type=F32, precision=precision)


NN = ((1,), (0,))
NT = ((1,), (1,))
TN = ((0,), (0,))


def _my_pos():
    return lax.axis_index("x"), lax.axis_index("y"), lax.axis_index("c")


def _mm(a, b, mode, out_dtype, name, tm=1024, tn=1024, tk=2048, a2=None, b2=None, dep=None):
    sharded = b.ndim == 3
    if sharded and mode == "nn":
        cs = b.shape[2]
        (m, k), n = a.shape, NDEV * cs
        gs = max(1, tn // cs)
        tm, tn, tk = _tile(m, tm), gs * cs, _tile(k, tk)
    elif sharded:
        assert mode == "nt"
        cs = b.shape[2]
        m, n, k = a.shape[0], b.shape[1], NDEV * cs
        gs = max(1, tk // cs)
        tm, tn, tk = _tile(m, tm), _tile(n, tn), gs * cs
    else:
        if mode == "nn":
            (m, k), (k2, n) = a.shape, b.shape
        elif mode == "nt":
            (m, k), (n, k2) = a.shape, b.shape
        else:
            (k, m), (k2, n) = a.shape, b.shape
        assert k == k2, (a.shape, b.shape, mode)
        n = n * (2 if b2 is not None else 1)
        tm, tn, tk = _tile(m, tm), _tile(n // (2 if b2 is not None else 1), tn), _tile(k, tk)
    nk, nj = k // tk, n // tn
    halfk, halfj = nk // 2, nj // 2
    dims = {"nn": NN, "nt": NT, "tn": TN}[mode]

    n_in = 2 + (a2 is not None) + (b2 is not None) + (dep is not None)

    def body(*refs):
        a_ref, b_ref = refs[0], refs[1]
        x_ref = refs[2] if (a2 is not None or b2 is not None) else None
        o_ref = refs[n_in]
        acc_ref = refs[n_in + 1] if nk > 1 else None
        j, kk = pl.program_id(1), pl.program_id(2)

        if nk > 1:
            @pl.when(kk == 0)
            def _():
                acc_ref[...] = jnp.zeros_like(acc_ref)

        def accumulate(product, cols=slice(None)):
            if nk == 1:
                o_ref[:, cols] = product().astype(o_ref.dtype)
            else:
                acc_ref[:, cols] += product()

        if sharded and mode == "nn":
            for q in range(gs):
                accumulate(lambda q=q: _dot(a_ref[...], b_ref[q], NN), slice(q * cs, (q + 1) * cs))
        elif sharded:
            def contract(lhs_ref):
                def product():
                    part = None
                    for q in range(gs):
                        term = _dot(lhs_ref[:, q * cs:(q + 1) * cs], b_ref[q], NT)
                        part = term if part is None else part + term
                    return part
                accumulate(product)

            if a2 is None:
                contract(a_ref)
            else:
                pl.when(kk < halfk)(lambda: contract(a_ref))
                pl.when(kk >= halfk)(lambda: contract(x_ref))
        elif b2 is not None:
            pl.when(j < halfj)(lambda: accumulate(lambda: _dot(a_ref[...], b_ref[...], dims)))
            pl.when(j >= halfj)(lambda: accumulate(lambda: _dot(a_ref[...], x_ref[...], dims)))
        else:
            accumulate(lambda: _dot(a_ref[...], b_ref[...], dims))

        if nk > 1:
            @pl.when(kk == nk - 1)
            def _():
                o_ref[...] = acc_ref[...].astype(o_ref.dtype)

    ins, in_specs = [a], []
    if mode == "tn":
        in_specs.append(pl.BlockSpec((tk, tm), lambda i, j, kk: (kk, i)))
    elif a2 is not None:
        in_specs.append(pl.BlockSpec((tm, tk), lambda i, j, kk: (i, jnp.minimum(kk, halfk - 1))))
    else:
        in_specs.append(pl.BlockSpec((tm, tk), lambda i, j, kk: (i, kk)))
    ins.append(b)
    if sharded and mode == "nn":
        in_specs.append(pl.BlockSpec((gs, tk, cs), lambda i, j, kk: (j, kk, 0)))
    elif sharded:
        in_specs.append(pl.BlockSpec((gs, tn, cs), lambda i, j, kk: (kk, j, 0)))
    elif mode == "nt":
        in_specs.append(pl.BlockSpec((tn, tk), lambda i, j, kk: (j, kk)))
    elif b2 is not None:
        in_specs.append(pl.BlockSpec((tk, tn), lambda i, j, kk: (kk, jnp.minimum(j, halfj - 1))))
    else:
        in_specs.append(pl.BlockSpec((tk, tn), lambda i, j, kk: (kk, j)))
    if a2 is not None:
        ins.append(a2)
        in_specs.append(pl.BlockSpec((tm, tk), lambda i, j, kk: (i, jnp.maximum(kk - halfk, 0))))
    if b2 is not None:
        ins.append(b2)
        in_specs.append(pl.BlockSpec((tk, tn), lambda i, j, kk: (kk, jnp.maximum(j - halfj, 0))))
    if dep is not None:
        ins.append(dep)
        in_specs.append(ANY)
    return _call(body, name, _sds((m, n), out_dtype), grid=(m // tm, nj, nk),
                 in_specs=in_specs, out_specs=pl.BlockSpec((tm, tn), lambda i, j, kk: (i, j)),
                 scratch=[pltpu.VMEM((tm, tn), F32)] if nk > 1 else [],
                 sem=("parallel", "parallel", "arbitrary"))(*ins)


def _ada_fwd(c_all, w_sh, b_sh):
    n = w_sh.shape[1]
    tn = 512

    def body(c_ref, w_ref, b_ref, o_ref):
        ca = _silu(c_ref[...]).astype(BF16)
        o_ref[...] = _dot(ca, w_ref[...].astype(BF16), NN) + b_ref[...]

    return _call(body, "ada_fwd", _sds((NDEV, n)), grid=(n // tn,),
                 in_specs=[pl.BlockSpec((NDEV, D), lambda j: (0, 0)), pl.BlockSpec((D, tn), lambda j: (0, j)),
                           pl.BlockSpec((1, tn), lambda j: (0, j))],
                 out_specs=pl.BlockSpec((NDEV, tn), lambda j: (0, j)), sem=("parallel",))(c_all, w_sh, b_sh)


def _adam(w, g, m, v):
    m = ADAM_B1 * m + (1.0 - ADAM_B1) * g
    v = ADAM_B2 * v + (1.0 - ADAM_B2) * (g * g)
    m_hat = m / (1.0 - ADAM_B1 ** ADAM_STEP)
    v_hat = v / (1.0 - ADAM_B2 ** ADAM_STEP)
    delta = -ADAM_LR * (m_hat / (jnp.sqrt(v_hat) + ADAM_EPS) + ADAM_WD * w)
    return delta, m, v


def _ada_bwd_adam(c_all, dmod_sel, w, m, v):
    r, n = w.shape
    tr = 256

    def body(c_ref, d_ref, w_ref, m_ref, v_ref, g_ref, dl_ref, nm_ref, nv_ref):
        ca = _silu(c_ref[...])
        g = _dot(ca, d_ref[...], TN, precision=HI)
        dl, nm, nv = _adam(w_ref[...], g, m_ref[...], v_ref[...])
        g_ref[...] = g
        dl_ref[...] = dl
        nm_ref[...] = nm
        nv_ref[...] = nv

    big = pl.BlockSpec((tr, n), lambda i: (i, 0))
    return _call(body, "ada_bwd_adam", [_sds((r, n))] * 4, grid=(r // tr,),
                 in_specs=[pl.BlockSpec((NDEV, tr), lambda i: (0, i)), pl.BlockSpec((NDEV, n), lambda i: (0, 0)),
                           big, big, big],
                 out_specs=[big] * 4, sem=("parallel",))(c_all, dmod_sel, w, m, v)


def _row_spec(width=D):
    return pl.BlockSpec((TS, width), lambda i: (i, 0))


def _vec_spec(width=D):
    return pl.BlockSpec((1, width), lambda i: (0, 0))


def _acc_spec(width=D):
    return pl.BlockSpec((8, width), lambda i: (0, 0))


def _norm_mod(x, g, sc, sh, name):
    def body(x_ref, g_ref, sc_ref, sh_ref, o_ref):
        xv = x_ref[...]
        r = lax.rsqrt(jnp.mean(xv * xv, axis=-1, keepdims=True) + EPS)
        o_ref[...] = ((xv * r) * g_ref[...] * (1.0 + sc_ref[...]) + sh_ref[...]).astype(BF16)

    return _call(body, name, _sds((S, D), BF16), grid=(S // TS,),
                 in_specs=[_row_spec(), _vec_spec(), _vec_spec(), _vec_spec()], out_specs=_row_spec(),
                 sem=("parallel",))(x, g, sc, sh)


def _resid_norm_mod(x, mix, gt, g, sc, sh, name):
    def body(x_ref, mix_ref, gt_ref, g_ref, sc_ref, sh_ref, x2_ref, o_ref):
        xv = x_ref[...] + gt_ref[...] * mix_ref[...]
        x2_ref[...] = xv
        r = lax.rsqrt(jnp.mean(xv * xv, axis=-1, keepdims=True) + EPS)
        o_ref[...] = ((xv * r) * g_ref[...] * (1.0 + sc_ref[...]) + sh_ref[...]).astype(BF16)

    return _call(body, name, [_sds((S, D)), _sds((S, D), BF16)], grid=(S // TS,),
                 in_specs=[_row_spec(), _row_spec()] + [_vec_spec()] * 4, out_specs=[_row_spec(), _row_spec()],
                 sem=("parallel",))(x, mix, gt, g, sc, sh)


def _acc_rows(acc_ref, rows):
    @pl.when(pl.program_id(0) == 0)
    def _():
        acc_ref[...] = jnp.zeros_like(acc_ref)

    for k, row in enumerate(rows):
        acc_ref[k:k + 1, :] += row


def _loss_head(x2, f, tgt, gt2, gf):
    def body(x2_ref, f_ref, t_ref, gt_ref, gf_ref, dx_ref, df_ref, acc_ref):
        fv = f_ref[...]
        x3 = x2_ref[...] + gt_ref[...] * fv
        r = lax.rsqrt(jnp.mean(x3 * x3, axis=-1, keepdims=True) + EPS)
        xn = x3 * r
        e = xn * gf_ref[...] - t_ref[...]
        loss = 0.5 * jnp.sum(jnp.mean(e * e, axis=-1, keepdims=True), axis=0, keepdims=True)
        dy = e * (1.0 / D)
        dxn = dy * gf_ref[...]
        dx3 = r * (dxn - xn * jnp.mean(dxn * xn, axis=-1, keepdims=True))
        dx_ref[...] = dx3
        df_ref[...] = (dx3 * gt_ref[...]).astype(BF16)
        _acc_rows(acc_ref, [jnp.sum(dy * xn, axis=0, keepdims=True), jnp.sum(dx3 * fv, axis=0, keepdims=True),
                            jnp.broadcast_to(loss, (1, D))])

    return _call(body, "loss_head", [_sds((S, D)), _sds((S, D), BF16), _sds((8, D))], grid=(S // TS,),
                 in_specs=[_row_spec(), _row_spec(), _row_spec(), _vec_spec(), _vec_spec()],
                 out_specs=[_row_spec(), _row_spec(), _acc_spec()], sem=("arbitrary",))(x2, f, tgt, gt2, gf)


def _norm_mod_bwd(dhn, x, dres, g, sc, name, mix=None, gt=None):
    gated = mix is not None

    def body(*refs):
        if gated:
            dhn_ref, x_ref, dres_ref, g_ref, sc_ref, mix_ref, gt_ref, dx_ref, dmix_ref, acc_ref = refs
        else:
            dhn_ref, x_ref, dres_ref, g_ref, sc_ref, dx_ref, acc_ref = refs
        xv = x_ref[...]
        dh = dhn_ref[...]
        r = lax.rsqrt(jnp.mean(xv * xv, axis=-1, keepdims=True) + EPS)
        xn = xv * r
        gv = g_ref[...]
        sc1 = 1.0 + sc_ref[...]
        dxn = dh * gv * sc1
        dx = dres_ref[...] + r * (dxn - xn * jnp.mean(dxn * xn, axis=-1, keepdims=True))
        dx_ref[...] = dx
        rows = [jnp.sum(dh, axis=0, keepdims=True), jnp.sum(dh * xn * gv, axis=0, keepdims=True),
                jnp.sum(dh * xn * sc1, axis=0, keepdims=True)]
        if gated:
            rows.append(jnp.sum(dx * mix_ref[...], axis=0, keepdims=True))
            dmix_ref[...] = (dx * gt_ref[...]).astype(BF16)
        _acc_rows(acc_ref, rows)

    ins = [dhn, x, dres, g, sc]
    in_specs = [_row_spec(), _row_spec(), _row_spec(), _vec_spec(), _vec_spec()]
    outs = [_sds((S, D))]
    out_specs = [_row_spec()]
    if gated:
        ins += [mix, gt]
        in_specs += [_row_spec(), _vec_spec()]
        outs.append(_sds((S, D), BF16))
        out_specs.append(_row_spec())
    outs.append(_sds((8, D)))
    out_specs.append(_acc_spec())
    return _call(body, name, outs, grid=(S // TS,), in_specs=in_specs, out_specs=out_specs,
                 sem=("arbitrary",))(*ins)


RC = 256


def _conv_fwd_rows(pad_ref, w_ref, kw, head, r0):
    acc = None
    for k in range(kw):
        term = w_ref[k:k + 1, :] * pad_ref[pl.ds(head - (kw - 1) + k + r0, RC), :]
        acc = term if acc is None else acc + term
    return acc


def _conv_bwd_rows(pad2_ref, w_ref, kw, r0):
    acc = None
    for k in range(kw):
        term = w_ref[k:k + 1, :] * pad2_ref[pl.ds(kw - 1 - k + r0, RC), :]
        acc = term if acc is None else acc + term
    return acc


def _conv_dw(pad_ref, dout_ref, dw_ref, kw, head):
    for k in range(kw):
        acc = None
        for r0 in range(0, S, RC):
            term = jnp.sum(pad_ref[pl.ds(head - (kw - 1) + k + r0, RC), :] * dout_ref[pl.ds(r0, RC), :],
                           axis=0, keepdims=True)
            acc = term if acc is None else acc + term
        dw_ref[k:k + 1, :] = acc


def _col_spec(width, off_blocks=0):
    return pl.BlockSpec((S, width), lambda j: (0, j + off_blocks))


def _dn_pre_fwd(proj, conv_w):
    head = 8

    def body(x_ref, w_ref, o_ref, pad_ref):
        j = pl.program_id(0)
        pad_ref[pl.ds(0, head), :] = jnp.zeros((head, DH), F32)
        pad_ref[pl.ds(head, S), :] = x_ref[...]
        scale = jnp.where(j < H, DH ** -0.5, 1.0)
        for r0 in range(0, S, RC):
            y = _silu(_conv_fwd_rows(pad_ref, w_ref, DNK, head, r0))
            rinv = lax.rsqrt(jnp.sum(y * y, axis=-1, keepdims=True) + EPS)
            o_ref[pl.ds(r0, RC), :] = jnp.where(j < 2 * H, y * rinv * scale, y)

    return _call(body, "dn_pre_fwd", _sds((S, 3 * DNW)), grid=(3 * H,),
                 in_specs=[_col_spec(DH), pl.BlockSpec((DNK, DH), lambda j: (0, j))], out_specs=_col_spec(DH),
                 scratch=[pltpu.VMEM((S + head, DH), F32)], sem=("parallel",))(proj, conv_w)


def _dn_pre_bwd(dq, dk, dv, proj, conv_w, dproj):
    head = 8

    def body(dq_ref, dk_ref, dv_ref, x_ref, w_ref, dproj_in, dx_ref, dw_ref, pad_ref, pad2_ref):
        j = pl.program_id(0)
        pad_ref[pl.ds(0, head), :] = jnp.zeros((head, DH), F32)
        pad_ref[pl.ds(head, S), :] = x_ref[...]
        pad2_ref[pl.ds(S, head), :] = jnp.zeros((head, DH), F32)
        scale = jnp.where(j < H, DH ** -0.5, 1.0)
        for r0 in range(0, S, RC):
            xc = _conv_fwd_rows(pad_ref, w_ref, DNK, head, r0)
            y, dy_dxc = _silu_both(xc)
            rinv = lax.rsqrt(jnp.sum(y * y, axis=-1, keepdims=True) + EPS)
            yn = y * rinv
            rows = pl.ds(r0, RC)
            do = jnp.where(j < H, dq_ref[rows, :], jnp.where(j < 2 * H, dk_ref[rows, :], dv_ref[rows, :]))
            dy_n = scale * rinv * (do - yn * jnp.sum(do * yn, axis=-1, keepdims=True))
            dy = jnp.where(j < 2 * H, dy_n, do)
            pad2_ref[rows, :] = dy * dy_dxc
        for r0 in range(0, S, RC):
            dx_ref[pl.ds(r0, RC), :] = _conv_bwd_rows(pad2_ref, w_ref, DNK, r0).astype(BF16)
        _conv_dw(pad_ref, pad2_ref, dw_ref, DNK, head)

    wspec = pl.BlockSpec((DNK, DH), lambda j: (0, j))
    head_col = lambda lo: pl.BlockSpec((S, DH), lambda j: (0, jnp.clip(j - lo, 0, H - 1)))
    return _call(body, "dn_pre_bwd", [_sds((S, NINP), BF16), _sds((DNK, 3 * DNW))], grid=(3 * H,),
                 in_specs=[head_col(0), head_col(H), head_col(2 * H), _col_spec(DH), wspec, ANY],
                 out_specs=[_col_spec(DH), wspec],
                 scratch=[pltpu.VMEM((S + head, DH), F32), pltpu.VMEM((S + head, DH), F32)],
                 sem=("parallel",), aliases={5: 0})(dq, dk, dv, proj, conv_w, dproj)


CF_HEAD = 32
CF_VAL = pl.BlockSpec((S, LANE), lambda j: (0, O_GLU // LANE + 2 * j))
CF_GL = pl.BlockSpec((S, LANE), lambda j: (0, O_GLU // LANE + 2 * j + 1))


def _cf_conv_fwd(proj, conv_w):
    def body(val_ref, gl_ref, w_ref, o_ref, pad_ref):
        pad_ref[pl.ds(0, CF_HEAD), :] = jnp.zeros((CF_HEAD, LANE), F32)
        pad_ref[pl.ds(CF_HEAD, S), :] = val_ref[...] * _sigmoid(gl_ref[...])
        for r0 in range(0, S, RC):
            o_ref[pl.ds(r0, RC), :] = _conv_fwd_rows(pad_ref, w_ref, CFK, CF_HEAD, r0)

    wspec = pl.BlockSpec((CFK, LANE), lambda j: (0, j))
    return _call(body, "cf_conv_fwd", _sds((S, CFW)), grid=(CFW // LANE,),
                 in_specs=[CF_VAL, CF_GL, wspec], out_specs=_col_spec(LANE),
                 scratch=[pltpu.VMEM((S + CF_HEAD, LANE), F32)], sem=("parallel",))(proj, proj, conv_w)


def _cf_conv_bwd(du1, proj, conv_w, dproj):
    def body(d_ref, val_ref, gl_ref, w_ref, dproj_in, dp_ref, dw_ref, pad_ref, pad2_ref):
        sg = _sigmoid(gl_ref[...])
        pad_ref[pl.ds(0, CF_HEAD), :] = jnp.zeros((CF_HEAD, LANE), F32)
        pad_ref[pl.ds(CF_HEAD, S), :] = val_ref[...] * sg
        pad2_ref[pl.ds(0, S), :] = d_ref[...]
        pad2_ref[pl.ds(S, CF_HEAD), :] = jnp.zeros((CF_HEAD, LANE), F32)
        for r0 in range(0, S, RC):
            du0 = _conv_bwd_rows(pad2_ref, w_ref, CFK, r0)
            rows = pl.ds(r0, RC)
            sgr = _sigmoid(gl_ref[rows, :])
            dp_ref[rows, 0:LANE] = (du0 * sgr).astype(BF16)
            dp_ref[rows, LANE:2 * LANE] = (du0 * val_ref[rows, :] * sgr * (1.0 - sgr)).astype(BF16)
        _conv_dw(pad_ref, pad2_ref, dw_ref, CFK, CF_HEAD)

    wspec = pl.BlockSpec((CFK, LANE), lambda j: (0, j))
    return _call(body, "cf_conv_bwd", [_sds((S, NINP), BF16), _sds((CFK, CFW))], grid=(CFW // LANE,),
                 in_specs=[_col_spec(LANE), CF_VAL, CF_GL, wspec, ANY],
                 out_specs=[pl.BlockSpec((S, 2 * LANE), lambda j: (0, O_GLU // (2 * LANE) + j)), wspec],
                 scratch=[pltpu.VMEM((S + CF_HEAD, LANE), F32), pltpu.VMEM((S + CF_HEAD, LANE), F32)],
                 sem=("parallel",), aliases={4: 0})(du1, proj, proj, conv_w, dproj)


def _cf_ln_fwd(u1, g, b):
    def body(u_ref, g_ref, b_ref, o_ref):
        u = u_ref[...]
        mu = jnp.mean(u, axis=-1, keepdims=True)
        xc = u - mu
        y = xc * lax.rsqrt(jnp.mean(xc * xc, axis=-1, keepdims=True) + EPS)
        o_ref[...] = _silu(y * g_ref[...] + b_ref[...]).astype(BF16)

    return _call(body, "cf_ln_fwd", _sds((S, CFW), BF16), grid=(S // TS,),
                 in_specs=[_row_spec(CFW), _vec_spec(CFW), _vec_spec(CFW)], out_specs=_row_spec(CFW),
                 sem=("parallel",))(u1, g, b)


def _cf_ln_bwd(du3, u1, g, b):
    def body(d_ref, u_ref, g_ref, b_ref, du_ref, acc_ref):
        u = u_ref[...]
        mu = jnp.mean(u, axis=-1, keepdims=True)
        xc = u - mu
        rstd = lax.rsqrt(jnp.mean(xc * xc, axis=-1, keepdims=True) + EPS)
        xh = xc * rstd
        du2 = d_ref[...] * _dsilu(xh * g_ref[...] + b_ref[...])
        dxh = du2 * g_ref[...]
        du_ref[...] = rstd * (dxh - jnp.mean(dxh, axis=-1, keepdims=True)
                              - xh * jnp.mean(dxh * xh, axis=-1, keepdims=True))
        _acc_rows(acc_ref, [jnp.sum(du2 * xh, axis=0, keepdims=True), jnp.sum(du2, axis=0, keepdims=True)])

    return _call(body, "cf_ln_bwd", [_sds((S, CFW)), _sds((8, CFW))], grid=(S // TS,),
                 in_specs=[_row_spec(CFW), _row_spec(CFW), _vec_spec(CFW), _vec_spec(CFW)],
                 out_specs=[_row_spec(CFW), _acc_spec(CFW)], sem=("arbitrary",))(du3, u1, g, b)


FB = 256
FNB = FFN // FB
FF_HEAD = 8


def _ffn_mid_fwd(upall, conv_w):
    def body(gate_ref, up_ref, w_ref, o_ref, pad_ref):
        pad_ref[pl.ds(0, FF_HEAD), :] = jnp.zeros((FF_HEAD, FB), F32)
        pad_ref[pl.ds(FF_HEAD, S), :] = gate_ref[...]
        for r0 in range(0, S, RC):
            gc = _conv_fwd_rows(pad_ref, w_ref, FFK, FF_HEAD, r0)
            o_ref[pl.ds(r0, RC), :] = (_silu(gc) * up_ref[pl.ds(r0, RC), :]).astype(BF16)

    wspec = pl.BlockSpec((FFK, FB), lambda j: (0, j))
    return _call(body, "ffn_mid_fwd", _sds((S, FFN), BF16), grid=(FNB,),
                 in_specs=[_col_spec(FB), _col_spec(FB, FNB), wspec], out_specs=_col_spec(FB),
                 scratch=[pltpu.VMEM((S + FF_HEAD, FB), F32)], sem=("parallel",))(upall, upall, conv_w)


def _ffn_mid_bwd(dh, upall, conv_w):
    def body(d_ref, gate_ref, up_ref, w_ref, dgate_ref, dup_ref, dw_ref, pad_ref, pad2_ref):
        pad_ref[pl.ds(0, FF_HEAD), :] = jnp.zeros((FF_HEAD, FB), F32)
        pad_ref[pl.ds(FF_HEAD, S), :] = gate_ref[...]
        pad2_ref[pl.ds(S, FF_HEAD), :] = jnp.zeros((FF_HEAD, FB), F32)
        for r0 in range(0, S, RC):
            rows = pl.ds(r0, RC)
            gc = _conv_fwd_rows(pad_ref, w_ref, FFK, FF_HEAD, r0)
            dhv = d_ref[rows, :]
            act, dact = _silu_both(gc)
            dup_ref[rows, :] = (dhv * act).astype(BF16)
            pad2_ref[rows, :] = dhv * up_ref[rows, :] * dact
        for r0 in range(0, S, RC):
            dgate_ref[pl.ds(r0, RC), :] = _conv_bwd_rows(pad2_ref, w_ref, FFK, r0).astype(BF16)
        _conv_dw(pad_ref, pad2_ref, dw_ref, FFK, FF_HEAD)

    wspec = pl.BlockSpec((FFK, FB), lambda j: (0, j))
    return _call(body, "ffn_mid_bwd", [_sds((S, FFN), BF16), _sds((S, FFN), BF16), _sds((FFK, FFN))],
                 grid=(FNB,), in_specs=[_col_spec(FB), _col_spec(FB), _col_spec(FB, FNB), wspec],
                 out_specs=[_col_spec(FB), _col_spec(FB), wspec],
                 scratch=[pltpu.VMEM((S + FF_HEAD, FB), F32), pltpu.VMEM((S + FF_HEAD, FB), F32)],
                 sem=("parallel",))(dh, upall, upall, conv_w)


GT = 256
SM_BLK = O_SM // LANE


def _chunk_tri(lower):
    r = lax.broadcasted_iota(jnp.int32, (GT, GT), 0)
    c = lax.broadcasted_iota(jnp.int32, (GT, GT), 1)
    same = (r // CH) == (c // CH)
    tri = (c <= r) if lower else (c >= r)
    return jnp.where(same & tri, 1.0, 0.0).astype(F32)


def _gates_fwd(proj, alog_v, dtb_v):
    def body(sm_ref, al_ref, dt_ref, o_ref):
        lane = lax.broadcasted_iota(jnp.int32, (GT, LANE), 1)
        tri = _chunk_tri(True)
        na = -jnp.exp(al_ref[...])
        for r0 in range(0, S, GT):
            sm = sm_ref[pl.ds(r0, GT), :]
            raw = jnp.where((lane >= H) & (lane < 2 * H), na * _softplus(sm + dt_ref[...]), 0.0)
            gc = _dot(tri, raw, NN, precision=HI)
            o_ref[pl.ds(r0, GT), :] = jnp.where(lane < H, _sigmoid(sm), gc)

    return _call(body, "gates_fwd", _sds((S, LANE)), grid=(1,),
                 in_specs=[pl.BlockSpec((S, LANE), lambda i: (0, SM_BLK)), _vec_spec(LANE), _vec_spec(LANE)],
                 out_specs=pl.BlockSpec((S, LANE), lambda i: (0, 0)), sem=("arbitrary",))(proj, alog_v, dtb_v)


def _gates_bwd(dgb, proj, alog_v, dtb_v, dproj):
    def body(d_ref, sm_ref, al_ref, dt_ref, dproj_in, o_ref, acc_ref):
        lane = lax.broadcasted_iota(jnp.int32, (GT, LANE), 1)
        is_g = (lane >= H) & (lane < 2 * H)
        tri = _chunk_tri(False)
        na = -jnp.exp(al_ref[...])
        d_al = jnp.zeros((1, LANE), F32)
        d_dt = jnp.zeros((1, LANE), F32)
        for r0 in range(0, S, GT):
            sm = sm_ref[pl.ds(r0, GT), :]
            dv = d_ref[pl.ds(r0, GT), :]
            z = sm + dt_ref[...]
            draw = _dot(tri, jnp.where(is_g, dv, 0.0), NN, precision=HI)
            dlogit = jnp.where(is_g, draw * na * _sigmoid(z), 0.0)
            d_al = d_al + jnp.sum(jnp.where(is_g, draw * na * _softplus(z), 0.0), axis=0, keepdims=True)
            d_dt = d_dt + jnp.sum(dlogit, axis=0, keepdims=True)
            bt = _sigmoid(sm)
            o_ref[pl.ds(r0, GT), :] = jnp.where(lane < H, dv * bt * (1.0 - bt), dlogit).astype(BF16)
        acc_ref[...] = jnp.zeros_like(acc_ref)
        acc_ref[0:1, :] = d_al
        acc_ref[1:2, :] = d_dt

    return _call(body, "gates_bwd", [_sds((S, NINP), BF16), _sds((8, LANE))], grid=(1,),
                 in_specs=[pl.BlockSpec((S, LANE), lambda i: (0, 0)), pl.BlockSpec((S, LANE), lambda i: (0, SM_BLK)),
                           _vec_spec(LANE), _vec_spec(LANE), ANY],
                 out_specs=[pl.BlockSpec((S, LANE), lambda i: (0, SM_BLK)), _acc_spec(LANE)],
                 sem=("arbitrary",), aliases={4: 0})(dgb, proj, alog_v, dtb_v, dproj)


HB = 4


def _each(fn, *lists):
    return [fn(*args) for args in zip(*lists)]


def _neumann_inv(a, eye):
    p = _each(lambda m: -m, a)
    t = _each(lambda m: eye + m, p)
    for _ in range(5):
        p = _each(lambda m: _dot(m, m, NN, precision=HI), p)
        t = _each(lambda tt, pp: tt + _dot(tt, pp, NN, precision=HI), t, p)
    return t


def _head_specs():
    q = pl.BlockSpec((S, HB * DH), lambda h: (0, h), pipeline_mode=ONE_BUF)
    k = pl.BlockSpec((S, HB * DH), lambda h: (0, H // HB + h), pipeline_mode=ONE_BUF)
    v = pl.BlockSpec((S, HB * DH), lambda h: (0, 2 * H // HB + h), pipeline_mode=ONE_BUF)
    gb = pl.BlockSpec((HB, S, DH), lambda h: (h, 0, 0), pipeline_mode=ONE_BUF)
    gr = pl.BlockSpec((HB, NCH, CH), lambda h: (h, 0, 0))
    return q, k, v, gb, gr


ONE_BUF = pl.Buffered(1)
ST_SPEC = pl.BlockSpec((HB, NCH, DH, DH), lambda h: (h, 0, 0, 0), pipeline_mode=ONE_BUF)
TM_SPEC = pl.BlockSpec((HB, NCH, CH, CH), lambda h: (h, 0, 0, 0), pipeline_mode=ONE_BUF)
HCOL = pl.BlockSpec((S, HB * DH), lambda h: (0, h), pipeline_mode=ONE_BUF)


def _delta_fwd(qkvn, gb, gr, bb):
    def body(q_ref, k_ref, v_ref, gb_ref, gr_ref, bb_ref, o_ref, st_ref, tm_ref):
        ri = lax.broadcasted_iota(jnp.int32, (CH, CH), 0)
        ci = lax.broadcasted_iota(jnp.int32, (CH, CH), 1)
        strict = ri > ci
        causal = ri >= ci
        eye = jnp.where(ri == ci, 1.0, 0.0).astype(F32)

        hs = list(range(HB))
        cols = [slice(hh * DH, (hh + 1) * DH) for hh in hs]
        bf = lambda m: m.astype(BF16)

        def local(n):
            rows = pl.ds(pl.multiple_of(n * CH, CH), CH)
            c = dict(rows=rows, n=n)
            c["q"] = [q_ref[rows, cc] for cc in cols]
            c["k"] = [k_ref[rows, cc] for cc in cols]
            c["v"] = [v_ref[rows, cc] for cc in cols]
            c["g"] = [gb_ref[hh, rows, :] for hh in hs]
            c["beta"] = [bb_ref[hh, rows, :] for hh in hs]
            diff = [c["g"][hh][:, :CH] - gr_ref[hh, pl.ds(n, 1), :] for hh in hs]
            c["el"] = _each(lambda d: jnp.exp(jnp.where(causal, d, 0.0)), diff)
            c["eg"] = _each(jnp.exp, c["g"])
            c["gl"] = _each(lambda m: m[CH - 1:CH, :], c["g"])
            c["kb"] = _each(lambda x, y: x * y, c["k"], c["beta"])
            c["kbf"] = _each(bf, c["k"])
            c["a"] = _each(lambda x, y, e: jnp.where(strict, _dot(bf(x), y, NT) * e, 0.0), c["kb"], c["kbf"], c["el"])
            return c

        def advance(c, t, sts):
            n, rows = c["n"], c["rows"]
            for hh in hs:
                tm_ref[hh, n] = t[hh]
                st_ref[hh, n] = sts[hh]
            sb = _each(bf, sts)
            r = _each(lambda vv, bb_, kk, ee, ss: vv * bb_ - _dot(bf(kk * ee), ss, NN), c["v"], c["beta"], c["kb"], c["eg"], sb)
            ub = _each(lambda tt, rr: bf(_dot(tt, rr, NN, precision=HI)), t, r)
            p = _each(lambda qq, kk, e: jnp.where(causal, _dot(bf(qq), kk, NT) * e, 0.0), c["q"], c["kbf"], c["el"])
            o = _each(lambda qq, ee, ss, pp, uu: _dot(bf(qq * ee), ss, NN) + _dot(bf(pp), uu, NN), c["q"], c["eg"], sb, p, ub)
            for hh in hs:
                o_ref[rows, cols[hh]] = o[hh]
            kd = _each(lambda kk, l, gg: kk * jnp.exp(l - gg), c["k"], c["gl"], c["g"])
            return _each(lambda st, l, kk, uu: st * jnp.exp(l) + _dot(bf(kk), uu, TN), sts, c["gl"], kd, ub)

        def step(i, sts):
            c0, c1 = local(2 * i), local(2 * i + 1)
            t = _neumann_inv(c0["a"] + c1["a"], eye)
            sts = advance(c0, t[:HB], list(sts))
            return tuple(advance(c1, t[HB:], sts))

        lax.fori_loop(0, NCH // 2, step, tuple(jnp.zeros((DH, DH), F32) for _ in hs))

    q, k, v, gbs, grs = _head_specs()
    return _call(body, "delta_fwd", [_sds((S, DNW)), _sds((H, NCH, DH, DH)), _sds((H, NCH, CH, CH))], grid=(H // HB,),
                 in_specs=[q, k, v, gbs, grs, gbs], out_specs=[HCOL, ST_SPEC, TM_SPEC],
                 sem=("parallel",))(qkvn, qkvn, qkvn, gb, gr, bb)


def _delta_bwd(qkvn, gb, gr, bb, st_all, tm_all, do_all):
    def body(q_ref, k_ref, v_ref, gb_ref, gr_ref, bb_ref, st_ref, tm_ref, do_ref,
             dq_ref, dk_ref, dv_ref, dg_ref, db_ref):
        ri = lax.broadcasted_iota(jnp.int32, (CH, CH), 0)
        ci = lax.broadcasted_iota(jnp.int32, (CH, CH), 1)
        lo_s, lo_c, up_s, up_c = ri > ci, ri >= ci, ri < ci, ri <= ci
        last_row = lax.broadcasted_iota(jnp.int32, (CH, 1), 0) == CH - 1

        def rs(mat):
            return jnp.sum(mat, axis=1, keepdims=True)

        def total(mat):
            return jnp.sum(rs(mat), axis=0, keepdims=True)

        hs = list(range(HB))
        cols = [slice(hh * DH, (hh + 1) * DH) for hh in hs]
        bf = lambda m: m.astype(BF16)
        mul = lambda x, y: x * y
        spread = jnp.full((8, DH), 1.0 / DH, F32)

        def as_row(col):
            return _dot(spread, jnp.broadcast_to(col, (CH, DH)), NT, precision=HI)[0:1, :]

        def step(i, dss):
            ns = [NCH - 1 - 2 * i, NCH - 2 - 2 * i]
            rws = [pl.ds(pl.multiple_of(n * CH, CH), CH) for n in ns]
            idx = [(cc, hh) for cc in range(2) for hh in hs]
            q = [q_ref[rws[cc], cols[hh]] for cc, hh in idx]
            k = [k_ref[rws[cc], cols[hh]] for cc, hh in idx]
            v = [v_ref[rws[cc], cols[hh]] for cc, hh in idx]
            do = [do_ref[rws[cc], cols[hh]] for cc, hh in idx]
            g = [gb_ref[hh, rws[cc], :] for cc, hh in idx]
            beta = [bb_ref[hh, rws[cc], :] for cc, hh in idx]
            t = [tm_ref[hh, ns[cc]] for cc, hh in idx]
            st = [st_ref[hh, ns[cc]] for cc, hh in idx]
            diff = [gg[:, :CH] - gr_ref[hh, pl.ds(ns[cc], 1), :] for gg, (cc, hh) in zip(g, idx)]
            el = _each(lambda d: jnp.exp(jnp.where(lo_c, d, 0.0)), diff)
            eu = _each(lambda d: jnp.exp(jnp.where(up_c, -d, 0.0)), diff)
            eg = _each(jnp.exp, g)
            gl = _each(lambda m: m[CH - 1:CH, :], g)
            egl = _each(jnp.exp, gl)
            ekd = _each(lambda l, m: jnp.exp(l - m), gl, g)
            kb = _each(mul, k, beta)
            kbg = _each(mul, kb, eg)
            qg = _each(mul, q, eg)
            kd = _each(mul, k, ekd)
            qb, kbf, kbb = _each(bf, q), _each(bf, k), _each(bf, kb)
            kbgb, qgb, kdb = _each(bf, kbg), _each(bf, qg), _each(bf, kd)
            sb, dob = _each(bf, st), _each(bf, do)
            r = _each(lambda vv, b, x, s: vv * b - _dot(x, s, NN), v, beta, kbgb, sb)
            u = _each(lambda tt, rr: _dot(tt, rr, NN, precision=HI), t, r)
            ub = _each(bf, u)
            kk = _each(lambda x, y: _dot(x, y, NT), kbb, kbf)
            qk = _each(lambda x, y: _dot(x, y, NT), qb, kbf)
            kkt = _each(lambda x, y: _dot(x, y, NT), kbf, kbb)
            qkt = _each(lambda x, y: _dot(x, y, NT), kbf, qb)
            pt = _each(lambda m, e: jnp.where(up_c, m * e, 0.0), qkt, eu)
            ds, du, dr, drb, ds_new = [], [], [], [], list(dss)
            for cc in range(2):
                sl = slice(cc * HB, (cc + 1) * HB)
                ds_c = ds_new
                dsb_c = _each(bf, ds_c)
                du_c = _each(lambda p, d, x, s: _dot(bf(p), d, NN) + _dot(x, s, NN), pt[sl], dob[sl], kdb[sl], dsb_c)
                dr_c = _each(lambda tt, d: _dot(tt, d, TN, precision=HI), t[sl], du_c)
                drb_c = _each(bf, dr_c)
                ds_new = _each(lambda x, d, e, s, y, z: _dot(x, d, TN) + e * s - _dot(y, z, TN),
                               qgb[sl], dob[sl], egl[sl], ds_c, kbgb[sl], drb_c)
                ds, du, dr, drb = ds + ds_c, du + du_c, dr + dr_c, drb + drb_c
            dsb = _each(bf, ds)
            dpg = _each(lambda d, uu, e: jnp.where(lo_c, _dot(d, uu, NT), 0.0) * e, dob, ub, el)
            dpgt = _each(lambda uu, d, e: jnp.where(up_c, _dot(uu, d, NT), 0.0) * e, ub, dob, eu)
            dag = _each(lambda d, uu, e: -jnp.where(lo_s, _dot(d, uu, NT), 0.0) * e, drb, ub, el)
            dagt = _each(lambda uu, d, e: -jnp.where(up_s, _dot(uu, d, NT), 0.0) * e, ub, drb, eu)
            dqg = _each(lambda d, s: _dot(d, s, NT), dob, sb)
            dkbg = _each(lambda d, s: -_dot(d, s, NT), drb, sb)
            dkd = _each(lambda uu, s: _dot(uu, s, NT), ub, dsb)
            dkb =_each(lambda a, x, y, e: _dot(bf(a), x, NN) + y * e, dag, kbf, dkbg, eg)
            dk = _each(lambda a, x, p, y, z, e, w, b: _dot(bf(a), x, NN) + _dot(bf(p), y, NN) + z * e + w * b,
                       dagt, kbb, dpgt, qb, dkd, ekd, dkb, beta)
            dq = _each(lambda p, x, y, e: _dot(bf(p), x, NN) + y * e, dpg, kbf, dqg, eg)
            dkd_kd = _each(lambda x, y: rs(x * y), dkd, kd)
            dg = _each(lambda a, x, p, y, at, xt, pt_, yt, z, w, c, d, e:
                       rs(a * x + p * y) - rs(at * xt + pt_ * yt) + rs(z * w) + rs(c * d) - e,
                       dag, kk, dpg, qk, dagt, kkt, dpgt, qkt, dqg, qg, dkbg, kbg, dkd_kd)
            dgl = _each(lambda x, e, s, y: jnp.sum(x, axis=0, keepdims=True) + e[:, 0:1] * total(s * y), dkd_kd, egl, ds, st)
            dg = _each(lambda x, y: x + jnp.where(last_row, y, 0.0), dg, dgl)
            dbeta = _each(lambda x, y, z, w: rs(x * y) + rs(z * w), dkb, k, dr, v)
            for j, (cc, hh) in enumerate(idx):
                dq_ref[rws[cc], cols[hh]] = dq[j]
                dk_ref[rws[cc], cols[hh]] = dk[j]
                dv_ref[rws[cc], cols[hh]] = dr[j] * beta[j]
                dg_ref[hh, pl.ds(ns[cc], 1), :] = as_row(dg[j])
                db_ref[hh, pl.ds(ns[cc], 1), :] = as_row(dbeta[j])
            return tuple(ds_new)

        lax.fori_loop(0, NCH // 2, step, tuple(jnp.zeros((DH, DH), F32) for _ in hs))

    q, k, v, gbs, grs = _head_specs()
    return _call(body, "delta_bwd",
                 [_sds((S, DNW)), _sds((S, DNW)), _sds((S, DNW)), _sds((H, NCH, CH)), _sds((H, NCH, CH))], grid=(H // HB,),
                 in_specs=[q, k, v, gbs, grs, gbs, ST_SPEC, TM_SPEC, HCOL], out_specs=[HCOL, HCOL, HCOL, grs, grs],
                 sem=("parallel",))(qkvn, qkvn, qkvn, gb, gr, bb, st_all, tm_all, do_all)


Z_BLK = O_Z // DNW


def _dn_post_fwd(o, proj, gn):
    def body(o_ref, z_ref, gn_ref, og_ref):
        for h in range(H):
            cols = slice(h * DH, (h + 1) * DH)
            ov = o_ref[:, cols]
            on = ov * lax.rsqrt(jnp.mean(ov * ov, axis=-1, keepdims=True) + EPS) * gn_ref[...]
            og_ref[:, cols] = (on * _silu(z_ref[:, cols])).astype(BF16)

    return _call(body, "dn_post_fwd", _sds((S, DNW), BF16), grid=(S // TS,),
                 in_specs=[_row_spec(DNW), pl.BlockSpec((TS, DNW), lambda i: (i, Z_BLK)), _vec_spec(DH)],
                 out_specs=_row_spec(DNW), sem=("parallel",))(o, proj, gn)


def _dn_post_bwd(dog, o, proj, gn, dproj):
    def body(d_ref, o_ref, z_ref, gn_ref, dproj_in, do_ref, dz_ref, acc_ref):
        dgn = jnp.zeros((1, DH), F32)
        for h in range(H):
            cols = slice(h * DH, (h + 1) * DH)
            ov, zv, dv = o_ref[:, cols], z_ref[:, cols], d_ref[:, cols]
            rinv = lax.rsqrt(jnp.mean(ov * ov, axis=-1, keepdims=True) + EPS)
            xn = ov * rinv
            act, dact = _silu_both(zv)
            don = dv * act
            dz_ref[:, cols] = (dv * xn * gn_ref[...] * dact).astype(BF16)
            dgn = dgn + jnp.sum(don * xn, axis=0, keepdims=True)
            dxn = don * gn_ref[...]
            do_ref[:, cols] = rinv * (dxn - xn * jnp.mean(dxn * xn, axis=-1, keepdims=True))
        _acc_rows(acc_ref, [dgn])

    zspec = pl.BlockSpec((TS, DNW), lambda i: (i, Z_BLK))
    return _call(body, "dn_post_bwd", [_sds((S, DNW)), _sds((S, NINP), BF16), _sds((8, DH))], grid=(S // TS,),
                 in_specs=[_row_spec(DNW), _row_spec(DNW), zspec, _vec_spec(DH), ANY],
                 out_specs=[_row_spec(DNW), zspec, _acc_spec(DH)], sem=("arbitrary",),
                 aliases={4: 1})(dog, o, proj, gn, dproj)


GA_BLK = O_GA // D
GB_BLK = O_GB // D


def _merge_fwd(ba, bb, proj):
    def body(a_ref, b_ref, ga_ref, gb_ref, o_ref):
        o_ref[...] = (_sigmoid(ga_ref[...]) * a_ref[...] + _sigmoid(gb_ref[...]) * b_ref[...]).astype(BF16)

    return _call(body, "merge_fwd", _sds((S, D), BF16), grid=(S // TS,),
                 in_specs=[_row_spec(), _row_spec(), pl.BlockSpec((TS, D), lambda i: (i, GA_BLK)),
                           pl.BlockSpec((TS, D), lambda i: (i, GB_BLK))],
                 out_specs=_row_spec(), sem=("parallel",))(ba, bb, proj, proj)


def _merge_bwd(dm, ba, bb, proj, dproj):
    def body(d_ref, a_ref, b_ref, ga_ref, gb_ref, dproj_in, dg_ref, da_ref, db_ref):
        d = d_ref[...]
        sa, sb = _sigmoid(ga_ref[...]), _sigmoid(gb_ref[...])
        dg_ref[:, 0:D] = (d * a_ref[...] * sa * (1.0 - sa)).astype(BF16)
        dg_ref[:, D:2 * D] = (d * b_ref[...] * sb * (1.0 - sb)).astype(BF16)
        da_ref[...] = (d * sa).astype(BF16)
        db_ref[...] = (d * sb).astype(BF16)

    return _call(body, "merge_bwd", [_sds((S, NINP), BF16), _sds((S, D), BF16), _sds((S, D), BF16)], grid=(S // TS,),
                 in_specs=[_row_spec(), _row_spec(), _row_spec(), pl.BlockSpec((TS, D), lambda i: (i, GA_BLK)),
                           pl.BlockSpec((TS, D), lambda i: (i, GB_BLK)), ANY],
                 out_specs=[pl.BlockSpec((TS, 2 * D), lambda i: (i, O_GA // (2 * D))), _row_spec(), _row_spec()],
                 sem=("parallel",), aliases={5: 0})(dm, ba, bb, proj, proj, dproj)


NSH = NIN // NDEV


def _pad_win(wt):
    rows = [wt[0:4096], wt[6160:6160 + 2 * D]]
    for j in range(CFW // LANE):
        rows += [wt[4112 + LANE * j:4112 + LANE * (j + 1)], wt[4112 + CFW + LANE * j:4112 + CFW + LANE * (j + 1)]]
    rows += [wt[4096:4112], jnp.zeros((NINP - NIN, wt.shape[1]), wt.dtype)]
    return jnp.concatenate(rows, axis=0)


def _unpad_win(gpt):
    rows = [gpt[0:4096], gpt[O_SM:O_SM + 16]]
    for half in range(2):
        rows += [gpt[O_GLU + (2 * j + half) * LANE:O_GLU + (2 * j + half + 1) * LANE] for j in range(CFW // LANE)]
    rows.append(gpt[O_GA:O_GA + 2 * D])
    return jnp.concatenate(rows, axis=0)


def _lane_vec(v8, offset):
    return jnp.pad(v8, ((0, 0), (offset, LANE - 8 - offset)))


def _tie(vec, token):
    return vec + token


def _local_step(x, tgt, mod, norm1_g, norm2_g, final_g, w_in_p, dn_conv_w, a_log, dt_bias, dn_norm_g,
                cf_conv_w, cf_ln_g, cf_ln_b, ffn_conv_w, comm):
    sh1, sc1, gt1, sh2, sc2, gt2 = (mod[:, i * D:(i + 1) * D] for i in range(6))
    alog_v, dtb_v = _lane_vec(a_log, H), _lane_vec(dt_bias, H)

    hn1 = _norm_mod(x, norm1_g, sc1, _tie(sh1, comm.token0), "norm_mod1")
    proj = _mm(hn1, w_in_p, "nt", F32, "mm_in", tn=1152)
    qkvn = _dn_pre_fwd(proj, dn_conv_w)
    gates = _gates_fwd(proj, alog_v, dtb_v)
    beta_t = gates[:, 0:H].T
    g_t = gates[:, H:2 * H].T
    gb = jnp.broadcast_to(g_t[:, :, None], (H, S, DH))
    bb = jnp.broadcast_to(beta_t[:, :, None], (H, S, DH))
    gr = g_t.reshape(H, NCH, CH)
    o, st_all, tm_all = _delta_fwd(qkvn, gb, gr, bb)
    og = _dn_post_fwd(o, proj, dn_norm_g)
    u1 = _cf_conv_fwd(proj, cf_conv_w)
    u3 = _cf_ln_fwd(u1, cf_ln_g, cf_ln_b)
    after = og[0:8, 0:LANE].astype(F32) + u3[0:8, 0:LANE].astype(F32)
    dn_w_o, cf_w_o, w_out = comm.late_weights("mix", after)
    br_a = _mm(og, dn_w_o, "nn", F32, "mm_dn_o")
    br_b = _mm(u3, cf_w_o, "nn", F32, "mm_cf_o")
    merged = _merge_fwd(br_a, br_b, proj)
    mix = _mm(merged, w_out, "nn", F32, "mm_out")
    x2, hn2 = _resid_norm_mod(x, mix, gt1, norm2_g, sc2, sh2, "resid_norm_mod2")
    ffn_w_up, ffn_w_down = comm.late_weights("ffn", hn2[0:8, 0:LANE].astype(F32))
    upall = _mm(hn2, ffn_w_up, "nn", F32, "mm_up")
    hmid = _ffn_mid_fwd(upall, ffn_conv_w)
    f = _mm(hmid, ffn_w_down, "nn", F32, "mm_down", tm=2048)

    dx3, df, acc_f = _loss_head(x2, f, tgt, gt2, final_g)
    d_final_g, d_gt2, loss = acc_f[0:1], acc_f[1:2], acc_f[2:3, 0:1]
    dhmid = _mm(df, ffn_w_down, "nt", F32, "mm_down_dx")
    g_w_down = _mm(hmid, df, "tn", BF16, "mm_down_dw", tm=FFN // 4)
    d_gate, d_up, g_ffn_conv = _ffn_mid_bwd(dhmid, upall, ffn_conv_w)
    g_w_up = _mm(hn2, d_gate, "tn", BF16, "mm_up_dw", tn=2 * FFN // NDEV, b2=d_up)
    tok_a = comm.grads_begin("a", dict(ffn_w_down=g_w_down, ffn_w_up=g_w_up))
    dhn2 = _mm(d_gate, ffn_w_up, "nt", F32, "mm_up_dx", a2=d_up, dep=jnp.broadcast_to(tok_a, (8, LANE)))
    tok_a = comm.grads_continue("a", dhn2)
    dx2, dmix, acc2 = _norm_mod_bwd(dhn2, x2, dx3, _tie(norm2_g, tok_a), sc2, "norm_mod2_bwd", mix=mix, gt=gt1)
    d_sh2, d_sc2, d_norm2_g, d_gt1 = acc2[0:1], acc2[1:2], acc2[2:3], acc2[3:4]
    dmerged = _mm(dmix, w_out, "nt", F32, "mm_out_dx")
    g_w_out = _mm(merged, dmix, "tn", BF16, "mm_out_dw")
    d_proj, d_bra, d_brb = _merge_bwd(dmerged, br_a, br_b, proj, lax.empty((S, NINP), BF16))
    du3 = _mm(d_brb, cf_w_o, "nt", F32, "mm_cf_o_dx")
    g_cf_w_o = _mm(u3, d_brb, "tn", BF16, "mm_cf_o_dw")
    du1, acc_ln = _cf_ln_bwd(du3, u1, cf_ln_g, cf_ln_b)
    d_proj, g_cf_conv = _cf_conv_bwd(du1, proj, cf_conv_w, d_proj)
    dog = _mm(d_bra, dn_w_o, "nt", F32, "mm_dn_o_dx")
    g_dn_w_o = _mm(og, d_bra, "tn", BF16, "mm_dn_o_dw")
    tok_b = comm.grads_begin("b", dict(w_out=g_w_out, cf_w_o=g_cf_w_o, dn_w_o=g_dn_w_o, ffn_conv_w=g_ffn_conv,
                                       cf_conv_w=g_cf_conv))
    do, d_proj, acc_gn = _dn_post_bwd(dog, o, proj, _tie(dn_norm_g, tok_b), d_proj)
    tok_b = comm.grads_continue("b", do)
    dq, dk, dv, dgr, dbr = _delta_bwd(qkvn, gb, _tie(gr, tok_b), bb, st_all, tm_all, do)
    d_proj, g_dn_conv = _dn_pre_bwd(dq, dk, dv, proj, dn_conv_w, d_proj)
    dgates = jnp.concatenate([dbr.reshape(H, S).T, dgr.reshape(H, S).T, jnp.zeros((S, LANE - 2 * H), F32)], axis=1)
    d_proj, acc_g = _gates_bwd(dgates, proj, alog_v, dtb_v, d_proj)
    g_w_in_p = _mm(d_proj, hn1, "tn", BF16, "mm_in_dw", tm=1152)
    comm.grads_begin("c", dict(w_in=g_w_in_p, dn_conv_w=g_dn_conv))
    tok_c = comm.grads_continue("c", g_dn_conv)
    dhn1 = _mm(d_proj, w_in_p, "nn", F32, "mm_in_dx", tk=NINP // 3, dep=jnp.broadcast_to(tok_c, (8, LANE)))
    grad_x, acc1 = _norm_mod_bwd(dhn1, x, dx2, norm1_g, sc1, "norm_mod1_bwd")
    d_sh1, d_sc1, d_norm1_g = acc1[0:1], acc1[1:2], acc1[2:3]

    d_mod = jnp.concatenate([d_sh1, d_sc1, d_gt1, d_sh2, d_sc2, d_gt2], axis=1)
    small = dict(mod=d_mod, norm1_g=d_norm1_g, norm2_g=d_norm2_g, final_norm_g=d_final_g,
                 cf_ln_g=acc_ln[0:1], cf_ln_b=acc_ln[1:2], dn_norm_g=acc_gn[0:1],
                 dn_a_log=acc_g[0:1, H:2 * H], dn_dt_bias=acc_g[1:2, H:2 * H])
    return loss, grad_x, small


def _dev_index(px, py, pc):
    return 4 * px + 2 * py + pc


def _all_gather(arrs, name):
    n = len(arrs)

    def body(*refs):
        ins, outs = refs[:n], refs[n:2 * n]
        send_sems, recv_sems, loc_sems = refs[2 * n:]
        x, y, c = _my_pos()
        me, sib = (x, y, c), (x, y, 1 - c)
        chips = [(1 - x, y), (x, 1 - y), (1 - x, 1 - y)]

        def cp(i, k, block, to, src=None):
            dst = outs[i].at[_dev_index(*block)]
            return pltpu.make_async_remote_copy(
                src_ref=dst if src is None else src, dst_ref=dst, send_sem=send_sems.at[i, k],
                recv_sem=recv_sems.at[i, k], device_id=to, device_id_type=MESH)

        mine = [pltpu.make_async_copy(ins[i], outs[i].at[_dev_index(*me)], loc_sems.at[i]) for i in range(n)]
        for m in mine:
            m.start()
        sent = []
        for i in range(n):
            sent.append(cp(i, 0, me, sib, src=ins[i]))
            sent += [cp(i, 1 + j, me, (*chip, c), src=ins[i]) for j, chip in enumerate(chips)]
        for s in sent:
            s.start()
        for i in range(n):
            for j, chip in enumerate(chips):
                cp(i, 1 + j, (*chip, c), me).wait_recv()
                fwd = cp(i, 4 + j, (*chip, c), sib)
                fwd.start()
                sent.append(fwd)
        for i in range(n):
            cp(i, 0, sib, me).wait_recv()
            for j, chip in enumerate(chips):
                cp(i, 4 + j, (*chip, 1 - c), me).wait_recv()
        for s in sent:
            s.wait_send()
        for m in mine:
            m.wait()

    outs = pl.pallas_call(
        body, out_shape=[_sds((NDEV,) + a.shape, a.dtype) for a in arrs], in_specs=[ANY] * n, out_specs=[ANY] * n,
        scratch_shapes=[pltpu.SemaphoreType.DMA((n, 7)), pltpu.SemaphoreType.DMA((n, 7)), pltpu.SemaphoreType.DMA((n,))],
        name=name)(*arrs)
    return list(outs)


def _slab(ref, layout, idx):
    kind, n = layout
    if kind == "rows":
        return ref.at[pl.ds(pl.multiple_of(idx * n, n), n), :]
    if kind == "cols":
        return ref.at[:, pl.ds(pl.multiple_of(idx * n, n), n)]
    return ref.at[idx]


def _slab_shape(arr, layout):
    kind, n = layout
    if kind == "rows":
        return (n, arr.shape[1])
    if kind == "cols":
        return (arr.shape[0], n)
    return tuple(arr.shape[1:])


HBM = pl.BlockSpec(memory_space=pltpu.HBM)
SEMS = pl.BlockSpec(memory_space=pltpu.SEMAPHORE)
EFFECT = pltpu.SideEffectType.DATAFLOW_SIDE_EFFECTING
TOKEN = jax.ShapeDtypeStruct((8, LANE), F32)


def _hbm(a):
    return pltpu.with_memory_space_constraint(a, pltpu.HBM)


def _gather_ici_copy(shard_ref, buf_ref, layout, send_sems, recv_sems, i, j, me, chip, c):
    return pltpu.make_async_remote_copy(
        src_ref=shard_ref, dst_ref=_slab(buf_ref, layout, me), send_sem=send_sems.at[3 * i + j],
        recv_sem=recv_sems.at[3 * i + j], device_id=(*chip, c), device_id_type=MESH)


def _gather_ici_start(shards, bufs, layouts, after, name):
    n = len(shards)

    def body(*refs):
        sh, bf = refs[:n], refs[n:2 * n]
        send_sems, recv_sems = refs[2 * n + 1], refs[2 * n + 2]
        token = refs[-1]
        x, y, c = _my_pos()
        me = _dev_index(x, y, c)
        for i in range(n):
            for j, chip in enumerate([(1 - x, y), (x, 1 - y), (1 - x, 1 - y)]):
                _gather_ici_copy(sh[i], bf[i], layouts[i], send_sems, recv_sems, i, j, me, chip, c).start()
        token[...] = jnp.zeros_like(token)

    outs = pl.pallas_call(
        body, name=name,
        out_shape=(pltpu.SemaphoreType.DMA((3 * n,)), pltpu.SemaphoreType.DMA((3 * n,)),
                   *[pltpu.HBM(a.shape, a.dtype) for a in shards], *[pltpu.HBM(a.shape, a.dtype) for a in bufs], TOKEN),
        in_specs=[HBM] * (2 * n) + [ANY],
        out_specs=(SEMS, SEMS, *[HBM] * (2 * n), pl.BlockSpec(memory_space=pltpu.VMEM)),
        input_output_aliases={i: 2 + i for i in range(2 * n)},
        compiler_params=pltpu.CompilerParams(has_side_effects=EFFECT),
    )(*[_hbm(a) for a in shards], *[_hbm(a) for a in bufs], after)
    return outs[0], outs[1], list(outs[2:2 + n]), list(outs[2 + n:2 + 2 * n]), outs[-1]


def _gather_ici_wait(send_sems, recv_sems, shards, bufs, layouts, after, name):
    n = len(shards)

    def body(*refs):
        sh, bf = refs[:n], refs[n:2 * n]
        ssem, rsem = refs[2 * n], refs[2 * n + 1]
        x, y, c = _my_pos()
        me = _dev_index(x, y, c)
        for i in range(n):
            for j, chip in enumerate([(1 - x, y), (x, 1 - y), (1 - x, 1 - y)]):
                cp = _gather_ici_copy(sh[i], bf[i], layouts[i], ssem, rsem, i, j, me, chip, c)
                cp.wait_send()
                cp.wait_recv()

    outs = pl.pallas_call(
        body, name=name,
        out_shape=(*[pltpu.HBM(a.shape, a.dtype) for a in shards], *[pltpu.HBM(a.shape, a.dtype) for a in bufs]),
        in_specs=[HBM] * (2 * n) + [SEMS, SEMS, ANY], out_specs=tuple([HBM] * (2 * n)),
        input_output_aliases={i: i for i in range(2 * n)},
        compiler_params=pltpu.CompilerParams(has_side_effects=EFFECT),
    )(*shards, *bufs, send_sems, recv_sems, after)
    return list(outs[:n]), list(outs[n:])


def _place_own(pos, shard, buf, layout, name):
    kind, n = layout
    r, cols = shard.shape
    tr = _row_tile(r, shard.dtype.itemsize)
    nr = r // tr
    if kind == "rows":
        ospec = pl.BlockSpec((tr, cols), lambda i, p: (p[2] * nr + i, 0))
    else:
        assert kind == "lead"
        ospec = pl.BlockSpec((None, tr, cols), lambda i, p: (p[2], i, 0))

    def body(pos_ref, s_ref, buf_in, o_ref):
        o_ref[...] = s_ref[...]

    return pl.pallas_call(
        body, out_shape=_sds(buf.shape, buf.dtype), name=name, input_output_aliases={2: 0},
        grid_spec=pltpu.PrefetchScalarGridSpec(
            num_scalar_prefetch=1, grid=(nr,), in_specs=[pl.BlockSpec((tr, cols), lambda i, p: (i, 0)), ANY],
            out_specs=ospec),
        compiler_params=pltpu.CompilerParams(dimension_semantics=("parallel",), vmem_limit_bytes=VMEM_LIMIT),
    )(pos, shard, buf)


def _gather_pair(shards, bufs, layouts, name):
    n = len(shards)

    def body(*refs):
        sh, bo = refs[:n], refs[2 * n:3 * n]
        send_sems, recv_sems = refs[3 * n:]
        x, y, c = _my_pos()
        sib = (x, y, 1 - c)
        copies = []
        for i in range(n):
            for k, (px, py) in enumerate([(x, y), (1 - x, y), (x, 1 - y), (1 - x, 1 - y)]):
                slab = _slab(bo[i], layouts[i], _dev_index(px, py, c))
                copies.append(pltpu.make_async_remote_copy(
                    src_ref=sh[i] if k == 0 else slab, dst_ref=slab, send_sem=send_sems.at[i, k],
                    recv_sem=recv_sems.at[i, k], device_id=sib, device_id_type=MESH))
        for cpy in copies:
            cpy.start()
        for cpy in copies:
            cpy.wait()

    outs = pl.pallas_call(
        body, out_shape=[_sds(a.shape, a.dtype) for a in bufs], in_specs=[ANY] * (2 * n), out_specs=[ANY] * n,
        input_output_aliases={n + i: i for i in range(n)},
        scratch_shapes=[pltpu.SemaphoreType.DMA((n, 4)), pltpu.SemaphoreType.DMA((n, 4))], name=name)(*shards, *bufs)
    return list(outs)


def _pair_copy(part_ref, land_ref, layout, send_sems, recv_sems, i, q, x, y, c):
    return pltpu.make_async_remote_copy(
        src_ref=_slab(part_ref, layout, 2 * q + (1 - c)), dst_ref=land_ref.at[q], send_sem=send_sems.at[4 * i + q],
        recv_sem=recv_sems.at[4 * i + q], device_id=(x, y, 1 - c), device_id_type=MESH)


def _pair_exchange_start(parts, layouts, name):
    n = len(parts)
    lands = [lax.empty((4,) + _slab_shape(p, lay), p.dtype) for p, lay in zip(parts, layouts)]

    def body(*refs):
        pt, ld = refs[:n], refs[n:2 * n]
        send_sems, recv_sems = refs[2 * n], refs[2 * n + 1]
        token = refs[-1]
        x, y, c = _my_pos()
        for i in range(n):
            for q in range(4):
                _pair_copy(pt[i], ld[i], layouts[i], send_sems, recv_sems, i, q, x, y, c).start()
        token[...] = jnp.zeros_like(token)

    outs = pl.pallas_call(
        body, name=name,
        out_shape=(pltpu.SemaphoreType.DMA((4 * n,)), pltpu.SemaphoreType.DMA((4 * n,)),
                   *[pltpu.HBM(a.shape, a.dtype) for a in parts], *[pltpu.HBM(a.shape, a.dtype) for a in lands], TOKEN),
        in_specs=[HBM] * (2 * n), out_specs=(SEMS, SEMS, *[HBM] * (2 * n), pl.BlockSpec(memory_space=pltpu.VMEM)),
        input_output_aliases={i: 2 + i for i in range(2 * n)},
        compiler_params=pltpu.CompilerParams(has_side_effects=EFFECT),
    )(*[_hbm(a) for a in parts], *[_hbm(a) for a in lands])
    return outs[0], outs[1], list(outs[2:2 + n]), list(outs[2 + n:2 + 2 * n]), outs[-1]


def _pair_exchange_wait(send_sems, recv_sems, parts, lands, layouts, after, name):
    n = len(parts)

    def body(*refs):
        pt, ld = refs[:n], refs[n:2 * n]
        ssem, rsem = refs[2 * n], refs[2 * n + 1]
        x, y, c = _my_pos()
        for i in range(n):
            for q in range(4):
                cp = _pair_copy(pt[i], ld[i], layouts[i], ssem, rsem, i, q, x, y, c)
                cp.wait_send()
                cp.wait_recv()

    outs = pl.pallas_call(
        body, name=name,
        out_shape=(*[pltpu.HBM(a.shape, a.dtype) for a in parts], *[pltpu.HBM(a.shape, a.dtype) for a in lands]),
        in_specs=[HBM] * (2 * n) + [SEMS, SEMS, ANY], out_specs=tuple([HBM] * (2 * n)),
        input_output_aliases={i: i for i in range(2 * n)},
        compiler_params=pltpu.CompilerParams(has_side_effects=EFFECT),
    )(*parts, *lands, send_sems, recv_sems, after)
    return list(outs[:n]), list(outs[n:])


def _chip_copy(sum_ref, land_ref, send_sems, recv_sems, i, j, chip, c):
    return pltpu.make_async_remote_copy(
        src_ref=sum_ref.at[2 * chip[0] + chip[1]], dst_ref=land_ref.at[j], send_sem=send_sems.at[3 * i + j],
        recv_sem=recv_sems.at[3 * i + j], device_id=(*chip, c), device_id_type=MESH)


def _chip_exchange_start(sums, name):
    n = len(sums)
    lands = [lax.empty((3,) + s.shape[1:], s.dtype) for s in sums]

    def body(*refs):
        sm, ld = refs[:n], refs[n:2 * n]
        send_sems, recv_sems = refs[2 * n], refs[2 * n + 1]
        token = refs[-1]
        x, y, c = _my_pos()
        for i in range(n):
            for j, chip in enumerate([(1 - x, y), (x, 1 - y), (1 - x, 1 - y)]):
                _chip_copy(sm[i], ld[i], send_sems, recv_sems, i, j, chip, c).start()
        token[...] = jnp.zeros_like(token)

    outs = pl.pallas_call(
        body, name=name,
        out_shape=(pltpu.SemaphoreType.DMA((3 * n,)), pltpu.SemaphoreType.DMA((3 * n,)),
                   *[pltpu.HBM(a.shape, a.dtype) for a in sums], *[pltpu.HBM(a.shape, a.dtype) for a in lands], TOKEN),
        in_specs=[HBM] * (2 * n), out_specs=(SEMS, SEMS, *[HBM] * (2 * n), pl.BlockSpec(memory_space=pltpu.VMEM)),
        input_output_aliases={i: 2 + i for i in range(2 * n)},
        compiler_params=pltpu.CompilerParams(has_side_effects=EFFECT),
    )(*[_hbm(a) for a in sums], *[_hbm(a) for a in lands])
    return outs[0], outs[1], list(outs[2:2 + n]), list(outs[2 + n:2 + 2 * n]), outs[-1]


def _chip_exchange_wait(send_sems, recv_sems, sums, lands, after, name):
    n = len(sums)

    def body(*refs):
        sm, ld = refs[:n], refs[n:2 * n]
        ssem, rsem = refs[2 * n], refs[2 * n + 1]
        x, y, c = _my_pos()
        for i in range(n):
            for j, chip in enumerate([(1 - x, y), (x, 1 - y), (1 - x, 1 - y)]):
                cp = _chip_copy(sm[i], ld[i], ssem, rsem, i, j, chip, c)
                cp.wait_send()
                cp.wait_recv()

    outs = pl.pallas_call(
        body, name=name,
        out_shape=(*[pltpu.HBM(a.shape, a.dtype) for a in sums], *[pltpu.HBM(a.shape, a.dtype) for a in lands]),
        in_specs=[HBM] * (2 * n) + [SEMS, SEMS, ANY], out_specs=tuple([HBM] * (2 * n)),
        input_output_aliases={i: i for i in range(2 * n)},
        compiler_params=pltpu.CompilerParams(has_side_effects=EFFECT),
    )(*sums, *lands, send_sems, recv_sems, after)
    return list(outs[:n]), list(outs[n:])


def _row_tile(r, itemsize):
    align = 32 // itemsize
    best = r
    for t in range(align, min(r, 256) + 1, align):
        if r % t == 0:
            best = t
    return best


def _prefetch_call(body, name, out_shape, grid, in_specs, out_specs, sem):
    return pl.pallas_call(
        body, out_shape=out_shape, name=name,
        grid_spec=pltpu.PrefetchScalarGridSpec(num_scalar_prefetch=1, grid=grid, in_specs=in_specs, out_specs=out_specs),
        compiler_params=pltpu.CompilerParams(dimension_semantics=sem, vmem_limit_bytes=VMEM_LIMIT))


def _pair_sum(pos, part, got, layout, name):
    kind, _ = layout
    _, r, cols = got.shape
    tr, tc = _tiles(r, cols, part.dtype.itemsize)
    nr, nc = r // tr, cols // tc
    if kind == "rows":
        pspec = pl.BlockSpec((tr, tc), lambda q, i, j, p: ((2 * q + p[0]) * nr + i, j))
    elif kind == "cols":
        pspec = pl.BlockSpec((tr, tc), lambda q, i, j, p: (i, (2 * q + p[0]) * nc + j))
    else:
        pspec = pl.BlockSpec((None, tr, tc), lambda q, i, j, p: (2 * q + p[0], i, j))

    def body(pos_ref, p_ref, g_ref, o_ref):
        o_ref[...] = (p_ref[...].astype(F32) + g_ref[...].astype(F32)).astype(o_ref.dtype)

    blk = pl.BlockSpec((None, tr, tc), lambda q, i, j, p: (q, i, j))
    return _prefetch_call(body, name, _sds((4, r, cols), part.dtype), (4, nr, nc), [pspec, blk], blk,
                          ("parallel", "parallel", "parallel"))(pos, part, got)


def _tiles(r, cols, itemsize):
    tr = _row_tile(r, itemsize)
    if tr < r or r * cols * 4 <= (2 << 20) or cols % 256:
        return tr, cols
    return r, 256


def _final_sum_adam(pos, sums, got, w, m, v, name):
    _, r, cols = w.shape
    tr, tc = _tiles(r, cols, sums.dtype.itemsize)

    def body(pos_ref, s_ref, g_ref, w_ref, m_ref, v_ref, go_ref, dl_ref, nm_ref, nv_ref):
        g = ((s_ref[...].astype(F32) + g_ref[0].astype(F32)) + g_ref[1].astype(F32)) + g_ref[2].astype(F32)
        dl, nm, nv = _adam(w_ref[...], g, m_ref[...], v_ref[...])
        go_ref[...] = g
        dl_ref[...] = dl
        nm_ref[...] = nm
        nv_ref[...] = nv

    big = pl.BlockSpec((None, tr, tc), lambda i, j, p: (0, i, j))
    return _prefetch_call(body, name, [_sds((1, r, cols))] * 4, (r // tr, cols // tc),
                          [pl.BlockSpec((None, tr, tc), lambda i, j, p: (p[1], i, j)),
                           pl.BlockSpec((3, tr, tc), lambda i, j, p: (0, i, j)), big, big, big],
                          [big] * 4, ("parallel", "parallel"))(pos, sums, got, w, m, v)


def _small_adam(g_all, w, m, v):
    npk = w.shape[1]

    def body(g_ref, w_ref, m_ref, v_ref, go_ref, dl_ref, nm_ref, nv_ref):
        g = g_ref[0:1, :]
        for k in range(1, NDEV):
            g = g + g_ref[k:k + 1, :]
        dl, nm, nv = _adam(w_ref[...], g, m_ref[...], v_ref[...])
        go_ref[...] = g
        dl_ref[...] = dl
        nm_ref[...] = nm
        nv_ref[...] = nv

    return _call(body, "small_adam", [_sds((1, npk))] * 4)(g_all, w, m, v)


SMALL = [("b_ada", 6 * D), ("norm1_g", D), ("norm2_g", D), ("final_norm_g", D), ("cf_ln_g", CFW), ("cf_ln_b", CFW),
         ("dn_norm_g", DH), ("dn_a_log", H), ("dn_dt_bias", H)]
LATE = ["dn_w_o", "cf_w_o", "w_out", "ffn_w_up", "ffn_w_down"]
LATE_SHAPE = {"dn_w_o": (NDEV, DNW, D // NDEV), "cf_w_o": (NDEV, CFW, D // NDEV), "w_out": (D, D),
              "ffn_w_up": (NDEV, D, 2 * FFN // NDEV), "ffn_w_down": (FFN, D)}
LATE_LAYOUT = {"dn_w_o": ("lead", NDEV), "cf_w_o": ("lead", NDEV), "w_out": ("rows", D // NDEV),
               "ffn_w_up": ("lead", NDEV), "ffn_w_down": ("rows", FFN // NDEV)}
LAYOUT = {"dn_w_o": ("cols", D // NDEV), "cf_w_o": ("cols", D // NDEV), "w_out": ("rows", D // NDEV),
          "ffn_w_up": ("cols", 2 * FFN // NDEV), "ffn_w_down": ("rows", FFN // NDEV),
          "w_in": ("lead", NDEV), "dn_conv_w": ("lead", NDEV), "cf_conv_w": ("lead", NDEV), "ffn_conv_w": ("lead", NDEV)}
NAMES = ["w_ada", "b_ada", "norm1_g", "w_in", "dn_conv_w", "dn_a_log", "dn_dt_bias", "dn_norm_g", "dn_w_o", "cf_conv_w",
         "cf_ln_g", "cf_ln_b", "cf_w_o", "w_out", "norm2_g", "ffn_w_up", "ffn_conv_w", "ffn_w_down", "final_norm_g"]


def _pack_small(d):
    rows = []
    for nm, n in SMALL:
        row = d[nm].reshape(1, n)
        pad = (-n) % LANE
        rows.append(jnp.pad(row, ((0, 0), (0, pad))) if pad else row)
    return jnp.concatenate(rows, axis=1)


def _unpack_small(row, shapes):
    out, off = {}, 0
    for nm, n in SMALL:
        out[nm] = row[0, off:off + n].reshape(shapes[nm])
        off += n + ((-n) % LANE)
    return out


def _cols_from_gathered(g):
    return jnp.transpose(g, (1, 0, 2)).reshape(g.shape[1], NDEV * g.shape[2])


def _cols_to_parts(full):
    r, ctot = full.shape
    return jnp.transpose(full.reshape(r, NDEV, ctot // NDEV), (1, 0, 2))


def kernel(x, c, w_ada, b_ada, norm1_g, w_in, dn_conv_w, dn_a_log, dn_dt_bias, dn_norm_g, dn_w_o, cf_conv_w, cf_ln_g, cf_ln_b, cf_w_o, w_out, norm2_g, ffn_w_up, ffn_conv_w, ffn_w_down, final_norm_g, loss_target, m_w_ada, m_b_ada, m_norm1_g, m_w_in, m_dn_conv_w, m_dn_a_log, m_dn_dt_bias, m_dn_norm_g, m_dn_w_o, m_cf_conv_w, m_cf_ln_g, m_cf_ln_b, m_cf_w_o, m_w_out, m_norm2_g, m_ffn_w_up, m_ffn_conv_w, m_ffn_w_down, m_final_norm_g, v_w_ada, v_b_ada, v_norm1_g, v_w_in, v_dn_conv_w, v_dn_a_log, v_dn_dt_bias, v_dn_norm_g, v_dn_w_o, v_cf_conv_w, v_cf_ln_g, v_cf_ln_b, v_cf_w_o, v_w_out, v_norm2_g, v_ffn_w_up, v_ffn_conv_w, v_ffn_w_down, v_final_norm_g):
    args = locals()
    w = {nm: args[nm] for nm in NAMES}
    mo = {nm: args["m_" + nm] for nm in NAMES}
    vo = {nm: args["v_" + nm] for nm in NAMES}
    shapes = {nm: w[nm].shape for nm in NAMES}
    px, py, pc = _my_pos()
    me = _dev_index(px, py, pc)

    def mat(a):
        return a.reshape(a.shape[-2:])

    pos = jnp.stack([pc, 2 * px + py, me]).astype(jnp.int32)

    first = ["w_in", "dn_conv_w", "cf_conv_w", "ffn_conv_w"]
    tr_in = lambda a: jnp.transpose(a, (0, 2, 1))
    got = _all_gather([tr_in(w["w_in"]).astype(BF16)] + [mat(w[nm]) for nm in first[1:]] + [c], "gather_first")
    full = {nm: _cols_from_gathered(g) for nm, g in zip(first[1:], got[1:-1])}
    c_all = got[-1].reshape(NDEV, D)
    w_in_p = _pad_win(got[0].reshape(NIN, D))

    ncol = 6 * D // NDEV
    b_sh = lax.dynamic_slice(b_ada.reshape(1, 6 * D), (0, me * ncol), (1, ncol))
    mod_sh = _ada_fwd(c_all, mat(w_ada), b_sh)
    mod_all = _all_gather([mod_sh], "gather_mod")[0]
    mod = lax.dynamic_index_in_dim(mod_all, me, axis=1, keepdims=False).reshape(1, 6 * D)

    late = {}
    dep = mod_all
    for grp, names in (("mix", LATE[:3]), ("ffn", LATE[3:])):
        shards = [mat(w[nm]).astype(BF16) for nm in names]
        lays = [LATE_LAYOUT[nm] for nm in names]
        bufs = [_place_own(pos, s, lax.empty(LATE_SHAPE[nm], BF16), lay, "place_" + nm)
                for nm, s, lay in zip(names, shards, lays)]
        started = _gather_ici_start(shards, bufs, lays, dep, "gather_" + grp + "_start")
        late[grp] = (lays,) + tuple(started)
        dep = started[4]

    res = {}

    class Comm:
        token0 = late["ffn"][5][0, 0]
        pending = {}

        @staticmethod
        def late_weights(grp, after):
            lays, ssem, rsem, shards, bufs, _ = late[grp]
            shards, bufs = _gather_ici_wait(ssem, rsem, shards, bufs, lays, after, "gather_" + grp + "_wait")
            return _gather_pair(shards, bufs, lays, "gather_" + grp + "_pair")

        @staticmethod
        def grads_begin(group, gd):
            names = list(gd)
            lays = [LAYOUT[nm] for nm in names]
            gl = []
            for nm in names:
                if nm == "w_in":
                    gl.append(_unpad_win(gd[nm]).reshape(NDEV, NSH, D))
                else:
                    gl.append(_cols_to_parts(gd[nm]) if LAYOUT[nm][0] == "lead" else gd[nm])
            started = _pair_exchange_start(gl, lays, "rs_pair_start_" + group)
            Comm.pending[group] = (names, lays) + tuple(started[:4])
            return started[4][0, 0]

        @staticmethod
        def grads_continue(group, after):
            names, lays, ssem, rsem, gl, lands = Comm.pending[group]
            gl, from_sib = _pair_exchange_wait(ssem, rsem, gl, lands, lays, after, "rs_pair_wait_" + group)
            sums = [_pair_sum(pos, g, r, lay, "rs_pair_sum_" + nm) for nm, g, r, lay in zip(names, gl, from_sib, lays)]
            started = _chip_exchange_start(sums, "rs_chips_start_" + group)
            Comm.pending[group] = (names,) + tuple(started[:4])
            return started[4][0, 0]

        @staticmethod
        def finish(group, after):
            names, ssem, rsem, sums, lands = Comm.pending[group]
            sums, lands = _chip_exchange_wait(ssem, rsem, sums, lands, after, "rs_chips_wait_" + group)
            for nm, s, r in zip(names, sums, lands):
                if nm == "w_in":
                    outs = _final_sum_adam(pos, s, r, tr_in(w[nm]), tr_in(mo[nm]), tr_in(vo[nm]), "adam_" + nm)
                    res[nm] = [tr_in(o) for o in outs]
                else:
                    res[nm] = _final_sum_adam(pos, s, r, w[nm], mo[nm], vo[nm], "adam_" + nm)
            return res[names[-1]][0]

    vec = lambda a: a.reshape(1, -1)
    loss, grad_x, small = _local_step(
        x.reshape(S, D), loss_target.reshape(S, D), mod, vec(norm1_g), vec(norm2_g), vec(final_norm_g), w_in_p,
        full["dn_conv_w"], vec(dn_a_log), vec(dn_dt_bias), vec(dn_norm_g), full["cf_conv_w"], vec(cf_ln_g),
        vec(cf_ln_b), full["ffn_conv_w"], Comm)

    done_a = Comm.finish("a", grad_x)
    done_b = Comm.finish("b", done_a)

    small["b_ada"] = small.pop("mod")
    packed = _pack_small(small) + 0.0 * done_b.reshape(-1)[0]
    g_small = _all_gather([packed], "gather_small")[0].reshape(NDEV, -1)
    outs = _small_adam(g_small, _pack_small({nm: w[nm] for nm, _ in SMALL}), _pack_small({nm: mo[nm] for nm, _ in SMALL}),
                       _pack_small({nm: vo[nm] for nm, _ in SMALL}))
    unpacked = [_unpack_small(o, shapes) for o in outs]
    for nm, _ in SMALL:
        res[nm] = [u[nm] for u in unpacked]

    dmod_sel = lax.dynamic_slice(g_small[:, :6 * D], (0, me * ncol), (NDEV, ncol))
    outs = _ada_bwd_adam(c_all, dmod_sel, mat(w_ada), mat(m_w_ada), mat(v_w_ada))
    res["w_ada"] = [o.reshape(shapes["w_ada"]) for o in outs]
    Comm.finish("c", jnp.concatenate([done_b.reshape(-1)[:LANE], outs[0].reshape(-1)[:LANE]]))

    loss = lax.psum(loss.reshape(()), ("x", "y", "c"))
    out = [loss, grad_x.reshape(x.shape)]
    for k in range(4):
        out += [res[nm][k] for nm in NAMES]
    return tuple(out)
```

```python
import functools

import jax
import jax.numpy as jnp
from jax import lax
from jax.experimental import pallas as pl
from jax.experimental.pallas import tpu as pltpu

F32 = jnp.float32
BF16 = jnp.bfloat16
HI = lax.Precision.HIGHEST
MESH = pl.DeviceIdType.MESH
ANY = pl.BlockSpec(memory_space=pl.ANY)

NDEV = 8
D = 2048
S = 2048
H = 8
DH = 128
DNW = H * DH
CFW = 1024
CFK = 31
DNK = 4
FFN = 5632
FFK = 3
CH = 64
NCH = S // CH
EPS = 1e-6
NIN = 10256
NINP = 10368
O_Z, O_GA, O_GB, O_GLU, O_SM = 3072, 4096, 6144, 8192, 10240
LANE = 128
TS = 256
VMEM_LIMIT = 56 * 1024 * 1024

ADAM_LR, ADAM_B1, ADAM_B2, ADAM_EPS, ADAM_WD, ADAM_STEP = 0.001, 0.9, 0.999, 1e-08, 0.01, 10


def _call(body, name, out_shape, grid=(), in_specs=None, out_specs=None, scratch=(), sem=None, aliases=None):
    kw = {}
    if aliases:
        kw["input_output_aliases"] = aliases
    if in_specs is not None:
        kw["in_specs"] = in_specs
    if out_specs is not None:
        kw["out_specs"] = out_specs
    return pl.pallas_call(
        body, out_shape=out_shape, grid=grid, scratch_shapes=scratch, name=name,
        compiler_params=pltpu.CompilerParams(dimension_semantics=sem, vmem_limit_bytes=VMEM_LIMIT), **kw)


def _sds(shape, dtype=F32):
    return jax.ShapeDtypeStruct(shape, dtype)


def _tile(dim, pref):
    if dim <= pref:
        return dim
    best = None
    for t in range(LANE, pref + 1, LANE):
        if dim % t == 0:
            best = t
    assert best is not None, (dim, pref)
    return best


def _sigmoid(x):
    return 1.0 / (1.0 + jnp.exp(-x))


def _silu(x):
    return x * _sigmoid(x)


def _dsilu(x):
    s = _sigmoid(x)
    return s * (1.0 + x * (1.0 - s))


def _silu_both(x):
    s = _sigmoid(x)
    return x * s, s * (1.0 + x * (1.0 - s))


def _softplus(x):
    return jnp.maximum(x, 0.0) + jnp.log(1.0 + jnp.exp(-jnp.abs(x)))


def _dot(a, b, dims, precision=None):
    return lax.dot_general(a, b, (dims, ((), ())), preferred_element_type=F32, precision=precision)


NN = ((1,), (0,))
NT = ((1,), (1,))
TN = ((0,), (0,))


def _my_pos():
    return lax.axis_index("x"), lax.axis_index("y"), lax.axis_index("c")


def _mm(a, b, mode, out_dtype, name, tm=1024, tn=1024, tk=2048, a2=None, b2=None, dep=None):
    sharded = b.ndim == 3
    if sharded and mode == "nn":
        cs = b.shape[2]
        (m, k), n = a.shape, NDEV * cs
        gs = max(1, tn // cs)
        tm, tn, tk = _tile(m, tm), gs * cs, _tile(k, tk)
    elif sharded:
        assert mode == "nt"
        cs = b.shape[2]
        m, n, k = a.shape[0], b.shape[1], NDEV * cs
        gs = max(1, tk // cs)
        tm, tn, tk = _tile(m, tm), _tile(n, tn), gs * cs
    else:
        if mode == "nn":
            (m, k), (k2, n) = a.shape, b.shape
        elif mode == "nt":
            (m, k), (n, k2) = a.shape, b.shape
        else:
            (k, m), (k2, n) = a.shape, b.shape
        assert k == k2, (a.shape, b.shape, mode)
        n = n * (2 if b2 is not None else 1)
        tm, tn, tk = _tile(m, tm), _tile(n // (2 if b2 is not None else 1), tn), _tile(k, tk)
    nk, nj = k // tk, n // tn
    halfk, halfj = nk // 2, nj // 2
    dims = {"nn": NN, "nt": NT, "tn": TN}[mode]

    n_in = 2 + (a2 is not None) + (b2 is not None) + (dep is not None)

    def body(*refs):
        a_ref, b_ref = refs[0], refs[1]
        x_ref = refs[2] if (a2 is not None or b2 is not None) else None
        o_ref = refs[n_in]
        acc_ref = refs[n_in + 1] if nk > 1 else None
        j, kk = pl.program_id(1), pl.program_id(2)

        if nk > 1:
            @pl.when(kk == 0)
            def _():
                acc_ref[...] = jnp.zeros_like(acc_ref)

        def accumulate(product, cols=slice(None)):
            if nk == 1:
                o_ref[:, cols] = product().astype(o_ref.dtype)
            else:
                acc_ref[:, cols] += product()

        if sharded and mode == "nn":
            for q in range(gs):
                accumulate(lambda q=q: _dot(a_ref[...], b_ref[q], NN), slice(q * cs, (q + 1) * cs))
        elif sharded:
            def contract(lhs_ref):
                def product():
                    part = None
                    for q in range(gs):
                        term = _dot(lhs_ref[:, q * cs:(q + 1) * cs], b_ref[q], NT)
                        part = term if part is None else part + term
                    return part
                accumulate(product)

            if a2 is None:
                contract(a_ref)
            else:
                pl.when(kk < halfk)(lambda: contract(a_ref))
                pl.when(kk >= halfk)(lambda: contract(x_ref))
        elif b2 is not None:
            pl.when(j < halfj)(lambda: accumulate(lambda: _dot(a_ref[...], b_ref[...], dims)))
            pl.when(j >= halfj)(lambda: accumulate(lambda: _dot(a_ref[...], x_ref[...], dims)))
        else:
            accumulate(lambda: _dot(a_ref[...], b_ref[...], dims))

        if nk > 1:
            @pl.when(kk == nk - 1)
            def _():
                o_ref[...] = acc_ref[...].astype(o_ref.dtype)

    ins, in_specs = [a], []
    if mode == "tn":
        in_specs.append(pl.BlockSpec((tk, tm), lambda i, j, kk: (kk, i)))
    elif a2 is not None:
        in_specs.append(pl.BlockSpec((tm, tk), lambda i, j, kk: (i, jnp.minimum(kk, halfk - 1))))
    else:
        in_specs.append(pl.BlockSpec((tm, tk), lambda i, j, kk: (i, kk)))
    ins.append(b)
    if sharded and mode == "nn":
        in_specs.append(pl.BlockSpec((gs, tk, cs), lambda i, j, kk: (j, kk, 0)))
    elif sharded:
        in_specs.append(pl.BlockSpec((gs, tn, cs), lambda i, j, kk: (kk, j, 0)))
    elif mode == "nt":
        in_specs.append(pl.BlockSpec((tn, tk), lambda i, j, kk: (j, kk)))
    elif b2 is not None:
        in_specs.append(pl.BlockSpec((tk, tn), lambda i, j, kk: (kk, jnp.minimum(j, halfj - 1))))
    else:
        in_specs.append(pl.BlockSpec((tk, tn), lambda i, j, kk: (kk, j)))
    if a2 is not None:
        ins.append(a2)
        in_specs.append(pl.BlockSpec((tm, tk), lambda i, j, kk: (i, jnp.maximum(kk - halfk, 0))))
    if b2 is not None:
        ins.append(b2)
        in_specs.append(pl.BlockSpec((tk, tn), lambda i, j, kk: (kk, jnp.maximum(j - halfj, 0))))
    if dep is not None:
        ins.append(dep)
        in_specs.append(ANY)
    return _call(body, name, _sds((m, n), out_dtype), grid=(m // tm, nj, nk),
                 in_specs=in_specs, out_specs=pl.BlockSpec((tm, tn), lambda i, j, kk: (i, j)),
                 scratch=[pltpu.VMEM((tm, tn), F32)] if nk > 1 else [],
                 sem=("parallel", "parallel", "arbitrary"))(*ins)


def _ada_fwd(c_all, w_sh, b_sh):
    n = w_sh.shape[1]
    tn = 512

    def body(c_ref, w_ref, b_ref, o_ref):
        ca = _silu(c_ref[...]).astype(BF16)
        o_ref[...] = _dot(ca, w_ref[...].astype(BF16), NN) + b_ref[...]

    return _call(body, "ada_fwd", _sds((NDEV, n)), grid=(n // tn,),
                 in_specs=[pl.BlockSpec((NDEV, D), lambda j: (0, 0)), pl.BlockSpec((D, tn), lambda j: (0, j)),
                           pl.BlockSpec((1, tn), lambda j: (0, j))],
                 out_specs=pl.BlockSpec((NDEV, tn), lambda j: (0, j)), sem=("parallel",))(c_all, w_sh, b_sh)


def _adam(w, g, m, v):
    m = ADAM_B1 * m + (1.0 - ADAM_B1) * g
    v = ADAM_B2 * v + (1.0 - ADAM_B2) * (g * g)
    m_hat = m / (1.0 - ADAM_B1 ** ADAM_STEP)
    v_hat = v / (1.0 - ADAM_B2 ** ADAM_STEP)
    delta = -ADAM_LR * (m_hat / (jnp.sqrt(v_hat) + ADAM_EPS) + ADAM_WD * w)
    return delta, m, v


def _ada_bwd_adam(c_all, dmod_sel, w, m, v):
    r, n = w.shape
    tr = 256

    def body(c_ref, d_ref, w_ref, m_ref, v_ref, g_ref, dl_ref, nm_ref, nv_ref):
        ca = _silu(c_ref[...])
        g = _dot(ca, d_ref[...], TN, precision=HI)
        dl, nm, nv = _adam(w_ref[...], g, m_ref[...], v_ref[...])
        g_ref[...] = g
        dl_ref[...] = dl
        nm_ref[...] = nm
        nv_ref[...] = nv

    big = pl.BlockSpec((tr, n), lambda i: (i, 0))
    return _call(body, "ada_bwd_adam", [_sds((r, n))] * 4, grid=(r // tr,),
                 in_specs=[pl.BlockSpec((NDEV, tr), lambda i: (0, i)), pl.BlockSpec((NDEV, n), lambda i: (0, 0)),
                           big, big, big],
                 out_specs=[big] * 4, sem=("parallel",))(c_all, dmod_sel, w, m, v)


def _row_spec(width=D):
    return pl.BlockSpec((TS, width), lambda i: (i, 0))


def _vec_spec(width=D):
    return pl.BlockSpec((1, width), lambda i: (0, 0))


def _acc_spec(width=D):
    return pl.BlockSpec((8, width), lambda i: (0, 0))


def _norm_mod(x, g, sc, sh, name):
    def body(x_ref, g_ref, sc_ref, sh_ref, o_ref):
        xv = x_ref[...]
        r = lax.rsqrt(jnp.mean(xv * xv, axis=-1, keepdims=True) + EPS)
        o_ref[...] = ((xv * r) * g_ref[...] * (1.0 + sc_ref[...]) + sh_ref[...]).astype(BF16)

    return _call(body, name, _sds((S, D), BF16), grid=(S // TS,),
                 in_specs=[_row_spec(), _vec_spec(), _vec_spec(), _vec_spec()], out_specs=_row_spec(),
                 sem=("parallel",))(x, g, sc, sh)


def _resid_norm_mod(x, mix, gt, g, sc, sh, name):
    def body(x_ref, mix_ref, gt_ref, g_ref, sc_ref, sh_ref, x2_ref, o_ref):
        xv = x_ref[...] + gt_ref[...] * mix_ref[...]
        x2_ref[...] = xv
        r = lax.rsqrt(jnp.mean(xv * xv, axis=-1, keepdims=True) + EPS)
        o_ref[...] = ((xv * r) * g_ref[...] * (1.0 + sc_ref[...]) + sh_ref[...]).astype(BF16)

    return _call(body, name, [_sds((S, D)), _sds((S, D), BF16)], grid=(S // TS,),
                 in_specs=[_row_spec(), _row_spec()] + [_vec_spec()] * 4, out_specs=[_row_spec(), _row_spec()],
                 sem=("parallel",))(x, mix, gt, g, sc, sh)


def _acc_rows(acc_ref, rows):
    @pl.when(pl.program_id(0) == 0)
    def _():
        acc_ref[...] = jnp.zeros_like(acc_ref)

    for k, row in enumerate(rows):
        acc_ref[k:k + 1, :] += row


def _loss_head(x2, f, tgt, gt2, gf):
    def body(x2_ref, f_ref, t_ref, gt_ref, gf_ref, dx_ref, df_ref, acc_ref):
        fv = f_ref[...]
        x3 = x2_ref[...] + gt_ref[...] * fv
        r = lax.rsqrt(jnp.mean(x3 * x3, axis=-1, keepdims=True) + EPS)
        xn = x3 * r
        e = xn * gf_ref[...] - t_ref[...]
        loss = 0.5 * jnp.sum(jnp.mean(e * e, axis=-1, keepdims=True), axis=0, keepdims=True)
        dy = e * (1.0 / D)
        dxn = dy * gf_ref[...]
        dx3 = r * (dxn - xn * jnp.mean(dxn * xn, axis=-1, keepdims=True))
        dx_ref[...] = dx3
        df_ref[...] = (dx3 * gt_ref[...]).astype(BF16)
        _acc_rows(acc_ref, [jnp.sum(dy * xn, axis=0, keepdims=True), jnp.sum(dx3 * fv, axis=0, keepdims=True),
                            jnp.broadcast_to(loss, (1, D))])

    return _call(body, "loss_head", [_sds((S, D)), _sds((S, D), BF16), _sds((8, D))], grid=(S // TS,),
                 in_specs=[_row_spec(), _row_spec(), _row_spec(), _vec_spec(), _vec_spec()],
                 out_specs=[_row_spec(), _row_spec(), _acc_spec()], sem=("arbitrary",))(x2, f, tgt, gt2, gf)


def _norm_mod_bwd(dhn, x, dres, g, sc, name, mix=None, gt=None):
    gated = mix is not None

    def body(*refs):
        if gated:
            dhn_ref, x_ref, dres_ref, g_ref, sc_ref, mix_ref, gt_ref, dx_ref, dmix_ref, acc_ref = refs
        else:
            dhn_ref, x_ref, dres_ref, g_ref, sc_ref, dx_ref, acc_ref = refs
        xv = x_ref[...]
        dh = dhn_ref[...]
        r = lax.rsqrt(jnp.mean(xv * xv, axis=-1, keepdims=True) + EPS)
        xn = xv * r
        gv = g_ref[...]
        sc1 = 1.0 + sc_ref[...]
        dxn = dh * gv * sc1
        dx = dres_ref[...] + r * (dxn - xn * jnp.mean(dxn * xn, axis=-1, keepdims=True))
        dx_ref[...] = dx
        rows = [jnp.sum(dh, axis=0, keepdims=True), jnp.sum(dh * xn * gv, axis=0, keepdims=True),
                jnp.sum(dh * xn * sc1, axis=0, keepdims=True)]
        if gated:
            rows.append(jnp.sum(dx * mix_ref[...], axis=0, keepdims=True))
            dmix_ref[...] = (dx * gt_ref[...]).astype(BF16)
        _acc_rows(acc_ref, rows)

    ins = [dhn, x, dres, g, sc]
    in_specs = [_row_spec(), _row_spec(), _row_spec(), _vec_spec(), _vec_spec()]
    outs = [_sds((S, D))]
    out_specs = [_row_spec()]
    if gated:
        ins += [mix, gt]
        in_specs += [_row_spec(), _vec_spec()]
        outs.append(_sds((S, D), BF16))
        out_specs.append(_row_spec())
    outs.append(_sds((8, D)))
    out_specs.append(_acc_spec())
    return _call(body, name, outs, grid=(S // TS,), in_specs=in_specs, out_specs=out_specs,
                 sem=("arbitrary",))(*ins)


RC = 64


RC_WIDE = 256


def _conv_fwd_rows(pad_ref, w_ref, kw, head, r0, rc=RC):
    acc = None
    for k in range(kw):
        term = w_ref[k:k + 1, :] * pad_ref[pl.ds(head - (kw - 1) + k + r0, rc), :]
        acc = term if acc is None else acc + term
    return acc


def _conv_bwd_rows(pad2_ref, w_ref, kw, r0, rc=RC):
    acc = None
    for k in range(kw):
        term = w_ref[k:k + 1, :] * pad2_ref[pl.ds(kw - 1 - k + r0, rc), :]
        acc = term if acc is None else acc + term
    return acc


def _conv_dw(pad_ref, dout_ref, dw_ref, kw, head, rc=RC):
    for k in range(kw):
        acc = None
        for r0 in range(0, S, rc):
            term = jnp.sum(pad_ref[pl.ds(head - (kw - 1) + k + r0, rc), :] * dout_ref[pl.ds(r0, rc), :],
                           axis=0, keepdims=True)
            acc = term if acc is None else acc + term
        dw_ref[k:k + 1, :] = acc


def _col_spec(width, off_blocks=0):
    return pl.BlockSpec((S, width), lambda j: (0, j + off_blocks))


def _dn_pre_fwd(proj, conv_w):
    head = 8

    def body(x_ref, w_ref, o_ref, pad_ref):
        j = pl.program_id(0)
        pad_ref[pl.ds(0, head), :] = jnp.zeros((head, DH), F32)
        pad_ref[pl.ds(head, S), :] = x_ref[...]
        scale = jnp.where(j < H, DH ** -0.5, 1.0)
        for r0 in range(0, S, RC):
            y = _silu(_conv_fwd_rows(pad_ref, w_ref, DNK, head, r0))
            rinv = lax.rsqrt(jnp.sum(y * y, axis=-1, keepdims=True) + EPS)
            o_ref[pl.ds(r0, RC), :] = jnp.where(j < 2 * H, y * rinv * scale, y)

    return _call(body, "dn_pre_fwd", _sds((S, 3 * DNW)), grid=(3 * H,),
                 in_specs=[_col_spec(DH), pl.BlockSpec((DNK, DH), lambda j: (0, j))], out_specs=_col_spec(DH),
                 scratch=[pltpu.VMEM((S + head, DH), F32)], sem=("parallel",))(proj, conv_w)


def _dn_pre_bwd(dq, dk, dv, proj, conv_w, dproj):
    head = 8

    def body(dq_ref, dk_ref, dv_ref, x_ref, w_ref, dproj_in, dx_ref, dw_ref, pad_ref, pad2_ref):
        j = pl.program_id(0)
        pad_ref[pl.ds(0, head), :] = jnp.zeros((head, DH), F32)
        pad_ref[pl.ds(head, S), :] = x_ref[...]
        pad2_ref[pl.ds(S, head), :] = jnp.zeros((head, DH), F32)
        scale = jnp.where(j < H, DH ** -0.5, 1.0)
        for r0 in range(0, S, RC_WIDE):
            xc = _conv_fwd_rows(pad_ref, w_ref, DNK, head, r0, RC_WIDE)
            y, dy_dxc = _silu_both(xc)
            rinv = lax.rsqrt(jnp.sum(y * y, axis=-1, keepdims=True) + EPS)
            yn = y * rinv
            rows = pl.ds(r0, RC_WIDE)
            do = jnp.where(j < H, dq_ref[rows, :], jnp.where(j < 2 * H, dk_ref[rows, :], dv_ref[rows, :]))
            dy_n = scale * rinv * (do - yn * jnp.sum(do * yn, axis=-1, keepdims=True))
            dy = jnp.where(j < 2 * H, dy_n, do)
            pad2_ref[rows, :] = dy * dy_dxc
        for r0 in range(0, S, RC_WIDE):
            dx_ref[pl.ds(r0, RC_WIDE), :] = _conv_bwd_rows(pad2_ref, w_ref, DNK, r0, RC_WIDE).astype(BF16)
        _conv_dw(pad_ref, pad2_ref, dw_ref, DNK, head, RC_WIDE)

    wspec = pl.BlockSpec((DNK, DH), lambda j: (0, j))
    head_col = lambda lo: pl.BlockSpec((S, DH), lambda j: (0, jnp.clip(j - lo, 0, H - 1)))
    return _call(body, "dn_pre_bwd", [_sds((S, NINP), BF16), _sds((DNK, 3 * DNW))], grid=(3 * H,),
                 in_specs=[head_col(0), head_col(H), head_col(2 * H), _col_spec(DH), wspec, ANY],
                 out_specs=[_col_spec(DH), wspec],
                 scratch=[pltpu.VMEM((S + head, DH), F32), pltpu.VMEM((S + head, DH), F32)],
                 sem=("parallel",), aliases={5: 0})(dq, dk, dv, proj, conv_w, dproj)


CF_HEAD = 32
CF_VAL = pl.BlockSpec((S, LANE), lambda j: (0, O_GLU // LANE + 2 * j))
CF_GL = pl.BlockSpec((S, LANE), lambda j: (0, O_GLU // LANE + 2 * j + 1))


def _cf_conv_fwd(proj, conv_w):
    def body(val_ref, gl_ref, w_ref, o_ref, pad_ref):
        pad_ref[pl.ds(0, CF_HEAD), :] = jnp.zeros((CF_HEAD, LANE), F32)
        pad_ref[pl.ds(CF_HEAD, S), :] = val_ref[...] * _sigmoid(gl_ref[...])
        for r0 in range(0, S, RC):
            o_ref[pl.ds(r0, RC), :] = _conv_fwd_rows(pad_ref, w_ref, CFK, CF_HEAD, r0)

    wspec = pl.BlockSpec((CFK, LANE), lambda j: (0, j))
    return _call(body, "cf_conv_fwd", _sds((S, CFW)), grid=(CFW // LANE,),
                 in_specs=[CF_VAL, CF_GL, wspec], out_specs=_col_spec(LANE),
                 scratch=[pltpu.VMEM((S + CF_HEAD, LANE), F32)], sem=("parallel",))(proj, proj, conv_w)


def _cf_conv_bwd(du1, proj, conv_w, dproj):
    def body(d_ref, val_ref, gl_ref, w_ref, dproj_in, dp_ref, dw_ref, pad_ref, pad2_ref):
        sg = _sigmoid(gl_ref[...])
        pad_ref[pl.ds(0, CF_HEAD), :] = jnp.zeros((CF_HEAD, LANE), F32)
        pad_ref[pl.ds(CF_HEAD, S), :] = val_ref[...] * sg
        pad2_ref[pl.ds(0, S), :] = d_ref[...]
        pad2_ref[pl.ds(S, CF_HEAD), :] = jnp.zeros((CF_HEAD, LANE), F32)
        for r0 in range(0, S, RC):
            du0 = _conv_bwd_rows(pad2_ref, w_ref, CFK, r0)
            rows = pl.ds(r0, RC)
            sgr = _sigmoid(gl_ref[rows, :])
            dp_ref[rows, 0:LANE] = (du0 * sgr).astype(BF16)
            dp_ref[rows, LANE:2 * LANE] = (du0 * val_ref[rows, :] * sgr * (1.0 - sgr)).astype(BF16)
        _conv_dw(pad_ref, pad2_ref, dw_ref, CFK, CF_HEAD)

    wspec = pl.BlockSpec((CFK, LANE), lambda j: (0, j))
    return _call(body, "cf_conv_bwd", [_sds((S, NINP), BF16), _sds((CFK, CFW))], grid=(CFW // LANE,),
                 in_specs=[_col_spec(LANE), CF_VAL, CF_GL, wspec, ANY],
                 out_specs=[pl.BlockSpec((S, 2 * LANE), lambda j: (0, O_GLU // (2 * LANE) + j)), wspec],
                 scratch=[pltpu.VMEM((S + CF_HEAD, LANE), F32), pltpu.VMEM((S + CF_HEAD, LANE), F32)],
                 sem=("parallel",), aliases={4: 0})(du1, proj, proj, conv_w, dproj)


def _cf_ln_fwd(u1, g, b):
    def body(u_ref, g_ref, b_ref, o_ref):
        u = u_ref[...]
        mu = jnp.mean(u, axis=-1, keepdims=True)
        xc = u - mu
        y = xc * lax.rsqrt(jnp.mean(xc * xc, axis=-1, keepdims=True) + EPS)
        o_ref[...] = _silu(y * g_ref[...] + b_ref[...]).astype(BF16)

    return _call(body, "cf_ln_fwd", _sds((S, CFW), BF16), grid=(S // TS,),
                 in_specs=[_row_spec(CFW), _vec_spec(CFW), _vec_spec(CFW)], out_specs=_row_spec(CFW),
                 sem=("parallel",))(u1, g, b)


def _cf_ln_bwd(du3, u1, g, b):
    def body(d_ref, u_ref, g_ref, b_ref, du_ref, acc_ref):
        u = u_ref[...]
        mu = jnp.mean(u, axis=-1, keepdims=True)
        xc = u - mu
        rstd = lax.rsqrt(jnp.mean(xc * xc, axis=-1, keepdims=True) + EPS)
        xh = xc * rstd
        du2 = d_ref[...] * _dsilu(xh * g_ref[...] + b_ref[...])
        dxh = du2 * g_ref[...]
        du_ref[...] = rstd * (dxh - jnp.mean(dxh, axis=-1, keepdims=True)
                              - xh * jnp.mean(dxh * xh, axis=-1, keepdims=True))
        _acc_rows(acc_ref, [jnp.sum(du2 * xh, axis=0, keepdims=True), jnp.sum(du2, axis=0, keepdims=True)])

    return _call(body, "cf_ln_bwd", [_sds((S, CFW)), _sds((8, CFW))], grid=(S // TS,),
                 in_specs=[_row_spec(CFW), _row_spec(CFW), _vec_spec(CFW), _vec_spec(CFW)],
                 out_specs=[_row_spec(CFW), _acc_spec(CFW)], sem=("arbitrary",))(du3, u1, g, b)


FB = 256
FNB = FFN // FB
FF_HEAD = 8


def _ffn_mid_fwd(upall, conv_w):
    def body(gate_ref, up_ref, w_ref, o_ref, pad_ref):
        pad_ref[pl.ds(0, FF_HEAD), :] = jnp.zeros((FF_HEAD, FB), F32)
        pad_ref[pl.ds(FF_HEAD, S), :] = gate_ref[...]
        for r0 in range(0, S, RC):
            gc = _conv_fwd_rows(pad_ref, w_ref, FFK, FF_HEAD, r0)
            o_ref[pl.ds(r0, RC), :] = (_silu(gc) * up_ref[pl.ds(r0, RC), :]).astype(BF16)

    wspec = pl.BlockSpec((FFK, FB), lambda j: (0, j))
    return _call(body, "ffn_mid_fwd", _sds((S, FFN), BF16), grid=(FNB,),
                 in_specs=[_col_spec(FB), _col_spec(FB, FNB), wspec], out_specs=_col_spec(FB),
                 scratch=[pltpu.VMEM((S + FF_HEAD, FB), F32)], sem=("parallel",))(upall, upall, conv_w)


def _ffn_mid_bwd(dh, upall, conv_w):
    def body(d_ref, gate_ref, up_ref, w_ref, dgate_ref, dup_ref, dw_ref, pad_ref, pad2_ref):
        pad_ref[pl.ds(0, FF_HEAD), :] = jnp.zeros((FF_HEAD, FB), F32)
        pad_ref[pl.ds(FF_HEAD, S), :] = gate_ref[...]
        pad2_ref[pl.ds(S, FF_HEAD), :] = jnp.zeros((FF_HEAD, FB), F32)
        for r0 in range(0, S, RC):
            rows = pl.ds(r0, RC)
            gc = _conv_fwd_rows(pad_ref, w_ref, FFK, FF_HEAD, r0)
            dhv = d_ref[rows, :]
            act, dact = _silu_both(gc)
            dup_ref[rows, :] = (dhv * act).astype(BF16)
            pad2_ref[rows, :] = dhv * up_ref[rows, :] * dact
        for r0 in range(0, S, RC):
            dgate_ref[pl.ds(r0, RC), :] = _conv_bwd_rows(pad2_ref, w_ref, FFK, r0).astype(BF16)
        _conv_dw(pad_ref, pad2_ref, dw_ref, FFK, FF_HEAD)

    wspec = pl.BlockSpec((FFK, FB), lambda j: (0, j))
    return _call(body, "ffn_mid_bwd", [_sds((S, FFN), BF16), _sds((S, FFN), BF16), _sds((FFK, FFN))],
                 grid=(FNB,), in_specs=[_col_spec(FB), _col_spec(FB), _col_spec(FB, FNB), wspec],
                 out_specs=[_col_spec(FB), _col_spec(FB), wspec],
                 scratch=[pltpu.VMEM((S + FF_HEAD, FB), F32), pltpu.VMEM((S + FF_HEAD, FB), F32)],
                 sem=("parallel",))(dh, upall, upall, conv_w)


GT = 256
SM_BLK = O_SM // LANE


def _chunk_tri(lower):
    r = lax.broadcasted_iota(jnp.int32, (GT, GT), 0)
    c = lax.broadcasted_iota(jnp.int32, (GT, GT), 1)
    same = (r // CH) == (c // CH)
    tri = (c <= r) if lower else (c >= r)
    return jnp.where(same & tri, 1.0, 0.0).astype(F32)


def _gates_fwd(proj, alog_v, dtb_v):
    def body(sm_ref, al_ref, dt_ref, o_ref):
        lane = lax.broadcasted_iota(jnp.int32, (GT, LANE), 1)
        tri = _chunk_tri(True)
        na = -jnp.exp(al_ref[...])
        for r0 in range(0, S, GT):
            sm = sm_ref[pl.ds(r0, GT), :]
            raw = jnp.where((lane >= H) & (lane < 2 * H), na * _softplus(sm + dt_ref[...]), 0.0)
            gc = _dot(tri, raw, NN, precision=HI)
            o_ref[pl.ds(r0, GT), :] = jnp.where(lane < H, _sigmoid(sm), gc)

    return _call(body, "gates_fwd", _sds((S, LANE)), grid=(1,),
                 in_specs=[pl.BlockSpec((S, LANE), lambda i: (0, SM_BLK)), _vec_spec(LANE), _vec_spec(LANE)],
                 out_specs=pl.BlockSpec((S, LANE), lambda i: (0, 0)), sem=("arbitrary",))(proj, alog_v, dtb_v)


def _gates_bwd(dgb, proj, alog_v, dtb_v, dproj):
    def body(d_ref, sm_ref, al_ref, dt_ref, dproj_in, o_ref, acc_ref):
        lane = lax.broadcasted_iota(jnp.int32, (GT, LANE), 1)
        is_g = (lane >= H) & (lane < 2 * H)
        tri = _chunk_tri(False)
        na = -jnp.exp(al_ref[...])
        d_al = jnp.zeros((1, LANE), F32)
        d_dt = jnp.zeros((1, LANE), F32)
        for r0 in range(0, S, GT):
            sm = sm_ref[pl.ds(r0, GT), :]
            dv = d_ref[pl.ds(r0, GT), :]
            z = sm + dt_ref[...]
            draw = _dot(tri, jnp.where(is_g, dv, 0.0), NN, precision=HI)
            dlogit = jnp.where(is_g, draw * na * _sigmoid(z), 0.0)
            d_al = d_al + jnp.sum(jnp.where(is_g, draw * na * _softplus(z), 0.0), axis=0, keepdims=True)
            d_dt = d_dt + jnp.sum(dlogit, axis=0, keepdims=True)
            bt = _sigmoid(sm)
            o_ref[pl.ds(r0, GT), :] = jnp.where(lane < H, dv * bt * (1.0 - bt), dlogit).astype(BF16)
        acc_ref[...] = jnp.zeros_like(acc_ref)
        acc_ref[0:1, :] = d_al
        acc_ref[1:2, :] = d_dt

    return _call(body, "gates_bwd", [_sds((S, NINP), BF16), _sds((8, LANE))], grid=(1,),
                 in_specs=[pl.BlockSpec((S, LANE), lambda i: (0, 0)), pl.BlockSpec((S, LANE), lambda i: (0, SM_BLK)),
                           _vec_spec(LANE), _vec_spec(LANE), ANY],
                 out_specs=[pl.BlockSpec((S, LANE), lambda i: (0, SM_BLK)), _acc_spec(LANE)],
                 sem=("arbitrary",), aliases={4: 0})(dgb, proj, alog_v, dtb_v, dproj)


HB = 4


def _each(fn, *lists):
    return [fn(*args) for args in zip(*lists)]


def _neumann_inv(a, eye):
    p = _each(lambda m: -m, a)
    t = _each(lambda m: eye + m, p)
    for _ in range(5):
        p = _each(lambda m: _dot(m, m, NN, precision=HI), p)
        t = _each(lambda tt, pp: tt + _dot(tt, pp, NN, precision=HI), t, p)
    return t


def _head_specs():
    q = pl.BlockSpec((S, HB * DH), lambda h: (0, h), pipeline_mode=ONE_BUF)
    k = pl.BlockSpec((S, HB * DH), lambda h: (0, H // HB + h), pipeline_mode=ONE_BUF)
    v = pl.BlockSpec((S, HB * DH), lambda h: (0, 2 * H // HB + h), pipeline_mode=ONE_BUF)
    gb = pl.BlockSpec((HB, S, DH), lambda h: (h, 0, 0), pipeline_mode=ONE_BUF)
    gr = pl.BlockSpec((HB, NCH, CH), lambda h: (h, 0, 0))
    return q, k, v, gb, gr


ONE_BUF = pl.Buffered(1)
ST_SPEC = pl.BlockSpec((HB, NCH, DH, DH), lambda h: (h, 0, 0, 0), pipeline_mode=ONE_BUF)
TM_SPEC = pl.BlockSpec((HB, NCH, CH, CH), lambda h: (h, 0, 0, 0), pipeline_mode=ONE_BUF)
HCOL = pl.BlockSpec((S, HB * DH), lambda h: (0, h), pipeline_mode=ONE_BUF)


def _delta_fwd(qkvn, gb, gr, bb):
    def body(q_ref, k_ref, v_ref, gb_ref, gr_ref, bb_ref, o_ref, st_ref, tm_ref):
        ri = lax.broadcasted_iota(jnp.int32, (CH, CH), 0)
        ci = lax.broadcasted_iota(jnp.int32, (CH, CH), 1)
        strict = ri > ci
        causal = ri >= ci
        eye = jnp.where(ri == ci, 1.0, 0.0).astype(F32)

        hs = list(range(HB))
        cols = [slice(hh * DH, (hh + 1) * DH) for hh in hs]
        bf = lambda m: m.astype(BF16)

        def local(n):
            rows = pl.ds(pl.multiple_of(n * CH, CH), CH)
            c = dict(rows=rows, n=n)
            c["q"] = [q_ref[rows, cc] for cc in cols]
            c["k"] = [k_ref[rows, cc] for cc in cols]
            c["v"] = [v_ref[rows, cc] for cc in cols]
            c["g"] = [gb_ref[hh, rows, :] for hh in hs]
            c["beta"] = [bb_ref[hh, rows, :] for hh in hs]
            diff = [c["g"][hh][:, :CH] - gr_ref[hh, pl.ds(n, 1), :] for hh in hs]
            c["el"] = _each(lambda d: jnp.exp(jnp.where(causal, d, 0.0)), diff)
            c["eg"] = _each(jnp.exp, c["g"])
            c["gl"] = _each(lambda m: m[CH - 1:CH, :], c["g"])
            c["kb"] = _each(lambda x, y: x * y, c["k"], c["beta"])
            c["kbf"] = _each(bf, c["k"])
            c["a"] = _each(lambda x, y, e: jnp.where(strict, _dot(bf(x), y, NT) * e, 0.0), c["kb"], c["kbf"], c["el"])
            return c

        def advance(c, t, sts):
            n, rows = c["n"], c["rows"]
            for hh in hs:
                tm_ref[hh, n] = t[hh]
                st_ref[hh, n] = sts[hh]
            sb = _each(bf, sts)
            r = _each(lambda vv, bb_, kk, ee, ss: vv * bb_ - _dot(bf(kk * ee), ss, NN), c["v"], c["beta"], c["kb"], c["eg"], sb)
            ub = _each(lambda tt, rr: bf(_dot(tt, rr, NN, precision=HI)), t, r)
            p = _each(lambda qq, kk, e: jnp.where(causal, _dot(bf(qq), kk, NT) * e, 0.0), c["q"], c["kbf"], c["el"])
            o = _each(lambda qq, ee, ss, pp, uu: _dot(bf(qq * ee), ss, NN) + _dot(bf(pp), uu, NN), c["q"], c["eg"], sb, p, ub)
            for hh in hs:
                o_ref[rows, cols[hh]] = o[hh]
            kd = _each(lambda kk, l, gg: kk * jnp.exp(l - gg), c["k"], c["gl"], c["g"])
            return _each(lambda st, l, kk, uu: st * jnp.exp(l) + _dot(bf(kk), uu, TN), sts, c["gl"], kd, ub)

        def step(i, sts):
            c0, c1 = local(2 * i), local(2 * i + 1)
            t = _neumann_inv(c0["a"] + c1["a"], eye)
            sts = advance(c0, t[:HB], list(sts))
            return tuple(advance(c1, t[HB:], sts))

        lax.fori_loop(0, NCH // 2, step, tuple(jnp.zeros((DH, DH), F32) for _ in hs))

    q, k, v, gbs, grs = _head_specs()
    return _call(body, "delta_fwd", [_sds((S, DNW)), _sds((H, NCH, DH, DH)), _sds((H, NCH, CH, CH))], grid=(H // HB,),
                 in_specs=[q, k, v, gbs, grs, gbs], out_specs=[HCOL, ST_SPEC, TM_SPEC],
                 sem=("parallel",))(qkvn, qkvn, qkvn, gb, gr, bb)


def _delta_bwd(qkvn, gb, gr, bb, st_all, tm_all, do_all):
    def body(q_ref, k_ref, v_ref, gb_ref, gr_ref, bb_ref, st_ref, tm_ref, do_ref,
             dq_ref, dk_ref, dv_ref, dg_ref, db_ref):
        ri = lax.broadcasted_iota(jnp.int32, (CH, CH), 0)
        ci = lax.broadcasted_iota(jnp.int32, (CH, CH), 1)
        lo_s, lo_c, up_s, up_c = ri > ci, ri >= ci, ri < ci, ri <= ci
        last_row = lax.broadcasted_iota(jnp.int32, (CH, 1), 0) == CH - 1

        def rs(mat):
            return jnp.sum(mat, axis=1, keepdims=True)

        def total(mat):
            return jnp.sum(rs(mat), axis=0, keepdims=True)

        hs = list(range(HB))
        cols = [slice(hh * DH, (hh + 1) * DH) for hh in hs]
        bf = lambda m: m.astype(BF16)
        mul = lambda x, y: x * y
        spread = jnp.full((8, DH), 1.0 / DH, F32)

        def as_row(col):
            return _dot(spread, jnp.broadcast_to(col, (CH, DH)), NT, precision=HI)[0:1, :]

        def step(i, dss):
            ns = [NCH - 1 - 2 * i, NCH - 2 - 2 * i]
            rws = [pl.ds(pl.multiple_of(n * CH, CH), CH) for n in ns]
            idx = [(cc, hh) for cc in range(2) for hh in hs]
            q = [q_ref[rws[cc], cols[hh]] for cc, hh in idx]
            k = [k_ref[rws[cc], cols[hh]] for cc, hh in idx]
            v = [v_ref[rws[cc], cols[hh]] for cc, hh in idx]
            do = [do_ref[rws[cc], cols[hh]] for cc, hh in idx]
            g = [gb_ref[hh, rws[cc], :] for cc, hh in idx]
            beta = [bb_ref[hh, rws[cc], :] for cc, hh in idx]
            t = [tm_ref[hh, ns[cc]] for cc, hh in idx]
            st = [st_ref[hh, ns[cc]] for cc, hh in idx]
            diff = [gg[:, :CH] - gr_ref[hh, pl.ds(ns[cc], 1), :] for gg, (cc, hh) in zip(g, idx)]
            el = _each(lambda d: jnp.exp(jnp.where(lo_c, d, 0.0)), diff)
            eu = _each(lambda d: jnp.exp(jnp.where(up_c, -d, 0.0)), diff)
            eg = _each(jnp.exp, g)
            gl = _each(lambda m: m[CH - 1:CH, :], g)
            egl = _each(jnp.exp, gl)
            ekd = _each(lambda l, m: jnp.exp(l - m), gl, g)
            kb = _each(mul, k, beta)
            kbg = _each(mul, kb, eg)
            qg = _each(mul, q, eg)
            kd = _each(mul, k, ekd)
            qb, kbf, kbb = _each(bf, q), _each(bf, k), _each(bf, kb)
            kbgb, qgb, kdb = _each(bf, kbg), _each(bf, qg), _each(bf, kd)
            sb, dob = _each(bf, st), _each(bf, do)
            r = _each(lambda vv, b, x, s: vv * b - _dot(x, s, NN), v, beta, kbgb, sb)
            u = _each(lambda tt, rr: _dot(tt, rr, NN, precision=HI), t, r)
            ub = _each(bf, u)
            kk = _each(lambda x, y: _dot(x, y, NT), kbb, kbf)
            qk = _each(lambda x, y: _dot(x, y, NT), qb, kbf)
            kkt = _each(lambda x, y: _dot(x, y, NT), kbf, kbb)
            qkt = _each(lambda x, y: _dot(x, y, NT), kbf, qb)
            pt = _each(lambda m, e: jnp.where(up_c, m * e, 0.0), qkt, eu)
            ds, du, dr, drb, ds_new = [], [], [], [], list(dss)
            for cc in range(2):
                sl = slice(cc * HB, (cc + 1) * HB)
                ds_c = ds_new
                dsb_c = _each(bf, ds_c)
                du_c = _each(lambda p, d, x, s: _dot(bf(p), d, NN) + _dot(x, s, NN), pt[sl], dob[sl], kdb[sl], dsb_c)
                dr_c = _each(lambda tt, d: _dot(tt, d, TN, precision=HI), t[sl], du_c)
                drb_c = _each(bf, dr_c)
                ds_new = _each(lambda x, d, e, s, y, z: _dot(x, d, TN) + e * s - _dot(y, z, TN),
                               qgb[sl], dob[sl], egl[sl], ds_c, kbgb[sl], drb_c)
                ds, du, dr, drb = ds + ds_c, du + du_c, dr + dr_c, drb + drb_c
            dsb = _each(bf, ds)
            dpg = _each(lambda d, uu, e: jnp.where(lo_c, _dot(d, uu, NT), 0.0) * e, dob, ub, el)
            dpgt = _each(lambda uu, d, e: jnp.where(up_c, _dot(uu, d, NT), 0.0) * e, ub, dob, eu)
            dag = _each(lambda d, uu, e: -jnp.where(lo_s, _dot(d, uu, NT), 0.0) * e, drb, ub, el)
            dagt = _each(lambda uu, d, e: -jnp.where(up_s, _dot(uu, d, NT), 0.0) * e, ub, drb, eu)
            dqg = _each(lambda d, s: _dot(d, s, NT), dob, sb)
            dkbg = _each(lambda d, s: -_dot(d, s, NT), drb, sb)
            dkd = _each(lambda uu, s: _dot(uu, s, NT), ub, dsb)
            dkb =_each(lambda a, x, y, e: _dot(bf(a), x, NN) + y * e, dag, kbf, dkbg, eg)
            dk = _each(lambda a, x, p, y, z, e, w, b: _dot(bf(a), x, NN) + _dot(bf(p), y, NN) + z * e + w * b,
                       dagt, kbb, dpgt, qb, dkd, ekd, dkb, beta)
            dq = _each(lambda p, x, y, e: _dot(bf(p), x, NN) + y * e, dpg, kbf, dqg, eg)
            dkd_kd = _each(lambda x, y: rs(x * y), dkd, kd)
            dg = _each(lambda a, x, p, y, at, xt, pt_, yt, z, w, c, d, e:
                       rs(a * x + p * y) - rs(at * xt + pt_ * yt) + rs(z * w) + rs(c * d) - e,
                       dag, kk, dpg, qk, dagt, kkt, dpgt, qkt, dqg, qg, dkbg, kbg, dkd_kd)
            dgl = _each(lambda x, e, s, y: jnp.sum(x, axis=0, keepdims=True) + e[:, 0:1] * total(s * y), dkd_kd, egl, ds, st)
            dg = _each(lambda x, y: x + jnp.where(last_row, y, 0.0), dg, dgl)
            dbeta = _each(lambda x, y, z, w: rs(x * y) + rs(z * w), dkb, k, dr, v)
            for j, (cc, hh) in enumerate(idx):
                dq_ref[rws[cc], cols[hh]] = dq[j]
                dk_ref[rws[cc], cols[hh]] = dk[j]
                dv_ref[rws[cc], cols[hh]] = dr[j] * beta[j]
                dg_ref[hh, pl.ds(ns[cc], 1), :] = as_row(dg[j])
                db_ref[hh, pl.ds(ns[cc], 1), :] = as_row(dbeta[j])
            return tuple(ds_new)

        lax.fori_loop(0, NCH // 2, step, tuple(jnp.zeros((DH, DH), F32) for _ in hs))

    q, k, v, gbs, grs = _head_specs()
    return _call(body, "delta_bwd",
                 [_sds((S, DNW)), _sds((S, DNW)), _sds((S, DNW)), _sds((H, NCH, CH)), _sds((H, NCH, CH))], grid=(H // HB,),
                 in_specs=[q, k, v, gbs, grs, gbs, ST_SPEC, TM_SPEC, HCOL], out_specs=[HCOL, HCOL, HCOL, grs, grs],
                 sem=("parallel",))(qkvn, qkvn, qkvn, gb, gr, bb, st_all, tm_all, do_all)


Z_BLK = O_Z // DNW


def _dn_post_fwd(o, proj, gn):
    def body(o_ref, z_ref, gn_ref, og_ref):
        for h in range(H):
            cols = slice(h * DH, (h + 1) * DH)
            ov = o_ref[:, cols]
            on = ov * lax.rsqrt(jnp.mean(ov * ov, axis=-1, keepdims=True) + EPS) * gn_ref[...]
            og_ref[:, cols] = (on * _silu(z_ref[:, cols])).astype(BF16)

    return _call(body, "dn_post_fwd", _sds((S, DNW), BF16), grid=(S // TS,),
                 in_specs=[_row_spec(DNW), pl.BlockSpec((TS, DNW), lambda i: (i, Z_BLK)), _vec_spec(DH)],
                 out_specs=_row_spec(DNW), sem=("parallel",))(o, proj, gn)


def _dn_post_bwd(dog, o, proj, gn, dproj):
    def body(d_ref, o_ref, z_ref, gn_ref, dproj_in, do_ref, dz_ref, acc_ref):
        dgn = jnp.zeros((1, DH), F32)
        for h in range(H):
            cols = slice(h * DH, (h + 1) * DH)
            ov, zv, dv = o_ref[:, cols], z_ref[:, cols], d_ref[:, cols]
            rinv = lax.rsqrt(jnp.mean(ov * ov, axis=-1, keepdims=True) + EPS)
            xn = ov * rinv
            act, dact = _silu_both(zv)
            don = dv * act
            dz_ref[:, cols] = (dv * xn * gn_ref[...] * dact).astype(BF16)
            dgn = dgn + jnp.sum(don * xn, axis=0, keepdims=True)
            dxn = don * gn_ref[...]
            do_ref[:, cols] = rinv * (dxn - xn * jnp.mean(dxn * xn, axis=-1, keepdims=True))
        _acc_rows(acc_ref, [dgn])

    zspec = pl.BlockSpec((TS, DNW), lambda i: (i, Z_BLK))
    return _call(body, "dn_post_bwd", [_sds((S, DNW)), _sds((S, NINP), BF16), _sds((8, DH))], grid=(S // TS,),
                 in_specs=[_row_spec(DNW), _row_spec(DNW), zspec, _vec_spec(DH), ANY],
                 out_specs=[_row_spec(DNW), zspec, _acc_spec(DH)], sem=("arbitrary",),
                 aliases={4: 1})(dog, o, proj, gn, dproj)


GA_BLK = O_GA // D
GB_BLK = O_GB // D


def _merge_fwd(ba, bb, proj):
    def body(a_ref, b_ref, ga_ref, gb_ref, o_ref):
        o_ref[...] = (_sigmoid(ga_ref[...]) * a_ref[...] + _sigmoid(gb_ref[...]) * b_ref[...]).astype(BF16)

    return _call(body, "merge_fwd", _sds((S, D), BF16), grid=(S // TS,),
                 in_specs=[_row_spec(), _row_spec(), pl.BlockSpec((TS, D), lambda i: (i, GA_BLK)),
                           pl.BlockSpec((TS, D), lambda i: (i, GB_BLK))],
                 out_specs=_row_spec(), sem=("parallel",))(ba, bb, proj, proj)


def _merge_bwd(dm, ba, bb, proj, dproj):
    def body(d_ref, a_ref, b_ref, ga_ref, gb_ref, dproj_in, dg_ref, da_ref, db_ref):
        d = d_ref[...]
        sa, sb = _sigmoid(ga_ref[...]), _sigmoid(gb_ref[...])
        dg_ref[:, 0:D] = (d * a_ref[...] * sa * (1.0 - sa)).astype(BF16)
        dg_ref[:, D:2 * D] = (d * b_ref[...] * sb * (1.0 - sb)).astype(BF16)
        da_ref[...] = (d * sa).astype(BF16)
        db_ref[...] = (d * sb).astype(BF16)

    return _call(body, "merge_bwd", [_sds((S, NINP), BF16), _sds((S, D), BF16), _sds((S, D), BF16)], grid=(S // TS,),
                 in_specs=[_row_spec(), _row_spec(), _row_spec(), pl.BlockSpec((TS, D), lambda i: (i, GA_BLK)),
                           pl.BlockSpec((TS, D), lambda i: (i, GB_BLK)), ANY],
                 out_specs=[pl.BlockSpec((TS, 2 * D), lambda i: (i, O_GA // (2 * D))), _row_spec(), _row_spec()],
                 sem=("parallel",), aliases={5: 0})(dm, ba, bb, proj, proj, dproj)


NSH = NIN // NDEV


def _pad_win(wt):
    rows = [wt[0:4096], wt[6160:6160 + 2 * D]]
    for j in range(CFW // LANE):
        rows += [wt[4112 + LANE * j:4112 + LANE * (j + 1)], wt[4112 + CFW + LANE * j:4112 + CFW + LANE * (j + 1)]]
    rows += [wt[4096:4112], jnp.zeros((NINP - NIN, wt.shape[1]), wt.dtype)]
    return jnp.concatenate(rows, axis=0)


def _unpad_win(gpt):
    rows = [gpt[0:4096], gpt[O_SM:O_SM + 16]]
    for half in range(2):
        rows += [gpt[O_GLU + (2 * j + half) * LANE:O_GLU + (2 * j + half + 1) * LANE] for j in range(CFW // LANE)]
    rows.append(gpt[O_GA:O_GA + 2 * D])
    return jnp.concatenate(rows, axis=0)


def _lane_vec(v8, offset):
    return jnp.pad(v8, ((0, 0), (offset, LANE - 8 - offset)))


def _tie(vec, token):
    return vec + token


def _local_step(x, tgt, mod, norm1_g, norm2_g, final_g, w_in_p, dn_conv_w, a_log, dt_bias, dn_norm_g,
                cf_conv_w, cf_ln_g, cf_ln_b, ffn_conv_w, comm):
    sh1, sc1, gt1, sh2, sc2, gt2 = (mod[:, i * D:(i + 1) * D] for i in range(6))
    alog_v, dtb_v = _lane_vec(a_log, H), _lane_vec(dt_bias, H)

    hn1 = _norm_mod(x, norm1_g, sc1, _tie(sh1, comm.token0), "norm_mod1")
    proj = _mm(hn1, w_in_p, "nt", F32, "mm_in", tn=1152)
    qkvn = _dn_pre_fwd(proj, dn_conv_w)
    gates = _gates_fwd(proj, alog_v, dtb_v)
    beta_t = gates[:, 0:H].T
    g_t = gates[:, H:2 * H].T
    gb = jnp.broadcast_to(g_t[:, :, None], (H, S, DH))
    bb = jnp.broadcast_to(beta_t[:, :, None], (H, S, DH))
    gr = g_t.reshape(H, NCH, CH)
    o, st_all, tm_all = _delta_fwd(qkvn, gb, gr, bb)
    og = _dn_post_fwd(o, proj, dn_norm_g)
    u1 = _cf_conv_fwd(proj, cf_conv_w)
    u3 = _cf_ln_fwd(u1, cf_ln_g, cf_ln_b)
    after = og[0:8, 0:LANE].astype(F32) + u3[0:8, 0:LANE].astype(F32)
    dn_w_o, cf_w_o, w_out = comm.late_weights("mix", after)
    br_a = _mm(og, dn_w_o, "nn", F32, "mm_dn_o")
    br_b = _mm(u3, cf_w_o, "nn", F32, "mm_cf_o")
    merged = _merge_fwd(br_a, br_b, proj)
    mix = _mm(merged, w_out, "nn", F32, "mm_out")
    x2, hn2 = _resid_norm_mod(x, mix, gt1, norm2_g, sc2, sh2, "resid_norm_mod2")
    ffn_w_up, ffn_w_down = comm.late_weights("ffn", hn2[0:8, 0:LANE].astype(F32))
    upall = _mm(hn2, ffn_w_up, "nn", F32, "mm_up")
    hmid = _ffn_mid_fwd(upall, ffn_conv_w)
    f = _mm(hmid, ffn_w_down, "nn", F32, "mm_down", tm=2048)

    dx3, df, acc_f = _loss_head(x2, f, tgt, gt2, final_g)
    d_final_g, d_gt2, loss = acc_f[0:1], acc_f[1:2], acc_f[2:3, 0:1]
    dhmid = _mm(df, ffn_w_down, "nt", F32, "mm_down_dx")
    g_w_down = _mm(hmid, df, "tn", BF16, "mm_down_dw", tm=FFN // 4)
    d_gate, d_up, g_ffn_conv = _ffn_mid_bwd(dhmid, upall, ffn_conv_w)
    g_w_up = _mm(hn2, d_gate, "tn", BF16, "mm_up_dw", tn=2 * FFN // NDEV, b2=d_up)
    tok_a = comm.grads_begin("a", dict(ffn_w_down=g_w_down, ffn_w_up=g_w_up))
    dhn2 = _mm(d_gate, ffn_w_up, "nt", F32, "mm_up_dx", a2=d_up, dep=jnp.broadcast_to(tok_a, (8, LANE)))
    tok_a = comm.grads_continue("a", dhn2)
    dx2, dmix, acc2 = _norm_mod_bwd(dhn2, x2, dx3, _tie(norm2_g, tok_a), sc2, "norm_mod2_bwd", mix=mix, gt=gt1)
    d_sh2, d_sc2, d_norm2_g, d_gt1 = acc2[0:1], acc2[1:2], acc2[2:3], acc2[3:4]
    dmerged = _mm(dmix, w_out, "nt", F32, "mm_out_dx")
    g_w_out = _mm(merged, dmix, "tn", BF16, "mm_out_dw")
    d_proj, d_bra, d_brb = _merge_bwd(dmerged, br_a, br_b, proj, lax.empty((S, NINP), BF16))
    du3 = _mm(d_brb, cf_w_o, "nt", F32, "mm_cf_o_dx")
    g_cf_w_o = _mm(u3, d_brb, "tn", BF16, "mm_cf_o_dw")
    du1, acc_ln = _cf_ln_bwd(du3, u1, cf_ln_g, cf_ln_b)
    d_proj, g_cf_conv = _cf_conv_bwd(du1, proj, cf_conv_w, d_proj)
    dog = _mm(d_bra, dn_w_o, "nt", F32, "mm_dn_o_dx")
    g_dn_w_o = _mm(og, d_bra, "tn", BF16, "mm_dn_o_dw")
    tok_b = comm.grads_begin("b", dict(w_out=g_w_out, cf_w_o=g_cf_w_o, dn_w_o=g_dn_w_o, ffn_conv_w=g_ffn_conv,
                                       cf_conv_w=g_cf_conv))
    do, d_proj, acc_gn = _dn_post_bwd(dog, o, proj, _tie(dn_norm_g, tok_b), d_proj)
    tok_b = comm.grads_continue("b", do)
    dq, dk, dv, dgr, dbr = _delta_bwd(qkvn, gb, _tie(gr, tok_b), bb, st_all, tm_all, do)
    d_proj, g_dn_conv = _dn_pre_bwd(dq, dk, dv, proj, dn_conv_w, d_proj)
    dgates = jnp.concatenate([dbr.reshape(H, S).T, dgr.reshape(H, S).T, jnp.zeros((S, LANE - 2 * H), F32)], axis=1)
    d_proj, acc_g = _gates_bwd(dgates, proj, alog_v, dtb_v, d_proj)
    g_w_in_p = _mm(d_proj, hn1, "tn", BF16, "mm_in_dw", tm=1152)
    tok_c = comm.grads_begin("c", dict(w_in=g_w_in_p, dn_conv_w=g_dn_conv))
    dhn1 = _mm(d_proj, w_in_p, "nn", F32, "mm_in_dx", tk=NINP // 3, dep=jnp.broadcast_to(tok_c, (8, LANE)))
    grad_x, acc1 = _norm_mod_bwd(dhn1, x, dx2, norm1_g, sc1, "norm_mod1_bwd")
    d_sh1, d_sc1, d_norm1_g = acc1[0:1], acc1[1:2], acc1[2:3]

    d_mod = jnp.concatenate([d_sh1, d_sc1, d_gt1, d_sh2, d_sc2, d_gt2], axis=1)
    small = dict(mod=d_mod, norm1_g=d_norm1_g, norm2_g=d_norm2_g, final_norm_g=d_final_g,
                 cf_ln_g=acc_ln[0:1], cf_ln_b=acc_ln[1:2], dn_norm_g=acc_gn[0:1],
                 dn_a_log=acc_g[0:1, H:2 * H], dn_dt_bias=acc_g[1:2, H:2 * H])
    return loss, grad_x, small


def _dev_index(px, py, pc):
    return 4 * px + 2 * py + pc


def _all_gather(arrs, name):
    n = len(arrs)

    def body(*refs):
        ins, outs = refs[:n], refs[n:2 * n]
        send_sems, recv_sems, loc_sems = refs[2 * n:]
        x, y, c = _my_pos()
        me, sib = (x, y, c), (x, y, 1 - c)
        chips = [(1 - x, y), (x, 1 - y), (1 - x, 1 - y)]

        def cp(i, k, block, to, src=None):
            dst = outs[i].at[_dev_index(*block)]
            return pltpu.make_async_remote_copy(
                src_ref=dst if src is None else src, dst_ref=dst, send_sem=send_sems.at[i, k],
                recv_sem=recv_sems.at[i, k], device_id=to, device_id_type=MESH)

        mine = [pltpu.make_async_copy(ins[i], outs[i].at[_dev_index(*me)], loc_sems.at[i]) for i in range(n)]
        for m in mine:
            m.start()
        sent = []
        for i in range(n):
            sent.append(cp(i, 0, me, sib, src=ins[i]))
            sent += [cp(i, 1 + j, me, (*chip, c), src=ins[i]) for j, chip in enumerate(chips)]
        for s in sent:
            s.start()
        for i in range(n):
            for j, chip in enumerate(chips):
                cp(i, 1 + j, (*chip, c), me).wait_recv()
                fwd = cp(i, 4 + j, (*chip, c), sib)
                fwd.start()
                sent.append(fwd)
        for i in range(n):
            cp(i, 0, sib, me).wait_recv()
            for j, chip in enumerate(chips):
                cp(i, 4 + j, (*chip, 1 - c), me).wait_recv()
        for s in sent:
            s.wait_send()
        for m in mine:
            m.wait()

    outs = pl.pallas_call(
        body, out_shape=[_sds((NDEV,) + a.shape, a.dtype) for a in arrs], in_specs=[ANY] * n, out_specs=[ANY] * n,
        scratch_shapes=[pltpu.SemaphoreType.DMA((n, 7)), pltpu.SemaphoreType.DMA((n, 7)), pltpu.SemaphoreType.DMA((n,))],
        name=name)(*arrs)
    return list(outs)


def _slab(ref, layout, idx):
    kind, n = layout
    if kind == "rows":
        return ref.at[pl.ds(pl.multiple_of(idx * n, n), n), :]
    if kind == "cols":
        return ref.at[:, pl.ds(pl.multiple_of(idx * n, n), n)]
    return ref.at[idx]


def _slab_shape(arr, layout):
    kind, n = layout
    if kind == "rows":
        return (n, arr.shape[1])
    if kind == "cols":
        return (arr.shape[0], n)
    return tuple(arr.shape[1:])


HBM = pl.BlockSpec(memory_space=pltpu.HBM)
SEMS = pl.BlockSpec(memory_space=pltpu.SEMAPHORE)
EFFECT = pltpu.SideEffectType.DATAFLOW_SIDE_EFFECTING
TOKEN = jax.ShapeDtypeStruct((8, LANE), F32)


def _hbm(a):
    return pltpu.with_memory_space_constraint(a, pltpu.HBM)


def _gather_ici_copy(shard_ref, buf_ref, layout, send_sems, recv_sems, i, j, me, chip, c):
    return pltpu.make_async_remote_copy(
        src_ref=shard_ref, dst_ref=_slab(buf_ref, layout, me), send_sem=send_sems.at[3 * i + j],
        recv_sem=recv_sems.at[3 * i + j], device_id=(*chip, c), device_id_type=MESH)


def _gather_ici_start(shards, bufs, layouts, after, name):
    n = len(shards)

    def body(*refs):
        sh, bf = refs[:n], refs[n:2 * n]
        send_sems, recv_sems = refs[2 * n + 1], refs[2 * n + 2]
        token = refs[-1]
        x, y, c = _my_pos()
        me = _dev_index(x, y, c)
        for i in range(n):
            for j, chip in enumerate([(1 - x, y), (x, 1 - y), (1 - x, 1 - y)]):
                _gather_ici_copy(sh[i], bf[i], layouts[i], send_sems, recv_sems, i, j, me, chip, c).start()
        token[...] = jnp.zeros_like(token)

    outs = pl.pallas_call(
        body, name=name,
        out_shape=(pltpu.SemaphoreType.DMA((3 * n,)), pltpu.SemaphoreType.DMA((3 * n,)),
                   *[pltpu.HBM(a.shape, a.dtype) for a in shards], *[pltpu.HBM(a.shape, a.dtype) for a in bufs], TOKEN),
        in_specs=[HBM] * (2 * n) + [ANY],
        out_specs=(SEMS, SEMS, *[HBM] * (2 * n), pl.BlockSpec(memory_space=pltpu.VMEM)),
        input_output_aliases={i: 2 + i for i in range(2 * n)},
        compiler_params=pltpu.CompilerParams(has_side_effects=EFFECT),
    )(*[_hbm(a) for a in shards], *[_hbm(a) for a in bufs], after)
    return outs[0], outs[1], list(outs[2:2 + n]), list(outs[2 + n:2 + 2 * n]), outs[-1]


def _gather_ici_wait(send_sems, recv_sems, shards, bufs, layouts, after, name):
    n = len(shards)

    def body(*refs):
        sh, bf = refs[:n], refs[n:2 * n]
        ssem, rsem = refs[2 * n], refs[2 * n + 1]
        x, y, c = _my_pos()
        me = _dev_index(x, y, c)
        for i in range(n):
            for j, chip in enumerate([(1 - x, y), (x, 1 - y), (1 - x, 1 - y)]):
                cp = _gather_ici_copy(sh[i], bf[i], layouts[i], ssem, rsem, i, j, me, chip, c)
                cp.wait_send()
                cp.wait_recv()

    outs = pl.pallas_call(
        body, name=name,
        out_shape=(*[pltpu.HBM(a.shape, a.dtype) for a in shards], *[pltpu.HBM(a.shape, a.dtype) for a in bufs]),
        in_specs=[HBM] * (2 * n) + [SEMS, SEMS, ANY], out_specs=tuple([HBM] * (2 * n)),
        input_output_aliases={i: i for i in range(2 * n)},
        compiler_params=pltpu.CompilerParams(has_side_effects=EFFECT),
    )(*shards, *bufs, send_sems, recv_sems, after)
    return list(outs[:n]), list(outs[n:])


def _place_own(pos, shard, buf, layout, name):
    kind, n = layout
    r, cols = shard.shape
    tr = _row_tile(r, shard.dtype.itemsize)
    nr = r // tr
    if kind == "rows":
        ospec = pl.BlockSpec((tr, cols), lambda i, p: (p[2] * nr + i, 0))
    else:
        assert kind == "lead"
        ospec = pl.BlockSpec((None, tr, cols), lambda i, p: (p[2], i, 0))

    def body(pos_ref, s_ref, buf_in, o_ref):
        o_ref[...] = s_ref[...]

    return pl.pallas_call(
        body, out_shape=_sds(buf.shape, buf.dtype), name=name, input_output_aliases={2: 0},
        grid_spec=pltpu.PrefetchScalarGridSpec(
            num_scalar_prefetch=1, grid=(nr,), in_specs=[pl.BlockSpec((tr, cols), lambda i, p: (i, 0)), ANY],
            out_specs=ospec),
        compiler_params=pltpu.CompilerParams(dimension_semantics=("parallel",), vmem_limit_bytes=VMEM_LIMIT),
    )(pos, shard, buf)


def _gather_pair(shards, bufs, layouts, name):
    n = len(shards)

    def body(*refs):
        sh, bo = refs[:n], refs[2 * n:3 * n]
        send_sems, recv_sems = refs[3 * n:]
        x, y, c = _my_pos()
        sib = (x, y, 1 - c)
        copies = []
        for i in range(n):
            for k, (px, py) in enumerate([(x, y), (1 - x, y), (x, 1 - y), (1 - x, 1 - y)]):
                slab = _slab(bo[i], layouts[i], _dev_index(px, py, c))
                copies.append(pltpu.make_async_remote_copy(
                    src_ref=sh[i] if k == 0 else slab, dst_ref=slab, send_sem=send_sems.at[i, k],
                    recv_sem=recv_sems.at[i, k], device_id=sib, device_id_type=MESH))
        for cpy in copies:
            cpy.start()
        for cpy in copies:
            cpy.wait()

    outs = pl.pallas_call(
        body, out_shape=[_sds(a.shape, a.dtype) for a in bufs], in_specs=[ANY] * (2 * n), out_specs=[ANY] * n,
        input_output_aliases={n + i: i for i in range(n)},
        scratch_shapes=[pltpu.SemaphoreType.DMA((n, 4)), pltpu.SemaphoreType.DMA((n, 4))], name=name)(*shards, *bufs)
    return list(outs)


def _pair_copy(part_ref, land_ref, layout, send_sems, recv_sems, i, q, x, y, c):
    return pltpu.make_async_remote_copy(
        src_ref=_slab(part_ref, layout, 2 * q + (1 - c)), dst_ref=land_ref.at[q], send_sem=send_sems.at[4 * i + q],
        recv_sem=recv_sems.at[4 * i + q], device_id=(x, y, 1 - c), device_id_type=MESH)


def _pair_exchange_start(parts, layouts, name):
    n = len(parts)
    lands = [lax.empty((4,) + _slab_shape(p, lay), p.dtype) for p, lay in zip(parts, layouts)]

    def body(*refs):
        pt, ld = refs[:n], refs[n:2 * n]
        send_sems, recv_sems = refs[2 * n], refs[2 * n + 1]
        token = refs[-1]
        x, y, c = _my_pos()
        for i in range(n):
            for q in range(4):
                _pair_copy(pt[i], ld[i], layouts[i], send_sems, recv_sems, i, q, x, y, c).start()
        token[...] = jnp.zeros_like(token)

    outs = pl.pallas_call(
        body, name=name,
        out_shape=(pltpu.SemaphoreType.DMA((4 * n,)), pltpu.SemaphoreType.DMA((4 * n,)),
                   *[pltpu.HBM(a.shape, a.dtype) for a in parts], *[pltpu.HBM(a.shape, a.dtype) for a in lands], TOKEN),
        in_specs=[HBM] * (2 * n), out_specs=(SEMS, SEMS, *[HBM] * (2 * n), pl.BlockSpec(memory_space=pltpu.VMEM)),
        input_output_aliases={i: 2 + i for i in range(2 * n)},
        compiler_params=pltpu.CompilerParams(has_side_effects=EFFECT),
    )(*[_hbm(a) for a in parts], *[_hbm(a) for a in lands])
    return outs[0], outs[1], list(outs[2:2 + n]), list(outs[2 + n:2 + 2 * n]), outs[-1]


def _pair_exchange_wait(send_sems, recv_sems, parts, lands, layouts, after, name):
    n = len(parts)

    def body(*refs):
        pt, ld = refs[:n], refs[n:2 * n]
        ssem, rsem = refs[2 * n], refs[2 * n + 1]
        x, y, c = _my_pos()
        for i in range(n):
            for q in range(4):
                cp = _pair_copy(pt[i], ld[i], layouts[i], ssem, rsem, i, q, x, y, c)
                cp.wait_send()
                cp.wait_recv()

    outs = pl.pallas_call(
        body, name=name,
        out_shape=(*[pltpu.HBM(a.shape, a.dtype) for a in parts], *[pltpu.HBM(a.shape, a.dtype) for a in lands]),
        in_specs=[HBM] * (2 * n) + [SEMS, SEMS, ANY], out_specs=tuple([HBM] * (2 * n)),
        input_output_aliases={i: i for i in range(2 * n)},
        compiler_params=pltpu.CompilerParams(has_side_effects=EFFECT),
    )(*parts, *lands, send_sems, recv_sems, after)
    return list(outs[:n]), list(outs[n:])


def _chip_copy(sum_ref, land_ref, send_sems, recv_sems, i, j, chip, c):
    return pltpu.make_async_remote_copy(
        src_ref=sum_ref.at[2 * chip[0] + chip[1]], dst_ref=land_ref.at[j], send_sem=send_sems.at[3 * i + j],
        recv_sem=recv_sems.at[3 * i + j], device_id=(*chip, c), device_id_type=MESH)


def _chip_exchange_start(sums, name):
    n = len(sums)
    lands = [lax.empty((3,) + s.shape[1:], s.dtype) for s in sums]

    def body(*refs):
        sm, ld = refs[:n], refs[n:2 * n]
        send_sems, recv_sems = refs[2 * n], refs[2 * n + 1]
        token = refs[-1]
        x, y, c = _my_pos()
        for i in range(n):
            for j, chip in enumerate([(1 - x, y), (x, 1 - y), (1 - x, 1 - y)]):
                _chip_copy(sm[i], ld[i], send_sems, recv_sems, i, j, chip, c).start()
        token[...] = jnp.zeros_like(token)

    outs = pl.pallas_call(
        body, name=name,
        out_shape=(pltpu.SemaphoreType.DMA((3 * n,)), pltpu.SemaphoreType.DMA((3 * n,)),
                   *[pltpu.HBM(a.shape, a.dtype) for a in sums], *[pltpu.HBM(a.shape, a.dtype) for a in lands], TOKEN),
        in_specs=[HBM] * (2 * n), out_specs=(SEMS, SEMS, *[HBM] * (2 * n), pl.BlockSpec(memory_space=pltpu.VMEM)),
        input_output_aliases={i: 2 + i for i in range(2 * n)},
        compiler_params=pltpu.CompilerParams(has_side_effects=EFFECT),
    )(*[_hbm(a) for a in sums], *[_hbm(a) for a in lands])
    return outs[0], outs[1], list(outs[2:2 + n]), list(outs[2 + n:2 + 2 * n]), outs[-1]


def _chip_exchange_wait(send_sems, recv_sems, sums, lands, after, name):
    n = len(sums)

    def body(*refs):
        sm, ld = refs[:n], refs[n:2 * n]
        ssem, rsem = refs[2 * n], refs[2 * n + 1]
        x, y, c = _my_pos()
        for i in range(n):
            for j, chip in enumerate([(1 - x, y), (x, 1 - y), (1 - x, 1 - y)]):
                cp = _chip_copy(sm[i], ld[i], ssem, rsem, i, j, chip, c)
                cp.wait_send()
                cp.wait_recv()

    outs = pl.pallas_call(
        body, name=name,
        out_shape=(*[pltpu.HBM(a.shape, a.dtype) for a in sums], *[pltpu.HBM(a.shape, a.dtype) for a in lands]),
        in_specs=[HBM] * (2 * n) + [SEMS, SEMS, ANY], out_specs=tuple([HBM] * (2 * n)),
        input_output_aliases={i: i for i in range(2 * n)},
        compiler_params=pltpu.CompilerParams(has_side_effects=EFFECT),
    )(*sums, *lands, send_sems, recv_sems, after)
    return list(outs[:n]), list(outs[n:])


def _row_tile(r, itemsize):
    align = 32 // itemsize
    best = r
    for t in range(align, min(r, 256) + 1, align):
        if r % t == 0:
            best = t
    return best


def _prefetch_call(body, name, out_shape, grid, in_specs, out_specs, sem):
    return pl.pallas_call(
        body, out_shape=out_shape, name=name,
        grid_spec=pltpu.PrefetchScalarGridSpec(num_scalar_prefetch=1, grid=grid, in_specs=in_specs, out_specs=out_specs),
        compiler_params=pltpu.CompilerParams(dimension_semantics=sem, vmem_limit_bytes=VMEM_LIMIT))


def _pair_sum(pos, part, got, layout, name):
    kind, _ = layout
    _, r, cols = got.shape
    tr, tc = _tiles(r, cols, part.dtype.itemsize)
    nr, nc = r // tr, cols // tc
    if kind == "rows":
        pspec = pl.BlockSpec((tr, tc), lambda q, i, j, p: ((2 * q + p[0]) * nr + i, j))
    elif kind == "cols":
        pspec = pl.BlockSpec((tr, tc), lambda q, i, j, p: (i, (2 * q + p[0]) * nc + j))
    else:
        pspec = pl.BlockSpec((None, tr, tc), lambda q, i, j, p: (2 * q + p[0], i, j))

    def body(pos_ref, p_ref, g_ref, o_ref):
        o_ref[...] = (p_ref[...].astype(F32) + g_ref[...].astype(F32)).astype(o_ref.dtype)

    blk = pl.BlockSpec((None, tr, tc), lambda q, i, j, p: (q, i, j))
    return _prefetch_call(body, name, _sds((4, r, cols), part.dtype), (4, nr, nc), [pspec, blk], blk,
                          ("parallel", "parallel", "parallel"))(pos, part, got)


def _tiles(r, cols, itemsize):
    tr = _row_tile(r, itemsize)
    if tr < r or r * cols * 4 <= (2 << 20) or cols % 256:
        return tr, cols
    return r, 256


def _final_sum_adam(pos, sums, got, w, m, v, name):
    _, r, cols = w.shape
    tr, tc = _tiles(r, cols, sums.dtype.itemsize)

    def body(pos_ref, s_ref, g_ref, w_ref, m_ref, v_ref, go_ref, dl_ref, nm_ref, nv_ref):
        g = ((s_ref[...].astype(F32) + g_ref[0].astype(F32)) + g_ref[1].astype(F32)) + g_ref[2].astype(F32)
        dl, nm, nv = _adam(w_ref[...], g, m_ref[...], v_ref[...])
        go_ref[...] = g
        dl_ref[...] = dl
        nm_ref[...] = nm
        nv_ref[...] = nv

    big = pl.BlockSpec((None, tr, tc), lambda i, j, p: (0, i, j))
    return _prefetch_call(body, name, [_sds((1, r, cols))] * 4, (r // tr, cols // tc),
                          [pl.BlockSpec((None, tr, tc), lambda i, j, p: (p[1], i, j)),
                           pl.BlockSpec((3, tr, tc), lambda i, j, p: (0, i, j)), big, big, big],
                          [big] * 4, ("parallel", "parallel"))(pos, sums, got, w, m, v)


def _small_adam(g_all, w, m, v):
    npk = w.shape[1]

    def body(g_ref, w_ref, m_ref, v_ref, go_ref, dl_ref, nm_ref, nv_ref):
        g = g_ref[0:1, :]
        for k in range(1, NDEV):
            g = g + g_ref[k:k + 1, :]
        dl, nm, nv = _adam(w_ref[...], g, m_ref[...], v_ref[...])
        go_ref[...] = g
        dl_ref[...] = dl
        nm_ref[...] = nm
        nv_ref[...] = nv

    return _call(body, "small_adam", [_sds((1, npk))] * 4)(g_all, w, m, v)


SMALL = [("b_ada", 6 * D), ("norm1_g", D), ("norm2_g", D), ("final_norm_g", D), ("cf_ln_g", CFW), ("cf_ln_b", CFW),
         ("dn_norm_g", DH), ("dn_a_log", H), ("dn_dt_bias", H)]
LATE = ["dn_w_o", "cf_w_o", "w_out", "ffn_w_up", "ffn_w_down"]
LATE_SHAPE = {"dn_w_o": (NDEV, DNW, D // NDEV), "cf_w_o": (NDEV, CFW, D // NDEV), "w_out": (D, D),
              "ffn_w_up": (NDEV, D, 2 * FFN // NDEV), "ffn_w_down": (FFN, D)}
LATE_LAYOUT = {"dn_w_o": ("lead", NDEV), "cf_w_o": ("lead", NDEV), "w_out": ("rows", D // NDEV),
               "ffn_w_up": ("lead", NDEV), "ffn_w_down": ("rows", FFN // NDEV)}
LAYOUT = {"dn_w_o": ("cols", D // NDEV), "cf_w_o": ("cols", D // NDEV), "w_out": ("rows", D // NDEV),
          "ffn_w_up": ("cols", 2 * FFN // NDEV), "ffn_w_down": ("rows", FFN // NDEV),
          "w_in": ("lead", NDEV), "dn_conv_w": ("lead", NDEV), "cf_conv_w": ("lead", NDEV), "ffn_conv_w": ("lead", NDEV)}
NAMES = ["w_ada", "b_ada", "norm1_g", "w_in", "dn_conv_w", "dn_a_log", "dn_dt_bias", "dn_norm_g", "dn_w_o", "cf_conv_w",
         "cf_ln_g", "cf_ln_b", "cf_w_o", "w_out", "norm2_g", "ffn_w_up", "ffn_conv_w", "ffn_w_down", "final_norm_g"]


def _pack_small(d):
    rows = []
    for nm, n in SMALL:
        row = d[nm].reshape(1, n)
        pad = (-n) % LANE
        rows.append(jnp.pad(row, ((0, 0), (0, pad))) if pad else row)
    return jnp.concatenate(rows, axis=1)


def _unpack_small(row, shapes):
    out, off = {}, 0
    for nm, n in SMALL:
        out[nm] = row[0, off:off + n].reshape(shapes[nm])
        off += n + ((-n) % LANE)
    return out


def _cols_from_gathered(g):
    return jnp.transpose(g, (1, 0, 2)).reshape(g.shape[1], NDEV * g.shape[2])


def _cols_to_parts(full):
    r, ctot = full.shape
    return jnp.transpose(full.reshape(r, NDEV, ctot // NDEV), (1, 0, 2))


def kernel(x, c, w_ada, b_ada, norm1_g, w_in, dn_conv_w, dn_a_log, dn_dt_bias, dn_norm_g, dn_w_o, cf_conv_w, cf_ln_g, cf_ln_b, cf_w_o, w_out, norm2_g, ffn_w_up, ffn_conv_w, ffn_w_down, final_norm_g, loss_target, m_w_ada, m_b_ada, m_norm1_g, m_w_in, m_dn_conv_w, m_dn_a_log, m_dn_dt_bias, m_dn_norm_g, m_dn_w_o, m_cf_conv_w, m_cf_ln_g, m_cf_ln_b, m_cf_w_o, m_w_out, m_norm2_g, m_ffn_w_up, m_ffn_conv_w, m_ffn_w_down, m_final_norm_g, v_w_ada, v_b_ada, v_norm1_g, v_w_in, v_dn_conv_w, v_dn_a_log, v_dn_dt_bias, v_dn_norm_g, v_dn_w_o, v_cf_conv_w, v_cf_ln_g, v_cf_ln_b, v_cf_w_o, v_w_out, v_norm2_g, v_ffn_w_up, v_ffn_conv_w, v_ffn_w_down, v_final_norm_g):
    args = locals()
    w = {nm: args[nm] for nm in NAMES}
    mo = {nm: args["m_" + nm] for nm in NAMES}
    vo = {nm: args["v_" + nm] for nm in NAMES}
    shapes = {nm: w[nm].shape for nm in NAMES}
    px, py, pc = _my_pos()
    me = _dev_index(px, py, pc)

    def mat(a):
        return a.reshape(a.shape[-2:])

    pos = jnp.stack([pc, 2 * px + py, me]).astype(jnp.int32)

    first = ["w_in", "dn_conv_w", "cf_conv_w", "ffn_conv_w"]
    tr_in = lambda a: jnp.transpose(a, (0, 2, 1))
    got = _all_gather([tr_in(w["w_in"]).astype(BF16)] + [mat(w[nm]) for nm in first[1:]] + [c], "gather_first")
    full = {nm: _cols_from_gathered(g) for nm, g in zip(first[1:], got[1:-1])}
    c_all = got[-1].reshape(NDEV, D)
    w_in_p = _pad_win(got[0].reshape(NIN, D))

    ncol = 6 * D // NDEV
    b_sh = lax.dynamic_slice(b_ada.reshape(1, 6 * D), (0, me * ncol), (1, ncol))
    mod_sh = _ada_fwd(c_all, mat(w_ada), b_sh)
    mod_all = _all_gather([mod_sh], "gather_mod")[0]
    mod = lax.dynamic_index_in_dim(mod_all, me, axis=1, keepdims=False).reshape(1, 6 * D)

    late = {}
    dep = mod_all
    for grp, names in (("mix", LATE[:3]), ("ffn", LATE[3:])):
        shards = [mat(w[nm]).astype(BF16) for nm in names]
        lays = [LATE_LAYOUT[nm] for nm in names]
        bufs = [_place_own(pos, s, lax.empty(LATE_SHAPE[nm], BF16), lay, "place_" + nm)
                for nm, s, lay in zip(names, shards, lays)]
        started = _gather_ici_start(shards, bufs, lays, dep, "gather_" + grp + "_start")
        late[grp] = (lays,) + tuple(started)
        dep = started[4]

    res = {}

    class Comm:
        token0 = late["ffn"][5][0, 0]
        pending = {}

        @staticmethod
        def late_weights(grp, after):
            lays, ssem, rsem, shards, bufs, _ = late[grp]
            shards, bufs = _gather_ici_wait(ssem, rsem, shards, bufs, lays, after, "gather_" + grp + "_wait")
            return _gather_pair(shards, bufs, lays, "gather_" + grp + "_pair")

        @staticmethod
        def grads_begin(group, gd):
            names = list(gd)
            lays = [LAYOUT[nm] for nm in names]
            gl = []
            for nm in names:
                if nm == "w_in":
                    gl.append(_unpad_win(gd[nm]).reshape(NDEV, NSH, D))
                else:
                    gl.append(_cols_to_parts(gd[nm]) if LAYOUT[nm][0] == "lead" else gd[nm])
            started = _pair_exchange_start(gl, lays, "rs_pair_start_" + group)
            Comm.pending[group] = (names, lays) + tuple(started[:4])
            return started[4][0, 0]

        @staticmethod
        def grads_continue(group, after):
            names, lays, ssem, rsem, gl, lands = Comm.pending[group]
            gl, from_sib = _pair_exchange_wait(ssem, rsem, gl, lands, lays, after, "rs_pair_wait_" + group)
            sums = [_pair_sum(pos, g, r, lay, "rs_pair_sum_" + nm) for nm, g, r, lay in zip(names, gl, from_sib, lays)]
            started = _chip_exchange_start(sums, "rs_chips_start_" + group)
            Comm.pending[group] = (names,) + tuple(started[:4])
            return started[4][0, 0]

        @staticmethod
        def finish(group, after):
            names, ssem, rsem, sums, lands = Comm.pending[group]
            sums, lands = _chip_exchange_wait(ssem, rsem, sums, lands, after, "rs_chips_wait_" + group)
            for nm, s, r in zip(names, sums, lands):
                if nm == "w_in":
                    outs = _final_sum_adam(pos, s, r, tr_in(w[nm]), tr_in(mo[nm]), tr_in(vo[nm]), "adam_" + nm)
                    res[nm] = [tr_in(o) for o in outs]
                else:
                    res[nm] = _final_sum_adam(pos, s, r, w[nm], mo[nm], vo[nm], "adam_" + nm)
            return res[names[-1]][0]

    vec = lambda a: a.reshape(1, -1)
    loss, grad_x, small = _local_step(
        x.reshape(S, D), loss_target.reshape(S, D), mod, vec(norm1_g), vec(norm2_g), vec(final_norm_g), w_in_p,
        full["dn_conv_w"], vec(dn_a_log), vec(dn_dt_bias), vec(dn_norm_g), full["cf_conv_w"], vec(cf_ln_g),
        vec(cf_ln_b), full["ffn_conv_w"], Comm)

    small["b_ada"] = small.pop("mod")
    g_small = _all_gather([_pack_small(small)], "gather_small")[0].reshape(NDEV, -1)

    tok_c = Comm.grads_continue("c", g_small)
    done_a = Comm.finish("a", jnp.broadcast_to(tok_c, (8, LANE)))
    done_b = Comm.finish("b", done_a)
    outs = _small_adam(g_small, _pack_small({nm: w[nm] for nm, _ in SMALL}), _pack_small({nm: mo[nm] for nm, _ in SMALL}),
                       _pack_small({nm: vo[nm] for nm, _ in SMALL}))
    unpacked = [_unpack_small(o, shapes) for o in outs]
    for nm, _ in SMALL:
        res[nm] = [u[nm] for u in unpacked]

    dmod_sel = lax.dynamic_slice(g_small[:, :6 * D], (0, me * ncol), (NDEV, ncol))
    outs = _ada_bwd_adam(c_all, dmod_sel, mat(w_ada), mat(m_w_ada), mat(v_w_ada))
    res["w_ada"] = [o.reshape(shapes["w_ada"]) for o in outs]
    Comm.finish("c", jnp.concatenate([done_b.reshape(-1)[:LANE], outs[0].reshape(-1)[:LANE]]))

    loss = lax.psum(loss.reshape(()), ("x", "y", "c"))
    out = [loss, grad_x.reshape(x.shape)]
    for k in range(4):
        out += [res[nm][k] for nm in NAMES]
    return tuple(out)
```

```python
import functools

import jax
import jax.numpy as jnp
from jax import lax
from jax.experimental import pallas as pl
from jax.experimental.pallas import tpu as pltpu

F32 = jnp.float32
BF16 = jnp.bfloat16
HI = lax.Precision.HIGHEST
MESH = pl.DeviceIdType.MESH
ANY = pl.BlockSpec(memory_space=pl.ANY)

NDEV = 8
D = 2048
S = 2048
H = 8
DH = 128
DNW = H * DH
CFW = 1024
CFK = 31
DNK = 4
FFN = 5632
FFK = 3
CH = 64
NCH = S // CH
EPS = 1e-6
NIN = 10256
NINP = 10368
O_Z, O_GA, O_GB, O_GLU, O_SM = 3072, 4096, 6144, 8192, 10240
LANE = 128
TS = 256
VMEM_LIMIT = 56 * 1024 * 1024

ADAM_LR, ADAM_B1, ADAM_B2, ADAM_EPS, ADAM_WD, ADAM_STEP = 0.001, 0.9, 0.999, 1e-08, 0.01, 10


def _call(body, name, out_shape, grid=(), in_specs=None, out_specs=None, scratch=(), sem=None, aliases=None):
    kw = {}
    if aliases:
        kw["input_output_aliases"] = aliases
    if in_specs is not None:
        kw["in_specs"] = in_specs
    if out_specs is not None:
        kw["out_specs"] = out_specs
    return pl.pallas_call(
        body, out_shape=out_shape, grid=grid, scratch_shapes=scratch, name=name,
        compiler_params=pltpu.CompilerParams(dimension_semantics=sem, vmem_limit_bytes=VMEM_LIMIT), **kw)


def _sds(shape, dtype=F32):
    return jax.ShapeDtypeStruct(shape, dtype)


def _tile(dim, pref):
    if dim <= pref:
        return dim
    best = None
    for t in range(LANE, pref + 1, LANE):
        if dim % t == 0:
            best = t
    assert best is not None, (dim, pref)
    return best


def _sigmoid(x):
    return 1.0 / (1.0 + jnp.exp(-x))


def _silu(x):
    return x * _sigmoid(x)


def _dsilu(x):
    s = _sigmoid(x)
    return s * (1.0 + x * (1.0 - s))


def _silu_both(x):
    s = _sigmoid(x)
    return x * s, s * (1.0 + x * (1.0 - s))


def _softplus(x):
    return jnp.maximum(x, 0.0) + jnp.log(1.0 + jnp.exp(-jnp.abs(x)))


def _dot(a, b, dims, precision=None):
    return lax.dot_general(a, b, (dims, ((), ())), preferred_element_type=F32, precision=precision)


NN = ((1,), (0,))
NT = ((1,), (1,))
TN = ((0,), (0,))


def _my_pos():
    return lax.axis_index("x"), lax.axis_index("y"), lax.axis_index("c")


def _mm(a, b, mode, out_dtype, name, tm=1024, tn=1024, tk=2048, a2=None, b2=None, dep=None):
    sharded = b.ndim == 3
    if sharded and mode == "nn":
        cs = b.shape[2]
        (m, k), n = a.shape, NDEV * cs
        gs = max(1, tn // cs)
        tm, tn, tk = _tile(m, tm), gs * cs, _tile(k, tk)
    elif sharded:
        assert mode == "nt"
        cs = b.shape[2]
        m, n, k = a.shape[0], b.shape[1], NDEV * cs
        gs = max(1, tk // cs)
        tm, tn, tk = _tile(m, tm), _tile(n, tn), gs * cs
    else:
        if mode == "nn":
            (m, k), (k2, n) = a.shape, b.shape
        elif mode == "nt":
            (m, k), (n, k2) = a.shape, b.shape
        else:
            (k, m), (k2, n) = a.shape, b.shape
        assert k == k2, (a.shape, b.shape, mode)
        n = n * (2 if b2 is not None else 1)
        tm, tn, tk = _tile(m, tm), _tile(n // (2 if b2 is not None else 1), tn), _tile(k, tk)
    nk, nj = k // tk, n // tn
    halfk, halfj = nk // 2, nj // 2
    dims = {"nn": NN, "nt": NT, "tn": TN}[mode]

    n_in = 2 + (a2 is not None) + (b2 is not None) + (dep is not None)

    def body(*refs):
        a_ref, b_ref = refs[0], refs[1]
        x_ref = refs[2] if (a2 is not None or b2 is not None) else None
        o_ref = refs[n_in]
        acc_ref = refs[n_in + 1] if nk > 1 else None
        j, kk = pl.program_id(1), pl.program_id(2)

        if nk > 1:
            @pl.when(kk == 0)
            def _():
                acc_ref[...] = jnp.zeros_like(acc_ref)

        def accumulate(product, cols=slice(None)):
            if nk == 1:
                o_ref[:, cols] = product().astype(o_ref.dtype)
            else:
                acc_ref[:, cols] += product()

        if sharded and mode == "nn":
            for q in range(gs):
                accumulate(lambda q=q: _dot(a_ref[...], b_ref[q], NN), slice(q * cs, (q + 1) * cs))
        elif sharded:
            def contract(lhs_ref):
                def product():
                    part = None
                    for q in range(gs):
                        term = _dot(lhs_ref[:, q * cs:(q + 1) * cs], b_ref[q], NT)
                        part = term if part is None else part + term
                    return part
                accumulate(product)

            if a2 is None:
                contract(a_ref)
            else:
                pl.when(kk < halfk)(lambda: contract(a_ref))
                pl.when(kk >= halfk)(lambda: contract(x_ref))
        elif b2 is not None:
            pl.when(j < halfj)(lambda: accumulate(lambda: _dot(a_ref[...], b_ref[...], dims)))
            pl.when(j >= halfj)(lambda: accumulate(lambda: _dot(a_ref[...], x_ref[...], dims)))
        else:
            accumulate(lambda: _dot(a_ref[...], b_ref[...], dims))

        if nk > 1:
            @pl.when(kk == nk - 1)
            def _():
                o_ref[...] = acc_ref[...].astype(o_ref.dtype)

    ins, in_specs = [a], []
    if mode == "tn":
        in_specs.append(pl.BlockSpec((tk, tm), lambda i, j, kk: (kk, i)))
    elif a2 is not None:
        in_specs.append(pl.BlockSpec((tm, tk), lambda i, j, kk: (i, jnp.minimum(kk, halfk - 1))))
    else:
        in_specs.append(pl.BlockSpec((tm, tk), lambda i, j, kk: (i, kk)))
    ins.append(b)
    if sharded and mode == "nn":
        in_specs.append(pl.BlockSpec((gs, tk, cs), lambda i, j, kk: (j, kk, 0)))
    elif sharded:
        in_specs.append(pl.BlockSpec((gs, tn, cs), lambda i, j, kk: (kk, j, 0)))
    elif mode == "nt":
        in_specs.append(pl.BlockSpec((tn, tk), lambda i, j, kk: (j, kk)))
    elif b2 is not None:
        in_specs.append(pl.BlockSpec((tk, tn), lambda i, j, kk: (kk, jnp.minimum(j, halfj - 1))))
    else:
        in_specs.append(pl.BlockSpec((tk, tn), lambda i, j, kk: (kk, j)))
    if a2 is not None:
        ins.append(a2)
        in_specs.append(pl.BlockSpec((tm, tk), lambda i, j, kk: (i, jnp.maximum(kk - halfk, 0))))
    if b2 is not None:
        ins.append(b2)
        in_specs.append(pl.BlockSpec((tk, tn), lambda i, j, kk: (kk, jnp.maximum(j - halfj, 0))))
    if dep is not None:
        ins.append(dep)
        in_specs.append(ANY)
    return _call(body, name, _sds((m, n), out_dtype), grid=(m // tm, nj, nk),
                 in_specs=in_specs, out_specs=pl.BlockSpec((tm, tn), lambda i, j, kk: (i, j)),
                 scratch=[pltpu.VMEM((tm, tn), F32)] if nk > 1 else [],
                 sem=("parallel", "parallel", "arbitrary"))(*ins)


def _ada_fwd(c_all, w_sh, b_sh):
    n = w_sh.shape[1]
    tn = 512

    def body(c_ref, w_ref, b_ref, o_ref):
        ca = _silu(c_ref[...]).astype(BF16)
        o_ref[...] = _dot(ca, w_ref[...].astype(BF16), NN) + b_ref[...]

    return _call(body, "ada_fwd", _sds((NDEV, n)), grid=(n // tn,),
                 in_specs=[pl.BlockSpec((NDEV, D), lambda j: (0, 0)), pl.BlockSpec((D, tn), lambda j: (0, j)),
                           pl.BlockSpec((1, tn), lambda j: (0, j))],
                 out_specs=pl.BlockSpec((NDEV, tn), lambda j: (0, j)), sem=("parallel",))(c_all, w_sh, b_sh)


def _adam(w, g, m, v):
    m = ADAM_B1 * m + (1.0 - ADAM_B1) * g
    v = ADAM_B2 * v + (1.0 - ADAM_B2) * (g * g)
    m_hat = m / (1.0 - ADAM_B1 ** ADAM_STEP)
    v_hat = v / (1.0 - ADAM_B2 ** ADAM_STEP)
    delta = -ADAM_LR * (m_hat / (jnp.sqrt(v_hat) + ADAM_EPS) + ADAM_WD * w)
    return delta, m, v


def _ada_bwd_adam(c_all, dmod_sel, w, m, v):
    r, n = w.shape
    tr = 256

    def body(c_ref, d_ref, w_ref, m_ref, v_ref, g_ref, dl_ref, nm_ref, nv_ref):
        ca = _silu(c_ref[...])
        g = _dot(ca, d_ref[...], TN, precision=HI)
        dl, nm, nv = _adam(w_ref[...], g, m_ref[...], v_ref[...])
        g_ref[...] = g
        dl_ref[...] = dl
        nm_ref[...] = nm
        nv_ref[...] = nv

    big = pl.BlockSpec((tr, n), lambda i: (i, 0))
    return _call(body, "ada_bwd_adam", [_sds((r, n))] * 4, grid=(r // tr,),
                 in_specs=[pl.BlockSpec((NDEV, tr), lambda i: (0, i)), pl.BlockSpec((NDEV, n), lambda i: (0, 0)),
                           big, big, big],
                 out_specs=[big] * 4, sem=("parallel",))(c_all, dmod_sel, w, m, v)


def _row_spec(width=D):
    return pl.BlockSpec((TS, width), lambda i: (i, 0))


def _vec_spec(width=D):
    return pl.BlockSpec((1, width), lambda i: (0, 0))


def _acc_spec(width=D):
    return pl.BlockSpec((8, width), lambda i: (0, 0))


def _norm_mod(x, g, sc, sh, name):
    def body(x_ref, g_ref, sc_ref, sh_ref, o_ref):
        xv = x_ref[...]
        r = lax.rsqrt(jnp.mean(xv * xv, axis=-1, keepdims=True) + EPS)
        o_ref[...] = ((xv * r) * g_ref[...] * (1.0 + sc_ref[...]) + sh_ref[...]).astype(BF16)

    return _call(body, name, _sds((S, D), BF16), grid=(S // TS,),
                 in_specs=[_row_spec(), _vec_spec(), _vec_spec(), _vec_spec()], out_specs=_row_spec(),
                 sem=("parallel",))(x, g, sc, sh)


def _resid_norm_mod(x, mix, gt, g, sc, sh, name):
    def body(x_ref, mix_ref, gt_ref, g_ref, sc_ref, sh_ref, x2_ref, o_ref):
        xv = x_ref[...] + gt_ref[...] * mix_ref[...]
        x2_ref[...] = xv
        r = lax.rsqrt(jnp.mean(xv * xv, axis=-1, keepdims=True) + EPS)
        o_ref[...] = ((xv * r) * g_ref[...] * (1.0 + sc_ref[...]) + sh_ref[...]).astype(BF16)

    return _call(body, name, [_sds((S, D)), _sds((S, D), BF16)], grid=(S // TS,),
                 in_specs=[_row_spec(), _row_spec()] + [_vec_spec()] * 4, out_specs=[_row_spec(), _row_spec()],
                 sem=("parallel",))(x, mix, gt, g, sc, sh)


def _acc_rows(acc_ref, rows):
    @pl.when(pl.program_id(0) == 0)
    def _():
        acc_ref[...] = jnp.zeros_like(acc_ref)

    for k, row in enumerate(rows):
        acc_ref[k:k + 1, :] += row


def _loss_head(x2, f, tgt, gt2, gf):
    def body(x2_ref, f_ref, t_ref, gt_ref, gf_ref, dx_ref, df_ref, acc_ref):
        fv = f_ref[...]
        x3 = x2_ref[...] + gt_ref[...] * fv
        r = lax.rsqrt(jnp.mean(x3 * x3, axis=-1, keepdims=True) + EPS)
        xn = x3 * r
        e = xn * gf_ref[...] - t_ref[...]
        loss = 0.5 * jnp.sum(jnp.mean(e * e, axis=-1, keepdims=True), axis=0, keepdims=True)
        dy = e * (1.0 / D)
        dxn = dy * gf_ref[...]
        dx3 = r * (dxn - xn * jnp.mean(dxn * xn, axis=-1, keepdims=True))
        dx_ref[...] = dx3
        df_ref[...] = (dx3 * gt_ref[...]).astype(BF16)
        _acc_rows(acc_ref, [jnp.sum(dy * xn, axis=0, keepdims=True), jnp.sum(dx3 * fv, axis=0, keepdims=True),
                            jnp.broadcast_to(loss, (1, D))])

    return _call(body, "loss_head", [_sds((S, D)), _sds((S, D), BF16), _sds((8, D))], grid=(S // TS,),
                 in_specs=[_row_spec(), _row_spec(), _row_spec(), _vec_spec(), _vec_spec()],
                 out_specs=[_row_spec(), _row_spec(), _acc_spec()], sem=("arbitrary",))(x2, f, tgt, gt2, gf)


def _norm_mod_bwd(dhn, x, dres, g, sc, name, mix=None, gt=None):
    gated = mix is not None

    def body(*refs):
        if gated:
            dhn_ref, x_ref, dres_ref, g_ref, sc_ref, mix_ref, gt_ref, dx_ref, dmix_ref, acc_ref = refs
        else:
            dhn_ref, x_ref, dres_ref, g_ref, sc_ref, dx_ref, acc_ref = refs
        xv = x_ref[...]
        dh = dhn_ref[...]
        r = lax.rsqrt(jnp.mean(xv * xv, axis=-1, keepdims=True) + EPS)
        xn = xv * r
        gv = g_ref[...]
        sc1 = 1.0 + sc_ref[...]
        dxn = dh * gv * sc1
        dx = dres_ref[...] + r * (dxn - xn * jnp.mean(dxn * xn, axis=-1, keepdims=True))
        dx_ref[...] = dx
        rows = [jnp.sum(dh, axis=0, keepdims=True), jnp.sum(dh * xn * gv, axis=0, keepdims=True),
                jnp.sum(dh * xn * sc1, axis=0, keepdims=True)]
        if gated:
            rows.append(jnp.sum(dx * mix_ref[...], axis=0, keepdims=True))
            dmix_ref[...] = (dx * gt_ref[...]).astype(BF16)
        _acc_rows(acc_ref, rows)

    ins = [dhn, x, dres, g, sc]
    in_specs = [_row_spec(), _row_spec(), _row_spec(), _vec_spec(), _vec_spec()]
    outs = [_sds((S, D))]
    out_specs = [_row_spec()]
    if gated:
        ins += [mix, gt]
        in_specs += [_row_spec(), _vec_spec()]
        outs.append(_sds((S, D), BF16))
        out_specs.append(_row_spec())
    outs.append(_sds((8, D)))
    out_specs.append(_acc_spec())
    return _call(body, name, outs, grid=(S // TS,), in_specs=in_specs, out_specs=out_specs,
                 sem=("arbitrary",))(*ins)


RC = 64


RC_WIDE = 256


def _conv_fwd_rows(pad_ref, w_ref, kw, head, r0, rc=RC):
    acc = None
    for k in range(kw):
        term = w_ref[k:k + 1, :] * pad_ref[pl.ds(head - (kw - 1) + k + r0, rc), :]
        acc = term if acc is None else acc + term
    return acc


def _conv_bwd_rows(pad2_ref, w_ref, kw, r0, rc=RC):
    acc = None
    for k in range(kw):
        term = w_ref[k:k + 1, :] * pad2_ref[pl.ds(kw - 1 - k + r0, rc), :]
        acc = term if acc is None else acc + term
    return acc


def _conv_dw(pad_ref, dout_ref, dw_ref, kw, head, rc=RC):
    for k in range(kw):
        acc = None
        for r0 in range(0, S, rc):
            term = jnp.sum(pad_ref[pl.ds(head - (kw - 1) + k + r0, rc), :] * dout_ref[pl.ds(r0, rc), :],
                           axis=0, keepdims=True)
            acc = term if acc is None else acc + term
        dw_ref[k:k + 1, :] = acc


def _col_spec(width, off_blocks=0):
    return pl.BlockSpec((S, width), lambda j: (0, j + off_blocks))


def _dn_pre_fwd(proj, conv_w):
    head = 8

    def body(x_ref, w_ref, o_ref, pad_ref):
        j = pl.program_id(0)
        pad_ref[pl.ds(0, head), :] = jnp.zeros((head, DH), F32)
        pad_ref[pl.ds(head, S), :] = x_ref[...]
        scale = jnp.where(j < H, DH ** -0.5, 1.0)
        for r0 in range(0, S, RC):
            y = _silu(_conv_fwd_rows(pad_ref, w_ref, DNK, head, r0))
            rinv = lax.rsqrt(jnp.sum(y * y, axis=-1, keepdims=True) + EPS)
            o_ref[pl.ds(r0, RC), :] = jnp.where(j < 2 * H, y * rinv * scale, y)

    return _call(body, "dn_pre_fwd", _sds((S, 3 * DNW)), grid=(3 * H,),
                 in_specs=[_col_spec(DH), pl.BlockSpec((DNK, DH), lambda j: (0, j))], out_specs=_col_spec(DH),
                 scratch=[pltpu.VMEM((S + head, DH), F32)], sem=("parallel",))(proj, conv_w)


def _dn_pre_bwd(dq, dk, dv, proj, conv_w, dproj):
    head = 8

    def body(dq_ref, dk_ref, dv_ref, x_ref, w_ref, dproj_in, dx_ref, dw_ref, pad_ref, pad2_ref):
        j = pl.program_id(0)
        pad_ref[pl.ds(0, head), :] = jnp.zeros((head, DH), F32)
        pad_ref[pl.ds(head, S), :] = x_ref[...]
        pad2_ref[pl.ds(S, head), :] = jnp.zeros((head, DH), F32)
        scale = jnp.where(j < H, DH ** -0.5, 1.0)
        for r0 in range(0, S, RC_WIDE):
            xc = _conv_fwd_rows(pad_ref, w_ref, DNK, head, r0, RC_WIDE)
            y, dy_dxc = _silu_both(xc)
            rinv = lax.rsqrt(jnp.sum(y * y, axis=-1, keepdims=True) + EPS)
            yn = y * rinv
            rows = pl.ds(r0, RC_WIDE)
            do = jnp.where(j < H, dq_ref[rows, :], jnp.where(j < 2 * H, dk_ref[rows, :], dv_ref[rows, :]))
            dy_n = scale * rinv * (do - yn * jnp.sum(do * yn, axis=-1, keepdims=True))
            dy = jnp.where(j < 2 * H, dy_n, do)
            pad2_ref[rows, :] = dy * dy_dxc
        for r0 in range(0, S, RC_WIDE):
            dx_ref[pl.ds(r0, RC_WIDE), :] = _conv_bwd_rows(pad2_ref, w_ref, DNK, r0, RC_WIDE).astype(BF16)
        _conv_dw(pad_ref, pad2_ref, dw_ref, DNK, head, RC_WIDE)

    wspec = pl.BlockSpec((DNK, DH), lambda j: (0, j))
    head_col = lambda lo: pl.BlockSpec((S, DH), lambda j: (0, jnp.clip(j - lo, 0, H - 1)))
    return _call(body, "dn_pre_bwd", [_sds((S, NINP), BF16), _sds((DNK, 3 * DNW))], grid=(3 * H,),
                 in_specs=[head_col(0), head_col(H), head_col(2 * H), _col_spec(DH), wspec, ANY],
                 out_specs=[_col_spec(DH), wspec],
                 scratch=[pltpu.VMEM((S + head, DH), F32), pltpu.VMEM((S + head, DH), F32)],
                 sem=("parallel",), aliases={5: 0})(dq, dk, dv, proj, conv_w, dproj)


CF_HEAD = 32
CF_VAL = pl.BlockSpec((S, LANE), lambda j: (0, O_GLU // LANE + 2 * j))
CF_GL = pl.BlockSpec((S, LANE), lambda j: (0, O_GLU // LANE + 2 * j + 1))


def _cf_conv_fwd(proj, conv_w):
    def body(val_ref, gl_ref, w_ref, o_ref, pad_ref):
        pad_ref[pl.ds(0, CF_HEAD), :] = jnp.zeros((CF_HEAD, LANE), F32)
        pad_ref[pl.ds(CF_HEAD, S), :] = val_ref[...] * _sigmoid(gl_ref[...])
        for r0 in range(0, S, RC):
            o_ref[pl.ds(r0, RC), :] = _conv_fwd_rows(pad_ref, w_ref, CFK, CF_HEAD, r0)

    wspec = pl.BlockSpec((CFK, LANE), lambda j: (0, j))
    return _call(body, "cf_conv_fwd", _sds((S, CFW)), grid=(CFW // LANE,),
                 in_specs=[CF_VAL, CF_GL, wspec], out_specs=_col_spec(LANE),
                 scratch=[pltpu.VMEM((S + CF_HEAD, LANE), F32)], sem=("parallel",))(proj, proj, conv_w)


def _cf_conv_bwd(du1, proj, conv_w, dproj):
    def body(d_ref, val_ref, gl_ref, w_ref, dproj_in, dp_ref, dw_ref, pad_ref, pad2_ref):
        sg = _sigmoid(gl_ref[...])
        pad_ref[pl.ds(0, CF_HEAD), :] = jnp.zeros((CF_HEAD, LANE), F32)
        pad_ref[pl.ds(CF_HEAD, S), :] = val_ref[...] * sg
        pad2_ref[pl.ds(0, S), :] = d_ref[...]
        pad2_ref[pl.ds(S, CF_HEAD), :] = jnp.zeros((CF_HEAD, LANE), F32)
        for r0 in range(0, S, RC):
            du0 = _conv_bwd_rows(pad2_ref, w_ref, CFK, r0)
            rows = pl.ds(r0, RC)
            sgr = _sigmoid(gl_ref[rows, :])
            dp_ref[rows, 0:LANE] = (du0 * sgr).astype(BF16)
            dp_ref[rows, LANE:2 * LANE] = (du0 * val_ref[rows, :] * sgr * (1.0 - sgr)).astype(BF16)
        _conv_dw(pad_ref, pad2_ref, dw_ref, CFK, CF_HEAD)

    wspec = pl.BlockSpec((CFK, LANE), lambda j: (0, j))
    return _call(body, "cf_conv_bwd", [_sds((S, NINP), BF16), _sds((CFK, CFW))], grid=(CFW // LANE,),
                 in_specs=[_col_spec(LANE), CF_VAL, CF_GL, wspec, ANY],
                 out_specs=[pl.BlockSpec((S, 2 * LANE), lambda j: (0, O_GLU // (2 * LANE) + j)), wspec],
                 scratch=[pltpu.VMEM((S + CF_HEAD, LANE), F32), pltpu.VMEM((S + CF_HEAD, LANE), F32)],
                 sem=("parallel",), aliases={4: 0})(du1, proj, proj, conv_w, dproj)


def _cf_ln_fwd(u1, g, b):
    def body(u_ref, g_ref, b_ref, o_ref):
        u = u_ref[...]
        mu = jnp.mean(u, axis=-1, keepdims=True)
        xc = u - mu
        y = xc * lax.rsqrt(jnp.mean(xc * xc, axis=-1, keepdims=True) + EPS)
        o_ref[...] = _silu(y * g_ref[...] + b_ref[...]).astype(BF16)

    return _call(body, "cf_ln_fwd", _sds((S, CFW), BF16), grid=(S // TS,),
                 in_specs=[_row_spec(CFW), _vec_spec(CFW), _vec_spec(CFW)], out_specs=_row_spec(CFW),
                 sem=("parallel",))(u1, g, b)


def _cf_ln_bwd(du3, u1, g, b):
    def body(d_ref, u_ref, g_ref, b_ref, du_ref, acc_ref):
        u = u_ref[...]
        mu = jnp.mean(u, axis=-1, keepdims=True)
        xc = u - mu
        rstd = lax.rsqrt(jnp.mean(xc * xc, axis=-1, keepdims=True) + EPS)
        xh = xc * rstd
        du2 = d_ref[...] * _dsilu(xh * g_ref[...] + b_ref[...])
        dxh = du2 * g_ref[...]
        du_ref[...] = rstd * (dxh - jnp.mean(dxh, axis=-1, keepdims=True)
                              - xh * jnp.mean(dxh * xh, axis=-1, keepdims=True))
        _acc_rows(acc_ref, [jnp.sum(du2 * xh, axis=0, keepdims=True), jnp.sum(du2, axis=0, keepdims=True)])

    return _call(body, "cf_ln_bwd", [_sds((S, CFW)), _sds((8, CFW))], grid=(S // TS,),
                 in_specs=[_row_spec(CFW), _row_spec(CFW), _vec_spec(CFW), _vec_spec(CFW)],
                 out_specs=[_row_spec(CFW), _acc_spec(CFW)], sem=("arbitrary",))(du3, u1, g, b)


FB = 256
FNB = FFN // FB
FF_HEAD = 8


def _ffn_mid_fwd(upall, conv_w):
    def body(gate_ref, up_ref, w_ref, o_ref, pad_ref):
        pad_ref[pl.ds(0, FF_HEAD), :] = jnp.zeros((FF_HEAD, FB), F32)
        pad_ref[pl.ds(FF_HEAD, S), :] = gate_ref[...]
        for r0 in range(0, S, RC):
            gc = _conv_fwd_rows(pad_ref, w_ref, FFK, FF_HEAD, r0)
            o_ref[pl.ds(r0, RC), :] = (_silu(gc) * up_ref[pl.ds(r0, RC), :]).astype(BF16)

    wspec = pl.BlockSpec((FFK, FB), lambda j: (0, j))
    return _call(body, "ffn_mid_fwd", _sds((S, FFN), BF16), grid=(FNB,),
                 in_specs=[_col_spec(FB), _col_spec(FB, FNB), wspec], out_specs=_col_spec(FB),
                 scratch=[pltpu.VMEM((S + FF_HEAD, FB), F32)], sem=("parallel",))(upall, upall, conv_w)


def _ffn_mid_bwd(dh, upall, conv_w):
    def body(d_ref, gate_ref, up_ref, w_ref, dgate_ref, dup_ref, dw_ref, pad_ref, pad2_ref):
        pad_ref[pl.ds(0, FF_HEAD), :] = jnp.zeros((FF_HEAD, FB), F32)
        pad_ref[pl.ds(FF_HEAD, S), :] = gate_ref[...]
        pad2_ref[pl.ds(S, FF_HEAD), :] = jnp.zeros((FF_HEAD, FB), F32)
        for r0 in range(0, S, RC):
            rows = pl.ds(r0, RC)
            gc = _conv_fwd_rows(pad_ref, w_ref, FFK, FF_HEAD, r0)
            dhv = d_ref[rows, :]
            act, dact = _silu_both(gc)
            dup_ref[rows, :] = (dhv * act).astype(BF16)
            pad2_ref[rows, :] = dhv * up_ref[rows, :] * dact
        for r0 in range(0, S, RC):
            dgate_ref[pl.ds(r0, RC), :] = _conv_bwd_rows(pad2_ref, w_ref, FFK, r0).astype(BF16)
        _conv_dw(pad_ref, pad2_ref, dw_ref, FFK, FF_HEAD)

    wspec = pl.BlockSpec((FFK, FB), lambda j: (0, j))
    return _call(body, "ffn_mid_bwd", [_sds((S, FFN), BF16), _sds((S, FFN), BF16), _sds((FFK, FFN))],
                 grid=(FNB,), in_specs=[_col_spec(FB), _col_spec(FB), _col_spec(FB, FNB), wspec],
                 out_specs=[_col_spec(FB), _col_spec(FB), wspec],
                 scratch=[pltpu.VMEM((S + FF_HEAD, FB), F32), pltpu.VMEM((S + FF_HEAD, FB), F32)],
                 sem=("parallel",))(dh, upall, upall, conv_w)


GT = 256
SM_BLK = O_SM // LANE


def _chunk_tri(lower):
    r = lax.broadcasted_iota(jnp.int32, (GT, GT), 0)
    c = lax.broadcasted_iota(jnp.int32, (GT, GT), 1)
    same = (r // CH) == (c // CH)
    tri = (c <= r) if lower else (c >= r)
    return jnp.where(same & tri, 1.0, 0.0).astype(F32)


def _gates_fwd(proj, alog_v, dtb_v):
    def body(sm_ref, al_ref, dt_ref, o_ref):
        lane = lax.broadcasted_iota(jnp.int32, (GT, LANE), 1)
        tri = _chunk_tri(True)
        na = -jnp.exp(al_ref[...])
        for r0 in range(0, S, GT):
            sm = sm_ref[pl.ds(r0, GT), :]
            raw = jnp.where((lane >= H) & (lane < 2 * H), na * _softplus(sm + dt_ref[...]), 0.0)
            gc = _dot(tri, raw, NN, precision=HI)
            o_ref[pl.ds(r0, GT), :] = jnp.where(lane < H, _sigmoid(sm), gc)

    return _call(body, "gates_fwd", _sds((S, LANE)), grid=(1,),
                 in_specs=[pl.BlockSpec((S, LANE), lambda i: (0, SM_BLK)), _vec_spec(LANE), _vec_spec(LANE)],
                 out_specs=pl.BlockSpec((S, LANE), lambda i: (0, 0)), sem=("arbitrary",))(proj, alog_v, dtb_v)


def _gates_bwd(dgb, proj, alog_v, dtb_v, dproj):
    def body(d_ref, sm_ref, al_ref, dt_ref, dproj_in, o_ref, acc_ref):
        lane = lax.broadcasted_iota(jnp.int32, (GT, LANE), 1)
        is_g = (lane >= H) & (lane < 2 * H)
        tri = _chunk_tri(False)
        na = -jnp.exp(al_ref[...])
        d_al = jnp.zeros((1, LANE), F32)
        d_dt = jnp.zeros((1, LANE), F32)
        for r0 in range(0, S, GT):
            sm = sm_ref[pl.ds(r0, GT), :]
            dv = d_ref[pl.ds(r0, GT), :]
            z = sm + dt_ref[...]
            draw = _dot(tri, jnp.where(is_g, dv, 0.0), NN, precision=HI)
            dlogit = jnp.where(is_g, draw * na * _sigmoid(z), 0.0)
            d_al = d_al + jnp.sum(jnp.where(is_g, draw * na * _softplus(z), 0.0), axis=0, keepdims=True)
            d_dt = d_dt + jnp.sum(dlogit, axis=0, keepdims=True)
            bt = _sigmoid(sm)
            o_ref[pl.ds(r0, GT), :] = jnp.where(lane < H, dv * bt * (1.0 - bt), dlogit).astype(BF16)
        acc_ref[...] = jnp.zeros_like(acc_ref)
        acc_ref[0:1, :] = d_al
        acc_ref[1:2, :] = d_dt

    return _call(body, "gates_bwd", [_sds((S, NINP), BF16), _sds((8, LANE))], grid=(1,),
                 in_specs=[pl.BlockSpec((S, LANE), lambda i: (0, 0)), pl.BlockSpec((S, LANE), lambda i: (0, SM_BLK)),
                           _vec_spec(LANE), _vec_spec(LANE), ANY],
                 out_specs=[pl.BlockSpec((S, LANE), lambda i: (0, SM_BLK)), _acc_spec(LANE)],
                 sem=("arbitrary",), aliases={4: 0})(dgb, proj, alog_v, dtb_v, dproj)


HB = 4


def _each(fn, *lists):
    return [fn(*args) for args in zip(*lists)]


def _neumann_inv(a, eye):
    p = _each(lambda m: -m, a)
    t = _each(lambda m: eye + m, p)
    for _ in range(5):
        p = _each(lambda m: _dot(m, m, NN, precision=HI), p)
        t = _each(lambda tt, pp: tt + _dot(tt, pp, NN, precision=HI), t, p)
    return t


def _head_specs():
    q = pl.BlockSpec((S, HB * DH), lambda h: (0, h), pipeline_mode=ONE_BUF)
    k = pl.BlockSpec((S, HB * DH), lambda h: (0, H // HB + h), pipeline_mode=ONE_BUF)
    v = pl.BlockSpec((S, HB * DH), lambda h: (0, 2 * H // HB + h), pipeline_mode=ONE_BUF)
    gb = pl.BlockSpec((HB, S, DH), lambda h: (h, 0, 0), pipeline_mode=ONE_BUF)
    gr = pl.BlockSpec((HB, NCH, CH), lambda h: (h, 0, 0))
    return q, k, v, gb, gr


ONE_BUF = pl.Buffered(1)
ST_SPEC = pl.BlockSpec((HB, NCH, DH, DH), lambda h: (h, 0, 0, 0), pipeline_mode=ONE_BUF)
TM_SPEC = pl.BlockSpec((HB, NCH, CH, CH), lambda h: (h, 0, 0, 0), pipeline_mode=ONE_BUF)
HCOL = pl.BlockSpec((S, HB * DH), lambda h: (0, h), pipeline_mode=ONE_BUF)


def _delta_fwd(qkvn, gb, gr, bb):
    def body(q_ref, k_ref, v_ref, gb_ref, gr_ref, bb_ref, o_ref, st_ref, tm_ref):
        ri = lax.broadcasted_iota(jnp.int32, (CH, CH), 0)
        ci = lax.broadcasted_iota(jnp.int32, (CH, CH), 1)
        strict = ri > ci
        causal = ri >= ci
        eye = jnp.where(ri == ci, 1.0, 0.0).astype(F32)

        hs = list(range(HB))
        cols = [slice(hh * DH, (hh + 1) * DH) for hh in hs]
        bf = lambda m: m.astype(BF16)

        def local(n):
            rows = pl.ds(pl.multiple_of(n * CH, CH), CH)
            c = dict(rows=rows, n=n)
            c["q"] = [q_ref[rows, cc] for cc in cols]
            c["k"] = [k_ref[rows, cc] for cc in cols]
            c["v"] = [v_ref[rows, cc] for cc in cols]
            c["g"] = [gb_ref[hh, rows, :] for hh in hs]
            c["beta"] = [bb_ref[hh, rows, :] for hh in hs]
            diff = [c["g"][hh][:, :CH] - gr_ref[hh, pl.ds(n, 1), :] for hh in hs]
            c["el"] = _each(lambda d: jnp.exp(jnp.where(causal, d, 0.0)), diff)
            c["eg"] = _each(jnp.exp, c["g"])
            c["gl"] = _each(lambda m: m[CH - 1:CH, :], c["g"])
            c["kb"] = _each(lambda x, y: x * y, c["k"], c["beta"])
            c["kbf"] = _each(bf, c["k"])
            c["a"] = _each(lambda x, y, e: jnp.where(strict, _dot(bf(x), y, NT) * e, 0.0), c["kb"], c["kbf"], c["el"])
            return c

        def advance(c, t, sts):
            n, rows = c["n"], c["rows"]
            for hh in hs:
                tm_ref[hh, n] = t[hh]
                st_ref[hh, n] = sts[hh]
            sb = _each(bf, sts)
            r = _each(lambda vv, bb_, kk, ee, ss: vv * bb_ - _dot(bf(kk * ee), ss, NN), c["v"], c["beta"], c["kb"], c["eg"], sb)
            ub = _each(lambda tt, rr: bf(_dot(tt, rr, NN, precision=HI)), t, r)
            p = _each(lambda qq, kk, e: jnp.where(causal, _dot(bf(qq), kk, NT) * e, 0.0), c["q"], c["kbf"], c["el"])
            o = _each(lambda qq, ee, ss, pp, uu: _dot(bf(qq * ee), ss, NN) + _dot(bf(pp), uu, NN), c["q"], c["eg"], sb, p, ub)
            for hh in hs:
                o_ref[rows, cols[hh]] = o[hh]
            kd = _each(lambda kk, l, gg: kk * jnp.exp(l - gg), c["k"], c["gl"], c["g"])
            return _each(lambda st, l, kk, uu: st * jnp.exp(l) + _dot(bf(kk), uu, TN), sts, c["gl"], kd, ub)

        def step(i, sts):
            c0, c1 = local(2 * i), local(2 * i + 1)
            t = _neumann_inv(c0["a"] + c1["a"], eye)
            sts = advance(c0, t[:HB], list(sts))
            return tuple(advance(c1, t[HB:], sts))

        lax.fori_loop(0, NCH // 2, step, tuple(jnp.zeros((DH, DH), F32) for _ in hs))

    q, k, v, gbs, grs = _head_specs()
    return _call(body, "delta_fwd", [_sds((S, DNW)), _sds((H, NCH, DH, DH)), _sds((H, NCH, CH, CH))], grid=(H // HB,),
                 in_specs=[q, k, v, gbs, grs, gbs], out_specs=[HCOL, ST_SPEC, TM_SPEC],
                 sem=("parallel",))(qkvn, qkvn, qkvn, gb, gr, bb)


def _delta_bwd(qkvn, gb, gr, bb, st_all, tm_all, do_all):
    def body(q_ref, k_ref, v_ref, gb_ref, gr_ref, bb_ref, st_ref, tm_ref, do_ref,
             dq_ref, dk_ref, dv_ref, dg_ref, db_ref):
        ri = lax.broadcasted_iota(jnp.int32, (CH, CH), 0)
        ci = lax.broadcasted_iota(jnp.int32, (CH, CH), 1)
        lo_s, lo_c, up_s, up_c = ri > ci, ri >= ci, ri < ci, ri <= ci
        last_row = lax.broadcasted_iota(jnp.int32, (CH, 1), 0) == CH - 1

        def rs(mat):
            return jnp.sum(mat, axis=1, keepdims=True)

        def total(mat):
            return jnp.sum(rs(mat), axis=0, keepdims=True)

        hs = list(range(HB))
        cols = [slice(hh * DH, (hh + 1) * DH) for hh in hs]
        bf = lambda m: m.astype(BF16)
        mul = lambda x, y: x * y
        spread = jnp.full((8, DH), 1.0 / DH, F32)

        def as_row(col):
            return _dot(spread, jnp.broadcast_to(col, (CH, DH)), NT, precision=HI)[0:1, :]

        def step(i, dss):
            ns = [NCH - 1 - 2 * i, NCH - 2 - 2 * i]
            rws = [pl.ds(pl.multiple_of(n * CH, CH), CH) for n in ns]
            idx = [(cc, hh) for cc in range(2) for hh in hs]
            q = [q_ref[rws[cc], cols[hh]] for cc, hh in idx]
            k = [k_ref[rws[cc], cols[hh]] for cc, hh in idx]
            v = [v_ref[rws[cc], cols[hh]] for cc, hh in idx]
            do = [do_ref[rws[cc], cols[hh]] for cc, hh in idx]
            g = [gb_ref[hh, rws[cc], :] for cc, hh in idx]
            beta = [bb_ref[hh, rws[cc], :] for cc, hh in idx]
            t = [tm_ref[hh, ns[cc]] for cc, hh in idx]
            st = [st_ref[hh, ns[cc]] for cc, hh in idx]
            diff = [gg[:, :CH] - gr_ref[hh, pl.ds(ns[cc], 1), :] for gg, (cc, hh) in zip(g, idx)]
            el = _each(lambda d: jnp.exp(jnp.where(lo_c, d, 0.0)), diff)
            eu = _each(lambda d: jnp.exp(jnp.where(up_c, -d, 0.0)), diff)
            eg = _each(jnp.exp, g)
            gl = _each(lambda m: m[CH - 1:CH, :], g)
            egl = _each(jnp.exp, gl)
            ekd = _each(lambda l, m: jnp.exp(l - m), gl, g)
            kb = _each(mul, k, beta)
            kbg = _each(mul, kb, eg)
            qg = _each(mul, q, eg)
            kd = _each(mul, k, ekd)
            qb, kbf, kbb = _each(bf, q), _each(bf, k), _each(bf, kb)
            kbgb, qgb, kdb = _each(bf, kbg), _each(bf, qg), _each(bf, kd)
            sb, dob = _each(bf, st), _each(bf, do)
            r = _each(lambda vv, b, x, s: vv * b - _dot(x, s, NN), v, beta, kbgb, sb)
            u = _each(lambda tt, rr: _dot(tt, rr, NN, precision=HI), t, r)
            ub = _each(bf, u)
            kk = _each(lambda x, y: _dot(x, y, NT), kbb, kbf)
            qk = _each(lambda x, y: _dot(x, y, NT), qb, kbf)
            kkt = _each(lambda x, y: _dot(x, y, NT), kbf, kbb)
            qkt = _each(lambda x, y: _dot(x, y, NT), kbf, qb)
            pt = _each(lambda m, e: jnp.where(up_c, m * e, 0.0), qkt, eu)
            ds, du, dr, drb, ds_new = [], [], [], [], list(dss)
            for cc in range(2):
                sl = slice(cc * HB, (cc + 1) * HB)
                ds_c = ds_new
                dsb_c = _each(bf, ds_c)
                du_c = _each(lambda p, d, x, s: _dot(bf(p), d, NN) + _dot(x, s, NN), pt[sl], dob[sl], kdb[sl], dsb_c)
                dr_c = _each(lambda tt, d: _dot(tt, d, TN, precision=HI), t[sl], du_c)
                drb_c = _each(bf, dr_c)
                ds_new = _each(lambda x, d, e, s, y, z: _dot(x, d, TN) + e * s - _dot(y, z, TN),
                               qgb[sl], dob[sl], egl[sl], ds_c, kbgb[sl], drb_c)
                ds, du, dr, drb = ds + ds_c, du + du_c, dr + dr_c, drb + drb_c
            dsb = _each(bf, ds)
            dpg = _each(lambda d, uu, e: jnp.where(lo_c, _dot(d, uu, NT), 0.0) * e, dob, ub, el)
            dpgt = _each(lambda uu, d, e: jnp.where(up_c, _dot(uu, d, NT), 0.0) * e, ub, dob, eu)
            dag = _each(lambda d, uu, e: -jnp.where(lo_s, _dot(d, uu, NT), 0.0) * e, drb, ub, el)
            dagt = _each(lambda uu, d, e: -jnp.where(up_s, _dot(uu, d, NT), 0.0) * e, ub, drb, eu)
            dqg = _each(lambda d, s: _dot(d, s, NT), dob, sb)
            dkbg = _each(lambda d, s: -_dot(d, s, NT), drb, sb)
            dkd = _each(lambda uu, s: _dot(uu, s, NT), ub, dsb)
            dkb =_each(lambda a, x, y, e: _dot(bf(a), x, NN) + y * e, dag, kbf, dkbg, eg)
            dk = _each(lambda a, x, p, y, z, e, w, b: _dot(bf(a), x, NN) + _dot(bf(p), y, NN) + z * e + w * b,
                       dagt, kbb, dpgt, qb, dkd, ekd, dkb, beta)
            dq = _each(lambda p, x, y, e: _dot(bf(p), x, NN) + y * e, dpg, kbf, dqg, eg)
            dkd_kd = _each(lambda x, y: rs(x * y), dkd, kd)
            dg = _each(lambda a, x, p, y, at, xt, pt_, yt, z, w, c, d, e:
                       rs(a * x + p * y) - rs(at * xt + pt_ * yt) + rs(z * w) + rs(c * d) - e,
                       dag, kk, dpg, qk, dagt, kkt, dpgt, qkt, dqg, qg, dkbg, kbg, dkd_kd)
            dgl = _each(lambda x, e, s, y: jnp.sum(x, axis=0, keepdims=True) + e[:, 0:1] * total(s * y), dkd_kd, egl, ds, st)
            dg = _each(lambda x, y: x + jnp.where(last_row, y, 0.0), dg, dgl)
            dbeta = _each(lambda x, y, z, w: rs(x * y) + rs(z * w), dkb, k, dr, v)
            for j, (cc, hh) in enumerate(idx):
                dq_ref[rws[cc], cols[hh]] = dq[j]
                dk_ref[rws[cc], cols[hh]] = dk[j]
                dv_ref[rws[cc], cols[hh]] = dr[j] * beta[j]
                dg_ref[hh, pl.ds(ns[cc], 1), :] = as_row(dg[j])
                db_ref[hh, pl.ds(ns[cc], 1), :] = as_row(dbeta[j])
            return tuple(ds_new)

        lax.fori_loop(0, NCH // 2, step, tuple(jnp.zeros((DH, DH), F32) for _ in hs))

    q, k, v, gbs, grs = _head_specs()
    return _call(body, "delta_bwd",
                 [_sds((S, DNW)), _sds((S, DNW)), _sds((S, DNW)), _sds((H, NCH, CH)), _sds((H, NCH, CH))], grid=(H // HB,),
                 in_specs=[q, k, v, gbs, grs, gbs, ST_SPEC, TM_SPEC, HCOL], out_specs=[HCOL, HCOL, HCOL, grs, grs],
                 sem=("parallel",))(qkvn, qkvn, qkvn, gb, gr, bb, st_all, tm_all, do_all)


Z_BLK = O_Z // DNW


def _dn_post_fwd(o, proj, gn):
    def body(o_ref, z_ref, gn_ref, og_ref):
        for h in range(H):
            cols = slice(h * DH, (h + 1) * DH)
            ov = o_ref[:, cols]
            on = ov * lax.rsqrt(jnp.mean(ov * ov, axis=-1, keepdims=True) + EPS) * gn_ref[...]
            og_ref[:, cols] = (on * _silu(z_ref[:, cols])).astype(BF16)

    return _call(body, "dn_post_fwd", _sds((S, DNW), BF16), grid=(S // TS,),
                 in_specs=[_row_spec(DNW), pl.BlockSpec((TS, DNW), lambda i: (i, Z_BLK)), _vec_spec(DH)],
                 out_specs=_row_spec(DNW), sem=("parallel",))(o, proj, gn)


def _dn_post_bwd(dog, o, proj, gn, dproj):
    def body(d_ref, o_ref, z_ref, gn_ref, dproj_in, do_ref, dz_ref, acc_ref):
        dgn = jnp.zeros((1, DH), F32)
        for h in range(H):
            cols = slice(h * DH, (h + 1) * DH)
            ov, zv, dv = o_ref[:, cols], z_ref[:, cols], d_ref[:, cols]
            rinv = lax.rsqrt(jnp.mean(ov * ov, axis=-1, keepdims=True) + EPS)
            xn = ov * rinv
            act, dact = _silu_both(zv)
            don = dv * act
            dz_ref[:, cols] = (dv * xn * gn_ref[...] * dact).astype(BF16)
            dgn = dgn + jnp.sum(don * xn, axis=0, keepdims=True)
            dxn = don * gn_ref[...]
            do_ref[:, cols] = rinv * (dxn - xn * jnp.mean(dxn * xn, axis=-1, keepdims=True))
        _acc_rows(acc_ref, [dgn])

    zspec = pl.BlockSpec((TS, DNW), lambda i: (i, Z_BLK))
    return _call(body, "dn_post_bwd", [_sds((S, DNW)), _sds((S, NINP), BF16), _sds((8, DH))], grid=(S // TS,),
                 in_specs=[_row_spec(DNW), _row_spec(DNW), zspec, _vec_spec(DH), ANY],
                 out_specs=[_row_spec(DNW), zspec, _acc_spec(DH)], sem=("arbitrary",),
                 aliases={4: 1})(dog, o, proj, gn, dproj)


GA_BLK = O_GA // D
GB_BLK = O_GB // D


def _merge_fwd(ba, bb, proj):
    def body(a_ref, b_ref, ga_ref, gb_ref, o_ref):
        o_ref[...] = (_sigmoid(ga_ref[...]) * a_ref[...] + _sigmoid(gb_ref[...]) * b_ref[...]).astype(BF16)

    return _call(body, "merge_fwd", _sds((S, D), BF16), grid=(S // TS,),
                 in_specs=[_row_spec(), _row_spec(), pl.BlockSpec((TS, D), lambda i: (i, GA_BLK)),
                           pl.BlockSpec((TS, D), lambda i: (i, GB_BLK))],
                 out_specs=_row_spec(), sem=("parallel",))(ba, bb, proj, proj)


def _merge_bwd(dm, ba, bb, proj, dproj):
    def body(d_ref, a_ref, b_ref, ga_ref, gb_ref, dproj_in, dg_ref, da_ref, db_ref):
        d = d_ref[...]
        sa, sb = _sigmoid(ga_ref[...]), _sigmoid(gb_ref[...])
        dg_ref[:, 0:D] = (d * a_ref[...] * sa * (1.0 - sa)).astype(BF16)
        dg_ref[:, D:2 * D] = (d * b_ref[...] * sb * (1.0 - sb)).astype(BF16)
        da_ref[...] = (d * sa).astype(BF16)
        db_ref[...] = (d * sb).astype(BF16)

    return _call(body, "merge_bwd", [_sds((S, NINP), BF16), _sds((S, D), BF16), _sds((S, D), BF16)], grid=(S // TS,),
                 in_specs=[_row_spec(), _row_spec(), _row_spec(), pl.BlockSpec((TS, D), lambda i: (i, GA_BLK)),
                           pl.BlockSpec((TS, D), lambda i: (i, GB_BLK)), ANY],
                 out_specs=[pl.BlockSpec((TS, 2 * D), lambda i: (i, O_GA // (2 * D))), _row_spec(), _row_spec()],
                 sem=("parallel",), aliases={5: 0})(dm, ba, bb, proj, proj, dproj)


NSH = NIN // NDEV


def _pad_win(wt):
    rows = [wt[0:4096], wt[6160:6160 + 2 * D]]
    for j in range(CFW // LANE):
        rows += [wt[4112 + LANE * j:4112 + LANE * (j + 1)], wt[4112 + CFW + LANE * j:4112 + CFW + LANE * (j + 1)]]
    rows += [wt[4096:4112], jnp.zeros((NINP - NIN, wt.shape[1]), wt.dtype)]
    return jnp.concatenate(rows, axis=0)


def _unpad_win(gpt):
    rows = [gpt[0:4096], gpt[O_SM:O_SM + 16]]
    for half in range(2):
        rows += [gpt[O_GLU + (2 * j + half) * LANE:O_GLU + (2 * j + half + 1) * LANE] for j in range(CFW // LANE)]
    rows.append(gpt[O_GA:O_GA + 2 * D])
    return jnp.concatenate(rows, axis=0)


def _lane_vec(v8, offset):
    return jnp.pad(v8, ((0, 0), (offset, LANE - 8 - offset)))


def _tie(vec, token):
    return vec + token


def _local_step(x, tgt, mod, norm1_g, norm2_g, final_g, w_in_p, dn_conv_w, a_log, dt_bias, dn_norm_g,
                cf_conv_w, cf_ln_g, cf_ln_b, ffn_conv_w, comm):
    sh1, sc1, gt1, sh2, sc2, gt2 = (mod[:, i * D:(i + 1) * D] for i in range(6))
    alog_v, dtb_v = _lane_vec(a_log, H), _lane_vec(dt_bias, H)

    hn1 = _norm_mod(x, norm1_g, sc1, _tie(sh1, comm.token0), "norm_mod1")
    proj = _mm(hn1, w_in_p, "nt", F32, "mm_in", tn=1152)
    qkvn = _dn_pre_fwd(proj, dn_conv_w)
    gates = _gates_fwd(proj, alog_v, dtb_v)
    beta_t = gates[:, 0:H].T
    g_t = gates[:, H:2 * H].T
    gb = jnp.broadcast_to(g_t[:, :, None], (H, S, DH))
    bb = jnp.broadcast_to(beta_t[:, :, None], (H, S, DH))
    gr = g_t.reshape(H, NCH, CH)
    o, st_all, tm_all = _delta_fwd(qkvn, gb, gr, bb)
    og = _dn_post_fwd(o, proj, dn_norm_g)
    u1 = _cf_conv_fwd(proj, cf_conv_w)
    u3 = _cf_ln_fwd(u1, cf_ln_g, cf_ln_b)
    after = og[0:8, 0:LANE].astype(F32) + u3[0:8, 0:LANE].astype(F32)
    dn_w_o, cf_w_o, w_out = comm.late_weights("mix", after)
    br_a = _mm(og, dn_w_o, "nn", F32, "mm_dn_o")
    br_b = _mm(u3, cf_w_o, "nn", F32, "mm_cf_o")
    merged = _merge_fwd(br_a, br_b, proj)
    mix = _mm(merged, w_out, "nn", F32, "mm_out")
    x2, hn2 = _resid_norm_mod(x, mix, gt1, norm2_g, sc2, sh2, "resid_norm_mod2")
    ffn_w_up, ffn_w_down = comm.late_weights("ffn", hn2[0:8, 0:LANE].astype(F32))
    upall = _mm(hn2, ffn_w_up, "nn", F32, "mm_up")
    hmid = _ffn_mid_fwd(upall, ffn_conv_w)
    f = _mm(hmid, ffn_w_down, "nn", F32, "mm_down", tm=2048)

    dx3, df, acc_f = _loss_head(x2, f, tgt, gt2, final_g)
    d_final_g, d_gt2, loss = acc_f[0:1], acc_f[1:2], acc_f[2:3, 0:1]
    dhmid = _mm(df, ffn_w_down, "nt", F32, "mm_down_dx")
    g_w_down = _mm(hmid, df, "tn", BF16, "mm_down_dw", tm=FFN // 4)
    d_gate, d_up, g_ffn_conv = _ffn_mid_bwd(dhmid, upall, ffn_conv_w)
    g_w_up = _mm(hn2, d_gate, "tn", BF16, "mm_up_dw", tn=2 * FFN // NDEV, b2=d_up)
    tok_a = comm.grads_begin("a", dict(ffn_w_down=g_w_down, ffn_w_up=g_w_up))
    dhn2 = _mm(d_gate, ffn_w_up, "nt", F32, "mm_up_dx", a2=d_up, dep=jnp.broadcast_to(tok_a, (8, LANE)))
    tok_a = comm.grads_continue("a", dhn2)
    dx2, dmix, acc2 = _norm_mod_bwd(dhn2, x2, dx3, _tie(norm2_g, tok_a), sc2, "norm_mod2_bwd", mix=mix, gt=gt1)
    d_sh2, d_sc2, d_norm2_g, d_gt1 = acc2[0:1], acc2[1:2], acc2[2:3], acc2[3:4]
    dmerged = _mm(dmix, w_out, "nt", F32, "mm_out_dx")
    g_w_out = _mm(merged, dmix, "tn", BF16, "mm_out_dw")
    d_proj, d_bra, d_brb = _merge_bwd(dmerged, br_a, br_b, proj, lax.empty((S, NINP), BF16))
    du3 = _mm(d_brb, cf_w_o, "nt", F32, "mm_cf_o_dx")
    g_cf_w_o = _mm(u3, d_brb, "tn", BF16, "mm_cf_o_dw")
    du1, acc_ln = _cf_ln_bwd(du3, u1, cf_ln_g, cf_ln_b)
    d_proj, g_cf_conv = _cf_conv_bwd(du1, proj, cf_conv_w, d_proj)
    dog = _mm(d_bra, dn_w_o, "nt", F32, "mm_dn_o_dx")
    g_dn_w_o = _mm(og, d_bra, "tn", BF16, "mm_dn_o_dw")
    tok_b = comm.grads_begin("b", dict(w_out=g_w_out, cf_w_o=g_cf_w_o, dn_w_o=g_dn_w_o, ffn_conv_w=g_ffn_conv,
                                       cf_conv_w=g_cf_conv))
    do, d_proj, acc_gn = _dn_post_bwd(dog, o, proj, _tie(dn_norm_g, tok_b), d_proj)
    tok_b = comm.grads_continue("b", do)
    dq, dk, dv, dgr, dbr = _delta_bwd(qkvn, gb, _tie(gr, tok_b), bb, st_all, tm_all, do)
    d_proj, g_dn_conv = _dn_pre_bwd(dq, dk, dv, proj, dn_conv_w, d_proj)
    dgates = jnp.concatenate([dbr.reshape(H, S).T, dgr.reshape(H, S).T, jnp.zeros((S, LANE - 2 * H), F32)], axis=1)
    d_proj, acc_g = _gates_bwd(dgates, proj, alog_v, dtb_v, d_proj)
    g_w_in_p = _mm(d_proj, hn1, "tn", BF16, "mm_in_dw", tm=1152)
    tok_c = comm.grads_begin("c", dict(w_in=g_w_in_p, dn_conv_w=g_dn_conv))
    tok_c = comm.grads_continue("c", jnp.broadcast_to(tok_c, (8, LANE)))
    dhn1 = _mm(d_proj, w_in_p, "nn", F32, "mm_in_dx", tk=NINP // 3, dep=jnp.broadcast_to(tok_c, (8, LANE)))
    grad_x, acc1 = _norm_mod_bwd(dhn1, x, dx2, norm1_g, sc1, "norm_mod1_bwd")
    d_sh1, d_sc1, d_norm1_g = acc1[0:1], acc1[1:2], acc1[2:3]

    d_mod = jnp.concatenate([d_sh1, d_sc1, d_gt1, d_sh2, d_sc2, d_gt2], axis=1)
    small = dict(mod=d_mod, norm1_g=d_norm1_g, norm2_g=d_norm2_g, final_norm_g=d_final_g,
                 cf_ln_g=acc_ln[0:1], cf_ln_b=acc_ln[1:2], dn_norm_g=acc_gn[0:1],
                 dn_a_log=acc_g[0:1, H:2 * H], dn_dt_bias=acc_g[1:2, H:2 * H])
    return loss, grad_x, small


def _dev_index(px, py, pc):
    return 4 * px + 2 * py + pc


def _all_gather(arrs, name):
    n = len(arrs)

    def body(*refs):
        ins, outs = refs[:n], refs[n:2 * n]
        send_sems, recv_sems, loc_sems = refs[2 * n:]
        x, y, c = _my_pos()
        me, sib = (x, y, c), (x, y, 1 - c)
        chips = [(1 - x, y), (x, 1 - y), (1 - x, 1 - y)]

        def cp(i, k, block, to, src=None):
            dst = outs[i].at[_dev_index(*block)]
            return pltpu.make_async_remote_copy(
                src_ref=dst if src is None else src, dst_ref=dst, send_sem=send_sems.at[i, k],
                recv_sem=recv_sems.at[i, k], device_id=to, device_id_type=MESH)

        mine = [pltpu.make_async_copy(ins[i], outs[i].at[_dev_index(*me)], loc_sems.at[i]) for i in range(n)]
        for m in mine:
            m.start()
        sent = []
        for i in range(n):
            sent.append(cp(i, 0, me, sib, src=ins[i]))
            sent += [cp(i, 1 + j, me, (*chip, c), src=ins[i]) for j, chip in enumerate(chips)]
        for s in sent:
            s.start()
        for i in range(n):
            for j, chip in enumerate(chips):
                cp(i, 1 + j, (*chip, c), me).wait_recv()
                fwd = cp(i, 4 + j, (*chip, c), sib)
                fwd.start()
                sent.append(fwd)
        for i in range(n):
            cp(i, 0, sib, me).wait_recv()
            for j, chip in enumerate(chips):
                cp(i, 4 + j, (*chip, 1 - c), me).wait_recv()
        for s in sent:
            s.wait_send()
        for m in mine:
            m.wait()

    outs = pl.pallas_call(
        body, out_shape=[_sds((NDEV,) + a.shape, a.dtype) for a in arrs], in_specs=[ANY] * n, out_specs=[ANY] * n,
        scratch_shapes=[pltpu.SemaphoreType.DMA((n, 7)), pltpu.SemaphoreType.DMA((n, 7)), pltpu.SemaphoreType.DMA((n,))],
        name=name)(*arrs)
    return list(outs)


def _slab(ref, layout, idx):
    kind, n = layout
    if kind == "rows":
        return ref.at[pl.ds(pl.multiple_of(idx * n, n), n), :]
    if kind == "cols":
        return ref.at[:, pl.ds(pl.multiple_of(idx * n, n), n)]
    return ref.at[idx]


def _slab_shape(arr, layout):
    kind, n = layout
    if kind == "rows":
        return (n, arr.shape[1])
    if kind == "cols":
        return (arr.shape[0], n)
    return tuple(arr.shape[1:])


HBM = pl.BlockSpec(memory_space=pltpu.HBM)
SEMS = pl.BlockSpec(memory_space=pltpu.SEMAPHORE)
EFFECT = pltpu.SideEffectType.DATAFLOW_SIDE_EFFECTING
TOKEN = jax.ShapeDtypeStruct((8, LANE), F32)


def _hbm(a):
    return pltpu.with_memory_space_constraint(a, pltpu.HBM)


def _gather_ici_copy(shard_ref, buf_ref, layout, send_sems, recv_sems, i, j, me, chip, c):
    return pltpu.make_async_remote_copy(
        src_ref=shard_ref, dst_ref=_slab(buf_ref, layout, me), send_sem=send_sems.at[3 * i + j],
        recv_sem=recv_sems.at[3 * i + j], device_id=(*chip, c), device_id_type=MESH)


def _gather_ici_start(shards, bufs, layouts, after, name):
    n = len(shards)

    def body(*refs):
        sh, bf = refs[:n], refs[n:2 * n]
        send_sems, recv_sems = refs[2 * n + 1], refs[2 * n + 2]
        token = refs[-1]
        x, y, c = _my_pos()
        me = _dev_index(x, y, c)
        for i in range(n):
            for j, chip in enumerate([(1 - x, y), (x, 1 - y), (1 - x, 1 - y)]):
                _gather_ici_copy(sh[i], bf[i], layouts[i], send_sems, recv_sems, i, j, me, chip, c).start()
        token[...] = jnp.zeros_like(token)

    outs = pl.pallas_call(
        body, name=name,
        out_shape=(pltpu.SemaphoreType.DMA((3 * n,)), pltpu.SemaphoreType.DMA((3 * n,)),
                   *[pltpu.HBM(a.shape, a.dtype) for a in shards], *[pltpu.HBM(a.shape, a.dtype) for a in bufs], TOKEN),
        in_specs=[HBM] * (2 * n) + [ANY],
        out_specs=(SEMS, SEMS, *[HBM] * (2 * n), pl.BlockSpec(memory_space=pltpu.VMEM)),
        input_output_aliases={i: 2 + i for i in range(2 * n)},
        compiler_params=pltpu.CompilerParams(has_side_effects=EFFECT),
    )(*[_hbm(a) for a in shards], *[_hbm(a) for a in bufs], after)
    return outs[0], outs[1], list(outs[2:2 + n]), list(outs[2 + n:2 + 2 * n]), outs[-1]


def _gather_ici_wait(send_sems, recv_sems, shards, bufs, layouts, after, name):
    n = len(shards)

    def body(*refs):
        sh, bf = refs[:n], refs[n:2 * n]
        ssem, rsem = refs[2 * n], refs[2 * n + 1]
        x, y, c = _my_pos()
        me = _dev_index(x, y, c)
        for i in range(n):
            for j, chip in enumerate([(1 - x, y), (x, 1 - y), (1 - x, 1 - y)]):
                cp = _gather_ici_copy(sh[i], bf[i], layouts[i], ssem, rsem, i, j, me, chip, c)
                cp.wait_send()
                cp.wait_recv()

    outs = pl.pallas_call(
        body, name=name,
        out_shape=(*[pltpu.HBM(a.shape, a.dtype) for a in shards], *[pltpu.HBM(a.shape, a.dtype) for a in bufs]),
        in_specs=[HBM] * (2 * n) + [SEMS, SEMS, ANY], out_specs=tuple([HBM] * (2 * n)),
        input_output_aliases={i: i for i in range(2 * n)},
        compiler_params=pltpu.CompilerParams(has_side_effects=EFFECT),
    )(*shards, *bufs, send_sems, recv_sems, after)
    return list(outs[:n]), list(outs[n:])


def _place_own(pos, shard, buf, layout, name):
    kind, n = layout
    r, cols = shard.shape
    tr = _row_tile(r, shard.dtype.itemsize)
    nr = r // tr
    if kind == "rows":
        ospec = pl.BlockSpec((tr, cols), lambda i, p: (p[2] * nr + i, 0))
    else:
        assert kind == "lead"
        ospec = pl.BlockSpec((None, tr, cols), lambda i, p: (p[2], i, 0))

    def body(pos_ref, s_ref, buf_in, o_ref):
        o_ref[...] = s_ref[...]

    return pl.pallas_call(
        body, out_shape=_sds(buf.shape, buf.dtype), name=name, input_output_aliases={2: 0},
        grid_spec=pltpu.PrefetchScalarGridSpec(
            num_scalar_prefetch=1, grid=(nr,), in_specs=[pl.BlockSpec((tr, cols), lambda i, p: (i, 0)), ANY],
            out_specs=ospec),
        compiler_params=pltpu.CompilerParams(dimension_semantics=("parallel",), vmem_limit_bytes=VMEM_LIMIT),
    )(pos, shard, buf)


def _gather_pair(shards, bufs, layouts, name):
    n = len(shards)

    def body(*refs):
        sh, bo = refs[:n], refs[2 * n:3 * n]
        send_sems, recv_sems = refs[3 * n:]
        x, y, c = _my_pos()
        sib = (x, y, 1 - c)
        copies = []
        for i in range(n):
            for k, (px, py) in enumerate([(x, y), (1 - x, y), (x, 1 - y), (1 - x, 1 - y)]):
                slab = _slab(bo[i], layouts[i], _dev_index(px, py, c))
                copies.append(pltpu.make_async_remote_copy(
                    src_ref=sh[i] if k == 0 else slab, dst_ref=slab, send_sem=send_sems.at[i, k],
                    recv_sem=recv_sems.at[i, k], device_id=sib, device_id_type=MESH))
        for cpy in copies:
            cpy.start()
        for cpy in copies:
            cpy.wait()

    outs = pl.pallas_call(
        body, out_shape=[_sds(a.shape, a.dtype) for a in bufs], in_specs=[ANY] * (2 * n), out_specs=[ANY] * n,
        input_output_aliases={n + i: i for i in range(n)},
        scratch_shapes=[pltpu.SemaphoreType.DMA((n, 4)), pltpu.SemaphoreType.DMA((n, 4))], name=name)(*shards, *bufs)
    return list(outs)


def _pair_copy(part_ref, land_ref, layout, send_sems, recv_sems, i, q, x, y, c):
    return pltpu.make_async_remote_copy(
        src_ref=_slab(part_ref, layout, 2 * q + (1 - c)), dst_ref=land_ref.at[q], send_sem=send_sems.at[4 * i + q],
        recv_sem=recv_sems.at[4 * i + q], device_id=(x, y, 1 - c), device_id_type=MESH)


def _pair_exchange_start(parts, layouts, name):
    n = len(parts)
    lands = [lax.empty((4,) + _slab_shape(p, lay), p.dtype) for p, lay in zip(parts, layouts)]

    def body(*refs):
        pt, ld = refs[:n], refs[n:2 * n]
        send_sems, recv_sems = refs[2 * n], refs[2 * n + 1]
        token = refs[-1]
        x, y, c = _my_pos()
        for i in range(n):
            for q in range(4):
                _pair_copy(pt[i], ld[i], layouts[i], send_sems, recv_sems, i, q, x, y, c).start()
        token[...] = jnp.zeros_like(token)

    outs = pl.pallas_call(
        body, name=name,
        out_shape=(pltpu.SemaphoreType.DMA((4 * n,)), pltpu.SemaphoreType.DMA((4 * n,)),
                   *[pltpu.HBM(a.shape, a.dtype) for a in parts], *[pltpu.HBM(a.shape, a.dtype) for a in lands], TOKEN),
        in_specs=[HBM] * (2 * n), out_specs=(SEMS, SEMS, *[HBM] * (2 * n), pl.BlockSpec(memory_space=pltpu.VMEM)),
        input_output_aliases={i: 2 + i for i in range(2 * n)},
        compiler_params=pltpu.CompilerParams(has_side_effects=EFFECT),
    )(*[_hbm(a) for a in parts], *[_hbm(a) for a in lands])
    return outs[0], outs[1], list(outs[2:2 + n]), list(outs[2 + n:2 + 2 * n]), outs[-1]


def _pair_exchange_wait(send_sems, recv_sems, parts, lands, layouts, after, name):
    n = len(parts)

    def body(*refs):
        pt, ld = refs[:n], refs[n:2 * n]
        ssem, rsem = refs[2 * n], refs[2 * n + 1]
        x, y, c = _my_pos()
        for i in range(n):
            for q in range(4):
                cp = _pair_copy(pt[i], ld[i], layouts[i], ssem, rsem, i, q, x, y, c)
                cp.wait_send()
                cp.wait_recv()

    outs = pl.pallas_call(
        body, name=name,
        out_shape=(*[pltpu.HBM(a.shape, a.dtype) for a in parts], *[pltpu.HBM(a.shape, a.dtype) for a in lands]),
        in_specs=[HBM] * (2 * n) + [SEMS, SEMS, ANY], out_specs=tuple([HBM] * (2 * n)),
        input_output_aliases={i: i for i in range(2 * n)},
        compiler_params=pltpu.CompilerParams(has_side_effects=EFFECT),
    )(*parts, *lands, send_sems, recv_sems, after)
    return list(outs[:n]), list(outs[n:])


def _chip_copy(sum_ref, land_ref, send_sems, recv_sems, i, j, chip, c):
    return pltpu.make_async_remote_copy(
        src_ref=sum_ref.at[2 * chip[0] + chip[1]], dst_ref=land_ref.at[j], send_sem=send_sems.at[3 * i + j],
        recv_sem=recv_sems.at[3 * i + j], device_id=(*chip, c), device_id_type=MESH)


def _chip_exchange_start(sums, name):
    n = len(sums)
    lands = [lax.empty((3,) + s.shape[1:], s.dtype) for s in sums]

    def body(*refs):
        sm, ld = refs[:n], refs[n:2 * n]
        send_sems, recv_sems = refs[2 * n], refs[2 * n + 1]
        token = refs[-1]
        x, y, c = _my_pos()
        for i in range(n):
            for j, chip in enumerate([(1 - x, y), (x, 1 - y), (1 - x, 1 - y)]):
                _chip_copy(sm[i], ld[i], send_sems, recv_sems, i, j, chip, c).start()
        token[...] = jnp.zeros_like(token)

    outs = pl.pallas_call(
        body, name=name,
        out_shape=(pltpu.SemaphoreType.DMA((3 * n,)), pltpu.SemaphoreType.DMA((3 * n,)),
                   *[pltpu.HBM(a.shape, a.dtype) for a in sums], *[pltpu.HBM(a.shape, a.dtype) for a in lands], TOKEN),
        in_specs=[HBM] * (2 * n), out_specs=(SEMS, SEMS, *[HBM] * (2 * n), pl.BlockSpec(memory_space=pltpu.VMEM)),
        input_output_aliases={i: 2 + i for i in range(2 * n)},
        compiler_params=pltpu.CompilerParams(has_side_effects=EFFECT),
    )(*[_hbm(a) for a in sums], *[_hbm(a) for a in lands])
    return outs[0], outs[1], list(outs[2:2 + n]), list(outs[2 + n:2 + 2 * n]), outs[-1]


def _chip_exchange_wait(send_sems, recv_sems, sums, lands, after, name):
    n = len(sums)

    def body(*refs):
        sm, ld = refs[:n], refs[n:2 * n]
        ssem, rsem = refs[2 * n], refs[2 * n + 1]
        x, y, c = _my_pos()
        for i in range(n):
            for j, chip in enumerate([(1 - x, y), (x, 1 - y), (1 - x, 1 - y)]):
                cp = _chip_copy(sm[i], ld[i], ssem, rsem, i, j, chip, c)
                cp.wait_send()
                cp.wait_recv()

    outs = pl.pallas_call(
        body, name=name,
        out_shape=(*[pltpu.HBM(a.shape, a.dtype) for a in sums], *[pltpu.HBM(a.shape, a.dtype) for a in lands]),
        in_specs=[HBM] * (2 * n) + [SEMS, SEMS, ANY], out_specs=tuple([HBM] * (2 * n)),
        input_output_aliases={i: i for i in range(2 * n)},
        compiler_params=pltpu.CompilerParams(has_side_effects=EFFECT),
    )(*sums, *lands, send_sems, recv_sems, after)
    return list(outs[:n]), list(outs[n:])


def _row_tile(r, itemsize):
    align = 32 // itemsize
    best = r
    for t in range(align, min(r, 256) + 1, align):
        if r % t == 0:
            best = t
    return best


def _prefetch_call(body, name, out_shape, grid, in_specs, out_specs, sem):
    return pl.pallas_call(
        body, out_shape=out_shape, name=name,
        grid_spec=pltpu.PrefetchScalarGridSpec(num_scalar_prefetch=1, grid=grid, in_specs=in_specs, out_specs=out_specs),
        compiler_params=pltpu.CompilerParams(dimension_semantics=sem, vmem_limit_bytes=VMEM_LIMIT))


def _pair_sum(pos, part, got, layout, name):
    kind, _ = layout
    _, r, cols = got.shape
    tr, tc = _tiles(r, cols, part.dtype.itemsize)
    nr, nc = r // tr, cols // tc
    if kind == "rows":
        pspec = pl.BlockSpec((tr, tc), lambda q, i, j, p: ((2 * q + p[0]) * nr + i, j))
    elif kind == "cols":
        pspec = pl.BlockSpec((tr, tc), lambda q, i, j, p: (i, (2 * q + p[0]) * nc + j))
    else:
        pspec = pl.BlockSpec((None, tr, tc), lambda q, i, j, p: (2 * q + p[0], i, j))

    def body(pos_ref, p_ref, g_ref, o_ref):
        o_ref[...] = (p_ref[...].astype(F32) + g_ref[...].astype(F32)).astype(o_ref.dtype)

    blk = pl.BlockSpec((None, tr, tc), lambda q, i, j, p: (q, i, j))
    return _prefetch_call(body, name, _sds((4, r, cols), part.dtype), (4, nr, nc), [pspec, blk], blk,
                          ("parallel", "parallel", "parallel"))(pos, part, got)


def _tiles(r, cols, itemsize):
    tr = _row_tile(r, itemsize)
    if tr < r or r * cols * 4 <= (2 << 20) or cols % 256:
        return tr, cols
    return r, 256


def _final_sum_adam(pos, sums, got, w, m, v, name):
    _, r, cols = w.shape
    tr, tc = _tiles(r, cols, sums.dtype.itemsize)

    def body(pos_ref, s_ref, g_ref, w_ref, m_ref, v_ref, go_ref, dl_ref, nm_ref, nv_ref):
        g = ((s_ref[...].astype(F32) + g_ref[0].astype(F32)) + g_ref[1].astype(F32)) + g_ref[2].astype(F32)
        dl, nm, nv = _adam(w_ref[...], g, m_ref[...], v_ref[...])
        go_ref[...] = g
        dl_ref[...] = dl
        nm_ref[...] = nm
        nv_ref[...] = nv

    big = pl.BlockSpec((None, tr, tc), lambda i, j, p: (0, i, j))
    return _prefetch_call(body, name, [_sds((1, r, cols))] * 4, (r // tr, cols // tc),
                          [pl.BlockSpec((None, tr, tc), lambda i, j, p: (p[1], i, j)),
                           pl.BlockSpec((3, tr, tc), lambda i, j, p: (0, i, j)), big, big, big],
                          [big] * 4, ("parallel", "parallel"))(pos, sums, got, w, m, v)


def _small_adam(g_all, w, m, v):
    npk = w.shape[1]

    def body(g_ref, w_ref, m_ref, v_ref, go_ref, dl_ref, nm_ref, nv_ref):
        g = g_ref[0:1, :]
        for k in range(1, NDEV):
            g = g + g_ref[k:k + 1, :]
        dl, nm, nv = _adam(w_ref[...], g, m_ref[...], v_ref[...])
        go_ref[...] = g
        dl_ref[...] = dl
        nm_ref[...] = nm
        nv_ref[...] = nv

    return _call(body, "small_adam", [_sds((1, npk))] * 4)(g_all, w, m, v)


SMALL = [("b_ada", 6 * D), ("norm1_g", D), ("norm2_g", D), ("final_norm_g", D), ("cf_ln_g", CFW), ("cf_ln_b", CFW),
         ("dn_norm_g", DH), ("dn_a_log", H), ("dn_dt_bias", H)]
LATE = ["dn_w_o", "cf_w_o", "w_out", "ffn_w_up", "ffn_w_down"]
LATE_SHAPE = {"dn_w_o": (NDEV, DNW, D // NDEV), "cf_w_o": (NDEV, CFW, D // NDEV), "w_out": (D, D),
              "ffn_w_up": (NDEV, D, 2 * FFN // NDEV), "ffn_w_down": (FFN, D)}
LATE_LAYOUT = {"dn_w_o": ("lead", NDEV), "cf_w_o": ("lead", NDEV), "w_out": ("rows", D // NDEV),
               "ffn_w_up": ("lead", NDEV), "ffn_w_down": ("rows", FFN // NDEV)}
LAYOUT = {"dn_w_o": ("cols", D // NDEV), "cf_w_o": ("cols", D // NDEV), "w_out": ("rows", D // NDEV),
          "ffn_w_up": ("cols", 2 * FFN // NDEV), "ffn_w_down": ("rows", FFN // NDEV),
          "w_in": ("lead", NDEV), "dn_conv_w": ("lead", NDEV), "cf_conv_w": ("lead", NDEV), "ffn_conv_w": ("lead", NDEV)}
NAMES = ["w_ada", "b_ada", "norm1_g", "w_in", "dn_conv_w", "dn_a_log", "dn_dt_bias", "dn_norm_g", "dn_w_o", "cf_conv_w",
         "cf_ln_g", "cf_ln_b", "cf_w_o", "w_out", "norm2_g", "ffn_w_up", "ffn_conv_w", "ffn_w_down", "final_norm_g"]


def _pack_small(d):
    rows = []
    for nm, n in SMALL:
        row = d[nm].reshape(1, n)
        pad = (-n) % LANE
        rows.append(jnp.pad(row, ((0, 0), (0, pad))) if pad else row)
    return jnp.concatenate(rows, axis=1)


def _unpack_small(row, shapes):
    out, off = {}, 0
    for nm, n in SMALL:
        out[nm] = row[0, off:off + n].reshape(shapes[nm])
        off += n + ((-n) % LANE)
    return out


def _cols_from_gathered(g):
    return jnp.transpose(g, (1, 0, 2)).reshape(g.shape[1], NDEV * g.shape[2])


def _cols_to_parts(full):
    r, ctot = full.shape
    return jnp.transpose(full.reshape(r, NDEV, ctot // NDEV), (1, 0, 2))


def kernel(x, c, w_ada, b_ada, norm1_g, w_in, dn_conv_w, dn_a_log, dn_dt_bias, dn_norm_g, dn_w_o, cf_conv_w, cf_ln_g, cf_ln_b, cf_w_o, w_out, norm2_g, ffn_w_up, ffn_conv_w, ffn_w_down, final_norm_g, loss_target, m_w_ada, m_b_ada, m_norm1_g, m_w_in, m_dn_conv_w, m_dn_a_log, m_dn_dt_bias, m_dn_norm_g, m_dn_w_o, m_cf_conv_w, m_cf_ln_g, m_cf_ln_b, m_cf_w_o, m_w_out, m_norm2_g, m_ffn_w_up, m_ffn_conv_w, m_ffn_w_down, m_final_norm_g, v_w_ada, v_b_ada, v_norm1_g, v_w_in, v_dn_conv_w, v_dn_a_log, v_dn_dt_bias, v_dn_norm_g, v_dn_w_o, v_cf_conv_w, v_cf_ln_g, v_cf_ln_b, v_cf_w_o, v_w_out, v_norm2_g, v_ffn_w_up, v_ffn_conv_w, v_ffn_w_down, v_final_norm_g):
    args = locals()
    w = {nm: args[nm] for nm in NAMES}
    mo = {nm: args["m_" + nm] for nm in NAMES}
    vo = {nm: args["v_" + nm] for nm in NAMES}
    shapes = {nm: w[nm].shape for nm in NAMES}
    px, py, pc = _my_pos()
    me = _dev_index(px, py, pc)

    def mat(a):
        return a.reshape(a.shape[-2:])

    pos = jnp.stack([pc, 2 * px + py, me]).astype(jnp.int32)

    first = ["w_in", "dn_conv_w", "cf_conv_w", "ffn_conv_w"]
    tr_in = lambda a: jnp.transpose(a, (0, 2, 1))
    got = _all_gather([tr_in(w["w_in"]).astype(BF16)] + [mat(w[nm]) for nm in first[1:]] + [c], "gather_first")
    full = {nm: _cols_from_gathered(g) for nm, g in zip(first[1:], got[1:-1])}
    c_all = got[-1].reshape(NDEV, D)
    w_in_p = _pad_win(got[0].reshape(NIN, D))

    ncol = 6 * D // NDEV
    b_sh = lax.dynamic_slice(b_ada.reshape(1, 6 * D), (0, me * ncol), (1, ncol))
    mod_sh = _ada_fwd(c_all, mat(w_ada), b_sh)
    mod_all = _all_gather([mod_sh], "gather_mod")[0]
    mod = lax.dynamic_index_in_dim(mod_all, me, axis=1, keepdims=False).reshape(1, 6 * D)

    late = {}
    dep = mod_all
    for grp, names in (("mix", LATE[:3]), ("ffn", LATE[3:])):
        shards = [mat(w[nm]).astype(BF16) for nm in names]
        lays = [LATE_LAYOUT[nm] for nm in names]
        bufs = [_place_own(pos, s, lax.empty(LATE_SHAPE[nm], BF16), lay, "place_" + nm)
                for nm, s, lay in zip(names, shards, lays)]
        started = _gather_ici_start(shards, bufs, lays, dep, "gather_" + grp + "_start")
        late[grp] = (lays,) + tuple(started)
        dep = started[4]

    res = {}

    class Comm:
        token0 = late["ffn"][5][0, 0]
        pending = {}

        @staticmethod
        def late_weights(grp, after):
            lays, ssem, rsem, shards, bufs, _ = late[grp]
            shards, bufs = _gather_ici_wait(ssem, rsem, shards, bufs, lays, after, "gather_" + grp + "_wait")
            return _gather_pair(shards, bufs, lays, "gather_" + grp + "_pair")

        @staticmethod
        def grads_begin(group, gd):
            names = list(gd)
            lays = [LAYOUT[nm] for nm in names]
            gl = []
            for nm in names:
                if nm == "w_in":
                    gl.append(_unpad_win(gd[nm]).reshape(NDEV, NSH, D))
                else:
                    gl.append(_cols_to_parts(gd[nm]) if LAYOUT[nm][0] == "lead" else gd[nm])
            started = _pair_exchange_start(gl, lays, "rs_pair_start_" + group)
            Comm.pending[group] = (names, lays) + tuple(started[:4])
            return started[4][0, 0]

        @staticmethod
        def grads_continue(group, after):
            names, lays, ssem, rsem, gl, lands = Comm.pending[group]
            if group == "c":
                after = Comm.finish("a", after)
            gl, from_sib = _pair_exchange_wait(ssem, rsem, gl, lands, lays, after, "rs_pair_wait_" + group)
            sums = [_pair_sum(pos, g, r, lay, "rs_pair_sum_" + nm) for nm, g, r, lay in zip(names, gl, from_sib, lays)]
            started = _chip_exchange_start(sums, "rs_chips_start_" + group)
            Comm.pending[group] = (names,) + tuple(started[:4])
            return started[4][0, 0]

        @staticmethod
        def finish(group, after):
            names, ssem, rsem, sums, lands = Comm.pending[group]
            sums, lands = _chip_exchange_wait(ssem, rsem, sums, lands, after, "rs_chips_wait_" + group)
            for nm, s, r in zip(names, sums, lands):
                if nm == "w_in":
                    outs = _final_sum_adam(pos, s, r, tr_in(w[nm]), tr_in(mo[nm]), tr_in(vo[nm]), "adam_" + nm)
                    res[nm] = [tr_in(o) for o in outs]
                else:
                    res[nm] = _final_sum_adam(pos, s, r, w[nm], mo[nm], vo[nm], "adam_" + nm)
            return res[names[-1]][0]

    vec = lambda a: a.reshape(1, -1)
    loss, grad_x, small = _local_step(
        x.reshape(S, D), loss_target.reshape(S, D), mod, vec(norm1_g), vec(norm2_g), vec(final_norm_g), w_in_p,
        full["dn_conv_w"], vec(dn_a_log), vec(dn_dt_bias), vec(dn_norm_g), full["cf_conv_w"], vec(cf_ln_g),
        vec(cf_ln_b), full["ffn_conv_w"], Comm)

    done_b = Comm.finish("b", grad_x)

    small["b_ada"] = small.pop("mod")
    packed = _pack_small(small) + 0.0 * done_b.reshape(-1)[0]
    g_small = _all_gather([packed], "gather_small")[0].reshape(NDEV, -1)
    outs = _small_adam(g_small, _pack_small({nm: w[nm] for nm, _ in SMALL}), _pack_small({nm: mo[nm] for nm, _ in SMALL}),
                       _pack_small({nm: vo[nm] for nm, _ in SMALL}))
    unpacked = [_unpack_small(o, shapes) for o in outs]
    for nm, _ in SMALL:
        res[nm] = [u[nm] for u in unpacked]

    dmod_sel = lax.dynamic_slice(g_small[:, :6 * D], (0, me * ncol), (NDEV, ncol))
    outs = _ada_bwd_adam(c_all, dmod_sel, mat(w_ada), mat(m_w_ada), mat(v_w_ada))
    res["w_ada"] = [o.reshape(shapes["w_ada"]) for o in outs]
    Comm.finish("c", jnp.concatenate([done_b.reshape(-1)[:LANE], outs[0].reshape(-1)[:LANE]]))

    loss = lax.psum(loss.reshape(()), ("x", "y", "c"))
    out = [loss, grad_x.reshape(x.shape)]
    for k in range(4):
        out += [res[nm][k] for nm in NAMES]
    return tuple(out)
```

```python
import functools

import jax
import jax.numpy as jnp
from jax import lax
from jax.experimental import pallas as pl
from jax.experimental.pallas import tpu as pltpu

F32 = jnp.float32
BF16 = jnp.bfloat16
HI = lax.Precision.HIGHEST
MESH = pl.DeviceIdType.MESH
ANY = pl.BlockSpec(memory_space=pl.ANY)

NDEV = 8
D = 2048
S = 2048
H = 8
DH = 128
DNW = H * DH
CFW = 1024
CFK = 31
DNK = 4
FFN = 5632
FFK = 3
CH = 64
NCH = S // CH
EPS = 1e-6
NIN = 10256
NINP = 10368
O_Z, O_GA, O_GB, O_GLU, O_SM = 3072, 4096, 6144, 8192, 10240
LANE = 128
TS = 256
VMEM_LIMIT = 56 * 1024 * 1024

ADAM_LR, ADAM_B1, ADAM_B2, ADAM_EPS, ADAM_WD, ADAM_STEP = 0.001, 0.9, 0.999, 1e-08, 0.01, 10


def _call(body, name, out_shape, grid=(), in_specs=None, out_specs=None, scratch=(), sem=None, aliases=None):
    kw = {}
    if aliases:
        kw["input_output_aliases"] = aliases
    if in_specs is not None:
        kw["in_specs"] = in_specs
    if out_specs is not None:
        kw["out_specs"] = out_specs
    return pl.pallas_call(
        body, out_shape=out_shape, grid=grid, scratch_shapes=scratch, name=name,
        compiler_params=pltpu.CompilerParams(dimension_semantics=sem, vmem_limit_bytes=VMEM_LIMIT), **kw)


def _sds(shape, dtype=F32):
    return jax.ShapeDtypeStruct(shape, dtype)


def _tile(dim, pref):
    if dim <= pref:
        return dim
    best = None
    for t in range(LANE, pref + 1, LANE):
        if dim % t == 0:
            best = t
    assert best is not None, (dim, pref)
    return best


def _sigmoid(x):
    return 1.0 / (1.0 + jnp.exp(-x))


def _silu(x):
    return x * _sigmoid(x)


def _dsilu(x):
    s = _sigmoid(x)
    return s * (1.0 + x * (1.0 - s))


def _silu_both(x):
    s = _sigmoid(x)
    return x * s, s * (1.0 + x * (1.0 - s))


def _softplus(x):
    return jnp.maximum(x, 0.0) + jnp.log(1.0 + jnp.exp(-jnp.abs(x)))


def _dot(a, b, dims, precision=None):
    return lax.dot_general(a, b, (dims, ((), ())), preferred_element_type=F32, precision=precision)


NN = ((1,), (0,))
NT = ((1,), (1,))
TN = ((0,), (0,))


def _my_pos():
    return lax.axis_index("x"), lax.axis_index("y"), lax.axis_index("c")


def _mm(a, b, mode, out_dtype, name, tm=1024, tn=1024, tk=2048, a2=None, b2=None, dep=None):
    sharded = b.ndim == 3
    if sharded and mode == "nn":
        cs = b.shape[2]
        (m, k), n = a.shape, NDEV * cs
        gs = max(1, tn // cs)
        tm, tn, tk = _tile(m, tm), gs * cs, _tile(k, tk)
    elif sharded:
        assert mode == "nt"
        cs = b.shape[2]
        m, n, k = a.shape[0], b.shape[1], NDEV * cs
        gs = max(1, tk // cs)
        tm, tn, tk = _tile(m, tm), _tile(n, tn), gs * cs
    else:
        if mode == "nn":
            (m, k), (k2, n) = a.shape, b.shape
        elif mode == "nt":
            (m, k), (n, k2) = a.shape, b.shape
        else:
            (k, m), (k2, n) = a.shape, b.shape
        assert k == k2, (a.shape, b.shape, mode)
        n = n * (2 if b2 is not None else 1)
        tm, tn, tk = _tile(m, tm), _tile(n // (2 if b2 is not None else 1), tn), _tile(k, tk)
    nk, nj = k // tk, n // tn
    halfk, halfj = nk // 2, nj // 2
    dims = {"nn": NN, "nt": NT, "tn": TN}[mode]

    n_in = 2 + (a2 is not None) + (b2 is not None) + (dep is not None)

    def body(*refs):
        a_ref, b_ref = refs[0], refs[1]
        x_ref = refs[2] if (a2 is not None or b2 is not None) else None
        o_ref = refs[n_in]
        acc_ref = refs[n_in + 1] if nk > 1 else None
        j, kk = pl.program_id(1), pl.program_id(2)

        if nk > 1:
            @pl.when(kk == 0)
            def _():
                acc_ref[...] = jnp.zeros_like(acc_ref)

        def accumulate(product, cols=slice(None)):
            if nk == 1:
                o_ref[:, cols] = product().astype(o_ref.dtype)
            else:
                acc_ref[:, cols] += product()

        if sharded and mode == "nn":
            for q in range(gs):
                accumulate(lambda q=q: _dot(a_ref[...], b_ref[q], NN), slice(q * cs, (q + 1) * cs))
        elif sharded:
            def contract(lhs_ref):
                def product():
                    part = None
                    for q in range(gs):
                        term = _dot(lhs_ref[:, q * cs:(q + 1) * cs], b_ref[q], NT)
                        part = term if part is None else part + term
                    return part
                accumulate(product)

            if a2 is None:
                contract(a_ref)
            else:
                pl.when(kk < halfk)(lambda: contract(a_ref))
                pl.when(kk >= halfk)(lambda: contract(x_ref))
        elif b2 is not None:
            pl.when(j < halfj)(lambda: accumulate(lambda: _dot(a_ref[...], b_ref[...], dims)))
            pl.when(j >= halfj)(lambda: accumulate(lambda: _dot(a_ref[...], x_ref[...], dims)))
        else:
            accumulate(lambda: _dot(a_ref[...], b_ref[...], dims))

        if nk > 1:
            @pl.when(kk == nk - 1)
            def _():
                o_ref[...] = acc_ref[...].astype(o_ref.dtype)

    ins, in_specs = [a], []
    if mode == "tn":
        in_specs.append(pl.BlockSpec((tk, tm), lambda i, j, kk: (kk, i)))
    elif a2 is not None:
        in_specs.append(pl.BlockSpec((tm, tk), lambda i, j, kk: (i, jnp.minimum(kk, halfk - 1))))
    else:
        in_specs.append(pl.BlockSpec((tm, tk), lambda i, j, kk: (i, kk)))
    ins.append(b)
    if sharded and mode == "nn":
        in_specs.append(pl.BlockSpec((gs, tk, cs), lambda i, j, kk: (j, kk, 0)))
    elif sharded:
        in_specs.append(pl.BlockSpec((gs, tn, cs), lambda i, j, kk: (kk, j, 0)))
    elif mode == "nt":
        in_specs.append(pl.BlockSpec((tn, tk), lambda i, j, kk: (j, kk)))
    elif b2 is not None:
        in_specs.append(pl.BlockSpec((tk, tn), lambda i, j, kk: (kk, jnp.minimum(j, halfj - 1))))
    else:
        in_specs.append(pl.BlockSpec((tk, tn), lambda i, j, kk: (kk, j)))
    if a2 is not None:
        ins.append(a2)
        in_specs.append(pl.BlockSpec((tm, tk), lambda i, j, kk: (i, jnp.maximum(kk - halfk, 0))))
    if b2 is not None:
        ins.append(b2)
        in_specs.append(pl.BlockSpec((tk, tn), lambda i, j, kk: (kk, jnp.maximum(j - halfj, 0))))
    if dep is not None:
        ins.append(dep)
        in_specs.append(ANY)
    return _call(body, name, _sds((m, n), out_dtype), grid=(m // tm, nj, nk),
                 in_specs=in_specs, out_specs=pl.BlockSpec((tm, tn), lambda i, j, kk: (i, j)),
                 scratch=[pltpu.VMEM((tm, tn), F32)] if nk > 1 else [],
                 sem=("parallel", "parallel", "arbitrary"))(*ins)


def _ada_fwd(c_all, w_sh, b_sh):
    n = w_sh.shape[1]
    tn = 512

    def body(c_ref, w_ref, b_ref, o_ref):
        ca = _silu(c_ref[...]).astype(BF16)
        o_ref[...] = _dot(ca, w_ref[...].astype(BF16), NN) + b_ref[...]

    return _call(body, "ada_fwd", _sds((NDEV, n)), grid=(n // tn,),
                 in_specs=[pl.BlockSpec((NDEV, D), lambda j: (0, 0)), pl.BlockSpec((D, tn), lambda j: (0, j)),
                           pl.BlockSpec((1, tn), lambda j: (0, j))],
                 out_specs=pl.BlockSpec((NDEV, tn), lambda j: (0, j)), sem=("parallel",))(c_all, w_sh, b_sh)


def _adam(w, g, m, v):
    m = ADAM_B1 * m + (1.0 - ADAM_B1) * g
    v = ADAM_B2 * v + (1.0 - ADAM_B2) * (g * g)
    m_hat = m / (1.0 - ADAM_B1 ** ADAM_STEP)
    v_hat = v / (1.0 - ADAM_B2 ** ADAM_STEP)
    delta = -ADAM_LR * (m_hat / (jnp.sqrt(v_hat) + ADAM_EPS) + ADAM_WD * w)
    return delta, m, v


def _ada_bwd_adam(c_all, dmod_sel, w, m, v):
    r, n = w.shape
    tr = 256

    def body(c_ref, d_ref, w_ref, m_ref, v_ref, g_ref, dl_ref, nm_ref, nv_ref):
        ca = _silu(c_ref[...])
        g = _dot(ca, d_ref[...], TN, precision=HI)
        dl, nm, nv = _adam(w_ref[...], g, m_ref[...], v_ref[...])
        g_ref[...] = g
        dl_ref[...] = dl
        nm_ref[...] = nm
        nv_ref[...] = nv

    big = pl.BlockSpec((tr, n), lambda i: (i, 0))
    return _call(body, "ada_bwd_adam", [_sds((r, n))] * 4, grid=(r // tr,),
                 in_specs=[pl.BlockSpec((NDEV, tr), lambda i: (0, i)), pl.BlockSpec((NDEV, n), lambda i: (0, 0)),
                           big, big, big],
                 out_specs=[big] * 4, sem=("parallel",))(c_all, dmod_sel, w, m, v)


def _row_spec(width=D):
    return pl.BlockSpec((TS, width), lambda i: (i, 0))


def _vec_spec(width=D):
    return pl.BlockSpec((1, width), lambda i: (0, 0))


def _acc_spec(width=D):
    return pl.BlockSpec((8, width), lambda i: (0, 0))


def _norm_mod(x, g, sc, sh, name):
    def body(x_ref, g_ref, sc_ref, sh_ref, o_ref):
        xv = x_ref[...]
        r = lax.rsqrt(jnp.mean(xv * xv, axis=-1, keepdims=True) + EPS)
        o_ref[...] = ((xv * r) * g_ref[...] * (1.0 + sc_ref[...]) + sh_ref[...]).astype(BF16)

    return _call(body, name, _sds((S, D), BF16), grid=(S // TS,),
                 in_specs=[_row_spec(), _vec_spec(), _vec_spec(), _vec_spec()], out_specs=_row_spec(),
                 sem=("parallel",))(x, g, sc, sh)


def _resid_norm_mod(x, mix, gt, g, sc, sh, name):
    def body(x_ref, mix_ref, gt_ref, g_ref, sc_ref, sh_ref, x2_ref, o_ref):
        xv = x_ref[...] + gt_ref[...] * mix_ref[...]
        x2_ref[...] = xv
        r = lax.rsqrt(jnp.mean(xv * xv, axis=-1, keepdims=True) + EPS)
        o_ref[...] = ((xv * r) * g_ref[...] * (1.0 + sc_ref[...]) + sh_ref[...]).astype(BF16)

    return _call(body, name, [_sds((S, D)), _sds((S, D), BF16)], grid=(S // TS,),
                 in_specs=[_row_spec(), _row_spec()] + [_vec_spec()] * 4, out_specs=[_row_spec(), _row_spec()],
                 sem=("parallel",))(x, mix, gt, g, sc, sh)


def _acc_rows(acc_ref, rows):
    @pl.when(pl.program_id(0) == 0)
    def _():
        acc_ref[...] = jnp.zeros_like(acc_ref)

    for k, row in enumerate(rows):
        acc_ref[k:k + 1, :] += row


def _loss_head(x2, f, tgt, gt2, gf):
    def body(x2_ref, f_ref, t_ref, gt_ref, gf_ref, dx_ref, df_ref, acc_ref):
        fv = f_ref[...]
        x3 = x2_ref[...] + gt_ref[...] * fv
        r = lax.rsqrt(jnp.mean(x3 * x3, axis=-1, keepdims=True) + EPS)
        xn = x3 * r
        e = xn * gf_ref[...] - t_ref[...]
        loss = 0.5 * jnp.sum(jnp.mean(e * e, axis=-1, keepdims=True), axis=0, keepdims=True)
        dy = e * (1.0 / D)
        dxn = dy * gf_ref[...]
        dx3 = r * (dxn - xn * jnp.mean(dxn * xn, axis=-1, keepdims=True))
        dx_ref[...] = dx3
        df_ref[...] = (dx3 * gt_ref[...]).astype(BF16)
        _acc_rows(acc_ref, [jnp.sum(dy * xn, axis=0, keepdims=True), jnp.sum(dx3 * fv, axis=0, keepdims=True),
                            jnp.broadcast_to(loss, (1, D))])

    return _call(body, "loss_head", [_sds((S, D)), _sds((S, D), BF16), _sds((8, D))], grid=(S // TS,),
                 in_specs=[_row_spec(), _row_spec(), _row_spec(), _vec_spec(), _vec_spec()],
                 out_specs=[_row_spec(), _row_spec(), _acc_spec()], sem=("arbitrary",))(x2, f, tgt, gt2, gf)


def _norm_mod_bwd(dhn, x, dres, g, sc, name, mix=None, gt=None):
    gated = mix is not None

    def body(*refs):
        if gated:
            dhn_ref, x_ref, dres_ref, g_ref, sc_ref, mix_ref, gt_ref, dx_ref, dmix_ref, acc_ref = refs
        else:
            dhn_ref, x_ref, dres_ref, g_ref, sc_ref, dx_ref, acc_ref = refs
        xv = x_ref[...]
        dh = dhn_ref[...]
        r = lax.rsqrt(jnp.mean(xv * xv, axis=-1, keepdims=True) + EPS)
        xn = xv * r
        gv = g_ref[...]
        sc1 = 1.0 + sc_ref[...]
        dxn = dh * gv * sc1
        dx = dres_ref[...] + r * (dxn - xn * jnp.mean(dxn * xn, axis=-1, keepdims=True))
        dx_ref[...] = dx
        rows = [jnp.sum(dh, axis=0, keepdims=True), jnp.sum(dh * xn * gv, axis=0, keepdims=True),
                jnp.sum(dh * xn * sc1, axis=0, keepdims=True)]
        if gated:
            rows.append(jnp.sum(dx * mix_ref[...], axis=0, keepdims=True))
            dmix_ref[...] = (dx * gt_ref[...]).astype(BF16)
        _acc_rows(acc_ref, rows)

    ins = [dhn, x, dres, g, sc]
    in_specs = [_row_spec(), _row_spec(), _row_spec(), _vec_spec(), _vec_spec()]
    outs = [_sds((S, D))]
    out_specs = [_row_spec()]
    if gated:
        ins += [mix, gt]
        in_specs += [_row_spec(), _vec_spec()]
        outs.append(_sds((S, D), BF16))
        out_specs.append(_row_spec())
    outs.append(_sds((8, D)))
    out_specs.append(_acc_spec())
    return _call(body, name, outs, grid=(S // TS,), in_specs=in_specs, out_specs=out_specs,
                 sem=("arbitrary",))(*ins)


RC = 64


RC_WIDE = 256


def _conv_fwd_rows(pad_ref, w_ref, kw, head, r0, rc=RC):
    acc = None
    for k in range(kw):
        term = w_ref[k:k + 1, :] * pad_ref[pl.ds(head - (kw - 1) + k + r0, rc), :]
        acc = term if acc is None else acc + term
    return acc


def _conv_bwd_rows(pad2_ref, w_ref, kw, r0, rc=RC):
    acc = None
    for k in range(kw):
        term = w_ref[k:k + 1, :] * pad2_ref[pl.ds(kw - 1 - k + r0, rc), :]
        acc = term if acc is None else acc + term
    return acc


def _conv_dw(pad_ref, dout_ref, dw_ref, kw, head, rc=RC):
    for k in range(kw):
        acc = None
        for r0 in range(0, S, rc):
            term = jnp.sum(pad_ref[pl.ds(head - (kw - 1) + k + r0, rc), :] * dout_ref[pl.ds(r0, rc), :],
                           axis=0, keepdims=True)
            acc = term if acc is None else acc + term
        dw_ref[k:k + 1, :] = acc


def _col_spec(width, off_blocks=0):
    return pl.BlockSpec((S, width), lambda j: (0, j + off_blocks))


def _dn_pre_fwd(proj, conv_w):
    head = 8

    def body(x_ref, w_ref, o_ref, pad_ref):
        j = pl.program_id(0)
        pad_ref[pl.ds(0, head), :] = jnp.zeros((head, DH), F32)
        pad_ref[pl.ds(head, S), :] = x_ref[...]
        scale = jnp.where(j < H, DH ** -0.5, 1.0)
        for r0 in range(0, S, RC):
            y = _silu(_conv_fwd_rows(pad_ref, w_ref, DNK, head, r0))
            rinv = lax.rsqrt(jnp.sum(y * y, axis=-1, keepdims=True) + EPS)
            o_ref[pl.ds(r0, RC), :] = jnp.where(j < 2 * H, y * rinv * scale, y)

    return _call(body, "dn_pre_fwd", _sds((S, 3 * DNW)), grid=(3 * H,),
                 in_specs=[_col_spec(DH), pl.BlockSpec((DNK, DH), lambda j: (0, j))], out_specs=_col_spec(DH),
                 scratch=[pltpu.VMEM((S + head, DH), F32)], sem=("parallel",))(proj, conv_w)


def _dn_pre_bwd(dq, dk, dv, proj, conv_w, dproj):
    head = 8

    def body(dq_ref, dk_ref, dv_ref, x_ref, w_ref, dproj_in, dx_ref, dw_ref, pad_ref, pad2_ref):
        j = pl.program_id(0)
        pad_ref[pl.ds(0, head), :] = jnp.zeros((head, DH), F32)
        pad_ref[pl.ds(head, S), :] = x_ref[...]
        pad2_ref[pl.ds(S, head), :] = jnp.zeros((head, DH), F32)
        scale = jnp.where(j < H, DH ** -0.5, 1.0)
        for r0 in range(0, S, RC_WIDE):
            xc = _conv_fwd_rows(pad_ref, w_ref, DNK, head, r0, RC_WIDE)
            y, dy_dxc = _silu_both(xc)
            rinv = lax.rsqrt(jnp.sum(y * y, axis=-1, keepdims=True) + EPS)
            yn = y * rinv
            rows = pl.ds(r0, RC_WIDE)
            do = jnp.where(j < H, dq_ref[rows, :], jnp.where(j < 2 * H, dk_ref[rows, :], dv_ref[rows, :]))
            dy_n = scale * rinv * (do - yn * jnp.sum(do * yn, axis=-1, keepdims=True))
            dy = jnp.where(j < 2 * H, dy_n, do)
            pad2_ref[rows, :] = dy * dy_dxc
        for r0 in range(0, S, RC_WIDE):
            dx_ref[pl.ds(r0, RC_WIDE), :] = _conv_bwd_rows(pad2_ref, w_ref, DNK, r0, RC_WIDE).astype(BF16)
        _conv_dw(pad_ref, pad2_ref, dw_ref, DNK, head, RC_WIDE)

    wspec = pl.BlockSpec((DNK, DH), lambda j: (0, j))
    head_col = lambda lo: pl.BlockSpec((S, DH), lambda j: (0, jnp.clip(j - lo, 0, H - 1)))
    return _call(body, "dn_pre_bwd", [_sds((S, NINP), BF16), _sds((DNK, 3 * DNW))], grid=(3 * H,),
                 in_specs=[head_col(0), head_col(H), head_col(2 * H), _col_spec(DH), wspec, ANY],
                 out_specs=[_col_spec(DH), wspec],
                 scratch=[pltpu.VMEM((S + head, DH), F32), pltpu.VMEM((S + head, DH), F32)],
                 sem=("parallel",), aliases={5: 0})(dq, dk, dv, proj, conv_w, dproj)


CF_HEAD = 32
CF_VAL = pl.BlockSpec((S, LANE), lambda j: (0, O_GLU // LANE + 2 * j))
CF_GL = pl.BlockSpec((S, LANE), lambda j: (0, O_GLU // LANE + 2 * j + 1))


def _cf_conv_fwd(proj, conv_w):
    def body(val_ref, gl_ref, w_ref, o_ref, pad_ref):
        pad_ref[pl.ds(0, CF_HEAD), :] = jnp.zeros((CF_HEAD, LANE), F32)
        pad_ref[pl.ds(CF_HEAD, S), :] = val_ref[...] * _sigmoid(gl_ref[...])
        for r0 in range(0, S, RC):
            o_ref[pl.ds(r0, RC), :] = _conv_fwd_rows(pad_ref, w_ref, CFK, CF_HEAD, r0)

    wspec = pl.BlockSpec((CFK, LANE), lambda j: (0, j))
    return _call(body, "cf_conv_fwd", _sds((S, CFW)), grid=(CFW // LANE,),
                 in_specs=[CF_VAL, CF_GL, wspec], out_specs=_col_spec(LANE),
                 scratch=[pltpu.VMEM((S + CF_HEAD, LANE), F32)], sem=("parallel",))(proj, proj, conv_w)


def _cf_conv_bwd(du1, proj, conv_w, dproj):
    def body(d_ref, val_ref, gl_ref, w_ref, dproj_in, dp_ref, dw_ref, pad_ref, pad2_ref):
        sg = _sigmoid(gl_ref[...])
        pad_ref[pl.ds(0, CF_HEAD), :] = jnp.zeros((CF_HEAD, LANE), F32)
        pad_ref[pl.ds(CF_HEAD, S), :] = val_ref[...] * sg
        pad2_ref[pl.ds(0, S), :] = d_ref[...]
        pad2_ref[pl.ds(S, CF_HEAD), :] = jnp.zeros((CF_HEAD, LANE), F32)
        for r0 in range(0, S, RC):
            du0 = _conv_bwd_rows(pad2_ref, w_ref, CFK, r0)
            rows = pl.ds(r0, RC)
            sgr = _sigmoid(gl_ref[rows, :])
            dp_ref[rows, 0:LANE] = (du0 * sgr).astype(BF16)
            dp_ref[rows, LANE:2 * LANE] = (du0 * val_ref[rows, :] * sgr * (1.0 - sgr)).astype(BF16)
        _conv_dw(pad_ref, pad2_ref, dw_ref, CFK, CF_HEAD)

    wspec = pl.BlockSpec((CFK, LANE), lambda j: (0, j))
    return _call(body, "cf_conv_bwd", [_sds((S, NINP), BF16), _sds((CFK, CFW))], grid=(CFW // LANE,),
                 in_specs=[_col_spec(LANE), CF_VAL, CF_GL, wspec, ANY],
                 out_specs=[pl.BlockSpec((S, 2 * LANE), lambda j: (0, O_GLU // (2 * LANE) + j)), wspec],
                 scratch=[pltpu.VMEM((S + CF_HEAD, LANE), F32), pltpu.VMEM((S + CF_HEAD, LANE), F32)],
                 sem=("parallel",), aliases={4: 0})(du1, proj, proj, conv_w, dproj)


def _cf_ln_fwd(u1, g, b):
    def body(u_ref, g_ref, b_ref, o_ref):
        u = u_ref[...]
        mu = jnp.mean(u, axis=-1, keepdims=True)
        xc = u - mu
        y = xc * lax.rsqrt(jnp.mean(xc * xc, axis=-1, keepdims=True) + EPS)
        o_ref[...] = _silu(y * g_ref[...] + b_ref[...]).astype(BF16)

    return _call(body, "cf_ln_fwd", _sds((S, CFW), BF16), grid=(S // TS,),
                 in_specs=[_row_spec(CFW), _vec_spec(CFW), _vec_spec(CFW)], out_specs=_row_spec(CFW),
                 sem=("parallel",))(u1, g, b)


def _cf_ln_bwd(du3, u1, g, b):
    def body(d_ref, u_ref, g_ref, b_ref, du_ref, acc_ref):
        u = u_ref[...]
        mu = jnp.mean(u, axis=-1, keepdims=True)
        xc = u - mu
        rstd = lax.rsqrt(jnp.mean(xc * xc, axis=-1, keepdims=True) + EPS)
        xh = xc * rstd
        du2 = d_ref[...] * _dsilu(xh * g_ref[...] + b_ref[...])
        dxh = du2 * g_ref[...]
        du_ref[...] = rstd * (dxh - jnp.mean(dxh, axis=-1, keepdims=True)
                              - xh * jnp.mean(dxh * xh, axis=-1, keepdims=True))
        _acc_rows(acc_ref, [jnp.sum(du2 * xh, axis=0, keepdims=True), jnp.sum(du2, axis=0, keepdims=True)])

    return _call(body, "cf_ln_bwd", [_sds((S, CFW)), _sds((8, CFW))], grid=(S // TS,),
                 in_specs=[_row_spec(CFW), _row_spec(CFW), _vec_spec(CFW), _vec_spec(CFW)],
                 out_specs=[_row_spec(CFW), _acc_spec(CFW)], sem=("arbitrary",))(du3, u1, g, b)


FB = 256
FNB = FFN // FB
FF_HEAD = 8


def _ffn_mid_fwd(upall, conv_w):
    def body(gate_ref, up_ref, w_ref, o_ref, pad_ref):
        pad_ref[pl.ds(0, FF_HEAD), :] = jnp.zeros((FF_HEAD, FB), F32)
        pad_ref[pl.ds(FF_HEAD, S), :] = gate_ref[...]
        for r0 in range(0, S, RC):
            gc = _conv_fwd_rows(pad_ref, w_ref, FFK, FF_HEAD, r0)
            o_ref[pl.ds(r0, RC), :] = (_silu(gc) * up_ref[pl.ds(r0, RC), :]).astype(BF16)

    wspec = pl.BlockSpec((FFK, FB), lambda j: (0, j))
    return _call(body, "ffn_mid_fwd", _sds((S, FFN), BF16), grid=(FNB,),
                 in_specs=[_col_spec(FB), _col_spec(FB, FNB), wspec], out_specs=_col_spec(FB),
                 scratch=[pltpu.VMEM((S + FF_HEAD, FB), F32)], sem=("parallel",))(upall, upall, conv_w)


def _ffn_mid_bwd(dh, upall, conv_w):
    def body(d_ref, gate_ref, up_ref, w_ref, dgate_ref, dup_ref, dw_ref, pad_ref, pad2_ref):
        pad_ref[pl.ds(0, FF_HEAD), :] = jnp.zeros((FF_HEAD, FB), F32)
        pad_ref[pl.ds(FF_HEAD, S), :] = gate_ref[...]
        pad2_ref[pl.ds(S, FF_HEAD), :] = jnp.zeros((FF_HEAD, FB), F32)
        for r0 in range(0, S, RC):
            rows = pl.ds(r0, RC)
            gc = _conv_fwd_rows(pad_ref, w_ref, FFK, FF_HEAD, r0)
            dhv = d_ref[rows, :]
            act, dact = _silu_both(gc)
            dup_ref[rows, :] = (dhv * act).astype(BF16)
            pad2_ref[rows, :] = dhv * up_ref[rows, :] * dact
        for r0 in range(0, S, RC):
            dgate_ref[pl.ds(r0, RC), :] = _conv_bwd_rows(pad2_ref, w_ref, FFK, r0).astype(BF16)
        _conv_dw(pad_ref, pad2_ref, dw_ref, FFK, FF_HEAD)

    wspec = pl.BlockSpec((FFK, FB), lambda j: (0, j))
    return _call(body, "ffn_mid_bwd", [_sds((S, FFN), BF16), _sds((S, FFN), BF16), _sds((FFK, FFN))],
                 grid=(FNB,), in_specs=[_col_spec(FB), _col_spec(FB), _col_spec(FB, FNB), wspec],
                 out_specs=[_col_spec(FB), _col_spec(FB), wspec],
                 scratch=[pltpu.VMEM((S + FF_HEAD, FB), F32), pltpu.VMEM((S + FF_HEAD, FB), F32)],
                 sem=("parallel",))(dh, upall, upall, conv_w)


GT = 256
SM_BLK = O_SM // LANE


def _chunk_tri(lower):
    r = lax.broadcasted_iota(jnp.int32, (GT, GT), 0)
    c = lax.broadcasted_iota(jnp.int32, (GT, GT), 1)
    same = (r // CH) == (c // CH)
    tri = (c <= r) if lower else (c >= r)
    return jnp.where(same & tri, 1.0, 0.0).astype(F32)


def _gates_fwd(proj, alog_v, dtb_v):
    def body(sm_ref, al_ref, dt_ref, o_ref):
        lane = lax.broadcasted_iota(jnp.int32, (GT, LANE), 1)
        tri = _chunk_tri(True)
        na = -jnp.exp(al_ref[...])
        for r0 in range(0, S, GT):
            sm = sm_ref[pl.ds(r0, GT), :]
            raw = jnp.where((lane >= H) & (lane < 2 * H), na * _softplus(sm + dt_ref[...]), 0.0)
            gc = _dot(tri, raw, NN, precision=HI)
            o_ref[pl.ds(r0, GT), :] = jnp.where(lane < H, _sigmoid(sm), gc)

    return _call(body, "gates_fwd", _sds((S, LANE)), grid=(1,),
                 in_specs=[pl.BlockSpec((S, LANE), lambda i: (0, SM_BLK)), _vec_spec(LANE), _vec_spec(LANE)],
                 out_specs=pl.BlockSpec((S, LANE), lambda i: (0, 0)), sem=("arbitrary",))(proj, alog_v, dtb_v)


def _gates_bwd(dgb, proj, alog_v, dtb_v, dproj):
    def body(d_ref, sm_ref, al_ref, dt_ref, dproj_in, o_ref, acc_ref):
        lane = lax.broadcasted_iota(jnp.int32, (GT, LANE), 1)
        is_g = (lane >= H) & (lane < 2 * H)
        tri = _chunk_tri(False)
        na = -jnp.exp(al_ref[...])
        d_al = jnp.zeros((1, LANE), F32)
        d_dt = jnp.zeros((1, LANE), F32)
        for r0 in range(0, S, GT):
            sm = sm_ref[pl.ds(r0, GT), :]
            dv = d_ref[pl.ds(r0, GT), :]
            z = sm + dt_ref[...]
            draw = _dot(tri, jnp.where(is_g, dv, 0.0), NN, precision=HI)
            dlogit = jnp.where(is_g, draw * na * _sigmoid(z), 0.0)
            d_al = d_al + jnp.sum(jnp.where(is_g, draw * na * _softplus(z), 0.0), axis=0, keepdims=True)
            d_dt = d_dt + jnp.sum(dlogit, axis=0, keepdims=True)
            bt = _sigmoid(sm)
            o_ref[pl.ds(r0, GT), :] = jnp.where(lane < H, dv * bt * (1.0 - bt), dlogit).astype(BF16)
        acc_ref[...] = jnp.zeros_like(acc_ref)
        acc_ref[0:1, :] = d_al
        acc_ref[1:2, :] = d_dt

    return _call(body, "gates_bwd", [_sds((S, NINP), BF16), _sds((8, LANE))], grid=(1,),
                 in_specs=[pl.BlockSpec((S, LANE), lambda i: (0, 0)), pl.BlockSpec((S, LANE), lambda i: (0, SM_BLK)),
                           _vec_spec(LANE), _vec_spec(LANE), ANY],
                 out_specs=[pl.BlockSpec((S, LANE), lambda i: (0, SM_BLK)), _acc_spec(LANE)],
                 sem=("arbitrary",), aliases={4: 0})(dgb, proj, alog_v, dtb_v, dproj)


HB = 4


def _each(fn, *lists):
    return [fn(*args) for args in zip(*lists)]


def _neumann_inv(a, eye):
    p = _each(lambda m: -m, a)
    t = _each(lambda m: eye + m, p)
    for _ in range(5):
        p = _each(lambda m: _dot(m, m, NN, precision=lax.Precision.HIGH), p)
        t = _each(lambda tt, pp: tt + _dot(tt, pp, NN, precision=lax.Precision.HIGH), t, p)
    return t


def _head_specs():
    q = pl.BlockSpec((S, HB * DH), lambda h: (0, h), pipeline_mode=ONE_BUF)
    k = pl.BlockSpec((S, HB * DH), lambda h: (0, H // HB + h), pipeline_mode=ONE_BUF)
    v = pl.BlockSpec((S, HB * DH), lambda h: (0, 2 * H // HB + h), pipeline_mode=ONE_BUF)
    gb = pl.BlockSpec((HB, S, DH), lambda h: (h, 0, 0), pipeline_mode=ONE_BUF)
    gr = pl.BlockSpec((HB, NCH, CH), lambda h: (h, 0, 0))
    return q, k, v, gb, gr


ONE_BUF = pl.Buffered(1)
ST_SPEC = pl.BlockSpec((HB, NCH, DH, DH), lambda h: (h, 0, 0, 0), pipeline_mode=ONE_BUF)
TM_SPEC = pl.BlockSpec((HB, NCH, CH, CH), lambda h: (h, 0, 0, 0), pipeline_mode=ONE_BUF)
HCOL = pl.BlockSpec((S, HB * DH), lambda h: (0, h), pipeline_mode=ONE_BUF)


def _delta_fwd(qkvn, gb, gr, bb):
    def body(q_ref, k_ref, v_ref, gb_ref, gr_ref, bb_ref, o_ref, st_ref, tm_ref):
        ri = lax.broadcasted_iota(jnp.int32, (CH, CH), 0)
        ci = lax.broadcasted_iota(jnp.int32, (CH, CH), 1)
        strict = ri > ci
        causal = ri >= ci
        eye = jnp.where(ri == ci, 1.0, 0.0).astype(F32)

        hs = list(range(HB))
        cols = [slice(hh * DH, (hh + 1) * DH) for hh in hs]
        bf = lambda m: m.astype(BF16)

        def local(n):
            rows = pl.ds(pl.multiple_of(n * CH, CH), CH)
            c = dict(rows=rows, n=n)
            c["q"] = [q_ref[rows, cc] for cc in cols]
            c["k"] = [k_ref[rows, cc] for cc in cols]
            c["v"] = [v_ref[rows, cc] for cc in cols]
            c["g"] = [gb_ref[hh, rows, :] for hh in hs]
            c["beta"] = [bb_ref[hh, rows, :] for hh in hs]
            diff = [c["g"][hh][:, :CH] - gr_ref[hh, pl.ds(n, 1), :] for hh in hs]
            c["el"] = _each(lambda d: jnp.exp(jnp.where(causal, d, 0.0)), diff)
            c["eg"] = _each(jnp.exp, c["g"])
            c["gl"] = _each(lambda m: m[CH - 1:CH, :], c["g"])
            c["kb"] = _each(lambda x, y: x * y, c["k"], c["beta"])
            c["kbf"] = _each(bf, c["k"])
            c["a"] = _each(lambda x, y, e: jnp.where(strict, _dot(bf(x), y, NT) * e, 0.0), c["kb"], c["kbf"], c["el"])
            return c

        def advance(c, t, sts):
            n, rows = c["n"], c["rows"]
            for hh in hs:
                tm_ref[hh, n] = t[hh]
                st_ref[hh, n] = sts[hh]
            sb = _each(bf, sts)
            r = _each(lambda vv, bb_, kk, ee, ss: vv * bb_ - _dot(bf(kk * ee), ss, NN), c["v"], c["beta"], c["kb"], c["eg"], sb)
            ub = _each(lambda tt, rr: bf(_dot(tt, rr, NN, precision=HI)), t, r)
            p = _each(lambda qq, kk, e: jnp.where(causal, _dot(bf(qq), kk, NT) * e, 0.0), c["q"], c["kbf"], c["el"])
            o = _each(lambda qq, ee, ss, pp, uu: _dot(bf(qq * ee), ss, NN) + _dot(bf(pp), uu, NN), c["q"], c["eg"], sb, p, ub)
            for hh in hs:
                o_ref[rows, cols[hh]] = o[hh]
            kd = _each(lambda kk, l, gg: kk * jnp.exp(l - gg), c["k"], c["gl"], c["g"])
            return _each(lambda st, l, kk, uu: st * jnp.exp(l) + _dot(bf(kk), uu, TN), sts, c["gl"], kd, ub)

        def step(i, sts):
            c0, c1 = local(2 * i), local(2 * i + 1)
            t = _neumann_inv(c0["a"] + c1["a"], eye)
            sts = advance(c0, t[:HB], list(sts))
            return tuple(advance(c1, t[HB:], sts))

        lax.fori_loop(0, NCH // 2, step, tuple(jnp.zeros((DH, DH), F32) for _ in hs))

    q, k, v, gbs, grs = _head_specs()
    return _call(body, "delta_fwd", [_sds((S, DNW)), _sds((H, NCH, DH, DH)), _sds((H, NCH, CH, CH))], grid=(H // HB,),
                 in_specs=[q, k, v, gbs, grs, gbs], out_specs=[HCOL, ST_SPEC, TM_SPEC],
                 sem=("parallel",))(qkvn, qkvn, qkvn, gb, gr, bb)


def _delta_bwd(qkvn, gb, gr, bb, st_all, tm_all, do_all):
    def body(q_ref, k_ref, v_ref, gb_ref, gr_ref, bb_ref, st_ref, tm_ref, do_ref,
             dq_ref, dk_ref, dv_ref, dg_ref, db_ref):
        ri = lax.broadcasted_iota(jnp.int32, (CH, CH), 0)
        ci = lax.broadcasted_iota(jnp.int32, (CH, CH), 1)
        lo_s, lo_c, up_s, up_c = ri > ci, ri >= ci, ri < ci, ri <= ci
        last_row = lax.broadcasted_iota(jnp.int32, (CH, 1), 0) == CH - 1

        def rs(mat):
            return jnp.sum(mat, axis=1, keepdims=True)

        def total(mat):
            return jnp.sum(rs(mat), axis=0, keepdims=True)

        hs = list(range(HB))
        cols = [slice(hh * DH, (hh + 1) * DH) for hh in hs]
        bf = lambda m: m.astype(BF16)
        mul = lambda x, y: x * y
        spread = jnp.full((8, DH), 1.0 / DH, F32)

        def as_row(col):
            return _dot(spread, jnp.broadcast_to(col, (CH, DH)), NT, precision=HI)[0:1, :]

        def step(i, dss):
            ns = [NCH - 1 - 2 * i, NCH - 2 - 2 * i]
            rws = [pl.ds(pl.multiple_of(n * CH, CH), CH) for n in ns]
            idx = [(cc, hh) for cc in range(2) for hh in hs]
            q = [q_ref[rws[cc], cols[hh]] for cc, hh in idx]
            k = [k_ref[rws[cc], cols[hh]] for cc, hh in idx]
            v = [v_ref[rws[cc], cols[hh]] for cc, hh in idx]
            do = [do_ref[rws[cc], cols[hh]] for cc, hh in idx]
            g = [gb_ref[hh, rws[cc], :] for cc, hh in idx]
            beta = [bb_ref[hh, rws[cc], :] for cc, hh in idx]
            t = [tm_ref[hh, ns[cc]] for cc, hh in idx]
            st = [st_ref[hh, ns[cc]] for cc, hh in idx]
            diff = [gg[:, :CH] - gr_ref[hh, pl.ds(ns[cc], 1), :] for gg, (cc, hh) in zip(g, idx)]
            el = _each(lambda d: jnp.exp(jnp.where(lo_c, d, 0.0)), diff)
            eu = _each(lambda d: jnp.exp(jnp.where(up_c, -d, 0.0)), diff)
            eg = _each(jnp.exp, g)
            gl = _each(lambda m: m[CH - 1:CH, :], g)
            egl = _each(jnp.exp, gl)
            ekd = _each(lambda l, m: jnp.exp(l - m), gl, g)
            kb = _each(mul, k, beta)
            kbg = _each(mul, kb, eg)
            qg = _each(mul, q, eg)
            kd = _each(mul, k, ekd)
            qb, kbf, kbb = _each(bf, q), _each(bf, k), _each(bf, kb)
            kbgb, qgb, kdb = _each(bf, kbg), _each(bf, qg), _each(bf, kd)
            sb, dob = _each(bf, st), _each(bf, do)
            r = _each(lambda vv, b, x, s: vv * b - _dot(x, s, NN), v, beta, kbgb, sb)
            u = _each(lambda tt, rr: _dot(tt, rr, NN, precision=HI), t, r)
            ub = _each(bf, u)
            kk = _each(lambda x, y: _dot(x, y, NT), kbb, kbf)
            qk = _each(lambda x, y: _dot(x, y, NT), qb, kbf)
            kkt = _each(lambda x, y: _dot(x, y, NT), kbf, kbb)
            qkt = _each(lambda x, y: _dot(x, y, NT), kbf, qb)
            pt = _each(lambda m, e: jnp.where(up_c, m * e, 0.0), qkt, eu)
            ds, du, dr, drb, ds_new = [], [], [], [], list(dss)
            for cc in range(2):
                sl = slice(cc * HB, (cc + 1) * HB)
                ds_c = ds_new
                dsb_c = _each(bf, ds_c)
                du_c = _each(lambda p, d, x, s: _dot(bf(p), d, NN) + _dot(x, s, NN), pt[sl], dob[sl], kdb[sl], dsb_c)
                dr_c = _each(lambda tt, d: _dot(tt, d, TN, precision=HI), t[sl], du_c)
                drb_c = _each(bf, dr_c)
                ds_new = _each(lambda x, d, e, s, y, z: _dot(x, d, TN) + e * s - _dot(y, z, TN),
                               qgb[sl], dob[sl], egl[sl], ds_c, kbgb[sl], drb_c)
                ds, du, dr, drb = ds + ds_c, du + du_c, dr + dr_c, drb + drb_c
            dsb = _each(bf, ds)
            dpg = _each(lambda d, uu, e: jnp.where(lo_c, _dot(d, uu, NT), 0.0) * e, dob, ub, el)
            dpgt = _each(lambda uu, d, e: jnp.where(up_c, _dot(uu, d, NT), 0.0) * e, ub, dob, eu)
            dag = _each(lambda d, uu, e: -jnp.where(lo_s, _dot(d, uu, NT), 0.0) * e, drb, ub, el)
            dagt = _each(lambda uu, d, e: -jnp.where(up_s, _dot(uu, d, NT), 0.0) * e, ub, drb, eu)
            dqg = _each(lambda d, s: _dot(d, s, NT), dob, sb)
            dkbg = _each(lambda d, s: -_dot(d, s, NT), drb, sb)
            dkd = _each(lambda uu, s: _dot(uu, s, NT), ub, dsb)
            dkb =_each(lambda a, x, y, e: _dot(bf(a), x, NN) + y * e, dag, kbf, dkbg, eg)
            dk = _each(lambda a, x, p, y, z, e, w, b: _dot(bf(a), x, NN) + _dot(bf(p), y, NN) + z * e + w * b,
                       dagt, kbb, dpgt, qb, dkd, ekd, dkb, beta)
            dq = _each(lambda p, x, y, e: _dot(bf(p), x, NN) + y * e, dpg, kbf, dqg, eg)
            dkd_kd = _each(lambda x, y: rs(x * y), dkd, kd)
            dg = _each(lambda a, x, p, y, at, xt, pt_, yt, z, w, c, d, e:
                       rs(a * x + p * y) - rs(at * xt + pt_ * yt) + rs(z * w) + rs(c * d) - e,
                       dag, kk, dpg, qk, dagt, kkt, dpgt, qkt, dqg, qg, dkbg, kbg, dkd_kd)
            dgl = _each(lambda x, e, s, y: jnp.sum(x, axis=0, keepdims=True) + e[:, 0:1] * total(s * y), dkd_kd, egl, ds, st)
            dg = _each(lambda x, y: x + jnp.where(last_row, y, 0.0), dg, dgl)
            dbeta = _each(lambda x, y, z, w: rs(x * y) + rs(z * w), dkb, k, dr, v)
            for j, (cc, hh) in enumerate(idx):
                dq_ref[rws[cc], cols[hh]] = dq[j]
                dk_ref[rws[cc], cols[hh]] = dk[j]
                dv_ref[rws[cc], cols[hh]] = dr[j] * beta[j]
                dg_ref[hh, pl.ds(ns[cc], 1), :] = as_row(dg[j])
                db_ref[hh, pl.ds(ns[cc], 1), :] = as_row(dbeta[j])
            return tuple(ds_new)

        lax.fori_loop(0, NCH // 2, step, tuple(jnp.zeros((DH, DH), F32) for _ in hs))

    q, k, v, gbs, grs = _head_specs()
    return _call(body, "delta_bwd",
                 [_sds((S, DNW)), _sds((S, DNW)), _sds((S, DNW)), _sds((H, NCH, CH)), _sds((H, NCH, CH))], grid=(H // HB,),
                 in_specs=[q, k, v, gbs, grs, gbs, ST_SPEC, TM_SPEC, HCOL], out_specs=[HCOL, HCOL, HCOL, grs, grs],
                 sem=("parallel",))(qkvn, qkvn, qkvn, gb, gr, bb, st_all, tm_all, do_all)


Z_BLK = O_Z // DNW


def _dn_post_fwd(o, proj, gn):
    def body(o_ref, z_ref, gn_ref, og_ref):
        for h in range(H):
            cols = slice(h * DH, (h + 1) * DH)
            ov = o_ref[:, cols]
            on = ov * lax.rsqrt(jnp.mean(ov * ov, axis=-1, keepdims=True) + EPS) * gn_ref[...]
            og_ref[:, cols] = (on * _silu(z_ref[:, cols])).astype(BF16)

    return _call(body, "dn_post_fwd", _sds((S, DNW), BF16), grid=(S // TS,),
                 in_specs=[_row_spec(DNW), pl.BlockSpec((TS, DNW), lambda i: (i, Z_BLK)), _vec_spec(DH)],
                 out_specs=_row_spec(DNW), sem=("parallel",))(o, proj, gn)


def _dn_post_bwd(dog, o, proj, gn, dproj):
    def body(d_ref, o_ref, z_ref, gn_ref, dproj_in, do_ref, dz_ref, acc_ref):
        dgn = jnp.zeros((1, DH), F32)
        for h in range(H):
            cols = slice(h * DH, (h + 1) * DH)
            ov, zv, dv = o_ref[:, cols], z_ref[:, cols], d_ref[:, cols]
            rinv = lax.rsqrt(jnp.mean(ov * ov, axis=-1, keepdims=True) + EPS)
            xn = ov * rinv
            act, dact = _silu_both(zv)
            don = dv * act
            dz_ref[:, cols] = (dv * xn * gn_ref[...] * dact).astype(BF16)
            dgn = dgn + jnp.sum(don * xn, axis=0, keepdims=True)
            dxn = don * gn_ref[...]
            do_ref[:, cols] = rinv * (dxn - xn * jnp.mean(dxn * xn, axis=-1, keepdims=True))
        _acc_rows(acc_ref, [dgn])

    zspec = pl.BlockSpec((TS, DNW), lambda i: (i, Z_BLK))
    return _call(body, "dn_post_bwd", [_sds((S, DNW)), _sds((S, NINP), BF16), _sds((8, DH))], grid=(S // TS,),
                 in_specs=[_row_spec(DNW), _row_spec(DNW), zspec, _vec_spec(DH), ANY],
                 out_specs=[_row_spec(DNW), zspec, _acc_spec(DH)], sem=("arbitrary",),
                 aliases={4: 1})(dog, o, proj, gn, dproj)


GA_BLK = O_GA // D
GB_BLK = O_GB // D


def _merge_fwd(ba, bb, proj):
    def body(a_ref, b_ref, ga_ref, gb_ref, o_ref):
        o_ref[...] = (_sigmoid(ga_ref[...]) * a_ref[...] + _sigmoid(gb_ref[...]) * b_ref[...]).astype(BF16)

    return _call(body, "merge_fwd", _sds((S, D), BF16), grid=(S // TS,),
                 in_specs=[_row_spec(), _row_spec(), pl.BlockSpec((TS, D), lambda i: (i, GA_BLK)),
                           pl.BlockSpec((TS, D), lambda i: (i, GB_BLK))],
                 out_specs=_row_spec(), sem=("parallel",))(ba, bb, proj, proj)


def _merge_bwd(dm, ba, bb, proj, dproj):
    def body(d_ref, a_ref, b_ref, ga_ref, gb_ref, dproj_in, dg_ref, da_ref, db_ref):
        d = d_ref[...]
        sa, sb = _sigmoid(ga_ref[...]), _sigmoid(gb_ref[...])
        dg_ref[:, 0:D] = (d * a_ref[...] * sa * (1.0 - sa)).astype(BF16)
        dg_ref[:, D:2 * D] = (d * b_ref[...] * sb * (1.0 - sb)).astype(BF16)
        da_ref[...] = (d * sa).astype(BF16)
        db_ref[...] = (d * sb).astype(BF16)

    return _call(body, "merge_bwd", [_sds((S, NINP), BF16), _sds((S, D), BF16), _sds((S, D), BF16)], grid=(S // TS,),
                 in_specs=[_row_spec(), _row_spec(), _row_spec(), pl.BlockSpec((TS, D), lambda i: (i, GA_BLK)),
                           pl.BlockSpec((TS, D), lambda i: (i, GB_BLK)), ANY],
                 out_specs=[pl.BlockSpec((TS, 2 * D), lambda i: (i, O_GA // (2 * D))), _row_spec(), _row_spec()],
                 sem=("parallel",), aliases={5: 0})(dm, ba, bb, proj, proj, dproj)


NSH = NIN // NDEV


def _pad_win(wt):
    rows = [wt[0:4096], wt[6160:6160 + 2 * D]]
    for j in range(CFW // LANE):
        rows += [wt[4112 + LANE * j:4112 + LANE * (j + 1)], wt[4112 + CFW + LANE * j:4112 + CFW + LANE * (j + 1)]]
    rows += [wt[4096:4112], jnp.zeros((NINP - NIN, wt.shape[1]), wt.dtype)]
    return jnp.concatenate(rows, axis=0)


def _unpad_win(gpt):
    rows = [gpt[0:4096], gpt[O_SM:O_SM + 16]]
    for half in range(2):
        rows += [gpt[O_GLU + (2 * j + half) * LANE:O_GLU + (2 * j + half + 1) * LANE] for j in range(CFW // LANE)]
    rows.append(gpt[O_GA:O_GA + 2 * D])
    return jnp.concatenate(rows, axis=0)


def _lane_vec(v8, offset):
    return jnp.pad(v8, ((0, 0), (offset, LANE - 8 - offset)))


def _tie(vec, token):
    return vec + token


def _local_step(x, tgt, mod, norm1_g, norm2_g, final_g, w_in_p, dn_conv_w, a_log, dt_bias, dn_norm_g,
                cf_conv_w, cf_ln_g, cf_ln_b, ffn_conv_w, comm):
    sh1, sc1, gt1, sh2, sc2, gt2 = (mod[:, i * D:(i + 1) * D] for i in range(6))
    alog_v, dtb_v = _lane_vec(a_log, H), _lane_vec(dt_bias, H)

    hn1 = _norm_mod(x, norm1_g, sc1, _tie(sh1, comm.token0), "norm_mod1")
    proj = _mm(hn1, w_in_p, "nt", F32, "mm_in", tn=1152)
    qkvn = _dn_pre_fwd(proj, dn_conv_w)
    gates = _gates_fwd(proj, alog_v, dtb_v)
    beta_t = gates[:, 0:H].T
    g_t = gates[:, H:2 * H].T
    gb = jnp.broadcast_to(g_t[:, :, None], (H, S, DH))
    bb = jnp.broadcast_to(beta_t[:, :, None], (H, S, DH))
    gr = g_t.reshape(H, NCH, CH)
    o, st_all, tm_all = _delta_fwd(qkvn, gb, gr, bb)
    og = _dn_post_fwd(o, proj, dn_norm_g)
    u1 = _cf_conv_fwd(proj, cf_conv_w)
    u3 = _cf_ln_fwd(u1, cf_ln_g, cf_ln_b)
    after = og[0:8, 0:LANE].astype(F32) + u3[0:8, 0:LANE].astype(F32)
    dn_w_o, cf_w_o, w_out = comm.late_weights("mix", after)
    br_a = _mm(og, dn_w_o, "nn", F32, "mm_dn_o")
    br_b = _mm(u3, cf_w_o, "nn", F32, "mm_cf_o")
    merged = _merge_fwd(br_a, br_b, proj)
    mix = _mm(merged, w_out, "nn", F32, "mm_out")
    x2, hn2 = _resid_norm_mod(x, mix, gt1, norm2_g, sc2, sh2, "resid_norm_mod2")
    ffn_w_up, ffn_w_down = comm.late_weights("ffn", hn2[0:8, 0:LANE].astype(F32))
    upall = _mm(hn2, ffn_w_up, "nn", F32, "mm_up")
    hmid = _ffn_mid_fwd(upall, ffn_conv_w)
    f = _mm(hmid, ffn_w_down, "nn", F32, "mm_down", tm=2048)

    dx3, df, acc_f = _loss_head(x2, f, tgt, gt2, final_g)
    d_final_g, d_gt2, loss = acc_f[0:1], acc_f[1:2], acc_f[2:3, 0:1]
    dhmid = _mm(df, ffn_w_down, "nt", F32, "mm_down_dx")
    g_w_down = _mm(hmid, df, "tn", BF16, "mm_down_dw", tm=FFN // 4)
    d_gate, d_up, g_ffn_conv = _ffn_mid_bwd(dhmid, upall, ffn_conv_w)
    g_w_up = _mm(hn2, d_gate, "tn", BF16, "mm_up_dw", tn=2 * FFN // NDEV, b2=d_up)
    tok_a = comm.grads_begin("a", dict(ffn_w_down=g_w_down, ffn_w_up=g_w_up))
    dhn2 = _mm(d_gate, ffn_w_up, "nt", F32, "mm_up_dx", a2=d_up, dep=jnp.broadcast_to(tok_a, (8, LANE)))
    tok_a = comm.grads_continue("a", dhn2)
    dx2, dmix, acc2 = _norm_mod_bwd(dhn2, x2, dx3, _tie(norm2_g, tok_a), sc2, "norm_mod2_bwd", mix=mix, gt=gt1)
    d_sh2, d_sc2, d_norm2_g, d_gt1 = acc2[0:1], acc2[1:2], acc2[2:3], acc2[3:4]
    dmerged = _mm(dmix, w_out, "nt", F32, "mm_out_dx")
    g_w_out = _mm(merged, dmix, "tn", BF16, "mm_out_dw")
    d_proj, d_bra, d_brb = _merge_bwd(dmerged, br_a, br_b, proj, lax.empty((S, NINP), BF16))
    du3 = _mm(d_brb, cf_w_o, "nt", F32, "mm_cf_o_dx")
    g_cf_w_o = _mm(u3, d_brb, "tn", BF16, "mm_cf_o_dw")
    du1, acc_ln = _cf_ln_bwd(du3, u1, cf_ln_g, cf_ln_b)
    d_proj, g_cf_conv = _cf_conv_bwd(du1, proj, cf_conv_w, d_proj)
    dog = _mm(d_bra, dn_w_o, "nt", F32, "mm_dn_o_dx")
    g_dn_w_o = _mm(og, d_bra, "tn", BF16, "mm_dn_o_dw")
    tok_b = comm.grads_begin("b", dict(w_out=g_w_out, cf_w_o=g_cf_w_o, dn_w_o=g_dn_w_o, ffn_conv_w=g_ffn_conv,
                                       cf_conv_w=g_cf_conv))
    do, d_proj, acc_gn = _dn_post_bwd(dog, o, proj, _tie(dn_norm_g, tok_b), d_proj)
    tok_b = comm.grads_continue("b", do)
    dq, dk, dv, dgr, dbr = _delta_bwd(qkvn, gb, _tie(gr, tok_b), bb, st_all, tm_all, do)
    d_proj, g_dn_conv = _dn_pre_bwd(dq, dk, dv, proj, dn_conv_w, d_proj)
    dgates = jnp.concatenate([dbr.reshape(H, S).T, dgr.reshape(H, S).T, jnp.zeros((S, LANE - 2 * H), F32)], axis=1)
    d_proj, acc_g = _gates_bwd(dgates, proj, alog_v, dtb_v, d_proj)
    g_w_in_p = _mm(d_proj, hn1, "tn", BF16, "mm_in_dw", tm=1152)
    tok_c = comm.grads_begin("c", dict(w_in=g_w_in_p, dn_conv_w=g_dn_conv))
    tok_c = comm.grads_continue("c", jnp.broadcast_to(tok_c, (8, LANE)))
    dhn1 = _mm(d_proj, w_in_p, "nn", F32, "mm_in_dx", tk=NINP // 3, dep=jnp.broadcast_to(tok_c, (8, LANE)))
    grad_x, acc1 = _norm_mod_bwd(dhn1, x, dx2, norm1_g, sc1, "norm_mod1_bwd")
    d_sh1, d_sc1, d_norm1_g = acc1[0:1], acc1[1:2], acc1[2:3]

    d_mod = jnp.concatenate([d_sh1, d_sc1, d_gt1, d_sh2, d_sc2, d_gt2], axis=1)
    small = dict(mod=d_mod, norm1_g=d_norm1_g, norm2_g=d_norm2_g, final_norm_g=d_final_g,
                 cf_ln_g=acc_ln[0:1], cf_ln_b=acc_ln[1:2], dn_norm_g=acc_gn[0:1],
                 dn_a_log=acc_g[0:1, H:2 * H], dn_dt_bias=acc_g[1:2, H:2 * H])
    return loss, grad_x, small


def _dev_index(px, py, pc):
    return 4 * px + 2 * py + pc


def _all_gather(arrs, name):
    n = len(arrs)

    def body(*refs):
        ins, outs = refs[:n], refs[n:2 * n]
        send_sems, recv_sems, loc_sems = refs[2 * n:]
        x, y, c = _my_pos()
        me, sib = (x, y, c), (x, y, 1 - c)
        chips = [(1 - x, y), (x, 1 - y), (1 - x, 1 - y)]

        def cp(i, k, block, to, src=None):
            dst = outs[i].at[_dev_index(*block)]
            return pltpu.make_async_remote_copy(
                src_ref=dst if src is None else src, dst_ref=dst, send_sem=send_sems.at[i, k],
                recv_sem=recv_sems.at[i, k], device_id=to, device_id_type=MESH)

        mine = [pltpu.make_async_copy(ins[i], outs[i].at[_dev_index(*me)], loc_sems.at[i]) for i in range(n)]
        for m in mine:
            m.start()
        sent = []
        for i in range(n):
            sent.append(cp(i, 0, me, sib, src=ins[i]))
            sent += [cp(i, 1 + j, me, (*chip, c), src=ins[i]) for j, chip in enumerate(chips)]
        for s in sent:
            s.start()
        for i in range(n):
            for j, chip in enumerate(chips):
                cp(i, 1 + j, (*chip, c), me).wait_recv()
                fwd = cp(i, 4 + j, (*chip, c), sib)
                fwd.start()
                sent.append(fwd)
        for i in range(n):
            cp(i, 0, sib, me).wait_recv()
            for j, chip in enumerate(chips):
                cp(i, 4 + j, (*chip, 1 - c), me).wait_recv()
        for s in sent:
            s.wait_send()
        for m in mine:
            m.wait()

    outs = pl.pallas_call(
        body, out_shape=[_sds((NDEV,) + a.shape, a.dtype) for a in arrs], in_specs=[ANY] * n, out_specs=[ANY] * n,
        scratch_shapes=[pltpu.SemaphoreType.DMA((n, 7)), pltpu.SemaphoreType.DMA((n, 7)), pltpu.SemaphoreType.DMA((n,))],
        name=name)(*arrs)
    return list(outs)


def _slab(ref, layout, idx):
    kind, n = layout
    if kind == "rows":
        return ref.at[pl.ds(pl.multiple_of(idx * n, n), n), :]
    if kind == "cols":
        return ref.at[:, pl.ds(pl.multiple_of(idx * n, n), n)]
    return ref.at[idx]


def _slab_shape(arr, layout):
    kind, n = layout
    if kind == "rows":
        return (n, arr.shape[1])
    if kind == "cols":
        return (arr.shape[0], n)
    return tuple(arr.shape[1:])


HBM = pl.BlockSpec(memory_space=pltpu.HBM)
SEMS = pl.BlockSpec(memory_space=pltpu.SEMAPHORE)
EFFECT = pltpu.SideEffectType.DATAFLOW_SIDE_EFFECTING
TOKEN = jax.ShapeDtypeStruct((8, LANE), F32)


def _hbm(a):
    return pltpu.with_memory_space_constraint(a, pltpu.HBM)


def _gather_ici_copy(shard_ref, buf_ref, layout, send_sems, recv_sems, i, j, me, chip, c):
    return pltpu.make_async_remote_copy(
        src_ref=shard_ref, dst_ref=_slab(buf_ref, layout, me), send_sem=send_sems.at[3 * i + j],
        recv_sem=recv_sems.at[3 * i + j], device_id=(*chip, c), device_id_type=MESH)


def _gather_ici_start(shards, bufs, layouts, after, name):
    n = len(shards)

    def body(*refs):
        sh, bf = refs[:n], refs[n:2 * n]
        send_sems, recv_sems = refs[2 * n + 1], refs[2 * n + 2]
        token = refs[-1]
        x, y, c = _my_pos()
        me = _dev_index(x, y, c)
        for i in range(n):
            for j, chip in enumerate([(1 - x, y), (x, 1 - y), (1 - x, 1 - y)]):
                _gather_ici_copy(sh[i], bf[i], layouts[i], send_sems, recv_sems, i, j, me, chip, c).start()
        token[...] = jnp.zeros_like(token)

    outs = pl.pallas_call(
        body, name=name,
        out_shape=(pltpu.SemaphoreType.DMA((3 * n,)), pltpu.SemaphoreType.DMA((3 * n,)),
                   *[pltpu.HBM(a.shape, a.dtype) for a in shards], *[pltpu.HBM(a.shape, a.dtype) for a in bufs], TOKEN),
        in_specs=[HBM] * (2 * n) + [ANY],
        out_specs=(SEMS, SEMS, *[HBM] * (2 * n), pl.BlockSpec(memory_space=pltpu.VMEM)),
        input_output_aliases={i: 2 + i for i in range(2 * n)},
        compiler_params=pltpu.CompilerParams(has_side_effects=EFFECT),
    )(*[_hbm(a) for a in shards], *[_hbm(a) for a in bufs], after)
    return outs[0], outs[1], list(outs[2:2 + n]), list(outs[2 + n:2 + 2 * n]), outs[-1]


def _gather_ici_wait(send_sems, recv_sems, shards, bufs, layouts, after, name):
    n = len(shards)

    def body(*refs):
        sh, bf = refs[:n], refs[n:2 * n]
        ssem, rsem = refs[2 * n], refs[2 * n + 1]
        x, y, c = _my_pos()
        me = _dev_index(x, y, c)
        for i in range(n):
            for j, chip in enumerate([(1 - x, y), (x, 1 - y), (1 - x, 1 - y)]):
                cp = _gather_ici_copy(sh[i], bf[i], layouts[i], ssem, rsem, i, j, me, chip, c)
                cp.wait_send()
                cp.wait_recv()

    outs = pl.pallas_call(
        body, name=name,
        out_shape=(*[pltpu.HBM(a.shape, a.dtype) for a in shards], *[pltpu.HBM(a.shape, a.dtype) for a in bufs]),
        in_specs=[HBM] * (2 * n) + [SEMS, SEMS, ANY], out_specs=tuple([HBM] * (2 * n)),
        input_output_aliases={i: i for i in range(2 * n)},
        compiler_params=pltpu.CompilerParams(has_side_effects=EFFECT),
    )(*shards, *bufs, send_sems, recv_sems, after)
    return list(outs[:n]), list(outs[n:])


def _place_own(pos, shard, buf, layout, name):
    kind, n = layout
    r, cols = shard.shape
    tr = _row_tile(r, shard.dtype.itemsize)
    nr = r // tr
    if kind == "rows":
        ospec = pl.BlockSpec((tr, cols), lambda i, p: (p[2] * nr + i, 0))
    else:
        assert kind == "lead"
        ospec = pl.BlockSpec((None, tr, cols), lambda i, p: (p[2], i, 0))

    def body(pos_ref, s_ref, buf_in, o_ref):
        o_ref[...] = s_ref[...]

    return pl.pallas_call(
        body, out_shape=_sds(buf.shape, buf.dtype), name=name, input_output_aliases={2: 0},
        grid_spec=pltpu.PrefetchScalarGridSpec(
            num_scalar_prefetch=1, grid=(nr,), in_specs=[pl.BlockSpec((tr, cols), lambda i, p: (i, 0)), ANY],
            out_specs=ospec),
        compiler_params=pltpu.CompilerParams(dimension_semantics=("parallel",), vmem_limit_bytes=VMEM_LIMIT),
    )(pos, shard, buf)


def _gather_pair(shards, bufs, layouts, name):
    n = len(shards)

    def body(*refs):
        sh, bo = refs[:n], refs[2 * n:3 * n]
        send_sems, recv_sems = refs[3 * n:]
        x, y, c = _my_pos()
        sib = (x, y, 1 - c)
        copies = []
        for i in range(n):
            for k, (px, py) in enumerate([(x, y), (1 - x, y), (x, 1 - y), (1 - x, 1 - y)]):
                slab = _slab(bo[i], layouts[i], _dev_index(px, py, c))
                copies.append(pltpu.make_async_remote_copy(
                    src_ref=sh[i] if k == 0 else slab, dst_ref=slab, send_sem=send_sems.at[i, k],
                    recv_sem=recv_sems.at[i, k], device_id=sib, device_id_type=MESH))
        for cpy in copies:
            cpy.start()
        for cpy in copies:
            cpy.wait()

    outs = pl.pallas_call(
        body, out_shape=[_sds(a.shape, a.dtype) for a in bufs], in_specs=[ANY] * (2 * n), out_specs=[ANY] * n,
        input_output_aliases={n + i: i for i in range(n)},
        scratch_shapes=[pltpu.SemaphoreType.DMA((n, 4)), pltpu.SemaphoreType.DMA((n, 4))], name=name)(*shards, *bufs)
    return list(outs)


def _pair_copy(part_ref, land_ref, layout, send_sems, recv_sems, i, q, x, y, c):
    return pltpu.make_async_remote_copy(
        src_ref=_slab(part_ref, layout, 2 * q + (1 - c)), dst_ref=land_ref.at[q], send_sem=send_sems.at[4 * i + q],
        recv_sem=recv_sems.at[4 * i + q], device_id=(x, y, 1 - c), device_id_type=MESH)


def _pair_exchange_start(parts, layouts, name):
    n = len(parts)
    lands = [lax.empty((4,) + _slab_shape(p, lay), p.dtype) for p, lay in zip(parts, layouts)]

    def body(*refs):
        pt, ld = refs[:n], refs[n:2 * n]
        send_sems, recv_sems = refs[2 * n], refs[2 * n + 1]
        token = refs[-1]
        x, y, c = _my_pos()
        for i in range(n):
            for q in range(4):
                _pair_copy(pt[i], ld[i], layouts[i], send_sems, recv_sems, i, q, x, y, c).start()
        token[...] = jnp.zeros_like(token)

    outs = pl.pallas_call(
        body, name=name,
        out_shape=(pltpu.SemaphoreType.DMA((4 * n,)), pltpu.SemaphoreType.DMA((4 * n,)),
                   *[pltpu.HBM(a.shape, a.dtype) for a in parts], *[pltpu.HBM(a.shape, a.dtype) for a in lands], TOKEN),
        in_specs=[HBM] * (2 * n), out_specs=(SEMS, SEMS, *[HBM] * (2 * n), pl.BlockSpec(memory_space=pltpu.VMEM)),
        input_output_aliases={i: 2 + i for i in range(2 * n)},
        compiler_params=pltpu.CompilerParams(has_side_effects=EFFECT),
    )(*[_hbm(a) for a in parts], *[_hbm(a) for a in lands])
    return outs[0], outs[1], list(outs[2:2 + n]), list(outs[2 + n:2 + 2 * n]), outs[-1]


def _pair_exchange_wait(send_sems, recv_sems, parts, lands, layouts, after, name):
    n = len(parts)

    def body(*refs):
        pt, ld = refs[:n], refs[n:2 * n]
        ssem, rsem = refs[2 * n], refs[2 * n + 1]
        x, y, c = _my_pos()
        for i in range(n):
            for q in range(4):
                cp = _pair_copy(pt[i], ld[i], layouts[i], ssem, rsem, i, q, x, y, c)
                cp.wait_send()
                cp.wait_recv()

    outs = pl.pallas_call(
        body, name=name,
        out_shape=(*[pltpu.HBM(a.shape, a.dtype) for a in parts], *[pltpu.HBM(a.shape, a.dtype) for a in lands]),
        in_specs=[HBM] * (2 * n) + [SEMS, SEMS, ANY], out_specs=tuple([HBM] * (2 * n)),
        input_output_aliases={i: i for i in range(2 * n)},
        compiler_params=pltpu.CompilerParams(has_side_effects=EFFECT),
    )(*parts, *lands, send_sems, recv_sems, after)
    return list(outs[:n]), list(outs[n:])


def _chip_copy(sum_ref, land_ref, send_sems, recv_sems, i, j, chip, c):
    return pltpu.make_async_remote_copy(
        src_ref=sum_ref.at[2 * chip[0] + chip[1]], dst_ref=land_ref.at[j], send_sem=send_sems.at[3 * i + j],
        recv_sem=recv_sems.at[3 * i + j], device_id=(*chip, c), device_id_type=MESH)


def _chip_exchange_start(sums, name):
    n = len(sums)
    lands = [lax.empty((3,) + s.shape[1:], s.dtype) for s in sums]

    def body(*refs):
        sm, ld = refs[:n], refs[n:2 * n]
        send_sems, recv_sems = refs[2 * n], refs[2 * n + 1]
        token = refs[-1]
        x, y, c = _my_pos()
        for i in range(n):
            for j, chip in enumerate([(1 - x, y), (x, 1 - y), (1 - x, 1 - y)]):
                _chip_copy(sm[i], ld[i], send_sems, recv_sems, i, j, chip, c).start()
        token[...] = jnp.zeros_like(token)

    outs = pl.pallas_call(
        body, name=name,
        out_shape=(pltpu.SemaphoreType.DMA((3 * n,)), pltpu.SemaphoreType.DMA((3 * n,)),
                   *[pltpu.HBM(a.shape, a.dtype) for a in sums], *[pltpu.HBM(a.shape, a.dtype) for a in lands], TOKEN),
        in_specs=[HBM] * (2 * n), out_specs=(SEMS, SEMS, *[HBM] * (2 * n), pl.BlockSpec(memory_space=pltpu.VMEM)),
        input_output_aliases={i: 2 + i for i in range(2 * n)},
        compiler_params=pltpu.CompilerParams(has_side_effects=EFFECT),
    )(*[_hbm(a) for a in sums], *[_hbm(a) for a in lands])
    return outs[0], outs[1], list(outs[2:2 + n]), list(outs[2 + n:2 + 2 * n]), outs[-1]


def _chip_exchange_wait(send_sems, recv_sems, sums, lands, after, name):
    n = len(sums)

    def body(*refs):
        sm, ld = refs[:n], refs[n:2 * n]
        ssem, rsem = refs[2 * n], refs[2 * n + 1]
        x, y, c = _my_pos()
        for i in range(n):
            for j, chip in enumerate([(1 - x, y), (x, 1 - y), (1 - x, 1 - y)]):
                cp = _chip_copy(sm[i], ld[i], ssem, rsem, i, j, chip, c)
                cp.wait_send()
                cp.wait_recv()

    outs = pl.pallas_call(
        body, name=name,
        out_shape=(*[pltpu.HBM(a.shape, a.dtype) for a in sums], *[pltpu.HBM(a.shape, a.dtype) for a in lands]),
        in_specs=[HBM] * (2 * n) + [SEMS, SEMS, ANY], out_specs=tuple([HBM] * (2 * n)),
        input_output_aliases={i: i for i in range(2 * n)},
        compiler_params=pltpu.CompilerParams(has_side_effects=EFFECT),
    )(*sums, *lands, send_sems, recv_sems, after)
    return list(outs[:n]), list(outs[n:])


def _row_tile(r, itemsize):
    align = 32 // itemsize
    best = r
    for t in range(align, min(r, 256) + 1, align):
        if r % t == 0:
            best = t
    return best


def _prefetch_call(body, name, out_shape, grid, in_specs, out_specs, sem):
    return pl.pallas_call(
        body, out_shape=out_shape, name=name,
        grid_spec=pltpu.PrefetchScalarGridSpec(num_scalar_prefetch=1, grid=grid, in_specs=in_specs, out_specs=out_specs),
        compiler_params=pltpu.CompilerParams(dimension_semantics=sem, vmem_limit_bytes=VMEM_LIMIT))


def _pair_sum(pos, part, got, layout, name):
    kind, _ = layout
    _, r, cols = got.shape
    tr, tc = _tiles(r, cols, part.dtype.itemsize)
    nr, nc = r // tr, cols // tc
    if kind == "rows":
        pspec = pl.BlockSpec((tr, tc), lambda q, i, j, p: ((2 * q + p[0]) * nr + i, j))
    elif kind == "cols":
        pspec = pl.BlockSpec((tr, tc), lambda q, i, j, p: (i, (2 * q + p[0]) * nc + j))
    else:
        pspec = pl.BlockSpec((None, tr, tc), lambda q, i, j, p: (2 * q + p[0], i, j))

    def body(pos_ref, p_ref, g_ref, o_ref):
        o_ref[...] = (p_ref[...].astype(F32) + g_ref[...].astype(F32)).astype(o_ref.dtype)

    blk = pl.BlockSpec((None, tr, tc), lambda q, i, j, p: (q, i, j))
    return _prefetch_call(body, name, _sds((4, r, cols), part.dtype), (4, nr, nc), [pspec, blk], blk,
                          ("parallel", "parallel", "parallel"))(pos, part, got)


def _tiles(r, cols, itemsize):
    tr = _row_tile(r, itemsize)
    if tr < r or r * cols * 4 <= (2 << 20) or cols % 256:
        return tr, cols
    return r, 256


def _final_sum_adam(pos, sums, got, w, m, v, name):
    _, r, cols = w.shape
    tr, tc = _tiles(r, cols, sums.dtype.itemsize)

    def body(pos_ref, s_ref, g_ref, w_ref, m_ref, v_ref, go_ref, dl_ref, nm_ref, nv_ref):
        g = ((s_ref[...].astype(F32) + g_ref[0].astype(F32)) + g_ref[1].astype(F32)) + g_ref[2].astype(F32)
        dl, nm, nv = _adam(w_ref[...], g, m_ref[...], v_ref[...])
        go_ref[...] = g
        dl_ref[...] = dl
        nm_ref[...] = nm
        nv_ref[...] = nv

    big = pl.BlockSpec((None, tr, tc), lambda i, j, p: (0, i, j))
    return _prefetch_call(body, name, [_sds((1, r, cols))] * 4, (r // tr, cols // tc),
                          [pl.BlockSpec((None, tr, tc), lambda i, j, p: (p[1], i, j)),
                           pl.BlockSpec((3, tr, tc), lambda i, j, p: (0, i, j)), big, big, big],
                          [big] * 4, ("parallel", "parallel"))(pos, sums, got, w, m, v)


def _small_adam(g_all, w, m, v):
    npk = w.shape[1]

    def body(g_ref, w_ref, m_ref, v_ref, go_ref, dl_ref, nm_ref, nv_ref):
        g = g_ref[0:1, :]
        for k in range(1, NDEV):
            g = g + g_ref[k:k + 1, :]
        dl, nm, nv = _adam(w_ref[...], g, m_ref[...], v_ref[...])
        go_ref[...] = g
        dl_ref[...] = dl
        nm_ref[...] = nm
        nv_ref[...] = nv

    return _call(body, "small_adam", [_sds((1, npk))] * 4)(g_all, w, m, v)


SMALL = [("b_ada", 6 * D), ("norm1_g", D), ("norm2_g", D), ("final_norm_g", D), ("cf_ln_g", CFW), ("cf_ln_b", CFW),
         ("dn_norm_g", DH), ("dn_a_log", H), ("dn_dt_bias", H)]
LATE = ["dn_w_o", "cf_w_o", "w_out", "ffn_w_up", "ffn_w_down"]
LATE_SHAPE = {"dn_w_o": (NDEV, DNW, D // NDEV), "cf_w_o": (NDEV, CFW, D // NDEV), "w_out": (D, D),
              "ffn_w_up": (NDEV, D, 2 * FFN // NDEV), "ffn_w_down": (FFN, D)}
LATE_LAYOUT = {"dn_w_o": ("lead", NDEV), "cf_w_o": ("lead", NDEV), "w_out": ("rows", D // NDEV),
               "ffn_w_up": ("lead", NDEV), "ffn_w_down": ("rows", FFN // NDEV)}
LAYOUT = {"dn_w_o": ("cols", D // NDEV), "cf_w_o": ("cols", D // NDEV), "w_out": ("rows", D // NDEV),
          "ffn_w_up": ("cols", 2 * FFN // NDEV), "ffn_w_down": ("rows", FFN // NDEV),
          "w_in": ("lead", NDEV), "dn_conv_w": ("lead", NDEV), "cf_conv_w": ("lead", NDEV), "ffn_conv_w": ("lead", NDEV)}
NAMES = ["w_ada", "b_ada", "norm1_g", "w_in", "dn_conv_w", "dn_a_log", "dn_dt_bias", "dn_norm_g", "dn_w_o", "cf_conv_w",
         "cf_ln_g", "cf_ln_b", "cf_w_o", "w_out", "norm2_g", "ffn_w_up", "ffn_conv_w", "ffn_w_down", "final_norm_g"]


def _pack_small(d):
    rows = []
    for nm, n in SMALL:
        row = d[nm].reshape(1, n)
        pad = (-n) % LANE
        rows.append(jnp.pad(row, ((0, 0), (0, pad))) if pad else row)
    return jnp.concatenate(rows, axis=1)


def _unpack_small(row, shapes):
    out, off = {}, 0
    for nm, n in SMALL:
        out[nm] = row[0, off:off + n].reshape(shapes[nm])
        off += n + ((-n) % LANE)
    return out


def _cols_from_gathered(g):
    return jnp.transpose(g, (1, 0, 2)).reshape(g.shape[1], NDEV * g.shape[2])


def _cols_to_parts(full):
    r, ctot = full.shape
    return jnp.transpose(full.reshape(r, NDEV, ctot // NDEV), (1, 0, 2))


def kernel(x, c, w_ada, b_ada, norm1_g, w_in, dn_conv_w, dn_a_log, dn_dt_bias, dn_norm_g, dn_w_o, cf_conv_w, cf_ln_g, cf_ln_b, cf_w_o, w_out, norm2_g, ffn_w_up, ffn_conv_w, ffn_w_down, final_norm_g, loss_target, m_w_ada, m_b_ada, m_norm1_g, m_w_in, m_dn_conv_w, m_dn_a_log, m_dn_dt_bias, m_dn_norm_g, m_dn_w_o, m_cf_conv_w, m_cf_ln_g, m_cf_ln_b, m_cf_w_o, m_w_out, m_norm2_g, m_ffn_w_up, m_ffn_conv_w, m_ffn_w_down, m_final_norm_g, v_w_ada, v_b_ada, v_norm1_g, v_w_in, v_dn_conv_w, v_dn_a_log, v_dn_dt_bias, v_dn_norm_g, v_dn_w_o, v_cf_conv_w, v_cf_ln_g, v_cf_ln_b, v_cf_w_o, v_w_out, v_norm2_g, v_ffn_w_up, v_ffn_conv_w, v_ffn_w_down, v_final_norm_g):
    args = locals()
    w = {nm: args[nm] for nm in NAMES}
    mo = {nm: args["m_" + nm] for nm in NAMES}
    vo = {nm: args["v_" + nm] for nm in NAMES}
    shapes = {nm: w[nm].shape for nm in NAMES}
    px, py, pc = _my_pos()
    me = _dev_index(px, py, pc)

    def mat(a):
        return a.reshape(a.shape[-2:])

    pos = jnp.stack([pc, 2 * px + py, me]).astype(jnp.int32)

    first = ["w_in", "dn_conv_w", "cf_conv_w", "ffn_conv_w"]
    tr_in = lambda a: jnp.transpose(a, (0, 2, 1))
    got = _all_gather([tr_in(w["w_in"]).astype(BF16)] + [mat(w[nm]) for nm in first[1:]] + [c], "gather_first")
    full = {nm: _cols_from_gathered(g) for nm, g in zip(first[1:], got[1:-1])}
    c_all = got[-1].reshape(NDEV, D)
    w_in_p = _pad_win(got[0].reshape(NIN, D))

    ncol = 6 * D // NDEV
    b_sh = lax.dynamic_slice(b_ada.reshape(1, 6 * D), (0, me * ncol), (1, ncol))
    mod_sh = _ada_fwd(c_all, mat(w_ada), b_sh)
    mod_all = _all_gather([mod_sh], "gather_mod")[0]
    mod = lax.dynamic_index_in_dim(mod_all, me, axis=1, keepdims=False).reshape(1, 6 * D)

    late = {}
    dep = mod_all
    for grp, names in (("mix", LATE[:3]), ("ffn", LATE[3:])):
        shards = [mat(w[nm]).astype(BF16) for nm in names]
        lays = [LATE_LAYOUT[nm] for nm in names]
        bufs = [_place_own(pos, s, lax.empty(LATE_SHAPE[nm], BF16), lay, "place_" + nm)
                for nm, s, lay in zip(names, shards, lays)]
        started = _gather_ici_start(shards, bufs, lays, dep, "gather_" + grp + "_start")
        late[grp] = (lays,) + tuple(started)
        dep = started[4]

    res = {}

    class Comm:
        token0 = late["ffn"][5][0, 0]
        pending = {}

        @staticmethod
        def late_weights(grp, after):
            lays, ssem, rsem, shards, bufs, _ = late[grp]
            shards, bufs = _gather_ici_wait(ssem, rsem, shards, bufs, lays, after, "gather_" + grp + "_wait")
            return _gather_pair(shards, bufs, lays, "gather_" + grp + "_pair")

        @staticmethod
        def grads_begin(group, gd):
            names = list(gd)
            lays = [LAYOUT[nm] for nm in names]
            gl = []
            for nm in names:
                if nm == "w_in":
                    gl.append(_unpad_win(gd[nm]).reshape(NDEV, NSH, D))
                else:
                    gl.append(_cols_to_parts(gd[nm]) if LAYOUT[nm][0] == "lead" else gd[nm])
            started = _pair_exchange_start(gl, lays, "rs_pair_start_" + group)
            Comm.pending[group] = (names, lays) + tuple(started[:4])
            return started[4][0, 0]

        @staticmethod
        def grads_continue(group, after):
            names, lays, ssem, rsem, gl, lands = Comm.pending[group]
            if group == "c":
                after = Comm.finish("a", after)
            gl, from_sib = _pair_exchange_wait(ssem, rsem, gl, lands, lays, after, "rs_pair_wait_" + group)
            sums = [_pair_sum(pos, g, r, lay, "rs_pair_sum_" + nm) for nm, g, r, lay in zip(names, gl, from_sib, lays)]
            started = _chip_exchange_start(sums, "rs_chips_start_" + group)
            Comm.pending[group] = (names,) + tuple(started[:4])
            return started[4][0, 0]

        @staticmethod
        def finish(group, after):
            names, ssem, rsem, sums, lands = Comm.pending[group]
            sums, lands = _chip_exchange_wait(ssem, rsem, sums, lands, after, "rs_chips_wait_" + group)
            for nm, s, r in zip(names, sums, lands):
                if nm == "w_in":
                    outs = _final_sum_adam(pos, s, r, tr_in(w[nm]), tr_in(mo[nm]), tr_in(vo[nm]), "adam_" + nm)
                    res[nm] = [tr_in(o) for o in outs]
                else:
                    res[nm] = _final_sum_adam(pos, s, r, w[nm], mo[nm], vo[nm], "adam_" + nm)
            return res[names[-1]][0]

    vec = lambda a: a.reshape(1, -1)
    loss, grad_x, small = _local_step(
        x.reshape(S, D), loss_target.reshape(S, D), mod, vec(norm1_g), vec(norm2_g), vec(final_norm_g), w_in_p,
        full["dn_conv_w"], vec(dn_a_log), vec(dn_dt_bias), vec(dn_norm_g), full["cf_conv_w"], vec(cf_ln_g),
        vec(cf_ln_b), full["ffn_conv_w"], Comm)

    done_b = Comm.finish("b", grad_x)

    small["b_ada"] = small.pop("mod")
    packed = _pack_small(small) + 0.0 * done_b.reshape(-1)[0]
    g_small = _all_gather([packed], "gather_small")[0].reshape(NDEV, -1)
    outs = _small_adam(g_small, _pack_small({nm: w[nm] for nm, _ in SMALL}), _pack_small({nm: mo[nm] for nm, _ in SMALL}),
                       _pack_small({nm: vo[nm] for nm, _ in SMALL}))
    unpacked = [_unpack_small(o, shapes) for o in outs]
    for nm, _ in SMALL:
        res[nm] = [u[nm] for u in unpacked]

    dmod_sel = lax.dynamic_slice(g_small[:, :6 * D], (0, me * ncol), (NDEV, ncol))
    outs = _ada_bwd_adam(c_all, dmod_sel, mat(w_ada), mat(m_w_ada), mat(v_w_ada))
    res["w_ada"] = [o.reshape(shapes["w_ada"]) for o in outs]
    Comm.finish("c", jnp.concatenate([done_b.reshape(-1)[:LANE], outs[0].reshape(-1)[:LANE]]))

    loss = lax.psum(loss.reshape(()), ("x", "y", "c"))
    out = [loss, grad_x.reshape(x.shape)]
    for k in range(4):
        out += [res[nm][k] for nm in NAMES]
    return tuple(out)
```

```python
import functools

import jax
import jax.numpy as jnp
from jax import lax
from jax.experimental import pallas as pl
from jax.experimental.pallas import tpu as pltpu

F32 = jnp.float32
BF16 = jnp.bfloat16
HI = lax.Precision.HIGHEST
MESH = pl.DeviceIdType.MESH
ANY = pl.BlockSpec(memory_space=pl.ANY)

NDEV = 8
D = 2048
S = 2048
H = 8
DH = 128
DNW = H * DH
CFW = 1024
CFK = 31
DNK = 4
FFN = 5632
FFK = 3
CH = 64
NCH = S // CH
EPS = 1e-6
NIN = 10256
NINP = 10368
O_Z, O_GA, O_GB, O_GLU, O_SM = 3072, 4096, 6144, 8192, 10240
LANE = 128
TS = 256
VMEM_LIMIT = 56 * 1024 * 1024

ADAM_LR, ADAM_B1, ADAM_B2, ADAM_EPS, ADAM_WD, ADAM_STEP = 0.001, 0.9, 0.999, 1e-08, 0.01, 10


def _call(body, name, out_shape, grid=(), in_specs=None, out_specs=None, scratch=(), sem=None, aliases=None):
    kw = {}
    if aliases:
        kw["input_output_aliases"] = aliases
    if in_specs is not None:
        kw["in_specs"] = in_specs
    if out_specs is not None:
        kw["out_specs"] = out_specs
    return pl.pallas_call(
        body, out_shape=out_shape, grid=grid, scratch_shapes=scratch, name=name,
        compiler_params=pltpu.CompilerParams(dimension_semantics=sem, vmem_limit_bytes=VMEM_LIMIT), **kw)


def _sds(shape, dtype=F32):
    return jax.ShapeDtypeStruct(shape, dtype)


def _tile(dim, pref):
    if dim <= pref:
        return dim
    best = None
    for t in range(LANE, pref + 1, LANE):
        if dim % t == 0:
            best = t
    assert best is not None, (dim, pref)
    return best


def _sigmoid(x):
    return 1.0 / (1.0 + jnp.exp(-x))


def _silu(x):
    return x * _sigmoid(x)


def _dsilu(x):
    s = _sigmoid(x)
    return s * (1.0 + x * (1.0 - s))


def _silu_both(x):
    s = _sigmoid(x)
    return x * s, s * (1.0 + x * (1.0 - s))


def _softplus(x):
    return jnp.maximum(x, 0.0) + jnp.log(1.0 + jnp.exp(-jnp.abs(x)))


def _dot(a, b, dims, precision=None):
    return lax.dot_general(a, b, (dims, ((), ())), preferred_element_type=F32, precision=precision)


NN = ((1,), (0,))
NT = ((1,), (1,))
TN = ((0,), (0,))


def _my_pos():
    return lax.axis_index("x"), lax.axis_index("y"), lax.axis_index("c")


def _mm(a, b, mode, out_dtype, name, tm=1024, tn=1024, tk=2048, a2=None, b2=None, dep=None):
    sharded = b.ndim == 3
    if sharded and mode == "nn":
        cs = b.shape[2]
        (m, k), n = a.shape, NDEV * cs
        gs = max(1, tn // cs)
        tm, tn, tk = _tile(m, tm), gs * cs, _tile(k, tk)
    elif sharded:
        assert mode == "nt"
        cs = b.shape[2]
        m, n, k = a.shape[0], b.shape[1], NDEV * cs
        gs = max(1, tk // cs)
        tm, tn, tk = _tile(m, tm), _tile(n, tn), gs * cs
    else:
        if mode == "nn":
            (m, k), (k2, n) = a.shape, b.shape
        elif mode == "nt":
            (m, k), (n, k2) = a.shape, b.shape
        else:
            (k, m), (k2, n) = a.shape, b.shape
        assert k == k2, (a.shape, b.shape, mode)
        n = n * (2 if b2 is not None else 1)
        tm, tn, tk = _tile(m, tm), _tile(n // (2 if b2 is not None else 1), tn), _tile(k, tk)
    nk, nj = k // tk, n // tn
    halfk, halfj = nk // 2, nj // 2
    dims = {"nn": NN, "nt": NT, "tn": TN}[mode]

    n_in = 2 + (a2 is not None) + (b2 is not None) + (dep is not None)

    def body(*refs):
        a_ref, b_ref = refs[0], refs[1]
        x_ref = refs[2] if (a2 is not None or b2 is not None) else None
        o_ref = refs[n_in]
        acc_ref = refs[n_in + 1] if nk > 1 else None
        j, kk = pl.program_id(1), pl.program_id(2)

        if nk > 1:
            @pl.when(kk == 0)
            def _():
                acc_ref[...] = jnp.zeros_like(acc_ref)

        def accumulate(product, cols=slice(None)):
            if nk == 1:
                o_ref[:, cols] = product().astype(o_ref.dtype)
            else:
                acc_ref[:, cols] += product()

        if sharded and mode == "nn":
            for q in range(gs):
                accumulate(lambda q=q: _dot(a_ref[...], b_ref[q], NN), slice(q * cs, (q + 1) * cs))
        elif sharded:
            def contract(lhs_ref):
                def product():
                    part = None
                    for q in range(gs):
                        term = _dot(lhs_ref[:, q * cs:(q + 1) * cs], b_ref[q], NT)
                        part = term if part is None else part + term
                    return part
                accumulate(product)

            if a2 is None:
                contract(a_ref)
            else:
                pl.when(kk < halfk)(lambda: contract(a_ref))
                pl.when(kk >= halfk)(lambda: contract(x_ref))
        elif b2 is not None:
            pl.when(j < halfj)(lambda: accumulate(lambda: _dot(a_ref[...], b_ref[...], dims)))
            pl.when(j >= halfj)(lambda: accumulate(lambda: _dot(a_ref[...], x_ref[...], dims)))
        else:
            accumulate(lambda: _dot(a_ref[...], b_ref[...], dims))

        if nk > 1:
            @pl.when(kk == nk - 1)
            def _():
                o_ref[...] = acc_ref[...].astype(o_ref.dtype)

    ins, in_specs = [a], []
    if mode == "tn":
        in_specs.append(pl.BlockSpec((tk, tm), lambda i, j, kk: (kk, i)))
    elif a2 is not None:
        in_specs.append(pl.BlockSpec((tm, tk), lambda i, j, kk: (i, jnp.minimum(kk, halfk - 1))))
    else:
        in_specs.append(pl.BlockSpec((tm, tk), lambda i, j, kk: (i, kk)))
    ins.append(b)
    if sharded and mode == "nn":
        in_specs.append(pl.BlockSpec((gs, tk, cs), lambda i, j, kk: (j, kk, 0)))
    elif sharded:
        in_specs.append(pl.BlockSpec((gs, tn, cs), lambda i, j, kk: (kk, j, 0)))
    elif mode == "nt":
        in_specs.append(pl.BlockSpec((tn, tk), lambda i, j, kk: (j, kk)))
    elif b2 is not None:
        in_specs.append(pl.BlockSpec((tk, tn), lambda i, j, kk: (kk, jnp.minimum(j, halfj - 1))))
    else:
        in_specs.append(pl.BlockSpec((tk, tn), lambda i, j, kk: (kk, j)))
    if a2 is not None:
        ins.append(a2)
        in_specs.append(pl.BlockSpec((tm, tk), lambda i, j, kk: (i, jnp.maximum(kk - halfk, 0))))
    if b2 is not None:
        ins.append(b2)
        in_specs.append(pl.BlockSpec((tk, tn), lambda i, j, kk: (kk, jnp.maximum(j - halfj, 0))))
    if dep is not None:
        ins.append(dep)
        in_specs.append(ANY)
    return _call(body, name, _sds((m, n), out_dtype), grid=(m // tm, nj, nk),
                 in_specs=in_specs, out_specs=pl.BlockSpec((tm, tn), lambda i, j, kk: (i, j)),
                 scratch=[pltpu.VMEM((tm, tn), F32)] if nk > 1 else [],
                 sem=("parallel", "parallel", "arbitrary"))(*ins)


def _ada_fwd(c_all, w_sh, b_sh):
    n = w_sh.shape[1]
    tn = 512

    def body(c_ref, w_ref, b_ref, o_ref):
        ca = _silu(c_ref[...]).astype(BF16)
        o_ref[...] = _dot(ca, w_ref[...].astype(BF16), NN) + b_ref[...]

    return _call(body, "ada_fwd", _sds((NDEV, n)), grid=(n // tn,),
                 in_specs=[pl.BlockSpec((NDEV, D), lambda j: (0, 0)), pl.BlockSpec((D, tn), lambda j: (0, j)),
                           pl.BlockSpec((1, tn), lambda j: (0, j))],
                 out_specs=pl.BlockSpec((NDEV, tn), lambda j: (0, j)), sem=("parallel",))(c_all, w_sh, b_sh)


def _adam(w, g, m, v):
    m = ADAM_B1 * m + (1.0 - ADAM_B1) * g
    v = ADAM_B2 * v + (1.0 - ADAM_B2) * (g * g)
    m_hat = m / (1.0 - ADAM_B1 ** ADAM_STEP)
    v_hat = v / (1.0 - ADAM_B2 ** ADAM_STEP)
    delta = -ADAM_LR * (m_hat / (jnp.sqrt(v_hat) + ADAM_EPS) + ADAM_WD * w)
    return delta, m, v


def _ada_bwd_adam(c_all, dmod_sel, w, m, v):
    r, n = w.shape
    tr = 256

    def body(c_ref, d_ref, w_ref, m_ref, v_ref, g_ref, dl_ref, nm_ref, nv_ref):
        ca = _silu(c_ref[...])
        g = _dot(ca, d_ref[...], TN, precision=HI)
        dl, nm, nv = _adam(w_ref[...], g, m_ref[...], v_ref[...])
        g_ref[...] = g
        dl_ref[...] = dl
        nm_ref[...] = nm
        nv_ref[...] = nv

    big = pl.BlockSpec((tr, n), lambda i: (i, 0))
    return _call(body, "ada_bwd_adam", [_sds((r, n))] * 4, grid=(r // tr,),
                 in_specs=[pl.BlockSpec((NDEV, tr), lambda i: (0, i)), pl.BlockSpec((NDEV, n), lambda i: (0, 0)),
                           big, big, big],
                 out_specs=[big] * 4, sem=("parallel",))(c_all, dmod_sel, w, m, v)


def _row_spec(width=D):
    return pl.BlockSpec((TS, width), lambda i: (i, 0))


def _vec_spec(width=D):
    return pl.BlockSpec((1, width), lambda i: (0, 0))


def _acc_spec(width=D):
    return pl.BlockSpec((8, width), lambda i: (0, 0))


def _norm_mod(x, g, sc, sh, name):
    def body(x_ref, g_ref, sc_ref, sh_ref, o_ref):
        xv = x_ref[...]
        r = lax.rsqrt(jnp.mean(xv * xv, axis=-1, keepdims=True) + EPS)
        o_ref[...] = ((xv * r) * g_ref[...] * (1.0 + sc_ref[...]) + sh_ref[...]).astype(BF16)

    return _call(body, name, _sds((S, D), BF16), grid=(S // TS,),
                 in_specs=[_row_spec(), _vec_spec(), _vec_spec(), _vec_spec()], out_specs=_row_spec(),
                 sem=("parallel",))(x, g, sc, sh)


def _resid_norm_mod(x, mix, gt, g, sc, sh, name):
    def body(x_ref, mix_ref, gt_ref, g_ref, sc_ref, sh_ref, x2_ref, o_ref):
        xv = x_ref[...] + gt_ref[...] * mix_ref[...]
        x2_ref[...] = xv
        r = lax.rsqrt(jnp.mean(xv * xv, axis=-1, keepdims=True) + EPS)
        o_ref[...] = ((xv * r) * g_ref[...] * (1.0 + sc_ref[...]) + sh_ref[...]).astype(BF16)

    return _call(body, name, [_sds((S, D)), _sds((S, D), BF16)], grid=(S // TS,),
                 in_specs=[_row_spec(), _row_spec()] + [_vec_spec()] * 4, out_specs=[_row_spec(), _row_spec()],
                 sem=("parallel",))(x, mix, gt, g, sc, sh)


def _acc_rows(acc_ref, rows):
    @pl.when(pl.program_id(0) == 0)
    def _():
        acc_ref[...] = jnp.zeros_like(acc_ref)

    for k, row in enumerate(rows):
        acc_ref[k:k + 1, :] += row


def _loss_head(x2, f, tgt, gt2, gf):
    def body(x2_ref, f_ref, t_ref, gt_ref, gf_ref, dx_ref, df_ref, acc_ref):
        fv = f_ref[...]
        x3 = x2_ref[...] + gt_ref[...] * fv
        r = lax.rsqrt(jnp.mean(x3 * x3, axis=-1, keepdims=True) + EPS)
        xn = x3 * r
        e = xn * gf_ref[...] - t_ref[...]
        loss = 0.5 * jnp.sum(jnp.mean(e * e, axis=-1, keepdims=True), axis=0, keepdims=True)
        dy = e * (1.0 / D)
        dxn = dy * gf_ref[...]
        dx3 = r * (dxn - xn * jnp.mean(dxn * xn, axis=-1, keepdims=True))
        dx_ref[...] = dx3
        df_ref[...] = (dx3 * gt_ref[...]).astype(BF16)
        _acc_rows(acc_ref, [jnp.sum(dy * xn, axis=0, keepdims=True), jnp.sum(dx3 * fv, axis=0, keepdims=True),
                            jnp.broadcast_to(loss, (1, D))])

    return _call(body, "loss_head", [_sds((S, D)), _sds((S, D), BF16), _sds((8, D))], grid=(S // TS,),
                 in_specs=[_row_spec(), _row_spec(), _row_spec(), _vec_spec(), _vec_spec()],
                 out_specs=[_row_spec(), _row_spec(), _acc_spec()], sem=("arbitrary",))(x2, f, tgt, gt2, gf)


def _norm_mod_bwd(dhn, x, dres, g, sc, name, mix=None, gt=None):
    gated = mix is not None

    def body(*refs):
        if gated:
            dhn_ref, x_ref, dres_ref, g_ref, sc_ref, mix_ref, gt_ref, dx_ref, dmix_ref, acc_ref = refs
        else:
            dhn_ref, x_ref, dres_ref, g_ref, sc_ref, dx_ref, acc_ref = refs
        xv = x_ref[...]
        dh = dhn_ref[...]
        r = lax.rsqrt(jnp.mean(xv * xv, axis=-1, keepdims=True) + EPS)
        xn = xv * r
        gv = g_ref[...]
        sc1 = 1.0 + sc_ref[...]
        dxn = dh * gv * sc1
        dx = dres_ref[...] + r * (dxn - xn * jnp.mean(dxn * xn, axis=-1, keepdims=True))
        dx_ref[...] = dx
        rows = [jnp.sum(dh, axis=0, keepdims=True), jnp.sum(dh * xn * gv, axis=0, keepdims=True),
                jnp.sum(dh * xn * sc1, axis=0, keepdims=True)]
        if gated:
            rows.append(jnp.sum(dx * mix_ref[...], axis=0, keepdims=True))
            dmix_ref[...] = (dx * gt_ref[...]).astype(BF16)
        _acc_rows(acc_ref, rows)

    ins = [dhn, x, dres, g, sc]
    in_specs = [_row_spec(), _row_spec(), _row_spec(), _vec_spec(), _vec_spec()]
    outs = [_sds((S, D))]
    out_specs = [_row_spec()]
    if gated:
        ins += [mix, gt]
        in_specs += [_row_spec(), _vec_spec()]
        outs.append(_sds((S, D), BF16))
        out_specs.append(_row_spec())
    outs.append(_sds((8, D)))
    out_specs.append(_acc_spec())
    return _call(body, name, outs, grid=(S // TS,), in_specs=in_specs, out_specs=out_specs,
                 sem=("arbitrary",))(*ins)


RC = 64


RC_WIDE = 256


def _conv_fwd_rows(pad_ref, w_ref, kw, head, r0, rc=RC):
    acc = None
    for k in range(kw):
        term = w_ref[k:k + 1, :] * pad_ref[pl.ds(head - (kw - 1) + k + r0, rc), :]
        acc = term if acc is None else acc + term
    return acc


def _conv_bwd_rows(pad2_ref, w_ref, kw, r0, rc=RC):
    acc = None
    for k in range(kw):
        term = w_ref[k:k + 1, :] * pad2_ref[pl.ds(kw - 1 - k + r0, rc), :]
        acc = term if acc is None else acc + term
    return acc


def _conv_dw(pad_ref, dout_ref, dw_ref, kw, head, rc=RC):
    for k in range(kw):
        acc = None
        for r0 in range(0, S, rc):
            term = jnp.sum(pad_ref[pl.ds(head - (kw - 1) + k + r0, rc), :] * dout_ref[pl.ds(r0, rc), :],
                           axis=0, keepdims=True)
            acc = term if acc is None else acc + term
        dw_ref[k:k + 1, :] = acc


def _col_spec(width, off_blocks=0):
    return pl.BlockSpec((S, width), lambda j: (0, j + off_blocks))


def _dn_pre_fwd(proj, conv_w):
    head = 8

    def body(x_ref, w_ref, o_ref, pad_ref):
        j = pl.program_id(0)
        pad_ref[pl.ds(0, head), :] = jnp.zeros((head, DH), F32)
        pad_ref[pl.ds(head, S), :] = x_ref[...]
        scale = jnp.where(j < H, DH ** -0.5, 1.0)
        for r0 in range(0, S, RC):
            y = _silu(_conv_fwd_rows(pad_ref, w_ref, DNK, head, r0))
            rinv = lax.rsqrt(jnp.sum(y * y, axis=-1, keepdims=True) + EPS)
            o_ref[pl.ds(r0, RC), :] = jnp.where(j < 2 * H, y * rinv * scale, y)

    return _call(body, "dn_pre_fwd", _sds((S, 3 * DNW)), grid=(3 * H,),
                 in_specs=[_col_spec(DH), pl.BlockSpec((DNK, DH), lambda j: (0, j))], out_specs=_col_spec(DH),
                 scratch=[pltpu.VMEM((S + head, DH), F32)], sem=("parallel",))(proj, conv_w)


def _dn_pre_bwd(dq, dk, dv, proj, conv_w, dproj):
    head = 8

    def body(dq_ref, dk_ref, dv_ref, x_ref, w_ref, dproj_in, dx_ref, dw_ref, pad_ref, pad2_ref):
        j = pl.program_id(0)
        pad_ref[pl.ds(0, head), :] = jnp.zeros((head, DH), F32)
        pad_ref[pl.ds(head, S), :] = x_ref[...]
        pad2_ref[pl.ds(S, head), :] = jnp.zeros((head, DH), F32)
        scale = jnp.where(j < H, DH ** -0.5, 1.0)
        for r0 in range(0, S, RC_WIDE):
            xc = _conv_fwd_rows(pad_ref, w_ref, DNK, head, r0, RC_WIDE)
            y, dy_dxc = _silu_both(xc)
            rinv = lax.rsqrt(jnp.sum(y * y, axis=-1, keepdims=True) + EPS)
            yn = y * rinv
            rows = pl.ds(r0, RC_WIDE)
            do = jnp.where(j < H, dq_ref[rows, :], jnp.where(j < 2 * H, dk_ref[rows, :], dv_ref[rows, :]))
            dy_n = scale * rinv * (do - yn * jnp.sum(do * yn, axis=-1, keepdims=True))
            dy = jnp.where(j < 2 * H, dy_n, do)
            pad2_ref[rows, :] = dy * dy_dxc
        for r0 in range(0, S, RC_WIDE):
            dx_ref[pl.ds(r0, RC_WIDE), :] = _conv_bwd_rows(pad2_ref, w_ref, DNK, r0, RC_WIDE).astype(BF16)
        _conv_dw(pad_ref, pad2_ref, dw_ref, DNK, head, RC_WIDE)

    wspec = pl.BlockSpec((DNK, DH), lambda j: (0, j))
    head_col = lambda lo: pl.BlockSpec((S, DH), lambda j: (0, jnp.clip(j - lo, 0, H - 1)))
    return _call(body, "dn_pre_bwd", [_sds((S, NINP), BF16), _sds((DNK, 3 * DNW))], grid=(3 * H,),
                 in_specs=[head_col(0), head_col(H), head_col(2 * H), _col_spec(DH), wspec, ANY],
                 out_specs=[_col_spec(DH), wspec],
                 scratch=[pltpu.VMEM((S + head, DH), F32), pltpu.VMEM((S + head, DH), F32)],
                 sem=("parallel",), aliases={5: 0})(dq, dk, dv, proj, conv_w, dproj)


CF_HEAD = 32
CF_VAL = pl.BlockSpec((S, LANE), lambda j: (0, O_GLU // LANE + 2 * j))
CF_GL = pl.BlockSpec((S, LANE), lambda j: (0, O_GLU // LANE + 2 * j + 1))


def _cf_conv_fwd(proj, conv_w):
    def body(val_ref, gl_ref, w_ref, o_ref, pad_ref):
        pad_ref[pl.ds(0, CF_HEAD), :] = jnp.zeros((CF_HEAD, LANE), F32)
        pad_ref[pl.ds(CF_HEAD, S), :] = val_ref[...] * _sigmoid(gl_ref[...])
        for r0 in range(0, S, RC):
            o_ref[pl.ds(r0, RC), :] = _conv_fwd_rows(pad_ref, w_ref, CFK, CF_HEAD, r0)

    wspec = pl.BlockSpec((CFK, LANE), lambda j: (0, j))
    return _call(body, "cf_conv_fwd", _sds((S, CFW)), grid=(CFW // LANE,),
                 in_specs=[CF_VAL, CF_GL, wspec], out_specs=_col_spec(LANE),
                 scratch=[pltpu.VMEM((S + CF_HEAD, LANE), F32)], sem=("parallel",))(proj, proj, conv_w)


def _cf_conv_bwd(du1, proj, conv_w, dproj):
    def body(d_ref, val_ref, gl_ref, w_ref, dproj_in, dp_ref, dw_ref, pad_ref, pad2_ref):
        sg = _sigmoid(gl_ref[...])
        pad_ref[pl.ds(0, CF_HEAD), :] = jnp.zeros((CF_HEAD, LANE), F32)
        pad_ref[pl.ds(CF_HEAD, S), :] = val_ref[...] * sg
        pad2_ref[pl.ds(0, S), :] = d_ref[...]
        pad2_ref[pl.ds(S, CF_HEAD), :] = jnp.zeros((CF_HEAD, LANE), F32)
        for r0 in range(0, S, RC):
            du0 = _conv_bwd_rows(pad2_ref, w_ref, CFK, r0)
            rows = pl.ds(r0, RC)
            sgr = _sigmoid(gl_ref[rows, :])
            dp_ref[rows, 0:LANE] = (du0 * sgr).astype(BF16)
            dp_ref[rows, LANE:2 * LANE] = (du0 * val_ref[rows, :] * sgr * (1.0 - sgr)).astype(BF16)
        _conv_dw(pad_ref, pad2_ref, dw_ref, CFK, CF_HEAD)

    wspec = pl.BlockSpec((CFK, LANE), lambda j: (0, j))
    return _call(body, "cf_conv_bwd", [_sds((S, NINP), BF16), _sds((CFK, CFW))], grid=(CFW // LANE,),
                 in_specs=[_col_spec(LANE), CF_VAL, CF_GL, wspec, ANY],
                 out_specs=[pl.BlockSpec((S, 2 * LANE), lambda j: (0, O_GLU // (2 * LANE) + j)), wspec],
                 scratch=[pltpu.VMEM((S + CF_HEAD, LANE), F32), pltpu.VMEM((S + CF_HEAD, LANE), F32)],
                 sem=("parallel",), aliases={4: 0})(du1, proj, proj, conv_w, dproj)


def _cf_ln_fwd(u1, g, b):
    def body(u_ref, g_ref, b_ref, o_ref):
        u = u_ref[...]
        mu = jnp.mean(u, axis=-1, keepdims=True)
        xc = u - mu
        y = xc * lax.rsqrt(jnp.mean(xc * xc, axis=-1, keepdims=True) + EPS)
        o_ref[...] = _silu(y * g_ref[...] + b_ref[...]).astype(BF16)

    return _call(body, "cf_ln_fwd", _sds((S, CFW), BF16), grid=(S // TS,),
                 in_specs=[_row_spec(CFW), _vec_spec(CFW), _vec_spec(CFW)], out_specs=_row_spec(CFW),
                 sem=("parallel",))(u1, g, b)


def _cf_ln_bwd(du3, u1, g, b):
    def body(d_ref, u_ref, g_ref, b_ref, du_ref, acc_ref):
        u = u_ref[...]
        mu = jnp.mean(u, axis=-1, keepdims=True)
        xc = u - mu
        rstd = lax.rsqrt(jnp.mean(xc * xc, axis=-1, keepdims=True) + EPS)
        xh = xc * rstd
        du2 = d_ref[...] * _dsilu(xh * g_ref[...] + b_ref[...])
        dxh = du2 * g_ref[...]
        du_ref[...] = rstd * (dxh - jnp.mean(dxh, axis=-1, keepdims=True)
                              - xh * jnp.mean(dxh * xh, axis=-1, keepdims=True))
        _acc_rows(acc_ref, [jnp.sum(du2 * xh, axis=0, keepdims=True), jnp.sum(du2, axis=0, keepdims=True)])

    return _call(body, "cf_ln_bwd", [_sds((S, CFW)), _sds((8, CFW))], grid=(S // TS,),
                 in_specs=[_row_spec(CFW), _row_spec(CFW), _vec_spec(CFW), _vec_spec(CFW)],
                 out_specs=[_row_spec(CFW), _acc_spec(CFW)], sem=("arbitrary",))(du3, u1, g, b)


FB = 256
FNB = FFN // FB
FF_HEAD = 8


def _ffn_mid_fwd(upall, conv_w):
    def body(gate_ref, up_ref, w_ref, o_ref, pad_ref):
        pad_ref[pl.ds(0, FF_HEAD), :] = jnp.zeros((FF_HEAD, FB), F32)
        pad_ref[pl.ds(FF_HEAD, S), :] = gate_ref[...]
        for r0 in range(0, S, RC):
            gc = _conv_fwd_rows(pad_ref, w_ref, FFK, FF_HEAD, r0)
            o_ref[pl.ds(r0, RC), :] = (_silu(gc) * up_ref[pl.ds(r0, RC), :]).astype(BF16)

    wspec = pl.BlockSpec((FFK, FB), lambda j: (0, j))
    return _call(body, "ffn_mid_fwd", _sds((S, FFN), BF16), grid=(FNB,),
                 in_specs=[_col_spec(FB), _col_spec(FB, FNB), wspec], out_specs=_col_spec(FB),
                 scratch=[pltpu.VMEM((S + FF_HEAD, FB), F32)], sem=("parallel",))(upall, upall, conv_w)


def _ffn_mid_bwd(dh, upall, conv_w):
    def body(d_ref, gate_ref, up_ref, w_ref, dgate_ref, dup_ref, dw_ref, pad_ref, pad2_ref):
        pad_ref[pl.ds(0, FF_HEAD), :] = jnp.zeros((FF_HEAD, FB), F32)
        pad_ref[pl.ds(FF_HEAD, S), :] = gate_ref[...]
        pad2_ref[pl.ds(S, FF_HEAD), :] = jnp.zeros((FF_HEAD, FB), F32)
        for r0 in range(0, S, RC):
            rows = pl.ds(r0, RC)
            gc = _conv_fwd_rows(pad_ref, w_ref, FFK, FF_HEAD, r0)
            dhv = d_ref[rows, :]
            act, dact = _silu_both(gc)
            dup_ref[rows, :] = (dhv * act).astype(BF16)
            pad2_ref[rows, :] = dhv * up_ref[rows, :] * dact
        for r0 in range(0, S, RC):
            dgate_ref[pl.ds(r0, RC), :] = _conv_bwd_rows(pad2_ref, w_ref, FFK, r0).astype(BF16)
        _conv_dw(pad_ref, pad2_ref, dw_ref, FFK, FF_HEAD)

    wspec = pl.BlockSpec((FFK, FB), lambda j: (0, j))
    return _call(body, "ffn_mid_bwd", [_sds((S, FFN), BF16), _sds((S, FFN), BF16), _sds((FFK, FFN))],
                 grid=(FNB,), in_specs=[_col_spec(FB), _col_spec(FB), _col_spec(FB, FNB), wspec],
                 out_specs=[_col_spec(FB), _col_spec(FB), wspec],
                 scratch=[pltpu.VMEM((S + FF_HEAD, FB), F32), pltpu.VMEM((S + FF_HEAD, FB), F32)],
                 sem=("parallel",))(dh, upall, upall, conv_w)


GT = 256
SM_BLK = O_SM // LANE


def _chunk_tri(lower):
    r = lax.broadcasted_iota(jnp.int32, (GT, GT), 0)
    c = lax.broadcasted_iota(jnp.int32, (GT, GT), 1)
    same = (r // CH) == (c // CH)
    tri = (c <= r) if lower else (c >= r)
    return jnp.where(same & tri, 1.0, 0.0).astype(F32)


def _gates_fwd(proj, alog_v, dtb_v):
    def body(sm_ref, al_ref, dt_ref, o_ref):
        lane = lax.broadcasted_iota(jnp.int32, (GT, LANE), 1)
        tri = _chunk_tri(True)
        na = -jnp.exp(al_ref[...])
        for r0 in range(0, S, GT):
            sm = sm_ref[pl.ds(r0, GT), :]
            raw = jnp.where((lane >= H) & (lane < 2 * H), na * _softplus(sm + dt_ref[...]), 0.0)
            gc = _dot(tri, raw, NN, precision=HI)
            o_ref[pl.ds(r0, GT), :] = jnp.where(lane < H, _sigmoid(sm), gc)

    return _call(body, "gates_fwd", _sds((S, LANE)), grid=(1,),
                 in_specs=[pl.BlockSpec((S, LANE), lambda i: (0, SM_BLK)), _vec_spec(LANE), _vec_spec(LANE)],
                 out_specs=pl.BlockSpec((S, LANE), lambda i: (0, 0)), sem=("arbitrary",))(proj, alog_v, dtb_v)


def _gates_bwd(dgb, proj, alog_v, dtb_v, dproj):
    def body(d_ref, sm_ref, al_ref, dt_ref, dproj_in, o_ref, acc_ref):
        lane = lax.broadcasted_iota(jnp.int32, (GT, LANE), 1)
        is_g = (lane >= H) & (lane < 2 * H)
        tri = _chunk_tri(False)
        na = -jnp.exp(al_ref[...])
        d_al = jnp.zeros((1, LANE), F32)
        d_dt = jnp.zeros((1, LANE), F32)
        for r0 in range(0, S, GT):
            sm = sm_ref[pl.ds(r0, GT), :]
            dv = d_ref[pl.ds(r0, GT), :]
            z = sm + dt_ref[...]
            draw = _dot(tri, jnp.where(is_g, dv, 0.0), NN, precision=HI)
            dlogit = jnp.where(is_g, draw * na * _sigmoid(z), 0.0)
            d_al = d_al + jnp.sum(jnp.where(is_g, draw * na * _softplus(z), 0.0), axis=0, keepdims=True)
            d_dt = d_dt + jnp.sum(dlogit, axis=0, keepdims=True)
            bt = _sigmoid(sm)
            o_ref[pl.ds(r0, GT), :] = jnp.where(lane < H, dv * bt * (1.0 - bt), dlogit).astype(BF16)
        acc_ref[...] = jnp.zeros_like(acc_ref)
        acc_ref[0:1, :] = d_al
        acc_ref[1:2, :] = d_dt

    return _call(body, "gates_bwd", [_sds((S, NINP), BF16), _sds((8, LANE))], grid=(1,),
                 in_specs=[pl.BlockSpec((S, LANE), lambda i: (0, 0)), pl.BlockSpec((S, LANE), lambda i: (0, SM_BLK)),
                           _vec_spec(LANE), _vec_spec(LANE), ANY],
                 out_specs=[pl.BlockSpec((S, LANE), lambda i: (0, SM_BLK)), _acc_spec(LANE)],
                 sem=("arbitrary",), aliases={4: 0})(dgb, proj, alog_v, dtb_v, dproj)


HB = 4


def _each(fn, *lists):
    return [fn(*args) for args in zip(*lists)]


def _neumann_inv(a, eye):
    p = _each(lambda m: -m, a)
    t = _each(lambda m: eye + m, p)
    for _ in range(5):
        p = _each(lambda m: _dot(m, m, NN, precision=lax.Precision.HIGH), p)
        t = _each(lambda tt, pp: tt + _dot(tt, pp, NN, precision=lax.Precision.HIGH), t, p)
    return t


def _head_specs():
    q = pl.BlockSpec((S, HB * DH), lambda h: (0, h), pipeline_mode=ONE_BUF)
    k = pl.BlockSpec((S, HB * DH), lambda h: (0, H // HB + h), pipeline_mode=ONE_BUF)
    v = pl.BlockSpec((S, HB * DH), lambda h: (0, 2 * H // HB + h), pipeline_mode=ONE_BUF)
    gb = pl.BlockSpec((HB, S, DH), lambda h: (h, 0, 0), pipeline_mode=ONE_BUF)
    gr = pl.BlockSpec((HB, NCH, CH), lambda h: (h, 0, 0))
    return q, k, v, gb, gr


ONE_BUF = pl.Buffered(1)
ST_SPEC = pl.BlockSpec((HB, NCH, DH, DH), lambda h: (h, 0, 0, 0), pipeline_mode=ONE_BUF)
TM_SPEC = pl.BlockSpec((HB, NCH, CH, CH), lambda h: (h, 0, 0, 0), pipeline_mode=ONE_BUF)
HCOL = pl.BlockSpec((S, HB * DH), lambda h: (0, h), pipeline_mode=ONE_BUF)


def _delta_fwd(qkvn, gb, gr, bb):
    def body(q_ref, k_ref, v_ref, gb_ref, gr_ref, bb_ref, o_ref, st_ref, tm_ref):
        ri = lax.broadcasted_iota(jnp.int32, (CH, CH), 0)
        ci = lax.broadcasted_iota(jnp.int32, (CH, CH), 1)
        strict = ri > ci
        causal = ri >= ci
        eye = jnp.where(ri == ci, 1.0, 0.0).astype(F32)

        hs = list(range(HB))
        cols = [slice(hh * DH, (hh + 1) * DH) for hh in hs]
        bf = lambda m: m.astype(BF16)

        def local(n):
            rows = pl.ds(pl.multiple_of(n * CH, CH), CH)
            c = dict(rows=rows, n=n)
            c["q"] = [q_ref[rows, cc] for cc in cols]
            c["k"] = [k_ref[rows, cc] for cc in cols]
            c["v"] = [v_ref[rows, cc] for cc in cols]
            c["g"] = [gb_ref[hh, rows, :] for hh in hs]
            c["beta"] = [bb_ref[hh, rows, :] for hh in hs]
            diff = [c["g"][hh][:, :CH] - gr_ref[hh, pl.ds(n, 1), :] for hh in hs]
            c["el"] = _each(lambda d: jnp.exp(jnp.where(causal, d, 0.0)), diff)
            c["eg"] = _each(jnp.exp, c["g"])
            c["gl"] = _each(lambda m: m[CH - 1:CH, :], c["g"])
            c["kb"] = _each(lambda x, y: x * y, c["k"], c["beta"])
            c["kbf"] = _each(bf, c["k"])
            c["a"] = _each(lambda x, y, e: jnp.where(strict, _dot(bf(x), y, NT) * e, 0.0), c["kb"], c["kbf"], c["el"])
            return c

        def advance(c, t, sts):
            n, rows = c["n"], c["rows"]
            for hh in hs:
                tm_ref[hh, n] = t[hh]
                st_ref[hh, n] = sts[hh]
            sb = _each(bf, sts)
            r = _each(lambda vv, bb_, kk, ee, ss: vv * bb_ - _dot(bf(kk * ee), ss, NN), c["v"], c["beta"], c["kb"], c["eg"], sb)
            ub = _each(lambda tt, rr: bf(_dot(tt, rr, NN, precision=lax.Precision.HIGH)), t, r)
            p = _each(lambda qq, kk, e: jnp.where(causal, _dot(bf(qq), kk, NT) * e, 0.0), c["q"], c["kbf"], c["el"])
            o = _each(lambda qq, ee, ss, pp, uu: _dot(bf(qq * ee), ss, NN) + _dot(bf(pp), uu, NN), c["q"], c["eg"], sb, p, ub)
            for hh in hs:
                o_ref[rows, cols[hh]] = o[hh]
            kd = _each(lambda kk, l, gg: kk * jnp.exp(l - gg), c["k"], c["gl"], c["g"])
            return _each(lambda st, l, kk, uu: st * jnp.exp(l) + _dot(bf(kk), uu, TN), sts, c["gl"], kd, ub)

        def step(i, sts):
            c0, c1 = local(2 * i), local(2 * i + 1)
            t = _neumann_inv(c0["a"] + c1["a"], eye)
            sts = advance(c0, t[:HB], list(sts))
            return tuple(advance(c1, t[HB:], sts))

        lax.fori_loop(0, NCH // 2, step, tuple(jnp.zeros((DH, DH), F32) for _ in hs))

    q, k, v, gbs, grs = _head_specs()
    return _call(body, "delta_fwd", [_sds((S, DNW)), _sds((H, NCH, DH, DH)), _sds((H, NCH, CH, CH))], grid=(H // HB,),
                 in_specs=[q, k, v, gbs, grs, gbs], out_specs=[HCOL, ST_SPEC, TM_SPEC],
                 sem=("parallel",))(qkvn, qkvn, qkvn, gb, gr, bb)


def _delta_bwd(qkvn, gb, gr, bb, st_all, tm_all, do_all):
    def body(q_ref, k_ref, v_ref, gb_ref, gr_ref, bb_ref, st_ref, tm_ref, do_ref,
             dq_ref, dk_ref, dv_ref, dg_ref, db_ref):
        ri = lax.broadcasted_iota(jnp.int32, (CH, CH), 0)
        ci = lax.broadcasted_iota(jnp.int32, (CH, CH), 1)
        lo_s, lo_c, up_s, up_c = ri > ci, ri >= ci, ri < ci, ri <= ci
        last_row = lax.broadcasted_iota(jnp.int32, (CH, 1), 0) == CH - 1

        def rs(mat):
            return jnp.sum(mat, axis=1, keepdims=True)

        def total(mat):
            return jnp.sum(rs(mat), axis=0, keepdims=True)

        hs = list(range(HB))
        cols = [slice(hh * DH, (hh + 1) * DH) for hh in hs]
        bf = lambda m: m.astype(BF16)
        mul = lambda x, y: x * y
        spread = jnp.full((8, DH), 1.0 / DH, F32)

        def as_row(col):
            return _dot(spread, jnp.broadcast_to(col, (CH, DH)), NT, precision=HI)[0:1, :]

        def step(i, dss):
            ns = [NCH - 1 - 2 * i, NCH - 2 - 2 * i]
            rws = [pl.ds(pl.multiple_of(n * CH, CH), CH) for n in ns]
            idx = [(cc, hh) for cc in range(2) for hh in hs]
            q = [q_ref[rws[cc], cols[hh]] for cc, hh in idx]
            k = [k_ref[rws[cc], cols[hh]] for cc, hh in idx]
            v = [v_ref[rws[cc], cols[hh]] for cc, hh in idx]
            do = [do_ref[rws[cc], cols[hh]] for cc, hh in idx]
            g = [gb_ref[hh, rws[cc], :] for cc, hh in idx]
            beta = [bb_ref[hh, rws[cc], :] for cc, hh in idx]
            t = [tm_ref[hh, ns[cc]] for cc, hh in idx]
            st = [st_ref[hh, ns[cc]] for cc, hh in idx]
            diff = [gg[:, :CH] - gr_ref[hh, pl.ds(ns[cc], 1), :] for gg, (cc, hh) in zip(g, idx)]
            el = _each(lambda d: jnp.exp(jnp.where(lo_c, d, 0.0)), diff)
            eu = _each(lambda d: jnp.exp(jnp.where(up_c, -d, 0.0)), diff)
            eg = _each(jnp.exp, g)
            gl = _each(lambda m: m[CH - 1:CH, :], g)
            egl = _each(jnp.exp, gl)
            ekd = _each(lambda l, m: jnp.exp(l - m), gl, g)
            kb = _each(mul, k, beta)
            kbg = _each(mul, kb, eg)
            qg = _each(mul, q, eg)
            kd = _each(mul, k, ekd)
            qb, kbf, kbb = _each(bf, q), _each(bf, k), _each(bf, kb)
            kbgb, qgb, kdb = _each(bf, kbg), _each(bf, qg), _each(bf, kd)
            sb, dob = _each(bf, st), _each(bf, do)
            r = _each(lambda vv, b, x, s: vv * b - _dot(x, s, NN), v, beta, kbgb, sb)
            u = _each(lambda tt, rr: _dot(tt, rr, NN, precision=lax.Precision.HIGH), t, r)
            ub = _each(bf, u)
            kk = _each(lambda x, y: _dot(x, y, NT), kbb, kbf)
            qk = _each(lambda x, y: _dot(x, y, NT), qb, kbf)
            kkt = _each(lambda x, y: _dot(x, y, NT), kbf, kbb)
            qkt = _each(lambda x, y: _dot(x, y, NT), kbf, qb)
            pt = _each(lambda m, e: jnp.where(up_c, m * e, 0.0), qkt, eu)
            ds, du, dr, drb, ds_new = [], [], [], [], list(dss)
            for cc in range(2):
                sl = slice(cc * HB, (cc + 1) * HB)
                ds_c = ds_new
                dsb_c = _each(bf, ds_c)
                du_c = _each(lambda p, d, x, s: _dot(bf(p), d, NN) + _dot(x, s, NN), pt[sl], dob[sl], kdb[sl], dsb_c)
                dr_c = _each(lambda tt, d: _dot(tt, d, TN, precision=lax.Precision.HIGH), t[sl], du_c)
                drb_c = _each(bf, dr_c)
                ds_new = _each(lambda x, d, e, s, y, z: _dot(x, d, TN) + e * s - _dot(y, z, TN),
                               qgb[sl], dob[sl], egl[sl], ds_c, kbgb[sl], drb_c)
                ds, du, dr, drb = ds + ds_c, du + du_c, dr + dr_c, drb + drb_c
            dsb = _each(bf, ds)
            dpg = _each(lambda d, uu, e: jnp.where(lo_c, _dot(d, uu, NT), 0.0) * e, dob, ub, el)
            dpgt = _each(lambda uu, d, e: jnp.where(up_c, _dot(uu, d, NT), 0.0) * e, ub, dob, eu)
            dag = _each(lambda d, uu, e: -jnp.where(lo_s, _dot(d, uu, NT), 0.0) * e, drb, ub, el)
            dagt = _each(lambda uu, d, e: -jnp.where(up_s, _dot(uu, d, NT), 0.0) * e, ub, drb, eu)
            dqg = _each(lambda d, s: _dot(d, s, NT), dob, sb)
            dkbg = _each(lambda d, s: -_dot(d, s, NT), drb, sb)
            dkd = _each(lambda uu, s: _dot(uu, s, NT), ub, dsb)
            dkb =_each(lambda a, x, y, e: _dot(bf(a), x, NN) + y * e, dag, kbf, dkbg, eg)
            dk = _each(lambda a, x, p, y, z, e, w, b: _dot(bf(a), x, NN) + _dot(bf(p), y, NN) + z * e + w * b,
                       dagt, kbb, dpgt, qb, dkd, ekd, dkb, beta)
            dq = _each(lambda p, x, y, e: _dot(bf(p), x, NN) + y * e, dpg, kbf, dqg, eg)
            dkd_kd = _each(lambda x, y: rs(x * y), dkd, kd)
            dg = _each(lambda a, x, p, y, at, xt, pt_, yt, z, w, c, d, e:
                       rs(a * x + p * y) - rs(at * xt + pt_ * yt) + rs(z * w) + rs(c * d) - e,
                       dag, kk, dpg, qk, dagt, kkt, dpgt, qkt, dqg, qg, dkbg, kbg, dkd_kd)
            dgl = _each(lambda x, e, s, y: jnp.sum(x, axis=0, keepdims=True) + e[:, 0:1] * total(s * y), dkd_kd, egl, ds, st)
            dg = _each(lambda x, y: x + jnp.where(last_row, y, 0.0), dg, dgl)
            dbeta = _each(lambda x, y, z, w: rs(x * y) + rs(z * w), dkb, k, dr, v)
            for j, (cc, hh) in enumerate(idx):
                dq_ref[rws[cc], cols[hh]] = dq[j]
                dk_ref[rws[cc], cols[hh]] = dk[j]
                dv_ref[rws[cc], cols[hh]] = dr[j] * beta[j]
                dg_ref[hh, pl.ds(ns[cc], 1), :] = as_row(dg[j])
                db_ref[hh, pl.ds(ns[cc], 1), :] = as_row(dbeta[j])
            return tuple(ds_new)

        lax.fori_loop(0, NCH // 2, step, tuple(jnp.zeros((DH, DH), F32) for _ in hs))

    q, k, v, gbs, grs = _head_specs()
    return _call(body, "delta_bwd",
                 [_sds((S, DNW)), _sds((S, DNW)), _sds((S, DNW)), _sds((H, NCH, CH)), _sds((H, NCH, CH))], grid=(H // HB,),
                 in_specs=[q, k, v, gbs, grs, gbs, ST_SPEC, TM_SPEC, HCOL], out_specs=[HCOL, HCOL, HCOL, grs, grs],
                 sem=("parallel",))(qkvn, qkvn, qkvn, gb, gr, bb, st_all, tm_all, do_all)


Z_BLK = O_Z // DNW


def _dn_post_fwd(o, proj, gn):
    def body(o_ref, z_ref, gn_ref, og_ref):
        for h in range(H):
            cols = slice(h * DH, (h + 1) * DH)
            ov = o_ref[:, cols]
            on = ov * lax.rsqrt(jnp.mean(ov * ov, axis=-1, keepdims=True) + EPS) * gn_ref[...]
            og_ref[:, cols] = (on * _silu(z_ref[:, cols])).astype(BF16)

    return _call(body, "dn_post_fwd", _sds((S, DNW), BF16), grid=(S // TS,),
                 in_specs=[_row_spec(DNW), pl.BlockSpec((TS, DNW), lambda i: (i, Z_BLK)), _vec_spec(DH)],
                 out_specs=_row_spec(DNW), sem=("parallel",))(o, proj, gn)


def _dn_post_bwd(dog, o, proj, gn, dproj):
    def body(d_ref, o_ref, z_ref, gn_ref, dproj_in, do_ref, dz_ref, acc_ref):
        dgn = jnp.zeros((1, DH), F32)
        for h in range(H):
            cols = slice(h * DH, (h + 1) * DH)
            ov, zv, dv = o_ref[:, cols], z_ref[:, cols], d_ref[:, cols]
            rinv = lax.rsqrt(jnp.mean(ov * ov, axis=-1, keepdims=True) + EPS)
            xn = ov * rinv
            act, dact = _silu_both(zv)
            don = dv * act
            dz_ref[:, cols] = (dv * xn * gn_ref[...] * dact).astype(BF16)
            dgn = dgn + jnp.sum(don * xn, axis=0, keepdims=True)
            dxn = don * gn_ref[...]
            do_ref[:, cols] = rinv * (dxn - xn * jnp.mean(dxn * xn, axis=-1, keepdims=True))
        _acc_rows(acc_ref, [dgn])

    zspec = pl.BlockSpec((TS, DNW), lambda i: (i, Z_BLK))
    return _call(body, "dn_post_bwd", [_sds((S, DNW)), _sds((S, NINP), BF16), _sds((8, DH))], grid=(S // TS,),
                 in_specs=[_row_spec(DNW), _row_spec(DNW), zspec, _vec_spec(DH), ANY],
                 out_specs=[_row_spec(DNW), zspec, _acc_spec(DH)], sem=("arbitrary",),
                 aliases={4: 1})(dog, o, proj, gn, dproj)


GA_BLK = O_GA // D
GB_BLK = O_GB // D


def _merge_fwd(ba, bb, proj):
    def body(a_ref, b_ref, ga_ref, gb_ref, o_ref):
        o_ref[...] = (_sigmoid(ga_ref[...]) * a_ref[...] + _sigmoid(gb_ref[...]) * b_ref[...]).astype(BF16)

    return _call(body, "merge_fwd", _sds((S, D), BF16), grid=(S // TS,),
                 in_specs=[_row_spec(), _row_spec(), pl.BlockSpec((TS, D), lambda i: (i, GA_BLK)),
                           pl.BlockSpec((TS, D), lambda i: (i, GB_BLK))],
                 out_specs=_row_spec(), sem=("parallel",))(ba, bb, proj, proj)


def _merge_bwd(dm, ba, bb, proj, dproj):
    def body(d_ref, a_ref, b_ref, ga_ref, gb_ref, dproj_in, dg_ref, da_ref, db_ref):
        d = d_ref[...]
        sa, sb = _sigmoid(ga_ref[...]), _sigmoid(gb_ref[...])
        dg_ref[:, 0:D] = (d * a_ref[...] * sa * (1.0 - sa)).astype(BF16)
        dg_ref[:, D:2 * D] = (d * b_ref[...] * sb * (1.0 - sb)).astype(BF16)
        da_ref[...] = (d * sa).astype(BF16)
        db_ref[...] = (d * sb).astype(BF16)

    return _call(body, "merge_bwd", [_sds((S, NINP), BF16), _sds((S, D), BF16), _sds((S, D), BF16)], grid=(S // TS,),
                 in_specs=[_row_spec(), _row_spec(), _row_spec(), pl.BlockSpec((TS, D), lambda i: (i, GA_BLK)),
                           pl.BlockSpec((TS, D), lambda i: (i, GB_BLK)), ANY],
                 out_specs=[pl.BlockSpec((TS, 2 * D), lambda i: (i, O_GA // (2 * D))), _row_spec(), _row_spec()],
                 sem=("parallel",), aliases={5: 0})(dm, ba, bb, proj, proj, dproj)


NSH = NIN // NDEV


def _pad_win(wt):
    rows = [wt[0:4096], wt[6160:6160 + 2 * D]]
    for j in range(CFW // LANE):
        rows += [wt[4112 + LANE * j:4112 + LANE * (j + 1)], wt[4112 + CFW + LANE * j:4112 + CFW + LANE * (j + 1)]]
    rows += [wt[4096:4112], jnp.zeros((NINP - NIN, wt.shape[1]), wt.dtype)]
    return jnp.concatenate(rows, axis=0)


def _unpad_win(gpt):
    rows = [gpt[0:4096], gpt[O_SM:O_SM + 16]]
    for half in range(2):
        rows += [gpt[O_GLU + (2 * j + half) * LANE:O_GLU + (2 * j + half + 1) * LANE] for j in range(CFW // LANE)]
    rows.append(gpt[O_GA:O_GA + 2 * D])
    return jnp.concatenate(rows, axis=0)


def _lane_vec(v8, offset):
    return jnp.pad(v8, ((0, 0), (offset, LANE - 8 - offset)))


def _tie(vec, token):
    return vec + token


def _local_step(x, tgt, mod, norm1_g, norm2_g, final_g, w_in_p, dn_conv_w, a_log, dt_bias, dn_norm_g,
                cf_conv_w, cf_ln_g, cf_ln_b, ffn_conv_w, comm):
    sh1, sc1, gt1, sh2, sc2, gt2 = (mod[:, i * D:(i + 1) * D] for i in range(6))
    alog_v, dtb_v = _lane_vec(a_log, H), _lane_vec(dt_bias, H)

    hn1 = _norm_mod(x, norm1_g, sc1, _tie(sh1, comm.token0), "norm_mod1")
    proj = _mm(hn1, w_in_p, "nt", F32, "mm_in", tn=1152)
    qkvn = _dn_pre_fwd(proj, dn_conv_w)
    gates = _gates_fwd(proj, alog_v, dtb_v)
    beta_t = gates[:, 0:H].T
    g_t = gates[:, H:2 * H].T
    gb = jnp.broadcast_to(g_t[:, :, None], (H, S, DH))
    bb = jnp.broadcast_to(beta_t[:, :, None], (H, S, DH))
    gr = g_t.reshape(H, NCH, CH)
    o, st_all, tm_all = _delta_fwd(qkvn, gb, gr, bb)
    og = _dn_post_fwd(o, proj, dn_norm_g)
    u1 = _cf_conv_fwd(proj, cf_conv_w)
    u3 = _cf_ln_fwd(u1, cf_ln_g, cf_ln_b)
    after = og[0:8, 0:LANE].astype(F32) + u3[0:8, 0:LANE].astype(F32)
    dn_w_o, cf_w_o, w_out = comm.late_weights("mix", after)
    br_a = _mm(og, dn_w_o, "nn", F32, "mm_dn_o")
    br_b = _mm(u3, cf_w_o, "nn", F32, "mm_cf_o")
    merged = _merge_fwd(br_a, br_b, proj)
    mix = _mm(merged, w_out, "nn", F32, "mm_out")
    x2, hn2 = _resid_norm_mod(x, mix, gt1, norm2_g, sc2, sh2, "resid_norm_mod2")
    ffn_w_up, ffn_w_down = comm.late_weights("ffn", hn2[0:8, 0:LANE].astype(F32))
    upall = _mm(hn2, ffn_w_up, "nn", F32, "mm_up")
    hmid = _ffn_mid_fwd(upall, ffn_conv_w)
    f = _mm(hmid, ffn_w_down, "nn", F32, "mm_down", tm=2048)

    dx3, df, acc_f = _loss_head(x2, f, tgt, gt2, final_g)
    d_final_g, d_gt2, loss = acc_f[0:1], acc_f[1:2], acc_f[2:3, 0:1]
    dhmid = _mm(df, ffn_w_down, "nt", F32, "mm_down_dx")
    g_w_down = _mm(hmid, df, "tn", BF16, "mm_down_dw", tm=FFN // 4)
    d_gate, d_up, g_ffn_conv = _ffn_mid_bwd(dhmid, upall, ffn_conv_w)
    g_w_up = _mm(hn2, d_gate, "tn", BF16, "mm_up_dw", tn=2 * FFN // NDEV, b2=d_up)
    tok_a = comm.grads_begin("a", dict(ffn_w_down=g_w_down, ffn_w_up=g_w_up))
    dhn2 = _mm(d_gate, ffn_w_up, "nt", F32, "mm_up_dx", a2=d_up, dep=jnp.broadcast_to(tok_a, (8, LANE)))
    tok_a = comm.grads_continue("a", dhn2)
    dx2, dmix, acc2 = _norm_mod_bwd(dhn2, x2, dx3, _tie(norm2_g, tok_a), sc2, "norm_mod2_bwd", mix=mix, gt=gt1)
    d_sh2, d_sc2, d_norm2_g, d_gt1 = acc2[0:1], acc2[1:2], acc2[2:3], acc2[3:4]
    dmerged = _mm(dmix, w_out, "nt", F32, "mm_out_dx")
    g_w_out = _mm(merged, dmix, "tn", BF16, "mm_out_dw")
    d_proj, d_bra, d_brb = _merge_bwd(dmerged, br_a, br_b, proj, lax.empty((S, NINP), BF16))
    du3 = _mm(d_brb, cf_w_o, "nt", F32, "mm_cf_o_dx")
    g_cf_w_o = _mm(u3, d_brb, "tn", BF16, "mm_cf_o_dw")
    du1, acc_ln = _cf_ln_bwd(du3, u1, cf_ln_g, cf_ln_b)
    d_proj, g_cf_conv = _cf_conv_bwd(du1, proj, cf_conv_w, d_proj)
    dog = _mm(d_bra, dn_w_o, "nt", F32, "mm_dn_o_dx")
    g_dn_w_o = _mm(og, d_bra, "tn", BF16, "mm_dn_o_dw")
    tok_b = comm.grads_begin("b", dict(w_out=g_w_out, cf_w_o=g_cf_w_o, dn_w_o=g_dn_w_o, ffn_conv_w=g_ffn_conv,
                                       cf_conv_w=g_cf_conv))
    do, d_proj, acc_gn = _dn_post_bwd(dog, o, proj, _tie(dn_norm_g, tok_b), d_proj)
    tok_b = comm.grads_continue("b", do)
    dq, dk, dv, dgr, dbr = _delta_bwd(qkvn, gb, _tie(gr, tok_b), bb, st_all, tm_all, do)
    d_proj, g_dn_conv = _dn_pre_bwd(dq, dk, dv, proj, dn_conv_w, d_proj)
    dgates = jnp.concatenate([dbr.reshape(H, S).T, dgr.reshape(H, S).T, jnp.zeros((S, LANE - 2 * H), F32)], axis=1)
    d_proj, acc_g = _gates_bwd(dgates, proj, alog_v, dtb_v, d_proj)
    g_w_in_p = _mm(d_proj, hn1, "tn", BF16, "mm_in_dw", tm=1152)
    tok_c = comm.grads_begin("c", dict(w_in=g_w_in_p, dn_conv_w=g_dn_conv))
    tok_c = comm.grads_continue("c", jnp.broadcast_to(tok_c, (8, LANE)))
    dhn1 = _mm(d_proj, w_in_p, "nn", F32, "mm_in_dx", tk=NINP // 3, dep=jnp.broadcast_to(tok_c, (8, LANE)))
    grad_x, acc1 = _norm_mod_bwd(dhn1, x, dx2, norm1_g, sc1, "norm_mod1_bwd")
    d_sh1, d_sc1, d_norm1_g = acc1[0:1], acc1[1:2], acc1[2:3]

    d_mod = jnp.concatenate([d_sh1, d_sc1, d_gt1, d_sh2, d_sc2, d_gt2], axis=1)
    small = dict(mod=d_mod, norm1_g=d_norm1_g, norm2_g=d_norm2_g, final_norm_g=d_final_g,
                 cf_ln_g=acc_ln[0:1], cf_ln_b=acc_ln[1:2], dn_norm_g=acc_gn[0:1],
                 dn_a_log=acc_g[0:1, H:2 * H], dn_dt_bias=acc_g[1:2, H:2 * H])
    return loss, grad_x, small


def _dev_index(px, py, pc):
    return 4 * px + 2 * py + pc


def _all_gather(arrs, name):
    n = len(arrs)

    def body(*refs):
        ins, outs = refs[:n], refs[n:2 * n]
        send_sems, recv_sems, loc_sems = refs[2 * n:]
        x, y, c = _my_pos()
        me, sib = (x, y, c), (x, y, 1 - c)
        chips = [(1 - x, y), (x, 1 - y), (1 - x, 1 - y)]

        def cp(i, k, block, to, src=None):
            dst = outs[i].at[_dev_index(*block)]
            return pltpu.make_async_remote_copy(
                src_ref=dst if src is None else src, dst_ref=dst, send_sem=send_sems.at[i, k],
                recv_sem=recv_sems.at[i, k], device_id=to, device_id_type=MESH)

        mine = [pltpu.make_async_copy(ins[i], outs[i].at[_dev_index(*me)], loc_sems.at[i]) for i in range(n)]
        for m in mine:
            m.start()
        sent = []
        for i in range(n):
            sent.append(cp(i, 0, me, sib, src=ins[i]))
            sent += [cp(i, 1 + j, me, (*chip, c), src=ins[i]) for j, chip in enumerate(chips)]
        for s in sent:
            s.start()
        for i in range(n):
            for j, chip in enumerate(chips):
                cp(i, 1 + j, (*chip, c), me).wait_recv()
                fwd = cp(i, 4 + j, (*chip, c), sib)
                fwd.start()
                sent.append(fwd)
        for i in range(n):
            cp(i, 0, sib, me).wait_recv()
            for j, chip in enumerate(chips):
                cp(i, 4 + j, (*chip, 1 - c), me).wait_recv()
        for s in sent:
            s.wait_send()
        for m in mine:
            m.wait()

    outs = pl.pallas_call(
        body, out_shape=[_sds((NDEV,) + a.shape, a.dtype) for a in arrs], in_specs=[ANY] * n, out_specs=[ANY] * n,
        scratch_shapes=[pltpu.SemaphoreType.DMA((n, 7)), pltpu.SemaphoreType.DMA((n, 7)), pltpu.SemaphoreType.DMA((n,))],
        name=name)(*arrs)
    return list(outs)


def _slab(ref, layout, idx):
    kind, n = layout
    if kind == "rows":
        return ref.at[pl.ds(pl.multiple_of(idx * n, n), n), :]
    if kind == "cols":
        return ref.at[:, pl.ds(pl.multiple_of(idx * n, n), n)]
    return ref.at[idx]


def _slab_shape(arr, layout):
    kind, n = layout
    if kind == "rows":
        return (n, arr.shape[1])
    if kind == "cols":
        return (arr.shape[0], n)
    return tuple(arr.shape[1:])


HBM = pl.BlockSpec(memory_space=pltpu.HBM)
SEMS = pl.BlockSpec(memory_space=pltpu.SEMAPHORE)
EFFECT = pltpu.SideEffectType.DATAFLOW_SIDE_EFFECTING
TOKEN = jax.ShapeDtypeStruct((8, LANE), F32)


def _hbm(a):
    return pltpu.with_memory_space_constraint(a, pltpu.HBM)


def _gather_ici_copy(shard_ref, buf_ref, layout, send_sems, recv_sems, i, j, me, chip, c):
    return pltpu.make_async_remote_copy(
        src_ref=shard_ref, dst_ref=_slab(buf_ref, layout, me), send_sem=send_sems.at[3 * i + j],
        recv_sem=recv_sems.at[3 * i + j], device_id=(*chip, c), device_id_type=MESH)


def _gather_ici_start(shards, bufs, layouts, after, name):
    n = len(shards)

    def body(*refs):
        sh, bf = refs[:n], refs[n:2 * n]
        send_sems, recv_sems = refs[2 * n + 1], refs[2 * n + 2]
        token = refs[-1]
        x, y, c = _my_pos()
        me = _dev_index(x, y, c)
        for i in range(n):
            for j, chip in enumerate([(1 - x, y), (x, 1 - y), (1 - x, 1 - y)]):
                _gather_ici_copy(sh[i], bf[i], layouts[i], send_sems, recv_sems, i, j, me, chip, c).start()
        token[...] = jnp.zeros_like(token)

    outs = pl.pallas_call(
        body, name=name,
        out_shape=(pltpu.SemaphoreType.DMA((3 * n,)), pltpu.SemaphoreType.DMA((3 * n,)),
                   *[pltpu.HBM(a.shape, a.dtype) for a in shards], *[pltpu.HBM(a.shape, a.dtype) for a in bufs], TOKEN),
        in_specs=[HBM] * (2 * n) + [ANY],
        out_specs=(SEMS, SEMS, *[HBM] * (2 * n), pl.BlockSpec(memory_space=pltpu.VMEM)),
        input_output_aliases={i: 2 + i for i in range(2 * n)},
        compiler_params=pltpu.CompilerParams(has_side_effects=EFFECT),
    )(*[_hbm(a) for a in shards], *[_hbm(a) for a in bufs], after)
    return outs[0], outs[1], list(outs[2:2 + n]), list(outs[2 + n:2 + 2 * n]), outs[-1]


def _gather_ici_wait(send_sems, recv_sems, shards, bufs, layouts, after, name):
    n = len(shards)

    def body(*refs):
        sh, bf = refs[:n], refs[n:2 * n]
        ssem, rsem = refs[2 * n], refs[2 * n + 1]
        x, y, c = _my_pos()
        me = _dev_index(x, y, c)
        for i in range(n):
            for j, chip in enumerate([(1 - x, y), (x, 1 - y), (1 - x, 1 - y)]):
                cp = _gather_ici_copy(sh[i], bf[i], layouts[i], ssem, rsem, i, j, me, chip, c)
                cp.wait_send()
                cp.wait_recv()

    outs = pl.pallas_call(
        body, name=name,
        out_shape=(*[pltpu.HBM(a.shape, a.dtype) for a in shards], *[pltpu.HBM(a.shape, a.dtype) for a in bufs]),
        in_specs=[HBM] * (2 * n) + [SEMS, SEMS, ANY], out_specs=tuple([HBM] * (2 * n)),
        input_output_aliases={i: i for i in range(2 * n)},
        compiler_params=pltpu.CompilerParams(has_side_effects=EFFECT),
    )(*shards, *bufs, send_sems, recv_sems, after)
    return list(outs[:n]), list(outs[n:])


def _place_own(pos, shard, buf, layout, name):
    kind, n = layout
    r, cols = shard.shape
    tr = _row_tile(r, shard.dtype.itemsize)
    nr = r // tr
    if kind == "rows":
        ospec = pl.BlockSpec((tr, cols), lambda i, p: (p[2] * nr + i, 0))
    else:
        assert kind == "lead"
        ospec = pl.BlockSpec((None, tr, cols), lambda i, p: (p[2], i, 0))

    def body(pos_ref, s_ref, buf_in, o_ref):
        o_ref[...] = s_ref[...]

    return pl.pallas_call(
        body, out_shape=_sds(buf.shape, buf.dtype), name=name, input_output_aliases={2: 0},
        grid_spec=pltpu.PrefetchScalarGridSpec(
            num_scalar_prefetch=1, grid=(nr,), in_specs=[pl.BlockSpec((tr, cols), lambda i, p: (i, 0)), ANY],
            out_specs=ospec),
        compiler_params=pltpu.CompilerParams(dimension_semantics=("parallel",), vmem_limit_bytes=VMEM_LIMIT),
    )(pos, shard, buf)


def _gather_pair(shards, bufs, layouts, name):
    n = len(shards)

    def body(*refs):
        sh, bo = refs[:n], refs[2 * n:3 * n]
        send_sems, recv_sems = refs[3 * n:]
        x, y, c = _my_pos()
        sib = (x, y, 1 - c)
        copies = []
        for i in range(n):
            for k, (px, py) in enumerate([(x, y), (1 - x, y), (x, 1 - y), (1 - x, 1 - y)]):
                slab = _slab(bo[i], layouts[i], _dev_index(px, py, c))
                copies.append(pltpu.make_async_remote_copy(
                    src_ref=sh[i] if k == 0 else slab, dst_ref=slab, send_sem=send_sems.at[i, k],
                    recv_sem=recv_sems.at[i, k], device_id=sib, device_id_type=MESH))
        for cpy in copies:
            cpy.start()
        for cpy in copies:
            cpy.wait()

    outs = pl.pallas_call(
        body, out_shape=[_sds(a.shape, a.dtype) for a in bufs], in_specs=[ANY] * (2 * n), out_specs=[ANY] * n,
        input_output_aliases={n + i: i for i in range(n)},
        scratch_shapes=[pltpu.SemaphoreType.DMA((n, 4)), pltpu.SemaphoreType.DMA((n, 4))], name=name)(*shards, *bufs)
    return list(outs)


def _pair_copy(part_ref, land_ref, layout, send_sems, recv_sems, i, q, x, y, c):
    return pltpu.make_async_remote_copy(
        src_ref=_slab(part_ref, layout, 2 * q + (1 - c)), dst_ref=land_ref.at[q], send_sem=send_sems.at[4 * i + q],
        recv_sem=recv_sems.at[4 * i + q], device_id=(x, y, 1 - c), device_id_type=MESH)


def _pair_exchange_start(parts, layouts, name):
    n = len(parts)
    lands = [lax.empty((4,) + _slab_shape(p, lay), p.dtype) for p, lay in zip(parts, layouts)]

    def body(*refs):
        pt, ld = refs[:n], refs[n:2 * n]
        send_sems, recv_sems = refs[2 * n], refs[2 * n + 1]
        token = refs[-1]
        x, y, c = _my_pos()
        for i in range(n):
            for q in range(4):
                _pair_copy(pt[i], ld[i], layouts[i], send_sems, recv_sems, i, q, x, y, c).start()
        token[...] = jnp.zeros_like(token)

    outs = pl.pallas_call(
        body, name=name,
        out_shape=(pltpu.SemaphoreType.DMA((4 * n,)), pltpu.SemaphoreType.DMA((4 * n,)),
                   *[pltpu.HBM(a.shape, a.dtype) for a in parts], *[pltpu.HBM(a.shape, a.dtype) for a in lands], TOKEN),
        in_specs=[HBM] * (2 * n), out_specs=(SEMS, SEMS, *[HBM] * (2 * n), pl.BlockSpec(memory_space=pltpu.VMEM)),
        input_output_aliases={i: 2 + i for i in range(2 * n)},
        compiler_params=pltpu.CompilerParams(has_side_effects=EFFECT),
    )(*[_hbm(a) for a in parts], *[_hbm(a) for a in lands])
    return outs[0], outs[1], list(outs[2:2 + n]), list(outs[2 + n:2 + 2 * n]), outs[-1]


def _pair_exchange_wait(send_sems, recv_sems, parts, lands, layouts, after, name):
    n = len(parts)

    def body(*refs):
        pt, ld = refs[:n], refs[n:2 * n]
        ssem, rsem = refs[2 * n], refs[2 * n + 1]
        x, y, c = _my_pos()
        for i in range(n):
            for q in range(4):
                cp = _pair_copy(pt[i], ld[i], layouts[i], ssem, rsem, i, q, x, y, c)
                cp.wait_send()
                cp.wait_recv()

    outs = pl.pallas_call(
        body, name=name,
        out_shape=(*[pltpu.HBM(a.shape, a.dtype) for a in parts], *[pltpu.HBM(a.shape, a.dtype) for a in lands]),
        in_specs=[HBM] * (2 * n) + [SEMS, SEMS, ANY], out_specs=tuple([HBM] * (2 * n)),
        input_output_aliases={i: i for i in range(2 * n)},
        compiler_params=pltpu.CompilerParams(has_side_effects=EFFECT),
    )(*parts, *lands, send_sems, recv_sems, after)
    return list(outs[:n]), list(outs[n:])


def _chip_copy(sum_ref, land_ref, send_sems, recv_sems, i, j, chip, c):
    return pltpu.make_async_remote_copy(
        src_ref=sum_ref.at[2 * chip[0] + chip[1]], dst_ref=land_ref.at[j], send_sem=send_sems.at[3 * i + j],
        recv_sem=recv_sems.at[3 * i + j], device_id=(*chip, c), device_id_type=MESH)


def _chip_exchange_start(sums, name):
    n = len(sums)
    lands = [lax.empty((3,) + s.shape[1:], s.dtype) for s in sums]

    def body(*refs):
        sm, ld = refs[:n], refs[n:2 * n]
        send_sems, recv_sems = refs[2 * n], refs[2 * n + 1]
        token = refs[-1]
        x, y, c = _my_pos()
        for i in range(n):
            for j, chip in enumerate([(1 - x, y), (x, 1 - y), (1 - x, 1 - y)]):
                _chip_copy(sm[i], ld[i], send_sems, recv_sems, i, j, chip, c).start()
        token[...] = jnp.zeros_like(token)

    outs = pl.pallas_call(
        body, name=name,
        out_shape=(pltpu.SemaphoreType.DMA((3 * n,)), pltpu.SemaphoreType.DMA((3 * n,)),
                   *[pltpu.HBM(a.shape, a.dtype) for a in sums], *[pltpu.HBM(a.shape, a.dtype) for a in lands], TOKEN),
        in_specs=[HBM] * (2 * n), out_specs=(SEMS, SEMS, *[HBM] * (2 * n), pl.BlockSpec(memory_space=pltpu.VMEM)),
        input_output_aliases={i: 2 + i for i in range(2 * n)},
        compiler_params=pltpu.CompilerParams(has_side_effects=EFFECT),
    )(*[_hbm(a) for a in sums], *[_hbm(a) for a in lands])
    return outs[0], outs[1], list(outs[2:2 + n]), list(outs[2 + n:2 + 2 * n]), outs[-1]


def _chip_exchange_wait(send_sems, recv_sems, sums, lands, after, name):
    n = len(sums)

    def body(*refs):
        sm, ld = refs[:n], refs[n:2 * n]
        ssem, rsem = refs[2 * n], refs[2 * n + 1]
        x, y, c = _my_pos()
        for i in range(n):
            for j, chip in enumerate([(1 - x, y), (x, 1 - y), (1 - x, 1 - y)]):
                cp = _chip_copy(sm[i], ld[i], ssem, rsem, i, j, chip, c)
                cp.wait_send()
                cp.wait_recv()

    outs = pl.pallas_call(
        body, name=name,
        out_shape=(*[pltpu.HBM(a.shape, a.dtype) for a in sums], *[pltpu.HBM(a.shape, a.dtype) for a in lands]),
        in_specs=[HBM] * (2 * n) + [SEMS, SEMS, ANY], out_specs=tuple([HBM] * (2 * n)),
        input_output_aliases={i: i for i in range(2 * n)},
        compiler_params=pltpu.CompilerParams(has_side_effects=EFFECT),
    )(*sums, *lands, send_sems, recv_sems, after)
    return list(outs[:n]), list(outs[n:])


def _row_tile(r, itemsize):
    align = 32 // itemsize
    best = r
    for t in range(align, min(r, 256) + 1, align):
        if r % t == 0:
            best = t
    return best


def _prefetch_call(body, name, out_shape, grid, in_specs, out_specs, sem):
    return pl.pallas_call(
        body, out_shape=out_shape, name=name,
        grid_spec=pltpu.PrefetchScalarGridSpec(num_scalar_prefetch=1, grid=grid, in_specs=in_specs, out_specs=out_specs),
        compiler_params=pltpu.CompilerParams(dimension_semantics=sem, vmem_limit_bytes=VMEM_LIMIT))


def _pair_sum(pos, part, got, layout, name):
    kind, _ = layout
    _, r, cols = got.shape
    tr, tc = _tiles(r, cols, part.dtype.itemsize)
    nr, nc = r // tr, cols // tc
    if kind == "rows":
        pspec = pl.BlockSpec((tr, tc), lambda q, i, j, p: ((2 * q + p[0]) * nr + i, j))
    elif kind == "cols":
        pspec = pl.BlockSpec((tr, tc), lambda q, i, j, p: (i, (2 * q + p[0]) * nc + j))
    else:
        pspec = pl.BlockSpec((None, tr, tc), lambda q, i, j, p: (2 * q + p[0], i, j))

    def body(pos_ref, p_ref, g_ref, o_ref):
        o_ref[...] = (p_ref[...].astype(F32) + g_ref[...].astype(F32)).astype(o_ref.dtype)

    blk = pl.BlockSpec((None, tr, tc), lambda q, i, j, p: (q, i, j))
    return _prefetch_call(body, name, _sds((4, r, cols), part.dtype), (4, nr, nc), [pspec, blk], blk,
                          ("parallel", "parallel", "parallel"))(pos, part, got)


def _tiles(r, cols, itemsize):
    tr = _row_tile(r, itemsize)
    if tr < r or r * cols * 4 <= (2 << 20) or cols % 256:
        return tr, cols
    return r, 256


def _final_sum_adam(pos, sums, got, w, m, v, name):
    _, r, cols = w.shape
    tr, tc = _tiles(r, cols, sums.dtype.itemsize)

    def body(pos_ref, s_ref, g_ref, w_ref, m_ref, v_ref, go_ref, dl_ref, nm_ref, nv_ref):
        g = ((s_ref[...].astype(F32) + g_ref[0].astype(F32)) + g_ref[1].astype(F32)) + g_ref[2].astype(F32)
        dl, nm, nv = _adam(w_ref[...], g, m_ref[...], v_ref[...])
        go_ref[...] = g
        dl_ref[...] = dl
        nm_ref[...] = nm
        nv_ref[...] = nv

    big = pl.BlockSpec((None, tr, tc), lambda i, j, p: (0, i, j))
    return _prefetch_call(body, name, [_sds((1, r, cols))] * 4, (r // tr, cols // tc),
                          [pl.BlockSpec((None, tr, tc), lambda i, j, p: (p[1], i, j)),
                           pl.BlockSpec((3, tr, tc), lambda i, j, p: (0, i, j)), big, big, big],
                          [big] * 4, ("parallel", "parallel"))(pos, sums, got, w, m, v)


def _small_adam(g_all, w, m, v):
    npk = w.shape[1]

    def body(g_ref, w_ref, m_ref, v_ref, go_ref, dl_ref, nm_ref, nv_ref):
        g = g_ref[0:1, :]
        for k in range(1, NDEV):
            g = g + g_ref[k:k + 1, :]
        dl, nm, nv = _adam(w_ref[...], g, m_ref[...], v_ref[...])
        go_ref[...] = g
        dl_ref[...] = dl
        nm_ref[...] = nm
        nv_ref[...] = nv

    return _call(body, "small_adam", [_sds((1, npk))] * 4)(g_all, w, m, v)


SMALL = [("b_ada", 6 * D), ("norm1_g", D), ("norm2_g", D), ("final_norm_g", D), ("cf_ln_g", CFW), ("cf_ln_b", CFW),
         ("dn_norm_g", DH), ("dn_a_log", H), ("dn_dt_bias", H)]
LATE = ["dn_w_o", "cf_w_o", "w_out", "ffn_w_up", "ffn_w_down"]
LATE_SHAPE = {"dn_w_o": (NDEV, DNW, D // NDEV), "cf_w_o": (NDEV, CFW, D // NDEV), "w_out": (D, D),
              "ffn_w_up": (NDEV, D, 2 * FFN // NDEV), "ffn_w_down": (FFN, D)}
LATE_LAYOUT = {"dn_w_o": ("lead", NDEV), "cf_w_o": ("lead", NDEV), "w_out": ("rows", D // NDEV),
               "ffn_w_up": ("lead", NDEV), "ffn_w_down": ("rows", FFN // NDEV)}
LAYOUT = {"dn_w_o": ("cols", D // NDEV), "cf_w_o": ("cols", D // NDEV), "w_out": ("rows", D // NDEV),
          "ffn_w_up": ("cols", 2 * FFN // NDEV), "ffn_w_down": ("rows", FFN // NDEV),
          "w_in": ("lead", NDEV), "dn_conv_w": ("lead", NDEV), "cf_conv_w": ("lead", NDEV), "ffn_conv_w": ("lead", NDEV)}
NAMES = ["w_ada", "b_ada", "norm1_g", "w_in", "dn_conv_w", "dn_a_log", "dn_dt_bias", "dn_norm_g", "dn_w_o", "cf_conv_w",
         "cf_ln_g", "cf_ln_b", "cf_w_o", "w_out", "norm2_g", "ffn_w_up", "ffn_conv_w", "ffn_w_down", "final_norm_g"]


def _pack_small(d):
    rows = []
    for nm, n in SMALL:
        row = d[nm].reshape(1, n)
        pad = (-n) % LANE
        rows.append(jnp.pad(row, ((0, 0), (0, pad))) if pad else row)
    return jnp.concatenate(rows, axis=1)


def _unpack_small(row, shapes):
    out, off = {}, 0
    for nm, n in SMALL:
        out[nm] = row[0, off:off + n].reshape(shapes[nm])
        off += n + ((-n) % LANE)
    return out


def _cols_from_gathered(g):
    return jnp.transpose(g, (1, 0, 2)).reshape(g.shape[1], NDEV * g.shape[2])


def _cols_to_parts(full):
    r, ctot = full.shape
    return jnp.transpose(full.reshape(r, NDEV, ctot // NDEV), (1, 0, 2))


def kernel(x, c, w_ada, b_ada, norm1_g, w_in, dn_conv_w, dn_a_log, dn_dt_bias, dn_norm_g, dn_w_o, cf_conv_w, cf_ln_g, cf_ln_b, cf_w_o, w_out, norm2_g, ffn_w_up, ffn_conv_w, ffn_w_down, final_norm_g, loss_target, m_w_ada, m_b_ada, m_norm1_g, m_w_in, m_dn_conv_w, m_dn_a_log, m_dn_dt_bias, m_dn_norm_g, m_dn_w_o, m_cf_conv_w, m_cf_ln_g, m_cf_ln_b, m_cf_w_o, m_w_out, m_norm2_g, m_ffn_w_up, m_ffn_conv_w, m_ffn_w_down, m_final_norm_g, v_w_ada, v_b_ada, v_norm1_g, v_w_in, v_dn_conv_w, v_dn_a_log, v_dn_dt_bias, v_dn_norm_g, v_dn_w_o, v_cf_conv_w, v_cf_ln_g, v_cf_ln_b, v_cf_w_o, v_w_out, v_norm2_g, v_ffn_w_up, v_ffn_conv_w, v_ffn_w_down, v_final_norm_g):
    args = locals()
    w = {nm: args[nm] for nm in NAMES}
    mo = {nm: args["m_" + nm] for nm in NAMES}
    vo = {nm: args["v_" + nm] for nm in NAMES}
    shapes = {nm: w[nm].shape for nm in NAMES}
    px, py, pc = _my_pos()
    me = _dev_index(px, py, pc)

    def mat(a):
        return a.reshape(a.shape[-2:])

    pos = jnp.stack([pc, 2 * px + py, me]).astype(jnp.int32)

    first = ["w_in", "dn_conv_w", "cf_conv_w", "ffn_conv_w"]
    tr_in = lambda a: jnp.transpose(a, (0, 2, 1))
    got = _all_gather([tr_in(w["w_in"]).astype(BF16)] + [mat(w[nm]) for nm in first[1:]] + [c], "gather_first")
    full = {nm: _cols_from_gathered(g) for nm, g in zip(first[1:], got[1:-1])}
    c_all = got[-1].reshape(NDEV, D)
    w_in_p = _pad_win(got[0].reshape(NIN, D))

    ncol = 6 * D // NDEV
    b_sh = lax.dynamic_slice(b_ada.reshape(1, 6 * D), (0, me * ncol), (1, ncol))
    mod_sh = _ada_fwd(c_all, mat(w_ada), b_sh)
    mod_all = _all_gather([mod_sh], "gather_mod")[0]
    mod = lax.dynamic_index_in_dim(mod_all, me, axis=1, keepdims=False).reshape(1, 6 * D)

    late = {}
    dep = mod_all
    for grp, names in (("mix", LATE[:3]), ("ffn", LATE[3:])):
        shards = [mat(w[nm]).astype(BF16) for nm in names]
        lays = [LATE_LAYOUT[nm] for nm in names]
        bufs = [_place_own(pos, s, lax.empty(LATE_SHAPE[nm], BF16), lay, "place_" + nm)
                for nm, s, lay in zip(names, shards, lays)]
        started = _gather_ici_start(shards, bufs, lays, dep, "gather_" + grp + "_start")
        late[grp] = (lays,) + tuple(started)
        dep = started[4]

    res = {}

    class Comm:
        token0 = late["ffn"][5][0, 0]
        pending = {}

        @staticmethod
        def late_weights(grp, after):
            lays, ssem, rsem, shards, bufs, _ = late[grp]
            shards, bufs = _gather_ici_wait(ssem, rsem, shards, bufs, lays, after, "gather_" + grp + "_wait")
            return _gather_pair(shards, bufs, lays, "gather_" + grp + "_pair")

        @staticmethod
        def grads_begin(group, gd):
            names = list(gd)
            lays = [LAYOUT[nm] for nm in names]
            gl = []
            for nm in names:
                if nm == "w_in":
                    gl.append(_unpad_win(gd[nm]).reshape(NDEV, NSH, D))
                else:
                    gl.append(_cols_to_parts(gd[nm]) if LAYOUT[nm][0] == "lead" else gd[nm])
            started = _pair_exchange_start(gl, lays, "rs_pair_start_" + group)
            Comm.pending[group] = (names, lays) + tuple(started[:4])
            return started[4][0, 0]

        @staticmethod
        def grads_continue(group, after):
            names, lays, ssem, rsem, gl, lands = Comm.pending[group]
            if group == "c":
                after = Comm.finish("a", after)
            gl, from_sib = _pair_exchange_wait(ssem, rsem, gl, lands, lays, after, "rs_pair_wait_" + group)
            sums = [_pair_sum(pos, g, r, lay, "rs_pair_sum_" + nm) for nm, g, r, lay in zip(names, gl, from_sib, lays)]
            started = _chip_exchange_start(sums, "rs_chips_start_" + group)
            Comm.pending[group] = (names,) + tuple(started[:4])
            return started[4][0, 0]

        @staticmethod
        def finish(group, after):
            names, ssem, rsem, sums, lands = Comm.pending[group]
            sums, lands = _chip_exchange_wait(ssem, rsem, sums, lands, after, "rs_chips_wait_" + group)
            for nm, s, r in zip(names, sums, lands):
                if nm == "w_in":
                    outs = _final_sum_adam(pos, s, r, tr_in(w[nm]), tr_in(mo[nm]), tr_in(vo[nm]), "adam_" + nm)
                    res[nm] = [tr_in(o) for o in outs]
                else:
                    res[nm] = _final_sum_adam(pos, s, r, w[nm], mo[nm], vo[nm], "adam_" + nm)
            return res[names[-1]][0]

    vec = lambda a: a.reshape(1, -1)
    loss, grad_x, small = _local_step(
        x.reshape(S, D), loss_target.reshape(S, D), mod, vec(norm1_g), vec(norm2_g), vec(final_norm_g), w_in_p,
        full["dn_conv_w"], vec(dn_a_log), vec(dn_dt_bias), vec(dn_norm_g), full["cf_conv_w"], vec(cf_ln_g),
        vec(cf_ln_b), full["ffn_conv_w"], Comm)

    done_b = Comm.finish("b", grad_x)

    small["b_ada"] = small.pop("mod")
    packed = _pack_small(small) + 0.0 * done_b.reshape(-1)[0]
    g_small = _all_gather([packed], "gather_small")[0].reshape(NDEV, -1)
    outs = _small_adam(g_small, _pack_small({nm: w[nm] for nm, _ in SMALL}), _pack_small({nm: mo[nm] for nm, _ in SMALL}),
                       _pack_small({nm: vo[nm] for nm, _ in SMALL}))
    unpacked = [_unpack_small(o, shapes) for o in outs]
    for nm, _ in SMALL:
        res[nm] = [u[nm] for u in unpacked]

    dmod_sel = lax.dynamic_slice(g_small[:, :6 * D], (0, me * ncol), (NDEV, ncol))
    outs = _ada_bwd_adam(c_all, dmod_sel, mat(w_ada), mat(m_w_ada), mat(v_w_ada))
    res["w_ada"] = [o.reshape(shapes["w_ada"]) for o in outs]
    Comm.finish("c", jnp.concatenate([done_b.reshape(-1)[:LANE], outs[0].reshape(-1)[:LANE]]))

    loss = lax.psum(loss.reshape(()), ("x", "y", "c"))
    out = [loss, grad_x.reshape(x.shape)]
    for k in range(4):
        out += [res[nm][k] for nm in NAMES]
    return tuple(out)
```
